```python
import math
import jax
import jax.numpy as jnp
from jax import lax
import numpy as np


D_MODEL = 1024
BATCH = 8
SEQ = 4096
DEPTH = 2

CHUNK = 64
N_BRANCH = 4
BRANCH_W = 512
SGU_BLOCK = 128
SGU_GROUPS = 4
SGU_GW = BRANCH_W // SGU_GROUPS
LRU_W = BRANCH_W
LRU_HEADS = 8
LRU_HD = LRU_W // LRU_HEADS
LRU_CONV = 4
LRU_C = 8.0
GDN_HEADS = 4
GDN_DK = 128
GDN_DV = 128
GDN_CONV = 4
POOL_WINDOWS = (2, 4, 8, 16)
POOL_GROUPS = 4
POOL_GW = BRANCH_W // POOL_GROUPS
D_FF = 2816
EPS = 1e-6
IN_SIZES = (BRANCH_W, BRANCH_W, LRU_W, LRU_W, GDN_HEADS * GDN_DK, GDN_HEADS * GDN_DK,
            GDN_HEADS * GDN_DV, GDN_HEADS * GDN_DV, GDN_HEADS, GDN_HEADS, BRANCH_W,
            N_BRANCH * D_MODEL)
P_IN = sum(IN_SIZES)

kernel_name = 'chunk_causal_hybrid_gated_merge'

F32 = jnp.float32


def rms_norm(x, g):
    xf = x.astype(F32)
    y = xf * lax.rsqrt(jnp.mean(xf * xf, axis=-1, keepdims=True) + EPS)
    return (y * g.astype(F32)).astype(x.dtype)


def swiglu(h, wg, wu, wd):
    a = jnp.einsum('bsd,df->bsf', h, wg)
    b = jnp.einsum('bsd,df->bsf', h, wu)
    return jnp.einsum('bsf,fd->bsd', jax.nn.silu(a) * b, wd)


def causal_dwconv(x, w):
    K = w.shape[0]
    S = x.shape[1]
    xp = jnp.pad(x, ((0, 0), (K - 1, 0), (0, 0)))
    y = xp[:, 0:S] * w[0]
    for k in range(1, K):
        y = y + xp[:, k:k + S] * w[k]
    return y


def l2norm(t):
    return t * lax.rsqrt(jnp.sum(t * t, axis=-1, keepdims=True) + EPS)


def sgu_mixer(u, v, ln_g, ln_b, w_s, b_s):
    B, S, _ = u.shape
    u = jax.nn.gelu(u)
    vf = jax.nn.gelu(v).astype(F32)
    mu = jnp.mean(vf, axis=-1, keepdims=True)
    var = jnp.mean(jnp.square(vf - mu), axis=-1, keepdims=True)
    vn = (vf - mu) * lax.rsqrt(var + EPS) * ln_g.astype(F32) + ln_b.astype(F32)
    n = S // SGU_BLOCK
    vb = vn.reshape(B, n, SGU_BLOCK, SGU_GROUPS, SGU_GW)
    pos_chunk = jnp.arange(SGU_BLOCK) // CHUNK
    mask = pos_chunk[:, None] >= pos_chunk[None, :]
    w = jnp.where(mask[None], w_s.astype(F32), 0.0)
    mixed = jnp.einsum('gij,bnjgc->bnigc', w, vb) + b_s.astype(F32).T[None, None, :, :, None]
    return u * mixed.reshape(B, S, BRANCH_W).astype(u.dtype)


def rglru_mixer(xb, gate, conv_w, conv_b, wa, ba, wx, bx, lam):
    B, S, _ = xb.shape
    xc = (causal_dwconv(xb, conv_w) + conv_b).astype(F32)
    xh = xc.reshape(B, S, LRU_HEADS, LRU_HD)
    r = jax.nn.sigmoid(jnp.einsum('bshi,hij->bshj', xh, wa.astype(F32)).reshape(B, S, LRU_W) + ba.astype(F32))
    i = jax.nn.sigmoid(jnp.einsum('bshi,hij->bshj', xh, wx.astype(F32)).reshape(B, S, LRU_W) + bx.astype(F32))
    log_a = -LRU_C * r * jax.nn.softplus(-lam.astype(F32))
    a = jnp.exp(log_a)
    mult = jnp.sqrt(-jnp.expm1(2.0 * log_a))
    b = mult * (i * xc)

    def combine(left, right):
        a1, b1 = left
        a2, b2 = right
        return a1 * a2, a2 * b1 + b2

    _, h = lax.associative_scan(combine, (a, b), axis=1)
    return (h * jax.nn.gelu(gate.astype(F32))).astype(xb.dtype)


def gated_deltanet_mixer(q, k, v, z, beta_pre, alpha_pre, conv_w, a_log, dt_bias, norm_g):
    B, S, _ = q.shape
    H, C = GDN_HEADS, CHUNK
    n = S // C
    qkv = jax.nn.silu(causal_dwconv(jnp.concatenate([q, k, v], axis=-1), conv_w)).astype(F32)
    q, k, v = jnp.split(qkv, [H * GDN_DK, 2 * H * GDN_DK], axis=-1)
    q = l2norm(q.reshape(B, S, H, GDN_DK)) * (GDN_DK ** -0.5)
    k = l2norm(k.reshape(B, S, H, GDN_DK))
    v = v.reshape(B, S, H, GDN_DV)
    beta = jax.nn.sigmoid(beta_pre.astype(F32))
    g = -jnp.exp(a_log.astype(F32)) * jax.nn.softplus(alpha_pre.astype(F32) + dt_bias.astype(F32))

    def chunks(t):
        t = t.reshape((B, n, C, H) + t.shape[3:])
        return jnp.moveaxis(t, 3, 1)

    qc, kc, vc = chunks(q), chunks(k), chunks(v)
    bc, gc = chunks(beta), chunks(g)
    gcum = jnp.cumsum(gc, axis=-1)
    idx = jnp.arange(C)
    incl = idx[:, None] >= idx[None, :]
    strict = idx[:, None] > idx[None, :]
    decay = jnp.exp(jnp.where(incl, gcum[..., :, None] - gcum[..., None, :], -jnp.inf))
    kb = kc * bc[..., None]
    a_mat = jnp.where(strict, jnp.einsum('bhnid,bhnjd->bhnij', kb, kc) * decay, 0.0)
    eye = jnp.eye(C, dtype=F32)
    t_mat = lax.linalg.triangular_solve(eye + a_mat, jnp.broadcast_to(eye, a_mat.shape),
                                        left_side=True, lower=True, unit_diagonal=True)
    u = jnp.einsum('bhnij,bhnjd->bhnid', t_mat, vc * bc[..., None])
    w = jnp.einsum('bhnij,bhnjd->bhnid', t_mat, kb * jnp.exp(gcum)[..., None])
    attn = jnp.where(incl, jnp.einsum('bhnid,bhnjd->bhnij', qc, kc) * decay, 0.0)

    def step(state, inp):
        q_i, k_i, u_i, w_i, g_i, attn_i = inp
        v_new = u_i - jnp.einsum('bhcd,bhde->bhce', w_i, state)
        o = (jnp.einsum('bhcd,bhde->bhce', q_i * jnp.exp(g_i)[..., None], state)
             + jnp.einsum('bhij,bhje->bhie', attn_i, v_new))
        g_last = g_i[..., -1]
        state = (state * jnp.exp(g_last)[..., None, None]
                 + jnp.einsum('bhcd,bhce->bhde', k_i * jnp.exp(g_last[..., None] - g_i)[..., None], v_new))
        return state, o

    xs = tuple(jnp.moveaxis(t, 2, 0) for t in (qc, kc, u, w, gcum, attn))
    state0 = jnp.zeros((B, H, GDN_DK, GDN_DV), F32)
    _, o = lax.scan(step, state0, xs)
    o = jnp.transpose(o, (1, 0, 3, 2, 4)).reshape(B, S, H, GDN_DV)
    o = o * lax.rsqrt(jnp.mean(o * o, axis=-1, keepdims=True) + EPS) * norm_g.astype(F32)
    o = o * jax.nn.silu(z.astype(F32).reshape(B, S, H, GDN_DV))
    return o.reshape(B, S, H * GDN_DV).astype(z.dtype)


def pool_mixer(xd, w_pool, scale):
    B, S, _ = xd.shape
    xf = xd.astype(F32).reshape(B, S, POOL_GROUPS, POOL_GW)
    cs = jnp.cumsum(xf, axis=1)
    t = jnp.arange(S)
    outs = []
    for gi, win in enumerate(POOL_WINDOWS):
        c = cs[:, :, gi]
        prev = jnp.pad(c, ((0, 0), (win, 0), (0, 0)))[:, :S]
        cnt = jnp.minimum(t + 1, win).astype(F32)[None, :, None]
        outs.append((c - prev) / cnt - xf[:, :, gi])
    pooled = jnp.stack(outs, axis=2)
    y = jnp.einsum('bsgc,gcd->bsgd', pooled, w_pool.astype(F32)).reshape(B, S, BRANCH_W)
    return (y * scale.astype(F32)).astype(xd.dtype)


def _fwd_setup_inputs(seed: int = 0) -> dict:
    key = jax.random.key(seed)
    ks = iter(jax.random.split(key, 40))
    L, D = DEPTH, D_MODEL

    def nrm(shape, scale):
        return jax.random.normal(next(ks), shape, F32) * scale

    def gain(shape):
        return 1.0 + 0.02 * jax.random.normal(next(ks), shape, F32)

    x = jax.random.normal(next(ks), (BATCH, SEQ, D), F32)
    ff1_norm = gain((L, D))
    ff1_wg = nrm((L, D, D_FF), D ** -0.5)
    ff1_wu = nrm((L, D, D_FF), D ** -0.5)
    ff1_wd = nrm((L, D_FF, D), D_FF ** -0.5)
    mix_norm = gain((L, D))
    w_in = nrm((L, D, P_IN), D ** -0.5)
    sgu_ln_g = gain((L, BRANCH_W))
    sgu_ln_b = nrm((L, BRANCH_W), 0.02)
    sgu_w = nrm((L, SGU_GROUPS, SGU_BLOCK, SGU_BLOCK), 0.5 * SGU_BLOCK ** -0.5)
    sgu_b = 1.0 + nrm((L, SGU_GROUPS, SGU_BLOCK), 0.1)
    lru_conv_w = nrm((L, LRU_CONV, LRU_W), LRU_CONV ** -0.5)
    lru_conv_b = nrm((L, LRU_W), 0.02)
    lru_wa = nrm((L, LRU_HEADS, LRU_HD, LRU_HD), LRU_HD ** -0.5)
    lru_ba = nrm((L, LRU_W), 0.02)
    lru_wx = nrm((L, LRU_HEADS, LRU_HD, LRU_HD), LRU_HD ** -0.5)
    lru_bx = nrm((L, LRU_W), 0.02)
    a_c = jax.random.uniform(next(ks), (L, LRU_W), F32, minval=0.9, maxval=0.999)
    a_base = a_c ** (1.0 / LRU_C)
    lru_lambda = jnp.log(a_base) - jnp.log1p(-a_base)
    gdn_conv_w = nrm((L, GDN_CONV, GDN_HEADS * (2 * GDN_DK + GDN_DV)), GDN_CONV ** -0.5)
    gdn_a_log = jnp.log(jax.random.uniform(next(ks), (L, GDN_HEADS), F32, minval=1.0, maxval=16.0))
    lo, hi = math.log(1e-3), math.log(1e-1)
    dt = jnp.exp(jax.random.uniform(next(ks), (L, GDN_HEADS), F32) * (hi - lo) + lo)
    gdn_dt_bias = dt + jnp.log(-jnp.expm1(-dt))
    gdn_norm_g = gain((L, GDN_DV))
    pool_w = nrm((L, POOL_GROUPS, POOL_GW, POOL_GW), POOL_GW ** -0.5)
    pool_scale = 1.0 + nrm((L, BRANCH_W), 0.1)
    w_branch = nrm((L, N_BRANCH, BRANCH_W, D), BRANCH_W ** -0.5)
    w_out = nrm((L, D, D), D ** -0.5)
    ff2_norm = gain((L, D))
    ff2_wg = nrm((L, D, D_FF), D ** -0.5)
    ff2_wu = nrm((L, D, D_FF), D ** -0.5)
    ff2_wd = nrm((L, D_FF, D), D_FF ** -0.5)
    final_norm = gain((D,))
    return {'x': x, 'ff1_norm': ff1_norm, 'ff1_wg': ff1_wg, 'ff1_wu': ff1_wu, 'ff1_wd': ff1_wd,
            'mix_norm': mix_norm, 'w_in': w_in, 'sgu_ln_g': sgu_ln_g, 'sgu_ln_b': sgu_ln_b,
            'sgu_w': sgu_w, 'sgu_b': sgu_b, 'lru_conv_w': lru_conv_w, 'lru_conv_b': lru_conv_b,
            'lru_wa': lru_wa, 'lru_ba': lru_ba, 'lru_wx': lru_wx, 'lru_bx': lru_bx,
            'lru_lambda': lru_lambda, 'gdn_conv_w': gdn_conv_w, 'gdn_a_log': gdn_a_log,
            'gdn_dt_bias': gdn_dt_bias, 'gdn_norm_g': gdn_norm_g, 'pool_w': pool_w,
            'pool_scale': pool_scale, 'w_branch': w_branch, 'w_out': w_out,
            'ff2_norm': ff2_norm, 'ff2_wg': ff2_wg, 'ff2_wu': ff2_wu, 'ff2_wd': ff2_wd,
            'final_norm': final_norm}


def _fwd_reference(x, ff1_norm, ff1_wg, ff1_wu, ff1_wd, mix_norm, w_in, sgu_ln_g, sgu_ln_b,
              sgu_w, sgu_b, lru_conv_w, lru_conv_b, lru_wa, lru_ba, lru_wx, lru_bx,
              lru_lambda, gdn_conv_w, gdn_a_log, gdn_dt_bias, gdn_norm_g, pool_w,
              pool_scale, w_branch, w_out, ff2_norm, ff2_wg, ff2_wu, ff2_wd, final_norm):
    B, S, _ = x.shape
    split_at = np.cumsum(IN_SIZES)[:-1].tolist()
    for l in range(DEPTH):
        x = x + 0.5 * swiglu(rms_norm(x, ff1_norm[l]), ff1_wg[l], ff1_wu[l], ff1_wd[l])
        h = rms_norm(x, mix_norm[l])
        proj = jnp.einsum('bsd,dp->bsp', h, w_in[l])
        (a_u, a_v, b_x, b_g, c_q, c_k, c_v, c_z, c_beta, c_alpha, d_x,
         gate_pre) = jnp.split(proj, split_at, axis=-1)
        y_a = sgu_mixer(a_u, a_v, sgu_ln_g[l], sgu_ln_b[l], sgu_w[l], sgu_b[l])
        y_b = rglru_mixer(b_x, b_g, lru_conv_w[l], lru_conv_b[l], lru_wa[l], lru_ba[l],
                          lru_wx[l], lru_bx[l], lru_lambda[l])
        y_c = gated_deltanet_mixer(c_q, c_k, c_v, c_z, c_beta, c_alpha, gdn_conv_w[l],
                                   gdn_a_log[l], gdn_dt_bias[l], gdn_norm_g[l])
        y_d = pool_mixer(d_x, pool_w[l], pool_scale[l])
        ys = jnp.stack([y_a, y_b, y_c, y_d], axis=2)
        br = jnp.einsum('bsgc,gcd->bsgd', ys, w_branch[l])
        gates = jax.nn.sigmoid(gate_pre.astype(F32)).astype(x.dtype).reshape(B, S, N_BRANCH, D_MODEL)
        merged = jnp.sum(gates * br, axis=2)
        x = x + jnp.einsum('bsd,de->bse', merged, w_out[l])
        x = x + 0.5 * swiglu(rms_norm(x, ff2_norm[l]), ff2_wg[l], ff2_wu[l], ff2_wd[l])
    return rms_norm(x, final_norm)


import jax as _jax
import jax.numpy as _jnp

TWIN_FORMAT = 'train_step'
FWD_PARAMS = ['x', 'ff1_norm', 'ff1_wg', 'ff1_wu', 'ff1_wd', 'mix_norm', 'w_in', 'sgu_ln_g', 'sgu_ln_b', 'sgu_w', 'sgu_b', 'lru_conv_w', 'lru_conv_b', 'lru_wa', 'lru_ba', 'lru_wx', 'lru_bx', 'lru_lambda', 'gdn_conv_w', 'gdn_a_log', 'gdn_dt_bias', 'gdn_norm_g', 'pool_w', 'pool_scale', 'w_branch', 'w_out', 'ff2_norm', 'ff2_wg', 'ff2_wu', 'ff2_wd', 'final_norm']
TWIN_WEIGHTS = ['ff1_norm', 'ff1_wg', 'ff1_wu', 'ff1_wd', 'mix_norm', 'w_in', 'sgu_ln_g', 'sgu_ln_b', 'sgu_w', 'sgu_b', 'lru_conv_w', 'lru_conv_b', 'lru_wa', 'lru_ba', 'lru_wx', 'lru_bx', 'lru_lambda', 'gdn_conv_w', 'gdn_a_log', 'gdn_dt_bias', 'gdn_norm_g', 'pool_w', 'pool_scale', 'w_branch', 'w_out', 'ff2_norm', 'ff2_wg', 'ff2_wu', 'ff2_wd', 'final_norm']
TWIN_DIFF_INPUT = 'x'
TWIN_INPUTS = ['x', 'ff1_norm', 'ff1_wg', 'ff1_wu', 'ff1_wd', 'mix_norm', 'w_in', 'sgu_ln_g', 'sgu_ln_b', 'sgu_w', 'sgu_b', 'lru_conv_w', 'lru_conv_b', 'lru_wa', 'lru_ba', 'lru_wx', 'lru_bx', 'lru_lambda', 'gdn_conv_w', 'gdn_a_log', 'gdn_dt_bias', 'gdn_norm_g', 'pool_w', 'pool_scale', 'w_branch', 'w_out', 'ff2_norm', 'ff2_wg', 'ff2_wu', 'ff2_wd', 'final_norm', 'loss_target', 'm_ff1_norm', 'm_ff1_wg', 'm_ff1_wu', 'm_ff1_wd', 'm_mix_norm', 'm_w_in', 'm_sgu_ln_g', 'm_sgu_ln_b', 'm_sgu_w', 'm_sgu_b', 'm_lru_conv_w', 'm_lru_conv_b', 'm_lru_wa', 'm_lru_ba', 'm_lru_wx', 'm_lru_bx', 'm_lru_lambda', 'm_gdn_conv_w', 'm_gdn_a_log', 'm_gdn_dt_bias', 'm_gdn_norm_g', 'm_pool_w', 'm_pool_scale', 'm_w_branch', 'm_w_out', 'm_ff2_norm', 'm_ff2_wg', 'm_ff2_wu', 'm_ff2_wd', 'm_final_norm', 'v_ff1_norm', 'v_ff1_wg', 'v_ff1_wu', 'v_ff1_wd', 'v_mix_norm', 'v_w_in', 'v_sgu_ln_g', 'v_sgu_ln_b', 'v_sgu_w', 'v_sgu_b', 'v_lru_conv_w', 'v_lru_conv_b', 'v_lru_wa', 'v_lru_ba', 'v_lru_wx', 'v_lru_bx', 'v_lru_lambda', 'v_gdn_conv_w', 'v_gdn_a_log', 'v_gdn_dt_bias', 'v_gdn_norm_g', 'v_pool_w', 'v_pool_scale', 'v_w_branch', 'v_w_out', 'v_ff2_norm', 'v_ff2_wg', 'v_ff2_wu', 'v_ff2_wd', 'v_final_norm']
TWIN_OUTPUTS = ['loss', 'grad_x', 'grad_ff1_norm', 'grad_ff1_wg', 'grad_ff1_wu', 'grad_ff1_wd', 'grad_mix_norm', 'grad_w_in', 'grad_sgu_ln_g', 'grad_sgu_ln_b', 'grad_sgu_w', 'grad_sgu_b', 'grad_lru_conv_w', 'grad_lru_conv_b', 'grad_lru_wa', 'grad_lru_ba', 'grad_lru_wx', 'grad_lru_bx', 'grad_lru_lambda', 'grad_gdn_conv_w', 'grad_gdn_a_log', 'grad_gdn_dt_bias', 'grad_gdn_norm_g', 'grad_pool_w', 'grad_pool_scale', 'grad_w_branch', 'grad_w_out', 'grad_ff2_norm', 'grad_ff2_wg', 'grad_ff2_wu', 'grad_ff2_wd', 'grad_final_norm', 'delta_ff1_norm', 'delta_ff1_wg', 'delta_ff1_wu', 'delta_ff1_wd', 'delta_mix_norm', 'delta_w_in', 'delta_sgu_ln_g', 'delta_sgu_ln_b', 'delta_sgu_w', 'delta_sgu_b', 'delta_lru_conv_w', 'delta_lru_conv_b', 'delta_lru_wa', 'delta_lru_ba', 'delta_lru_wx', 'delta_lru_bx', 'delta_lru_lambda', 'delta_gdn_conv_w', 'delta_gdn_a_log', 'delta_gdn_dt_bias', 'delta_gdn_norm_g', 'delta_pool_w', 'delta_pool_scale', 'delta_w_branch', 'delta_w_out', 'delta_ff2_norm', 'delta_ff2_wg', 'delta_ff2_wu', 'delta_ff2_wd', 'delta_final_norm', 'new_m_ff1_norm', 'new_m_ff1_wg', 'new_m_ff1_wu', 'new_m_ff1_wd', 'new_m_mix_norm', 'new_m_w_in', 'new_m_sgu_ln_g', 'new_m_sgu_ln_b', 'new_m_sgu_w', 'new_m_sgu_b', 'new_m_lru_conv_w', 'new_m_lru_conv_b', 'new_m_lru_wa', 'new_m_lru_ba', 'new_m_lru_wx', 'new_m_lru_bx', 'new_m_lru_lambda', 'new_m_gdn_conv_w', 'new_m_gdn_a_log', 'new_m_gdn_dt_bias', 'new_m_gdn_norm_g', 'new_m_pool_w', 'new_m_pool_scale', 'new_m_w_branch', 'new_m_w_out', 'new_m_ff2_norm', 'new_m_ff2_wg', 'new_m_ff2_wu', 'new_m_ff2_wd', 'new_m_final_norm', 'new_v_ff1_norm', 'new_v_ff1_wg', 'new_v_ff1_wu', 'new_v_ff1_wd', 'new_v_mix_norm', 'new_v_w_in', 'new_v_sgu_ln_g', 'new_v_sgu_ln_b', 'new_v_sgu_w', 'new_v_sgu_b', 'new_v_lru_conv_w', 'new_v_lru_conv_b', 'new_v_lru_wa', 'new_v_lru_ba', 'new_v_lru_wx', 'new_v_lru_bx', 'new_v_lru_lambda', 'new_v_gdn_conv_w', 'new_v_gdn_a_log', 'new_v_gdn_dt_bias', 'new_v_gdn_norm_g', 'new_v_pool_w', 'new_v_pool_scale', 'new_v_w_branch', 'new_v_w_out', 'new_v_ff2_norm', 'new_v_ff2_wg', 'new_v_ff2_wu', 'new_v_ff2_wd', 'new_v_final_norm']
TWIN_LEAF_KINDS = {'loss': 'loss', 'grad_x': 'grad_x', 'grad_ff1_norm': 'grad_w', 'grad_ff1_wg': 'grad_w', 'grad_ff1_wu': 'grad_w', 'grad_ff1_wd': 'grad_w', 'grad_mix_norm': 'grad_w', 'grad_w_in': 'grad_w', 'grad_sgu_ln_g': 'grad_w', 'grad_sgu_ln_b': 'grad_w', 'grad_sgu_w': 'grad_w', 'grad_sgu_b': 'grad_w', 'grad_lru_conv_w': 'grad_w', 'grad_lru_conv_b': 'grad_w', 'grad_lru_wa': 'grad_w', 'grad_lru_ba': 'grad_w', 'grad_lru_wx': 'grad_w', 'grad_lru_bx': 'grad_w', 'grad_lru_lambda': 'grad_w', 'grad_gdn_conv_w': 'grad_w', 'grad_gdn_a_log': 'grad_w', 'grad_gdn_dt_bias': 'grad_w', 'grad_gdn_norm_g': 'grad_w', 'grad_pool_w': 'grad_w', 'grad_pool_scale': 'grad_w', 'grad_w_branch': 'grad_w', 'grad_w_out': 'grad_w', 'grad_ff2_norm': 'grad_w', 'grad_ff2_wg': 'grad_w', 'grad_ff2_wu': 'grad_w', 'grad_ff2_wd': 'grad_w', 'grad_final_norm': 'grad_w', 'delta_ff1_norm': 'delta_w', 'delta_ff1_wg': 'delta_w', 'delta_ff1_wu': 'delta_w', 'delta_ff1_wd': 'delta_w', 'delta_mix_norm': 'delta_w', 'delta_w_in': 'delta_w', 'delta_sgu_ln_g': 'delta_w', 'delta_sgu_ln_b': 'delta_w', 'delta_sgu_w': 'delta_w', 'delta_sgu_b': 'delta_w', 'delta_lru_conv_w': 'delta_w', 'delta_lru_conv_b': 'delta_w', 'delta_lru_wa': 'delta_w', 'delta_lru_ba': 'delta_w', 'delta_lru_wx': 'delta_w', 'delta_lru_bx': 'delta_w', 'delta_lru_lambda': 'delta_w', 'delta_gdn_conv_w': 'delta_w', 'delta_gdn_a_log': 'delta_w', 'delta_gdn_dt_bias': 'delta_w', 'delta_gdn_norm_g': 'delta_w', 'delta_pool_w': 'delta_w', 'delta_pool_scale': 'delta_w', 'delta_w_branch': 'delta_w', 'delta_w_out': 'delta_w', 'delta_ff2_norm': 'delta_w', 'delta_ff2_wg': 'delta_w', 'delta_ff2_wu': 'delta_w', 'delta_ff2_wd': 'delta_w', 'delta_final_norm': 'delta_w', 'new_m_ff1_norm': 'new_m', 'new_m_ff1_wg': 'new_m', 'new_m_ff1_wu': 'new_m', 'new_m_ff1_wd': 'new_m', 'new_m_mix_norm': 'new_m', 'new_m_w_in': 'new_m', 'new_m_sgu_ln_g': 'new_m', 'new_m_sgu_ln_b': 'new_m', 'new_m_sgu_w': 'new_m', 'new_m_sgu_b': 'new_m', 'new_m_lru_conv_w': 'new_m', 'new_m_lru_conv_b': 'new_m', 'new_m_lru_wa': 'new_m', 'new_m_lru_ba': 'new_m', 'new_m_lru_wx': 'new_m', 'new_m_lru_bx': 'new_m', 'new_m_lru_lambda': 'new_m', 'new_m_gdn_conv_w': 'new_m', 'new_m_gdn_a_log': 'new_m', 'new_m_gdn_dt_bias': 'new_m', 'new_m_gdn_norm_g': 'new_m', 'new_m_pool_w': 'new_m', 'new_m_pool_scale': 'new_m', 'new_m_w_branch': 'new_m', 'new_m_w_out': 'new_m', 'new_m_ff2_norm': 'new_m', 'new_m_ff2_wg': 'new_m', 'new_m_ff2_wu': 'new_m', 'new_m_ff2_wd': 'new_m', 'new_m_final_norm': 'new_m', 'new_v_ff1_norm': 'new_v', 'new_v_ff1_wg': 'new_v', 'new_v_ff1_wu': 'new_v', 'new_v_ff1_wd': 'new_v', 'new_v_mix_norm': 'new_v', 'new_v_w_in': 'new_v', 'new_v_sgu_ln_g': 'new_v', 'new_v_sgu_ln_b': 'new_v', 'new_v_sgu_w': 'new_v', 'new_v_sgu_b': 'new_v', 'new_v_lru_conv_w': 'new_v', 'new_v_lru_conv_b': 'new_v', 'new_v_lru_wa': 'new_v', 'new_v_lru_ba': 'new_v', 'new_v_lru_wx': 'new_v', 'new_v_lru_bx': 'new_v', 'new_v_lru_lambda': 'new_v', 'new_v_gdn_conv_w': 'new_v', 'new_v_gdn_a_log': 'new_v', 'new_v_gdn_dt_bias': 'new_v', 'new_v_gdn_norm_g': 'new_v', 'new_v_pool_w': 'new_v', 'new_v_pool_scale': 'new_v', 'new_v_w_branch': 'new_v', 'new_v_w_out': 'new_v', 'new_v_ff2_norm': 'new_v', 'new_v_ff2_wg': 'new_v', 'new_v_ff2_wu': 'new_v', 'new_v_ff2_wd': 'new_v', 'new_v_final_norm': 'new_v'}


def _forward(args):
    return _fwd_reference(*[args[k] for k in FWD_PARAMS])


def _output_shape():
    out = _jax.eval_shape(lambda: _forward(_fwd_setup_inputs(0)))
    return out.shape, out.dtype

N_MICROBATCH = 1
ADAM_LR = 0.001
ADAM_B1 = 0.9
ADAM_B2 = 0.999
ADAM_EPS = 1e-08
ADAM_WD = 0.01
ADAM_STEP = 10
PER_EXAMPLE_BATCH_AXIS = {'x': 0, 'loss_target': 0}
SHARED_INPUTS = []
_WEIGHT_DTYPES = {'ff1_norm': _jnp.float32, 'ff1_wg': _jnp.float32, 'ff1_wu': _jnp.float32, 'ff1_wd': _jnp.float32, 'mix_norm': _jnp.float32, 'w_in': _jnp.float32, 'sgu_ln_g': _jnp.float32, 'sgu_ln_b': _jnp.float32, 'sgu_w': _jnp.float32, 'sgu_b': _jnp.float32, 'lru_conv_w': _jnp.float32, 'lru_conv_b': _jnp.float32, 'lru_wa': _jnp.float32, 'lru_ba': _jnp.float32, 'lru_wx': _jnp.float32, 'lru_bx': _jnp.float32, 'lru_lambda': _jnp.float32, 'gdn_conv_w': _jnp.float32, 'gdn_a_log': _jnp.float32, 'gdn_dt_bias': _jnp.float32, 'gdn_norm_g': _jnp.float32, 'pool_w': _jnp.float32, 'pool_scale': _jnp.float32, 'w_branch': _jnp.float32, 'w_out': _jnp.float32, 'ff2_norm': _jnp.float32, 'ff2_wg': _jnp.float32, 'ff2_wu': _jnp.float32, 'ff2_wd': _jnp.float32, 'final_norm': _jnp.float32}
MOMENT_SCALE = {'ff1_norm': 8.096842e-02, 'ff1_wg': 3.426067e-02, 'ff1_wu': 3.312581e-02, 'ff1_wd': 5.490497e-02, 'mix_norm': 1.476549e-01, 'w_in': 4.734818e-02, 'sgu_ln_g': 3.066145e-02, 'sgu_ln_b': 3.089124e-02, 'sgu_w': 6.103829e-02, 'sgu_b': 7.139596e-02, 'lru_conv_w': 6.983735e-02, 'lru_conv_b': 7.064703e-01, 'lru_wa': 2.337589e-02, 'lru_ba': 2.039689e-02, 'lru_wx': 4.293709e-02, 'lru_bx': 2.401087e-02, 'lru_lambda': 3.791538e-02, 'gdn_conv_w': 4.826058e-02, 'gdn_a_log': 6.751195e-01, 'gdn_dt_bias': 6.703460e-01, 'gdn_norm_g': 1.307671e-01, 'pool_w': 9.416826e-02, 'pool_scale': 9.501886e-02, 'w_branch': 5.489182e-02, 'w_out': 1.103458e-01, 'ff2_norm': 5.829655e-02, 'ff2_wg': 2.508049e-02, 'ff2_wu': 2.422469e-02, 'ff2_wd': 4.023957e-02, 'final_norm': 3.206643e+01}


def _to_microbatches(a, axis):
    t = _jnp.moveaxis(a, axis, 0)
    t = t.reshape((N_MICROBATCH, t.shape[0] // N_MICROBATCH) + t.shape[1:])
    return _jnp.moveaxis(t, 1, axis + 1)


def setup_inputs(seed: int = 0) -> dict:
    inp = _fwd_setup_inputs(seed)
    key = _jax.random.fold_in(_jax.random.key(seed), 7919)
    shape, _ = _output_shape()
    out = dict(inp)
    out["loss_target"] = _jax.random.normal(_jax.random.fold_in(key, 0), shape, _jnp.float32)
    for i, name in enumerate(TWIN_WEIGHTS):
        w = inp[name].astype(_jnp.float32)
        if MOMENT_SCALE is None:
            s = _jnp.sqrt(_jnp.mean(_jnp.square(w)) + 1e-30)
        else:
            s = MOMENT_SCALE[name]
        km, kv = _jax.random.split(_jax.random.fold_in(key, i + 1))
        out[name] = w
        out["m_" + name] = s * _jax.random.normal(km, w.shape, _jnp.float32)
        out["v_" + name] = (s * s) * _jax.random.uniform(kv, w.shape, _jnp.float32, 0.5, 1.5)
    if N_MICROBATCH > 1:
        for name, axis in PER_EXAMPLE_BATCH_AXIS.items():
            out[name] = _to_microbatches(out[name], axis)
    return {'x': out['x'], 'ff1_norm': out['ff1_norm'], 'ff1_wg': out['ff1_wg'], 'ff1_wu': out['ff1_wu'], 'ff1_wd': out['ff1_wd'], 'mix_norm': out['mix_norm'], 'w_in': out['w_in'], 'sgu_ln_g': out['sgu_ln_g'], 'sgu_ln_b': out['sgu_ln_b'], 'sgu_w': out['sgu_w'], 'sgu_b': out['sgu_b'], 'lru_conv_w': out['lru_conv_w'], 'lru_conv_b': out['lru_conv_b'], 'lru_wa': out['lru_wa'], 'lru_ba': out['lru_ba'], 'lru_wx': out['lru_wx'], 'lru_bx': out['lru_bx'], 'lru_lambda': out['lru_lambda'], 'gdn_conv_w': out['gdn_conv_w'], 'gdn_a_log': out['gdn_a_log'], 'gdn_dt_bias': out['gdn_dt_bias'], 'gdn_norm_g': out['gdn_norm_g'], 'pool_w': out['pool_w'], 'pool_scale': out['pool_scale'], 'w_branch': out['w_branch'], 'w_out': out['w_out'], 'ff2_norm': out['ff2_norm'], 'ff2_wg': out['ff2_wg'], 'ff2_wu': out['ff2_wu'], 'ff2_wd': out['ff2_wd'], 'final_norm': out['final_norm'], 'loss_target': out['loss_target'], 'm_ff1_norm': out['m_ff1_norm'], 'm_ff1_wg': out['m_ff1_wg'], 'm_ff1_wu': out['m_ff1_wu'], 'm_ff1_wd': out['m_ff1_wd'], 'm_mix_norm': out['m_mix_norm'], 'm_w_in': out['m_w_in'], 'm_sgu_ln_g': out['m_sgu_ln_g'], 'm_sgu_ln_b': out['m_sgu_ln_b'], 'm_sgu_w': out['m_sgu_w'], 'm_sgu_b': out['m_sgu_b'], 'm_lru_conv_w': out['m_lru_conv_w'], 'm_lru_conv_b': out['m_lru_conv_b'], 'm_lru_wa': out['m_lru_wa'], 'm_lru_ba': out['m_lru_ba'], 'm_lru_wx': out['m_lru_wx'], 'm_lru_bx': out['m_lru_bx'], 'm_lru_lambda': out['m_lru_lambda'], 'm_gdn_conv_w': out['m_gdn_conv_w'], 'm_gdn_a_log': out['m_gdn_a_log'], 'm_gdn_dt_bias': out['m_gdn_dt_bias'], 'm_gdn_norm_g': out['m_gdn_norm_g'], 'm_pool_w': out['m_pool_w'], 'm_pool_scale': out['m_pool_scale'], 'm_w_branch': out['m_w_branch'], 'm_w_out': out['m_w_out'], 'm_ff2_norm': out['m_ff2_norm'], 'm_ff2_wg': out['m_ff2_wg'], 'm_ff2_wu': out['m_ff2_wu'], 'm_ff2_wd': out['m_ff2_wd'], 'm_final_norm': out['m_final_norm'], 'v_ff1_norm': out['v_ff1_norm'], 'v_ff1_wg': out['v_ff1_wg'], 'v_ff1_wu': out['v_ff1_wu'], 'v_ff1_wd': out['v_ff1_wd'], 'v_mix_norm': out['v_mix_norm'], 'v_w_in': out['v_w_in'], 'v_sgu_ln_g': out['v_sgu_ln_g'], 'v_sgu_ln_b': out['v_sgu_ln_b'], 'v_sgu_w': out['v_sgu_w'], 'v_sgu_b': out['v_sgu_b'], 'v_lru_conv_w': out['v_lru_conv_w'], 'v_lru_conv_b': out['v_lru_conv_b'], 'v_lru_wa': out['v_lru_wa'], 'v_lru_ba': out['v_lru_ba'], 'v_lru_wx': out['v_lru_wx'], 'v_lru_bx': out['v_lru_bx'], 'v_lru_lambda': out['v_lru_lambda'], 'v_gdn_conv_w': out['v_gdn_conv_w'], 'v_gdn_a_log': out['v_gdn_a_log'], 'v_gdn_dt_bias': out['v_gdn_dt_bias'], 'v_gdn_norm_g': out['v_gdn_norm_g'], 'v_pool_w': out['v_pool_w'], 'v_pool_scale': out['v_pool_scale'], 'v_w_branch': out['v_w_branch'], 'v_w_out': out['v_w_out'], 'v_ff2_norm': out['v_ff2_norm'], 'v_ff2_wg': out['v_ff2_wg'], 'v_ff2_wu': out['v_ff2_wu'], 'v_ff2_wd': out['v_ff2_wd'], 'v_final_norm': out['v_final_norm']}


def _loss(weights, diff, rest, loss_target):
    with _jax.named_scope("forward"):
        args = {**rest, TWIN_DIFF_INPUT: diff, **{k: w.astype(_WEIGHT_DTYPES[k]) for k, w in weights.items()}}
        y = _forward(args)
    with _jax.named_scope("loss_head"):
        err = _jnp.square(y.astype(_jnp.float32) - loss_target)
        return 0.5 * _jnp.sum(_jnp.mean(err, axis=-1)) if err.ndim else 0.5 * err


def _adamw(w, g, m, v):
    m = ADAM_B1 * m + (1.0 - ADAM_B1) * g
    v = ADAM_B2 * v + (1.0 - ADAM_B2) * _jnp.square(g)
    m_hat = m / (1.0 - ADAM_B1 ** ADAM_STEP)
    v_hat = v / (1.0 - ADAM_B2 ** ADAM_STEP)
    delta = -ADAM_LR * (m_hat / (_jnp.sqrt(v_hat) + ADAM_EPS) + ADAM_WD * w)
    return delta, m, v


def reference(x, ff1_norm, ff1_wg, ff1_wu, ff1_wd, mix_norm, w_in, sgu_ln_g, sgu_ln_b, sgu_w, sgu_b, lru_conv_w, lru_conv_b, lru_wa, lru_ba, lru_wx, lru_bx, lru_lambda, gdn_conv_w, gdn_a_log, gdn_dt_bias, gdn_norm_g, pool_w, pool_scale, w_branch, w_out, ff2_norm, ff2_wg, ff2_wu, ff2_wd, final_norm, loss_target, m_ff1_norm, m_ff1_wg, m_ff1_wu, m_ff1_wd, m_mix_norm, m_w_in, m_sgu_ln_g, m_sgu_ln_b, m_sgu_w, m_sgu_b, m_lru_conv_w, m_lru_conv_b, m_lru_wa, m_lru_ba, m_lru_wx, m_lru_bx, m_lru_lambda, m_gdn_conv_w, m_gdn_a_log, m_gdn_dt_bias, m_gdn_norm_g, m_pool_w, m_pool_scale, m_w_branch, m_w_out, m_ff2_norm, m_ff2_wg, m_ff2_wu, m_ff2_wd, m_final_norm, v_ff1_norm, v_ff1_wg, v_ff1_wu, v_ff1_wd, v_mix_norm, v_w_in, v_sgu_ln_g, v_sgu_ln_b, v_sgu_w, v_sgu_b, v_lru_conv_w, v_lru_conv_b, v_lru_wa, v_lru_ba, v_lru_wx, v_lru_bx, v_lru_lambda, v_gdn_conv_w, v_gdn_a_log, v_gdn_dt_bias, v_gdn_norm_g, v_pool_w, v_pool_scale, v_w_branch, v_w_out, v_ff2_norm, v_ff2_wg, v_ff2_wu, v_ff2_wd, v_final_norm):
    given = dict(x=x, ff1_norm=ff1_norm, ff1_wg=ff1_wg, ff1_wu=ff1_wu, ff1_wd=ff1_wd, mix_norm=mix_norm, w_in=w_in, sgu_ln_g=sgu_ln_g, sgu_ln_b=sgu_ln_b, sgu_w=sgu_w, sgu_b=sgu_b, lru_conv_w=lru_conv_w, lru_conv_b=lru_conv_b, lru_wa=lru_wa, lru_ba=lru_ba, lru_wx=lru_wx, lru_bx=lru_bx, lru_lambda=lru_lambda, gdn_conv_w=gdn_conv_w, gdn_a_log=gdn_a_log, gdn_dt_bias=gdn_dt_bias, gdn_norm_g=gdn_norm_g, pool_w=pool_w, pool_scale=pool_scale, w_branch=w_branch, w_out=w_out, ff2_norm=ff2_norm, ff2_wg=ff2_wg, ff2_wu=ff2_wu, ff2_wd=ff2_wd, final_norm=final_norm, loss_target=loss_target, m_ff1_norm=m_ff1_norm, m_ff1_wg=m_ff1_wg, m_ff1_wu=m_ff1_wu, m_ff1_wd=m_ff1_wd, m_mix_norm=m_mix_norm, m_w_in=m_w_in, m_sgu_ln_g=m_sgu_ln_g, m_sgu_ln_b=m_sgu_ln_b, m_sgu_w=m_sgu_w, m_sgu_b=m_sgu_b, m_lru_conv_w=m_lru_conv_w, m_lru_conv_b=m_lru_conv_b, m_lru_wa=m_lru_wa, m_lru_ba=m_lru_ba, m_lru_wx=m_lru_wx, m_lru_bx=m_lru_bx, m_lru_lambda=m_lru_lambda, m_gdn_conv_w=m_gdn_conv_w, m_gdn_a_log=m_gdn_a_log, m_gdn_dt_bias=m_gdn_dt_bias, m_gdn_norm_g=m_gdn_norm_g, m_pool_w=m_pool_w, m_pool_scale=m_pool_scale, m_w_branch=m_w_branch, m_w_out=m_w_out, m_ff2_norm=m_ff2_norm, m_ff2_wg=m_ff2_wg, m_ff2_wu=m_ff2_wu, m_ff2_wd=m_ff2_wd, m_final_norm=m_final_norm, v_ff1_norm=v_ff1_norm, v_ff1_wg=v_ff1_wg, v_ff1_wu=v_ff1_wu, v_ff1_wd=v_ff1_wd, v_mix_norm=v_mix_norm, v_w_in=v_w_in, v_sgu_ln_g=v_sgu_ln_g, v_sgu_ln_b=v_sgu_ln_b, v_sgu_w=v_sgu_w, v_sgu_b=v_sgu_b, v_lru_conv_w=v_lru_conv_w, v_lru_conv_b=v_lru_conv_b, v_lru_wa=v_lru_wa, v_lru_ba=v_lru_ba, v_lru_wx=v_lru_wx, v_lru_bx=v_lru_bx, v_lru_lambda=v_lru_lambda, v_gdn_conv_w=v_gdn_conv_w, v_gdn_a_log=v_gdn_a_log, v_gdn_dt_bias=v_gdn_dt_bias, v_gdn_norm_g=v_gdn_norm_g, v_pool_w=v_pool_w, v_pool_scale=v_pool_scale, v_w_branch=v_w_branch, v_w_out=v_w_out, v_ff2_norm=v_ff2_norm, v_ff2_wg=v_ff2_wg, v_ff2_wu=v_ff2_wu, v_ff2_wd=v_ff2_wd, v_final_norm=v_final_norm)
    weights = {n: given[n] for n in TWIN_WEIGHTS}
    shared = {n: given[n] for n in SHARED_INPUTS}
    per_example = {n: given[n] for n in ['x']}
    grad_fn = _jax.value_and_grad(_loss, argnums=(0, 1))

    def one_microbatch(ex, loss_target):
        ex = dict(ex)
        diff = ex.pop(TWIN_DIFF_INPUT)
        return grad_fn(weights, diff, {**shared, **ex}, loss_target)

    if N_MICROBATCH == 1:
        loss, (grad_w, grad_x) = one_microbatch(per_example, given["loss_target"])
    else:
        def body(carry, xs):
            loss_sum, grad_sum = carry
            l_k, (gw_k, gx_k) = one_microbatch(xs[0], xs[1])
            with _jax.named_scope("update"):
                return (loss_sum + l_k, _jax.tree.map(_jnp.add, grad_sum, gw_k)), gx_k

        init = (_jnp.zeros((), _jnp.float32), _jax.tree.map(_jnp.zeros_like, weights))
        (loss, grad_w), grad_x = _jax.lax.scan(body, init, (per_example, given["loss_target"]))
    with _jax.named_scope("update"):
        delta_w, new_m, new_v = {}, {}, {}
        for n in TWIN_WEIGHTS:
            delta_w[n], new_m[n], new_v[n] = _adamw(weights[n], grad_w[n], given["m_" + n], given["v_" + n])
    return (loss, grad_x, *[grad_w[n] for n in TWIN_WEIGHTS], *[delta_w[n] for n in TWIN_WEIGHTS],
            *[new_m[n] for n in TWIN_WEIGHTS], *[new_v[n] for n in TWIN_WEIGHTS])
```

```python
import functools
import math

import jax
import jax.numpy as jnp
from jax import lax
from jax.experimental import pallas as pl
from jax.experimental.pallas import tpu as pltpu

F32 = jnp.float32
BF16 = jnp.bfloat16
MESH = pl.DeviceIdType.MESH

D_MODEL = 1024
N_SHARD = 4
D_FF = 2816
FF_SHARD = D_FF // N_SHARD
BRANCH_W = 512
CHUNK = 64
GDN_HEADS = 4
GDN_DK = 128
LRU_C = 8.0
EPS = 1e-6
P_IN = 8712
P_IN_SHARD = P_IN // N_SHARD
P_IN_SHARD_PAD = 2304
P_PAD = 9216
C_QKV, C_Z, C_AU, C_AV, C_BX, C_BG, C_DX, C_BA, C_GATE = 0, 1536, 2048, 2560, 3072, 3584, 4096, 4608, 5120

ADAM_LR, ADAM_B1, ADAM_B2, ADAM_EPS, ADAM_WD, ADAM_STEP = 0.001, 0.9, 0.999, 1e-08, 0.01, 10


def _gelu(x):
    return 0.5 * x * (1.0 + jnp.tanh(0.7978845608028654 * (x + 0.044715 * (x * x * x))))


def _gelu_grad(x):
    u = 0.7978845608028654 * (x + 0.044715 * (x * x * x))
    t = jnp.tanh(u)
    return 0.5 * (1.0 + t) + 0.5 * x * (1.0 - t * t) * 0.7978845608028654 * (1.0 + 3.0 * 0.044715 * x * x)


def _silu(x):
    return x * jax.nn.sigmoid(x)


def _softplus(x):
    return jnp.maximum(x, 0.0) + jnp.log1p(jnp.exp(-jnp.abs(x)))


def _dg(a, b, ca, cb):
    return lax.dot_general(a.astype(BF16), b.astype(BF16), (((ca,), (cb,)), ((), ())),
                           preferred_element_type=F32)


def _dg3(a, b, ca, cb):
    a1 = a.astype(BF16)
    a2 = (a - a1.astype(F32)).astype(BF16)
    b1 = b.astype(BF16)
    b2 = (b - b1.astype(F32)).astype(BF16)
    dn = (((ca,), (cb,)), ((), ()))
    d = functools.partial(lax.dot_general, dimension_numbers=dn, preferred_element_type=F32)
    return d(a1, b1) + (d(a1, b2) + d(a2, b1))


def _make_mm(ca, cb, dot):
    @jax.custom_vjp
    def f(a, b):
        return dot(a, b, ca, cb)

    def fwd(a, b):
        return dot(a, b, ca, cb), (a, b)

    def bwd(res, g):
        a, b = res
        if (ca, cb) == (1, 0):
            return dot(g, b, 1, 1), dot(a, g, 0, 0)
        if (ca, cb) == (1, 1):
            return dot(g, b, 1, 0), dot(g, a, 0, 0)
        return dot(b, g, 1, 1), dot(a, g, 1, 0)

    f.defvjp(fwd, bwd)
    return f


mm_nn = _make_mm(1, 0, _dg)
mm_nt = _make_mm(1, 1, _dg)
mm_tn = _make_mm(0, 0, _dg)
mm3_nn = _make_mm(1, 0, _dg3)


@jax.custom_vjp
def _tri_inv(a):
    n = a.shape[0]
    ii = lax.broadcasted_iota(jnp.int32, (n, n), 0)
    jj = lax.broadcasted_iota(jnp.int32, (n, n), 1)
    eye = (ii == jj).astype(F32)
    m = -a
    p = eye + m
    k = 1
    while 2 * k < n:
        m = _dg3(m, m, 1, 0)
        p = p + _dg3(p, m, 1, 0)
        k *= 2
    return p


def _tri_inv_fwd(a):
    t = _tri_inv(a)
    return t, t


def _tri_inv_bwd(t, dt):
    x = _dg3(t, dt, 0, 0)
    return (-_dg3(x, t, 1, 1),)


_tri_inv.defvjp(_tri_inv_fwd, _tri_inv_bwd)


def _rows(shape):
    return lax.broadcasted_iota(jnp.int32, shape, 0)


def _lanes(shape):
    return lax.broadcasted_iota(jnp.int32, shape, 1)


def _col(x, j):
    return jnp.sum(jnp.where(_lanes(x.shape) == j, x, 0.0), axis=1, keepdims=True)


def _put_col(col, j, width):
    shape = (col.shape[0], width)
    return jnp.where(_lanes(shape) == j, jnp.broadcast_to(col, shape), 0.0)


def _params(*sem):
    return pltpu.CompilerParams(dimension_semantics=sem)


def matmul(a, b, *, ta=False, tb=False, tm, tn, tk, out_dtype=F32, name):
    ba, bb = a.shape[:-2], b.shape[:-2]
    batch = ba if len(ba) >= len(bb) else bb
    assert ba in ((), batch) and bb in ((), batch)
    nb = len(batch)
    (m, k) = (a.shape[-1], a.shape[-2]) if ta else (a.shape[-2], a.shape[-1])
    (k2, n) = (b.shape[-1], b.shape[-2]) if tb else (b.shape[-2], b.shape[-1])
    tm, tn, tk = min(tm, m), min(tn, n), min(tk, k)
    assert k == k2 and m % tm == 0 and n % tn == 0 and k % tk == 0, (a.shape, b.shape, tm, tn, tk)
    nk = k // tk
    grid = batch + (m // tm, n // tn, nk)

    def a_map(*g):
        i, kk = g[nb], g[nb + 2]
        return (g[:nb] if ba else ()) + ((kk, i) if ta else (i, kk))

    def b_map(*g):
        j, kk = g[nb + 1], g[nb + 2]
        return (g[:nb] if bb else ()) + ((j, kk) if tb else (kk, j))

    def o_map(*g):
        return g[:nb] + (g[nb], g[nb + 1])

    a_blk = (None,) * len(ba) + ((tk, tm) if ta else (tm, tk))
    b_blk = (None,) * len(bb) + ((tn, tk) if tb else (tk, tn))
    o_blk = (None,) * nb + (tm, tn)

    def body(a_ref, b_ref, o_ref, acc_ref):
        kk = pl.program_id(nb + 2)

        @pl.when(kk == 0)
        def _():
            acc_ref[...] = jnp.zeros_like(acc_ref)

        acc_ref[...] += _dg(a_ref[...], b_ref[...], 0 if ta else 1, 1 if tb else 0)

        @pl.when(kk == nk - 1)
        def _():
            o_ref[...] = acc_ref[...].astype(o_ref.dtype)

    return pl.pallas_call(
        body, name=name, grid=grid,
        in_specs=[pl.BlockSpec(a_blk, a_map), pl.BlockSpec(b_blk, b_map)],
        out_specs=pl.BlockSpec(o_blk, o_map),
        out_shape=jax.ShapeDtypeStruct(batch + (m, n), out_dtype),
        scratch_shapes=[pltpu.VMEM((tm, tn), F32)],
        compiler_params=_params(*(("parallel",) * (nb + 2) + ("arbitrary",))),
    )(a, b)


def norm_fwd(x, g, *, tb, name):
    s, d = x.shape

    def body(x_ref, g_ref, h_ref):
        xv = x_ref[...]
        r = lax.rsqrt(jnp.mean(xv * xv, axis=-1, keepdims=True) + EPS)
        h_ref[...] = (xv * r * g_ref[...]).astype(BF16)

    return pl.pallas_call(
        body, name=name, grid=(s // tb,),
        in_specs=[pl.BlockSpec((tb, d), lambda i: (i, 0)), pl.BlockSpec((1, d), lambda i: (0, 0))],
        out_specs=pl.BlockSpec((tb, d), lambda i: (i, 0)),
        out_shape=jax.ShapeDtypeStruct((s, d), BF16),
        compiler_params=_params("parallel"),
    )(x, g)


def _rms_bwd(xv, gv, dh):
    r = lax.rsqrt(jnp.mean(xv * xv, axis=-1, keepdims=True) + EPS)
    xh = xv * r
    dxh = dh * gv
    dx = r * (dxh - xh * jnp.mean(dxh * xh, axis=-1, keepdims=True))
    return dx, jnp.sum(dh * xh, axis=0, keepdims=True)


def norm_bwd(x, g, dh, dres, *, tb, name):
    s, d = x.shape

    def body(x_ref, g_ref, dh_ref, dres_ref, dx_ref, dg_ref):
        dx, dg = _rms_bwd(x_ref[...], g_ref[...], dh_ref[...])
        dx_ref[...] = dres_ref[...] + dx

        @pl.when(pl.program_id(0) == 0)
        def _():
            dg_ref[...] = jnp.zeros_like(dg_ref)

        dg_ref[...] += dg

    row = pl.BlockSpec((tb, d), lambda i: (i, 0))
    one = pl.BlockSpec((1, d), lambda i: (0, 0))
    return pl.pallas_call(
        body, name=name, grid=(s // tb,),
        in_specs=[row, one, row, row], out_specs=[row, one],
        out_shape=[jax.ShapeDtypeStruct((s, d), F32), jax.ShapeDtypeStruct((1, d), F32)],
        compiler_params=_params("arbitrary"),
    )(x, g, dh, dres)


def ffn_fwd(x, g, wg, wu, wd, layer, *, tb, name):
    s, d = x.shape
    fc = wd.shape[-2]

    def body(x_ref, g_ref, wg_ref, wu_ref, wd_ref, o_ref, h_scr, acc_scr):
        j = pl.program_id(1)

        @pl.when(j == 0)
        def _():
            xv = x_ref[...]
            r = lax.rsqrt(jnp.mean(xv * xv, axis=-1, keepdims=True) + EPS)
            h_scr[...] = (xv * r * g_ref[...]).astype(BF16)
            acc_scr[...] = jnp.zeros_like(acc_scr)

        h = h_scr[...]
        a = _dg(h, wg_ref[...], 1, 0)
        b = _dg(h, wu_ref[...], 1, 0)
        acc_scr[...] += _dg(_silu(a) * b, wd_ref[...], 1, 0)

        @pl.when(j == N_SHARD - 1)
        def _():
            o_ref[...] = x_ref[...] + 0.5 * acc_scr[...]

    row = pl.BlockSpec((tb, d), lambda i, j: (i, 0))
    w_in = pl.BlockSpec((None, None, d, fc), lambda i, j: (j, layer, 0, 0))
    return pl.pallas_call(
        body, name=name, grid=(s // tb, N_SHARD),
        in_specs=[row, pl.BlockSpec((1, d), lambda i, j: (0, 0)), w_in, w_in,
                  pl.BlockSpec((None, None, fc, d), lambda i, j: (j, layer, 0, 0))],
        out_specs=row,
        out_shape=jax.ShapeDtypeStruct((s, d), F32),
        scratch_shapes=[pltpu.VMEM((tb, d), BF16), pltpu.VMEM((tb, d), F32)],
        compiler_params=_params("parallel", "arbitrary"),
    )(x, g, wg, wu, wd)


def ffn_bwd(x, dxo, g, wg, wu, wd, layer, *, tb, name):
    s, d = x.shape
    fc = wd.shape[-2]

    def body(x_ref, dxo_ref, g_ref, wg_ref, wu_ref, wd_ref,
             dx_ref, dg_ref, dab_ref, hid_ref, h_ref, dy_ref, dh_scr):
        i, j = pl.program_id(0), pl.program_id(1)

        @pl.when(j == 0)
        def _():
            xv = x_ref[...]
            r = lax.rsqrt(jnp.mean(xv * xv, axis=-1, keepdims=True) + EPS)
            h_ref[...] = (xv * r * g_ref[...]).astype(BF16)
            dy_ref[...] = (0.5 * dxo_ref[...]).astype(BF16)
            dh_scr[...] = jnp.zeros_like(dh_scr)

        h = h_ref[...]
        a = _dg(h, wg_ref[...], 1, 0)
        b = _dg(h, wu_ref[...], 1, 0)
        sg = jax.nn.sigmoid(a)
        sl = a * sg
        dhid = _dg(dy_ref[...], wd_ref[...], 1, 1)
        da = (dhid * b * (sg * (1.0 + a * (1.0 - sg)))).astype(BF16)
        db = (dhid * sl).astype(BF16)
        dab_ref[0] = da
        dab_ref[1] = db
        hid_ref[...] = (sl * b).astype(BF16)
        dh_scr[...] += _dg(da, wg_ref[...], 1, 1) + _dg(db, wu_ref[...], 1, 1)

        @pl.when(jnp.logical_and(i == 0, j == 0))
        def _():
            dg_ref[...] = jnp.zeros_like(dg_ref)

        @pl.when(j == N_SHARD - 1)
        def _():
            dx, dg = _rms_bwd(x_ref[...], g_ref[...], dh_scr[...])
            dx_ref[...] = dxo_ref[...] + dx
            dg_ref[...] += dg

    row = pl.BlockSpec((tb, d), lambda i, j: (i, 0))
    one = pl.BlockSpec((1, d), lambda i, j: (0, 0))
    w_in = pl.BlockSpec((None, None, d, fc), lambda i, j: (j, layer, 0, 0))
    return pl.pallas_call(
        body, name=name, grid=(s // tb, N_SHARD),
        in_specs=[row, row, one, w_in, w_in,
                  pl.BlockSpec((None, None, fc, d), lambda i, j: (j, layer, 0, 0))],
        out_specs=[row, one,
                   pl.BlockSpec((None, 2, tb, fc), lambda i, j: (j, 0, i, 0)),
                   pl.BlockSpec((None, tb, fc), lambda i, j: (j, i, 0)),
                   row, row],
        out_shape=[jax.ShapeDtypeStruct((s, d), F32), jax.ShapeDtypeStruct((1, d), F32),
                   jax.ShapeDtypeStruct((N_SHARD, 2, s, fc), BF16),
                   jax.ShapeDtypeStruct((N_SHARD, s, fc), BF16),
                   jax.ShapeDtypeStruct((s, d), BF16), jax.ShapeDtypeStruct((s, d), BF16)],
        scratch_shapes=[pltpu.VMEM((tb, d), F32)],
        compiler_params=_params("arbitrary", "arbitrary"),
    )(x, dxo, g, wg, wu, wd)


def _pblk(tb, width, col0, rev_n=None):
    assert col0 % width == 0
    cb = col0 // width
    if rev_n is None:
        return pl.BlockSpec((tb, width), lambda i: (i, cb))
    return pl.BlockSpec((tb, width), lambda i: (rev_n - 1 - i, cb))


def _full(shape):
    return pl.BlockSpec(shape, lambda i: (0,) * len(shape))


def conv_fwd(src, col0, width, w, bias, *, tb, name):
    s = src.shape[0]
    cb = col0 // width

    def body(x_ref, p_ref, w_ref, b_ref, y_ref):
        i = pl.program_id(0)
        prev = jnp.where(i > 0, p_ref[...], 0.0)
        xe = jnp.concatenate([prev, x_ref[...]], axis=0)
        wv = w_ref[...]
        acc = b_ref[...] + wv[3:4] * xe[8:]
        for k in range(3):
            acc = acc + wv[k:k + 1] * pltpu.roll(xe, 3 - k, 0)[8:]
        y_ref[...] = acc

    return pl.pallas_call(
        body, name=name, grid=(s // tb,),
        in_specs=[_pblk(tb, width, col0),
                  pl.BlockSpec((8, width), lambda i: (jnp.maximum(i * (tb // 8) - 1, 0), cb)),
                  _full((4, width)), _full((1, width))],
        out_specs=pl.BlockSpec((tb, width), lambda i: (i, 0)),
        out_shape=jax.ShapeDtypeStruct((s, width), F32),
        compiler_params=_params("parallel"),
    )(src, src, w, bias)


def conv_bwd(src, col0, width, dpre, w, *, tb, name):
    s = src.shape[0]
    cb = col0 // width
    nt = s // tb

    def body(x_ref, p_ref, d_ref, n_ref, w_ref, dx_ref, dwb_ref):
        i = pl.program_id(0)
        d = d_ref[...]
        nxt = jnp.where(i < nt - 1, n_ref[...], 0.0)
        de = jnp.concatenate([d, nxt], axis=0)
        wv = w_ref[...]
        dx = wv[3:4] * d
        for k in range(3):
            dx = dx + wv[k:k + 1] * pltpu.roll(de, tb + 8 - (3 - k), 0)[:tb]
        dx_ref[...] = dx.astype(BF16)

        @pl.when(i == 0)
        def _():
            dwb_ref[...] = jnp.zeros_like(dwb_ref)

        prev = jnp.where(i > 0, p_ref[...], 0.0)
        xe = jnp.concatenate([prev, x_ref[...]], axis=0)
        dwb_ref[3:4, :] += jnp.sum(d * xe[8:], axis=0, keepdims=True)
        for k in range(3):
            dwb_ref[k:k + 1, :] += jnp.sum(d * pltpu.roll(xe, 3 - k, 0)[8:], axis=0, keepdims=True)
        dwb_ref[4:5, :] += jnp.sum(d, axis=0, keepdims=True)

    return pl.pallas_call(
        body, name=name, grid=(nt,),
        in_specs=[_pblk(tb, width, col0),
                  pl.BlockSpec((8, width), lambda i: (jnp.maximum(i * (tb // 8) - 1, 0), cb)),
                  pl.BlockSpec((tb, width), lambda i: (i, 0)),
                  pl.BlockSpec((8, width), lambda i: (jnp.minimum((i + 1) * (tb // 8), s // 8 - 1), 0)),
                  _full((4, width))],
        out_specs=[pl.BlockSpec((tb, width), lambda i: (i, 0)), _full((8, width))],
        out_shape=[jax.ShapeDtypeStruct((s, width), BF16), jax.ShapeDtypeStruct((8, width), F32)],
        compiler_params=_params("arbitrary"),
    )(src, src, dpre, dpre, w)


SGU_BLOCK = 128


def _sgu_pre(u, v, lg, lb):
    ug = _gelu(u)
    vg = _gelu(v)
    mu = jnp.mean(vg, axis=-1, keepdims=True)
    var = jnp.mean(jnp.square(vg - mu), axis=-1, keepdims=True)
    return ug, (vg - mu) * lax.rsqrt(var + EPS) * lg + lb


def _sgu_mask():
    shp = (SGU_BLOCK, SGU_BLOCK)
    return (_rows(shp) // CHUNK) >= (_lanes(shp) // CHUNK)


def sgu_fwd(proj, lg, lb, ws, bt, *, tb, name):
    s = proj.shape[0]
    w = BRANCH_W

    def body(u_ref, v_ref, lg_ref, lb_ref, w_ref, b_ref, y_ref):
        ug, vn = _sgu_pre(u_ref[...], v_ref[...], lg_ref[...], lb_ref[...])
        mask = _sgu_mask()
        for g in range(4):
            wm = jnp.where(mask, w_ref[g], 0.0)
            bcol = _col(b_ref[...], g)
            cs = slice(128 * g, 128 * g + 128)
            for n in range(tb // SGU_BLOCK):
                rs = slice(SGU_BLOCK * n, SGU_BLOCK * (n + 1))
                mixed = _dg(wm, vn[rs, cs], 1, 0) + bcol
                y_ref[rs, cs] = (ug[rs, cs] * mixed).astype(BF16)

    return pl.pallas_call(
        body, name=name, grid=(s // tb,),
        in_specs=[_pblk(tb, w, C_AU), _pblk(tb, w, C_AV), _full((1, w)), _full((1, w)),
                  _full((4, 128, 128)), _full((128, 128))],
        out_specs=pl.BlockSpec((tb, w), lambda i: (i, 0)),
        out_shape=jax.ShapeDtypeStruct((s, w), BF16),
        compiler_params=_params("parallel"),
    )(proj, proj, lg, lb, ws, bt)


def sgu_bwd(proj, dy, lg, lb, ws, bt, *, tb, name):
    s = proj.shape[0]
    w = BRANCH_W

    def body(u_ref, v_ref, dy_ref, lg_ref, lb_ref, w_ref, b_ref,
             du_ref, dv_ref, dlg_ref, dlb_ref, dw_ref, db_ref, dug_scr, dvn_scr):
        @pl.when(pl.program_id(0) == 0)
        def _():
            dlg_ref[...] = jnp.zeros_like(dlg_ref)
            dlb_ref[...] = jnp.zeros_like(dlb_ref)
            dw_ref[...] = jnp.zeros_like(dw_ref)
            db_ref[...] = jnp.zeros_like(db_ref)

        (ug, vn), vf = jax.vjp(_sgu_pre, u_ref[...], v_ref[...], lg_ref[...], lb_ref[...])
        dyv = dy_ref[...]
        mask = _sgu_mask()
        for g in range(4):
            wm = jnp.where(mask, w_ref[g], 0.0)
            bcol = _col(b_ref[...], g)
            cs = slice(128 * g, 128 * g + 128)
            dwg = jnp.zeros((SGU_BLOCK, SGU_BLOCK), F32)
            dbc = jnp.zeros((SGU_BLOCK, 1), F32)
            for n in range(tb // SGU_BLOCK):
                rs = slice(SGU_BLOCK * n, SGU_BLOCK * (n + 1))
                blk = vn[rs, cs]
                mixed = _dg(wm, blk, 1, 0) + bcol
                dmix = dyv[rs, cs] * ug[rs, cs]
                dug_scr[rs, cs] = dyv[rs, cs] * mixed
                dvn_scr[rs, cs] = _dg(wm, dmix, 0, 0)
                dwg = dwg + _dg(dmix, blk, 1, 1)
                dbc = dbc + jnp.sum(dmix, axis=1, keepdims=True)
            dw_ref[g] += jnp.where(mask, dwg, 0.0)
            db_ref[...] += _put_col(dbc, g, 128)
        du, dv, dlg, dlb = vf((dug_scr[...], dvn_scr[...]))
        du_ref[...] = du.astype(BF16)
        dv_ref[...] = dv.astype(BF16)
        dlg_ref[...] += dlg
        dlb_ref[...] += dlb

    row = pl.BlockSpec((tb, w), lambda i: (i, 0))
    return pl.pallas_call(
        body, name=name, grid=(s // tb,),
        in_specs=[_pblk(tb, w, C_AU), _pblk(tb, w, C_AV), row, _full((1, w)), _full((1, w)),
                  _full((4, 128, 128)), _full((128, 128))],
        out_specs=[row, row, _full((1, w)), _full((1, w)), _full((4, 128, 128)), _full((128, 128))],
        out_shape=[jax.ShapeDtypeStruct((s, w), BF16), jax.ShapeDtypeStruct((s, w), BF16),
                   jax.ShapeDtypeStruct((1, w), F32), jax.ShapeDtypeStruct((1, w), F32),
                   jax.ShapeDtypeStruct((4, 128, 128), F32), jax.ShapeDtypeStruct((128, 128), F32)],
        scratch_shapes=[pltpu.VMEM((tb, w), F32), pltpu.VMEM((tb, w), F32)],
        compiler_params=_params("arbitrary"),
    )(proj, proj, dy, lg, lb, ws, bt)


POOL_HALO = 16


def _pooled(xe, t0, tb):
    s2 = xe + pltpu.roll(xe, 1, 0)
    s4 = s2 + pltpu.roll(s2, 2, 0)
    s8 = s4 + pltpu.roll(s4, 4, 0)
    s16 = s8 + pltpu.roll(s8, 8, 0)
    t = (t0 + _rows((tb, 128))).astype(F32)
    outs = []
    for g, sw in enumerate((s2, s4, s8, s16)):
        cs = slice(128 * g, 128 * g + 128)
        cnt = jnp.minimum(t + 1.0, float(2 ** (g + 1)))
        outs.append(sw[POOL_HALO:, cs] / cnt - xe[POOL_HALO:, cs])
    return outs


def pool_fwd(proj, wp, scale, *, tb, name):
    s = proj.shape[0]
    w = BRANCH_W

    def body(x_ref, p_ref, w_ref, sc_ref, y_ref):
        i = pl.program_id(0)
        prev = jnp.where(i > 0, p_ref[...], 0.0)
        xe = jnp.concatenate([prev, x_ref[...]], axis=0)
        pooled = _pooled(xe, i * tb, tb)
        sc = sc_ref[...]
        for g in range(4):
            cs = slice(128 * g, 128 * g + 128)
            y_ref[:, cs] = (_dg(pooled[g], w_ref[g], 1, 0) * sc[:, cs]).astype(BF16)

    cb = C_DX // w
    return pl.pallas_call(
        body, name=name, grid=(s // tb,),
        in_specs=[_pblk(tb, w, C_DX),
                  pl.BlockSpec((POOL_HALO, w), lambda i: (jnp.maximum(i * (tb // POOL_HALO) - 1, 0), cb)),
                  _full((4, 128, 128)), _full((1, w))],
        out_specs=pl.BlockSpec((tb, w), lambda i: (i, 0)),
        out_shape=jax.ShapeDtypeStruct((s, w), BF16),
        compiler_params=_params("parallel"),
    )(proj, proj, wp, scale)


def pool_bwd(proj, dy, wp, scale, *, tb, name):
    s = proj.shape[0]
    w = BRANCH_W
    nt = s // tb
    te = tb + POOL_HALO

    def body(x_ref, p_ref, dy_ref, n_ref, w_ref, sc_ref, dx_ref, dw_ref, dsc_ref):
        i = pl.program_id(0)

        @pl.when(i == 0)
        def _():
            dw_ref[...] = jnp.zeros_like(dw_ref)
            dsc_ref[...] = jnp.zeros_like(dsc_ref)

        prev = jnp.where(i > 0, p_ref[...], 0.0)
        xe = jnp.concatenate([prev, x_ref[...]], axis=0)
        pooled = _pooled(xe, i * tb, tb)
        dyv = dy_ref[...]
        nxt = jnp.where(i < nt - 1, n_ref[...], 0.0)
        dye = jnp.concatenate([dyv, nxt], axis=0)
        sc = sc_ref[...]
        t = (i * tb + _rows((te, 128))).astype(F32)
        for g in range(4):
            cs = slice(128 * g, 128 * g + 128)
            win = 2 ** (g + 1)
            dpm = dye[:, cs] * sc[:, cs]
            dpool = _dg(dpm, w_ref[g], 1, 1)
            q = dpool / jnp.minimum(t + 1.0, float(win))
            r, sh = q, 1
            while sh < win:
                r = r + pltpu.roll(r, te - sh, 0)
                sh *= 2
            dx_ref[:, cs] = (r[:tb] - dpool[:tb]).astype(BF16)
            dw_ref[g] += _dg(pooled[g], dpm[:tb], 0, 0)
            dsc_ref[:, cs] += jnp.sum(dyv[:, cs] * _dg(pooled[g], w_ref[g], 1, 0), axis=0, keepdims=True)

    cb = C_DX // w
    row = pl.BlockSpec((tb, w), lambda i: (i, 0))
    return pl.pallas_call(
        body, name=name, grid=(nt,),
        in_specs=[_pblk(tb, w, C_DX),
                  pl.BlockSpec((POOL_HALO, w), lambda i: (jnp.maximum(i * (tb // POOL_HALO) - 1, 0), cb)),
                  row,
                  pl.BlockSpec((POOL_HALO, w),
                               lambda i: (jnp.minimum((i + 1) * (tb // POOL_HALO), s // POOL_HALO - 1), 0)),
                  _full((4, 128, 128)), _full((1, w))],
        out_specs=[row, _full((4, 128, 128)), _full((1, w))],
        out_shape=[jax.ShapeDtypeStruct((s, w), BF16), jax.ShapeDtypeStruct((4, 128, 128), F32),
                   jax.ShapeDtypeStruct((1, w), F32)],
        compiler_params=_params("arbitrary"),
    )(proj, proj, dy, dy, wp, scale)


def _neg_expm1(z):
    p = 1.0 + z * (1.0 / 9.0)
    for n in (8.0, 7.0, 6.0, 5.0, 4.0, 3.0, 2.0):
        p = 1.0 + z * (1.0 / n) * p
    return jnp.where(z > -0.5, -z * p, 1.0 - jnp.exp(z))


def _lru_gates(xc, wa, wx, ba, bx, lam):
    r = jax.nn.sigmoid(mm_nn(xc, wa) + ba)
    i = jax.nn.sigmoid(mm_nn(xc, wx) + bx)
    log_a = -LRU_C * r * _softplus(-lam)
    a = jnp.exp(log_a)
    return a, jnp.sqrt(_neg_expm1(2.0 * log_a)) * (i * xc)


def _scan_down(a, b):
    n = a.shape[0]
    row = _rows(a.shape)
    sh = 1
    while sh < n:
        keep = row >= sh
        a_sh = jnp.where(keep, pltpu.roll(a, sh, 0), 1.0)
        b_sh = jnp.where(keep, pltpu.roll(b, sh, 0), 0.0)
        b = a * b_sh + b
        a = a * a_sh
        sh *= 2
    return a, b


def _scan_up(a, b):
    n = a.shape[0]
    row = _rows(a.shape)
    sh = 1
    while sh < n:
        keep = row < n - sh
        a_sh = jnp.where(keep, pltpu.roll(a, n - sh, 0), 1.0)
        b_sh = jnp.where(keep, pltpu.roll(b, n - sh, 0), 0.0)
        b = a * b_sh + b
        a = a * a_sh
        sh *= 2
    return a, b


def lru_fwd(xc, proj, wa, wx, ba, bx, lam, *, tb, name):
    s = xc.shape[0]
    w = BRANCH_W

    def body(xc_ref, gate_ref, wa_ref, wx_ref, ba_ref, bx_ref, lam_ref, y_ref, h_ref, carry):
        @pl.when(pl.program_id(0) == 0)
        def _():
            carry[...] = jnp.zeros_like(carry)

        a, b = _lru_gates(xc_ref[...], wa_ref[...], wx_ref[...], ba_ref[...], bx_ref[...], lam_ref[...])
        pa, hb = _scan_down(a, b)
        h = pa * carry[...] + hb
        h_ref[...] = h
        carry[...] = h_ref[tb - 1:tb, :]
        y_ref[...] = (h * _gelu(gate_ref[...])).astype(BF16)

    row = pl.BlockSpec((tb, w), lambda i: (i, 0))
    return pl.pallas_call(
        body, name=name, grid=(s // tb,),
        in_specs=[row, _pblk(tb, w, C_BG), _full((w, w)), _full((w, w)),
                  _full((1, w)), _full((1, w)), _full((1, w))],
        out_specs=[row, row],
        out_shape=[jax.ShapeDtypeStruct((s, w), BF16), jax.ShapeDtypeStruct((s, w), F32)],
        scratch_shapes=[pltpu.VMEM((1, w), F32)],
        compiler_params=_params("arbitrary"),
    )(xc, proj, wa, wx, ba, bx, lam)


def lru_bwd(xc, proj, h, dy, wa, wx, ba, bx, lam, *, tb, name):
    s = xc.shape[0]
    w = BRANCH_W
    nt = s // tb

    def body(xc_ref, gate_ref, h_ref, hp_ref, dy_ref, wa_ref, wx_ref, ba_ref, bx_ref, lam_ref,
             dxc_ref, dgate_ref, dwa_ref, dwx_ref, dba_ref, dbx_ref, dlam_ref, carry):
        i = pl.program_id(0)

        @pl.when(i == 0)
        def _():
            carry[...] = jnp.zeros_like(carry)
            for r in (dwa_ref, dwx_ref, dba_ref, dbx_ref, dlam_ref):
                r[...] = jnp.zeros_like(r)

        (a, b), vf = jax.vjp(_lru_gates, xc_ref[...], wa_ref[...], wx_ref[...],
                             ba_ref[...], bx_ref[...], lam_ref[...])
        gate = gate_ref[...]
        dyv = dy_ref[...]
        hv = h_ref[...]
        row = _rows((tb, w))
        a_next = jnp.where(row < tb - 1, pltpu.roll(a, tb - 1, 0), 1.0)
        pa, xb = _scan_up(a_next, dyv * _gelu(gate))
        dh = xb + pa * carry[...]
        carry[...] = a[0:1, :] * dh[0:1, :]
        h_first = jnp.where(i < nt - 1, hp_ref[7:8, :], 0.0)
        h_prev = jnp.where(row >= 1, pltpu.roll(hv, 1, 0), h_first)
        dxc, dwa, dwx, dba, dbx, dlam = vf((dh * h_prev, dh))
        dxc_ref[...] = dxc
        dgate_ref[...] = (dyv * hv * _gelu_grad(gate)).astype(BF16)
        dwa_ref[...] += dwa
        dwx_ref[...] += dwx
        dba_ref[...] += dba
        dbx_ref[...] += dbx
        dlam_ref[...] += dlam

    row_spec = pl.BlockSpec((tb, w), lambda i: (nt - 1 - i, 0))
    hp_spec = pl.BlockSpec((8, w), lambda i: (jnp.maximum((nt - 1 - i) * (tb // 8) - 1, 0), 0))
    return pl.pallas_call(
        body, name=name, grid=(nt,),
        in_specs=[row_spec, _pblk(tb, w, C_BG, rev_n=nt), row_spec, hp_spec, row_spec,
                  _full((w, w)), _full((w, w)), _full((1, w)), _full((1, w)), _full((1, w))],
        out_specs=[row_spec, row_spec, _full((w, w)), _full((w, w)),
                   _full((1, w)), _full((1, w)), _full((1, w))],
        out_shape=[jax.ShapeDtypeStruct((s, w), F32), jax.ShapeDtypeStruct((s, w), BF16),
                   jax.ShapeDtypeStruct((w, w), F32), jax.ShapeDtypeStruct((w, w), F32),
                   jax.ShapeDtypeStruct((1, w), F32), jax.ShapeDtypeStruct((1, w), F32),
                   jax.ShapeDtypeStruct((1, w), F32)],
        scratch_shapes=[pltpu.VMEM((1, w), F32)],
        compiler_params=_params("arbitrary"),
    )(xc, proj, h, h, dy, wa, wx, ba, bx, lam)


def _gdn_g(ba, alog, dt):
    return -jnp.exp(alog) * _softplus(ba + dt)


def _gdn_head(s_in, qp, kp, vp, zz, bpre, gc, ng):
    c = CHUNK
    ii, jj = _rows((c, c)), _lanes((c, c))
    q = _silu(qp)
    k = _silu(kp)
    v = _silu(vp)
    q = q * lax.rsqrt(jnp.sum(q * q, axis=-1, keepdims=True) + EPS) * (GDN_DK ** -0.5)
    k = k * lax.rsqrt(jnp.sum(k * k, axis=-1, keepdims=True) + EPS)
    beta = jax.nn.sigmoid(bpre)
    eye = (ii == jj).astype(F32)
    gr = jnp.sum(gc * eye, axis=0, keepdims=True)
    decay = jnp.exp(jnp.where(ii >= jj, gc - gr, -jnp.inf))
    kb = k * beta
    t = _tri_inv(jnp.where(ii > jj, mm_nt(kb, k) * decay, 0.0))
    u = mm_nn(t, v * beta)
    wk = mm_nn(t, kb * jnp.exp(gc))
    attn = mm_nt(q, k) * decay
    v_new = u - mm_nn(wk, s_in)
    o = mm_nn(q * jnp.exp(gc), s_in) + mm_nn(attn, v_new)
    g_last = jnp.sum(jnp.where(_rows((c, 1)) == c - 1, gc, 0.0), axis=0, keepdims=True)
    s_out = s_in * jnp.exp(g_last) + mm_tn(k * jnp.exp(g_last - gc), v_new)
    o = o * lax.rsqrt(jnp.mean(o * o, axis=-1, keepdims=True) + EPS) * ng
    return o * _silu(zz), s_out


def _chunk_cumsum(g, transpose=False):
    c = CHUNK
    tri = (_rows((c, c)) >= _lanes((c, c))).astype(F32)
    return _dg3(tri, g, 0 if transpose else 1, 0)


def gdn_fwd(qkv, proj, alog, dt, ng, *, name):
    s = qkv.shape[0]
    c = CHUNK
    nc = s // c
    hd = GDN_DK

    def body(qkv_ref, z_ref, ba_ref, al_ref, dt_ref, ng_ref, y_ref, sall_ref, s_scr):
        @pl.when(pl.program_id(0) == 0)
        def _():
            s_scr[...] = jnp.zeros_like(s_scr)

        ba = ba_ref[...]
        gcum = _chunk_cumsum(_gdn_g(ba, al_ref[...], dt_ref[...]))
        for h in range(GDN_HEADS):
            cs = slice(hd * h, hd * (h + 1))
            s_in = s_scr[h]
            sall_ref[h] = s_in
            y, s_out = _gdn_head(s_in, qkv_ref[:, hd * h:hd * (h + 1)],
                                 qkv_ref[:, 512 + hd * h:512 + hd * (h + 1)],
                                 qkv_ref[:, 1024 + hd * h:1024 + hd * (h + 1)],
                                 z_ref[:, cs], _col(ba, h), _col(gcum, 4 + h), ng_ref[...])
            y_ref[:, cs] = y.astype(BF16)
            s_scr[h] = s_out

    return pl.pallas_call(
        body, name=name, grid=(nc,),
        in_specs=[pl.BlockSpec((c, 1536), lambda i: (i, 0)), _pblk(c, 512, C_Z), _pblk(c, 128, C_BA),
                  _full((1, 128)), _full((1, 128)), _full((1, 128))],
        out_specs=[pl.BlockSpec((c, 512), lambda i: (i, 0)),
                   pl.BlockSpec((None, GDN_HEADS, hd, hd), lambda i: (i, 0, 0, 0))],
        out_shape=[jax.ShapeDtypeStruct((s, 512), BF16), jax.ShapeDtypeStruct((nc, GDN_HEADS, hd, hd), F32)],
        scratch_shapes=[pltpu.VMEM((GDN_HEADS, hd, hd), F32)],
        compiler_params=_params("arbitrary"),
    )(qkv, proj, proj, alog, dt, ng)


def gdn_bwd(qkv, proj, sall, dy, alog, dt, ng, *, name):
    s = qkv.shape[0]
    c = CHUNK
    nc = s // c
    hd = GDN_DK

    def body(qkv_ref, z_ref, ba_ref, sall_ref, dy_ref, al_ref, dt_ref, ng_ref,
             dqkv_ref, dz_ref, dba_ref, dal_ref, ddt_ref, dng_ref, ds_scr):
        @pl.when(pl.program_id(0) == 0)
        def _():
            ds_scr[...] = jnp.zeros_like(ds_scr)
            dal_ref[...] = jnp.zeros_like(dal_ref)
            ddt_ref[...] = jnp.zeros_like(ddt_ref)
            dng_ref[...] = jnp.zeros_like(dng_ref)

        ba = ba_ref[...]
        g, gvf = jax.vjp(_gdn_g, ba, al_ref[...], dt_ref[...])
        gcum = _chunk_cumsum(g)
        dgcum = jnp.zeros((c, 128), F32)
        dba = jnp.zeros((c, 128), F32)
        dng = jnp.zeros((1, 128), F32)
        for h in range(GDN_HEADS):
            cs = slice(hd * h, hd * (h + 1))
            _, vf = jax.vjp(_gdn_head, sall_ref[h], qkv_ref[:, hd * h:hd * (h + 1)],
                            qkv_ref[:, 512 + hd * h:512 + hd * (h + 1)],
                            qkv_ref[:, 1024 + hd * h:1024 + hd * (h + 1)],
                            z_ref[:, cs], _col(ba, h), _col(gcum, 4 + h), ng_ref[...])
            ds_in, dq, dk, dv, dz, dbp, dgc, dn = vf((dy_ref[:, cs], ds_scr[h]))
            ds_scr[h] = ds_in
            dqkv_ref[:, hd * h:hd * (h + 1)] = dq
            dqkv_ref[:, 512 + hd * h:512 + hd * (h + 1)] = dk
            dqkv_ref[:, 1024 + hd * h:1024 + hd * (h + 1)] = dv
            dz_ref[:, cs] = dz.astype(BF16)
            dgcum = dgcum + _put_col(dgc, 4 + h, 128)
            dba = dba + _put_col(dbp, h, 128)
            dng = dng + dn
        dba2, dal, ddt = gvf(_chunk_cumsum(dgcum, transpose=True))
        dba_ref[...] = (dba + dba2).astype(BF16)
        dal_ref[...] += dal
        ddt_ref[...] += ddt
        dng_ref[...] += dng

    def rev(width):
        return pl.BlockSpec((c, width), lambda i: (nc - 1 - i, 0))

    return pl.pallas_call(
        body, name=name, grid=(nc,),
        in_specs=[rev(1536), _pblk(c, 512, C_Z, rev_n=nc), _pblk(c, 128, C_BA, rev_n=nc),
                  pl.BlockSpec((None, GDN_HEADS, hd, hd), lambda i: (nc - 1 - i, 0, 0, 0)), rev(512),
                  _full((1, 128)), _full((1, 128)), _full((1, 128))],
        out_specs=[rev(1536), rev(512), rev(128), _full((1, 128)), _full((1, 128)), _full((1, 128))],
        out_shape=[jax.ShapeDtypeStruct((s, 1536), F32), jax.ShapeDtypeStruct((s, 512), BF16),
                   jax.ShapeDtypeStruct((s, 128), BF16), jax.ShapeDtypeStruct((1, 128), F32),
                   jax.ShapeDtypeStruct((1, 128), F32), jax.ShapeDtypeStruct((1, 128), F32)],
        scratch_shapes=[pltpu.VMEM((GDN_HEADS, hd, hd), F32)],
        compiler_params=_params("arbitrary"),
    )(qkv, proj, proj, sall, dy, alog, dt, ng)


def merge_fwd(x, ys, proj, wb, wo, *, tb, name):
    s, d = x.shape

    def body(x_ref, ya, yb, yc, yd, g0, g1, g2, g3, wb_ref, wo_ref, o_ref, m_ref):
        m = jnp.zeros((tb, d), F32)
        for g, (y_ref, gp_ref) in enumerate(zip((ya, yb, yc, yd), (g0, g1, g2, g3))):
            m = m + jax.nn.sigmoid(gp_ref[...]) * _dg(y_ref[...], wb_ref[g], 1, 0)
        mb = m.astype(BF16)
        m_ref[...] = mb
        o_ref[...] = x_ref[...] + _dg(mb, wo_ref[...], 1, 0)

    row = pl.BlockSpec((tb, d), lambda i: (i, 0))
    yrow = pl.BlockSpec((tb, BRANCH_W), lambda i: (i, 0))
    return pl.pallas_call(
        body, name=name, grid=(s // tb,),
        in_specs=[row] + [yrow] * 4 + [_pblk(tb, d, C_GATE + d * g) for g in range(4)]
        + [_full((4, BRANCH_W, d)), _full((d, d))],
        out_specs=[row, row],
        out_shape=[jax.ShapeDtypeStruct((s, d), F32), jax.ShapeDtypeStruct((s, d), BF16)],
        compiler_params=_params("parallel"),
    )(x, *ys, proj, proj, proj, proj, wb, wo)


def merge_bwd(dx, ys, proj, wb, wo, *, tb, name):
    s, d = dx.shape

    def body(dx_ref, ya, yb, yc, yd, g0, g1, g2, g3, wb_ref, wo_ref,
             dgp_ref, dbr_ref, da, db, dc, dd, dxb_ref):
        dxb = dx_ref[...].astype(BF16)
        dxb_ref[...] = dxb
        dm = _dg(dxb, wo_ref[...], 1, 1)
        for g, (y_ref, gp_ref, dy_ref) in enumerate(zip((ya, yb, yc, yd), (g0, g1, g2, g3), (da, db, dc, dd))):
            br = _dg(y_ref[...], wb_ref[g], 1, 0)
            sg = jax.nn.sigmoid(gp_ref[...])
            dgp_ref[:, d * g:d * (g + 1)] = (dm * br * sg * (1.0 - sg)).astype(BF16)
            dbr = (dm * sg).astype(BF16)
            dbr_ref[g] = dbr
            dy_ref[...] = _dg(dbr, wb_ref[g], 1, 1)

    row = pl.BlockSpec((tb, d), lambda i: (i, 0))
    yrow = pl.BlockSpec((tb, BRANCH_W), lambda i: (i, 0))
    return pl.pallas_call(
        body, name=name, grid=(s // tb,),
        in_specs=[row] + [yrow] * 4 + [_pblk(tb, d, C_GATE + d * g) for g in range(4)]
        + [_full((4, BRANCH_W, d)), _full((d, d))],
        out_specs=[pl.BlockSpec((tb, 4 * d), lambda i: (i, 0)), pl.BlockSpec((4, tb, d), lambda i: (0, i, 0)),
                   yrow, yrow, yrow, yrow, row],
        out_shape=[jax.ShapeDtypeStruct((s, 4 * d), BF16), jax.ShapeDtypeStruct((4, s, d), BF16)]
        + [jax.ShapeDtypeStruct((s, BRANCH_W), F32)] * 4 + [jax.ShapeDtypeStruct((s, d), BF16)],
        compiler_params=_params("parallel"),
    )(dx, *ys, proj, proj, proj, proj, wb, wo)


def loss_head(x, g, target, *, tb, name):
    s, d = x.shape

    def body(x_ref, g_ref, t_ref, dx_ref, dg_ref, loss_ref):
        @pl.when(pl.program_id(0) == 0)
        def _():
            dg_ref[...] = jnp.zeros_like(dg_ref)
            loss_ref[...] = jnp.zeros_like(loss_ref)

        xv, gv = x_ref[...], g_ref[...]
        r = lax.rsqrt(jnp.mean(xv * xv, axis=-1, keepdims=True) + EPS)
        err = xv * r * gv - t_ref[...]
        loss_ref[...] += 0.5 * jnp.sum(jnp.mean(err * err, axis=-1, keepdims=True), axis=0, keepdims=True)
        dx, dg = _rms_bwd(xv, gv, err * (1.0 / d))
        dx_ref[...] = dx
        dg_ref[...] += dg

    row = pl.BlockSpec((tb, d), lambda i: (i, 0))
    return pl.pallas_call(
        body, name=name, grid=(s // tb,),
        in_specs=[row, _full((1, d)), row],
        out_specs=[row, _full((1, d)), _full((1, 128))],
        out_shape=[jax.ShapeDtypeStruct((s, d), F32), jax.ShapeDtypeStruct((1, d), F32),
                   jax.ShapeDtypeStruct((1, 128), F32)],
        compiler_params=_params("arbitrary"),
    )(x, g, target)


def adamw(w, g, m, v, *, name):
    shape = w.shape
    c = shape[-1]
    r = math.prod(shape[:-1])
    tr = r
    while tr * c * 4 > (1 << 20) and tr % 16 == 0:
        tr //= 2
    c1 = 1.0 / (1.0 - ADAM_B1 ** ADAM_STEP)
    c2 = 1.0 / (1.0 - ADAM_B2 ** ADAM_STEP)

    def body(w_ref, g_ref, m_ref, v_ref, d_ref, nm_ref, nv_ref):
        gv = g_ref[...]
        nm = ADAM_B1 * m_ref[...] + (1.0 - ADAM_B1) * gv
        nv = ADAM_B2 * v_ref[...] + (1.0 - ADAM_B2) * (gv * gv)
        nm_ref[...] = nm
        nv_ref[...] = nv
        d_ref[...] = -ADAM_LR * ((nm * c1) / (jnp.sqrt(nv * c2) + ADAM_EPS) + ADAM_WD * w_ref[...])

    blk = pl.BlockSpec((tr, c), lambda i: (i, 0))
    outs = pl.pallas_call(
        body, name=name, grid=(r // tr,),
        in_specs=[blk] * 4, out_specs=[blk] * 3,
        out_shape=[jax.ShapeDtypeStruct((r, c), F32)] * 3,
        compiler_params=_params("parallel"),
    )(*(t.reshape(r, c) for t in (w, g, m, v)))
    return tuple(o.reshape(shape) for o in outs)


_ANY = pl.BlockSpec(memory_space=pl.ANY)


def _place():
    x, y, c = lax.axis_index("x"), lax.axis_index("y"), lax.axis_index("c")
    return x, y, c, [(1 - x, y), (x, 1 - y), (1 - x, 1 - y)]


def gather_shards(arrs, *, name):
    n = len(arrs)

    def body(*refs):
        ins, outs = refs[:n], refs[n:2 * n]
        send_sems, recv_sems, loc_sems = refs[2 * n:]
        x, y, c, chips = _place()
        me = 2 * x + y
        copies = []
        for a in range(n):
            loc = pltpu.make_async_copy(ins[a], outs[a].at[me], loc_sems.at[a])
            loc.start()
            copies.append(loc)
            for j, (px, py) in enumerate(chips):
                cp = pltpu.make_async_remote_copy(
                    src_ref=ins[a], dst_ref=outs[a].at[me], send_sem=send_sems.at[a, j],
                    recv_sem=recv_sems.at[a, j], device_id=(px, py, c), device_id_type=MESH)
                cp.start()
                copies.append(cp)
        for cp in copies:
            cp.wait()

    return pl.pallas_call(
        body, name=name,
        in_specs=[_ANY] * n, out_specs=[_ANY] * n,
        out_shape=[jax.ShapeDtypeStruct((N_SHARD,) + t.shape, t.dtype) for t in arrs],
        scratch_shapes=[pltpu.SemaphoreType.DMA((n, 3)), pltpu.SemaphoreType.DMA((n, 3)),
                        pltpu.SemaphoreType.DMA((n,))],
    )(*arrs)


def pair_send_halves(arrs, *, name):
    n = len(arrs)

    def body(*refs):
        ins, outs = refs[:n], refs[n:2 * n]
        send_sems, recv_sems = refs[2 * n:]
        x, y, c, _ = _place()
        copies = []
        for a in range(n):
            for j in range(N_SHARD):
                cp = pltpu.make_async_remote_copy(
                    src_ref=ins[a].at[j, 1 - c], dst_ref=outs[a].at[j], send_sem=send_sems.at[a, j],
                    recv_sem=recv_sems.at[a, j], device_id=(x, y, 1 - c), device_id_type=MESH)
                cp.start()
                copies.append(cp)
        for cp in copies:
            cp.wait()

    return pl.pallas_call(
        body, name=name,
        in_specs=[_ANY] * n, out_specs=[_ANY] * n,
        out_shape=[jax.ShapeDtypeStruct((N_SHARD,) + t.shape[2:], t.dtype) for t in arrs],
        scratch_shapes=[pltpu.SemaphoreType.DMA((n, N_SHARD)), pltpu.SemaphoreType.DMA((n, N_SHARD))],
    )(*arrs)


def chip_scatter(arrs, *, name):
    n = len(arrs)

    def body(*refs):
        ins, outs = refs[:n], refs[n:2 * n]
        send_sems, recv_sems, loc_sems = refs[2 * n:]
        x, y, c, chips = _place()
        me = 2 * x + y
        copies = []
        for a in range(n):
            loc = pltpu.make_async_copy(ins[a].at[me], outs[a].at[me], loc_sems.at[a])
            loc.start()
            copies.append(loc)
            for j, (px, py) in enumerate(chips):
                cp = pltpu.make_async_remote_copy(
                    src_ref=ins[a].at[2 * px + py], dst_ref=outs[a].at[me], send_sem=send_sems.at[a, j],
                    recv_sem=recv_sems.at[a, j], device_id=(px, py, c), device_id_type=MESH)
                cp.start()
                copies.append(cp)
        for cp in copies:
            cp.wait()

    return pl.pallas_call(
        body, name=name,
        in_specs=[_ANY] * n, out_specs=[_ANY] * n,
        out_shape=[jax.ShapeDtypeStruct(t.shape, t.dtype) for t in arrs],
        scratch_shapes=[pltpu.SemaphoreType.DMA((n, 3)), pltpu.SemaphoreType.DMA((n, 3)),
                        pltpu.SemaphoreType.DMA((n,))],
    )(*arrs)


def pair_gather(arrs, *, name):
    n = len(arrs)

    def body(*refs):
        ins, outs = refs[:n], refs[n:2 * n]
        send_sems, recv_sems, loc_sems = refs[2 * n:]
        x, y, c, _ = _place()
        copies = []
        for a in range(n):
            loc = pltpu.make_async_copy(ins[a], outs[a].at[c], loc_sems.at[a])
            loc.start()
            copies.append(loc)
            cp = pltpu.make_async_remote_copy(
                src_ref=ins[a], dst_ref=outs[a].at[c], send_sem=send_sems.at[a],
                recv_sem=recv_sems.at[a], device_id=(x, y, 1 - c), device_id_type=MESH)
            cp.start()
            copies.append(cp)
        for cp in copies:
            cp.wait()

    return pl.pallas_call(
        body, name=name,
        in_specs=[_ANY] * n, out_specs=[_ANY] * n,
        out_shape=[jax.ShapeDtypeStruct((2,) + t.shape, t.dtype) for t in arrs],
        scratch_shapes=[pltpu.SemaphoreType.DMA((n,)), pltpu.SemaphoreType.DMA((n,)),
                        pltpu.SemaphoreType.DMA((n,))],
    )(*arrs)


def all_sum_small(v, *, name):
    m = v.shape[0]

    def body(v_ref, o_ref, buf, send_sems, recv_sems, local_sem):
        x, y, c, chips = _place()
        me, sibling = (x, y, c), (x, y, 1 - c)

        def rows(px, py, pc):
            return buf.at[pl.ds((4 * px + 2 * py + pc) * m, m), :]

        def copy(k, block, to, src=None):
            return pltpu.make_async_remote_copy(
                src_ref=rows(*block) if src is None else src, dst_ref=rows(*block),
                send_sem=send_sems.at[k], recv_sem=recv_sems.at[k], device_id=to, device_id_type=MESH)

        mine = pltpu.make_async_copy(v_ref, rows(*me), local_sem)
        mine.start()
        first = [copy(0, me, sibling, src=v_ref)]
        first += [copy(1 + j, me, (*chip, c), src=v_ref) for j, chip in enumerate(chips)]
        for cp in first:
            cp.start()
        passed = [copy(4 + j, (*chip, c), sibling) for j, chip in enumerate(chips)]
        for j, chip in enumerate(chips):
            copy(1 + j, (*chip, c), me).wait_recv()
            passed[j].start()
        copy(0, sibling, me).wait_recv()
        for j, chip in enumerate(chips):
            copy(4 + j, (*chip, 1 - c), me).wait_recv()
        for cp in first + passed:
            cp.wait_send()
        mine.wait()
        acc = buf[pl.ds(0, m), :]
        for k in range(1, 8):
            acc = acc + buf[pl.ds(k * m, m), :]
        o_ref[...] = acc

    vm = pl.BlockSpec(memory_space=pltpu.VMEM)
    return pl.pallas_call(
        body, name=name, in_specs=[vm], out_specs=vm,
        out_shape=jax.ShapeDtypeStruct((m, 128), F32),
        scratch_shapes=[pltpu.VMEM((8 * m, 128), F32), pltpu.SemaphoreType.DMA((7,)),
                        pltpu.SemaphoreType.DMA((7,)), pltpu.SemaphoreType.DMA],
    )(v)


def sum_halves(arr, recv, c_idx, *, name):
    _, _, r, c = arr.shape
    tr = r
    while tr * c * 4 > (1 << 20) and tr % 32 == 0:
        tr //= 2

    def body(c_ref, a_ref, b_ref, o_ref):
        o_ref[...] = (a_ref[...] + b_ref[...]).astype(BF16)

    return pl.pallas_call(
        body, name=name,
        grid_spec=pltpu.PrefetchScalarGridSpec(
            num_scalar_prefetch=1, grid=(N_SHARD, r // tr),
            in_specs=[pl.BlockSpec((None, None, tr, c), lambda j, i, cr: (j, cr[0], i, 0)),
                      pl.BlockSpec((None, tr, c), lambda j, i, cr: (j, i, 0))],
            out_specs=pl.BlockSpec((None, tr, c), lambda j, i, cr: (j, i, 0))),
        out_shape=jax.ShapeDtypeStruct((N_SHARD, r, c), BF16),
        compiler_params=_params("parallel", "parallel"),
    )(c_idx, arr, recv)


def sum_chips(recv, *, name):
    _, r, c = recv.shape
    tr = r
    while tr * c * 4 > (1 << 20) and tr % 32 == 0:
        tr //= 2

    def body(a_ref, o_ref):
        acc = a_ref[0].astype(F32)
        for k in range(1, N_SHARD):
            acc = acc + a_ref[k].astype(F32)
        o_ref[...] = acc

    return pl.pallas_call(
        body, name=name, grid=(r // tr,),
        in_specs=[pl.BlockSpec((N_SHARD, tr, c), lambda i: (0, i, 0))],
        out_specs=pl.BlockSpec((tr, c), lambda i: (i, 0)),
        out_shape=jax.ShapeDtypeStruct((r, c), F32),
        compiler_params=_params("parallel"),
    )(recv)


_WEIGHTS = ("ff1_norm", "ff1_wg", "ff1_wu", "ff1_wd", "mix_norm", "w_in", "sgu_ln_g", "sgu_ln_b", "sgu_w",
            "sgu_b", "lru_conv_w", "lru_conv_b", "lru_wa", "lru_ba", "lru_wx", "lru_bx", "lru_lambda",
            "gdn_conv_w", "gdn_a_log", "gdn_dt_bias", "gdn_norm_g", "pool_w", "pool_scale", "w_branch",
            "w_out", "ff2_norm", "ff2_wg", "ff2_wu", "ff2_wd", "final_norm")
_BIG = ("ff1_wg", "ff1_wu", "ff1_wd", "w_in", "w_branch", "w_out", "ff2_wg", "ff2_wu", "ff2_wd")
_SMALL = tuple(n for n in _WEIGHTS if n not in _BIG)
_CONV = ("lru_conv_w", "gdn_conv_w")


def _pack(arrs):
    flat = jnp.concatenate([t.reshape(-1) for t in arrs])
    n = flat.shape[0]
    m = -(-n // 1024) * 1024
    return jnp.pad(flat, (0, m - n)).reshape(m // 128, 128)


def _unpack(buf, shapes):
    flat = buf.reshape(-1)
    out, o = [], 0
    for shp in shapes:
        k = math.prod(shp)
        out.append(flat[o:o + k].reshape(shp))
        o += k
    return out


def _block_diag(w):
    h, n, _ = w.shape
    out = jnp.zeros((h * n, h * n), w.dtype)
    for i in range(h):
        out = lax.dynamic_update_slice(out, w[i], (i * n, i * n))
    return out


def _diag_blocks(m, h=8, n=64):
    return jnp.stack([m[i * n:(i + 1) * n, i * n:(i + 1) * n] for i in range(h)])


def _lane_row(v):
    return jnp.zeros((1, 128), F32).at[0, 4:8].set(v)


def _w_in_to_padded(raw):
    sem = jnp.concatenate([raw[k, :, :P_IN_SHARD] for k in range(N_SHARD)], axis=1)
    zero = jnp.zeros((sem.shape[0], C_GATE - C_BA - 8), sem.dtype)
    return jnp.concatenate([sem[:, 2048:4096], sem[:, 0:2048], sem[:, 4104:4616], sem[:, 4096:4104], zero,
                            sem[:, 4616:P_IN]], axis=1)


def _w_in_grad_to_shards(dwp):
    sem = jnp.concatenate([dwp[:, C_AU:C_DX], dwp[:, 0:C_AU], dwp[:, C_BA:C_BA + 8], dwp[:, C_DX:C_BA],
                           dwp[:, C_GATE:]], axis=1)
    d = sem.shape[0]
    sh = sem.reshape(d, N_SHARD, P_IN_SHARD).transpose(1, 0, 2)
    sh = jnp.pad(sh, ((0, 0), (0, 0), (0, P_IN_SHARD_PAD - P_IN_SHARD)))
    return sh.reshape(N_SHARD, 2, d // 2, P_IN_SHARD_PAD)


def kernel(x, ff1_norm, ff1_wg, ff1_wu, ff1_wd, mix_norm, w_in, sgu_ln_g, sgu_ln_b, sgu_w, sgu_b, lru_conv_w,
           lru_conv_b, lru_wa, lru_ba, lru_wx, lru_bx, lru_lambda, gdn_conv_w, gdn_a_log, gdn_dt_bias, gdn_norm_g,
           pool_w, pool_scale, w_branch, w_out, ff2_norm, ff2_wg, ff2_wu, ff2_wd, final_norm, loss_target, m_ff1_norm,
           m_ff1_wg, m_ff1_wu, m_ff1_wd, m_mix_norm, m_w_in, m_sgu_ln_g, m_sgu_ln_b, m_sgu_w, m_sgu_b, m_lru_conv_w,
           m_lru_conv_b, m_lru_wa, m_lru_ba, m_lru_wx, m_lru_bx, m_lru_lambda, m_gdn_conv_w, m_gdn_a_log, m_gdn_dt_bias,
           m_gdn_norm_g, m_pool_w, m_pool_scale, m_w_branch, m_w_out, m_ff2_norm, m_ff2_wg, m_ff2_wu, m_ff2_wd,
           m_final_norm, v_ff1_norm, v_ff1_wg, v_ff1_wu, v_ff1_wd, v_mix_norm, v_w_in, v_sgu_ln_g, v_sgu_ln_b, v_sgu_w,
           v_sgu_b, v_lru_conv_w, v_lru_conv_b, v_lru_wa, v_lru_ba, v_lru_wx, v_lru_bx, v_lru_lambda, v_gdn_conv_w,
           v_gdn_a_log, v_gdn_dt_bias, v_gdn_norm_g, v_pool_w, v_pool_scale, v_w_branch, v_w_out, v_ff2_norm, v_ff2_wg,
           v_ff2_wu, v_ff2_wd, v_final_norm):
    a = dict(locals())
    n_layer = ff1_norm.shape[0]
    d = D_MODEL
    x0 = x[0]
    chip = 2 * lax.axis_index("x") + lax.axis_index("y")
    c_idx = jnp.reshape(lax.axis_index("c"), (1,)).astype(jnp.int32)

    def bf(t):
        return t.astype(BF16)

    w_in_loc = jnp.pad(bf(w_in), ((0, 0), (0, 0), (0, P_IN_SHARD_PAD - P_IN_SHARD)))
    (g_ff1_wg, g_ff1_wu, g_ff1_wd, g_w_in, g_wb, g_wo, g_ff2_wg, g_ff2_wu, g_ff2_wd, g_lcw, g_gcw) = gather_shards(
        [bf(ff1_wg), bf(ff1_wu), bf(ff1_wd), w_in_loc, bf(w_branch), bf(w_out), bf(ff2_wg), bf(ff2_wu),
         bf(ff2_wd), lru_conv_w, gdn_conv_w], name="gather_weights")

    def layer_params(l):
        p = {}
        p["w_in"] = _w_in_to_padded(g_w_in[:, l])
        p["wb"] = g_wb[:, l].transpose(1, 2, 0, 3).reshape(4, BRANCH_W, d)
        p["wo"] = g_wo[:, l].reshape(d, d)
        p["lcw"] = g_lcw[:, l].transpose(1, 0, 2).reshape(4, BRANCH_W)
        p["gcw"] = g_gcw[:, l].transpose(1, 0, 2).reshape(4, 3 * BRANCH_W)
        p["lcb"] = lru_conv_b[l][None]
        p["bt"] = jnp.zeros((128, 128), F32).at[:, :4].set(sgu_b[l].T)
        p["wa"] = _block_diag(lru_wa[l])
        p["wx"] = _block_diag(lru_wx[l])
        p["alog"] = _lane_row(gdn_a_log[l])
        p["dt"] = _lane_row(gdn_dt_bias[l])
        return p

    saved = []
    xs = x0
    for l in range(n_layer):
        p = layer_params(l)
        sv = {"p": p, "x0": xs}
        x1 = ffn_fwd(xs, ff1_norm[l][None], g_ff1_wg, g_ff1_wu, g_ff1_wd, l, tb=512, name=f"ffn1_fwd_{l}")
        h = norm_fwd(x1, mix_norm[l][None], tb=512, name=f"mix_norm_fwd_{l}")
        proj = matmul(h, p["w_in"], tm=1024, tn=1536, tk=1024, name=f"proj_{l}")
        ya = sgu_fwd(proj, sgu_ln_g[l][None], sgu_ln_b[l][None], sgu_w[l], p["bt"], tb=512, name=f"sgu_fwd_{l}")
        xc = conv_fwd(proj, C_BX, BRANCH_W, p["lcw"], p["lcb"], tb=512, name=f"lru_conv_fwd_{l}")
        yb, hl = lru_fwd(xc, proj, p["wa"], p["wx"], lru_ba[l][None], lru_bx[l][None], lru_lambda[l][None],
                         tb=256, name=f"lru_fwd_{l}")
        qkv = conv_fwd(proj, C_QKV, 3 * BRANCH_W, p["gcw"], jnp.zeros((1, 3 * BRANCH_W), F32), tb=512,
                       name=f"gdn_conv_fwd_{l}")
        yc, sall = gdn_fwd(qkv, proj, p["alog"], p["dt"], gdn_norm_g[l][None], name=f"gdn_fwd_{l}")
        yd = pool_fwd(proj, pool_w[l], pool_scale[l][None], tb=512, name=f"pool_fwd_{l}")
        ys = (ya, yb, yc, yd)
        x2, merged = merge_fwd(x1, ys, proj, p["wb"], p["wo"], tb=256, name=f"merge_fwd_{l}")
        x3 = ffn_fwd(x2, ff2_norm[l][None], g_ff2_wg, g_ff2_wu, g_ff2_wd, l, tb=512, name=f"ffn2_fwd_{l}")
        sv.update(x1=x1, h=h, proj=proj, xc=xc, hl=hl, qkv=qkv, sall=sall, ys=ys, merged=merged, x2=x2)
        saved.append(sv)
        xs = x3

    dx, d_final, loss_blk = loss_head(xs, final_norm[None], loss_target[0], tb=512, name="loss_head")
    loss = lax.psum(loss_blk[0, 0], ("x", "y", "c"))

    small_g = {n: [None] * n_layer for n in _SMALL if n != "final_norm"}
    big_g = [None] * n_layer
    for l in reversed(range(n_layer)):
        sv = saved[l]
        p = sv["p"]
        proj = sv["proj"]
        dx2, dg_ff2, dab2, hid2, h2, dy2 = ffn_bwd(sv["x2"], dx, ff2_norm[l][None], g_ff2_wg, g_ff2_wu, g_ff2_wd,
                                                  l, tb=256, name=f"ffn2_bwd_{l}")
        dwgu2 = matmul(h2, dab2, ta=True, tm=1024, tn=FF_SHARD, tk=512, name=f"ffn2_dwgu_{l}")
        dwd2 = matmul(hid2, dy2, ta=True, tm=FF_SHARD, tn=1024, tk=512, name=f"ffn2_dwd_{l}")
        dgp, dbr, dya, dyb, dyc, dyd, dx2b = merge_bwd(dx2, sv["ys"], proj, p["wb"], p["wo"], tb=256,
                                                       name=f"merge_bwd_{l}")
        dwo = matmul(sv["merged"], dx2b, ta=True, tm=1024, tn=1024, tk=512, name=f"dwo_{l}")
        dwb = jnp.stack([matmul(sv["ys"][g], dbr[g], ta=True, tm=BRANCH_W, tn=1024, tk=512, name=f"dwb{g}_{l}")
                         for g in range(4)])
        du, dv, dlg, dlb, dws, dbt = sgu_bwd(proj, dya, sgu_ln_g[l][None], sgu_ln_b[l][None], sgu_w[l], p["bt"],
                                             tb=512, name=f"sgu_bwd_{l}")
        dxc, dgate, dwa, dwx, dba, dbx, dlam = lru_bwd(sv["xc"], proj, sv["hl"], dyb, p["wa"], p["wx"],
                                                       lru_ba[l][None], lru_bx[l][None], lru_lambda[l][None],
                                                       tb=256, name=f"lru_bwd_{l}")
        dbx_in, dcw_l = conv_bwd(proj, C_BX, BRANCH_W, dxc, p["lcw"], tb=512, name=f"lru_conv_bwd_{l}")
        dqkv, dz, dbeta, dal, ddt, dng = gdn_bwd(sv["qkv"], proj, sv["sall"], dyc, p["alog"], p["dt"],
                                                 gdn_norm_g[l][None], name=f"gdn_bwd_{l}")
        dqkv_in, dcw_g = conv_bwd(proj, C_QKV, 3 * BRANCH_W, dqkv, p["gcw"], tb=512, name=f"gdn_conv_bwd_{l}")
        ddx, dwp, dsc = pool_bwd(proj, dyd, pool_w[l], pool_scale[l][None], tb=512, name=f"pool_bwd_{l}")
        s = dx.shape[0]
        dproj = jnp.concatenate([dqkv_in, dz, du, dv, dbx_in, dgate, ddx, dbeta,
                                 jnp.zeros((s, C_GATE - C_BA - 128), BF16), dgp], axis=1)
        dh = matmul(dproj, p["w_in"], tb=True, tm=1024, tn=1024, tk=1536, name=f"dh_mix_{l}")
        dw_in = matmul(sv["h"], dproj, ta=True, tm=1024, tn=1536, tk=512, name=f"dw_in_{l}")
        dx1, dg_mix = norm_bwd(sv["x1"], mix_norm[l][None], dh, dx2, tb=512, name=f"mix_norm_bwd_{l}")
        dx, dg_ff1, dab1, hid1, h1, dy1 = ffn_bwd(sv["x0"], dx1, ff1_norm[l][None], g_ff1_wg, g_ff1_wu, g_ff1_wd,
                                                 l, tb=256, name=f"ffn1_bwd_{l}")
        dwgu1 = matmul(h1, dab1, ta=True, tm=1024, tn=FF_SHARD, tk=512, name=f"ffn1_dwgu_{l}")
        dwd1 = matmul(hid1, dy1, ta=True, tm=FF_SHARD, tn=1024, tk=512, name=f"ffn1_dwd_{l}")

        for n, g in (("ff1_norm", dg_ff1[0]), ("mix_norm", dg_mix[0]), ("sgu_ln_g", dlg[0]), ("sgu_ln_b", dlb[0]),
                     ("sgu_w", dws), ("sgu_b", dbt[:, :4].T), ("lru_conv_w", dcw_l[:4]), ("lru_conv_b", dcw_l[4]),
                     ("lru_wa", _diag_blocks(dwa)), ("lru_ba", dba[0]), ("lru_wx", _diag_blocks(dwx)),
                     ("lru_bx", dbx[0]), ("lru_lambda", dlam[0]), ("gdn_conv_w", dcw_g[:4]),
                     ("gdn_a_log", dal[0, 4:8]), ("gdn_dt_bias", ddt[0, 4:8]), ("gdn_norm_g", dng[0]),
                     ("pool_w", dwp), ("pool_scale", dsc[0]), ("ff2_norm", dg_ff2[0])):
            small_g[n][l] = g
        dwb_sh = dwb.reshape(4, BRANCH_W, N_SHARD, d // N_SHARD).transpose(2, 0, 1, 3)
        big_g[l] = [dwgu1, dwd1.reshape(N_SHARD, 2, FF_SHARD // 2, d),
                    _w_in_grad_to_shards(dw_in), dwb_sh.reshape(N_SHARD, 2, 2 * BRANCH_W, d // N_SHARD),
                    dwo.reshape(N_SHARD, 2, d // 8, d), dwgu2,
                    dwd2.reshape(N_SHARD, 2, FF_SHARD // 2, d)]

    bigs = [t for l in range(n_layer) for t in big_g[l]]
    recv1 = pair_send_halves(bigs, name="grad_pair_exchange")
    part = [sum_halves(t, r, c_idx, name=f"grad_pair_sum_{i}") for i, (t, r) in enumerate(zip(bigs, recv1))]
    recv2 = chip_scatter(part, name="grad_chip_exchange")
    red = [sum_chips(r, name=f"grad_chip_sum_{i}") for i, r in enumerate(recv2)]
    full = pair_gather(red, name="grad_pair_gather")
    per_layer = len(big_g[0])
    grads = {}

    def layers(k, f):
        return jnp.stack([f(full[l * per_layer + k]) for l in range(n_layer)])

    grads["ff1_wg"] = layers(0, lambda t: t[0])
    grads["ff1_wu"] = layers(0, lambda t: t[1])
    grads["ff1_wd"] = layers(1, lambda t: t.reshape(FF_SHARD, d))
    grads["w_in"] = layers(2, lambda t: t.reshape(d, P_IN_SHARD_PAD)[:, :P_IN_SHARD])
    grads["w_branch"] = layers(3, lambda t: t.reshape(4, BRANCH_W, d // N_SHARD))
    grads["w_out"] = layers(4, lambda t: t.reshape(d // N_SHARD, d))
    grads["ff2_wg"] = layers(5, lambda t: t[0])
    grads["ff2_wu"] = layers(5, lambda t: t[1])
    grads["ff2_wd"] = layers(6, lambda t: t.reshape(FF_SHARD, d))

    small_full = [jnp.stack(small_g[n]) for n in _SMALL if n != "final_norm"] + [d_final[0]]
    summed = _unpack(all_sum_small(_pack(small_full), name="grad_small_allreduce"), [t.shape for t in small_full])
    for n, g in zip(_SMALL, summed):
        if n in _CONV:
            width = a[n].shape[-1]
            g = lax.dynamic_slice_in_dim(g, chip * width, width, axis=2)
        grads[n] = g

    delta, new_m, new_v = {}, {}, {}
    for n in _BIG:
        delta[n], new_m[n], new_v[n] = adamw(a[n], grads[n], a["m_" + n], a["v_" + n], name=f"adamw_{n}")
    shapes = [a[n].shape for n in _SMALL]
    packed = [_pack([src[pre + n] for n in _SMALL]) for src, pre in ((a, ""), (grads, ""), (a, "m_"), (a, "v_"))]
    for store, buf in zip((delta, new_m, new_v), adamw(*packed, name="adamw_small")):
        store.update(zip(_SMALL, _unpack(buf, shapes)))

    return (loss, dx[None], *[grads[n] for n in _WEIGHTS], *[delta[n] for n in _WEIGHTS],
            *[new_m[n] for n in _WEIGHTS], *[new_v[n] for n in _WEIGHTS])
```

```python
import functools
import math

import jax
import jax.numpy as jnp
from jax import lax
from jax.experimental import pallas as pl
from jax.experimental.pallas import tpu as pltpu

F32 = jnp.float32
BF16 = jnp.bfloat16
MESH = pl.DeviceIdType.MESH

D_MODEL = 1024
N_SHARD = 4
D_FF = 2816
FF_SHARD = D_FF // N_SHARD
BRANCH_W = 512
CHUNK = 64
GDN_HEADS = 4
GDN_DK = 128
GDN_CPS = 4
LRU_C = 8.0
EPS = 1e-6
P_IN = 8712
P_IN_SHARD = P_IN // N_SHARD
P_IN_SHARD_PAD = 2304
P_PAD = 9216
C_QKV, C_Z, C_AU, C_AV, C_BX, C_BG, C_DX, C_BA, C_GATE = 0, 1536, 2048, 2560, 3072, 3584, 4096, 4608, 5120

ADAM_LR, ADAM_B1, ADAM_B2, ADAM_EPS, ADAM_WD, ADAM_STEP = 0.001, 0.9, 0.999, 1e-08, 0.01, 10


def _gelu(x):
    return 0.5 * x * (1.0 + jnp.tanh(0.7978845608028654 * (x + 0.044715 * (x * x * x))))


def _gelu_grad(x):
    u = 0.7978845608028654 * (x + 0.044715 * (x * x * x))
    t = jnp.tanh(u)
    return 0.5 * (1.0 + t) + 0.5 * x * (1.0 - t * t) * 0.7978845608028654 * (1.0 + 3.0 * 0.044715 * x * x)


def _silu(x):
    return x * jax.nn.sigmoid(x)


def _softplus(x):
    return jnp.maximum(x, 0.0) + jnp.log1p(jnp.exp(-jnp.abs(x)))


def _dg(a, b, ca, cb):
    return lax.dot_general(a.astype(BF16), b.astype(BF16), (((ca,), (cb,)), ((), ())),
                           preferred_element_type=F32)


def _dg3(a, b, ca, cb):
    a1 = a.astype(BF16)
    a2 = (a - a1.astype(F32)).astype(BF16)
    b1 = b.astype(BF16)
    b2 = (b - b1.astype(F32)).astype(BF16)
    dn = (((ca,), (cb,)), ((), ()))
    d = functools.partial(lax.dot_general, dimension_numbers=dn, preferred_element_type=F32)
    return d(a1, b1) + (d(a1, b2) + d(a2, b1))


def _make_mm(ca, cb, dot):
    @jax.custom_vjp
    def f(a, b):
        return dot(a, b, ca, cb)

    def fwd(a, b):
        return dot(a, b, ca, cb), (a, b)

    def bwd(res, g):
        a, b = res
        if (ca, cb) == (1, 0):
            return dot(g, b, 1, 1), dot(a, g, 0, 0)
        if (ca, cb) == (1, 1):
            return dot(g, b, 1, 0), dot(g, a, 0, 0)
        return dot(b, g, 1, 1), dot(a, g, 1, 0)

    f.defvjp(fwd, bwd)
    return f


mm_nn = _make_mm(1, 0, _dg)
mm_nt = _make_mm(1, 1, _dg)
mm_tn = _make_mm(0, 0, _dg)
mm3_nn = _make_mm(1, 0, _dg3)


@jax.custom_vjp
def _tri_inv(aw):
    n = aw.shape[0]
    shp = (n, 2 * n)
    left = _lanes(shp) < n
    x = jnp.where(_rows(shp) == _lanes(shp), 1.0, 0.0) - aw
    zero = jnp.zeros((n, 4 * n), BF16)
    k = 1
    while k < n:
        x1 = x.astype(BF16)
        x2 = (x - x1.astype(F32)).astype(BF16)
        lhs = jnp.concatenate([x1, x2], axis=0)
        rhs = jnp.concatenate([zero, jnp.concatenate([x1, x2], axis=1)], axis=0)
        r = lax.dot_general(lhs, rhs, (((1,), (0,)), ((), ())), preferred_element_type=F32)
        mx = (r[:n, :2 * n] + r[:n, 2 * n:]) + (r[n:, :2 * n] + r[n:, 2 * n:])
        x = jnp.where(left, x, 0.0) + mx
        k *= 2
    return jnp.where(left, x, 0.0)


def _tri_inv_fwd(aw):
    t = _tri_inv(aw)
    return t, t


def _tri_inv_bwd(t, dt):
    n = t.shape[0]
    x = _dg3(t, dt, 0, 0)[:n]
    tp = jnp.concatenate([jnp.zeros_like(t), t], axis=0)
    return (-_dg3(x, tp, 1, 1),)


_tri_inv.defvjp(_tri_inv_fwd, _tri_inv_bwd)


def _rows(shape):
    return lax.broadcasted_iota(jnp.int32, shape, 0)


def _lanes(shape):
    return lax.broadcasted_iota(jnp.int32, shape, 1)


def _col(x, j):
    return jnp.sum(jnp.where(_lanes(x.shape) == j, x, 0.0), axis=1, keepdims=True)


def _put_col(col, j, width):
    shape = (col.shape[0], width)
    return jnp.where(_lanes(shape) == j, jnp.broadcast_to(col, shape), 0.0)


def _params(*sem):
    return pltpu.CompilerParams(dimension_semantics=sem)


def matmul(a, b, *, ta=False, tb=False, tm, tn, tk, out_dtype=F32, name):
    ba, bb = a.shape[:-2], b.shape[:-2]
    batch = ba if len(ba) >= len(bb) else bb
    assert ba in ((), batch) and bb in ((), batch)
    nb = len(batch)
    (m, k) = (a.shape[-1], a.shape[-2]) if ta else (a.shape[-2], a.shape[-1])
    (k2, n) = (b.shape[-1], b.shape[-2]) if tb else (b.shape[-2], b.shape[-1])
    tm, tn, tk = min(tm, m), min(tn, n), min(tk, k)
    assert k == k2 and m % tm == 0 and n % tn == 0 and k % tk == 0, (a.shape, b.shape, tm, tn, tk)
    nk = k // tk
    grid = batch + (m // tm, n // tn, nk)

    def a_map(*g):
        i, kk = g[nb], g[nb + 2]
        return (g[:nb] if ba else ()) + ((kk, i) if ta else (i, kk))

    def b_map(*g):
        j, kk = g[nb + 1], g[nb + 2]
        return (g[:nb] if bb else ()) + ((j, kk) if tb else (kk, j))

    def o_map(*g):
        return g[:nb] + (g[nb], g[nb + 1])

    a_blk = (None,) * len(ba) + ((tk, tm) if ta else (tm, tk))
    b_blk = (None,) * len(bb) + ((tn, tk) if tb else (tk, tn))
    o_blk = (None,) * nb + (tm, tn)

    def body(a_ref, b_ref, o_ref, acc_ref):
        kk = pl.program_id(nb + 2)

        @pl.when(kk == 0)
        def _():
            acc_ref[...] = jnp.zeros_like(acc_ref)

        acc_ref[...] += _dg(a_ref[...], b_ref[...], 0 if ta else 1, 1 if tb else 0)

        @pl.when(kk == nk - 1)
        def _():
            o_ref[...] = acc_ref[...].astype(o_ref.dtype)

    return pl.pallas_call(
        body, name=name, grid=grid,
        in_specs=[pl.BlockSpec(a_blk, a_map), pl.BlockSpec(b_blk, b_map)],
        out_specs=pl.BlockSpec(o_blk, o_map),
        out_shape=jax.ShapeDtypeStruct(batch + (m, n), out_dtype),
        scratch_shapes=[pltpu.VMEM((tm, tn), F32)],
        compiler_params=_params(*(("parallel",) * (nb + 2) + ("arbitrary",))),
    )(a, b)


def norm_fwd(x, g, *, tb, name):
    s, d = x.shape

    def body(x_ref, g_ref, h_ref):
        xv = x_ref[...]
        r = lax.rsqrt(jnp.mean(xv * xv, axis=-1, keepdims=True) + EPS)
        h_ref[...] = (xv * r * g_ref[...]).astype(BF16)

    return pl.pallas_call(
        body, name=name, grid=(s // tb,),
        in_specs=[pl.BlockSpec((tb, d), lambda i: (i, 0)), pl.BlockSpec((1, d), lambda i: (0, 0))],
        out_specs=pl.BlockSpec((tb, d), lambda i: (i, 0)),
        out_shape=jax.ShapeDtypeStruct((s, d), BF16),
        compiler_params=_params("parallel"),
    )(x, g)


def _rms_bwd(xv, gv, dh):
    r = lax.rsqrt(jnp.mean(xv * xv, axis=-1, keepdims=True) + EPS)
    xh = xv * r
    dxh = dh * gv
    dx = r * (dxh - xh * jnp.mean(dxh * xh, axis=-1, keepdims=True))
    return dx, jnp.sum(dh * xh, axis=0, keepdims=True)


def norm_bwd(x, g, dh, dres, *, tb, name):
    s, d = x.shape

    def body(x_ref, g_ref, dh_ref, dres_ref, dx_ref, dg_ref):
        dx, dg = _rms_bwd(x_ref[...], g_ref[...], dh_ref[...])
        dx_ref[...] = dres_ref[...] + dx

        @pl.when(pl.program_id(0) == 0)
        def _():
            dg_ref[...] = jnp.zeros_like(dg_ref)

        dg_ref[...] += dg

    row = pl.BlockSpec((tb, d), lambda i: (i, 0))
    one = pl.BlockSpec((1, d), lambda i: (0, 0))
    return pl.pallas_call(
        body, name=name, grid=(s // tb,),
        in_specs=[row, one, row, row], out_specs=[row, one],
        out_shape=[jax.ShapeDtypeStruct((s, d), F32), jax.ShapeDtypeStruct((1, d), F32)],
        compiler_params=_params("arbitrary"),
    )(x, g, dh, dres)


def ffn_fwd(x, g, wg, wu, wd, layer, *, tb, name):
    s, d = x.shape
    fc = wd.shape[-2]

    def body(x_ref, g_ref, wg_ref, wu_ref, wd_ref, o_ref, h_scr, acc_scr):
        j = pl.program_id(1)

        @pl.when(j == 0)
        def _():
            xv = x_ref[...]
            r = lax.rsqrt(jnp.mean(xv * xv, axis=-1, keepdims=True) + EPS)
            h_scr[...] = (xv * r * g_ref[...]).astype(BF16)
            acc_scr[...] = jnp.zeros_like(acc_scr)

        h = h_scr[...]
        a = _dg(h, wg_ref[...], 1, 0)
        b = _dg(h, wu_ref[...], 1, 0)
        acc_scr[...] += _dg(_silu(a) * b, wd_ref[...], 1, 0)

        @pl.when(j == N_SHARD - 1)
        def _():
            o_ref[...] = x_ref[...] + 0.5 * acc_scr[...]

    row = pl.BlockSpec((tb, d), lambda i, j: (i, 0))
    w_in = pl.BlockSpec((None, None, d, fc), lambda i, j: (j, layer, 0, 0))
    return pl.pallas_call(
        body, name=name, grid=(s // tb, N_SHARD),
        in_specs=[row, pl.BlockSpec((1, d), lambda i, j: (0, 0)), w_in, w_in,
                  pl.BlockSpec((None, None, fc, d), lambda i, j: (j, layer, 0, 0))],
        out_specs=row,
        out_shape=jax.ShapeDtypeStruct((s, d), F32),
        scratch_shapes=[pltpu.VMEM((tb, d), BF16), pltpu.VMEM((tb, d), F32)],
        compiler_params=_params("parallel", "arbitrary"),
    )(x, g, wg, wu, wd)


def ffn_bwd(x, dxo, g, wg, wu, wd, layer, *, tb, name):
    s, d = x.shape
    fc = wd.shape[-2]

    def body(x_ref, dxo_ref, g_ref, wg_ref, wu_ref, wd_ref,
             dx_ref, dg_ref, dab_ref, hid_ref, h_ref, dy_ref, dh_scr):
        i, j = pl.program_id(0), pl.program_id(1)

        @pl.when(j == 0)
        def _():
            xv = x_ref[...]
            r = lax.rsqrt(jnp.mean(xv * xv, axis=-1, keepdims=True) + EPS)
            h_ref[...] = (xv * r * g_ref[...]).astype(BF16)
            dy_ref[...] = (0.5 * dxo_ref[...]).astype(BF16)
            dh_scr[...] = jnp.zeros_like(dh_scr)

        h = h_ref[...]
        a = _dg(h, wg_ref[...], 1, 0)
        b = _dg(h, wu_ref[...], 1, 0)
        sg = jax.nn.sigmoid(a)
        sl = a * sg
        dhid = _dg(dy_ref[...], wd_ref[...], 1, 1)
        da = (dhid * b * (sg * (1.0 + a * (1.0 - sg)))).astype(BF16)
        db = (dhid * sl).astype(BF16)
        dab_ref[0] = da
        dab_ref[1] = db
        hid_ref[...] = (sl * b).astype(BF16)
        dh_scr[...] += _dg(da, wg_ref[...], 1, 1) + _dg(db, wu_ref[...], 1, 1)

        @pl.when(jnp.logical_and(i == 0, j == 0))
        def _():
            dg_ref[...] = jnp.zeros_like(dg_ref)

        @pl.when(j == N_SHARD - 1)
        def _():
            dx, dg = _rms_bwd(x_ref[...], g_ref[...], dh_scr[...])
            dx_ref[...] = dxo_ref[...] + dx
            dg_ref[...] += dg

    row = pl.BlockSpec((tb, d), lambda i, j: (i, 0))
    one = pl.BlockSpec((1, d), lambda i, j: (0, 0))
    w_in = pl.BlockSpec((None, None, d, fc), lambda i, j: (j, layer, 0, 0))
    return pl.pallas_call(
        body, name=name, grid=(s // tb, N_SHARD),
        in_specs=[row, row, one, w_in, w_in,
                  pl.BlockSpec((None, None, fc, d), lambda i, j: (j, layer, 0, 0))],
        out_specs=[row, one,
                   pl.BlockSpec((None, 2, tb, fc), lambda i, j: (j, 0, i, 0)),
                   pl.BlockSpec((None, tb, fc), lambda i, j: (j, i, 0)),
                   row, row],
        out_shape=[jax.ShapeDtypeStruct((s, d), F32), jax.ShapeDtypeStruct((1, d), F32),
                   jax.ShapeDtypeStruct((N_SHARD, 2, s, fc), BF16),
                   jax.ShapeDtypeStruct((N_SHARD, s, fc), BF16),
                   jax.ShapeDtypeStruct((s, d), BF16), jax.ShapeDtypeStruct((s, d), BF16)],
        scratch_shapes=[pltpu.VMEM((tb, d), F32)],
        compiler_params=_params("arbitrary", "arbitrary"),
    )(x, dxo, g, wg, wu, wd)


def _pblk(tb, width, col0, rev_n=None):
    assert col0 % width == 0
    cb = col0 // width
    if rev_n is None:
        return pl.BlockSpec((tb, width), lambda i: (i, cb))
    return pl.BlockSpec((tb, width), lambda i: (rev_n - 1 - i, cb))


def _full(shape):
    return pl.BlockSpec(shape, lambda i: (0,) * len(shape))


def conv_fwd(src, col0, width, w, bias, *, tb, name):
    s = src.shape[0]
    cb = col0 // width

    def body(x_ref, p_ref, w_ref, b_ref, y_ref):
        i = pl.program_id(0)
        prev = jnp.where(i > 0, p_ref[...], 0.0)
        xe = jnp.concatenate([prev, x_ref[...]], axis=0)
        wv = w_ref[...]
        acc = b_ref[...] + wv[3:4] * xe[8:]
        for k in range(3):
            acc = acc + wv[k:k + 1] * pltpu.roll(xe, 3 - k, 0)[8:]
        y_ref[...] = acc

    return pl.pallas_call(
        body, name=name, grid=(s // tb,),
        in_specs=[_pblk(tb, width, col0),
                  pl.BlockSpec((8, width), lambda i: (jnp.maximum(i * (tb // 8) - 1, 0), cb)),
                  _full((4, width)), _full((1, width))],
        out_specs=pl.BlockSpec((tb, width), lambda i: (i, 0)),
        out_shape=jax.ShapeDtypeStruct((s, width), F32),
        compiler_params=_params("parallel"),
    )(src, src, w, bias)


def conv_bwd(src, col0, width, dpre, w, *, tb, name):
    s = src.shape[0]
    cb = col0 // width
    nt = s // tb

    def body(x_ref, p_ref, d_ref, n_ref, w_ref, dx_ref, dwb_ref):
        i = pl.program_id(0)
        d = d_ref[...]
        nxt = jnp.where(i < nt - 1, n_ref[...], 0.0)
        de = jnp.concatenate([d, nxt], axis=0)
        wv = w_ref[...]
        dx = wv[3:4] * d
        for k in range(3):
            dx = dx + wv[k:k + 1] * pltpu.roll(de, tb + 8 - (3 - k), 0)[:tb]
        dx_ref[...] = dx.astype(BF16)

        @pl.when(i == 0)
        def _():
            dwb_ref[...] = jnp.zeros_like(dwb_ref)

        prev = jnp.where(i > 0, p_ref[...], 0.0)
        xe = jnp.concatenate([prev, x_ref[...]], axis=0)
        dwb_ref[3:4, :] += jnp.sum(d * xe[8:], axis=0, keepdims=True)
        for k in range(3):
            dwb_ref[k:k + 1, :] += jnp.sum(d * pltpu.roll(xe, 3 - k, 0)[8:], axis=0, keepdims=True)
        dwb_ref[4:5, :] += jnp.sum(d, axis=0, keepdims=True)

    return pl.pallas_call(
        body, name=name, grid=(nt,),
        in_specs=[_pblk(tb, width, col0),
                  pl.BlockSpec((8, width), lambda i: (jnp.maximum(i * (tb // 8) - 1, 0), cb)),
                  pl.BlockSpec((tb, width), lambda i: (i, 0)),
                  pl.BlockSpec((8, width), lambda i: (jnp.minimum((i + 1) * (tb // 8), s // 8 - 1), 0)),
                  _full((4, width))],
        out_specs=[pl.BlockSpec((tb, width), lambda i: (i, 0)), _full((8, width))],
        out_shape=[jax.ShapeDtypeStruct((s, width), BF16), jax.ShapeDtypeStruct((8, width), F32)],
        compiler_params=_params("arbitrary"),
    )(src, src, dpre, dpre, w)


SGU_BLOCK = 128


def _sgu_pre(u, v, lg, lb):
    ug = _gelu(u)
    vg = _gelu(v)
    mu = jnp.mean(vg, axis=-1, keepdims=True)
    var = jnp.mean(jnp.square(vg - mu), axis=-1, keepdims=True)
    return ug, (vg - mu) * lax.rsqrt(var + EPS) * lg + lb


def _sgu_mask():
    shp = (SGU_BLOCK, SGU_BLOCK)
    return (_rows(shp) // CHUNK) >= (_lanes(shp) // CHUNK)


def sgu_fwd(proj, lg, lb, ws, bt, *, tb, name):
    s = proj.shape[0]
    w = BRANCH_W

    def body(u_ref, v_ref, lg_ref, lb_ref, w_ref, b_ref, y_ref):
        ug, vn = _sgu_pre(u_ref[...], v_ref[...], lg_ref[...], lb_ref[...])
        mask = _sgu_mask()
        for g in range(4):
            wm = jnp.where(mask, w_ref[g], 0.0)
            bcol = _col(b_ref[...], g)
            cs = slice(128 * g, 128 * g + 128)
            for n in range(tb // SGU_BLOCK):
                rs = slice(SGU_BLOCK * n, SGU_BLOCK * (n + 1))
                mixed = _dg(wm, vn[rs, cs], 1, 0) + bcol
                y_ref[rs, cs] = (ug[rs, cs] * mixed).astype(BF16)

    return pl.pallas_call(
        body, name=name, grid=(s // tb,),
        in_specs=[_pblk(tb, w, C_AU), _pblk(tb, w, C_AV), _full((1, w)), _full((1, w)),
                  _full((4, 128, 128)), _full((128, 128))],
        out_specs=pl.BlockSpec((tb, w), lambda i: (i, 0)),
        out_shape=jax.ShapeDtypeStruct((s, w), BF16),
        compiler_params=_params("parallel"),
    )(proj, proj, lg, lb, ws, bt)


def sgu_bwd(proj, dy, lg, lb, ws, bt, *, tb, name):
    s = proj.shape[0]
    w = BRANCH_W

    def body(u_ref, v_ref, dy_ref, lg_ref, lb_ref, w_ref, b_ref,
             du_ref, dv_ref, dlg_ref, dlb_ref, dw_ref, db_ref, dug_scr, dvn_scr):
        @pl.when(pl.program_id(0) == 0)
        def _():
            dlg_ref[...] = jnp.zeros_like(dlg_ref)
            dlb_ref[...] = jnp.zeros_like(dlb_ref)
            dw_ref[...] = jnp.zeros_like(dw_ref)
            db_ref[...] = jnp.zeros_like(db_ref)

        (ug, vn), vf = jax.vjp(_sgu_pre, u_ref[...], v_ref[...], lg_ref[...], lb_ref[...])
        dyv = dy_ref[...]
        mask = _sgu_mask()
        for g in range(4):
            wm = jnp.where(mask, w_ref[g], 0.0)
            bcol = _col(b_ref[...], g)
            cs = slice(128 * g, 128 * g + 128)
            dwg = jnp.zeros((SGU_BLOCK, SGU_BLOCK), F32)
            dbc = jnp.zeros((SGU_BLOCK, 1), F32)
            for n in range(tb // SGU_BLOCK):
                rs = slice(SGU_BLOCK * n, SGU_BLOCK * (n + 1))
                blk = vn[rs, cs]
                mixed = _dg(wm, blk, 1, 0) + bcol
                dmix = dyv[rs, cs] * ug[rs, cs]
                dug_scr[rs, cs] = dyv[rs, cs] * mixed
                dvn_scr[rs, cs] = _dg(wm, dmix, 0, 0)
                dwg = dwg + _dg(dmix, blk, 1, 1)
                dbc = dbc + jnp.sum(dmix, axis=1, keepdims=True)
            dw_ref[g] += jnp.where(mask, dwg, 0.0)
            db_ref[...] += _put_col(dbc, g, 128)
        du, dv, dlg, dlb = vf((dug_scr[...], dvn_scr[...]))
        du_ref[...] = du.astype(BF16)
        dv_ref[...] = dv.astype(BF16)
        dlg_ref[...] += dlg
        dlb_ref[...] += dlb

    row = pl.BlockSpec((tb, w), lambda i: (i, 0))
    return pl.pallas_call(
        body, name=name, grid=(s // tb,),
        in_specs=[_pblk(tb, w, C_AU), _pblk(tb, w, C_AV), row, _full((1, w)), _full((1, w)),
                  _full((4, 128, 128)), _full((128, 128))],
        out_specs=[row, row, _full((1, w)), _full((1, w)), _full((4, 128, 128)), _full((128, 128))],
        out_shape=[jax.ShapeDtypeStruct((s, w), BF16), jax.ShapeDtypeStruct((s, w), BF16),
                   jax.ShapeDtypeStruct((1, w), F32), jax.ShapeDtypeStruct((1, w), F32),
                   jax.ShapeDtypeStruct((4, 128, 128), F32), jax.ShapeDtypeStruct((128, 128), F32)],
        scratch_shapes=[pltpu.VMEM((tb, w), F32), pltpu.VMEM((tb, w), F32)],
        compiler_params=_params("arbitrary"),
    )(proj, proj, dy, lg, lb, ws, bt)


POOL_HALO = 16


def _pooled(xe, t0, tb):
    s2 = xe + pltpu.roll(xe, 1, 0)
    s4 = s2 + pltpu.roll(s2, 2, 0)
    s8 = s4 + pltpu.roll(s4, 4, 0)
    s16 = s8 + pltpu.roll(s8, 8, 0)
    t = (t0 + _rows((tb, 128))).astype(F32)
    outs = []
    for g, sw in enumerate((s2, s4, s8, s16)):
        cs = slice(128 * g, 128 * g + 128)
        cnt = jnp.minimum(t + 1.0, float(2 ** (g + 1)))
        outs.append(sw[POOL_HALO:, cs] / cnt - xe[POOL_HALO:, cs])
    return outs


def pool_fwd(proj, wp, scale, *, tb, name):
    s = proj.shape[0]
    w = BRANCH_W

    def body(x_ref, p_ref, w_ref, sc_ref, y_ref):
        i = pl.program_id(0)
        prev = jnp.where(i > 0, p_ref[...], 0.0)
        xe = jnp.concatenate([prev, x_ref[...]], axis=0)
        pooled = _pooled(xe, i * tb, tb)
        sc = sc_ref[...]
        for g in range(4):
            cs = slice(128 * g, 128 * g + 128)
            y_ref[:, cs] = (_dg(pooled[g], w_ref[g], 1, 0) * sc[:, cs]).astype(BF16)

    cb = C_DX // w
    return pl.pallas_call(
        body, name=name, grid=(s // tb,),
        in_specs=[_pblk(tb, w, C_DX),
                  pl.BlockSpec((POOL_HALO, w), lambda i: (jnp.maximum(i * (tb // POOL_HALO) - 1, 0), cb)),
                  _full((4, 128, 128)), _full((1, w))],
        out_specs=pl.BlockSpec((tb, w), lambda i: (i, 0)),
        out_shape=jax.ShapeDtypeStruct((s, w), BF16),
        compiler_params=_params("parallel"),
    )(proj, proj, wp, scale)


def pool_bwd(proj, dy, wp, scale, *, tb, name):
    s = proj.shape[0]
    w = BRANCH_W
    nt = s // tb
    te = tb + POOL_HALO

    def body(x_ref, p_ref, dy_ref, n_ref, w_ref, sc_ref, dx_ref, dw_ref, dsc_ref):
        i = pl.program_id(0)

        @pl.when(i == 0)
        def _():
            dw_ref[...] = jnp.zeros_like(dw_ref)
            dsc_ref[...] = jnp.zeros_like(dsc_ref)

        prev = jnp.where(i > 0, p_ref[...], 0.0)
        xe = jnp.concatenate([prev, x_ref[...]], axis=0)
        pooled = _pooled(xe, i * tb, tb)
        dyv = dy_ref[...]
        nxt = jnp.where(i < nt - 1, n_ref[...], 0.0)
        dye = jnp.concatenate([dyv, nxt], axis=0)
        sc = sc_ref[...]
        t = (i * tb + _rows((te, 128))).astype(F32)
        for g in range(4):
            cs = slice(128 * g, 128 * g + 128)
            win = 2 ** (g + 1)
            dpm = dye[:, cs] * sc[:, cs]
            dpool = _dg(dpm, w_ref[g], 1, 1)
            q = dpool / jnp.minimum(t + 1.0, float(win))
            r, sh = q, 1
            while sh < win:
                r = r + pltpu.roll(r, te - sh, 0)
                sh *= 2
            dx_ref[:, cs] = (r[:tb] - dpool[:tb]).astype(BF16)
            dw_ref[g] += _dg(pooled[g], dpm[:tb], 0, 0)
            dsc_ref[:, cs] += jnp.sum(dyv[:, cs] * _dg(pooled[g], w_ref[g], 1, 0), axis=0, keepdims=True)

    cb = C_DX // w
    row = pl.BlockSpec((tb, w), lambda i: (i, 0))
    return pl.pallas_call(
        body, name=name, grid=(nt,),
        in_specs=[_pblk(tb, w, C_DX),
                  pl.BlockSpec((POOL_HALO, w), lambda i: (jnp.maximum(i * (tb // POOL_HALO) - 1, 0), cb)),
                  row,
                  pl.BlockSpec((POOL_HALO, w),
                               lambda i: (jnp.minimum((i + 1) * (tb // POOL_HALO), s // POOL_HALO - 1), 0)),
                  _full((4, 128, 128)), _full((1, w))],
        out_specs=[row, _full((4, 128, 128)), _full((1, w))],
        out_shape=[jax.ShapeDtypeStruct((s, w), BF16), jax.ShapeDtypeStruct((4, 128, 128), F32),
                   jax.ShapeDtypeStruct((1, w), F32)],
        compiler_params=_params("arbitrary"),
    )(proj, proj, dy, dy, wp, scale)


def _neg_expm1(z):
    p = 1.0 + z * (1.0 / 9.0)
    for n in (8.0, 7.0, 6.0, 5.0, 4.0, 3.0, 2.0):
        p = 1.0 + z * (1.0 / n) * p
    return jnp.where(z > -0.5, -z * p, 1.0 - jnp.exp(z))


def _lru_gates(xc, wa, wx, ba, bx, lam):
    r = jax.nn.sigmoid(mm_nn(xc, wa) + ba)
    i = jax.nn.sigmoid(mm_nn(xc, wx) + bx)
    log_a = -LRU_C * r * _softplus(-lam)
    a = jnp.exp(log_a)
    return a, jnp.sqrt(_neg_expm1(2.0 * log_a)) * (i * xc)


def _scan_down(a, b):
    n = a.shape[0]
    row = _rows(a.shape)
    sh = 1
    while sh < n:
        keep = row >= sh
        a_sh = jnp.where(keep, pltpu.roll(a, sh, 0), 1.0)
        b_sh = jnp.where(keep, pltpu.roll(b, sh, 0), 0.0)
        b = a * b_sh + b
        a = a * a_sh
        sh *= 2
    return a, b


def _scan_up(a, b):
    n = a.shape[0]
    row = _rows(a.shape)
    sh = 1
    while sh < n:
        keep = row < n - sh
        a_sh = jnp.where(keep, pltpu.roll(a, n - sh, 0), 1.0)
        b_sh = jnp.where(keep, pltpu.roll(b, n - sh, 0), 0.0)
        b = a * b_sh + b
        a = a * a_sh
        sh *= 2
    return a, b


def lru_fwd(xc, proj, wa, wx, ba, bx, lam, *, tb, name):
    s = xc.shape[0]
    w = BRANCH_W

    def body(xc_ref, gate_ref, wa_ref, wx_ref, ba_ref, bx_ref, lam_ref, y_ref, h_ref, carry):
        @pl.when(pl.program_id(0) == 0)
        def _():
            carry[...] = jnp.zeros_like(carry)

        a, b = _lru_gates(xc_ref[...], wa_ref[...], wx_ref[...], ba_ref[...], bx_ref[...], lam_ref[...])
        pa, hb = _scan_down(a, b)
        h = pa * carry[...] + hb
        h_ref[...] = h
        carry[...] = h_ref[tb - 1:tb, :]
        y_ref[...] = (h * _gelu(gate_ref[...])).astype(BF16)

    row = pl.BlockSpec((tb, w), lambda i: (i, 0))
    return pl.pallas_call(
        body, name=name, grid=(s // tb,),
        in_specs=[row, _pblk(tb, w, C_BG), _full((w, w)), _full((w, w)),
                  _full((1, w)), _full((1, w)), _full((1, w))],
        out_specs=[row, row],
        out_shape=[jax.ShapeDtypeStruct((s, w), BF16), jax.ShapeDtypeStruct((s, w), F32)],
        scratch_shapes=[pltpu.VMEM((1, w), F32)],
        compiler_params=_params("arbitrary"),
    )(xc, proj, wa, wx, ba, bx, lam)


def lru_bwd(xc, proj, h, dy, wa, wx, ba, bx, lam, *, tb, name):
    s = xc.shape[0]
    w = BRANCH_W
    nt = s // tb

    def body(xc_ref, gate_ref, h_ref, hp_ref, dy_ref, wa_ref, wx_ref, ba_ref, bx_ref, lam_ref,
             dxc_ref, dgate_ref, dwa_ref, dwx_ref, dba_ref, dbx_ref, dlam_ref, carry):
        i = pl.program_id(0)

        @pl.when(i == 0)
        def _():
            carry[...] = jnp.zeros_like(carry)
            for r in (dwa_ref, dwx_ref, dba_ref, dbx_ref, dlam_ref):
                r[...] = jnp.zeros_like(r)

        (a, b), vf = jax.vjp(_lru_gates, xc_ref[...], wa_ref[...], wx_ref[...],
                             ba_ref[...], bx_ref[...], lam_ref[...])
        gate = gate_ref[...]
        dyv = dy_ref[...]
        hv = h_ref[...]
        row = _rows((tb, w))
        a_next = jnp.where(row < tb - 1, pltpu.roll(a, tb - 1, 0), 1.0)
        pa, xb = _scan_up(a_next, dyv * _gelu(gate))
        dh = xb + pa * carry[...]
        carry[...] = a[0:1, :] * dh[0:1, :]
        h_first = jnp.where(i < nt - 1, hp_ref[7:8, :], 0.0)
        h_prev = jnp.where(row >= 1, pltpu.roll(hv, 1, 0), h_first)
        dxc, dwa, dwx, dba, dbx, dlam = vf((dh * h_prev, dh))
        dxc_ref[...] = dxc
        dgate_ref[...] = (dyv * hv * _gelu_grad(gate)).astype(BF16)
        dwa_ref[...] += dwa
        dwx_ref[...] += dwx
        dba_ref[...] += dba
        dbx_ref[...] += dbx
        dlam_ref[...] += dlam

    row_spec = pl.BlockSpec((tb, w), lambda i: (nt - 1 - i, 0))
    hp_spec = pl.BlockSpec((8, w), lambda i: (jnp.maximum((nt - 1 - i) * (tb // 8) - 1, 0), 0))
    return pl.pallas_call(
        body, name=name, grid=(nt,),
        in_specs=[row_spec, _pblk(tb, w, C_BG, rev_n=nt), row_spec, hp_spec, row_spec,
                  _full((w, w)), _full((w, w)), _full((1, w)), _full((1, w)), _full((1, w))],
        out_specs=[row_spec, row_spec, _full((w, w)), _full((w, w)),
                   _full((1, w)), _full((1, w)), _full((1, w))],
        out_shape=[jax.ShapeDtypeStruct((s, w), F32), jax.ShapeDtypeStruct((s, w), BF16),
                   jax.ShapeDtypeStruct((w, w), F32), jax.ShapeDtypeStruct((w, w), F32),
                   jax.ShapeDtypeStruct((1, w), F32), jax.ShapeDtypeStruct((1, w), F32),
                   jax.ShapeDtypeStruct((1, w), F32)],
        scratch_shapes=[pltpu.VMEM((1, w), F32)],
        compiler_params=_params("arbitrary"),
    )(xc, proj, h, h, dy, wa, wx, ba, bx, lam)


def _gdn_g(ba, alog, dt):
    return -jnp.exp(alog) * _softplus(ba + dt)


def _gdn_head(s_in, qp, kp, vp, zz, bpre, gc, ng):
    c = CHUNK
    ii, jj = _rows((c, c)), _lanes((c, c))
    q = _silu(qp)
    k = _silu(kp)
    v = _silu(vp)
    q = q * lax.rsqrt(jnp.sum(q * q, axis=-1, keepdims=True) + EPS) * (GDN_DK ** -0.5)
    k = k * lax.rsqrt(jnp.sum(k * k, axis=-1, keepdims=True) + EPS)
    beta = jax.nn.sigmoid(bpre)
    eye = (ii == jj).astype(F32)
    gr = jnp.sum(gc * eye, axis=0, keepdims=True)
    decay = jnp.exp(jnp.where(ii >= jj, gc - gr, -jnp.inf))
    kb = k * beta
    wi, wj = _rows((c, 2 * c)), _lanes((c, 2 * c)) - c
    pad = jnp.zeros((c, GDN_DK), F32)
    gr_w = jnp.sum(gc * (wi == wj).astype(F32), axis=0, keepdims=True)
    decay_w = jnp.exp(jnp.where(jnp.logical_and(wj >= 0, wi > wj), gc - gr_w, -jnp.inf))
    t = _tri_inv(mm_nt(kb, jnp.concatenate([pad, k], axis=0)) * decay_w)
    u = mm_nn(t, jnp.concatenate([v * beta, pad], axis=0))
    wk = mm_nn(t, jnp.concatenate([kb * jnp.exp(gc), pad], axis=0))
    attn = mm_nt(q, k) * decay
    v_new = u - mm_nn(wk, s_in)
    o = mm_nn(q * jnp.exp(gc), s_in) + mm_nn(attn, v_new)
    g_last = jnp.sum(jnp.where(_rows((c, 1)) == c - 1, gc, 0.0), axis=0, keepdims=True)
    s_out = s_in * jnp.exp(g_last) + mm_tn(k * jnp.exp(g_last - gc), v_new)
    o = o * lax.rsqrt(jnp.mean(o * o, axis=-1, keepdims=True) + EPS) * ng
    return o * _silu(zz), s_out


def _chunk_cumsum(g, transpose=False):
    c = CHUNK
    tri = (_rows((c, c)) >= _lanes((c, c))).astype(F32)
    return _dg3(tri, g, 0 if transpose else 1, 0)


def gdn_fwd(qkv, proj, alog, dt, ng, *, cps, name):
    s = qkv.shape[0]
    c = CHUNK
    nc = s // c
    hd = GDN_DK
    tb = c * cps

    def body(qkv_ref, z_ref, ba_ref, al_ref, dt_ref, ng_ref, y_ref, sall_ref, s_scr):
        @pl.when(pl.program_id(0) == 0)
        def _():
            s_scr[...] = jnp.zeros_like(s_scr)

        for ci in range(cps):
            rs = slice(c * ci, c * (ci + 1))
            ba = ba_ref[rs, :]
            gcum = _chunk_cumsum(_gdn_g(ba, al_ref[...], dt_ref[...]))
            for h in range(GDN_HEADS):
                cs = slice(hd * h, hd * (h + 1))
                s_in = s_scr[h]
                sall_ref[ci, h] = s_in
                y, s_out = _gdn_head(s_in, qkv_ref[rs, hd * h:hd * (h + 1)],
                                     qkv_ref[rs, 512 + hd * h:512 + hd * (h + 1)],
                                     qkv_ref[rs, 1024 + hd * h:1024 + hd * (h + 1)],
                                     z_ref[rs, cs], _col(ba, h), _col(gcum, 4 + h), ng_ref[...])
                y_ref[rs, cs] = y.astype(BF16)
                s_scr[h] = s_out

    return pl.pallas_call(
        body, name=name, grid=(nc // cps,),
        in_specs=[pl.BlockSpec((tb, 1536), lambda i: (i, 0)), _pblk(tb, 512, C_Z), _pblk(tb, 128, C_BA),
                  _full((1, 128)), _full((1, 128)), _full((1, 128))],
        out_specs=[pl.BlockSpec((tb, 512), lambda i: (i, 0)),
                   pl.BlockSpec((cps, GDN_HEADS, hd, hd), lambda i: (i, 0, 0, 0))],
        out_shape=[jax.ShapeDtypeStruct((s, 512), BF16), jax.ShapeDtypeStruct((nc, GDN_HEADS, hd, hd), F32)],
        scratch_shapes=[pltpu.VMEM((GDN_HEADS, hd, hd), F32)],
        compiler_params=_params("arbitrary"),
    )(qkv, proj, proj, alog, dt, ng)


def gdn_bwd(qkv, proj, sall, dy, alog, dt, ng, *, cps, name):
    s = qkv.shape[0]
    c = CHUNK
    nc = s // c
    hd = GDN_DK
    tb = c * cps
    nt = nc // cps

    def body(qkv_ref, z_ref, ba_ref, sall_ref, dy_ref, al_ref, dt_ref, ng_ref,
             dqkv_ref, dz_ref, dba_ref, dal_ref, ddt_ref, dng_ref, ds_scr):
        @pl.when(pl.program_id(0) == 0)
        def _():
            ds_scr[...] = jnp.zeros_like(ds_scr)
            dal_ref[...] = jnp.zeros_like(dal_ref)
            ddt_ref[...] = jnp.zeros_like(ddt_ref)
            dng_ref[...] = jnp.zeros_like(dng_ref)

        for ci in reversed(range(cps)):
            rs = slice(c * ci, c * (ci + 1))
            ba = ba_ref[rs, :]
            g, gvf = jax.vjp(_gdn_g, ba, al_ref[...], dt_ref[...])
            gcum = _chunk_cumsum(g)
            dgcum = jnp.zeros((c, 128), F32)
            dba = jnp.zeros((c, 128), F32)
            dng = jnp.zeros((1, 128), F32)
            for h in range(GDN_HEADS):
                cs = slice(hd * h, hd * (h + 1))
                _, vf = jax.vjp(_gdn_head, sall_ref[ci, h], qkv_ref[rs, hd * h:hd * (h + 1)],
                                qkv_ref[rs, 512 + hd * h:512 + hd * (h + 1)],
                                qkv_ref[rs, 1024 + hd * h:1024 + hd * (h + 1)],
                                z_ref[rs, cs], _col(ba, h), _col(gcum, 4 + h), ng_ref[...])
                ds_in, dq, dk, dv, dz, dbp, dgc, dn = vf((dy_ref[rs, cs], ds_scr[h]))
                ds_scr[h] = ds_in
                dqkv_ref[rs, hd * h:hd * (h + 1)] = dq
                dqkv_ref[rs, 512 + hd * h:512 + hd * (h + 1)] = dk
                dqkv_ref[rs, 1024 + hd * h:1024 + hd * (h + 1)] = dv
                dz_ref[rs, cs] = dz.astype(BF16)
                dgcum = dgcum + _put_col(dgc, 4 + h, 128)
                dba = dba + _put_col(dbp, h, 128)
                dng = dng + dn
            dba2, dal, ddt = gvf(_chunk_cumsum(dgcum, transpose=True))
            dba_ref[rs, :] = (dba + dba2).astype(BF16)
            dal_ref[...] += dal
            ddt_ref[...] += ddt
            dng_ref[...] += dng

    def rev(width):
        return pl.BlockSpec((tb, width), lambda i: (nt - 1 - i, 0))

    return pl.pallas_call(
        body, name=name, grid=(nt,),
        in_specs=[rev(1536), _pblk(tb, 512, C_Z, rev_n=nt), _pblk(tb, 128, C_BA, rev_n=nt),
                  pl.BlockSpec((cps, GDN_HEADS, hd, hd), lambda i: (nt - 1 - i, 0, 0, 0)), rev(512),
                  _full((1, 128)), _full((1, 128)), _full((1, 128))],
        out_specs=[rev(1536), rev(512), rev(128), _full((1, 128)), _full((1, 128)), _full((1, 128))],
        out_shape=[jax.ShapeDtypeStruct((s, 1536), F32), jax.ShapeDtypeStruct((s, 512), BF16),
                   jax.ShapeDtypeStruct((s, 128), BF16), jax.ShapeDtypeStruct((1, 128), F32),
                   jax.ShapeDtypeStruct((1, 128), F32), jax.ShapeDtypeStruct((1, 128), F32)],
        scratch_shapes=[pltpu.VMEM((GDN_HEADS, hd, hd), F32)],
        compiler_params=_params("arbitrary"),
    )(qkv, proj, proj, sall, dy, alog, dt, ng)


def merge_fwd(x, ys, proj, wb, wo, *, tb, name):
    s, d = x.shape

    def body(x_ref, ya, yb, yc, yd, g0, g1, g2, g3, wb_ref, wo_ref, o_ref, m_ref):
        m = jnp.zeros((tb, d), F32)
        for g, (y_ref, gp_ref) in enumerate(zip((ya, yb, yc, yd), (g0, g1, g2, g3))):
            m = m + jax.nn.sigmoid(gp_ref[...]) * _dg(y_ref[...], wb_ref[g], 1, 0)
        mb = m.astype(BF16)
        m_ref[...] = mb
        o_ref[...] = x_ref[...] + _dg(mb, wo_ref[...], 1, 0)

    row = pl.BlockSpec((tb, d), lambda i: (i, 0))
    yrow = pl.BlockSpec((tb, BRANCH_W), lambda i: (i, 0))
    return pl.pallas_call(
        body, name=name, grid=(s // tb,),
        in_specs=[row] + [yrow] * 4 + [_pblk(tb, d, C_GATE + d * g) for g in range(4)]
        + [_full((4, BRANCH_W, d)), _full((d, d))],
        out_specs=[row, row],
        out_shape=[jax.ShapeDtypeStruct((s, d), F32), jax.ShapeDtypeStruct((s, d), BF16)],
        compiler_params=_params("parallel"),
    )(x, *ys, proj, proj, proj, proj, wb, wo)


def merge_bwd(dx, ys, proj, wb, wo, *, tb, name):
    s, d = dx.shape

    def body(dx_ref, ya, yb, yc, yd, g0, g1, g2, g3, wb_ref, wo_ref,
             dgp_ref, dbr_ref, da, db, dc, dd, dxb_ref):
        dxb = dx_ref[...].astype(BF16)
        dxb_ref[...] = dxb
        dm = _dg(dxb, wo_ref[...], 1, 1)
        for g, (y_ref, gp_ref, dy_ref) in enumerate(zip((ya, yb, yc, yd), (g0, g1, g2, g3), (da, db, dc, dd))):
            br = _dg(y_ref[...], wb_ref[g], 1, 0)
            sg = jax.nn.sigmoid(gp_ref[...])
            dgp_ref[:, d * g:d * (g + 1)] = (dm * br * sg * (1.0 - sg)).astype(BF16)
            dbr = (dm * sg).astype(BF16)
            dbr_ref[g] = dbr
            dy_ref[...] = _dg(dbr, wb_ref[g], 1, 1)

    row = pl.BlockSpec((tb, d), lambda i: (i, 0))
    yrow = pl.BlockSpec((tb, BRANCH_W), lambda i: (i, 0))
    return pl.pallas_call(
        body, name=name, grid=(s // tb,),
        in_specs=[row] + [yrow] * 4 + [_pblk(tb, d, C_GATE + d * g) for g in range(4)]
        + [_full((4, BRANCH_W, d)), _full((d, d))],
        out_specs=[pl.BlockSpec((tb, 4 * d), lambda i: (i, 0)), pl.BlockSpec((4, tb, d), lambda i: (0, i, 0)),
                   yrow, yrow, yrow, yrow, row],
        out_shape=[jax.ShapeDtypeStruct((s, 4 * d), BF16), jax.ShapeDtypeStruct((4, s, d), BF16)]
        + [jax.ShapeDtypeStruct((s, BRANCH_W), F32)] * 4 + [jax.ShapeDtypeStruct((s, d), BF16)],
        compiler_params=_params("parallel"),
    )(dx, *ys, proj, proj, proj, proj, wb, wo)


def loss_head(x, g, target, *, tb, name):
    s, d = x.shape

    def body(x_ref, g_ref, t_ref, dx_ref, dg_ref, loss_ref):
        @pl.when(pl.program_id(0) == 0)
        def _():
            dg_ref[...] = jnp.zeros_like(dg_ref)
            loss_ref[...] = jnp.zeros_like(loss_ref)

        xv, gv = x_ref[...], g_ref[...]
        r = lax.rsqrt(jnp.mean(xv * xv, axis=-1, keepdims=True) + EPS)
        err = xv * r * gv - t_ref[...]
        loss_ref[...] += 0.5 * jnp.sum(jnp.mean(err * err, axis=-1, keepdims=True), axis=0, keepdims=True)
        dx, dg = _rms_bwd(xv, gv, err * (1.0 / d))
        dx_ref[...] = dx
        dg_ref[...] += dg

    row = pl.BlockSpec((tb, d), lambda i: (i, 0))
    return pl.pallas_call(
        body, name=name, grid=(s // tb,),
        in_specs=[row, _full((1, d)), row],
        out_specs=[row, _full((1, d)), _full((1, 128))],
        out_shape=[jax.ShapeDtypeStruct((s, d), F32), jax.ShapeDtypeStruct((1, d), F32),
                   jax.ShapeDtypeStruct((1, 128), F32)],
        compiler_params=_params("arbitrary"),
    )(x, g, target)


def adamw(w, g, m, v, *, name):
    shape = w.shape
    c = shape[-1]
    r = math.prod(shape[:-1])
    tr = r
    while tr * c * 4 > (1 << 20) and tr % 16 == 0:
        tr //= 2
    c1 = 1.0 / (1.0 - ADAM_B1 ** ADAM_STEP)
    c2 = 1.0 / (1.0 - ADAM_B2 ** ADAM_STEP)

    def body(w_ref, g_ref, m_ref, v_ref, d_ref, nm_ref, nv_ref):
        gv = g_ref[...]
        nm = ADAM_B1 * m_ref[...] + (1.0 - ADAM_B1) * gv
        nv = ADAM_B2 * v_ref[...] + (1.0 - ADAM_B2) * (gv * gv)
        nm_ref[...] = nm
        nv_ref[...] = nv
        d_ref[...] = -ADAM_LR * ((nm * c1) / (jnp.sqrt(nv * c2) + ADAM_EPS) + ADAM_WD * w_ref[...])

    blk = pl.BlockSpec((tr, c), lambda i: (i, 0))
    outs = pl.pallas_call(
        body, name=name, grid=(r // tr,),
        in_specs=[blk] * 4, out_specs=[blk] * 3,
        out_shape=[jax.ShapeDtypeStruct((r, c), F32)] * 3,
        compiler_params=_params("parallel"),
    )(*(t.reshape(r, c) for t in (w, g, m, v)))
    return tuple(o.reshape(shape) for o in outs)


_ANY = pl.BlockSpec(memory_space=pl.ANY)


def _place():
    x, y, c = lax.axis_index("x"), lax.axis_index("y"), lax.axis_index("c")
    return x, y, c, [(1 - x, y), (x, 1 - y), (1 - x, 1 - y)]


def gather_shards(bufs, *, name):
    n = len(bufs)

    def body(*refs):
        outs = refs[n:2 * n]
        ici_send, ici_recv, d2d_send, d2d_recv = refs[2 * n:]
        x, y, c, chips = _place()
        me = 2 * x + y

        def ici(a, j, slab, to):
            return pltpu.make_async_remote_copy(
                src_ref=outs[a].at[slab, c], dst_ref=outs[a].at[slab, c], send_sem=ici_send.at[a, j],
                recv_sem=ici_recv.at[a, j], device_id=to, device_id_type=MESH)

        def d2d(a, j, slab, half):
            return pltpu.make_async_remote_copy(
                src_ref=outs[a].at[slab, half], dst_ref=outs[a].at[slab, half], send_sem=d2d_send.at[a, j],
                recv_sem=d2d_recv.at[a, j], device_id=(x, y, 1 - c), device_id_type=MESH)

        sends = []
        for a in range(n):
            for j, (px, py) in enumerate(chips):
                cp = ici(a, j, me, (px, py, c))
                cp.start()
                sends.append(cp)
        for a in range(n):
            for j, (px, py) in enumerate(chips):
                ici(a, j, 2 * px + py, (px, py, c)).wait_recv()
                cp = d2d(a, j, 2 * px + py, c)
                cp.start()
                sends.append(cp)
        for a in range(n):
            for j, (px, py) in enumerate(chips):
                d2d(a, j, 2 * px + py, 1 - c).wait_recv()
        for cp in sends:
            cp.wait_send()

    return pl.pallas_call(
        body, name=name,
        in_specs=[_ANY] * n, out_specs=[_ANY] * n,
        out_shape=[jax.ShapeDtypeStruct(t.shape, t.dtype) for t in bufs],
        input_output_aliases={a: a for a in range(n)},
        scratch_shapes=[pltpu.SemaphoreType.DMA((n, 3))] * 4,
    )(*bufs)


def pair_send_halves(arrs, *, name):
    n = len(arrs)

    def body(*refs):
        ins, outs = refs[:n], refs[n:2 * n]
        send_sems, recv_sems = refs[2 * n:]
        x, y, c, _ = _place()
        copies = []
        for a in range(n):
            for j in range(N_SHARD):
                cp = pltpu.make_async_remote_copy(
                    src_ref=ins[a].at[j, 1 - c], dst_ref=outs[a].at[j], send_sem=send_sems.at[a, j],
                    recv_sem=recv_sems.at[a, j], device_id=(x, y, 1 - c), device_id_type=MESH)
                cp.start()
                copies.append(cp)
        for cp in copies:
            cp.wait()

    return pl.pallas_call(
        body, name=name,
        in_specs=[_ANY] * n, out_specs=[_ANY] * n,
        out_shape=[jax.ShapeDtypeStruct((N_SHARD,) + t.shape[2:], t.dtype) for t in arrs],
        scratch_shapes=[pltpu.SemaphoreType.DMA((n, N_SHARD)), pltpu.SemaphoreType.DMA((n, N_SHARD))],
    )(*arrs)


def chip_scatter(arrs, *, name):
    n = len(arrs)

    def body(*refs):
        ins, outs = refs[:n], refs[n:2 * n]
        send_sems, recv_sems = refs[2 * n:]
        x, y, c, chips = _place()
        me = 2 * x + y
        copies = []
        for a in range(n):
            for j, (px, py) in enumerate(chips):
                cp = pltpu.make_async_remote_copy(
                    src_ref=ins[a].at[2 * px + py], dst_ref=outs[a].at[me], send_sem=send_sems.at[a, j],
                    recv_sem=recv_sems.at[a, j], device_id=(px, py, c), device_id_type=MESH)
                cp.start()
                copies.append(cp)
        for cp in copies:
            cp.wait()

    return pl.pallas_call(
        body, name=name,
        in_specs=[_ANY] * n, out_specs=[_ANY] * n,
        out_shape=[jax.ShapeDtypeStruct(t.shape, t.dtype) for t in arrs],
        scratch_shapes=[pltpu.SemaphoreType.DMA((n, 3)), pltpu.SemaphoreType.DMA((n, 3))],
    )(*arrs)


def pair_gather(bufs, *, name):
    n = len(bufs)

    def body(*refs):
        outs = refs[n:2 * n]
        send_sems, recv_sems = refs[2 * n:]
        x, y, c, _ = _place()
        copies = []
        for a in range(n):
            cp = pltpu.make_async_remote_copy(
                src_ref=outs[a].at[c], dst_ref=outs[a].at[c], send_sem=send_sems.at[a],
                recv_sem=recv_sems.at[a], device_id=(x, y, 1 - c), device_id_type=MESH)
            cp.start()
            copies.append(cp)
        for cp in copies:
            cp.wait()

    return pl.pallas_call(
        body, name=name,
        in_specs=[_ANY] * n, out_specs=[_ANY] * n,
        out_shape=[jax.ShapeDtypeStruct(t.shape, t.dtype) for t in bufs],
        input_output_aliases={a: a for a in range(n)},
        scratch_shapes=[pltpu.SemaphoreType.DMA((n,)), pltpu.SemaphoreType.DMA((n,))],
    )(*bufs)


def all_sum_small(v, *, name):
    m = v.shape[0]

    def body(v_ref, o_ref, buf, send_sems, recv_sems, local_sem):
        x, y, c, chips = _place()
        me, sibling = (x, y, c), (x, y, 1 - c)

        def rows(px, py, pc):
            return buf.at[pl.ds((4 * px + 2 * py + pc) * m, m), :]

        def copy(k, block, to, src=None):
            return pltpu.make_async_remote_copy(
                src_ref=rows(*block) if src is None else src, dst_ref=rows(*block),
                send_sem=send_sems.at[k], recv_sem=recv_sems.at[k], device_id=to, device_id_type=MESH)

        mine = pltpu.make_async_copy(v_ref, rows(*me), local_sem)
        mine.start()
        first = [copy(0, me, sibling, src=v_ref)]
        first += [copy(1 + j, me, (*chip, c), src=v_ref) for j, chip in enumerate(chips)]
        for cp in first:
            cp.start()
        passed = [copy(4 + j, (*chip, c), sibling) for j, chip in enumerate(chips)]
        for j, chip in enumerate(chips):
            copy(1 + j, (*chip, c), me).wait_recv()
            passed[j].start()
        copy(0, sibling, me).wait_recv()
        for j, chip in enumerate(chips):
            copy(4 + j, (*chip, 1 - c), me).wait_recv()
        for cp in first + passed:
            cp.wait_send()
        mine.wait()
        acc = buf[pl.ds(0, m), :]
        for k in range(1, 8):
            acc = acc + buf[pl.ds(k * m, m), :]
        o_ref[...] = acc

    vm = pl.BlockSpec(memory_space=pltpu.VMEM)
    return pl.pallas_call(
        body, name=name, in_specs=[vm], out_specs=vm,
        out_shape=jax.ShapeDtypeStruct((m, 128), F32),
        scratch_shapes=[pltpu.VMEM((8 * m, 128), F32), pltpu.SemaphoreType.DMA((7,)),
                        pltpu.SemaphoreType.DMA((7,)), pltpu.SemaphoreType.DMA],
    )(v)


def sum_halves(arr, recv, c_idx, *, name):
    _, _, r, c = arr.shape
    tr = r
    while tr * c * 4 > (1 << 20) and tr % 32 == 0:
        tr //= 2

    def body(c_ref, a_ref, b_ref, o_ref):
        o_ref[...] = (a_ref[...] + b_ref[...]).astype(BF16)

    return pl.pallas_call(
        body, name=name,
        grid_spec=pltpu.PrefetchScalarGridSpec(
            num_scalar_prefetch=1, grid=(N_SHARD, r // tr),
            in_specs=[pl.BlockSpec((None, None, tr, c), lambda j, i, cr: (j, cr[0], i, 0)),
                      pl.BlockSpec((None, tr, c), lambda j, i, cr: (j, i, 0))],
            out_specs=pl.BlockSpec((None, tr, c), lambda j, i, cr: (j, i, 0))),
        out_shape=jax.ShapeDtypeStruct((N_SHARD, r, c), BF16),
        compiler_params=_params("parallel", "parallel"),
    )(c_idx, arr, recv)


def sum_chips(recv, part, place, *, name):
    _, r, c = recv.shape
    tr = r
    while tr * c * 4 > (1 << 20) and tr % 32 == 0:
        tr //= 2

    def body(place_ref, a0, a1, a2, a3, p_ref, o_ref):
        me = place_ref[1]
        acc = None
        for k, a_ref in enumerate((a0, a1, a2, a3)):
            term = jnp.where(me == k, p_ref[...], a_ref[...]).astype(F32)
            acc = term if acc is None else acc + term
        o_ref[...] = acc

    def slab(k):
        return pl.BlockSpec((None, tr, c), lambda i, pr: (jnp.where(pr[1] == k, (k + 1) % N_SHARD, k), i, 0))

    return pl.pallas_call(
        body, name=name,
        grid_spec=pltpu.PrefetchScalarGridSpec(
            num_scalar_prefetch=1, grid=(r // tr,),
            in_specs=[slab(k) for k in range(N_SHARD)]
            + [pl.BlockSpec((None, tr, c), lambda i, pr: (pr[1], i, 0))],
            out_specs=pl.BlockSpec((None, tr, c), lambda i, pr: (pr[0], i, 0))),
        out_shape=jax.ShapeDtypeStruct((2, r, c), F32),
        compiler_params=_params("parallel"),
    )(place, recv, recv, recv, recv, part)


_WEIGHTS = ("ff1_norm", "ff1_wg", "ff1_wu", "ff1_wd", "mix_norm", "w_in", "sgu_ln_g", "sgu_ln_b", "sgu_w",
            "sgu_b", "lru_conv_w", "lru_conv_b", "lru_wa", "lru_ba", "lru_wx", "lru_bx", "lru_lambda",
            "gdn_conv_w", "gdn_a_log", "gdn_dt_bias", "gdn_norm_g", "pool_w", "pool_scale", "w_branch",
            "w_out", "ff2_norm", "ff2_wg", "ff2_wu", "ff2_wd", "final_norm")
_BIG = ("ff1_wg", "ff1_wu", "ff1_wd", "w_in", "w_branch", "w_out", "ff2_wg", "ff2_wu", "ff2_wd")
_SMALL = tuple(n for n in _WEIGHTS if n not in _BIG)
_CONV = ("lru_conv_w", "gdn_conv_w")


def _seg_rows(shape):
    return -(-math.prod(shape) // 1024) * 8


def _pack(arrs):
    segs = []
    for t in arrs:
        rows = _seg_rows(t.shape)
        flat = t.reshape(-1)
        segs.append(jnp.pad(flat, (0, rows * 128 - flat.shape[0])).reshape(rows, 128))
    return jnp.concatenate(segs, axis=0)


def _unpack(buf, shapes):
    out, o = [], 0
    for shp in shapes:
        rows = _seg_rows(shp)
        out.append(buf[o:o + rows].reshape(-1)[:math.prod(shp)].reshape(shp))
        o += rows
    return out


def _block_diag(w):
    h, n, _ = w.shape
    out = jnp.zeros((h * n, h * n), w.dtype)
    for i in range(h):
        out = lax.dynamic_update_slice(out, w[i], (i * n, i * n))
    return out


def _diag_blocks(m, h=8, n=64):
    return jnp.stack([m[i * n:(i + 1) * n, i * n:(i + 1) * n] for i in range(h)])


def _lane_row(v):
    return jnp.zeros((1, 128), F32).at[0, 4:8].set(v)


def _w_in_to_padded(raw):
    sem = jnp.concatenate([raw[k, :, :P_IN_SHARD] for k in range(N_SHARD)], axis=1)
    zero = jnp.zeros((sem.shape[0], C_GATE - C_BA - 8), sem.dtype)
    return jnp.concatenate([sem[:, 2048:4096], sem[:, 0:2048], sem[:, 4104:4616], sem[:, 4096:4104], zero,
                            sem[:, 4616:P_IN]], axis=1)


def _w_in_grad_to_shards(dwp):
    sem = jnp.concatenate([dwp[:, C_AU:C_DX], dwp[:, 0:C_AU], dwp[:, C_BA:C_BA + 8], dwp[:, C_DX:C_BA],
                           dwp[:, C_GATE:]], axis=1)
    d = sem.shape[0]
    sh = sem.reshape(d, N_SHARD, P_IN_SHARD).transpose(1, 0, 2)
    sh = jnp.pad(sh, ((0, 0), (0, 0), (0, P_IN_SHARD_PAD - P_IN_SHARD)))
    return sh.reshape(N_SHARD, 2, d // 2, P_IN_SHARD_PAD)


def kernel(x, ff1_norm, ff1_wg, ff1_wu, ff1_wd, mix_norm, w_in, sgu_ln_g, sgu_ln_b, sgu_w, sgu_b, lru_conv_w,
           lru_conv_b, lru_wa, lru_ba, lru_wx, lru_bx, lru_lambda, gdn_conv_w, gdn_a_log, gdn_dt_bias, gdn_norm_g,
           pool_w, pool_scale, w_branch, w_out, ff2_norm, ff2_wg, ff2_wu, ff2_wd, final_norm, loss_target, m_ff1_norm,
           m_ff1_wg, m_ff1_wu, m_ff1_wd, m_mix_norm, m_w_in, m_sgu_ln_g, m_sgu_ln_b, m_sgu_w, m_sgu_b, m_lru_conv_w,
           m_lru_conv_b, m_lru_wa, m_lru_ba, m_lru_wx, m_lru_bx, m_lru_lambda, m_gdn_conv_w, m_gdn_a_log, m_gdn_dt_bias,
           m_gdn_norm_g, m_pool_w, m_pool_scale, m_w_branch, m_w_out, m_ff2_norm, m_ff2_wg, m_ff2_wu, m_ff2_wd,
           m_final_norm, v_ff1_norm, v_ff1_wg, v_ff1_wu, v_ff1_wd, v_mix_norm, v_w_in, v_sgu_ln_g, v_sgu_ln_b, v_sgu_w,
           v_sgu_b, v_lru_conv_w, v_lru_conv_b, v_lru_wa, v_lru_ba, v_lru_wx, v_lru_bx, v_lru_lambda, v_gdn_conv_w,
           v_gdn_a_log, v_gdn_dt_bias, v_gdn_norm_g, v_pool_w, v_pool_scale, v_w_branch, v_w_out, v_ff2_norm, v_ff2_wg,
           v_ff2_wu, v_ff2_wd, v_final_norm):
    a = dict(locals())
    n_layer = ff1_norm.shape[0]
    d = D_MODEL
    x0 = x[0]
    chip = 2 * lax.axis_index("x") + lax.axis_index("y")
    c_idx = jnp.reshape(lax.axis_index("c"), (1,)).astype(jnp.int32)
    place = jnp.stack([lax.axis_index("c"), chip]).astype(jnp.int32)

    def own_slab(t, dtype=BF16):
        buf = jnp.zeros((N_SHARD,) + t.shape, dtype)
        return lax.dynamic_update_slice(buf, t.astype(dtype)[None], (chip,) + (0,) * t.ndim)

    w_in_loc = jnp.pad(w_in, ((0, 0), (0, 0), (0, P_IN_SHARD_PAD - P_IN_SHARD)))
    (g_ff1_wg, g_ff1_wu, g_ff1_wd, g_w_in, g_wb, g_wo, g_ff2_wg, g_ff2_wu, g_ff2_wd, g_lcw, g_gcw) = gather_shards(
        [own_slab(t) for t in (ff1_wg, ff1_wu, ff1_wd, w_in_loc, w_branch, w_out, ff2_wg, ff2_wu, ff2_wd)]
        + [own_slab(lru_conv_w, F32), own_slab(gdn_conv_w, F32)], name="gather_weights")

    def layer_params(l):
        p = {}
        p["w_in"] = _w_in_to_padded(g_w_in[:, l])
        p["wb"] = g_wb[:, l].transpose(1, 2, 0, 3).reshape(4, BRANCH_W, d)
        p["wo"] = g_wo[:, l].reshape(d, d)
        p["lcw"] = g_lcw[:, l].transpose(1, 0, 2).reshape(4, BRANCH_W)
        p["gcw"] = g_gcw[:, l].transpose(1, 0, 2).reshape(4, 3 * BRANCH_W)
        p["lcb"] = lru_conv_b[l][None]
        p["bt"] = jnp.zeros((128, 128), F32).at[:, :4].set(sgu_b[l].T)
        p["wa"] = _block_diag(lru_wa[l])
        p["wx"] = _block_diag(lru_wx[l])
        p["alog"] = _lane_row(gdn_a_log[l])
        p["dt"] = _lane_row(gdn_dt_bias[l])
        return p

    saved = []
    xs = x0
    for l in range(n_layer):
        p = layer_params(l)
        sv = {"p": p, "x0": xs}
        x1 = ffn_fwd(xs, ff1_norm[l][None], g_ff1_wg, g_ff1_wu, g_ff1_wd, l, tb=512, name=f"ffn1_fwd_{l}")
        h = norm_fwd(x1, mix_norm[l][None], tb=512, name=f"mix_norm_fwd_{l}")
        proj = matmul(h, p["w_in"], tm=1024, tn=1536, tk=1024, name=f"proj_{l}")
        ya = sgu_fwd(proj, sgu_ln_g[l][None], sgu_ln_b[l][None], sgu_w[l], p["bt"], tb=512, name=f"sgu_fwd_{l}")
        xc = conv_fwd(proj, C_BX, BRANCH_W, p["lcw"], p["lcb"], tb=512, name=f"lru_conv_fwd_{l}")
        yb, hl = lru_fwd(xc, proj, p["wa"], p["wx"], lru_ba[l][None], lru_bx[l][None], lru_lambda[l][None],
                         tb=256, name=f"lru_fwd_{l}")
        qkv = conv_fwd(proj, C_QKV, 3 * BRANCH_W, p["gcw"], jnp.zeros((1, 3 * BRANCH_W), F32), tb=512,
                       name=f"gdn_conv_fwd_{l}")
        yc, sall = gdn_fwd(qkv, proj, p["alog"], p["dt"], gdn_norm_g[l][None], cps=GDN_CPS, name=f"gdn_fwd_{l}")
        yd = pool_fwd(proj, pool_w[l], pool_scale[l][None], tb=512, name=f"pool_fwd_{l}")
        ys = (ya, yb, yc, yd)
        x2, merged = merge_fwd(x1, ys, proj, p["wb"], p["wo"], tb=256, name=f"merge_fwd_{l}")
        x3 = ffn_fwd(x2, ff2_norm[l][None], g_ff2_wg, g_ff2_wu, g_ff2_wd, l, tb=512, name=f"ffn2_fwd_{l}")
        sv.update(x1=x1, h=h, proj=proj, xc=xc, hl=hl, qkv=qkv, sall=sall, ys=ys, merged=merged, x2=x2)
        saved.append(sv)
        xs = x3

    dx, d_final, loss_blk = loss_head(xs, final_norm[None], loss_target[0], tb=512, name="loss_head")
    loss = lax.psum(loss_blk[0, 0], ("x", "y", "c"))

    small_g = {n: [None] * n_layer for n in _SMALL if n != "final_norm"}
    big_g = [None] * n_layer
    for l in reversed(range(n_layer)):
        sv = saved[l]
        p = sv["p"]
        proj = sv["proj"]
        dx2, dg_ff2, dab2, hid2, h2, dy2 = ffn_bwd(sv["x2"], dx, ff2_norm[l][None], g_ff2_wg, g_ff2_wu, g_ff2_wd,
                                                  l, tb=256, name=f"ffn2_bwd_{l}")
        dwgu2 = matmul(h2, dab2, ta=True, tm=1024, tn=FF_SHARD, tk=512, name=f"ffn2_dwgu_{l}")
        dwd2 = matmul(hid2, dy2, ta=True, tm=FF_SHARD, tn=1024, tk=512, name=f"ffn2_dwd_{l}")
        dgp, dbr, dya, dyb, dyc, dyd, dx2b = merge_bwd(dx2, sv["ys"], proj, p["wb"], p["wo"], tb=256,
                                                       name=f"merge_bwd_{l}")
        dwo = matmul(sv["merged"], dx2b, ta=True, tm=1024, tn=1024, tk=512, name=f"dwo_{l}")
        dwb = jnp.stack([matmul(sv["ys"][g], dbr[g], ta=True, tm=BRANCH_W, tn=1024, tk=512, name=f"dwb{g}_{l}")
                         for g in range(4)])
        du, dv, dlg, dlb, dws, dbt = sgu_bwd(proj, dya, sgu_ln_g[l][None], sgu_ln_b[l][None], sgu_w[l], p["bt"],
                                             tb=512, name=f"sgu_bwd_{l}")
        dxc, dgate, dwa, dwx, dba, dbx, dlam = lru_bwd(sv["xc"], proj, sv["hl"], dyb, p["wa"], p["wx"],
                                                       lru_ba[l][None], lru_bx[l][None], lru_lambda[l][None],
                                                       tb=256, name=f"lru_bwd_{l}")
        dbx_in, dcw_l = conv_bwd(proj, C_BX, BRANCH_W, dxc, p["lcw"], tb=512, name=f"lru_conv_bwd_{l}")
        dqkv, dz, dbeta, dal, ddt, dng = gdn_bwd(sv["qkv"], proj, sv["sall"], dyc, p["alog"], p["dt"],
                                                 gdn_norm_g[l][None], cps=GDN_CPS, name=f"gdn_bwd_{l}")
        dqkv_in, dcw_g = conv_bwd(proj, C_QKV, 3 * BRANCH_W, dqkv, p["gcw"], tb=512, name=f"gdn_conv_bwd_{l}")
        ddx, dwp, dsc = pool_bwd(proj, dyd, pool_w[l], pool_scale[l][None], tb=512, name=f"pool_bwd_{l}")
        s = dx.shape[0]
        dproj = jnp.concatenate([dqkv_in, dz, du, dv, dbx_in, dgate, ddx, dbeta,
                                 jnp.zeros((s, C_GATE - C_BA - 128), BF16), dgp], axis=1)
        dh = matmul(dproj, p["w_in"], tb=True, tm=1024, tn=1024, tk=1536, name=f"dh_mix_{l}")
        dw_in = matmul(sv["h"], dproj, ta=True, tm=1024, tn=1536, tk=512, name=f"dw_in_{l}")
        dx1, dg_mix = norm_bwd(sv["x1"], mix_norm[l][None], dh, dx2, tb=512, name=f"mix_norm_bwd_{l}")
        dx, dg_ff1, dab1, hid1, h1, dy1 = ffn_bwd(sv["x0"], dx1, ff1_norm[l][None], g_ff1_wg, g_ff1_wu, g_ff1_wd,
                                                 l, tb=256, name=f"ffn1_bwd_{l}")
        dwgu1 = matmul(h1, dab1, ta=True, tm=1024, tn=FF_SHARD, tk=512, name=f"ffn1_dwgu_{l}")
        dwd1 = matmul(hid1, dy1, ta=True, tm=FF_SHARD, tn=1024, tk=512, name=f"ffn1_dwd_{l}")

        for n, g in (("ff1_norm", dg_ff1[0]), ("mix_norm", dg_mix[0]), ("sgu_ln_g", dlg[0]), ("sgu_ln_b", dlb[0]),
                     ("sgu_w", dws), ("sgu_b", dbt[:, :4].T), ("lru_conv_w", dcw_l[:4]), ("lru_conv_b", dcw_l[4]),
                     ("lru_wa", _diag_blocks(dwa)), ("lru_ba", dba[0]), ("lru_wx", _diag_blocks(dwx)),
                     ("lru_bx", dbx[0]), ("lru_lambda", dlam[0]), ("gdn_conv_w", dcw_g[:4]),
                     ("gdn_a_log", dal[0, 4:8]), ("gdn_dt_bias", ddt[0, 4:8]), ("gdn_norm_g", dng[0]),
                     ("pool_w", dwp), ("pool_scale", dsc[0]), ("ff2_norm", dg_ff2[0])):
            small_g[n][l] = g
        dwb_sh = dwb.reshape(4, BRANCH_W, N_SHARD, d // N_SHARD).transpose(2, 0, 1, 3)
        big_g[l] = [dwgu1, dwd1.reshape(N_SHARD, 2, FF_SHARD // 2, d),
                    _w_in_grad_to_shards(dw_in), dwb_sh.reshape(N_SHARD, 2, 2 * BRANCH_W, d // N_SHARD),
                    dwo.reshape(N_SHARD, 2, d // 8, d), dwgu2,
                    dwd2.reshape(N_SHARD, 2, FF_SHARD // 2, d)]

    bigs = [t for l in range(n_layer) for t in big_g[l]]
    recv1 = pair_send_halves(bigs, name="grad_pair_exchange")
    part = [sum_halves(t, r, c_idx, name=f"grad_pair_sum_{i}") for i, (t, r) in enumerate(zip(bigs, recv1))]
    recv2 = chip_scatter(part, name="grad_chip_exchange")
    red = [sum_chips(r, p, place, name=f"grad_chip_sum_{i}") for i, (r, p) in enumerate(zip(recv2, part))]
    full = pair_gather(red, name="grad_pair_gather")
    per_layer = len(big_g[0])
    grads = {}

    def layers(k, f):
        return jnp.stack([f(full[l * per_layer + k]) for l in range(n_layer)])

    grads["ff1_wg"] = layers(0, lambda t: t[0])
    grads["ff1_wu"] = layers(0, lambda t: t[1])
    grads["ff1_wd"] = layers(1, lambda t: t.reshape(FF_SHARD, d))
    grads["w_in"] = layers(2, lambda t: t.reshape(d, P_IN_SHARD_PAD)[:, :P_IN_SHARD])
    grads["w_branch"] = layers(3, lambda t: t.reshape(4, BRANCH_W, d // N_SHARD))
    grads["w_out"] = layers(4, lambda t: t.reshape(d // N_SHARD, d))
    grads["ff2_wg"] = layers(5, lambda t: t[0])
    grads["ff2_wu"] = layers(5, lambda t: t[1])
    grads["ff2_wd"] = layers(6, lambda t: t.reshape(FF_SHARD, d))

    small_full = [jnp.stack(small_g[n]) for n in _SMALL if n != "final_norm"] + [d_final[0]]
    summed = _unpack(all_sum_small(_pack(small_full), name="grad_small_allreduce"), [t.shape for t in small_full])
    for n, g in zip(_SMALL, summed):
        if n in _CONV:
            width = a[n].shape[-1]
            g = lax.dynamic_slice_in_dim(g, chip * width, width, axis=2)
        grads[n] = g

    delta, new_m, new_v = {}, {}, {}
    for n in _BIG:
        delta[n], new_m[n], new_v[n] = adamw(a[n], grads[n], a["m_" + n], a["v_" + n], name=f"adamw_{n}")
    shapes = [a[n].shape for n in _SMALL]
    packed = [_pack([src[pre + n] for n in _SMALL]) for src, pre in ((a, ""), (grads, ""), (a, "m_"), (a, "v_"))]
    for store, buf in zip((delta, new_m, new_v), adamw(*packed, name="adamw_small")):
        store.update(zip(_SMALL, _unpack(buf, shapes)))

    return (loss, dx[None], *[grads[n] for n in _WEIGHTS], *[delta[n] for n in _WEIGHTS],
            *[new_m[n] for n in _WEIGHTS], *[new_v[n] for n in _WEIGHTS])
```

```python
import functools
import math

import jax
import jax.numpy as jnp
from jax import lax
from jax.experimental import pallas as pl
from jax.experimental.pallas import tpu as pltpu

F32 = jnp.float32
BF16 = jnp.bfloat16
MESH = pl.DeviceIdType.MESH

D_MODEL = 1024
N_SHARD = 4
D_FF = 2816
FF_SHARD = D_FF // N_SHARD
BRANCH_W = 512
CHUNK = 64
GDN_HEADS = 4
GDN_DK = 128
GDN_CPS = 4
LRU_C = 8.0
EPS = 1e-6
P_IN = 8712
P_IN_SHARD = P_IN // N_SHARD
P_IN_SHARD_PAD = 2304
P_PAD = 9216
C_QKV, C_Z, C_AU, C_AV, C_BX, C_BG, C_DX, C_BA, C_GATE = 0, 1536, 2048, 2560, 3072, 3584, 4096, 4608, 5120

ADAM_LR, ADAM_B1, ADAM_B2, ADAM_EPS, ADAM_WD, ADAM_STEP = 0.001, 0.9, 0.999, 1e-08, 0.01, 10


def _gelu(x):
    return 0.5 * x * (1.0 + jnp.tanh(0.7978845608028654 * (x + 0.044715 * (x * x * x))))


def _gelu_grad(x):
    u = 0.7978845608028654 * (x + 0.044715 * (x * x * x))
    t = jnp.tanh(u)
    return 0.5 * (1.0 + t) + 0.5 * x * (1.0 - t * t) * 0.7978845608028654 * (1.0 + 3.0 * 0.044715 * x * x)


def _silu(x):
    return x * jax.nn.sigmoid(x)


def _softplus(x):
    return jnp.maximum(x, 0.0) + jnp.log1p(jnp.exp(-jnp.abs(x)))


def _dg(a, b, ca, cb):
    return lax.dot_general(a.astype(BF16), b.astype(BF16), (((ca,), (cb,)), ((), ())),
                           preferred_element_type=F32)


def _dg3(a, b, ca, cb):
    a1 = a.astype(BF16)
    a2 = (a - a1.astype(F32)).astype(BF16)
    b1 = b.astype(BF16)
    b2 = (b - b1.astype(F32)).astype(BF16)
    dn = (((ca,), (cb,)), ((), ()))
    d = functools.partial(lax.dot_general, dimension_numbers=dn, preferred_element_type=F32)
    return d(a1, b1) + (d(a1, b2) + d(a2, b1))


def _make_mm(ca, cb, dot):
    @jax.custom_vjp
    def f(a, b):
        return dot(a, b, ca, cb)

    def fwd(a, b):
        return dot(a, b, ca, cb), (a, b)

    def bwd(res, g):
        a, b = res
        if (ca, cb) == (1, 0):
            return dot(g, b, 1, 1), dot(a, g, 0, 0)
        if (ca, cb) == (1, 1):
            return dot(g, b, 1, 0), dot(g, a, 0, 0)
        return dot(b, g, 1, 1), dot(a, g, 1, 0)

    f.defvjp(fwd, bwd)
    return f


mm_nn = _make_mm(1, 0, _dg)


def _bdg(a, b, ca, cb):
    return lax.dot_general(a.astype(BF16), b.astype(BF16), (((ca,), (cb,)), ((0,), (0,))),
                           preferred_element_type=F32)


def _split2(x):
    x1 = x.astype(BF16)
    return x1, (x - x1.astype(F32)).astype(BF16)


def _bdg3(a, b, ca, cb):
    a1, a2 = _split2(a)
    b1, b2 = _split2(b)
    d = functools.partial(lax.dot_general, dimension_numbers=(((ca,), (cb,)), ((0,), (0,))),
                          preferred_element_type=F32)
    return d(a1, b1) + (d(a1, b2) + d(a2, b1))


def _make_bmm(ca, cb):
    @jax.custom_vjp
    def f(a, b):
        return _bdg(a, b, ca, cb)

    def fwd(a, b):
        return _bdg(a, b, ca, cb), (a, b)

    def bwd(res, g):
        a, b = res
        if (ca, cb) == (2, 1):
            return _bdg(g, b, 2, 2), _bdg(a, g, 1, 1)
        if (ca, cb) == (2, 2):
            return _bdg(g, b, 2, 1), _bdg(g, a, 1, 1)
        return _bdg(b, g, 2, 2), _bdg(a, g, 2, 1)

    f.defvjp(fwd, bwd)
    return f


bmm_nn = _make_bmm(2, 1)
bmm_nt = _make_bmm(2, 2)
bmm_tn = _make_bmm(1, 1)


def _iota3(shape, dim):
    return lax.broadcasted_iota(jnp.int32, shape, dim)


@jax.custom_vjp
def _tri_inv(aw):
    b, n, _ = aw.shape
    shp = (b, n, 2 * n)
    left = _iota3(shp, 2) < n
    x = jnp.where(_iota3(shp, 1) == _iota3(shp, 2), 1.0, 0.0) - aw
    zero = jnp.zeros((b, n, 4 * n), BF16)
    k = 1
    while k < n:
        x1, x2 = _split2(x)
        lhs = jnp.concatenate([x1, x2], axis=1)
        rhs = jnp.concatenate([zero, jnp.concatenate([x1, x2], axis=2)], axis=1)
        r = lax.dot_general(lhs, rhs, (((2,), (1,)), ((0,), (0,))), preferred_element_type=F32)
        mx = (r[:, :n, :2 * n] + r[:, :n, 2 * n:]) + (r[:, n:, :2 * n] + r[:, n:, 2 * n:])
        x = jnp.where(left, x, 0.0) + mx
        k *= 2
    return jnp.where(left, x, 0.0)


def _tri_inv_fwd(aw):
    t = _tri_inv(aw)
    return t, t


def _tri_inv_bwd(t, dt):
    n = t.shape[1]
    x = _bdg3(t, dt, 1, 1)[:, :n]
    tp = jnp.concatenate([jnp.zeros_like(t), t], axis=1)
    return (-_bdg3(x, tp, 2, 2),)


_tri_inv.defvjp(_tri_inv_fwd, _tri_inv_bwd)


def _rows(shape):
    return lax.broadcasted_iota(jnp.int32, shape, 0)


def _lanes(shape):
    return lax.broadcasted_iota(jnp.int32, shape, 1)


def _col(x, j):
    return jnp.sum(jnp.where(_lanes(x.shape) == j, x, 0.0), axis=1, keepdims=True)


def _put_col(col, j, width):
    shape = (col.shape[0], width)
    return jnp.where(_lanes(shape) == j, jnp.broadcast_to(col, shape), 0.0)


def _params(*sem):
    return pltpu.CompilerParams(dimension_semantics=sem)


def matmul(a, b, *, ta=False, tb=False, tm, tn, tk, out_dtype=F32, col_groups=1, name):
    ba, bb = a.shape[:-2], b.shape[:-2]
    batch = ba if len(ba) >= len(bb) else bb
    assert ba in ((), batch) and bb in ((), batch)
    nb = len(batch)
    (m, k) = (a.shape[-1], a.shape[-2]) if ta else (a.shape[-2], a.shape[-1])
    (k2, n) = (b.shape[-1], b.shape[-2]) if tb else (b.shape[-2], b.shape[-1])
    tm, tn, tk = min(tm, m), min(tn, n), min(tk, k)
    assert k == k2 and m % tm == 0 and n % tn == 0 and k % tk == 0, (a.shape, b.shape, tm, tn, tk)
    nk = k // tk
    grid = batch + (m // tm, n // tn, nk)

    def a_map(*g):
        i, kk = g[nb], g[nb + 2]
        return (g[:nb] if ba else ()) + ((kk, i) if ta else (i, kk))

    def b_map(*g):
        j, kk = g[nb + 1], g[nb + 2]
        return (g[:nb] if bb else ()) + ((j, kk) if tb else (kk, j))

    per_group = n // tn // col_groups
    assert col_groups == 1 or (nb == 0 and per_group * col_groups * tn == n)

    def o_map(*g):
        if col_groups > 1:
            return (g[1] // per_group, g[0], g[1] % per_group)
        return g[:nb] + (g[nb], g[nb + 1])

    a_blk = (None,) * len(ba) + ((tk, tm) if ta else (tm, tk))
    b_blk = (None,) * len(bb) + ((tn, tk) if tb else (tk, tn))
    o_blk = (None,) * (nb + (col_groups > 1)) + (tm, tn)
    o_shape = (col_groups, m, n // col_groups) if col_groups > 1 else batch + (m, n)

    def body(a_ref, b_ref, o_ref, acc_ref):
        kk = pl.program_id(nb + 2)

        @pl.when(kk == 0)
        def _():
            acc_ref[...] = jnp.zeros_like(acc_ref)

        acc_ref[...] += _dg(a_ref[...], b_ref[...], 0 if ta else 1, 1 if tb else 0)

        @pl.when(kk == nk - 1)
        def _():
            o_ref[...] = acc_ref[...].astype(o_ref.dtype)

    return pl.pallas_call(
        body, name=name, grid=grid,
        in_specs=[pl.BlockSpec(a_blk, a_map), pl.BlockSpec(b_blk, b_map)],
        out_specs=pl.BlockSpec(o_blk, o_map),
        out_shape=jax.ShapeDtypeStruct(o_shape, out_dtype),
        scratch_shapes=[pltpu.VMEM((tm, tn), F32)],
        compiler_params=_params(*(("parallel",) * (nb + 2) + ("arbitrary",))),
    )(a, b)


def norm_fwd(x, g, *, tb, name):
    s, d = x.shape

    def body(x_ref, g_ref, h_ref):
        xv = x_ref[...]
        r = lax.rsqrt(jnp.mean(xv * xv, axis=-1, keepdims=True) + EPS)
        h_ref[...] = (xv * r * g_ref[...]).astype(BF16)

    return pl.pallas_call(
        body, name=name, grid=(s // tb,),
        in_specs=[pl.BlockSpec((tb, d), lambda i: (i, 0)), pl.BlockSpec((1, d), lambda i: (0, 0))],
        out_specs=pl.BlockSpec((tb, d), lambda i: (i, 0)),
        out_shape=jax.ShapeDtypeStruct((s, d), BF16),
        compiler_params=_params("parallel"),
    )(x, g)


def _rms_bwd(xv, gv, dh):
    r = lax.rsqrt(jnp.mean(xv * xv, axis=-1, keepdims=True) + EPS)
    xh = xv * r
    dxh = dh * gv
    dx = r * (dxh - xh * jnp.mean(dxh * xh, axis=-1, keepdims=True))
    return dx, jnp.sum(dh * xh, axis=0, keepdims=True)


def norm_bwd(x, g, dh, dres, *, tb, name):
    s, d = x.shape

    def body(x_ref, g_ref, dh_ref, dres_ref, dx_ref, dg_ref):
        dx, dg = _rms_bwd(x_ref[...], g_ref[...], dh_ref[...])
        dx_ref[...] = dres_ref[...] + dx

        @pl.when(pl.program_id(0) == 0)
        def _():
            dg_ref[...] = jnp.zeros_like(dg_ref)

        dg_ref[...] += dg

    row = pl.BlockSpec((tb, d), lambda i: (i, 0))
    one = pl.BlockSpec((1, d), lambda i: (0, 0))
    return pl.pallas_call(
        body, name=name, grid=(s // tb,),
        in_specs=[row, one, row, row], out_specs=[row, one],
        out_shape=[jax.ShapeDtypeStruct((s, d), F32), jax.ShapeDtypeStruct((1, d), F32)],
        compiler_params=_params("arbitrary"),
    )(x, g, dh, dres)


def ffn_fwd(x, g, wg, wu, wd, layer, *, tb, name):
    s, d = x.shape
    fc = wd.shape[-2]

    def body(x_ref, g_ref, wg_ref, wu_ref, wd_ref, o_ref, h_scr, acc_scr):
        j = pl.program_id(1)

        @pl.when(j == 0)
        def _():
            xv = x_ref[...]
            r = lax.rsqrt(jnp.mean(xv * xv, axis=-1, keepdims=True) + EPS)
            h_scr[...] = (xv * r * g_ref[...]).astype(BF16)
            acc_scr[...] = jnp.zeros_like(acc_scr)

        h = h_scr[...]
        a = _dg(h, wg_ref[...], 1, 0)
        b = _dg(h, wu_ref[...], 1, 0)
        acc_scr[...] += _dg(_silu(a) * b, wd_ref[...], 1, 0)

        @pl.when(j == N_SHARD - 1)
        def _():
            o_ref[...] = x_ref[...] + 0.5 * acc_scr[...]

    row = pl.BlockSpec((tb, d), lambda i, j: (i, 0))
    w_in = pl.BlockSpec((None, None, d, fc), lambda i, j: (j, layer, 0, 0))
    return pl.pallas_call(
        body, name=name, grid=(s // tb, N_SHARD),
        in_specs=[row, pl.BlockSpec((1, d), lambda i, j: (0, 0)), w_in, w_in,
                  pl.BlockSpec((None, None, fc, d), lambda i, j: (j, layer, 0, 0))],
        out_specs=row,
        out_shape=jax.ShapeDtypeStruct((s, d), F32),
        scratch_shapes=[pltpu.VMEM((tb, d), BF16), pltpu.VMEM((tb, d), F32)],
        compiler_params=_params("parallel", "arbitrary"),
    )(x, g, wg, wu, wd)


def ffn_bwd(x, dxo, g, wg, wu, wd, layer, *, tb, name):
    s, d = x.shape
    fc = wd.shape[-2]

    def body(x_ref, dxo_ref, g_ref, wg_ref, wu_ref, wd_ref,
             dx_ref, dg_ref, dab_ref, hid_ref, h_ref, dy_ref, dh_scr):
        i, j = pl.program_id(0), pl.program_id(1)

        @pl.when(j == 0)
        def _():
            xv = x_ref[...]
            r = lax.rsqrt(jnp.mean(xv * xv, axis=-1, keepdims=True) + EPS)
            h_ref[...] = (xv * r * g_ref[...]).astype(BF16)
            dy_ref[...] = (0.5 * dxo_ref[...]).astype(BF16)
            dh_scr[...] = jnp.zeros_like(dh_scr)

        h = h_ref[...]
        a = _dg(h, wg_ref[...], 1, 0)
        b = _dg(h, wu_ref[...], 1, 0)
        sg = jax.nn.sigmoid(a)
        sl = a * sg
        dhid = _dg(dy_ref[...], wd_ref[...], 1, 1)
        da = (dhid * b * (sg * (1.0 + a * (1.0 - sg)))).astype(BF16)
        db = (dhid * sl).astype(BF16)
        dab_ref[0] = da
        dab_ref[1] = db
        hid_ref[...] = (sl * b).astype(BF16)
        dh_scr[...] += _dg(da, wg_ref[...], 1, 1) + _dg(db, wu_ref[...], 1, 1)

        @pl.when(jnp.logical_and(i == 0, j == 0))
        def _():
            dg_ref[...] = jnp.zeros_like(dg_ref)

        @pl.when(j == N_SHARD - 1)
        def _():
            dx, dg = _rms_bwd(x_ref[...], g_ref[...], dh_scr[...])
            dx_ref[...] = dxo_ref[...] + dx
            dg_ref[...] += dg

    row = pl.BlockSpec((tb, d), lambda i, j: (i, 0))
    one = pl.BlockSpec((1, d), lambda i, j: (0, 0))
    w_in = pl.BlockSpec((None, None, d, fc), lambda i, j: (j, layer, 0, 0))
    return pl.pallas_call(
        body, name=name, grid=(s // tb, N_SHARD),
        in_specs=[row, row, one, w_in, w_in,
                  pl.BlockSpec((None, None, fc, d), lambda i, j: (j, layer, 0, 0))],
        out_specs=[row, one,
                   pl.BlockSpec((None, 2, tb, fc), lambda i, j: (j, 0, i, 0)),
                   pl.BlockSpec((None, tb, fc), lambda i, j: (j, i, 0)),
                   row, row],
        out_shape=[jax.ShapeDtypeStruct((s, d), F32), jax.ShapeDtypeStruct((1, d), F32),
                   jax.ShapeDtypeStruct((N_SHARD, 2, s, fc), BF16),
                   jax.ShapeDtypeStruct((N_SHARD, s, fc), BF16),
                   jax.ShapeDtypeStruct((s, d), BF16), jax.ShapeDtypeStruct((s, d), BF16)],
        scratch_shapes=[pltpu.VMEM((tb, d), F32)],
        compiler_params=_params("arbitrary", "arbitrary"),
    )(x, dxo, g, wg, wu, wd)


def _pblk(tb, width, col0, rev_n=None):
    assert col0 % width == 0
    cb = col0 // width
    if rev_n is None:
        return pl.BlockSpec((tb, width), lambda i: (i, cb))
    return pl.BlockSpec((tb, width), lambda i: (rev_n - 1 - i, cb))


def _full(shape):
    return pl.BlockSpec(shape, lambda i: (0,) * len(shape))


def conv_fwd(src, col0, width, w, bias, *, tb, name):
    s = src.shape[0]
    cb = col0 // width

    def body(x_ref, p_ref, w_ref, b_ref, y_ref):
        i = pl.program_id(0)
        prev = jnp.where(i > 0, p_ref[...], 0.0)
        xe = jnp.concatenate([prev, x_ref[...]], axis=0)
        wv = w_ref[...]
        acc = b_ref[...] + wv[3:4] * xe[8:]
        for k in range(3):
            acc = acc + wv[k:k + 1] * pltpu.roll(xe, 3 - k, 0)[8:]
        y_ref[...] = acc

    return pl.pallas_call(
        body, name=name, grid=(s // tb,),
        in_specs=[_pblk(tb, width, col0),
                  pl.BlockSpec((8, width), lambda i: (jnp.maximum(i * (tb // 8) - 1, 0), cb)),
                  _full((4, width)), _full((1, width))],
        out_specs=pl.BlockSpec((tb, width), lambda i: (i, 0)),
        out_shape=jax.ShapeDtypeStruct((s, width), F32),
        compiler_params=_params("parallel"),
    )(src, src, w, bias)


def conv_bwd(src, col0, width, dpre, w, *, tb, name):
    s = src.shape[0]
    cb = col0 // width
    nt = s // tb

    def body(x_ref, p_ref, d_ref, n_ref, w_ref, dx_ref, dwb_ref):
        i = pl.program_id(0)
        d = d_ref[...]
        nxt = jnp.where(i < nt - 1, n_ref[...], 0.0)
        de = jnp.concatenate([d, nxt], axis=0)
        wv = w_ref[...]
        dx = wv[3:4] * d
        for k in range(3):
            dx = dx + wv[k:k + 1] * pltpu.roll(de, tb + 8 - (3 - k), 0)[:tb]
        dx_ref[...] = dx.astype(BF16)

        @pl.when(i == 0)
        def _():
            dwb_ref[...] = jnp.zeros_like(dwb_ref)

        prev = jnp.where(i > 0, p_ref[...], 0.0)
        xe = jnp.concatenate([prev, x_ref[...]], axis=0)
        dwb_ref[3:4, :] += jnp.sum(d * xe[8:], axis=0, keepdims=True)
        for k in range(3):
            dwb_ref[k:k + 1, :] += jnp.sum(d * pltpu.roll(xe, 3 - k, 0)[8:], axis=0, keepdims=True)
        dwb_ref[4:5, :] += jnp.sum(d, axis=0, keepdims=True)

    return pl.pallas_call(
        body, name=name, grid=(nt,),
        in_specs=[_pblk(tb, width, col0),
                  pl.BlockSpec((8, width), lambda i: (jnp.maximum(i * (tb // 8) - 1, 0), cb)),
                  pl.BlockSpec((tb, width), lambda i: (i, 0)),
                  pl.BlockSpec((8, width), lambda i: (jnp.minimum((i + 1) * (tb // 8), s // 8 - 1), 0)),
                  _full((4, width))],
        out_specs=[pl.BlockSpec((tb, width), lambda i: (i, 0)), _full((8, width))],
        out_shape=[jax.ShapeDtypeStruct((s, width), BF16), jax.ShapeDtypeStruct((8, width), F32)],
        compiler_params=_params("arbitrary"),
    )(src, src, dpre, dpre, w)


SGU_BLOCK = 128


def _sgu_pre(u, v, lg, lb):
    ug = _gelu(u)
    vg = _gelu(v)
    mu = jnp.mean(vg, axis=-1, keepdims=True)
    var = jnp.mean(jnp.square(vg - mu), axis=-1, keepdims=True)
    return ug, (vg - mu) * lax.rsqrt(var + EPS) * lg + lb


def _sgu_mask():
    shp = (SGU_BLOCK, SGU_BLOCK)
    return (_rows(shp) // CHUNK) >= (_lanes(shp) // CHUNK)


def sgu_fwd(proj, lg, lb, ws, bt, *, tb, name):
    s = proj.shape[0]
    w = BRANCH_W

    def body(u_ref, v_ref, lg_ref, lb_ref, w_ref, b_ref, y_ref):
        ug, vn = _sgu_pre(u_ref[...], v_ref[...], lg_ref[...], lb_ref[...])
        mask = _sgu_mask()
        for g in range(4):
            wm = jnp.where(mask, w_ref[g], 0.0)
            bcol = _col(b_ref[...], g)
            cs = slice(128 * g, 128 * g + 128)
            for n in range(tb // SGU_BLOCK):
                rs = slice(SGU_BLOCK * n, SGU_BLOCK * (n + 1))
                mixed = _dg(wm, vn[rs, cs], 1, 0) + bcol
                y_ref[rs, cs] = (ug[rs, cs] * mixed).astype(BF16)

    return pl.pallas_call(
        body, name=name, grid=(s // tb,),
        in_specs=[_pblk(tb, w, C_AU), _pblk(tb, w, C_AV), _full((1, w)), _full((1, w)),
                  _full((4, 128, 128)), _full((128, 128))],
        out_specs=pl.BlockSpec((tb, w), lambda i: (i, 0)),
        out_shape=jax.ShapeDtypeStruct((s, w), BF16),
        compiler_params=_params("parallel"),
    )(proj, proj, lg, lb, ws, bt)


def sgu_bwd(proj, dy, lg, lb, ws, bt, *, tb, name):
    s = proj.shape[0]
    w = BRANCH_W

    def body(u_ref, v_ref, dy_ref, lg_ref, lb_ref, w_ref, b_ref,
             du_ref, dv_ref, dlg_ref, dlb_ref, dw_ref, db_ref, dug_scr, dvn_scr):
        @pl.when(pl.program_id(0) == 0)
        def _():
            dlg_ref[...] = jnp.zeros_like(dlg_ref)
            dlb_ref[...] = jnp.zeros_like(dlb_ref)
            dw_ref[...] = jnp.zeros_like(dw_ref)
            db_ref[...] = jnp.zeros_like(db_ref)

        (ug, vn), vf = jax.vjp(_sgu_pre, u_ref[...], v_ref[...], lg_ref[...], lb_ref[...])
        dyv = dy_ref[...]
        mask = _sgu_mask()
        for g in range(4):
            wm = jnp.where(mask, w_ref[g], 0.0)
            bcol = _col(b_ref[...], g)
            cs = slice(128 * g, 128 * g + 128)
            dwg = jnp.zeros((SGU_BLOCK, SGU_BLOCK), F32)
            dbc = jnp.zeros((SGU_BLOCK, 1), F32)
            for n in range(tb // SGU_BLOCK):
                rs = slice(SGU_BLOCK * n, SGU_BLOCK * (n + 1))
                blk = vn[rs, cs]
                mixed = _dg(wm, blk, 1, 0) + bcol
                dmix = dyv[rs, cs] * ug[rs, cs]
                dug_scr[rs, cs] = dyv[rs, cs] * mixed
                dvn_scr[rs, cs] = _dg(wm, dmix, 0, 0)
                dwg = dwg + _dg(dmix, blk, 1, 1)
                dbc = dbc + jnp.sum(dmix, axis=1, keepdims=True)
            dw_ref[g] += jnp.where(mask, dwg, 0.0)
            db_ref[...] += _put_col(dbc, g, 128)
        du, dv, dlg, dlb = vf((dug_scr[...], dvn_scr[...]))
        du_ref[...] = du.astype(BF16)
        dv_ref[...] = dv.astype(BF16)
        dlg_ref[...] += dlg
        dlb_ref[...] += dlb

    row = pl.BlockSpec((tb, w), lambda i: (i, 0))
    return pl.pallas_call(
        body, name=name, grid=(s // tb,),
        in_specs=[_pblk(tb, w, C_AU), _pblk(tb, w, C_AV), row, _full((1, w)), _full((1, w)),
                  _full((4, 128, 128)), _full((128, 128))],
        out_specs=[row, row, _full((1, w)), _full((1, w)), _full((4, 128, 128)), _full((128, 128))],
        out_shape=[jax.ShapeDtypeStruct((s, w), BF16), jax.ShapeDtypeStruct((s, w), BF16),
                   jax.ShapeDtypeStruct((1, w), F32), jax.ShapeDtypeStruct((1, w), F32),
                   jax.ShapeDtypeStruct((4, 128, 128), F32), jax.ShapeDtypeStruct((128, 128), F32)],
        scratch_shapes=[pltpu.VMEM((tb, w), F32), pltpu.VMEM((tb, w), F32)],
        compiler_params=_params("arbitrary"),
    )(proj, proj, dy, lg, lb, ws, bt)


POOL_HALO = 16


def _pooled(xe, t0, tb):
    s2 = xe + pltpu.roll(xe, 1, 0)
    s4 = s2 + pltpu.roll(s2, 2, 0)
    s8 = s4 + pltpu.roll(s4, 4, 0)
    s16 = s8 + pltpu.roll(s8, 8, 0)
    t = (t0 + _rows((tb, 128))).astype(F32)
    outs = []
    for g, sw in enumerate((s2, s4, s8, s16)):
        cs = slice(128 * g, 128 * g + 128)
        cnt = jnp.minimum(t + 1.0, float(2 ** (g + 1)))
        outs.append(sw[POOL_HALO:, cs] / cnt - xe[POOL_HALO:, cs])
    return outs


def pool_fwd(proj, wp, scale, *, tb, name):
    s = proj.shape[0]
    w = BRANCH_W

    def body(x_ref, p_ref, w_ref, sc_ref, y_ref):
        i = pl.program_id(0)
        prev = jnp.where(i > 0, p_ref[...], 0.0)
        xe = jnp.concatenate([prev, x_ref[...]], axis=0)
        pooled = _pooled(xe, i * tb, tb)
        sc = sc_ref[...]
        for g in range(4):
            cs = slice(128 * g, 128 * g + 128)
            y_ref[:, cs] = (_dg(pooled[g], w_ref[g], 1, 0) * sc[:, cs]).astype(BF16)

    cb = C_DX // w
    return pl.pallas_call(
        body, name=name, grid=(s // tb,),
        in_specs=[_pblk(tb, w, C_DX),
                  pl.BlockSpec((POOL_HALO, w), lambda i: (jnp.maximum(i * (tb // POOL_HALO) - 1, 0), cb)),
                  _full((4, 128, 128)), _full((1, w))],
        out_specs=pl.BlockSpec((tb, w), lambda i: (i, 0)),
        out_shape=jax.ShapeDtypeStruct((s, w), BF16),
        compiler_params=_params("parallel"),
    )(proj, proj, wp, scale)


def pool_bwd(proj, dy, wp, scale, *, tb, name):
    s = proj.shape[0]
    w = BRANCH_W
    nt = s // tb
    te = tb + POOL_HALO

    def body(x_ref, p_ref, dy_ref, n_ref, w_ref, sc_ref, dx_ref, dw_ref, dsc_ref):
        i = pl.program_id(0)

        @pl.when(i == 0)
        def _():
            dw_ref[...] = jnp.zeros_like(dw_ref)
            dsc_ref[...] = jnp.zeros_like(dsc_ref)

        prev = jnp.where(i > 0, p_ref[...], 0.0)
        xe = jnp.concatenate([prev, x_ref[...]], axis=0)
        pooled = _pooled(xe, i * tb, tb)
        dyv = dy_ref[...]
        nxt = jnp.where(i < nt - 1, n_ref[...], 0.0)
        dye = jnp.concatenate([dyv, nxt], axis=0)
        sc = sc_ref[...]
        t = (i * tb + _rows((te, 128))).astype(F32)
        for g in range(4):
            cs = slice(128 * g, 128 * g + 128)
            win = 2 ** (g + 1)
            dpm = dye[:, cs] * sc[:, cs]
            dpool = _dg(dpm, w_ref[g], 1, 1)
            q = dpool / jnp.minimum(t + 1.0, float(win))
            r, sh = q, 1
            while sh < win:
                r = r + pltpu.roll(r, te - sh, 0)
                sh *= 2
            dx_ref[:, cs] = (r[:tb] - dpool[:tb]).astype(BF16)
            dw_ref[g] += _dg(pooled[g], dpm[:tb], 0, 0)
            dsc_ref[:, cs] += jnp.sum(dyv[:, cs] * _dg(pooled[g], w_ref[g], 1, 0), axis=0, keepdims=True)

    cb = C_DX // w
    row = pl.BlockSpec((tb, w), lambda i: (i, 0))
    return pl.pallas_call(
        body, name=name, grid=(nt,),
        in_specs=[_pblk(tb, w, C_DX),
                  pl.BlockSpec((POOL_HALO, w), lambda i: (jnp.maximum(i * (tb // POOL_HALO) - 1, 0), cb)),
                  row,
                  pl.BlockSpec((POOL_HALO, w),
                               lambda i: (jnp.minimum((i + 1) * (tb // POOL_HALO), s // POOL_HALO - 1), 0)),
                  _full((4, 128, 128)), _full((1, w))],
        out_specs=[row, _full((4, 128, 128)), _full((1, w))],
        out_shape=[jax.ShapeDtypeStruct((s, w), BF16), jax.ShapeDtypeStruct((4, 128, 128), F32),
                   jax.ShapeDtypeStruct((1, w), F32)],
        compiler_params=_params("arbitrary"),
    )(proj, proj, dy, dy, wp, scale)


def _neg_expm1(z):
    p = 1.0 + z * (1.0 / 9.0)
    for n in (8.0, 7.0, 6.0, 5.0, 4.0, 3.0, 2.0):
        p = 1.0 + z * (1.0 / n) * p
    return jnp.where(z > -0.5, -z * p, 1.0 - jnp.exp(z))


def _lru_gates(xc, wa, wx, ba, bx, lam):
    r = jax.nn.sigmoid(mm_nn(xc, wa) + ba)
    i = jax.nn.sigmoid(mm_nn(xc, wx) + bx)
    log_a = -LRU_C * r * _softplus(-lam)
    a = jnp.exp(log_a)
    return a, jnp.sqrt(_neg_expm1(2.0 * log_a)) * (i * xc)


def _scan_down(a, b):
    n = a.shape[0]
    row = _rows(a.shape)
    sh = 1
    while sh < n:
        keep = row >= sh
        a_sh = jnp.where(keep, pltpu.roll(a, sh, 0), 1.0)
        b_sh = jnp.where(keep, pltpu.roll(b, sh, 0), 0.0)
        b = a * b_sh + b
        a = a * a_sh
        sh *= 2
    return a, b


def _scan_up(a, b):
    n = a.shape[0]
    row = _rows(a.shape)
    sh = 1
    while sh < n:
        keep = row < n - sh
        a_sh = jnp.where(keep, pltpu.roll(a, n - sh, 0), 1.0)
        b_sh = jnp.where(keep, pltpu.roll(b, n - sh, 0), 0.0)
        b = a * b_sh + b
        a = a * a_sh
        sh *= 2
    return a, b


def lru_fwd(xc, proj, wa, wx, ba, bx, lam, *, tb, name):
    s = xc.shape[0]
    w = BRANCH_W

    def body(xc_ref, gate_ref, wa_ref, wx_ref, ba_ref, bx_ref, lam_ref, y_ref, h_ref, carry):
        @pl.when(pl.program_id(0) == 0)
        def _():
            carry[...] = jnp.zeros_like(carry)

        a, b = _lru_gates(xc_ref[...], wa_ref[...], wx_ref[...], ba_ref[...], bx_ref[...], lam_ref[...])
        pa, hb = _scan_down(a, b)
        h = pa * carry[...] + hb
        h_ref[...] = h
        carry[...] = h_ref[tb - 1:tb, :]
        y_ref[...] = (h * _gelu(gate_ref[...])).astype(BF16)

    row = pl.BlockSpec((tb, w), lambda i: (i, 0))
    return pl.pallas_call(
        body, name=name, grid=(s // tb,),
        in_specs=[row, _pblk(tb, w, C_BG), _full((w, w)), _full((w, w)),
                  _full((1, w)), _full((1, w)), _full((1, w))],
        out_specs=[row, row],
        out_shape=[jax.ShapeDtypeStruct((s, w), BF16), jax.ShapeDtypeStruct((s, w), F32)],
        scratch_shapes=[pltpu.VMEM((1, w), F32)],
        compiler_params=_params("arbitrary"),
    )(xc, proj, wa, wx, ba, bx, lam)


def lru_bwd(xc, proj, h, dy, wa, wx, ba, bx, lam, *, tb, name):
    s = xc.shape[0]
    w = BRANCH_W
    nt = s // tb

    def body(xc_ref, gate_ref, h_ref, hp_ref, dy_ref, wa_ref, wx_ref, ba_ref, bx_ref, lam_ref,
             dxc_ref, dgate_ref, dwa_ref, dwx_ref, dba_ref, dbx_ref, dlam_ref, carry):
        i = pl.program_id(0)

        @pl.when(i == 0)
        def _():
            carry[...] = jnp.zeros_like(carry)
            for r in (dwa_ref, dwx_ref, dba_ref, dbx_ref, dlam_ref):
                r[...] = jnp.zeros_like(r)

        (a, b), vf = jax.vjp(_lru_gates, xc_ref[...], wa_ref[...], wx_ref[...],
                             ba_ref[...], bx_ref[...], lam_ref[...])
        gate = gate_ref[...]
        dyv = dy_ref[...]
        hv = h_ref[...]
        row = _rows((tb, w))
        a_next = jnp.where(row < tb - 1, pltpu.roll(a, tb - 1, 0), 1.0)
        pa, xb = _scan_up(a_next, dyv * _gelu(gate))
        dh = xb + pa * carry[...]
        carry[...] = a[0:1, :] * dh[0:1, :]
        h_first = jnp.where(i < nt - 1, hp_ref[7:8, :], 0.0)
        h_prev = jnp.where(row >= 1, pltpu.roll(hv, 1, 0), h_first)
        dxc, dwa, dwx, dba, dbx, dlam = vf((dh * h_prev, dh))
        dxc_ref[...] = dxc
        dgate_ref[...] = (dyv * hv * _gelu_grad(gate)).astype(BF16)
        dwa_ref[...] += dwa
        dwx_ref[...] += dwx
        dba_ref[...] += dba
        dbx_ref[...] += dbx
        dlam_ref[...] += dlam

    row_spec = pl.BlockSpec((tb, w), lambda i: (nt - 1 - i, 0))
    hp_spec = pl.BlockSpec((8, w), lambda i: (jnp.maximum((nt - 1 - i) * (tb // 8) - 1, 0), 0))
    return pl.pallas_call(
        body, name=name, grid=(nt,),
        in_specs=[row_spec, _pblk(tb, w, C_BG, rev_n=nt), row_spec, hp_spec, row_spec,
                  _full((w, w)), _full((w, w)), _full((1, w)), _full((1, w)), _full((1, w))],
        out_specs=[row_spec, row_spec, _full((w, w)), _full((w, w)),
                   _full((1, w)), _full((1, w)), _full((1, w))],
        out_shape=[jax.ShapeDtypeStruct((s, w), F32), jax.ShapeDtypeStruct((s, w), BF16),
                   jax.ShapeDtypeStruct((w, w), F32), jax.ShapeDtypeStruct((w, w), F32),
                   jax.ShapeDtypeStruct((1, w), F32), jax.ShapeDtypeStruct((1, w), F32),
                   jax.ShapeDtypeStruct((1, w), F32)],
        scratch_shapes=[pltpu.VMEM((1, w), F32)],
        compiler_params=_params("arbitrary"),
    )(xc, proj, h, h, dy, wa, wx, ba, bx, lam)


def _gdn_g(ba, alog, dt):
    return -jnp.exp(alog) * _softplus(ba + dt)


def _gdn_local(qp, kp, vp, bpre, gc):
    b, c, dk = qp.shape
    q = _silu(qp)
    k = _silu(kp)
    v = _silu(vp)
    q = q * lax.rsqrt(jnp.sum(q * q, axis=-1, keepdims=True) + EPS) * (dk ** -0.5)
    k = k * lax.rsqrt(jnp.sum(k * k, axis=-1, keepdims=True) + EPS)
    beta = jax.nn.sigmoid(bpre)
    sq = (b, c, c)
    ii, jj = _iota3(sq, 1), _iota3(sq, 2)
    gr = jnp.sum(gc * (ii == jj).astype(F32), axis=1, keepdims=True)
    decay = jnp.exp(jnp.where(ii >= jj, gc - gr, -jnp.inf))
    wd = (b, c, 2 * c)
    wi, wj = _iota3(wd, 1), _iota3(wd, 2) - c
    gr_w = jnp.sum(gc * (wi == wj).astype(F32), axis=1, keepdims=True)
    decay_w = jnp.exp(jnp.where(jnp.logical_and(wj >= 0, wi > wj), gc - gr_w, -jnp.inf))
    kb = k * beta
    pad = jnp.zeros((b, c, dk), F32)
    t = _tri_inv(bmm_nt(kb, jnp.concatenate([pad, k], axis=1)) * decay_w)
    u = bmm_nn(t, jnp.concatenate([v * beta, pad], axis=1))
    wk = bmm_nn(t, jnp.concatenate([kb * jnp.exp(gc), pad], axis=1))
    attn = bmm_nt(q, k) * decay
    g_last = jnp.sum(jnp.where(_iota3((b, c, 1), 1) == c - 1, gc, 0.0), axis=1, keepdims=True)
    return u, wk, attn, q * jnp.exp(gc), k * jnp.exp(g_last - gc), jnp.exp(g_last)


def _gdn_recur(s, u, wk, attn, qg, kd, eg, zz, ng):
    v_new = u - bmm_nn(wk, s)
    o = bmm_nn(qg, s) + bmm_nn(attn, v_new)
    s_out = s * eg + bmm_tn(kd, v_new)
    o = o * lax.rsqrt(jnp.mean(o * o, axis=-1, keepdims=True) + EPS) * ng
    return o * _silu(zz), s_out


def _gdn_inputs(qkv_ref, ba, gcums, cps):
    c, hd = CHUNK, GDN_DK
    qs, ks, vs, bs, gs = [], [], [], [], []
    for ci in range(cps):
        rs = slice(c * ci, c * (ci + 1))
        for h in range(GDN_HEADS):
            qs.append(qkv_ref[rs, hd * h:hd * (h + 1)])
            ks.append(qkv_ref[rs, 512 + hd * h:512 + hd * (h + 1)])
            vs.append(qkv_ref[rs, 1024 + hd * h:1024 + hd * (h + 1)])
            bs.append(_col(ba[rs], h))
            gs.append(_col(gcums[ci], 4 + h))
    return tuple(jnp.stack(t) for t in (qs, ks, vs, bs, gs))


def _chunk_cumsum(g, transpose=False):
    c = CHUNK
    tri = (_rows((c, c)) >= _lanes((c, c))).astype(F32)
    return _dg3(tri, g, 0 if transpose else 1, 0)


def gdn_fwd(qkv, proj, alog, dt, ng, *, cps, name):
    s = qkv.shape[0]
    c = CHUNK
    nc = s // c
    hd = GDN_DK
    tb = c * cps

    def body(qkv_ref, z_ref, ba_ref, al_ref, dt_ref, ng_ref, y_ref, sall_ref, s_scr):
        @pl.when(pl.program_id(0) == 0)
        def _():
            s_scr[...] = jnp.zeros_like(s_scr)

        nh = GDN_HEADS
        ba = ba_ref[...]
        g = _gdn_g(ba, al_ref[...], dt_ref[...])
        gcums = [_chunk_cumsum(g[c * ci:c * (ci + 1)]) for ci in range(cps)]
        loc = _gdn_local(*_gdn_inputs(qkv_ref, ba, gcums, cps))
        for ci in range(cps):
            rs = slice(c * ci, c * (ci + 1))
            sl = slice(nh * ci, nh * (ci + 1))
            s_in = s_scr[...]
            sall_ref[ci] = s_in
            zz = jnp.stack([z_ref[rs, hd * h:hd * (h + 1)] for h in range(nh)])
            y, s_out = _gdn_recur(s_in, *(t[sl] for t in loc), zz, ng_ref[...])
            s_scr[...] = s_out
            for h in range(nh):
                y_ref[rs, hd * h:hd * (h + 1)] = y[h].astype(BF16)

    return pl.pallas_call(
        body, name=name, grid=(nc // cps,),
        in_specs=[pl.BlockSpec((tb, 1536), lambda i: (i, 0)), _pblk(tb, 512, C_Z), _pblk(tb, 128, C_BA),
                  _full((1, 128)), _full((1, 128)), _full((1, 128))],
        out_specs=[pl.BlockSpec((tb, 512), lambda i: (i, 0)),
                   pl.BlockSpec((cps, GDN_HEADS, hd, hd), lambda i: (i, 0, 0, 0))],
        out_shape=[jax.ShapeDtypeStruct((s, 512), BF16), jax.ShapeDtypeStruct((nc, GDN_HEADS, hd, hd), F32)],
        scratch_shapes=[pltpu.VMEM((GDN_HEADS, hd, hd), F32)],
        compiler_params=_params("arbitrary"),
    )(qkv, proj, proj, alog, dt, ng)


def gdn_bwd(qkv, proj, sall, dy, alog, dt, ng, *, cps, name):
    s = qkv.shape[0]
    c = CHUNK
    nc = s // c
    hd = GDN_DK
    tb = c * cps
    nt = nc // cps

    def body(qkv_ref, z_ref, ba_ref, sall_ref, dy_ref, al_ref, dt_ref, ng_ref,
             dqkv_ref, dz_ref, dba_ref, dal_ref, ddt_ref, dng_ref, ds_scr):
        @pl.when(pl.program_id(0) == 0)
        def _():
            ds_scr[...] = jnp.zeros_like(ds_scr)
            dal_ref[...] = jnp.zeros_like(dal_ref)
            ddt_ref[...] = jnp.zeros_like(ddt_ref)
            dng_ref[...] = jnp.zeros_like(dng_ref)

        nh = GDN_HEADS
        ba = ba_ref[...]
        g, gvf = jax.vjp(_gdn_g, ba, al_ref[...], dt_ref[...])
        gcums = [_chunk_cumsum(g[c * ci:c * (ci + 1)]) for ci in range(cps)]
        loc, loc_vf = jax.vjp(_gdn_local, *_gdn_inputs(qkv_ref, ba, gcums, cps))
        d_loc = [None] * cps
        dng = jnp.zeros((1, 128), F32)
        for ci in reversed(range(cps)):
            rs = slice(c * ci, c * (ci + 1))
            sl = slice(nh * ci, nh * (ci + 1))
            zz = jnp.stack([z_ref[rs, hd * h:hd * (h + 1)] for h in range(nh)])
            dyy = jnp.stack([dy_ref[rs, hd * h:hd * (h + 1)] for h in range(nh)])
            _, rvf = jax.vjp(_gdn_recur, sall_ref[ci], *(t[sl] for t in loc), zz, ng_ref[...])
            ds, du, dwk, dattn, dqg, dkd, deg, dzz, dn = rvf((dyy, ds_scr[...]))
            ds_scr[...] = ds
            d_loc[ci] = (du, dwk, dattn, dqg, dkd, deg)
            dng = dng + dn
            for h in range(nh):
                dz_ref[rs, hd * h:hd * (h + 1)] = dzz[h].astype(BF16)
        dq, dk, dv, dbp, dgc = loc_vf(tuple(jnp.concatenate([d_loc[ci][j] for ci in range(cps)], axis=0)
                                            for j in range(6)))
        dgs, dbas = [], []
        for ci in range(cps):
            rs = slice(c * ci, c * (ci + 1))
            dgcum = jnp.zeros((c, 128), F32)
            dba = jnp.zeros((c, 128), F32)
            for h in range(nh):
                b = nh * ci + h
                dqkv_ref[rs, hd * h:hd * (h + 1)] = dq[b]
                dqkv_ref[rs, 512 + hd * h:512 + hd * (h + 1)] = dk[b]
                dqkv_ref[rs, 1024 + hd * h:1024 + hd * (h + 1)] = dv[b]
                dgcum = dgcum + _put_col(dgc[b], 4 + h, 128)
                dba = dba + _put_col(dbp[b], h, 128)
            dgs.append(_chunk_cumsum(dgcum, transpose=True))
            dbas.append(dba)
        dba2, dal, ddt = gvf(jnp.concatenate(dgs, axis=0))
        dba_ref[...] = (jnp.concatenate(dbas, axis=0) + dba2).astype(BF16)
        dal_ref[...] += dal
        ddt_ref[...] += ddt
        dng_ref[...] += dng

    def rev(width):
        return pl.BlockSpec((tb, width), lambda i: (nt - 1 - i, 0))

    return pl.pallas_call(
        body, name=name, grid=(nt,),
        in_specs=[rev(1536), _pblk(tb, 512, C_Z, rev_n=nt), _pblk(tb, 128, C_BA, rev_n=nt),
                  pl.BlockSpec((cps, GDN_HEADS, hd, hd), lambda i: (nt - 1 - i, 0, 0, 0)), rev(512),
                  _full((1, 128)), _full((1, 128)), _full((1, 128))],
        out_specs=[rev(1536), rev(512), rev(128), _full((1, 128)), _full((1, 128)), _full((1, 128))],
        out_shape=[jax.ShapeDtypeStruct((s, 1536), F32), jax.ShapeDtypeStruct((s, 512), BF16),
                   jax.ShapeDtypeStruct((s, 128), BF16), jax.ShapeDtypeStruct((1, 128), F32),
                   jax.ShapeDtypeStruct((1, 128), F32), jax.ShapeDtypeStruct((1, 128), F32)],
        scratch_shapes=[pltpu.VMEM((GDN_HEADS, hd, hd), F32)],
        compiler_params=_params("arbitrary"),
    )(qkv, proj, proj, sall, dy, alog, dt, ng)


def merge_fwd(x, ys, proj, wb, wo, *, tb, name):
    s, d = x.shape

    def body(x_ref, ya, yb, yc, yd, g0, g1, g2, g3, wb_ref, wo_ref, o_ref, m_ref):
        m = jnp.zeros((tb, d), F32)
        for g, (y_ref, gp_ref) in enumerate(zip((ya, yb, yc, yd), (g0, g1, g2, g3))):
            m = m + jax.nn.sigmoid(gp_ref[...]) * _dg(y_ref[...], wb_ref[g], 1, 0)
        mb = m.astype(BF16)
        m_ref[...] = mb
        o_ref[...] = x_ref[...] + _dg(mb, wo_ref[...], 1, 0)

    row = pl.BlockSpec((tb, d), lambda i: (i, 0))
    yrow = pl.BlockSpec((tb, BRANCH_W), lambda i: (i, 0))
    return pl.pallas_call(
        body, name=name, grid=(s // tb,),
        in_specs=[row] + [yrow] * 4 + [_pblk(tb, d, C_GATE + d * g) for g in range(4)]
        + [_full((4, BRANCH_W, d)), _full((d, d))],
        out_specs=[row, row],
        out_shape=[jax.ShapeDtypeStruct((s, d), F32), jax.ShapeDtypeStruct((s, d), BF16)],
        compiler_params=_params("parallel"),
    )(x, *ys, proj, proj, proj, proj, wb, wo)


def merge_bwd(dx, ys, proj, wb, wo, *, tb, name):
    s, d = dx.shape

    def body(dx_ref, ya, yb, yc, yd, g0, g1, g2, g3, wb_ref, wo_ref,
             dgp_ref, dbr_ref, da, db, dc, dd, dxb_ref):
        dxb = dx_ref[...].astype(BF16)
        dxb_ref[...] = dxb
        dm = _dg(dxb, wo_ref[...], 1, 1)
        for g, (y_ref, gp_ref, dy_ref) in enumerate(zip((ya, yb, yc, yd), (g0, g1, g2, g3), (da, db, dc, dd))):
            br = _dg(y_ref[...], wb_ref[g], 1, 0)
            sg = jax.nn.sigmoid(gp_ref[...])
            dgp_ref[:, d * g:d * (g + 1)] = (dm * br * sg * (1.0 - sg)).astype(BF16)
            dbr = (dm * sg).astype(BF16)
            dbr_ref[g] = dbr
            dy_ref[...] = _dg(dbr, wb_ref[g], 1, 1)

    row = pl.BlockSpec((tb, d), lambda i: (i, 0))
    yrow = pl.BlockSpec((tb, BRANCH_W), lambda i: (i, 0))
    return pl.pallas_call(
        body, name=name, grid=(s // tb,),
        in_specs=[row] + [yrow] * 4 + [_pblk(tb, d, C_GATE + d * g) for g in range(4)]
        + [_full((4, BRANCH_W, d)), _full((d, d))],
        out_specs=[pl.BlockSpec((tb, 4 * d), lambda i: (i, 0)), pl.BlockSpec((4, tb, d), lambda i: (0, i, 0)),
                   yrow, yrow, yrow, yrow, row],
        out_shape=[jax.ShapeDtypeStruct((s, 4 * d), BF16), jax.ShapeDtypeStruct((4, s, d), BF16)]
        + [jax.ShapeDtypeStruct((s, BRANCH_W), F32)] * 4 + [jax.ShapeDtypeStruct((s, d), BF16)],
        compiler_params=_params("parallel"),
    )(dx, *ys, proj, proj, proj, proj, wb, wo)


def loss_head(x, g, target, *, tb, name):
    s, d = x.shape

    def body(x_ref, g_ref, t_ref, dx_ref, dg_ref, loss_ref):
        @pl.when(pl.program_id(0) == 0)
        def _():
            dg_ref[...] = jnp.zeros_like(dg_ref)
            loss_ref[...] = jnp.zeros_like(loss_ref)

        xv, gv = x_ref[...], g_ref[...]
        r = lax.rsqrt(jnp.mean(xv * xv, axis=-1, keepdims=True) + EPS)
        err = xv * r * gv - t_ref[...]
        loss_ref[...] += 0.5 * jnp.sum(jnp.mean(err * err, axis=-1, keepdims=True), axis=0, keepdims=True)
        dx, dg = _rms_bwd(xv, gv, err * (1.0 / d))
        dx_ref[...] = dx
        dg_ref[...] += dg

    row = pl.BlockSpec((tb, d), lambda i: (i, 0))
    return pl.pallas_call(
        body, name=name, grid=(s // tb,),
        in_specs=[row, _full((1, d)), row],
        out_specs=[row, _full((1, d)), _full((1, 128))],
        out_shape=[jax.ShapeDtypeStruct((s, d), F32), jax.ShapeDtypeStruct((1, d), F32),
                   jax.ShapeDtypeStruct((1, 128), F32)],
        compiler_params=_params("arbitrary"),
    )(x, g, target)


def adamw(w, g, m, v, *, name):
    shape = w.shape
    c = shape[-1]
    r = math.prod(shape[:-1])
    tr = r
    while tr * c * 4 > (1 << 20) and tr % 16 == 0:
        tr //= 2
    c1 = 1.0 / (1.0 - ADAM_B1 ** ADAM_STEP)
    c2 = 1.0 / (1.0 - ADAM_B2 ** ADAM_STEP)

    def body(w_ref, g_ref, m_ref, v_ref, d_ref, nm_ref, nv_ref):
        gv = g_ref[...]
        nm = ADAM_B1 * m_ref[...] + (1.0 - ADAM_B1) * gv
        nv = ADAM_B2 * v_ref[...] + (1.0 - ADAM_B2) * (gv * gv)
        nm_ref[...] = nm
        nv_ref[...] = nv
        d_ref[...] = -ADAM_LR * ((nm * c1) / (jnp.sqrt(nv * c2) + ADAM_EPS) + ADAM_WD * w_ref[...])

    blk = pl.BlockSpec((tr, c), lambda i: (i, 0))
    outs = pl.pallas_call(
        body, name=name, grid=(r // tr,),
        in_specs=[blk] * 4, out_specs=[blk] * 3,
        out_shape=[jax.ShapeDtypeStruct((r, c), F32)] * 3,
        compiler_params=_params("parallel"),
    )(*(t.reshape(r, c) for t in (w, g, m, v)))
    return tuple(o.reshape(shape) for o in outs)


_ANY = pl.BlockSpec(memory_space=pl.ANY)


def _place():
    x, y, c = lax.axis_index("x"), lax.axis_index("y"), lax.axis_index("c")
    return x, y, c, [(1 - x, y), (x, 1 - y), (1 - x, 1 - y)]


def gather_shards(bufs, *, name):
    n = len(bufs)

    def body(*refs):
        outs = refs[n:2 * n]
        ici_send, ici_recv, d2d_send, d2d_recv = refs[2 * n:]
        x, y, c, chips = _place()
        me = 2 * x + y

        def ici(a, j, slab, to):
            return pltpu.make_async_remote_copy(
                src_ref=outs[a].at[slab, c], dst_ref=outs[a].at[slab, c], send_sem=ici_send.at[a, j],
                recv_sem=ici_recv.at[a, j], device_id=to, device_id_type=MESH)

        def d2d(a, j, slab, half):
            return pltpu.make_async_remote_copy(
                src_ref=outs[a].at[slab, half], dst_ref=outs[a].at[slab, half], send_sem=d2d_send.at[a, j],
                recv_sem=d2d_recv.at[a, j], device_id=(x, y, 1 - c), device_id_type=MESH)

        sends = []
        for a in range(n):
            for j, (px, py) in enumerate(chips):
                cp = ici(a, j, me, (px, py, c))
                cp.start()
                sends.append(cp)
        for a in range(n):
            for j, (px, py) in enumerate(chips):
                ici(a, j, 2 * px + py, (px, py, c)).wait_recv()
                cp = d2d(a, j, 2 * px + py, c)
                cp.start()
                sends.append(cp)
        for a in range(n):
            for j, (px, py) in enumerate(chips):
                d2d(a, j, 2 * px + py, 1 - c).wait_recv()
        for cp in sends:
            cp.wait_send()

    return pl.pallas_call(
        body, name=name,
        in_specs=[_ANY] * n, out_specs=[_ANY] * n,
        out_shape=[jax.ShapeDtypeStruct(t.shape, t.dtype) for t in bufs],
        input_output_aliases={a: a for a in range(n)},
        scratch_shapes=[pltpu.SemaphoreType.DMA((n, 3))] * 4,
    )(*bufs)


def pair_send_halves(arrs, *, name):
    n = len(arrs)

    def body(*refs):
        ins, outs = refs[:n], refs[n:2 * n]
        send_sems, recv_sems = refs[2 * n:]
        x, y, c, _ = _place()
        copies = []
        for a in range(n):
            for j in range(N_SHARD):
                cp = pltpu.make_async_remote_copy(
                    src_ref=ins[a].at[j, 1 - c], dst_ref=outs[a].at[j], send_sem=send_sems.at[a, j],
                    recv_sem=recv_sems.at[a, j], device_id=(x, y, 1 - c), device_id_type=MESH)
                cp.start()
                copies.append(cp)
        for cp in copies:
            cp.wait()

    return pl.pallas_call(
        body, name=name,
        in_specs=[_ANY] * n, out_specs=[_ANY] * n,
        out_shape=[jax.ShapeDtypeStruct((N_SHARD,) + t.shape[2:], t.dtype) for t in arrs],
        scratch_shapes=[pltpu.SemaphoreType.DMA((n, N_SHARD)), pltpu.SemaphoreType.DMA((n, N_SHARD))],
    )(*arrs)


def chip_scatter(arrs, *, name):
    n = len(arrs)

    def body(*refs):
        ins, outs = refs[:n], refs[n:2 * n]
        send_sems, recv_sems = refs[2 * n:]
        x, y, c, chips = _place()
        me = 2 * x + y
        copies = []
        for a in range(n):
            for j, (px, py) in enumerate(chips):
                cp = pltpu.make_async_remote_copy(
                    src_ref=ins[a].at[2 * px + py], dst_ref=outs[a].at[me], send_sem=send_sems.at[a, j],
                    recv_sem=recv_sems.at[a, j], device_id=(px, py, c), device_id_type=MESH)
                cp.start()
                copies.append(cp)
        for cp in copies:
            cp.wait()

    return pl.pallas_call(
        body, name=name,
        in_specs=[_ANY] * n, out_specs=[_ANY] * n,
        out_shape=[jax.ShapeDtypeStruct(t.shape, t.dtype) for t in arrs],
        scratch_shapes=[pltpu.SemaphoreType.DMA((n, 3)), pltpu.SemaphoreType.DMA((n, 3))],
    )(*arrs)


def pair_gather(bufs, *, name):
    n = len(bufs)

    def body(*refs):
        outs = refs[n:2 * n]
        send_sems, recv_sems = refs[2 * n:]
        x, y, c, _ = _place()
        copies = []
        for a in range(n):
            cp = pltpu.make_async_remote_copy(
                src_ref=outs[a].at[c], dst_ref=outs[a].at[c], send_sem=send_sems.at[a],
                recv_sem=recv_sems.at[a], device_id=(x, y, 1 - c), device_id_type=MESH)
            cp.start()
            copies.append(cp)
        for cp in copies:
            cp.wait()

    return pl.pallas_call(
        body, name=name,
        in_specs=[_ANY] * n, out_specs=[_ANY] * n,
        out_shape=[jax.ShapeDtypeStruct(t.shape, t.dtype) for t in bufs],
        input_output_aliases={a: a for a in range(n)},
        scratch_shapes=[pltpu.SemaphoreType.DMA((n,)), pltpu.SemaphoreType.DMA((n,))],
    )(*bufs)


def all_sum_small(v, *, name):
    m = v.shape[0]

    def body(v_ref, o_ref, buf, send_sems, recv_sems, local_sem):
        x, y, c, chips = _place()
        me, sibling = (x, y, c), (x, y, 1 - c)

        def rows(px, py, pc):
            return buf.at[pl.ds((4 * px + 2 * py + pc) * m, m), :]

        def copy(k, block, to, src=None):
            return pltpu.make_async_remote_copy(
                src_ref=rows(*block) if src is None else src, dst_ref=rows(*block),
                send_sem=send_sems.at[k], recv_sem=recv_sems.at[k], device_id=to, device_id_type=MESH)

        mine = pltpu.make_async_copy(v_ref, rows(*me), local_sem)
        mine.start()
        first = [copy(0, me, sibling, src=v_ref)]
        first += [copy(1 + j, me, (*chip, c), src=v_ref) for j, chip in enumerate(chips)]
        for cp in first:
            cp.start()
        passed = [copy(4 + j, (*chip, c), sibling) for j, chip in enumerate(chips)]
        for j, chip in enumerate(chips):
            copy(1 + j, (*chip, c), me).wait_recv()
            passed[j].start()
        copy(0, sibling, me).wait_recv()
        for j, chip in enumerate(chips):
            copy(4 + j, (*chip, 1 - c), me).wait_recv()
        for cp in first + passed:
            cp.wait_send()
        mine.wait()
        acc = buf[pl.ds(0, m), :]
        for k in range(1, 8):
            acc = acc + buf[pl.ds(k * m, m), :]
        o_ref[...] = acc

    vm = pl.BlockSpec(memory_space=pltpu.VMEM)
    return pl.pallas_call(
        body, name=name, in_specs=[vm], out_specs=vm,
        out_shape=jax.ShapeDtypeStruct((m, 128), F32),
        scratch_shapes=[pltpu.VMEM((8 * m, 128), F32), pltpu.SemaphoreType.DMA((7,)),
                        pltpu.SemaphoreType.DMA((7,)), pltpu.SemaphoreType.DMA],
    )(v)


def sum_halves(arr, recv, c_idx, *, name):
    _, _, r, c = arr.shape
    tr = r
    while tr * c * 4 > (1 << 20) and tr % 32 == 0:
        tr //= 2

    def body(c_ref, a_ref, b_ref, o_ref):
        o_ref[...] = (a_ref[...] + b_ref[...]).astype(BF16)

    return pl.pallas_call(
        body, name=name,
        grid_spec=pltpu.PrefetchScalarGridSpec(
            num_scalar_prefetch=1, grid=(N_SHARD, r // tr),
            in_specs=[pl.BlockSpec((None, None, tr, c), lambda j, i, cr: (j, cr[0], i, 0)),
                      pl.BlockSpec((None, tr, c), lambda j, i, cr: (j, i, 0))],
            out_specs=pl.BlockSpec((None, tr, c), lambda j, i, cr: (j, i, 0))),
        out_shape=jax.ShapeDtypeStruct((N_SHARD, r, c), BF16),
        compiler_params=_params("parallel", "parallel"),
    )(c_idx, arr, recv)


def sum_chips(recv, part, place, *, name):
    _, r, c = recv.shape
    tr = r
    while tr * c * 4 > (1 << 20) and tr % 32 == 0:
        tr //= 2

    def body(place_ref, a0, a1, a2, a3, p_ref, o_ref):
        me = place_ref[1]
        acc = None
        for k, a_ref in enumerate((a0, a1, a2, a3)):
            term = jnp.where(me == k, p_ref[...], a_ref[...]).astype(F32)
            acc = term if acc is None else acc + term
        o_ref[...] = acc

    def slab(k):
        return pl.BlockSpec((None, tr, c), lambda i, pr: (jnp.where(pr[1] == k, (k + 1) % N_SHARD, k), i, 0))

    return pl.pallas_call(
        body, name=name,
        grid_spec=pltpu.PrefetchScalarGridSpec(
            num_scalar_prefetch=1, grid=(r // tr,),
            in_specs=[slab(k) for k in range(N_SHARD)]
            + [pl.BlockSpec((None, tr, c), lambda i, pr: (pr[1], i, 0))],
            out_specs=pl.BlockSpec((None, tr, c), lambda i, pr: (pr[0], i, 0))),
        out_shape=jax.ShapeDtypeStruct((2, r, c), F32),
        compiler_params=_params("parallel"),
    )(place, recv, recv, recv, recv, part)


_WEIGHTS = ("ff1_norm", "ff1_wg", "ff1_wu", "ff1_wd", "mix_norm", "w_in", "sgu_ln_g", "sgu_ln_b", "sgu_w",
            "sgu_b", "lru_conv_w", "lru_conv_b", "lru_wa", "lru_ba", "lru_wx", "lru_bx", "lru_lambda",
            "gdn_conv_w", "gdn_a_log", "gdn_dt_bias", "gdn_norm_g", "pool_w", "pool_scale", "w_branch",
            "w_out", "ff2_norm", "ff2_wg", "ff2_wu", "ff2_wd", "final_norm")
_BIG = ("ff1_wg", "ff1_wu", "ff1_wd", "w_in", "w_branch", "w_out", "ff2_wg", "ff2_wu", "ff2_wd")
_SMALL = tuple(n for n in _WEIGHTS if n not in _BIG)
_CONV = ("lru_conv_w", "gdn_conv_w")


def _seg_rows(shape):
    return -(-math.prod(shape) // 1024) * 8


def _pack(arrs):
    segs = []
    for t in arrs:
        rows = _seg_rows(t.shape)
        flat = t.reshape(-1)
        segs.append(jnp.pad(flat, (0, rows * 128 - flat.shape[0])).reshape(rows, 128))
    return jnp.concatenate(segs, axis=0)


def _unpack(buf, shapes):
    out, o = [], 0
    for shp in shapes:
        rows = _seg_rows(shp)
        out.append(buf[o:o + rows].reshape(-1)[:math.prod(shp)].reshape(shp))
        o += rows
    return out


def _block_diag(w):
    h, n, _ = w.shape
    same = jnp.arange(h)[:, None, None, None] == jnp.arange(h)[None, None, :, None]
    return jnp.where(same, w[:, :, None, :], 0.0).reshape(h * n, h * n)


def _diag_blocks(m, h=8, n=64):
    return jnp.stack([m[i * n:(i + 1) * n, i * n:(i + 1) * n] for i in range(h)])


def _lane_row(v):
    return jnp.zeros((1, 128), F32).at[0, 4:8].set(v)


def _w_in_to_padded(raw):
    sem = jnp.concatenate([raw[k, :, :P_IN_SHARD] for k in range(N_SHARD)], axis=1)
    zero = jnp.zeros((sem.shape[0], C_GATE - C_BA - 8), sem.dtype)
    return jnp.concatenate([sem[:, 2048:4096], sem[:, 0:2048], sem[:, 4104:4616], sem[:, 4096:4104], zero,
                            sem[:, 4616:P_IN]], axis=1)


def _w_in_grad_to_shards(part):
    r = part.shape[1]
    dwp = part.transpose(1, 0, 2).reshape(r, P_PAD)
    sem = jnp.concatenate([dwp[:, C_AU:C_DX], dwp[:, 0:C_AU], dwp[:, C_BA:C_BA + 8], dwp[:, C_DX:C_BA],
                           dwp[:, C_GATE:]], axis=1)
    sh = sem.reshape(r, N_SHARD, P_IN_SHARD).transpose(1, 0, 2)
    return jnp.pad(sh, ((0, 0), (0, 0), (0, P_IN_SHARD_PAD - P_IN_SHARD)))


def kernel(x, ff1_norm, ff1_wg, ff1_wu, ff1_wd, mix_norm, w_in, sgu_ln_g, sgu_ln_b, sgu_w, sgu_b, lru_conv_w,
           lru_conv_b, lru_wa, lru_ba, lru_wx, lru_bx, lru_lambda, gdn_conv_w, gdn_a_log, gdn_dt_bias, gdn_norm_g,
           pool_w, pool_scale, w_branch, w_out, ff2_norm, ff2_wg, ff2_wu, ff2_wd, final_norm, loss_target, m_ff1_norm,
           m_ff1_wg, m_ff1_wu, m_ff1_wd, m_mix_norm, m_w_in, m_sgu_ln_g, m_sgu_ln_b, m_sgu_w, m_sgu_b, m_lru_conv_w,
           m_lru_conv_b, m_lru_wa, m_lru_ba, m_lru_wx, m_lru_bx, m_lru_lambda, m_gdn_conv_w, m_gdn_a_log, m_gdn_dt_bias,
           m_gdn_norm_g, m_pool_w, m_pool_scale, m_w_branch, m_w_out, m_ff2_norm, m_ff2_wg, m_ff2_wu, m_ff2_wd,
           m_final_norm, v_ff1_norm, v_ff1_wg, v_ff1_wu, v_ff1_wd, v_mix_norm, v_w_in, v_sgu_ln_g, v_sgu_ln_b, v_sgu_w,
           v_sgu_b, v_lru_conv_w, v_lru_conv_b, v_lru_wa, v_lru_ba, v_lru_wx, v_lru_bx, v_lru_lambda, v_gdn_conv_w,
           v_gdn_a_log, v_gdn_dt_bias, v_gdn_norm_g, v_pool_w, v_pool_scale, v_w_branch, v_w_out, v_ff2_norm, v_ff2_wg,
           v_ff2_wu, v_ff2_wd, v_final_norm):
    a = dict(locals())
    n_layer = ff1_norm.shape[0]
    d = D_MODEL
    x0 = x[0]
    chip = 2 * lax.axis_index("x") + lax.axis_index("y")
    c_idx = jnp.reshape(lax.axis_index("c"), (1,)).astype(jnp.int32)
    place = jnp.stack([lax.axis_index("c"), chip]).astype(jnp.int32)

    def own_slab(t, dtype=BF16):
        buf = lax.empty((N_SHARD,) + t.shape, dtype)
        return lax.dynamic_update_slice(buf, t.astype(dtype)[None], (chip,) + (0,) * t.ndim)

    w_in_loc = jnp.pad(w_in, ((0, 0), (0, 0), (0, P_IN_SHARD_PAD - P_IN_SHARD)))
    (g_ff1_wg, g_ff1_wu, g_ff1_wd, g_w_in, g_wb, g_wo, g_ff2_wg, g_ff2_wu, g_ff2_wd, g_lcw, g_gcw) = gather_shards(
        [own_slab(t) for t in (ff1_wg, ff1_wu, ff1_wd, w_in_loc, w_branch, w_out, ff2_wg, ff2_wu, ff2_wd)]
        + [own_slab(lru_conv_w, F32), own_slab(gdn_conv_w, F32)], name="gather_weights")

    def layer_params(l):
        p = {}
        p["w_in"] = _w_in_to_padded(g_w_in[:, l])
        p["wb"] = g_wb[:, l].transpose(1, 2, 0, 3).reshape(4, BRANCH_W, d)
        p["wo"] = g_wo[:, l].reshape(d, d)
        p["lcw"] = g_lcw[:, l].transpose(1, 0, 2).reshape(4, BRANCH_W)
        p["gcw"] = g_gcw[:, l].transpose(1, 0, 2).reshape(4, 3 * BRANCH_W)
        p["lcb"] = lru_conv_b[l][None]
        p["bt"] = jnp.zeros((128, 128), F32).at[:, :4].set(sgu_b[l].T)
        p["wa"] = _block_diag(lru_wa[l])
        p["wx"] = _block_diag(lru_wx[l])
        p["alog"] = _lane_row(gdn_a_log[l])
        p["dt"] = _lane_row(gdn_dt_bias[l])
        return p

    saved = []
    xs = x0
    for l in range(n_layer):
        p = layer_params(l)
        sv = {"p": p, "x0": xs}
        x1 = ffn_fwd(xs, ff1_norm[l][None], g_ff1_wg, g_ff1_wu, g_ff1_wd, l, tb=512, name=f"ffn1_fwd_{l}")
        h = norm_fwd(x1, mix_norm[l][None], tb=512, name=f"mix_norm_fwd_{l}")
        proj = matmul(h, p["w_in"], tm=1024, tn=1536, tk=1024, name=f"proj_{l}")
        ya = sgu_fwd(proj, sgu_ln_g[l][None], sgu_ln_b[l][None], sgu_w[l], p["bt"], tb=512, name=f"sgu_fwd_{l}")
        xc = conv_fwd(proj, C_BX, BRANCH_W, p["lcw"], p["lcb"], tb=512, name=f"lru_conv_fwd_{l}")
        yb, hl = lru_fwd(xc, proj, p["wa"], p["wx"], lru_ba[l][None], lru_bx[l][None], lru_lambda[l][None],
                         tb=256, name=f"lru_fwd_{l}")
        qkv = conv_fwd(proj, C_QKV, 3 * BRANCH_W, p["gcw"], jnp.zeros((1, 3 * BRANCH_W), F32), tb=512,
                       name=f"gdn_conv_fwd_{l}")
        yc, sall = gdn_fwd(qkv, proj, p["alog"], p["dt"], gdn_norm_g[l][None], cps=GDN_CPS, name=f"gdn_fwd_{l}")
        yd = pool_fwd(proj, pool_w[l], pool_scale[l][None], tb=512, name=f"pool_fwd_{l}")
        ys = (ya, yb, yc, yd)
        x2, merged = merge_fwd(x1, ys, proj, p["wb"], p["wo"], tb=256, name=f"merge_fwd_{l}")
        x3 = ffn_fwd(x2, ff2_norm[l][None], g_ff2_wg, g_ff2_wu, g_ff2_wd, l, tb=512, name=f"ffn2_fwd_{l}")
        sv.update(x1=x1, h=h, proj=proj, xc=xc, hl=hl, qkv=qkv, sall=sall, ys=ys, merged=merged, x2=x2)
        saved.append(sv)
        xs = x3

    dx, d_final, loss_blk = loss_head(xs, final_norm[None], loss_target[0], tb=512, name="loss_head")
    loss = lax.psum(loss_blk[0, 0], ("x", "y", "c"))

    small_g = {n: [None] * n_layer for n in _SMALL if n != "final_norm"}
    big_g = [None] * n_layer
    for l in reversed(range(n_layer)):
        sv = saved[l]
        p = sv["p"]
        proj = sv["proj"]
        dx2, dg_ff2, dab2, hid2, h2, dy2 = ffn_bwd(sv["x2"], dx, ff2_norm[l][None], g_ff2_wg, g_ff2_wu, g_ff2_wd,
                                                  l, tb=256, name=f"ffn2_bwd_{l}")
        dwgu2 = matmul(h2, dab2, ta=True, tm=1024, tn=FF_SHARD, tk=512, name=f"ffn2_dwgu_{l}")
        dwd2 = matmul(hid2, dy2, ta=True, tm=FF_SHARD, tn=1024, tk=512, name=f"ffn2_dwd_{l}")
        dgp, dbr, dya, dyb, dyc, dyd, dx2b = merge_bwd(dx2, sv["ys"], proj, p["wb"], p["wo"], tb=256,
                                                       name=f"merge_bwd_{l}")
        dwo = matmul(sv["merged"], dx2b, ta=True, tm=1024, tn=1024, tk=512, name=f"dwo_{l}")
        dwb = jnp.stack([matmul(sv["ys"][g], dbr[g], ta=True, tm=BRANCH_W, tn=1024, tk=512, name=f"dwb{g}_{l}")
                         for g in range(4)])
        du, dv, dlg, dlb, dws, dbt = sgu_bwd(proj, dya, sgu_ln_g[l][None], sgu_ln_b[l][None], sgu_w[l], p["bt"],
                                             tb=512, name=f"sgu_bwd_{l}")
        dxc, dgate, dwa, dwx, dba, dbx, dlam = lru_bwd(sv["xc"], proj, sv["hl"], dyb, p["wa"], p["wx"],
                                                       lru_ba[l][None], lru_bx[l][None], lru_lambda[l][None],
                                                       tb=256, name=f"lru_bwd_{l}")
        dbx_in, dcw_l = conv_bwd(proj, C_BX, BRANCH_W, dxc, p["lcw"], tb=512, name=f"lru_conv_bwd_{l}")
        dqkv, dz, dbeta, dal, ddt, dng = gdn_bwd(sv["qkv"], proj, sv["sall"], dyc, p["alog"], p["dt"],
                                                 gdn_norm_g[l][None], cps=GDN_CPS, name=f"gdn_bwd_{l}")
        dqkv_in, dcw_g = conv_bwd(proj, C_QKV, 3 * BRANCH_W, dqkv, p["gcw"], tb=512, name=f"gdn_conv_bwd_{l}")
        ddx, dwp, dsc = pool_bwd(proj, dyd, pool_w[l], pool_scale[l][None], tb=512, name=f"pool_bwd_{l}")
        s = dx.shape[0]
        dproj = jnp.concatenate([dqkv_in, dz, du, dv, dbx_in, dgate, ddx, dbeta,
                                 jnp.zeros((s, C_GATE - C_BA - 128), BF16), dgp], axis=1)
        dh = matmul(dproj, p["w_in"], tb=True, tm=1024, tn=1024, tk=1536, name=f"dh_mix_{l}")
        dw_in = matmul(sv["h"], dproj, ta=True, tm=1024, tn=768, tk=512, col_groups=N_SHARD, name=f"dw_in_{l}")
        dx1, dg_mix = norm_bwd(sv["x1"], mix_norm[l][None], dh, dx2, tb=512, name=f"mix_norm_bwd_{l}")
        dx, dg_ff1, dab1, hid1, h1, dy1 = ffn_bwd(sv["x0"], dx1, ff1_norm[l][None], g_ff1_wg, g_ff1_wu, g_ff1_wd,
                                                 l, tb=256, name=f"ffn1_bwd_{l}")
        dwgu1 = matmul(h1, dab1, ta=True, tm=1024, tn=FF_SHARD, tk=512, name=f"ffn1_dwgu_{l}")
        dwd1 = matmul(hid1, dy1, ta=True, tm=FF_SHARD, tn=1024, tk=512, name=f"ffn1_dwd_{l}")

        for n, g in (("ff1_norm", dg_ff1[0]), ("mix_norm", dg_mix[0]), ("sgu_ln_g", dlg[0]), ("sgu_ln_b", dlb[0]),
                     ("sgu_w", dws), ("sgu_b", dbt[:, :4].T), ("lru_conv_w", dcw_l[:4]), ("lru_conv_b", dcw_l[4]),
                     ("lru_wa", _diag_blocks(dwa)), ("lru_ba", dba[0]), ("lru_wx", _diag_blocks(dwx)),
                     ("lru_bx", dbx[0]), ("lru_lambda", dlam[0]), ("gdn_conv_w", dcw_g[:4]),
                     ("gdn_a_log", dal[0, 4:8]), ("gdn_dt_bias", ddt[0, 4:8]), ("gdn_norm_g", dng[0]),
                     ("pool_w", dwp), ("pool_scale", dsc[0]), ("ff2_norm", dg_ff2[0])):
            small_g[n][l] = g
        dwb_sh = dwb.reshape(4, BRANCH_W, N_SHARD, d // N_SHARD).transpose(2, 0, 1, 3)
        big_g[l] = [dwgu1, dwd1.reshape(N_SHARD, 2, FF_SHARD // 2, d),
                    dw_in.reshape(N_SHARD, 2, d // 2, P_PAD // N_SHARD),
                    dwb_sh.reshape(N_SHARD, 2, 2 * BRANCH_W, d // N_SHARD),
                    dwo.reshape(N_SHARD, 2, d // 8, d), dwgu2,
                    dwd2.reshape(N_SHARD, 2, FF_SHARD // 2, d)]

    bigs = [t for l in range(n_layer) for t in big_g[l]]
    recv1 = pair_send_halves(bigs, name="grad_pair_exchange")
    part = [sum_halves(t, r, c_idx, name=f"grad_pair_sum_{i}") for i, (t, r) in enumerate(zip(bigs, recv1))]
    per_layer = len(big_g[0])
    for l in range(n_layer):
        part[l * per_layer + 2] = _w_in_grad_to_shards(part[l * per_layer + 2])
    recv2 = chip_scatter(part, name="grad_chip_exchange")
    red = [sum_chips(r, p, place, name=f"grad_chip_sum_{i}") for i, (r, p) in enumerate(zip(recv2, part))]
    full = pair_gather(red, name="grad_pair_gather")
    per_layer = len(big_g[0])
    grads = {}

    def layers(k, f):
        return jnp.stack([f(full[l * per_layer + k]) for l in range(n_layer)])

    grads["ff1_wg"] = layers(0, lambda t: t[0])
    grads["ff1_wu"] = layers(0, lambda t: t[1])
    grads["ff1_wd"] = layers(1, lambda t: t.reshape(FF_SHARD, d))
    grads["w_in"] = layers(2, lambda t: t.reshape(d, P_IN_SHARD_PAD)[:, :P_IN_SHARD])
    grads["w_branch"] = layers(3, lambda t: t.reshape(4, BRANCH_W, d // N_SHARD))
    grads["w_out"] = layers(4, lambda t: t.reshape(d // N_SHARD, d))
    grads["ff2_wg"] = layers(5, lambda t: t[0])
    grads["ff2_wu"] = layers(5, lambda t: t[1])
    grads["ff2_wd"] = layers(6, lambda t: t.reshape(FF_SHARD, d))

    small_full = [jnp.stack(small_g[n]) for n in _SMALL if n != "final_norm"] + [d_final[0]]
    summed = _unpack(all_sum_small(_pack(small_full), name="grad_small_allreduce"), [t.shape for t in small_full])
    for n, g in zip(_SMALL, summed):
        if n in _CONV:
            width = a[n].shape[-1]
            g = lax.dynamic_slice_in_dim(g, chip * width, width, axis=2)
        grads[n] = g

    delta, new_m, new_v = {}, {}, {}
    for n in _BIG:
        delta[n], new_m[n], new_v[n] = adamw(a[n], grads[n], a["m_" + n], a["v_" + n], name=f"adamw_{n}")
    shapes = [a[n].shape for n in _SMALL]
    packed = [_pack([src[pre + n] for n in _SMALL]) for src, pre in ((a, ""), (grads, ""), (a, "m_"), (a, "v_"))]
    for store, buf in zip((delta, new_m, new_v), adamw(*packed, name="adamw_small")):
        store.update(zip(_SMALL, _unpack(buf, shapes)))

    return (loss, dx[None], *[grads[n] for n in _WEIGHTS], *[delta[n] for n in _WEIGHTS],
            *[new_m[n] for n in _WEIGHTS], *[new_v[n] for n in _WEIGHTS])
```

```python
import functools
import math

import jax
import jax.numpy as jnp
from jax import lax
from jax.experimental import pallas as pl
from jax.experimental.pallas import tpu as pltpu

F32 = jnp.float32
BF16 = jnp.bfloat16
MESH = pl.DeviceIdType.MESH

D_MODEL = 1024
N_SHARD = 4
D_FF = 2816
FF_SHARD = D_FF // N_SHARD
BRANCH_W = 512
CHUNK = 64
GDN_HEADS = 4
GDN_DK = 128
GDN_CPS = 4
TK_DW = 2048
LRU_C = 8.0
EPS = 1e-6
P_IN = 8712
P_IN_SHARD = P_IN // N_SHARD
P_IN_SHARD_PAD = 2304
P_PAD = 9216
C_QKV, C_Z, C_AU, C_AV, C_BX, C_BG, C_DX, C_BA, C_GATE = 0, 1536, 2048, 2560, 3072, 3584, 4096, 4608, 5120

ADAM_LR, ADAM_B1, ADAM_B2, ADAM_EPS, ADAM_WD, ADAM_STEP = 0.001, 0.9, 0.999, 1e-08, 0.01, 10


def _gelu(x):
    return 0.5 * x * (1.0 + jnp.tanh(0.7978845608028654 * (x + 0.044715 * (x * x * x))))


def _gelu_grad(x):
    u = 0.7978845608028654 * (x + 0.044715 * (x * x * x))
    t = jnp.tanh(u)
    return 0.5 * (1.0 + t) + 0.5 * x * (1.0 - t * t) * 0.7978845608028654 * (1.0 + 3.0 * 0.044715 * x * x)


def _silu(x):
    return x * jax.nn.sigmoid(x)


def _softplus(x):
    return jnp.maximum(x, 0.0) + jnp.log1p(jnp.exp(-jnp.abs(x)))


def _dg(a, b, ca, cb):
    return lax.dot_general(a.astype(BF16), b.astype(BF16), (((ca,), (cb,)), ((), ())),
                           preferred_element_type=F32)


def _dg3(a, b, ca, cb):
    a1 = a.astype(BF16)
    a2 = (a - a1.astype(F32)).astype(BF16)
    b1 = b.astype(BF16)
    b2 = (b - b1.astype(F32)).astype(BF16)
    dn = (((ca,), (cb,)), ((), ()))
    d = functools.partial(lax.dot_general, dimension_numbers=dn, preferred_element_type=F32)
    return d(a1, b1) + (d(a1, b2) + d(a2, b1))


def _make_mm(ca, cb, dot):
    @jax.custom_vjp
    def f(a, b):
        return dot(a, b, ca, cb)

    def fwd(a, b):
        return dot(a, b, ca, cb), (a, b)

    def bwd(res, g):
        a, b = res
        if (ca, cb) == (1, 0):
            return dot(g, b, 1, 1), dot(a, g, 0, 0)
        if (ca, cb) == (1, 1):
            return dot(g, b, 1, 0), dot(g, a, 0, 0)
        return dot(b, g, 1, 1), dot(a, g, 1, 0)

    f.defvjp(fwd, bwd)
    return f


mm_nn = _make_mm(1, 0, _dg)


def _bdg(a, b, ca, cb):
    return lax.dot_general(a.astype(BF16), b.astype(BF16), (((ca,), (cb,)), ((0,), (0,))),
                           preferred_element_type=F32)


def _split2(x):
    x1 = x.astype(BF16)
    return x1, (x - x1.astype(F32)).astype(BF16)


def _bdg3(a, b, ca, cb):
    a1, a2 = _split2(a)
    b1, b2 = _split2(b)
    d = functools.partial(lax.dot_general, dimension_numbers=(((ca,), (cb,)), ((0,), (0,))),
                          preferred_element_type=F32)
    return d(a1, b1) + (d(a1, b2) + d(a2, b1))


def _make_bmm(ca, cb):
    @jax.custom_vjp
    def f(a, b):
        return _bdg(a, b, ca, cb)

    def fwd(a, b):
        return _bdg(a, b, ca, cb), (a, b)

    def bwd(res, g):
        a, b = res
        if (ca, cb) == (2, 1):
            return _bdg(g, b, 2, 2), _bdg(a, g, 1, 1)
        if (ca, cb) == (2, 2):
            return _bdg(g, b, 2, 1), _bdg(g, a, 1, 1)
        return _bdg(b, g, 2, 2), _bdg(a, g, 2, 1)

    f.defvjp(fwd, bwd)
    return f


bmm_nn = _make_bmm(2, 1)
bmm_nt = _make_bmm(2, 2)
bmm_tn = _make_bmm(1, 1)


def _iota3(shape, dim):
    return lax.broadcasted_iota(jnp.int32, shape, dim)


@jax.custom_vjp
def _tri_inv(aw):
    b, n, _ = aw.shape
    shp = (b, n, 2 * n)
    left = _iota3(shp, 2) < n
    x = jnp.where(_iota3(shp, 1) == _iota3(shp, 2), 1.0, 0.0) - aw
    zero = jnp.zeros((b, n, 4 * n), BF16)
    k = 1
    while k < n:
        x1, x2 = _split2(x)
        lhs = jnp.concatenate([x1, x2], axis=1)
        rhs = jnp.concatenate([zero, jnp.concatenate([x1, x2], axis=2)], axis=1)
        r = lax.dot_general(lhs, rhs, (((2,), (1,)), ((0,), (0,))), preferred_element_type=F32)
        mx = (r[:, :n, :2 * n] + r[:, :n, 2 * n:]) + (r[:, n:, :2 * n] + r[:, n:, 2 * n:])
        x = jnp.where(left, x, 0.0) + mx
        k *= 2
    return jnp.where(left, x, 0.0)


def _tri_inv_fwd(aw):
    t = _tri_inv(aw)
    return t, t


def _tri_inv_bwd(t, dt):
    n = t.shape[1]
    x = _bdg3(t, dt, 1, 1)[:, :n]
    tp = jnp.concatenate([jnp.zeros_like(t), t], axis=1)
    return (-_bdg3(x, tp, 2, 2),)


_tri_inv.defvjp(_tri_inv_fwd, _tri_inv_bwd)


def _rows(shape):
    return lax.broadcasted_iota(jnp.int32, shape, 0)


def _lanes(shape):
    return lax.broadcasted_iota(jnp.int32, shape, 1)


def _col(x, j):
    return jnp.sum(jnp.where(_lanes(x.shape) == j, x, 0.0), axis=1, keepdims=True)


def _put_col(col, j, width):
    shape = (col.shape[0], width)
    return jnp.where(_lanes(shape) == j, jnp.broadcast_to(col, shape), 0.0)


VMEM_BYTES_V7X = 64 << 20


def _params(*sem, vmem_bytes=None):
    return pltpu.CompilerParams(dimension_semantics=sem, vmem_limit_bytes=vmem_bytes)


def matmul(a, b, *, ta=False, tb=False, tm, tn, tk, out_dtype=F32, col_groups=1, name):
    ba, bb = a.shape[:-2], b.shape[:-2]
    batch = ba if len(ba) >= len(bb) else bb
    assert ba in ((), batch) and bb in ((), batch)
    nb = len(batch)
    (m, k) = (a.shape[-1], a.shape[-2]) if ta else (a.shape[-2], a.shape[-1])
    (k2, n) = (b.shape[-1], b.shape[-2]) if tb else (b.shape[-2], b.shape[-1])
    tm, tn, tk = min(tm, m), min(tn, n), min(tk, k)
    assert k == k2 and m % tm == 0 and n % tn == 0 and k % tk == 0, (a.shape, b.shape, tm, tn, tk)
    nk = k // tk
    grid = batch + (m // tm, n // tn, nk)

    def a_map(*g):
        i, kk = g[nb], g[nb + 2]
        return (g[:nb] if ba else ()) + ((kk, i) if ta else (i, kk))

    def b_map(*g):
        j, kk = g[nb + 1], g[nb + 2]
        return (g[:nb] if bb else ()) + ((j, kk) if tb else (kk, j))

    per_group = n // tn // col_groups
    assert col_groups == 1 or (nb == 0 and per_group * col_groups * tn == n)

    def o_map(*g):
        if col_groups > 1:
            return (g[1] // per_group, g[0], g[1] % per_group)
        return g[:nb] + (g[nb], g[nb + 1])

    a_blk = (None,) * len(ba) + ((tk, tm) if ta else (tm, tk))
    b_blk = (None,) * len(bb) + ((tn, tk) if tb else (tk, tn))
    o_blk = (None,) * (nb + (col_groups > 1)) + (tm, tn)
    o_shape = (col_groups, m, n // col_groups) if col_groups > 1 else batch + (m, n)

    def body(a_ref, b_ref, o_ref, acc_ref):
        kk = pl.program_id(nb + 2)

        @pl.when(kk == 0)
        def _():
            acc_ref[...] = jnp.zeros_like(acc_ref)

        acc_ref[...] += _dg(a_ref[...], b_ref[...], 0 if ta else 1, 1 if tb else 0)

        @pl.when(kk == nk - 1)
        def _():
            o_ref[...] = acc_ref[...].astype(o_ref.dtype)

    return pl.pallas_call(
        body, name=name, grid=grid,
        in_specs=[pl.BlockSpec(a_blk, a_map), pl.BlockSpec(b_blk, b_map)],
        out_specs=pl.BlockSpec(o_blk, o_map),
        out_shape=jax.ShapeDtypeStruct(o_shape, out_dtype),
        scratch_shapes=[pltpu.VMEM((tm, tn), F32)],
        compiler_params=_params(*(("parallel",) * (nb + 2) + ("arbitrary",))),
    )(a, b)


def norm_fwd(x, g, *, tb, name):
    s, d = x.shape

    def body(x_ref, g_ref, h_ref):
        xv = x_ref[...]
        r = lax.rsqrt(jnp.mean(xv * xv, axis=-1, keepdims=True) + EPS)
        h_ref[...] = (xv * r * g_ref[...]).astype(BF16)

    return pl.pallas_call(
        body, name=name, grid=(s // tb,),
        in_specs=[pl.BlockSpec((tb, d), lambda i: (i, 0)), pl.BlockSpec((1, d), lambda i: (0, 0))],
        out_specs=pl.BlockSpec((tb, d), lambda i: (i, 0)),
        out_shape=jax.ShapeDtypeStruct((s, d), BF16),
        compiler_params=_params("parallel"),
    )(x, g)


def _rms_bwd(xv, gv, dh):
    r = lax.rsqrt(jnp.mean(xv * xv, axis=-1, keepdims=True) + EPS)
    xh = xv * r
    dxh = dh * gv
    dx = r * (dxh - xh * jnp.mean(dxh * xh, axis=-1, keepdims=True))
    return dx, jnp.sum(dh * xh, axis=0, keepdims=True)


def norm_bwd(x, g, dh, dres, *, tb, name):
    s, d = x.shape

    def body(x_ref, g_ref, dh_ref, dres_ref, dx_ref, dg_ref):
        dx, dg = _rms_bwd(x_ref[...], g_ref[...], dh_ref[...])
        dx_ref[...] = dres_ref[...] + dx

        @pl.when(pl.program_id(0) == 0)
        def _():
            dg_ref[...] = jnp.zeros_like(dg_ref)

        dg_ref[...] += dg

    row = pl.BlockSpec((tb, d), lambda i: (i, 0))
    one = pl.BlockSpec((1, d), lambda i: (0, 0))
    return pl.pallas_call(
        body, name=name, grid=(s // tb,),
        in_specs=[row, one, row, row], out_specs=[row, one],
        out_shape=[jax.ShapeDtypeStruct((s, d), F32), jax.ShapeDtypeStruct((1, d), F32)],
        compiler_params=_params("arbitrary"),
    )(x, g, dh, dres)


def _dep_operand(dep):
    return ([], []) if dep is None else ([_ANY], [dep])


def ffn_fwd(x, g, wg, wu, wd, layer, *, tb, dep=None, name):
    s, d = x.shape
    fc = wd.shape[-2]
    dep_specs, dep_args = _dep_operand(dep)

    def body(x_ref, g_ref, wg_ref, wu_ref, wd_ref, *rest):
        o_ref, h_scr, acc_scr = rest[len(dep_args):]
        j = pl.program_id(1)

        @pl.when(j == 0)
        def _():
            xv = x_ref[...]
            r = lax.rsqrt(jnp.mean(xv * xv, axis=-1, keepdims=True) + EPS)
            h_scr[...] = (xv * r * g_ref[...]).astype(BF16)
            acc_scr[...] = jnp.zeros_like(acc_scr)

        h = h_scr[...]
        a = _dg(h, wg_ref[...], 1, 0)
        b = _dg(h, wu_ref[...], 1, 0)
        acc_scr[...] += _dg(_silu(a) * b, wd_ref[...], 1, 0)

        @pl.when(j == N_SHARD - 1)
        def _():
            o_ref[...] = x_ref[...] + 0.5 * acc_scr[...]

    row = pl.BlockSpec((tb, d), lambda i, j: (i, 0))
    lg, lu, ld = layer
    return pl.pallas_call(
        body, name=name, grid=(s // tb, N_SHARD),
        in_specs=[row, pl.BlockSpec((1, d), lambda i, j: (0, 0)),
                  pl.BlockSpec((None, None, d, fc), lambda i, j: (j, lg, 0, 0)),
                  pl.BlockSpec((None, None, d, fc), lambda i, j: (j, lu, 0, 0)),
                  pl.BlockSpec((None, None, fc, d), lambda i, j: (j, ld, 0, 0))] + dep_specs,
        out_specs=row,
        out_shape=jax.ShapeDtypeStruct((s, d), F32),
        scratch_shapes=[pltpu.VMEM((tb, d), BF16), pltpu.VMEM((tb, d), F32)],
        compiler_params=_params("parallel", "arbitrary"),
    )(x, g, wg, wu, wd, *dep_args)


def ffn_bwd(x, dxo, g, wg, wu, wd, layer, *, tb, name):
    s, d = x.shape
    fc = wd.shape[-2]

    def body(x_ref, dxo_ref, g_ref, wg_ref, wu_ref, wd_ref,
             dx_ref, dg_ref, dab_ref, hid_ref, h_ref, dy_ref, dh_scr):
        i, j = pl.program_id(0), pl.program_id(1)

        @pl.when(j == 0)
        def _():
            xv = x_ref[...]
            r = lax.rsqrt(jnp.mean(xv * xv, axis=-1, keepdims=True) + EPS)
            h_ref[...] = (xv * r * g_ref[...]).astype(BF16)
            dy_ref[...] = (0.5 * dxo_ref[...]).astype(BF16)
            dh_scr[...] = jnp.zeros_like(dh_scr)

        h = h_ref[...]
        a = _dg(h, wg_ref[...], 1, 0)
        b = _dg(h, wu_ref[...], 1, 0)
        sg = jax.nn.sigmoid(a)
        sl = a * sg
        dhid = _dg(dy_ref[...], wd_ref[...], 1, 1)
        da = (dhid * b * (sg * (1.0 + a * (1.0 - sg)))).astype(BF16)
        db = (dhid * sl).astype(BF16)
        dab_ref[0] = da
        dab_ref[1] = db
        hid_ref[...] = (sl * b).astype(BF16)
        dh_scr[...] += _dg(da, wg_ref[...], 1, 1) + _dg(db, wu_ref[...], 1, 1)

        @pl.when(jnp.logical_and(i == 0, j == 0))
        def _():
            dg_ref[...] = jnp.zeros_like(dg_ref)

        @pl.when(j == N_SHARD - 1)
        def _():
            dx, dg = _rms_bwd(x_ref[...], g_ref[...], dh_scr[...])
            dx_ref[...] = dxo_ref[...] + dx
            dg_ref[...] += dg

    row = pl.BlockSpec((tb, d), lambda i, j: (i, 0))
    one = pl.BlockSpec((1, d), lambda i, j: (0, 0))
    lg, lu, ld = layer
    return pl.pallas_call(
        body, name=name, grid=(s // tb, N_SHARD),
        in_specs=[row, row, one,
                  pl.BlockSpec((None, None, d, fc), lambda i, j: (j, lg, 0, 0)),
                  pl.BlockSpec((None, None, d, fc), lambda i, j: (j, lu, 0, 0)),
                  pl.BlockSpec((None, None, fc, d), lambda i, j: (j, ld, 0, 0))],
        out_specs=[row, one,
                   pl.BlockSpec((None, 2, tb, fc), lambda i, j: (j, 0, i, 0)),
                   pl.BlockSpec((None, tb, fc), lambda i, j: (j, i, 0)),
                   row, row],
        out_shape=[jax.ShapeDtypeStruct((s, d), F32), jax.ShapeDtypeStruct((1, d), F32),
                   jax.ShapeDtypeStruct((N_SHARD, 2, s, fc), BF16),
                   jax.ShapeDtypeStruct((N_SHARD, s, fc), BF16),
                   jax.ShapeDtypeStruct((s, d), BF16), jax.ShapeDtypeStruct((s, d), BF16)],
        scratch_shapes=[pltpu.VMEM((tb, d), F32)],
        compiler_params=_params("arbitrary", "arbitrary",
                                vmem_bytes=min(VMEM_BYTES_V7X - (8 << 20),
                                               12 * d * fc + 44 * tb * d + 40 * tb * fc + (4 << 20))),
    )(x, dxo, g, wg, wu, wd)


def _pblk(tb, width, col0, rev_n=None):
    assert col0 % width == 0
    cb = col0 // width
    if rev_n is None:
        return pl.BlockSpec((tb, width), lambda i: (i, cb))
    return pl.BlockSpec((tb, width), lambda i: (rev_n - 1 - i, cb))


def _full(shape):
    return pl.BlockSpec(shape, lambda i: (0,) * len(shape))


def conv_fwd(src, col0, width, w, bias, *, tb, name):
    s = src.shape[0]
    cb = col0 // width

    def body(x_ref, p_ref, w_ref, b_ref, y_ref):
        i = pl.program_id(0)
        prev = jnp.where(i > 0, p_ref[...], 0.0)
        xe = jnp.concatenate([prev, x_ref[...]], axis=0)
        wv = w_ref[...]
        acc = b_ref[...] + wv[3:4] * xe[8:]
        for k in range(3):
            acc = acc + wv[k:k + 1] * pltpu.roll(xe, 3 - k, 0)[8:]
        y_ref[...] = acc

    return pl.pallas_call(
        body, name=name, grid=(s // tb,),
        in_specs=[_pblk(tb, width, col0),
                  pl.BlockSpec((8, width), lambda i: (jnp.maximum(i * (tb // 8) - 1, 0), cb)),
                  _full((4, width)), _full((1, width))],
        out_specs=pl.BlockSpec((tb, width), lambda i: (i, 0)),
        out_shape=jax.ShapeDtypeStruct((s, width), F32),
        compiler_params=_params("parallel"),
    )(src, src, w, bias)


def conv_bwd(src, col0, width, dpre, w, *, tb, name):
    s = src.shape[0]
    cb = col0 // width
    nt = s // tb

    def body(x_ref, p_ref, d_ref, n_ref, w_ref, dx_ref, dwb_ref):
        i = pl.program_id(0)
        d = d_ref[...]
        nxt = jnp.where(i < nt - 1, n_ref[...], 0.0)
        de = jnp.concatenate([d, nxt], axis=0)
        wv = w_ref[...]
        dx = wv[3:4] * d
        for k in range(3):
            dx = dx + wv[k:k + 1] * pltpu.roll(de, tb + 8 - (3 - k), 0)[:tb]
        dx_ref[...] = dx.astype(BF16)

        @pl.when(i == 0)
        def _():
            dwb_ref[...] = jnp.zeros_like(dwb_ref)

        prev = jnp.where(i > 0, p_ref[...], 0.0)
        xe = jnp.concatenate([prev, x_ref[...]], axis=0)
        dwb_ref[3:4, :] += jnp.sum(d * xe[8:], axis=0, keepdims=True)
        for k in range(3):
            dwb_ref[k:k + 1, :] += jnp.sum(d * pltpu.roll(xe, 3 - k, 0)[8:], axis=0, keepdims=True)
        dwb_ref[4:5, :] += jnp.sum(d, axis=0, keepdims=True)

    return pl.pallas_call(
        body, name=name, grid=(nt,),
        in_specs=[_pblk(tb, width, col0),
                  pl.BlockSpec((8, width), lambda i: (jnp.maximum(i * (tb // 8) - 1, 0), cb)),
                  pl.BlockSpec((tb, width), lambda i: (i, 0)),
                  pl.BlockSpec((8, width), lambda i: (jnp.minimum((i + 1) * (tb // 8), s // 8 - 1), 0)),
                  _full((4, width))],
        out_specs=[pl.BlockSpec((tb, width), lambda i: (i, 0)), _full((8, width))],
        out_shape=[jax.ShapeDtypeStruct((s, width), BF16), jax.ShapeDtypeStruct((8, width), F32)],
        compiler_params=_params("arbitrary"),
    )(src, src, dpre, dpre, w)


SGU_BLOCK = 128


def _sgu_pre(u, v, lg, lb):
    ug = _gelu(u)
    vg = _gelu(v)
    mu = jnp.mean(vg, axis=-1, keepdims=True)
    var = jnp.mean(jnp.square(vg - mu), axis=-1, keepdims=True)
    return ug, (vg - mu) * lax.rsqrt(var + EPS) * lg + lb


def _sgu_mask():
    shp = (SGU_BLOCK, SGU_BLOCK)
    return (_rows(shp) // CHUNK) >= (_lanes(shp) // CHUNK)


def sgu_fwd(proj, lg, lb, ws, bt, *, tb, name):
    s = proj.shape[0]
    w = BRANCH_W

    def body(u_ref, v_ref, lg_ref, lb_ref, w_ref, b_ref, y_ref):
        ug, vn = _sgu_pre(u_ref[...], v_ref[...], lg_ref[...], lb_ref[...])
        mask = _sgu_mask()
        for g in range(4):
            wm = jnp.where(mask, w_ref[g], 0.0)
            bcol = _col(b_ref[...], g)
            cs = slice(128 * g, 128 * g + 128)
            for n in range(tb // SGU_BLOCK):
                rs = slice(SGU_BLOCK * n, SGU_BLOCK * (n + 1))
                mixed = _dg(wm, vn[rs, cs], 1, 0) + bcol
                y_ref[rs, cs] = (ug[rs, cs] * mixed).astype(BF16)

    return pl.pallas_call(
        body, name=name, grid=(s // tb,),
        in_specs=[_pblk(tb, w, C_AU), _pblk(tb, w, C_AV), _full((1, w)), _full((1, w)),
                  _full((4, 128, 128)), _full((128, 128))],
        out_specs=pl.BlockSpec((tb, w), lambda i: (i, 0)),
        out_shape=jax.ShapeDtypeStruct((s, w), BF16),
        compiler_params=_params("parallel"),
    )(proj, proj, lg, lb, ws, bt)


def sgu_bwd(proj, dy, lg, lb, ws, bt, *, tb, name):
    s = proj.shape[0]
    w = BRANCH_W

    def body(u_ref, v_ref, dy_ref, lg_ref, lb_ref, w_ref, b_ref,
             du_ref, dv_ref, dlg_ref, dlb_ref, dw_ref, db_ref, dug_scr, dvn_scr):
        @pl.when(pl.program_id(0) == 0)
        def _():
            dlg_ref[...] = jnp.zeros_like(dlg_ref)
            dlb_ref[...] = jnp.zeros_like(dlb_ref)
            dw_ref[...] = jnp.zeros_like(dw_ref)
            db_ref[...] = jnp.zeros_like(db_ref)

        (ug, vn), vf = jax.vjp(_sgu_pre, u_ref[...], v_ref[...], lg_ref[...], lb_ref[...])
        dyv = dy_ref[...]
        mask = _sgu_mask()
        for g in range(4):
            wm = jnp.where(mask, w_ref[g], 0.0)
            bcol = _col(b_ref[...], g)
            cs = slice(128 * g, 128 * g + 128)
            dwg = jnp.zeros((SGU_BLOCK, SGU_BLOCK), F32)
            dbc = jnp.zeros((SGU_BLOCK, 1), F32)
            for n in range(tb // SGU_BLOCK):
                rs = slice(SGU_BLOCK * n, SGU_BLOCK * (n + 1))
                blk = vn[rs, cs]
                mixed = _dg(wm, blk, 1, 0) + bcol
                dmix = dyv[rs, cs] * ug[rs, cs]
                dug_scr[rs, cs] = dyv[rs, cs] * mixed
                dvn_scr[rs, cs] = _dg(wm, dmix, 0, 0)
                dwg = dwg + _dg(dmix, blk, 1, 1)
                dbc = dbc + jnp.sum(dmix, axis=1, keepdims=True)
            dw_ref[g] += jnp.where(mask, dwg, 0.0)
            db_ref[...] += _put_col(dbc, g, 128)
        du, dv, dlg, dlb = vf((dug_scr[...], dvn_scr[...]))
        du_ref[...] = du.astype(BF16)
        dv_ref[...] = dv.astype(BF16)
        dlg_ref[...] += dlg
        dlb_ref[...] += dlb

    row = pl.BlockSpec((tb, w), lambda i: (i, 0))
    return pl.pallas_call(
        body, name=name, grid=(s // tb,),
        in_specs=[_pblk(tb, w, C_AU), _pblk(tb, w, C_AV), row, _full((1, w)), _full((1, w)),
                  _full((4, 128, 128)), _full((128, 128))],
        out_specs=[row, row, _full((1, w)), _full((1, w)), _full((4, 128, 128)), _full((128, 128))],
        out_shape=[jax.ShapeDtypeStruct((s, w), BF16), jax.ShapeDtypeStruct((s, w), BF16),
                   jax.ShapeDtypeStruct((1, w), F32), jax.ShapeDtypeStruct((1, w), F32),
                   jax.ShapeDtypeStruct((4, 128, 128), F32), jax.ShapeDtypeStruct((128, 128), F32)],
        scratch_shapes=[pltpu.VMEM((tb, w), F32), pltpu.VMEM((tb, w), F32)],
        compiler_params=_params("arbitrary"),
    )(proj, proj, dy, lg, lb, ws, bt)


POOL_HALO = 16


def _pooled(xe, t0, tb):
    s2 = xe + pltpu.roll(xe, 1, 0)
    s4 = s2 + pltpu.roll(s2, 2, 0)
    s8 = s4 + pltpu.roll(s4, 4, 0)
    s16 = s8 + pltpu.roll(s8, 8, 0)
    t = (t0 + _rows((tb, 128))).astype(F32)
    outs = []
    for g, sw in enumerate((s2, s4, s8, s16)):
        cs = slice(128 * g, 128 * g + 128)
        cnt = jnp.minimum(t + 1.0, float(2 ** (g + 1)))
        outs.append(sw[POOL_HALO:, cs] / cnt - xe[POOL_HALO:, cs])
    return outs


def pool_fwd(proj, wp, scale, *, tb, name):
    s = proj.shape[0]
    w = BRANCH_W

    def body(x_ref, p_ref, w_ref, sc_ref, y_ref):
        i = pl.program_id(0)
        prev = jnp.where(i > 0, p_ref[...], 0.0)
        xe = jnp.concatenate([prev, x_ref[...]], axis=0)
        pooled = _pooled(xe, i * tb, tb)
        sc = sc_ref[...]
        for g in range(4):
            cs = slice(128 * g, 128 * g + 128)
            y_ref[:, cs] = (_dg(pooled[g], w_ref[g], 1, 0) * sc[:, cs]).astype(BF16)

    cb = C_DX // w
    return pl.pallas_call(
        body, name=name, grid=(s // tb,),
        in_specs=[_pblk(tb, w, C_DX),
                  pl.BlockSpec((POOL_HALO, w), lambda i: (jnp.maximum(i * (tb // POOL_HALO) - 1, 0), cb)),
                  _full((4, 128, 128)), _full((1, w))],
        out_specs=pl.BlockSpec((tb, w), lambda i: (i, 0)),
        out_shape=jax.ShapeDtypeStruct((s, w), BF16),
        compiler_params=_params("parallel"),
    )(proj, proj, wp, scale)


def pool_bwd(proj, dy, wp, scale, *, tb, name):
    s = proj.shape[0]
    w = BRANCH_W
    nt = s // tb
    te = tb + POOL_HALO

    def body(x_ref, p_ref, dy_ref, n_ref, w_ref, sc_ref, dx_ref, dw_ref, dsc_ref):
        i = pl.program_id(0)

        @pl.when(i == 0)
        def _():
            dw_ref[...] = jnp.zeros_like(dw_ref)
            dsc_ref[...] = jnp.zeros_like(dsc_ref)

        prev = jnp.where(i > 0, p_ref[...], 0.0)
        xe = jnp.concatenate([prev, x_ref[...]], axis=0)
        pooled = _pooled(xe, i * tb, tb)
        dyv = dy_ref[...]
        nxt = jnp.where(i < nt - 1, n_ref[...], 0.0)
        dye = jnp.concatenate([dyv, nxt], axis=0)
        sc = sc_ref[...]
        t = (i * tb + _rows((te, 128))).astype(F32)
        for g in range(4):
            cs = slice(128 * g, 128 * g + 128)
            win = 2 ** (g + 1)
            dpm = dye[:, cs] * sc[:, cs]
            dpool = _dg(dpm, w_ref[g], 1, 1)
            q = dpool / jnp.minimum(t + 1.0, float(win))
            r, sh = q, 1
            while sh < win:
                r = r + pltpu.roll(r, te - sh, 0)
                sh *= 2
            dx_ref[:, cs] = (r[:tb] - dpool[:tb]).astype(BF16)
            dw_ref[g] += _dg(pooled[g], dpm[:tb], 0, 0)
            dsc_ref[:, cs] += jnp.sum(dyv[:, cs] * _dg(pooled[g], w_ref[g], 1, 0), axis=0, keepdims=True)

    cb = C_DX // w
    row = pl.BlockSpec((tb, w), lambda i: (i, 0))
    return pl.pallas_call(
        body, name=name, grid=(nt,),
        in_specs=[_pblk(tb, w, C_DX),
                  pl.BlockSpec((POOL_HALO, w), lambda i: (jnp.maximum(i * (tb // POOL_HALO) - 1, 0), cb)),
                  row,
                  pl.BlockSpec((POOL_HALO, w),
                               lambda i: (jnp.minimum((i + 1) * (tb // POOL_HALO), s // POOL_HALO - 1), 0)),
                  _full((4, 128, 128)), _full((1, w))],
        out_specs=[row, _full((4, 128, 128)), _full((1, w))],
        out_shape=[jax.ShapeDtypeStruct((s, w), BF16), jax.ShapeDtypeStruct((4, 128, 128), F32),
                   jax.ShapeDtypeStruct((1, w), F32)],
        compiler_params=_params("arbitrary"),
    )(proj, proj, dy, dy, wp, scale)


def _neg_expm1(z):
    p = 1.0 + z * (1.0 / 9.0)
    for n in (8.0, 7.0, 6.0, 5.0, 4.0, 3.0, 2.0):
        p = 1.0 + z * (1.0 / n) * p
    return jnp.where(z > -0.5, -z * p, 1.0 - jnp.exp(z))


def _lru_gates(xc, wa, wx, ba, bx, lam):
    r = jax.nn.sigmoid(mm_nn(xc, wa) + ba)
    i = jax.nn.sigmoid(mm_nn(xc, wx) + bx)
    log_a = -LRU_C * r * _softplus(-lam)
    a = jnp.exp(log_a)
    return a, jnp.sqrt(_neg_expm1(2.0 * log_a)) * (i * xc)


def _scan_down(a, b):
    n = a.shape[0]
    row = _rows(a.shape)
    sh = 1
    while sh < n:
        keep = row >= sh
        a_sh = jnp.where(keep, pltpu.roll(a, sh, 0), 1.0)
        b_sh = jnp.where(keep, pltpu.roll(b, sh, 0), 0.0)
        b = a * b_sh + b
        a = a * a_sh
        sh *= 2
    return a, b


def _scan_up(a, b):
    n = a.shape[0]
    row = _rows(a.shape)
    sh = 1
    while sh < n:
        keep = row < n - sh
        a_sh = jnp.where(keep, pltpu.roll(a, n - sh, 0), 1.0)
        b_sh = jnp.where(keep, pltpu.roll(b, n - sh, 0), 0.0)
        b = a * b_sh + b
        a = a * a_sh
        sh *= 2
    return a, b


def lru_fwd(xc, proj, wa, wx, ba, bx, lam, *, tb, name):
    s = xc.shape[0]
    w = BRANCH_W

    def body(xc_ref, gate_ref, wa_ref, wx_ref, ba_ref, bx_ref, lam_ref, y_ref, h_ref, carry):
        @pl.when(pl.program_id(0) == 0)
        def _():
            carry[...] = jnp.zeros_like(carry)

        a, b = _lru_gates(xc_ref[...], wa_ref[...], wx_ref[...], ba_ref[...], bx_ref[...], lam_ref[...])
        pa, hb = _scan_down(a, b)
        h = pa * carry[...] + hb
        h_ref[...] = h
        carry[...] = h_ref[tb - 1:tb, :]
        y_ref[...] = (h * _gelu(gate_ref[...])).astype(BF16)

    row = pl.BlockSpec((tb, w), lambda i: (i, 0))
    return pl.pallas_call(
        body, name=name, grid=(s // tb,),
        in_specs=[row, _pblk(tb, w, C_BG), _full((w, w)), _full((w, w)),
                  _full((1, w)), _full((1, w)), _full((1, w))],
        out_specs=[row, row],
        out_shape=[jax.ShapeDtypeStruct((s, w), BF16), jax.ShapeDtypeStruct((s, w), F32)],
        scratch_shapes=[pltpu.VMEM((1, w), F32)],
        compiler_params=_params("arbitrary"),
    )(xc, proj, wa, wx, ba, bx, lam)


def lru_bwd(xc, proj, h, dy, wa, wx, ba, bx, lam, *, tb, name):
    s = xc.shape[0]
    w = BRANCH_W
    nt = s // tb

    def body(xc_ref, gate_ref, h_ref, hp_ref, dy_ref, wa_ref, wx_ref, ba_ref, bx_ref, lam_ref,
             dxc_ref, dgate_ref, dwa_ref, dwx_ref, dba_ref, dbx_ref, dlam_ref, carry):
        i = pl.program_id(0)

        @pl.when(i == 0)
        def _():
            carry[...] = jnp.zeros_like(carry)
            for r in (dwa_ref, dwx_ref, dba_ref, dbx_ref, dlam_ref):
                r[...] = jnp.zeros_like(r)

        (a, b), vf = jax.vjp(_lru_gates, xc_ref[...], wa_ref[...], wx_ref[...],
                             ba_ref[...], bx_ref[...], lam_ref[...])
        gate = gate_ref[...]
        dyv = dy_ref[...]
        hv = h_ref[...]
        row = _rows((tb, w))
        a_next = jnp.where(row < tb - 1, pltpu.roll(a, tb - 1, 0), 1.0)
        pa, xb = _scan_up(a_next, dyv * _gelu(gate))
        dh = xb + pa * carry[...]
        carry[...] = a[0:1, :] * dh[0:1, :]
        h_first = jnp.where(i < nt - 1, hp_ref[7:8, :], 0.0)
        h_prev = jnp.where(row >= 1, pltpu.roll(hv, 1, 0), h_first)
        dxc, dwa, dwx, dba, dbx, dlam = vf((dh * h_prev, dh))
        dxc_ref[...] = dxc
        dgate_ref[...] = (dyv * hv * _gelu_grad(gate)).astype(BF16)
        dwa_ref[...] += dwa
        dwx_ref[...] += dwx
        dba_ref[...] += dba
        dbx_ref[...] += dbx
        dlam_ref[...] += dlam

    row_spec = pl.BlockSpec((tb, w), lambda i: (nt - 1 - i, 0))
    hp_spec = pl.BlockSpec((8, w), lambda i: (jnp.maximum((nt - 1 - i) * (tb // 8) - 1, 0), 0))
    return pl.pallas_call(
        body, name=name, grid=(nt,),
        in_specs=[row_spec, _pblk(tb, w, C_BG, rev_n=nt), row_spec, hp_spec, row_spec,
                  _full((w, w)), _full((w, w)), _full((1, w)), _full((1, w)), _full((1, w))],
        out_specs=[row_spec, row_spec, _full((w, w)), _full((w, w)),
                   _full((1, w)), _full((1, w)), _full((1, w))],
        out_shape=[jax.ShapeDtypeStruct((s, w), F32), jax.ShapeDtypeStruct((s, w), BF16),
                   jax.ShapeDtypeStruct((w, w), F32), jax.ShapeDtypeStruct((w, w), F32),
                   jax.ShapeDtypeStruct((1, w), F32), jax.ShapeDtypeStruct((1, w), F32),
                   jax.ShapeDtypeStruct((1, w), F32)],
        scratch_shapes=[pltpu.VMEM((1, w), F32)],
        compiler_params=_params("arbitrary"),
    )(xc, proj, h, h, dy, wa, wx, ba, bx, lam)


def _gdn_g(ba, alog, dt):
    return -jnp.exp(alog) * _softplus(ba + dt)


def _gdn_local(qp, kp, vp, bpre, gc):
    b, c, dk = qp.shape
    q = _silu(qp)
    k = _silu(kp)
    v = _silu(vp)
    q = q * lax.rsqrt(jnp.sum(q * q, axis=-1, keepdims=True) + EPS) * (dk ** -0.5)
    k = k * lax.rsqrt(jnp.sum(k * k, axis=-1, keepdims=True) + EPS)
    beta = jax.nn.sigmoid(bpre)
    sq = (b, c, c)
    ii, jj = _iota3(sq, 1), _iota3(sq, 2)
    gr = jnp.sum(gc * (ii == jj).astype(F32), axis=1, keepdims=True)
    decay = jnp.exp(jnp.where(ii >= jj, gc - gr, -jnp.inf))
    wd = (b, c, 2 * c)
    wi, wj = _iota3(wd, 1), _iota3(wd, 2) - c
    gr_w = jnp.sum(gc * (wi == wj).astype(F32), axis=1, keepdims=True)
    decay_w = jnp.exp(jnp.where(jnp.logical_and(wj >= 0, wi > wj), gc - gr_w, -jnp.inf))
    kb = k * beta
    pad = jnp.zeros((b, c, dk), F32)
    t = _tri_inv(bmm_nt(kb, jnp.concatenate([pad, k], axis=1)) * decay_w)
    u = bmm_nn(t, jnp.concatenate([v * beta, pad], axis=1))
    wk = bmm_nn(t, jnp.concatenate([kb * jnp.exp(gc), pad], axis=1))
    attn = bmm_nt(q, k) * decay
    g_last = jnp.sum(jnp.where(_iota3((b, c, 1), 1) == c - 1, gc, 0.0), axis=1, keepdims=True)
    return u, wk, attn, q * jnp.exp(gc), k * jnp.exp(g_last - gc), jnp.exp(g_last)


def _gdn_recur(s, u, wk, attn, qg, kd, eg, zz, ng):
    v_new = u - bmm_nn(wk, s)
    o = bmm_nn(qg, s) + bmm_nn(attn, v_new)
    s_out = s * eg + bmm_tn(kd, v_new)
    o = o * lax.rsqrt(jnp.mean(o * o, axis=-1, keepdims=True) + EPS) * ng
    return o * _silu(zz), s_out


def _gdn_inputs(qkv_ref, ba, gcums, cps):
    c, hd = CHUNK, GDN_DK
    qs, ks, vs, bs, gs = [], [], [], [], []
    for ci in range(cps):
        rs = slice(c * ci, c * (ci + 1))
        for h in range(GDN_HEADS):
            qs.append(qkv_ref[rs, hd * h:hd * (h + 1)])
            ks.append(qkv_ref[rs, 512 + hd * h:512 + hd * (h + 1)])
            vs.append(qkv_ref[rs, 1024 + hd * h:1024 + hd * (h + 1)])
            bs.append(_col(ba[rs], h))
            gs.append(_col(gcums[ci], 4 + h))
    return tuple(jnp.stack(t) for t in (qs, ks, vs, bs, gs))


def _chunk_cumsum(g, transpose=False):
    c = CHUNK
    tri = (_rows((c, c)) >= _lanes((c, c))).astype(F32)
    return _dg3(tri, g, 0 if transpose else 1, 0)


def gdn_fwd(qkv, proj, alog, dt, ng, *, cps, name):
    s = qkv.shape[0]
    c = CHUNK
    nc = s // c
    hd = GDN_DK
    tb = c * cps

    def body(qkv_ref, z_ref, ba_ref, al_ref, dt_ref, ng_ref, y_ref, sall_ref, s_scr):
        @pl.when(pl.program_id(0) == 0)
        def _():
            s_scr[...] = jnp.zeros_like(s_scr)

        nh = GDN_HEADS
        ba = ba_ref[...]
        g = _gdn_g(ba, al_ref[...], dt_ref[...])
        gcums = [_chunk_cumsum(g[c * ci:c * (ci + 1)]) for ci in range(cps)]
        loc = _gdn_local(*_gdn_inputs(qkv_ref, ba, gcums, cps))
        for ci in range(cps):
            rs = slice(c * ci, c * (ci + 1))
            sl = slice(nh * ci, nh * (ci + 1))
            s_in = s_scr[...]
            sall_ref[ci] = s_in
            zz = jnp.stack([z_ref[rs, hd * h:hd * (h + 1)] for h in range(nh)])
            y, s_out = _gdn_recur(s_in, *(t[sl] for t in loc), zz, ng_ref[...])
            s_scr[...] = s_out
            for h in range(nh):
                y_ref[rs, hd * h:hd * (h + 1)] = y[h].astype(BF16)

    return pl.pallas_call(
        body, name=name, grid=(nc // cps,),
        in_specs=[pl.BlockSpec((tb, 1536), lambda i: (i, 0)), _pblk(tb, 512, C_Z), _pblk(tb, 128, C_BA),
                  _full((1, 128)), _full((1, 128)), _full((1, 128))],
        out_specs=[pl.BlockSpec((tb, 512), lambda i: (i, 0)),
                   pl.BlockSpec((cps, GDN_HEADS, hd, hd), lambda i: (i, 0, 0, 0))],
        out_shape=[jax.ShapeDtypeStruct((s, 512), BF16), jax.ShapeDtypeStruct((nc, GDN_HEADS, hd, hd), F32)],
        scratch_shapes=[pltpu.VMEM((GDN_HEADS, hd, hd), F32)],
        compiler_params=_params("arbitrary"),
    )(qkv, proj, proj, alog, dt, ng)


def gdn_bwd(qkv, proj, sall, dy, alog, dt, ng, *, cps, name):
    s = qkv.shape[0]
    c = CHUNK
    nc = s // c
    hd = GDN_DK
    tb = c * cps
    nt = nc // cps

    def body(qkv_ref, z_ref, ba_ref, sall_ref, dy_ref, al_ref, dt_ref, ng_ref,
             dqkv_ref, dz_ref, dba_ref, dal_ref, ddt_ref, dng_ref, ds_scr):
        @pl.when(pl.program_id(0) == 0)
        def _():
            ds_scr[...] = jnp.zeros_like(ds_scr)
            dal_ref[...] = jnp.zeros_like(dal_ref)
            ddt_ref[...] = jnp.zeros_like(ddt_ref)
            dng_ref[...] = jnp.zeros_like(dng_ref)

        nh = GDN_HEADS
        ba = ba_ref[...]
        g, gvf = jax.vjp(_gdn_g, ba, al_ref[...], dt_ref[...])
        gcums = [_chunk_cumsum(g[c * ci:c * (ci + 1)]) for ci in range(cps)]
        loc, loc_vf = jax.vjp(_gdn_local, *_gdn_inputs(qkv_ref, ba, gcums, cps))
        d_loc = [None] * cps
        dng = jnp.zeros((1, 128), F32)
        for ci in reversed(range(cps)):
            rs = slice(c * ci, c * (ci + 1))
            sl = slice(nh * ci, nh * (ci + 1))
            zz = jnp.stack([z_ref[rs, hd * h:hd * (h + 1)] for h in range(nh)])
            dyy = jnp.stack([dy_ref[rs, hd * h:hd * (h + 1)] for h in range(nh)])
            _, rvf = jax.vjp(_gdn_recur, sall_ref[ci], *(t[sl] for t in loc), zz, ng_ref[...])
            ds, du, dwk, dattn, dqg, dkd, deg, dzz, dn = rvf((dyy, ds_scr[...]))
            ds_scr[...] = ds
            d_loc[ci] = (du, dwk, dattn, dqg, dkd, deg)
            dng = dng + dn
            for h in range(nh):
                dz_ref[rs, hd * h:hd * (h + 1)] = dzz[h].astype(BF16)
        dq, dk, dv, dbp, dgc = loc_vf(tuple(jnp.concatenate([d_loc[ci][j] for ci in range(cps)], axis=0)
                                            for j in range(6)))
        dgs, dbas = [], []
        for ci in range(cps):
            rs = slice(c * ci, c * (ci + 1))
            dgcum = jnp.zeros((c, 128), F32)
            dba = jnp.zeros((c, 128), F32)
            for h in range(nh):
                b = nh * ci + h
                dqkv_ref[rs, hd * h:hd * (h + 1)] = dq[b]
                dqkv_ref[rs, 512 + hd * h:512 + hd * (h + 1)] = dk[b]
                dqkv_ref[rs, 1024 + hd * h:1024 + hd * (h + 1)] = dv[b]
                dgcum = dgcum + _put_col(dgc[b], 4 + h, 128)
                dba = dba + _put_col(dbp[b], h, 128)
            dgs.append(_chunk_cumsum(dgcum, transpose=True))
            dbas.append(dba)
        dba2, dal, ddt = gvf(jnp.concatenate(dgs, axis=0))
        dba_ref[...] = (jnp.concatenate(dbas, axis=0) + dba2).astype(BF16)
        dal_ref[...] += dal
        ddt_ref[...] += ddt
        dng_ref[...] += dng

    def rev(width):
        return pl.BlockSpec((tb, width), lambda i: (nt - 1 - i, 0))

    return pl.pallas_call(
        body, name=name, grid=(nt,),
        in_specs=[rev(1536), _pblk(tb, 512, C_Z, rev_n=nt), _pblk(tb, 128, C_BA, rev_n=nt),
                  pl.BlockSpec((cps, GDN_HEADS, hd, hd), lambda i: (nt - 1 - i, 0, 0, 0)), rev(512),
                  _full((1, 128)), _full((1, 128)), _full((1, 128))],
        out_specs=[rev(1536), rev(512), rev(128), _full((1, 128)), _full((1, 128)), _full((1, 128))],
        out_shape=[jax.ShapeDtypeStruct((s, 1536), F32), jax.ShapeDtypeStruct((s, 512), BF16),
                   jax.ShapeDtypeStruct((s, 128), BF16), jax.ShapeDtypeStruct((1, 128), F32),
                   jax.ShapeDtypeStruct((1, 128), F32), jax.ShapeDtypeStruct((1, 128), F32)],
        scratch_shapes=[pltpu.VMEM((GDN_HEADS, hd, hd), F32)],
        compiler_params=_params("arbitrary"),
    )(qkv, proj, proj, sall, dy, alog, dt, ng)


def merge_fwd(x, ys, proj, wb, wo, *, tb, dep=None, name):
    s, d = x.shape
    dep_specs, dep_args = _dep_operand(dep)

    def body(x_ref, ya, yb, yc, yd, g0, g1, g2, g3, wb_ref, wo_ref, *rest):
        o_ref, m_ref = rest[len(dep_args):]
        m = jnp.zeros((tb, d), F32)
        for g, (y_ref, gp_ref) in enumerate(zip((ya, yb, yc, yd), (g0, g1, g2, g3))):
            m = m + jax.nn.sigmoid(gp_ref[...]) * _dg(y_ref[...], wb_ref[g], 1, 0)
        mb = m.astype(BF16)
        m_ref[...] = mb
        o_ref[...] = x_ref[...] + _dg(mb, wo_ref[...], 1, 0)

    row = pl.BlockSpec((tb, d), lambda i: (i, 0))
    yrow = pl.BlockSpec((tb, BRANCH_W), lambda i: (i, 0))
    return pl.pallas_call(
        body, name=name, grid=(s // tb,),
        in_specs=[row] + [yrow] * 4 + [_pblk(tb, d, C_GATE + d * g) for g in range(4)]
        + [_full((4, BRANCH_W, d)), _full((d, d))] + dep_specs,
        out_specs=[row, row],
        out_shape=[jax.ShapeDtypeStruct((s, d), F32), jax.ShapeDtypeStruct((s, d), BF16)],
        compiler_params=_params("parallel"),
    )(x, *ys, proj, proj, proj, proj, wb, wo, *dep_args)


def merge_bwd(dx, ys, proj, wb, wo, *, tb, name):
    s, d = dx.shape

    def body(dx_ref, ya, yb, yc, yd, g0, g1, g2, g3, wb_ref, wo_ref,
             dgp_ref, dbr_ref, da, db, dc, dd, dxb_ref):
        dxb = dx_ref[...].astype(BF16)
        dxb_ref[...] = dxb
        dm = _dg(dxb, wo_ref[...], 1, 1)
        for g, (y_ref, gp_ref, dy_ref) in enumerate(zip((ya, yb, yc, yd), (g0, g1, g2, g3), (da, db, dc, dd))):
            br = _dg(y_ref[...], wb_ref[g], 1, 0)
            sg = jax.nn.sigmoid(gp_ref[...])
            dgp_ref[:, d * g:d * (g + 1)] = (dm * br * sg * (1.0 - sg)).astype(BF16)
            dbr = (dm * sg).astype(BF16)
            dbr_ref[g] = dbr
            dy_ref[...] = _dg(dbr, wb_ref[g], 1, 1)

    row = pl.BlockSpec((tb, d), lambda i: (i, 0))
    yrow = pl.BlockSpec((tb, BRANCH_W), lambda i: (i, 0))
    return pl.pallas_call(
        body, name=name, grid=(s // tb,),
        in_specs=[row] + [yrow] * 4 + [_pblk(tb, d, C_GATE + d * g) for g in range(4)]
        + [_full((4, BRANCH_W, d)), _full((d, d))],
        out_specs=[pl.BlockSpec((tb, 4 * d), lambda i: (i, 0)), pl.BlockSpec((4, tb, d), lambda i: (0, i, 0)),
                   yrow, yrow, yrow, yrow, row],
        out_shape=[jax.ShapeDtypeStruct((s, 4 * d), BF16), jax.ShapeDtypeStruct((4, s, d), BF16)]
        + [jax.ShapeDtypeStruct((s, BRANCH_W), F32)] * 4 + [jax.ShapeDtypeStruct((s, d), BF16)],
        compiler_params=_params("parallel"),
    )(dx, *ys, proj, proj, proj, proj, wb, wo)


def loss_head(x, g, target, *, tb, name):
    s, d = x.shape

    def body(x_ref, g_ref, t_ref, dx_ref, dg_ref, loss_ref):
        @pl.when(pl.program_id(0) == 0)
        def _():
            dg_ref[...] = jnp.zeros_like(dg_ref)
            loss_ref[...] = jnp.zeros_like(loss_ref)

        xv, gv = x_ref[...], g_ref[...]
        r = lax.rsqrt(jnp.mean(xv * xv, axis=-1, keepdims=True) + EPS)
        err = xv * r * gv - t_ref[...]
        loss_ref[...] += 0.5 * jnp.sum(jnp.mean(err * err, axis=-1, keepdims=True), axis=0, keepdims=True)
        dx, dg = _rms_bwd(xv, gv, err * (1.0 / d))
        dx_ref[...] = dx
        dg_ref[...] += dg

    row = pl.BlockSpec((tb, d), lambda i: (i, 0))
    return pl.pallas_call(
        body, name=name, grid=(s // tb,),
        in_specs=[row, _full((1, d)), row],
        out_specs=[row, _full((1, d)), _full((1, 128))],
        out_shape=[jax.ShapeDtypeStruct((s, d), F32), jax.ShapeDtypeStruct((1, d), F32),
                   jax.ShapeDtypeStruct((1, 128), F32)],
        compiler_params=_params("arbitrary"),
    )(x, g, target)


def adamw(w, g, m, v, *, name):
    shape = w.shape
    c = shape[-1]
    r = math.prod(shape[:-1])
    tr = r
    while tr * c * 4 > (1 << 20) and tr % 16 == 0:
        tr //= 2
    c1 = 1.0 / (1.0 - ADAM_B1 ** ADAM_STEP)
    c2 = 1.0 / (1.0 - ADAM_B2 ** ADAM_STEP)

    def body(w_ref, g_ref, m_ref, v_ref, d_ref, nm_ref, nv_ref):
        gv = g_ref[...]
        nm = ADAM_B1 * m_ref[...] + (1.0 - ADAM_B1) * gv
        nv = ADAM_B2 * v_ref[...] + (1.0 - ADAM_B2) * (gv * gv)
        nm_ref[...] = nm
        nv_ref[...] = nv
        d_ref[...] = -ADAM_LR * ((nm * c1) / (jnp.sqrt(nv * c2) + ADAM_EPS) + ADAM_WD * w_ref[...])

    blk = pl.BlockSpec((tr, c), lambda i: (i, 0))
    outs = pl.pallas_call(
        body, name=name, grid=(r // tr,),
        in_specs=[blk] * 4, out_specs=[blk] * 3,
        out_shape=[jax.ShapeDtypeStruct((r, c), F32)] * 3,
        compiler_params=_params("parallel"),
    )(*(t.reshape(r, c) for t in (w, g, m, v)))
    return tuple(o.reshape(shape) for o in outs)


_ANY = pl.BlockSpec(memory_space=pl.ANY)


def _place():
    x, y, c = lax.axis_index("x"), lax.axis_index("y"), lax.axis_index("c")
    return x, y, c, [(1 - x, y), (x, 1 - y), (1 - x, 1 - y)]


def gather_shards(bufs, *, name):
    n = len(bufs)

    def body(*refs):
        outs = refs[n:2 * n]
        ici_send, ici_recv, d2d_send, d2d_recv = refs[2 * n:]
        x, y, c, chips = _place()
        me = 2 * x + y

        def ici(a, j, slab, to):
            return pltpu.make_async_remote_copy(
                src_ref=outs[a].at[slab, c], dst_ref=outs[a].at[slab, c], send_sem=ici_send.at[a, j],
                recv_sem=ici_recv.at[a, j], device_id=to, device_id_type=MESH)

        def d2d(a, j, slab, half):
            return pltpu.make_async_remote_copy(
                src_ref=outs[a].at[slab, half], dst_ref=outs[a].at[slab, half], send_sem=d2d_send.at[a, j],
                recv_sem=d2d_recv.at[a, j], device_id=(x, y, 1 - c), device_id_type=MESH)

        sends = []
        for a in range(n):
            for j, (px, py) in enumerate(chips):
                cp = ici(a, j, me, (px, py, c))
                cp.start()
                sends.append(cp)
        for a in range(n):
            for j, (px, py) in enumerate(chips):
                ici(a, j, 2 * px + py, (px, py, c)).wait_recv()
                cp = d2d(a, j, 2 * px + py, c)
                cp.start()
                sends.append(cp)
        for a in range(n):
            for j, (px, py) in enumerate(chips):
                d2d(a, j, 2 * px + py, 1 - c).wait_recv()
        for cp in sends:
            cp.wait_send()

    return pl.pallas_call(
        body, name=name,
        in_specs=[_ANY] * n, out_specs=[_ANY] * n,
        out_shape=[jax.ShapeDtypeStruct(t.shape, t.dtype) for t in bufs],
        input_output_aliases={a: a for a in range(n)},
        scratch_shapes=[pltpu.SemaphoreType.DMA((n, 3))] * 4,
    )(*bufs)


_SEM = pl.BlockSpec(memory_space=pltpu.SEMAPHORE)
_VMEM = pl.BlockSpec(memory_space=pltpu.VMEM)
_EFFECT = pltpu.SideEffectType.DATAFLOW_SIDE_EFFECTING
_TOKEN = jax.ShapeDtypeStruct((8, 128), F32)


def _gather_copies(outs, sems_ici, sems_d2d):
    x, y, c, chips = _place()

    def ici(a, j, slab, to):
        return pltpu.make_async_remote_copy(
            src_ref=outs[a].at[slab, c], dst_ref=outs[a].at[slab, c], send_sem=sems_ici[0][3 * a + j],
            recv_sem=sems_ici[1][3 * a + j], device_id=to, device_id_type=MESH)

    def d2d(a, j, slab, half):
        return pltpu.make_async_remote_copy(
            src_ref=outs[a].at[slab, half], dst_ref=outs[a].at[slab, half], send_sem=sems_d2d[0][3 * a + j],
            recv_sem=sems_d2d[1][3 * a + j], device_id=(x, y, 1 - c), device_id_type=MESH)

    return x, y, c, chips, ici, d2d


def _async_call(body, bufs, sems_in, after, sems_out, token, *, name):
    n = len(bufs)
    n_out = 6 * n if sems_out else 0
    extra = [] if after is None else [after]
    return pl.pallas_call(
        body, name=name,
        in_specs=[_ANY] * n + [_SEM] * len(sems_in) + [_ANY] * len(extra),
        out_specs=[_ANY] * n + [_SEM] * n_out + [_VMEM] * token,
        out_shape=[jax.ShapeDtypeStruct(t.shape, t.dtype) for t in bufs]
        + [pltpu.SemaphoreType.DMA(())] * n_out + [_TOKEN] * token,
        input_output_aliases={a: a for a in range(n)},
        compiler_params=pltpu.CompilerParams(has_side_effects=_EFFECT),
    )(*bufs, *sems_in, *extra)


def gather_start(bufs, *, name):
    n = len(bufs)

    def body(*refs):
        outs = refs[n:2 * n]
        sems, token = refs[2 * n:8 * n], refs[8 * n]
        x, y, c, chips, ici, _ = _gather_copies(outs, (sems[:3 * n], sems[3 * n:]), None)
        for a in range(n):
            for j, (px, py) in enumerate(chips):
                ici(a, j, 2 * x + y, (px, py, c)).start()
        token[...] = jnp.zeros_like(token)

    res = _async_call(body, bufs, [], None, True, True, name=name)
    return res[:n], res[n:7 * n], res[7 * n]


def gather_pass(bufs, sems, after, *, name):
    n = len(bufs)

    def body(*refs):
        s_in = refs[n:7 * n]
        outs = refs[7 * n + 1:8 * n + 1]
        s_out, token = refs[8 * n + 1:14 * n + 1], refs[14 * n + 1]
        x, y, c, chips, ici, d2d = _gather_copies(outs, (s_in[:3 * n], s_in[3 * n:]),
                                                  (s_out[:3 * n], s_out[3 * n:]))
        for a in range(n):
            for j, (px, py) in enumerate(chips):
                ici(a, j, 2 * px + py, (px, py, c)).wait_recv()
                d2d(a, j, 2 * px + py, c).start()
        for a in range(n):
            for j, (px, py) in enumerate(chips):
                ici(a, j, 2 * x + y, (px, py, c)).wait_send()
        token[...] = jnp.zeros_like(token)

    res = _async_call(body, bufs, list(sems), after, True, True, name=name)
    return res[:n], res[n:7 * n], res[7 * n]


def gather_wait(bufs, sems, after, *, name):
    n = len(bufs)

    def body(*refs):
        s_in = refs[n:7 * n]
        outs = refs[7 * n + 1:]
        x, y, c, chips, _, d2d = _gather_copies(outs, None, (s_in[:3 * n], s_in[3 * n:]))
        for a in range(n):
            for j, (px, py) in enumerate(chips):
                d2d(a, j, 2 * px + py, 1 - c).wait_recv()
                d2d(a, j, 2 * px + py, c).wait_send()

    return _async_call(body, bufs, list(sems), after, False, False, name=name)


def pair_send_halves(arrs, *, name):
    n = len(arrs)

    def body(*refs):
        ins, outs = refs[:n], refs[n:2 * n]
        send_sems, recv_sems = refs[2 * n:]
        x, y, c, _ = _place()
        copies = []
        for a in range(n):
            for j in range(N_SHARD):
                cp = pltpu.make_async_remote_copy(
                    src_ref=ins[a].at[j, 1 - c], dst_ref=outs[a].at[j], send_sem=send_sems.at[a, j],
                    recv_sem=recv_sems.at[a, j], device_id=(x, y, 1 - c), device_id_type=MESH)
                cp.start()
                copies.append(cp)
        for cp in copies:
            cp.wait()

    return pl.pallas_call(
        body, name=name,
        in_specs=[_ANY] * n, out_specs=[_ANY] * n,
        out_shape=[jax.ShapeDtypeStruct((N_SHARD,) + t.shape[2:], t.dtype) for t in arrs],
        scratch_shapes=[pltpu.SemaphoreType.DMA((n, N_SHARD)), pltpu.SemaphoreType.DMA((n, N_SHARD))],
    )(*arrs)


def chip_scatter(arrs, *, name):
    n = len(arrs)

    def body(*refs):
        ins, outs = refs[:n], refs[n:2 * n]
        send_sems, recv_sems = refs[2 * n:]
        x, y, c, chips = _place()
        me = 2 * x + y
        copies = []
        for a in range(n):
            for j, (px, py) in enumerate(chips):
                cp = pltpu.make_async_remote_copy(
                    src_ref=ins[a].at[2 * px + py], dst_ref=outs[a].at[me], send_sem=send_sems.at[a, j],
                    recv_sem=recv_sems.at[a, j], device_id=(px, py, c), device_id_type=MESH)
                cp.start()
                copies.append(cp)
        for cp in copies:
            cp.wait()

    return pl.pallas_call(
        body, name=name,
        in_specs=[_ANY] * n, out_specs=[_ANY] * n,
        out_shape=[jax.ShapeDtypeStruct(t.shape, t.dtype) for t in arrs],
        scratch_shapes=[pltpu.SemaphoreType.DMA((n, 3)), pltpu.SemaphoreType.DMA((n, 3))],
    )(*arrs)


def pair_gather(bufs, *, name):
    n = len(bufs)

    def body(*refs):
        outs = refs[n:2 * n]
        send_sems, recv_sems = refs[2 * n:]
        x, y, c, _ = _place()
        copies = []
        for a in range(n):
            cp = pltpu.make_async_remote_copy(
                src_ref=outs[a].at[c], dst_ref=outs[a].at[c], send_sem=send_sems.at[a],
                recv_sem=recv_sems.at[a], device_id=(x, y, 1 - c), device_id_type=MESH)
            cp.start()
            copies.append(cp)
        for cp in copies:
            cp.wait()

    return pl.pallas_call(
        body, name=name,
        in_specs=[_ANY] * n, out_specs=[_ANY] * n,
        out_shape=[jax.ShapeDtypeStruct(t.shape, t.dtype) for t in bufs],
        input_output_aliases={a: a for a in range(n)},
        scratch_shapes=[pltpu.SemaphoreType.DMA((n,)), pltpu.SemaphoreType.DMA((n,))],
    )(*bufs)


def all_sum_small(v, *, name):
    m = v.shape[0]

    def body(v_ref, o_ref, buf, send_sems, recv_sems, local_sem):
        x, y, c, chips = _place()
        me, sibling = (x, y, c), (x, y, 1 - c)

        def rows(px, py, pc):
            return buf.at[pl.ds((4 * px + 2 * py + pc) * m, m), :]

        def copy(k, block, to, src=None):
            return pltpu.make_async_remote_copy(
                src_ref=rows(*block) if src is None else src, dst_ref=rows(*block),
                send_sem=send_sems.at[k], recv_sem=recv_sems.at[k], device_id=to, device_id_type=MESH)

        mine = pltpu.make_async_copy(v_ref, rows(*me), local_sem)
        mine.start()
        first = [copy(0, me, sibling, src=v_ref)]
        first += [copy(1 + j, me, (*chip, c), src=v_ref) for j, chip in enumerate(chips)]
        for cp in first:
            cp.start()
        passed = [copy(4 + j, (*chip, c), sibling) for j, chip in enumerate(chips)]
        for j, chip in enumerate(chips):
            copy(1 + j, (*chip, c), me).wait_recv()
            passed[j].start()
        copy(0, sibling, me).wait_recv()
        for j, chip in enumerate(chips):
            copy(4 + j, (*chip, 1 - c), me).wait_recv()
        for cp in first + passed:
            cp.wait_send()
        mine.wait()
        acc = buf[pl.ds(0, m), :]
        for k in range(1, 8):
            acc = acc + buf[pl.ds(k * m, m), :]
        o_ref[...] = acc

    vm = pl.BlockSpec(memory_space=pltpu.VMEM)
    return pl.pallas_call(
        body, name=name, in_specs=[vm], out_specs=vm,
        out_shape=jax.ShapeDtypeStruct((m, 128), F32),
        scratch_shapes=[pltpu.VMEM((8 * m, 128), F32), pltpu.SemaphoreType.DMA((7,)),
                        pltpu.SemaphoreType.DMA((7,)), pltpu.SemaphoreType.DMA],
    )(v)


def sum_halves(arr, recv, c_idx, *, name):
    _, _, r, c = arr.shape
    tr = r
    while tr * c * 4 > (1 << 20) and tr % 32 == 0:
        tr //= 2

    def body(c_ref, a_ref, b_ref, o_ref):
        o_ref[...] = (a_ref[...] + b_ref[...]).astype(BF16)

    return pl.pallas_call(
        body, name=name,
        grid_spec=pltpu.PrefetchScalarGridSpec(
            num_scalar_prefetch=1, grid=(N_SHARD, r // tr),
            in_specs=[pl.BlockSpec((None, None, tr, c), lambda j, i, cr: (j, cr[0], i, 0)),
                      pl.BlockSpec((None, tr, c), lambda j, i, cr: (j, i, 0))],
            out_specs=pl.BlockSpec((None, tr, c), lambda j, i, cr: (j, i, 0))),
        out_shape=jax.ShapeDtypeStruct((N_SHARD, r, c), BF16),
        compiler_params=_params("parallel", "parallel"),
    )(c_idx, arr, recv)


def sum_chips(recv, part, place, *, name):
    _, r, c = recv.shape
    tr = r
    while tr * c * 4 > (1 << 20) and tr % 32 == 0:
        tr //= 2

    def body(place_ref, a0, a1, a2, a3, p_ref, o_ref):
        me = place_ref[1]
        acc = None
        for k, a_ref in enumerate((a0, a1, a2, a3)):
            term = jnp.where(me == k, p_ref[...], a_ref[...]).astype(F32)
            acc = term if acc is None else acc + term
        o_ref[...] = acc

    def slab(k):
        return pl.BlockSpec((None, tr, c), lambda i, pr: (jnp.where(pr[1] == k, (k + 1) % N_SHARD, k), i, 0))

    return pl.pallas_call(
        body, name=name,
        grid_spec=pltpu.PrefetchScalarGridSpec(
            num_scalar_prefetch=1, grid=(r // tr,),
            in_specs=[slab(k) for k in range(N_SHARD)]
            + [pl.BlockSpec((None, tr, c), lambda i, pr: (pr[1], i, 0))],
            out_specs=pl.BlockSpec((None, tr, c), lambda i, pr: (pr[0], i, 0))),
        out_shape=jax.ShapeDtypeStruct((2, r, c), F32),
        compiler_params=_params("parallel"),
    )(place, recv, recv, recv, recv, part)


_WEIGHTS = ("ff1_norm", "ff1_wg", "ff1_wu", "ff1_wd", "mix_norm", "w_in", "sgu_ln_g", "sgu_ln_b", "sgu_w",
            "sgu_b", "lru_conv_w", "lru_conv_b", "lru_wa", "lru_ba", "lru_wx", "lru_bx", "lru_lambda",
            "gdn_conv_w", "gdn_a_log", "gdn_dt_bias", "gdn_norm_g", "pool_w", "pool_scale", "w_branch",
            "w_out", "ff2_norm", "ff2_wg", "ff2_wu", "ff2_wd", "final_norm")
_BIG = ("ff1_wg", "ff1_wu", "ff1_wd", "w_in", "w_branch", "w_out", "ff2_wg", "ff2_wu", "ff2_wd")
_SMALL = tuple(n for n in _WEIGHTS if n not in _BIG)
_CONV = ("lru_conv_w", "gdn_conv_w")


def _seg_rows(shape):
    return -(-math.prod(shape) // 1024) * 8


def _pack(arrs):
    segs = []
    for t in arrs:
        rows = _seg_rows(t.shape)
        flat = t.reshape(-1)
        segs.append(jnp.pad(flat, (0, rows * 128 - flat.shape[0])).reshape(rows, 128))
    return jnp.concatenate(segs, axis=0)


def _unpack(buf, shapes):
    out, o = [], 0
    for shp in shapes:
        rows = _seg_rows(shp)
        out.append(buf[o:o + rows].reshape(-1)[:math.prod(shp)].reshape(shp))
        o += rows
    return out


def _block_diag(w):
    h, n, _ = w.shape
    same = jnp.arange(h)[:, None, None, None] == jnp.arange(h)[None, None, :, None]
    return jnp.where(same, w[:, :, None, :], 0.0).reshape(h * n, h * n)


def _diag_blocks(m, h=8, n=64):
    return jnp.stack([m[i * n:(i + 1) * n, i * n:(i + 1) * n] for i in range(h)])


def _lane_row(v):
    return jnp.zeros((1, 128), F32).at[0, 4:8].set(v)


def _w_in_to_padded(raw):
    sem = jnp.concatenate([raw[k, :, :P_IN_SHARD] for k in range(N_SHARD)], axis=1)
    zero = jnp.zeros((sem.shape[0], C_GATE - C_BA - 8), sem.dtype)
    return jnp.concatenate([sem[:, 2048:4096], sem[:, 0:2048], sem[:, 4104:4616], sem[:, 4096:4104], zero,
                            sem[:, 4616:P_IN]], axis=1)


def _w_in_grad_to_shards(part):
    r = part.shape[1]
    dwp = part.transpose(1, 0, 2).reshape(r, P_PAD)
    sem = jnp.concatenate([dwp[:, C_AU:C_DX], dwp[:, 0:C_AU], dwp[:, C_BA:C_BA + 8], dwp[:, C_DX:C_BA],
                           dwp[:, C_GATE:]], axis=1)
    sh = sem.reshape(r, N_SHARD, P_IN_SHARD).transpose(1, 0, 2)
    return jnp.pad(sh, ((0, 0), (0, 0), (0, P_IN_SHARD_PAD - P_IN_SHARD)))


def kernel(x, ff1_norm, ff1_wg, ff1_wu, ff1_wd, mix_norm, w_in, sgu_ln_g, sgu_ln_b, sgu_w, sgu_b, lru_conv_w,
           lru_conv_b, lru_wa, lru_ba, lru_wx, lru_bx, lru_lambda, gdn_conv_w, gdn_a_log, gdn_dt_bias, gdn_norm_g,
           pool_w, pool_scale, w_branch, w_out, ff2_norm, ff2_wg, ff2_wu, ff2_wd, final_norm, loss_target, m_ff1_norm,
           m_ff1_wg, m_ff1_wu, m_ff1_wd, m_mix_norm, m_w_in, m_sgu_ln_g, m_sgu_ln_b, m_sgu_w, m_sgu_b, m_lru_conv_w,
           m_lru_conv_b, m_lru_wa, m_lru_ba, m_lru_wx, m_lru_bx, m_lru_lambda, m_gdn_conv_w, m_gdn_a_log, m_gdn_dt_bias,
           m_gdn_norm_g, m_pool_w, m_pool_scale, m_w_branch, m_w_out, m_ff2_norm, m_ff2_wg, m_ff2_wu, m_ff2_wd,
           m_final_norm, v_ff1_norm, v_ff1_wg, v_ff1_wu, v_ff1_wd, v_mix_norm, v_w_in, v_sgu_ln_g, v_sgu_ln_b, v_sgu_w,
           v_sgu_b, v_lru_conv_w, v_lru_conv_b, v_lru_wa, v_lru_ba, v_lru_wx, v_lru_bx, v_lru_lambda, v_gdn_conv_w,
           v_gdn_a_log, v_gdn_dt_bias, v_gdn_norm_g, v_pool_w, v_pool_scale, v_w_branch, v_w_out, v_ff2_norm, v_ff2_wg,
           v_ff2_wu, v_ff2_wd, v_final_norm):
    a = dict(locals())
    n_layer = ff1_norm.shape[0]
    d = D_MODEL
    x0 = x[0]
    chip = 2 * lax.axis_index("x") + lax.axis_index("y")
    c_idx = jnp.reshape(lax.axis_index("c"), (1,)).astype(jnp.int32)
    place = jnp.stack([lax.axis_index("c"), chip]).astype(jnp.int32)

    def own_slab(t, dtype=BF16):
        buf = lax.empty((N_SHARD,) + t.shape, dtype)
        return lax.dynamic_update_slice(buf, t.astype(dtype)[None], (chip,) + (0,) * t.ndim)

    def halves(t):
        return t.reshape((2, t.shape[0] // 2) + t.shape[1:])

    assert n_layer == 2
    w_in_loc = jnp.pad(w_in, ((0, 0), (0, 0), (0, P_IN_SHARD_PAD - P_IN_SHARD)))

    def layer_bufs(l):
        wgu = jnp.stack([ff1_wg[l], ff1_wu[l], ff2_wg[l], ff2_wu[l]]).reshape(2, 2, d, FF_SHARD)
        wd = jnp.stack([ff1_wd[l], ff2_wd[l]])
        return [own_slab(wgu), own_slab(wd), own_slab(halves(w_in_loc[l])), own_slab(halves(w_branch[l])),
                own_slab(halves(w_out[l]))]

    bufs = [layer_bufs(l) for l in range(n_layer)]
    *gathered0, g_lcw, g_gcw = gather_shards(bufs[0] + [own_slab(lru_conv_w, F32), own_slab(gdn_conv_w, F32)],
                                             name="gather_weights_0")
    first, bufs1 = lax.optimization_barrier((gathered0[0], bufs[1]))
    gathered = [[first] + gathered0[1:], None]
    in_flight, gather_sems, gather_token = gather_start(bufs1, name="gather_weights_1_start")

    def layer_params(l):
        g_wgu, g_wd, g_w_in, g_wb, g_wo = gathered[l]
        g_wgu = g_wgu.reshape(N_SHARD, 4, d, FF_SHARD)
        p = {}
        p["ff1"] = (g_wgu, g_wgu, g_wd, (0, 1, 0))
        p["ff2"] = (g_wgu, g_wgu, g_wd, (2, 3, 1))
        p["w_in"] = _w_in_to_padded(g_w_in.reshape(4, d, P_IN_SHARD_PAD))
        p["wb"] = g_wb.reshape(4, 4, BRANCH_W, d // 4).transpose(1, 2, 0, 3).reshape(4, BRANCH_W, d)
        p["wo"] = g_wo.reshape(d, d)
        p["lcw"] = g_lcw[:, l].transpose(1, 0, 2).reshape(4, BRANCH_W)
        p["gcw"] = g_gcw[:, l].transpose(1, 0, 2).reshape(4, 3 * BRANCH_W)
        p["lcb"] = lru_conv_b[l][None]
        p["dep1"] = gather_token if l == 0 else None
        p["bt"] = jnp.zeros((128, 128), F32).at[:, :4].set(sgu_b[l].T)
        p["wa"] = _block_diag(lru_wa[l])
        p["wx"] = _block_diag(lru_wx[l])
        p["alog"] = _lane_row(gdn_a_log[l])
        p["dt"] = _lane_row(gdn_dt_bias[l])
        return p

    saved = []
    xs = x0
    for l in range(n_layer):
        p = layer_params(l)
        sv = {"p": p, "x0": xs}
        x1 = ffn_fwd(xs, ff1_norm[l][None], *p["ff1"], tb=512, dep=p["dep1"], name=f"ffn1_fwd_{l}")
        h = norm_fwd(x1, mix_norm[l][None], tb=512, name=f"mix_norm_fwd_{l}")
        proj = matmul(h, p["w_in"], tm=1024, tn=1536, tk=1024, name=f"proj_{l}")
        ya = sgu_fwd(proj, sgu_ln_g[l][None], sgu_ln_b[l][None], sgu_w[l], p["bt"], tb=512, name=f"sgu_fwd_{l}")
        xc = conv_fwd(proj, C_BX, BRANCH_W, p["lcw"], p["lcb"], tb=512, name=f"lru_conv_fwd_{l}")
        yb, hl = lru_fwd(xc, proj, p["wa"], p["wx"], lru_ba[l][None], lru_bx[l][None], lru_lambda[l][None],
                         tb=256, name=f"lru_fwd_{l}")
        qkv = conv_fwd(proj, C_QKV, 3 * BRANCH_W, p["gcw"], jnp.zeros((1, 3 * BRANCH_W), F32), tb=512,
                       name=f"gdn_conv_fwd_{l}")
        yc, sall = gdn_fwd(qkv, proj, p["alog"], p["dt"], gdn_norm_g[l][None], cps=GDN_CPS, name=f"gdn_fwd_{l}")
        dep2 = None
        if l == 0:
            in_flight, gather_sems, dep2 = gather_pass(in_flight, gather_sems, yc, name="gather_weights_1_pass")
        yd = pool_fwd(proj, pool_w[l], pool_scale[l][None], tb=512, name=f"pool_fwd_{l}")
        ys = (ya, yb, yc, yd)
        x2, merged = merge_fwd(x1, ys, proj, p["wb"], p["wo"], tb=256, dep=dep2, name=f"merge_fwd_{l}")
        x3 = ffn_fwd(x2, ff2_norm[l][None], *p["ff2"], tb=512, name=f"ffn2_fwd_{l}")
        if l == 0:
            gathered[1] = gather_wait(in_flight, gather_sems, x3, name="gather_weights_1_wait")
        sv.update(x1=x1, h=h, proj=proj, xc=xc, hl=hl, qkv=qkv, sall=sall, ys=ys, merged=merged, x2=x2)
        saved.append(sv)
        xs = x3

    dx, d_final, loss_blk = loss_head(xs, final_norm[None], loss_target[0], tb=512, name="loss_head")
    loss = lax.psum(loss_blk[0, 0], ("x", "y", "c"))

    small_g = {n: [None] * n_layer for n in _SMALL if n != "final_norm"}
    big_g = [None] * n_layer
    for l in reversed(range(n_layer)):
        sv = saved[l]
        p = sv["p"]
        proj = sv["proj"]
        dx2, dg_ff2, dab2, hid2, h2, dy2 = ffn_bwd(sv["x2"], dx, ff2_norm[l][None], *p["ff2"],
                                                  tb=512, name=f"ffn2_bwd_{l}")
        dwgu2 = matmul(h2, dab2, ta=True, tm=1024, tn=FF_SHARD, tk=TK_DW, name=f"ffn2_dwgu_{l}")
        dwd2 = matmul(hid2, dy2, ta=True, tm=FF_SHARD, tn=1024, tk=TK_DW, name=f"ffn2_dwd_{l}")
        dgp, dbr, dya, dyb, dyc, dyd, dx2b = merge_bwd(dx2, sv["ys"], proj, p["wb"], p["wo"], tb=256,
                                                       name=f"merge_bwd_{l}")
        dwo = matmul(sv["merged"], dx2b, ta=True, tm=1024, tn=1024, tk=TK_DW, name=f"dwo_{l}")
        dwb = jnp.stack([matmul(sv["ys"][g], dbr[g], ta=True, tm=BRANCH_W, tn=1024, tk=TK_DW, name=f"dwb{g}_{l}")
                         for g in range(4)])
        du, dv, dlg, dlb, dws, dbt = sgu_bwd(proj, dya, sgu_ln_g[l][None], sgu_ln_b[l][None], sgu_w[l], p["bt"],
                                             tb=512, name=f"sgu_bwd_{l}")
        dxc, dgate, dwa, dwx, dba, dbx, dlam = lru_bwd(sv["xc"], proj, sv["hl"], dyb, p["wa"], p["wx"],
                                                       lru_ba[l][None], lru_bx[l][None], lru_lambda[l][None],
                                                       tb=256, name=f"lru_bwd_{l}")
        dbx_in, dcw_l = conv_bwd(proj, C_BX, BRANCH_W, dxc, p["lcw"], tb=512, name=f"lru_conv_bwd_{l}")
        dqkv, dz, dbeta, dal, ddt, dng = gdn_bwd(sv["qkv"], proj, sv["sall"], dyc, p["alog"], p["dt"],
                                                 gdn_norm_g[l][None], cps=GDN_CPS, name=f"gdn_bwd_{l}")
        dqkv_in, dcw_g = conv_bwd(proj, C_QKV, 3 * BRANCH_W, dqkv, p["gcw"], tb=512, name=f"gdn_conv_bwd_{l}")
        ddx, dwp, dsc = pool_bwd(proj, dyd, pool_w[l], pool_scale[l][None], tb=512, name=f"pool_bwd_{l}")
        s = dx.shape[0]
        dproj = jnp.concatenate([dqkv_in, dz, du, dv, dbx_in, dgate, ddx, dbeta,
                                 jnp.zeros((s, C_GATE - C_BA - 128), BF16), dgp], axis=1)
        dh = matmul(dproj, p["w_in"], tb=True, tm=1024, tn=1024, tk=1536, name=f"dh_mix_{l}")
        dw_in = matmul(sv["h"], dproj, ta=True, tm=1024, tn=1152, tk=TK_DW, col_groups=N_SHARD, name=f"dw_in_{l}")
        dx1, dg_mix = norm_bwd(sv["x1"], mix_norm[l][None], dh, dx2, tb=512, name=f"mix_norm_bwd_{l}")
        dx, dg_ff1, dab1, hid1, h1, dy1 = ffn_bwd(sv["x0"], dx1, ff1_norm[l][None], *p["ff1"],
                                                 tb=512, name=f"ffn1_bwd_{l}")
        dwgu1 = matmul(h1, dab1, ta=True, tm=1024, tn=FF_SHARD, tk=TK_DW, name=f"ffn1_dwgu_{l}")
        dwd1 = matmul(hid1, dy1, ta=True, tm=FF_SHARD, tn=1024, tk=TK_DW, name=f"ffn1_dwd_{l}")

        for n, g in (("ff1_norm", dg_ff1[0]), ("mix_norm", dg_mix[0]), ("sgu_ln_g", dlg[0]), ("sgu_ln_b", dlb[0]),
                     ("sgu_w", dws), ("sgu_b", dbt[:, :4].T), ("lru_conv_w", dcw_l[:4]), ("lru_conv_b", dcw_l[4]),
                     ("lru_wa", _diag_blocks(dwa)), ("lru_ba", dba[0]), ("lru_wx", _diag_blocks(dwx)),
                     ("lru_bx", dbx[0]), ("lru_lambda", dlam[0]), ("gdn_conv_w", dcw_g[:4]),
                     ("gdn_a_log", dal[0, 4:8]), ("gdn_dt_bias", ddt[0, 4:8]), ("gdn_norm_g", dng[0]),
                     ("pool_w", dwp), ("pool_scale", dsc[0]), ("ff2_norm", dg_ff2[0])):
            small_g[n][l] = g
        dwb_sh = dwb.reshape(4, BRANCH_W, N_SHARD, d // N_SHARD).transpose(2, 0, 1, 3)
        big_g[l] = [dwgu1, dwd1.reshape(N_SHARD, 2, FF_SHARD // 2, d),
                    dw_in.reshape(N_SHARD, 2, d // 2, P_PAD // N_SHARD),
                    dwb_sh.reshape(N_SHARD, 2, 2 * BRANCH_W, d // N_SHARD),
                    dwo.reshape(N_SHARD, 2, d // 8, d), dwgu2,
                    dwd2.reshape(N_SHARD, 2, FF_SHARD // 2, d)]

    bigs = [t for l in range(n_layer) for t in big_g[l]]
    recv1 = pair_send_halves(bigs, name="grad_pair_exchange")
    part = [sum_halves(t, r, c_idx, name=f"grad_pair_sum_{i}") for i, (t, r) in enumerate(zip(bigs, recv1))]
    per_layer = len(big_g[0])
    for l in range(n_layer):
        part[l * per_layer + 2] = _w_in_grad_to_shards(part[l * per_layer + 2])
    recv2 = chip_scatter(part, name="grad_chip_exchange")
    red = [sum_chips(r, p, place, name=f"grad_chip_sum_{i}") for i, (r, p) in enumerate(zip(recv2, part))]
    full = pair_gather(red, name="grad_pair_gather")
    per_layer = len(big_g[0])
    grads = {}

    def layers(k, f):
        return jnp.stack([f(full[l * per_layer + k]) for l in range(n_layer)])

    grads["ff1_wg"] = layers(0, lambda t: t[0])
    grads["ff1_wu"] = layers(0, lambda t: t[1])
    grads["ff1_wd"] = layers(1, lambda t: t.reshape(FF_SHARD, d))
    grads["w_in"] = layers(2, lambda t: t.reshape(d, P_IN_SHARD_PAD)[:, :P_IN_SHARD])
    grads["w_branch"] = layers(3, lambda t: t.reshape(4, BRANCH_W, d // N_SHARD))
    grads["w_out"] = layers(4, lambda t: t.reshape(d // N_SHARD, d))
    grads["ff2_wg"] = layers(5, lambda t: t[0])
    grads["ff2_wu"] = layers(5, lambda t: t[1])
    grads["ff2_wd"] = layers(6, lambda t: t.reshape(FF_SHARD, d))

    small_full = [jnp.stack(small_g[n]) for n in _SMALL if n != "final_norm"] + [d_final[0]]
    summed = _unpack(all_sum_small(_pack(small_full), name="grad_small_allreduce"), [t.shape for t in small_full])
    for n, g in zip(_SMALL, summed):
        if n in _CONV:
            width = a[n].shape[-1]
            g = lax.dynamic_slice_in_dim(g, chip * width, width, axis=2)
        grads[n] = g

    delta, new_m, new_v = {}, {}, {}
    for n in _BIG:
        delta[n], new_m[n], new_v[n] = adamw(a[n], grads[n], a["m_" + n], a["v_" + n], name=f"adamw_{n}")
    shapes = [a[n].shape for n in _SMALL]
    packed = [_pack([src[pre + n] for n in _SMALL]) for src, pre in ((a, ""), (grads, ""), (a, "m_"), (a, "v_"))]
    for store, buf in zip((delta, new_m, new_v), adamw(*packed, name="adamw_small")):
        store.update(zip(_SMALL, _unpack(buf, shapes)))

    return (loss, dx[None], *[grads[n] for n in _WEIGHTS], *[delta[n] for n in _WEIGHTS],
            *[new_m[n] for n in _WEIGHTS], *[new_v[n] for n in _WEIGHTS])
```

```python
import functools
import math

import jax
import jax.numpy as jnp
from jax import lax
from jax.experimental import pallas as pl
from jax.experimental.pallas import tpu as pltpu

F32 = jnp.float32
BF16 = jnp.bfloat16
MESH = pl.DeviceIdType.MESH

D_MODEL = 1024
N_SHARD = 4
D_FF = 2816
FF_SHARD = D_FF // N_SHARD
BRANCH_W = 512
CHUNK = 64
GDN_HEADS = 4
GDN_DK = 128
GDN_CPS = 4
TK_DW = 2048
LRU_C = 8.0
EPS = 1e-6
P_IN = 8712
P_IN_SHARD = P_IN // N_SHARD
P_IN_SHARD_PAD = 2304
P_PAD = 9216
C_QKV, C_Z, C_AU, C_AV, C_BX, C_BG, C_DX, C_BA, C_GATE = 0, 1536, 2048, 2560, 3072, 3584, 4096, 4608, 5120

ADAM_LR, ADAM_B1, ADAM_B2, ADAM_EPS, ADAM_WD, ADAM_STEP = 0.001, 0.9, 0.999, 1e-08, 0.01, 10


def _gelu(x):
    return 0.5 * x * (1.0 + jnp.tanh(0.7978845608028654 * (x + 0.044715 * (x * x * x))))


def _gelu_grad(x):
    u = 0.7978845608028654 * (x + 0.044715 * (x * x * x))
    t = jnp.tanh(u)
    return 0.5 * (1.0 + t) + 0.5 * x * (1.0 - t * t) * 0.7978845608028654 * (1.0 + 3.0 * 0.044715 * x * x)


def _silu(x):
    return x * jax.nn.sigmoid(x)


def _softplus(x):
    return jnp.maximum(x, 0.0) + jnp.log1p(jnp.exp(-jnp.abs(x)))


def _dg(a, b, ca, cb):
    return lax.dot_general(a.astype(BF16), b.astype(BF16), (((ca,), (cb,)), ((), ())),
                           preferred_element_type=F32)


def _dg3(a, b, ca, cb):
    a1 = a.astype(BF16)
    a2 = (a - a1.astype(F32)).astype(BF16)
    b1 = b.astype(BF16)
    b2 = (b - b1.astype(F32)).astype(BF16)
    dn = (((ca,), (cb,)), ((), ()))
    d = functools.partial(lax.dot_general, dimension_numbers=dn, preferred_element_type=F32)
    return d(a1, b1) + (d(a1, b2) + d(a2, b1))


def _make_mm(ca, cb, dot):
    @jax.custom_vjp
    def f(a, b):
        return dot(a, b, ca, cb)

    def fwd(a, b):
        return dot(a, b, ca, cb), (a, b)

    def bwd(res, g):
        a, b = res
        if (ca, cb) == (1, 0):
            return dot(g, b, 1, 1), dot(a, g, 0, 0)
        if (ca, cb) == (1, 1):
            return dot(g, b, 1, 0), dot(g, a, 0, 0)
        return dot(b, g, 1, 1), dot(a, g, 1, 0)

    f.defvjp(fwd, bwd)
    return f


mm_nn = _make_mm(1, 0, _dg)


def _bdg(a, b, ca, cb):
    return lax.dot_general(a.astype(BF16), b.astype(BF16), (((ca,), (cb,)), ((0,), (0,))),
                           preferred_element_type=F32)


def _split2(x):
    x1 = x.astype(BF16)
    return x1, (x - x1.astype(F32)).astype(BF16)


def _bdg3(a, b, ca, cb):
    a1, a2 = _split2(a)
    b1, b2 = _split2(b)
    d = functools.partial(lax.dot_general, dimension_numbers=(((ca,), (cb,)), ((0,), (0,))),
                          preferred_element_type=F32)
    return d(a1, b1) + (d(a1, b2) + d(a2, b1))


def _make_bmm(ca, cb):
    @jax.custom_vjp
    def f(a, b):
        return _bdg(a, b, ca, cb)

    def fwd(a, b):
        return _bdg(a, b, ca, cb), (a, b)

    def bwd(res, g):
        a, b = res
        if (ca, cb) == (2, 1):
            return _bdg(g, b, 2, 2), _bdg(a, g, 1, 1)
        if (ca, cb) == (2, 2):
            return _bdg(g, b, 2, 1), _bdg(g, a, 1, 1)
        return _bdg(b, g, 2, 2), _bdg(a, g, 2, 1)

    f.defvjp(fwd, bwd)
    return f


bmm_nn = _make_bmm(2, 1)
bmm_nt = _make_bmm(2, 2)
bmm_tn = _make_bmm(1, 1)


def _iota3(shape, dim):
    return lax.broadcasted_iota(jnp.int32, shape, dim)


@jax.custom_vjp
def _tri_inv(aw):
    b, n, _ = aw.shape
    shp = (b, n, 2 * n)
    left = _iota3(shp, 2) < n
    x = jnp.where(_iota3(shp, 1) == _iota3(shp, 2), 1.0, 0.0) - aw
    zero = jnp.zeros((b, n, 4 * n), BF16)
    k = 1
    while k < n:
        x1, x2 = _split2(x)
        lhs = jnp.concatenate([x1, x2], axis=1)
        rhs = jnp.concatenate([zero, jnp.concatenate([x1, x2], axis=2)], axis=1)
        r = lax.dot_general(lhs, rhs, (((2,), (1,)), ((0,), (0,))), preferred_element_type=F32)
        mx = (r[:, :n, :2 * n] + r[:, :n, 2 * n:]) + (r[:, n:, :2 * n] + r[:, n:, 2 * n:])
        x = jnp.where(left, x, 0.0) + mx
        k *= 2
    return jnp.where(left, x, 0.0)


def _tri_inv_fwd(aw):
    t = _tri_inv(aw)
    return t, t


def _tri_inv_bwd(t, dt):
    n = t.shape[1]
    x = _bdg3(t, dt, 1, 1)[:, :n]
    tp = jnp.concatenate([jnp.zeros_like(t), t], axis=1)
    return (-_bdg3(x, tp, 2, 2),)


_tri_inv.defvjp(_tri_inv_fwd, _tri_inv_bwd)


def _rows(shape):
    return lax.broadcasted_iota(jnp.int32, shape, 0)


def _lanes(shape):
    return lax.broadcasted_iota(jnp.int32, shape, 1)


def _col(x, j):
    return jnp.sum(jnp.where(_lanes(x.shape) == j, x, 0.0), axis=1, keepdims=True)


def _put_col(col, j, width):
    shape = (col.shape[0], width)
    return jnp.where(_lanes(shape) == j, jnp.broadcast_to(col, shape), 0.0)


VMEM_BYTES_V7X = 64 << 20


def _params(*sem, vmem_bytes=None):
    return pltpu.CompilerParams(dimension_semantics=sem, vmem_limit_bytes=vmem_bytes)


def matmul(a, b, *, ta=False, tb=False, tm, tn, tk, out_dtype=F32, col_groups=1, name):
    ba, bb = a.shape[:-2], b.shape[:-2]
    batch = ba if len(ba) >= len(bb) else bb
    assert ba in ((), batch) and bb in ((), batch)
    nb = len(batch)
    (m, k) = (a.shape[-1], a.shape[-2]) if ta else (a.shape[-2], a.shape[-1])
    (k2, n) = (b.shape[-1], b.shape[-2]) if tb else (b.shape[-2], b.shape[-1])
    tm, tn, tk = min(tm, m), min(tn, n), min(tk, k)
    assert k == k2 and m % tm == 0 and n % tn == 0 and k % tk == 0, (a.shape, b.shape, tm, tn, tk)
    nk = k // tk
    grid = batch + (m // tm, n // tn, nk)

    def a_map(*g):
        i, kk = g[nb], g[nb + 2]
        return (g[:nb] if ba else ()) + ((kk, i) if ta else (i, kk))

    def b_map(*g):
        j, kk = g[nb + 1], g[nb + 2]
        return (g[:nb] if bb else ()) + ((j, kk) if tb else (kk, j))

    per_group = n // tn // col_groups
    assert col_groups == 1 or (nb == 0 and per_group * col_groups * tn == n)

    def o_map(*g):
        if col_groups > 1:
            return (g[1] // per_group, g[0], g[1] % per_group)
        return g[:nb] + (g[nb], g[nb + 1])

    a_blk = (None,) * len(ba) + ((tk, tm) if ta else (tm, tk))
    b_blk = (None,) * len(bb) + ((tn, tk) if tb else (tk, tn))
    o_blk = (None,) * (nb + (col_groups > 1)) + (tm, tn)
    o_shape = (col_groups, m, n // col_groups) if col_groups > 1 else batch + (m, n)

    def body(a_ref, b_ref, o_ref, acc_ref):
        kk = pl.program_id(nb + 2)

        @pl.when(kk == 0)
        def _():
            acc_ref[...] = jnp.zeros_like(acc_ref)

        acc_ref[...] += _dg(a_ref[...], b_ref[...], 0 if ta else 1, 1 if tb else 0)

        @pl.when(kk == nk - 1)
        def _():
            o_ref[...] = acc_ref[...].astype(o_ref.dtype)

    return pl.pallas_call(
        body, name=name, grid=grid,
        in_specs=[pl.BlockSpec(a_blk, a_map), pl.BlockSpec(b_blk, b_map)],
        out_specs=pl.BlockSpec(o_blk, o_map),
        out_shape=jax.ShapeDtypeStruct(o_shape, out_dtype),
        scratch_shapes=[pltpu.VMEM((tm, tn), F32)],
        compiler_params=_params(*(("parallel",) * (nb + 2) + ("arbitrary",))),
    )(a, b)


def norm_fwd(x, g, *, tb, name):
    s, d = x.shape

    def body(x_ref, g_ref, h_ref):
        xv = x_ref[...]
        r = lax.rsqrt(jnp.mean(xv * xv, axis=-1, keepdims=True) + EPS)
        h_ref[...] = (xv * r * g_ref[...]).astype(BF16)

    return pl.pallas_call(
        body, name=name, grid=(s // tb,),
        in_specs=[pl.BlockSpec((tb, d), lambda i: (i, 0)), pl.BlockSpec((1, d), lambda i: (0, 0))],
        out_specs=pl.BlockSpec((tb, d), lambda i: (i, 0)),
        out_shape=jax.ShapeDtypeStruct((s, d), BF16),
        compiler_params=_params("parallel"),
    )(x, g)


def _rms_bwd(xv, gv, dh):
    r = lax.rsqrt(jnp.mean(xv * xv, axis=-1, keepdims=True) + EPS)
    xh = xv * r
    dxh = dh * gv
    dx = r * (dxh - xh * jnp.mean(dxh * xh, axis=-1, keepdims=True))
    return dx, jnp.sum(dh * xh, axis=0, keepdims=True)


def norm_bwd(x, g, dh, dres, *, tb, name):
    s, d = x.shape

    def body(x_ref, g_ref, dh_ref, dres_ref, dx_ref, dg_ref):
        dx, dg = _rms_bwd(x_ref[...], g_ref[...], dh_ref[...])
        dx_ref[...] = dres_ref[...] + dx

        @pl.when(pl.program_id(0) == 0)
        def _():
            dg_ref[...] = jnp.zeros_like(dg_ref)

        dg_ref[...] += dg

    row = pl.BlockSpec((tb, d), lambda i: (i, 0))
    one = pl.BlockSpec((1, d), lambda i: (0, 0))
    return pl.pallas_call(
        body, name=name, grid=(s // tb,),
        in_specs=[row, one, row, row], out_specs=[row, one],
        out_shape=[jax.ShapeDtypeStruct((s, d), F32), jax.ShapeDtypeStruct((1, d), F32)],
        compiler_params=_params("arbitrary"),
    )(x, g, dh, dres)


def _dep_operand(dep):
    return ([], []) if dep is None else ([_ANY], [dep])


def ffn_fwd(x, g, wg, wu, wd, layer, *, tb, dep=None, name):
    s, d = x.shape
    fc = wd.shape[-2]
    dep_specs, dep_args = _dep_operand(dep)

    def body(x_ref, g_ref, wg_ref, wu_ref, wd_ref, *rest):
        o_ref, h_scr, acc_scr = rest[len(dep_args):]
        j = pl.program_id(1)

        @pl.when(j == 0)
        def _():
            xv = x_ref[...]
            r = lax.rsqrt(jnp.mean(xv * xv, axis=-1, keepdims=True) + EPS)
            h_scr[...] = (xv * r * g_ref[...]).astype(BF16)
            acc_scr[...] = jnp.zeros_like(acc_scr)

        h = h_scr[...]
        a = _dg(h, wg_ref[...], 1, 0)
        b = _dg(h, wu_ref[...], 1, 0)
        acc_scr[...] += _dg(_silu(a) * b, wd_ref[...], 1, 0)

        @pl.when(j == N_SHARD - 1)
        def _():
            o_ref[...] = x_ref[...] + 0.5 * acc_scr[...]

    row = pl.BlockSpec((tb, d), lambda i, j: (i, 0))
    lg, lu, ld = layer
    return pl.pallas_call(
        body, name=name, grid=(s // tb, N_SHARD),
        in_specs=[row, pl.BlockSpec((1, d), lambda i, j: (0, 0)),
                  pl.BlockSpec((None, None, d, fc), lambda i, j: (j, lg, 0, 0)),
                  pl.BlockSpec((None, None, d, fc), lambda i, j: (j, lu, 0, 0)),
                  pl.BlockSpec((None, None, fc, d), lambda i, j: (j, ld, 0, 0))] + dep_specs,
        out_specs=row,
        out_shape=jax.ShapeDtypeStruct((s, d), F32),
        scratch_shapes=[pltpu.VMEM((tb, d), BF16), pltpu.VMEM((tb, d), F32)],
        compiler_params=_params("parallel", "arbitrary"),
    )(x, g, wg, wu, wd, *dep_args)


def ffn_bwd(x, dxo, g, wg, wu, wd, layer, *, tb, dep=None, name):
    s, d = x.shape
    fc = wd.shape[-2]
    dep_specs, dep_args = _dep_operand(dep)

    def body(x_ref, dxo_ref, g_ref, wg_ref, wu_ref, wd_ref, *rest):
        dx_ref, dg_ref, dab_ref, hid_ref, h_ref, dy_ref, dh_scr = rest[len(dep_args):]
        i, j = pl.program_id(0), pl.program_id(1)

        @pl.when(j == 0)
        def _():
            xv = x_ref[...]
            r = lax.rsqrt(jnp.mean(xv * xv, axis=-1, keepdims=True) + EPS)
            h_ref[...] = (xv * r * g_ref[...]).astype(BF16)
            dy_ref[...] = (0.5 * dxo_ref[...]).astype(BF16)
            dh_scr[...] = jnp.zeros_like(dh_scr)

        h = h_ref[...]
        a = _dg(h, wg_ref[...], 1, 0)
        b = _dg(h, wu_ref[...], 1, 0)
        sg = jax.nn.sigmoid(a)
        sl = a * sg
        dhid = _dg(dy_ref[...], wd_ref[...], 1, 1)
        da = (dhid * b * (sg * (1.0 + a * (1.0 - sg)))).astype(BF16)
        db = (dhid * sl).astype(BF16)
        dab_ref[0] = da
        dab_ref[1] = db
        hid_ref[...] = (sl * b).astype(BF16)
        dh_scr[...] += _dg(da, wg_ref[...], 1, 1) + _dg(db, wu_ref[...], 1, 1)

        @pl.when(jnp.logical_and(i == 0, j == 0))
        def _():
            dg_ref[...] = jnp.zeros_like(dg_ref)

        @pl.when(j == N_SHARD - 1)
        def _():
            dx, dg = _rms_bwd(x_ref[...], g_ref[...], dh_scr[...])
            dx_ref[...] = dxo_ref[...] + dx
            dg_ref[...] += dg

    row = pl.BlockSpec((tb, d), lambda i, j: (i, 0))
    one = pl.BlockSpec((1, d), lambda i, j: (0, 0))
    lg, lu, ld = layer
    return pl.pallas_call(
        body, name=name, grid=(s // tb, N_SHARD),
        in_specs=[row, row, one,
                  pl.BlockSpec((None, None, d, fc), lambda i, j: (j, lg, 0, 0)),
                  pl.BlockSpec((None, None, d, fc), lambda i, j: (j, lu, 0, 0)),
                  pl.BlockSpec((None, None, fc, d), lambda i, j: (j, ld, 0, 0))] + dep_specs,
        out_specs=[row, one,
                   pl.BlockSpec((None, 2, tb, fc), lambda i, j: (j, 0, i, 0)),
                   pl.BlockSpec((None, tb, fc), lambda i, j: (j, i, 0)),
                   row, row],
        out_shape=[jax.ShapeDtypeStruct((s, d), F32), jax.ShapeDtypeStruct((1, d), F32),
                   jax.ShapeDtypeStruct((N_SHARD, 2, s, fc), BF16),
                   jax.ShapeDtypeStruct((N_SHARD, s, fc), BF16),
                   jax.ShapeDtypeStruct((s, d), BF16), jax.ShapeDtypeStruct((s, d), BF16)],
        scratch_shapes=[pltpu.VMEM((tb, d), F32)],
        compiler_params=_params("arbitrary", "arbitrary",
                                vmem_bytes=min(VMEM_BYTES_V7X - (8 << 20),
                                               12 * d * fc + 44 * tb * d + 40 * tb * fc + (4 << 20))),
    )(x, dxo, g, wg, wu, wd, *dep_args)


def _pblk(tb, width, col0, rev_n=None):
    assert col0 % width == 0
    cb = col0 // width
    if rev_n is None:
        return pl.BlockSpec((tb, width), lambda i: (i, cb))
    return pl.BlockSpec((tb, width), lambda i: (rev_n - 1 - i, cb))


def _full(shape):
    return pl.BlockSpec(shape, lambda i: (0,) * len(shape))


def conv_fwd(src, col0, width, w, bias, *, tb, name):
    s = src.shape[0]
    cb = col0 // width

    def body(x_ref, p_ref, w_ref, b_ref, y_ref):
        i = pl.program_id(0)
        prev = jnp.where(i > 0, p_ref[...], 0.0)
        xe = jnp.concatenate([prev, x_ref[...]], axis=0)
        wv = w_ref[...]
        acc = b_ref[...] + wv[3:4] * xe[8:]
        for k in range(3):
            acc = acc + wv[k:k + 1] * pltpu.roll(xe, 3 - k, 0)[8:]
        y_ref[...] = acc

    return pl.pallas_call(
        body, name=name, grid=(s // tb,),
        in_specs=[_pblk(tb, width, col0),
                  pl.BlockSpec((8, width), lambda i: (jnp.maximum(i * (tb // 8) - 1, 0), cb)),
                  _full((4, width)), _full((1, width))],
        out_specs=pl.BlockSpec((tb, width), lambda i: (i, 0)),
        out_shape=jax.ShapeDtypeStruct((s, width), F32),
        compiler_params=_params("parallel"),
    )(src, src, w, bias)


def conv_bwd(src, col0, width, dpre, w, *, tb, name):
    s = src.shape[0]
    cb = col0 // width
    nt = s // tb

    def body(x_ref, p_ref, d_ref, n_ref, w_ref, dx_ref, dwb_ref):
        i = pl.program_id(0)
        d = d_ref[...]
        nxt = jnp.where(i < nt - 1, n_ref[...], 0.0)
        de = jnp.concatenate([d, nxt], axis=0)
        wv = w_ref[...]
        dx = wv[3:4] * d
        for k in range(3):
            dx = dx + wv[k:k + 1] * pltpu.roll(de, tb + 8 - (3 - k), 0)[:tb]
        dx_ref[...] = dx.astype(BF16)

        @pl.when(i == 0)
        def _():
            dwb_ref[...] = jnp.zeros_like(dwb_ref)

        prev = jnp.where(i > 0, p_ref[...], 0.0)
        xe = jnp.concatenate([prev, x_ref[...]], axis=0)
        dwb_ref[3:4, :] += jnp.sum(d * xe[8:], axis=0, keepdims=True)
        for k in range(3):
            dwb_ref[k:k + 1, :] += jnp.sum(d * pltpu.roll(xe, 3 - k, 0)[8:], axis=0, keepdims=True)
        dwb_ref[4:5, :] += jnp.sum(d, axis=0, keepdims=True)

    return pl.pallas_call(
        body, name=name, grid=(nt,),
        in_specs=[_pblk(tb, width, col0),
                  pl.BlockSpec((8, width), lambda i: (jnp.maximum(i * (tb // 8) - 1, 0), cb)),
                  pl.BlockSpec((tb, width), lambda i: (i, 0)),
                  pl.BlockSpec((8, width), lambda i: (jnp.minimum((i + 1) * (tb // 8), s // 8 - 1), 0)),
                  _full((4, width))],
        out_specs=[pl.BlockSpec((tb, width), lambda i: (i, 0)), _full((8, width))],
        out_shape=[jax.ShapeDtypeStruct((s, width), BF16), jax.ShapeDtypeStruct((8, width), F32)],
        compiler_params=_params("arbitrary"),
    )(src, src, dpre, dpre, w)


SGU_BLOCK = 128


def _sgu_pre(u, v, lg, lb):
    ug = _gelu(u)
    vg = _gelu(v)
    mu = jnp.mean(vg, axis=-1, keepdims=True)
    var = jnp.mean(jnp.square(vg - mu), axis=-1, keepdims=True)
    return ug, (vg - mu) * lax.rsqrt(var + EPS) * lg + lb


def _sgu_mask():
    shp = (SGU_BLOCK, SGU_BLOCK)
    return (_rows(shp) // CHUNK) >= (_lanes(shp) // CHUNK)


def sgu_fwd(proj, lg, lb, ws, bt, *, tb, name):
    s = proj.shape[0]
    w = BRANCH_W

    def body(u_ref, v_ref, lg_ref, lb_ref, w_ref, b_ref, y_ref):
        ug, vn = _sgu_pre(u_ref[...], v_ref[...], lg_ref[...], lb_ref[...])
        mask = _sgu_mask()
        for g in range(4):
            wm = jnp.where(mask, w_ref[g], 0.0)
            bcol = _col(b_ref[...], g)
            cs = slice(128 * g, 128 * g + 128)
            for n in range(tb // SGU_BLOCK):
                rs = slice(SGU_BLOCK * n, SGU_BLOCK * (n + 1))
                mixed = _dg(wm, vn[rs, cs], 1, 0) + bcol
                y_ref[rs, cs] = (ug[rs, cs] * mixed).astype(BF16)

    return pl.pallas_call(
        body, name=name, grid=(s // tb,),
        in_specs=[_pblk(tb, w, C_AU), _pblk(tb, w, C_AV), _full((1, w)), _full((1, w)),
                  _full((4, 128, 128)), _full((128, 128))],
        out_specs=pl.BlockSpec((tb, w), lambda i: (i, 0)),
        out_shape=jax.ShapeDtypeStruct((s, w), BF16),
        compiler_params=_params("parallel"),
    )(proj, proj, lg, lb, ws, bt)


def sgu_bwd(proj, dy, lg, lb, ws, bt, *, tb, name):
    s = proj.shape[0]
    w = BRANCH_W

    def body(u_ref, v_ref, dy_ref, lg_ref, lb_ref, w_ref, b_ref,
             du_ref, dv_ref, dlg_ref, dlb_ref, dw_ref, db_ref, dug_scr, dvn_scr):
        @pl.when(pl.program_id(0) == 0)
        def _():
            dlg_ref[...] = jnp.zeros_like(dlg_ref)
            dlb_ref[...] = jnp.zeros_like(dlb_ref)
            dw_ref[...] = jnp.zeros_like(dw_ref)
            db_ref[...] = jnp.zeros_like(db_ref)

        (ug, vn), vf = jax.vjp(_sgu_pre, u_ref[...], v_ref[...], lg_ref[...], lb_ref[...])
        dyv = dy_ref[...]
        mask = _sgu_mask()
        for g in range(4):
            wm = jnp.where(mask, w_ref[g], 0.0)
            bcol = _col(b_ref[...], g)
            cs = slice(128 * g, 128 * g + 128)
            dwg = jnp.zeros((SGU_BLOCK, SGU_BLOCK), F32)
            dbc = jnp.zeros((SGU_BLOCK, 1), F32)
            for n in range(tb // SGU_BLOCK):
                rs = slice(SGU_BLOCK * n, SGU_BLOCK * (n + 1))
                blk = vn[rs, cs]
                mixed = _dg(wm, blk, 1, 0) + bcol
                dmix = dyv[rs, cs] * ug[rs, cs]
                dug_scr[rs, cs] = dyv[rs, cs] * mixed
                dvn_scr[rs, cs] = _dg(wm, dmix, 0, 0)
                dwg = dwg + _dg(dmix, blk, 1, 1)
                dbc = dbc + jnp.sum(dmix, axis=1, keepdims=True)
            dw_ref[g] += jnp.where(mask, dwg, 0.0)
            db_ref[...] += _put_col(dbc, g, 128)
        du, dv, dlg, dlb = vf((dug_scr[...], dvn_scr[...]))
        du_ref[...] = du.astype(BF16)
        dv_ref[...] = dv.astype(BF16)
        dlg_ref[...] += dlg
        dlb_ref[...] += dlb

    row = pl.BlockSpec((tb, w), lambda i: (i, 0))
    return pl.pallas_call(
        body, name=name, grid=(s // tb,),
        in_specs=[_pblk(tb, w, C_AU), _pblk(tb, w, C_AV), row, _full((1, w)), _full((1, w)),
                  _full((4, 128, 128)), _full((128, 128))],
        out_specs=[row, row, _full((1, w)), _full((1, w)), _full((4, 128, 128)), _full((128, 128))],
        out_shape=[jax.ShapeDtypeStruct((s, w), BF16), jax.ShapeDtypeStruct((s, w), BF16),
                   jax.ShapeDtypeStruct((1, w), F32), jax.ShapeDtypeStruct((1, w), F32),
                   jax.ShapeDtypeStruct((4, 128, 128), F32), jax.ShapeDtypeStruct((128, 128), F32)],
        scratch_shapes=[pltpu.VMEM((tb, w), F32), pltpu.VMEM((tb, w), F32)],
        compiler_params=_params("arbitrary"),
    )(proj, proj, dy, lg, lb, ws, bt)


POOL_HALO = 16


def _pooled(xe, t0, tb):
    s2 = xe + pltpu.roll(xe, 1, 0)
    s4 = s2 + pltpu.roll(s2, 2, 0)
    s8 = s4 + pltpu.roll(s4, 4, 0)
    s16 = s8 + pltpu.roll(s8, 8, 0)
    t = (t0 + _rows((tb, 128))).astype(F32)
    outs = []
    for g, sw in enumerate((s2, s4, s8, s16)):
        cs = slice(128 * g, 128 * g + 128)
        cnt = jnp.minimum(t + 1.0, float(2 ** (g + 1)))
        outs.append(sw[POOL_HALO:, cs] / cnt - xe[POOL_HALO:, cs])
    return outs


def pool_fwd(proj, wp, scale, *, tb, name):
    s = proj.shape[0]
    w = BRANCH_W

    def body(x_ref, p_ref, w_ref, sc_ref, y_ref):
        i = pl.program_id(0)
        prev = jnp.where(i > 0, p_ref[...], 0.0)
        xe = jnp.concatenate([prev, x_ref[...]], axis=0)
        pooled = _pooled(xe, i * tb, tb)
        sc = sc_ref[...]
        for g in range(4):
            cs = slice(128 * g, 128 * g + 128)
            y_ref[:, cs] = (_dg(pooled[g], w_ref[g], 1, 0) * sc[:, cs]).astype(BF16)

    cb = C_DX // w
    return pl.pallas_call(
        body, name=name, grid=(s // tb,),
        in_specs=[_pblk(tb, w, C_DX),
                  pl.BlockSpec((POOL_HALO, w), lambda i: (jnp.maximum(i * (tb // POOL_HALO) - 1, 0), cb)),
                  _full((4, 128, 128)), _full((1, w))],
        out_specs=pl.BlockSpec((tb, w), lambda i: (i, 0)),
        out_shape=jax.ShapeDtypeStruct((s, w), BF16),
        compiler_params=_params("parallel"),
    )(proj, proj, wp, scale)


def pool_bwd(proj, dy, wp, scale, *, tb, name):
    s = proj.shape[0]
    w = BRANCH_W
    nt = s // tb
    te = tb + POOL_HALO

    def body(x_ref, p_ref, dy_ref, n_ref, w_ref, sc_ref, dx_ref, dw_ref, dsc_ref):
        i = pl.program_id(0)

        @pl.when(i == 0)
        def _():
            dw_ref[...] = jnp.zeros_like(dw_ref)
            dsc_ref[...] = jnp.zeros_like(dsc_ref)

        prev = jnp.where(i > 0, p_ref[...], 0.0)
        xe = jnp.concatenate([prev, x_ref[...]], axis=0)
        pooled = _pooled(xe, i * tb, tb)
        dyv = dy_ref[...]
        nxt = jnp.where(i < nt - 1, n_ref[...], 0.0)
        dye = jnp.concatenate([dyv, nxt], axis=0)
        sc = sc_ref[...]
        t = (i * tb + _rows((te, 128))).astype(F32)
        for g in range(4):
            cs = slice(128 * g, 128 * g + 128)
            win = 2 ** (g + 1)
            dpm = dye[:, cs] * sc[:, cs]
            dpool = _dg(dpm, w_ref[g], 1, 1)
            q = dpool / jnp.minimum(t + 1.0, float(win))
            r, sh = q, 1
            while sh < win:
                r = r + pltpu.roll(r, te - sh, 0)
                sh *= 2
            dx_ref[:, cs] = (r[:tb] - dpool[:tb]).astype(BF16)
            dw_ref[g] += _dg(pooled[g], dpm[:tb], 0, 0)
            dsc_ref[:, cs] += jnp.sum(dyv[:, cs] * _dg(pooled[g], w_ref[g], 1, 0), axis=0, keepdims=True)

    cb = C_DX // w
    row = pl.BlockSpec((tb, w), lambda i: (i, 0))
    return pl.pallas_call(
        body, name=name, grid=(nt,),
        in_specs=[_pblk(tb, w, C_DX),
                  pl.BlockSpec((POOL_HALO, w), lambda i: (jnp.maximum(i * (tb // POOL_HALO) - 1, 0), cb)),
                  row,
                  pl.BlockSpec((POOL_HALO, w),
                               lambda i: (jnp.minimum((i + 1) * (tb // POOL_HALO), s // POOL_HALO - 1), 0)),
                  _full((4, 128, 128)), _full((1, w))],
        out_specs=[row, _full((4, 128, 128)), _full((1, w))],
        out_shape=[jax.ShapeDtypeStruct((s, w), BF16), jax.ShapeDtypeStruct((4, 128, 128), F32),
                   jax.ShapeDtypeStruct((1, w), F32)],
        compiler_params=_params("arbitrary"),
    )(proj, proj, dy, dy, wp, scale)


def _neg_expm1(z):
    p = 1.0 + z * (1.0 / 9.0)
    for n in (8.0, 7.0, 6.0, 5.0, 4.0, 3.0, 2.0):
        p = 1.0 + z * (1.0 / n) * p
    return jnp.where(z > -0.5, -z * p, 1.0 - jnp.exp(z))


def _lru_gates(xc, wa, wx, ba, bx, lam):
    r = jax.nn.sigmoid(mm_nn(xc, wa) + ba)
    i = jax.nn.sigmoid(mm_nn(xc, wx) + bx)
    log_a = -LRU_C * r * _softplus(-lam)
    a = jnp.exp(log_a)
    return a, jnp.sqrt(_neg_expm1(2.0 * log_a)) * (i * xc)


def _scan_down(a, b):
    n = a.shape[0]
    row = _rows(a.shape)
    sh = 1
    while sh < n:
        keep = row >= sh
        a_sh = jnp.where(keep, pltpu.roll(a, sh, 0), 1.0)
        b_sh = jnp.where(keep, pltpu.roll(b, sh, 0), 0.0)
        b = a * b_sh + b
        a = a * a_sh
        sh *= 2
    return a, b


def _scan_up(a, b):
    n = a.shape[0]
    row = _rows(a.shape)
    sh = 1
    while sh < n:
        keep = row < n - sh
        a_sh = jnp.where(keep, pltpu.roll(a, n - sh, 0), 1.0)
        b_sh = jnp.where(keep, pltpu.roll(b, n - sh, 0), 0.0)
        b = a * b_sh + b
        a = a * a_sh
        sh *= 2
    return a, b


def lru_fwd(xc, proj, wa, wx, ba, bx, lam, *, tb, name):
    s = xc.shape[0]
    w = BRANCH_W

    def body(xc_ref, gate_ref, wa_ref, wx_ref, ba_ref, bx_ref, lam_ref, y_ref, h_ref, carry):
        @pl.when(pl.program_id(0) == 0)
        def _():
            carry[...] = jnp.zeros_like(carry)

        a, b = _lru_gates(xc_ref[...], wa_ref[...], wx_ref[...], ba_ref[...], bx_ref[...], lam_ref[...])
        pa, hb = _scan_down(a, b)
        h = pa * carry[...] + hb
        h_ref[...] = h
        carry[...] = h_ref[tb - 1:tb, :]
        y_ref[...] = (h * _gelu(gate_ref[...])).astype(BF16)

    row = pl.BlockSpec((tb, w), lambda i: (i, 0))
    return pl.pallas_call(
        body, name=name, grid=(s // tb,),
        in_specs=[row, _pblk(tb, w, C_BG), _full((w, w)), _full((w, w)),
                  _full((1, w)), _full((1, w)), _full((1, w))],
        out_specs=[row, row],
        out_shape=[jax.ShapeDtypeStruct((s, w), BF16), jax.ShapeDtypeStruct((s, w), F32)],
        scratch_shapes=[pltpu.VMEM((1, w), F32)],
        compiler_params=_params("arbitrary"),
    )(xc, proj, wa, wx, ba, bx, lam)


def lru_bwd(xc, proj, h, dy, wa, wx, ba, bx, lam, *, tb, name):
    s = xc.shape[0]
    w = BRANCH_W
    nt = s // tb

    def body(xc_ref, gate_ref, h_ref, hp_ref, dy_ref, wa_ref, wx_ref, ba_ref, bx_ref, lam_ref,
             dxc_ref, dgate_ref, dwa_ref, dwx_ref, dba_ref, dbx_ref, dlam_ref, carry):
        i = pl.program_id(0)

        @pl.when(i == 0)
        def _():
            carry[...] = jnp.zeros_like(carry)
            for r in (dwa_ref, dwx_ref, dba_ref, dbx_ref, dlam_ref):
                r[...] = jnp.zeros_like(r)

        (a, b), vf = jax.vjp(_lru_gates, xc_ref[...], wa_ref[...], wx_ref[...],
                             ba_ref[...], bx_ref[...], lam_ref[...])
        gate = gate_ref[...]
        dyv = dy_ref[...]
        hv = h_ref[...]
        row = _rows((tb, w))
        a_next = jnp.where(row < tb - 1, pltpu.roll(a, tb - 1, 0), 1.0)
        pa, xb = _scan_up(a_next, dyv * _gelu(gate))
        dh = xb + pa * carry[...]
        carry[...] = a[0:1, :] * dh[0:1, :]
        h_first = jnp.where(i < nt - 1, hp_ref[7:8, :], 0.0)
        h_prev = jnp.where(row >= 1, pltpu.roll(hv, 1, 0), h_first)
        dxc, dwa, dwx, dba, dbx, dlam = vf((dh * h_prev, dh))
        dxc_ref[...] = dxc
        dgate_ref[...] = (dyv * hv * _gelu_grad(gate)).astype(BF16)
        dwa_ref[...] += dwa
        dwx_ref[...] += dwx
        dba_ref[...] += dba
        dbx_ref[...] += dbx
        dlam_ref[...] += dlam

    row_spec = pl.BlockSpec((tb, w), lambda i: (nt - 1 - i, 0))
    hp_spec = pl.BlockSpec((8, w), lambda i: (jnp.maximum((nt - 1 - i) * (tb // 8) - 1, 0), 0))
    return pl.pallas_call(
        body, name=name, grid=(nt,),
        in_specs=[row_spec, _pblk(tb, w, C_BG, rev_n=nt), row_spec, hp_spec, row_spec,
                  _full((w, w)), _full((w, w)), _full((1, w)), _full((1, w)), _full((1, w))],
        out_specs=[row_spec, row_spec, _full((w, w)), _full((w, w)),
                   _full((1, w)), _full((1, w)), _full((1, w))],
        out_shape=[jax.ShapeDtypeStruct((s, w), F32), jax.ShapeDtypeStruct((s, w), BF16),
                   jax.ShapeDtypeStruct((w, w), F32), jax.ShapeDtypeStruct((w, w), F32),
                   jax.ShapeDtypeStruct((1, w), F32), jax.ShapeDtypeStruct((1, w), F32),
                   jax.ShapeDtypeStruct((1, w), F32)],
        scratch_shapes=[pltpu.VMEM((1, w), F32)],
        compiler_params=_params("arbitrary"),
    )(xc, proj, h, h, dy, wa, wx, ba, bx, lam)


def _gdn_g(ba, alog, dt):
    return -jnp.exp(alog) * _softplus(ba + dt)


def _gdn_local(qp, kp, vp, bpre, gc):
    b, c, dk = qp.shape
    q = _silu(qp)
    k = _silu(kp)
    v = _silu(vp)
    q = q * lax.rsqrt(jnp.sum(q * q, axis=-1, keepdims=True) + EPS) * (dk ** -0.5)
    k = k * lax.rsqrt(jnp.sum(k * k, axis=-1, keepdims=True) + EPS)
    beta = jax.nn.sigmoid(bpre)
    sq = (b, c, c)
    ii, jj = _iota3(sq, 1), _iota3(sq, 2)
    gr = jnp.sum(gc * (ii == jj).astype(F32), axis=1, keepdims=True)
    decay = jnp.exp(jnp.where(ii >= jj, gc - gr, -jnp.inf))
    wd = (b, c, 2 * c)
    wi, wj = _iota3(wd, 1), _iota3(wd, 2) - c
    gr_w = jnp.sum(gc * (wi == wj).astype(F32), axis=1, keepdims=True)
    decay_w = jnp.exp(jnp.where(jnp.logical_and(wj >= 0, wi > wj), gc - gr_w, -jnp.inf))
    kb = k * beta
    pad = jnp.zeros((b, c, dk), F32)
    t = _tri_inv(bmm_nt(kb, jnp.concatenate([pad, k], axis=1)) * decay_w)
    u = bmm_nn(t, jnp.concatenate([v * beta, pad], axis=1))
    wk = bmm_nn(t, jnp.concatenate([kb * jnp.exp(gc), pad], axis=1))
    attn = bmm_nt(q, k) * decay
    g_last = jnp.sum(jnp.where(_iota3((b, c, 1), 1) == c - 1, gc, 0.0), axis=1, keepdims=True)
    return u, wk, attn, q * jnp.exp(gc), k * jnp.exp(g_last - gc), jnp.exp(g_last)


def _gdn_recur(s, u, wk, attn, qg, kd, eg, zz, ng):
    v_new = u - bmm_nn(wk, s)
    o = bmm_nn(qg, s) + bmm_nn(attn, v_new)
    s_out = s * eg + bmm_tn(kd, v_new)
    o = o * lax.rsqrt(jnp.mean(o * o, axis=-1, keepdims=True) + EPS) * ng
    return o * _silu(zz), s_out


def _gdn_inputs(qkv_ref, ba, gcums, cps):
    c, hd = CHUNK, GDN_DK
    qs, ks, vs, bs, gs = [], [], [], [], []
    for ci in range(cps):
        rs = slice(c * ci, c * (ci + 1))
        for h in range(GDN_HEADS):
            qs.append(qkv_ref[rs, hd * h:hd * (h + 1)])
            ks.append(qkv_ref[rs, 512 + hd * h:512 + hd * (h + 1)])
            vs.append(qkv_ref[rs, 1024 + hd * h:1024 + hd * (h + 1)])
            bs.append(_col(ba[rs], h))
            gs.append(_col(gcums[ci], 4 + h))
    return tuple(jnp.stack(t) for t in (qs, ks, vs, bs, gs))


def _chunk_cumsum(g, transpose=False):
    c = CHUNK
    tri = (_rows((c, c)) >= _lanes((c, c))).astype(F32)
    return _dg3(tri, g, 0 if transpose else 1, 0)


def gdn_fwd(qkv, proj, alog, dt, ng, *, cps, name):
    s = qkv.shape[0]
    c = CHUNK
    nc = s // c
    hd = GDN_DK
    tb = c * cps

    def body(qkv_ref, z_ref, ba_ref, al_ref, dt_ref, ng_ref, y_ref, sall_ref, s_scr):
        @pl.when(pl.program_id(0) == 0)
        def _():
            s_scr[...] = jnp.zeros_like(s_scr)

        nh = GDN_HEADS
        ba = ba_ref[...]
        g = _gdn_g(ba, al_ref[...], dt_ref[...])
        gcums = [_chunk_cumsum(g[c * ci:c * (ci + 1)]) for ci in range(cps)]
        loc = _gdn_local(*_gdn_inputs(qkv_ref, ba, gcums, cps))
        for ci in range(cps):
            rs = slice(c * ci, c * (ci + 1))
            sl = slice(nh * ci, nh * (ci + 1))
            s_in = s_scr[...]
            sall_ref[ci] = s_in
            zz = jnp.stack([z_ref[rs, hd * h:hd * (h + 1)] for h in range(nh)])
            y, s_out = _gdn_recur(s_in, *(t[sl] for t in loc), zz, ng_ref[...])
            s_scr[...] = s_out
            for h in range(nh):
                y_ref[rs, hd * h:hd * (h + 1)] = y[h].astype(BF16)

    return pl.pallas_call(
        body, name=name, grid=(nc // cps,),
        in_specs=[pl.BlockSpec((tb, 1536), lambda i: (i, 0)), _pblk(tb, 512, C_Z), _pblk(tb, 128, C_BA),
                  _full((1, 128)), _full((1, 128)), _full((1, 128))],
        out_specs=[pl.BlockSpec((tb, 512), lambda i: (i, 0)),
                   pl.BlockSpec((cps, GDN_HEADS, hd, hd), lambda i: (i, 0, 0, 0))],
        out_shape=[jax.ShapeDtypeStruct((s, 512), BF16), jax.ShapeDtypeStruct((nc, GDN_HEADS, hd, hd), F32)],
        scratch_shapes=[pltpu.VMEM((GDN_HEADS, hd, hd), F32)],
        compiler_params=_params("arbitrary"),
    )(qkv, proj, proj, alog, dt, ng)


def gdn_bwd(qkv, proj, sall, dy, alog, dt, ng, *, cps, name):
    s = qkv.shape[0]
    c = CHUNK
    nc = s // c
    hd = GDN_DK
    tb = c * cps
    nt = nc // cps

    def body(qkv_ref, z_ref, ba_ref, sall_ref, dy_ref, al_ref, dt_ref, ng_ref,
             dqkv_ref, dz_ref, dba_ref, dal_ref, ddt_ref, dng_ref, ds_scr):
        @pl.when(pl.program_id(0) == 0)
        def _():
            ds_scr[...] = jnp.zeros_like(ds_scr)
            dal_ref[...] = jnp.zeros_like(dal_ref)
            ddt_ref[...] = jnp.zeros_like(ddt_ref)
            dng_ref[...] = jnp.zeros_like(dng_ref)

        nh = GDN_HEADS
        ba = ba_ref[...]
        g, gvf = jax.vjp(_gdn_g, ba, al_ref[...], dt_ref[...])
        gcums = [_chunk_cumsum(g[c * ci:c * (ci + 1)]) for ci in range(cps)]
        loc, loc_vf = jax.vjp(_gdn_local, *_gdn_inputs(qkv_ref, ba, gcums, cps))
        d_loc = [None] * cps
        dng = jnp.zeros((1, 128), F32)
        for ci in reversed(range(cps)):
            rs = slice(c * ci, c * (ci + 1))
            sl = slice(nh * ci, nh * (ci + 1))
            zz = jnp.stack([z_ref[rs, hd * h:hd * (h + 1)] for h in range(nh)])
            dyy = jnp.stack([dy_ref[rs, hd * h:hd * (h + 1)] for h in range(nh)])
            _, rvf = jax.vjp(_gdn_recur, sall_ref[ci], *(t[sl] for t in loc), zz, ng_ref[...])
            ds, du, dwk, dattn, dqg, dkd, deg, dzz, dn = rvf((dyy, ds_scr[...]))
            ds_scr[...] = ds
            d_loc[ci] = (du, dwk, dattn, dqg, dkd, deg)
            dng = dng + dn
            for h in range(nh):
                dz_ref[rs, hd * h:hd * (h + 1)] = dzz[h].astype(BF16)
        dq, dk, dv, dbp, dgc = loc_vf(tuple(jnp.concatenate([d_loc[ci][j] for ci in range(cps)], axis=0)
                                            for j in range(6)))
        dgs, dbas = [], []
        for ci in range(cps):
            rs = slice(c * ci, c * (ci + 1))
            dgcum = jnp.zeros((c, 128), F32)
            dba = jnp.zeros((c, 128), F32)
            for h in range(nh):
                b = nh * ci + h
                dqkv_ref[rs, hd * h:hd * (h + 1)] = dq[b]
                dqkv_ref[rs, 512 + hd * h:512 + hd * (h + 1)] = dk[b]
                dqkv_ref[rs, 1024 + hd * h:1024 + hd * (h + 1)] = dv[b]
                dgcum = dgcum + _put_col(dgc[b], 4 + h, 128)
                dba = dba + _put_col(dbp[b], h, 128)
            dgs.append(_chunk_cumsum(dgcum, transpose=True))
            dbas.append(dba)
        dba2, dal, ddt = gvf(jnp.concatenate(dgs, axis=0))
        dba_ref[...] = (jnp.concatenate(dbas, axis=0) + dba2).astype(BF16)
        dal_ref[...] += dal
        ddt_ref[...] += ddt
        dng_ref[...] += dng

    def rev(width):
        return pl.BlockSpec((tb, width), lambda i: (nt - 1 - i, 0))

    return pl.pallas_call(
        body, name=name, grid=(nt,),
        in_specs=[rev(1536), _pblk(tb, 512, C_Z, rev_n=nt), _pblk(tb, 128, C_BA, rev_n=nt),
                  pl.BlockSpec((cps, GDN_HEADS, hd, hd), lambda i: (nt - 1 - i, 0, 0, 0)), rev(512),
                  _full((1, 128)), _full((1, 128)), _full((1, 128))],
        out_specs=[rev(1536), rev(512), rev(128), _full((1, 128)), _full((1, 128)), _full((1, 128))],
        out_shape=[jax.ShapeDtypeStruct((s, 1536), F32), jax.ShapeDtypeStruct((s, 512), BF16),
                   jax.ShapeDtypeStruct((s, 128), BF16), jax.ShapeDtypeStruct((1, 128), F32),
                   jax.ShapeDtypeStruct((1, 128), F32), jax.ShapeDtypeStruct((1, 128), F32)],
        scratch_shapes=[pltpu.VMEM((GDN_HEADS, hd, hd), F32)],
        compiler_params=_params("arbitrary"),
    )(qkv, proj, proj, sall, dy, alog, dt, ng)


def merge_fwd(x, ys, proj, wb, wo, *, tb, dep=None, name):
    s, d = x.shape
    dep_specs, dep_args = _dep_operand(dep)

    def body(x_ref, ya, yb, yc, yd, g0, g1, g2, g3, wb_ref, wo_ref, *rest):
        o_ref, m_ref = rest[len(dep_args):]
        m = jnp.zeros((tb, d), F32)
        for g, (y_ref, gp_ref) in enumerate(zip((ya, yb, yc, yd), (g0, g1, g2, g3))):
            m = m + jax.nn.sigmoid(gp_ref[...]) * _dg(y_ref[...], wb_ref[g], 1, 0)
        mb = m.astype(BF16)
        m_ref[...] = mb
        o_ref[...] = x_ref[...] + _dg(mb, wo_ref[...], 1, 0)

    row = pl.BlockSpec((tb, d), lambda i: (i, 0))
    yrow = pl.BlockSpec((tb, BRANCH_W), lambda i: (i, 0))
    return pl.pallas_call(
        body, name=name, grid=(s // tb,),
        in_specs=[row] + [yrow] * 4 + [_pblk(tb, d, C_GATE + d * g) for g in range(4)]
        + [_full((4, BRANCH_W, d)), _full((d, d))] + dep_specs,
        out_specs=[row, row],
        out_shape=[jax.ShapeDtypeStruct((s, d), F32), jax.ShapeDtypeStruct((s, d), BF16)],
        compiler_params=_params("parallel"),
    )(x, *ys, proj, proj, proj, proj, wb, wo, *dep_args)


def merge_bwd(dx, ys, proj, wb, wo, *, tb, dep=None, name):
    s, d = dx.shape
    dep_specs, dep_args = _dep_operand(dep)

    def body(dx_ref, ya, yb, yc, yd, g0, g1, g2, g3, wb_ref, wo_ref, *rest):
        dgp_ref, dbr_ref, da, db, dc, dd, dxb_ref = rest[len(dep_args):]
        dxb = dx_ref[...].astype(BF16)
        dxb_ref[...] = dxb
        dm = _dg(dxb, wo_ref[...], 1, 1)
        for g, (y_ref, gp_ref, dy_ref) in enumerate(zip((ya, yb, yc, yd), (g0, g1, g2, g3), (da, db, dc, dd))):
            br = _dg(y_ref[...], wb_ref[g], 1, 0)
            sg = jax.nn.sigmoid(gp_ref[...])
            dgp_ref[:, d * g:d * (g + 1)] = (dm * br * sg * (1.0 - sg)).astype(BF16)
            dbr = (dm * sg).astype(BF16)
            dbr_ref[g] = dbr
            dy_ref[...] = _dg(dbr, wb_ref[g], 1, 1)

    row = pl.BlockSpec((tb, d), lambda i: (i, 0))
    yrow = pl.BlockSpec((tb, BRANCH_W), lambda i: (i, 0))
    return pl.pallas_call(
        body, name=name, grid=(s // tb,),
        in_specs=[row] + [yrow] * 4 + [_pblk(tb, d, C_GATE + d * g) for g in range(4)]
        + [_full((4, BRANCH_W, d)), _full((d, d))] + dep_specs,
        out_specs=[pl.BlockSpec((tb, 4 * d), lambda i: (i, 0)), pl.BlockSpec((4, tb, d), lambda i: (0, i, 0)),
                   yrow, yrow, yrow, yrow, row],
        out_shape=[jax.ShapeDtypeStruct((s, 4 * d), BF16), jax.ShapeDtypeStruct((4, s, d), BF16)]
        + [jax.ShapeDtypeStruct((s, BRANCH_W), F32)] * 4 + [jax.ShapeDtypeStruct((s, d), BF16)],
        compiler_params=_params("parallel"),
    )(dx, *ys, proj, proj, proj, proj, wb, wo, *dep_args)


def loss_head(x, g, target, *, tb, name):
    s, d = x.shape

    def body(x_ref, g_ref, t_ref, dx_ref, dg_ref, loss_ref):
        @pl.when(pl.program_id(0) == 0)
        def _():
            dg_ref[...] = jnp.zeros_like(dg_ref)
            loss_ref[...] = jnp.zeros_like(loss_ref)

        xv, gv = x_ref[...], g_ref[...]
        r = lax.rsqrt(jnp.mean(xv * xv, axis=-1, keepdims=True) + EPS)
        err = xv * r * gv - t_ref[...]
        loss_ref[...] += 0.5 * jnp.sum(jnp.mean(err * err, axis=-1, keepdims=True), axis=0, keepdims=True)
        dx, dg = _rms_bwd(xv, gv, err * (1.0 / d))
        dx_ref[...] = dx
        dg_ref[...] += dg

    row = pl.BlockSpec((tb, d), lambda i: (i, 0))
    return pl.pallas_call(
        body, name=name, grid=(s // tb,),
        in_specs=[row, _full((1, d)), row],
        out_specs=[row, _full((1, d)), _full((1, 128))],
        out_shape=[jax.ShapeDtypeStruct((s, d), F32), jax.ShapeDtypeStruct((1, d), F32),
                   jax.ShapeDtypeStruct((1, 128), F32)],
        compiler_params=_params("arbitrary"),
    )(x, g, target)


def adamw(w, g, m, v, *, name):
    shape = w.shape
    c = shape[-1]
    r = math.prod(shape[:-1])
    tr = r
    while tr * c * 4 > (1 << 20) and tr % 16 == 0:
        tr //= 2
    c1 = 1.0 / (1.0 - ADAM_B1 ** ADAM_STEP)
    c2 = 1.0 / (1.0 - ADAM_B2 ** ADAM_STEP)

    def body(w_ref, g_ref, m_ref, v_ref, d_ref, nm_ref, nv_ref):
        gv = g_ref[...]
        nm = ADAM_B1 * m_ref[...] + (1.0 - ADAM_B1) * gv
        nv = ADAM_B2 * v_ref[...] + (1.0 - ADAM_B2) * (gv * gv)
        nm_ref[...] = nm
        nv_ref[...] = nv
        d_ref[...] = -ADAM_LR * ((nm * c1) / (jnp.sqrt(nv * c2) + ADAM_EPS) + ADAM_WD * w_ref[...])

    blk = pl.BlockSpec((tr, c), lambda i: (i, 0))
    outs = pl.pallas_call(
        body, name=name, grid=(r // tr,),
        in_specs=[blk] * 4, out_specs=[blk] * 3,
        out_shape=[jax.ShapeDtypeStruct((r, c), F32)] * 3,
        compiler_params=_params("parallel"),
    )(*(t.reshape(r, c) for t in (w, g, m, v)))
    return tuple(o.reshape(shape) for o in outs)


_ANY = pl.BlockSpec(memory_space=pl.ANY)


def _place():
    x, y, c = lax.axis_index("x"), lax.axis_index("y"), lax.axis_index("c")
    return x, y, c, [(1 - x, y), (x, 1 - y), (1 - x, 1 - y)]


def gather_shards(bufs, *, name):
    n = len(bufs)

    def body(*refs):
        outs = refs[n:2 * n]
        ici_send, ici_recv, d2d_send, d2d_recv = refs[2 * n:]
        x, y, c, chips = _place()
        me = 2 * x + y

        def ici(a, j, slab, to):
            return pltpu.make_async_remote_copy(
                src_ref=outs[a].at[slab, c], dst_ref=outs[a].at[slab, c], send_sem=ici_send.at[a, j],
                recv_sem=ici_recv.at[a, j], device_id=to, device_id_type=MESH)

        def d2d(a, j, slab, half):
            return pltpu.make_async_remote_copy(
                src_ref=outs[a].at[slab, half], dst_ref=outs[a].at[slab, half], send_sem=d2d_send.at[a, j],
                recv_sem=d2d_recv.at[a, j], device_id=(x, y, 1 - c), device_id_type=MESH)

        sends = []
        for a in range(n):
            for j, (px, py) in enumerate(chips):
                cp = ici(a, j, me, (px, py, c))
                cp.start()
                sends.append(cp)
        for a in range(n):
            for j, (px, py) in enumerate(chips):
                ici(a, j, 2 * px + py, (px, py, c)).wait_recv()
                cp = d2d(a, j, 2 * px + py, c)
                cp.start()
                sends.append(cp)
        for a in range(n):
            for j, (px, py) in enumerate(chips):
                d2d(a, j, 2 * px + py, 1 - c).wait_recv()
        for cp in sends:
            cp.wait_send()

    return pl.pallas_call(
        body, name=name,
        in_specs=[_ANY] * n, out_specs=[_ANY] * n,
        out_shape=[jax.ShapeDtypeStruct(t.shape, t.dtype) for t in bufs],
        input_output_aliases={a: a for a in range(n)},
        scratch_shapes=[pltpu.SemaphoreType.DMA((n, 3))] * 4,
    )(*bufs)


_SEM = pl.BlockSpec(memory_space=pltpu.SEMAPHORE)
_VMEM = pl.BlockSpec(memory_space=pltpu.VMEM)
_EFFECT = pltpu.SideEffectType.DATAFLOW_SIDE_EFFECTING
_TOKEN = jax.ShapeDtypeStruct((8, 128), F32)


def _gather_copies(outs, sems_ici, sems_d2d):
    x, y, c, chips = _place()

    def ici(a, j, slab, to):
        return pltpu.make_async_remote_copy(
            src_ref=outs[a].at[slab, c], dst_ref=outs[a].at[slab, c], send_sem=sems_ici[0][3 * a + j],
            recv_sem=sems_ici[1][3 * a + j], device_id=to, device_id_type=MESH)

    def d2d(a, j, slab, half):
        return pltpu.make_async_remote_copy(
            src_ref=outs[a].at[slab, half], dst_ref=outs[a].at[slab, half], send_sem=sems_d2d[0][3 * a + j],
            recv_sem=sems_d2d[1][3 * a + j], device_id=(x, y, 1 - c), device_id_type=MESH)

    return x, y, c, chips, ici, d2d


def _async_call(body, bufs, sems_in, after, sems_out, token, *, name):
    n = len(bufs)
    n_out = 6 * n if sems_out else 0
    extra = [] if after is None else [after]
    return pl.pallas_call(
        body, name=name,
        in_specs=[_ANY] * n + [_SEM] * len(sems_in) + [_ANY] * len(extra),
        out_specs=[_ANY] * n + [_SEM] * n_out + [_VMEM] * token,
        out_shape=[jax.ShapeDtypeStruct(t.shape, t.dtype) for t in bufs]
        + [pltpu.SemaphoreType.DMA(())] * n_out + [_TOKEN] * token,
        input_output_aliases={a: a for a in range(n)},
        compiler_params=pltpu.CompilerParams(has_side_effects=_EFFECT),
    )(*bufs, *sems_in, *extra)


def gather_start(bufs, *, name):
    n = len(bufs)

    def body(*refs):
        outs = refs[n:2 * n]
        sems, token = refs[2 * n:8 * n], refs[8 * n]
        x, y, c, chips, ici, _ = _gather_copies(outs, (sems[:3 * n], sems[3 * n:]), None)
        for a in range(n):
            for j, (px, py) in enumerate(chips):
                ici(a, j, 2 * x + y, (px, py, c)).start()
        token[...] = jnp.zeros_like(token)

    res = _async_call(body, bufs, [], None, True, True, name=name)
    return res[:n], res[n:7 * n], res[7 * n]


def gather_pass(bufs, sems, after, *, name):
    n = len(bufs)

    def body(*refs):
        s_in = refs[n:7 * n]
        outs = refs[7 * n + 1:8 * n + 1]
        s_out, token = refs[8 * n + 1:14 * n + 1], refs[14 * n + 1]
        x, y, c, chips, ici, d2d = _gather_copies(outs, (s_in[:3 * n], s_in[3 * n:]),
                                                  (s_out[:3 * n], s_out[3 * n:]))
        for a in range(n):
            for j, (px, py) in enumerate(chips):
                ici(a, j, 2 * px + py, (px, py, c)).wait_recv()
                d2d(a, j, 2 * px + py, c).start()
        for a in range(n):
            for j, (px, py) in enumerate(chips):
                ici(a, j, 2 * x + y, (px, py, c)).wait_send()
        token[...] = jnp.zeros_like(token)

    res = _async_call(body, bufs, list(sems), after, True, True, name=name)
    return res[:n], res[n:7 * n], res[7 * n]


def gather_wait(bufs, sems, after, *, name):
    n = len(bufs)

    def body(*refs):
        s_in = refs[n:7 * n]
        outs = refs[7 * n + 1:]
        x, y, c, chips, _, d2d = _gather_copies(outs, None, (s_in[:3 * n], s_in[3 * n:]))
        for a in range(n):
            for j, (px, py) in enumerate(chips):
                d2d(a, j, 2 * px + py, 1 - c).wait_recv()
                d2d(a, j, 2 * px + py, c).wait_send()

    return _async_call(body, bufs, list(sems), after, False, False, name=name)


def _pair_plan(srcs, lands):
    x, y, c, _ = _place()
    return [(srcs[a].at[j, 1 - c], lands[a].at[j], (x, y, 1 - c))
            for a in range(len(srcs)) for j in range(N_SHARD)]


def _chip_plan(srcs, lands):
    x, y, c, chips = _place()
    return [(srcs[a].at[2 * px + py], lands[a].at[2 * x + y], (px, py, c))
            for a in range(len(srcs)) for (px, py) in chips]


def copies_start(arrs, land_shapes, plan, per_array, *, name):
    n = len(arrs)
    k = per_array * n

    def body(*refs):
        lands = refs[2 * n:3 * n]
        sems, token = refs[3 * n:3 * n + 2 * k], refs[3 * n + 2 * k]
        for i, (src, dst, dev) in enumerate(plan(refs[n:2 * n], lands)):
            pltpu.make_async_remote_copy(src_ref=src, dst_ref=dst, send_sem=sems[i], recv_sem=sems[k + i],
                                         device_id=dev, device_id_type=MESH).start()
        token[...] = jnp.zeros_like(token)

    res = pl.pallas_call(
        body, name=name,
        in_specs=[_ANY] * n,
        out_specs=[_ANY] * (2 * n) + [_SEM] * (2 * k) + [_VMEM],
        out_shape=[jax.ShapeDtypeStruct(t.shape, t.dtype) for t in arrs] + list(land_shapes)
        + [pltpu.SemaphoreType.DMA(())] * (2 * k) + [_TOKEN],
        input_output_aliases={a: a for a in range(n)},
        compiler_params=pltpu.CompilerParams(has_side_effects=_EFFECT),
    )(*arrs)
    return res[:n], res[n:2 * n], res[2 * n:2 * n + 2 * k], res[2 * n + 2 * k]


def copies_wait(arrs, lands, sems, after, plan, *, name):
    n = len(arrs)
    k = len(sems) // 2

    def body(*refs):
        s_in = refs[2 * n:2 * n + 2 * k]
        outs = refs[2 * n + 2 * k + 1:]
        for i, (src, dst, dev) in enumerate(plan(outs[:n], outs[n:])):
            cp = pltpu.make_async_remote_copy(src_ref=src, dst_ref=dst, send_sem=s_in[i], recv_sem=s_in[k + i],
                                              device_id=dev, device_id_type=MESH)
            cp.wait_send()
            cp.wait_recv()

    res = pl.pallas_call(
        body, name=name,
        in_specs=[_ANY] * (2 * n) + [_SEM] * (2 * k) + [_ANY],
        out_specs=[_ANY] * (2 * n),
        out_shape=[jax.ShapeDtypeStruct(t.shape, t.dtype) for t in list(arrs) + list(lands)],
        input_output_aliases={a: a for a in range(2 * n)},
        compiler_params=pltpu.CompilerParams(has_side_effects=_EFFECT),
    )(*arrs, *lands, *sems, after)
    return res[:n], res[n:]


def pair_send_halves(arrs, *, name):
    n = len(arrs)

    def body(*refs):
        ins, outs = refs[:n], refs[n:2 * n]
        send_sems, recv_sems = refs[2 * n:]
        x, y, c, _ = _place()
        copies = []
        for a in range(n):
            for j in range(N_SHARD):
                cp = pltpu.make_async_remote_copy(
                    src_ref=ins[a].at[j, 1 - c], dst_ref=outs[a].at[j], send_sem=send_sems.at[a, j],
                    recv_sem=recv_sems.at[a, j], device_id=(x, y, 1 - c), device_id_type=MESH)
                cp.start()
                copies.append(cp)
        for cp in copies:
            cp.wait()

    return pl.pallas_call(
        body, name=name,
        in_specs=[_ANY] * n, out_specs=[_ANY] * n,
        out_shape=[jax.ShapeDtypeStruct((N_SHARD,) + t.shape[2:], t.dtype) for t in arrs],
        scratch_shapes=[pltpu.SemaphoreType.DMA((n, N_SHARD)), pltpu.SemaphoreType.DMA((n, N_SHARD))],
    )(*arrs)


def chip_scatter(arrs, *, name):
    n = len(arrs)

    def body(*refs):
        ins, outs = refs[:n], refs[n:2 * n]
        send_sems, recv_sems = refs[2 * n:]
        x, y, c, chips = _place()
        me = 2 * x + y
        copies = []
        for a in range(n):
            for j, (px, py) in enumerate(chips):
                cp = pltpu.make_async_remote_copy(
                    src_ref=ins[a].at[2 * px + py], dst_ref=outs[a].at[me], send_sem=send_sems.at[a, j],
                    recv_sem=recv_sems.at[a, j], device_id=(px, py, c), device_id_type=MESH)
                cp.start()
                copies.append(cp)
        for cp in copies:
            cp.wait()

    return pl.pallas_call(
        body, name=name,
        in_specs=[_ANY] * n, out_specs=[_ANY] * n,
        out_shape=[jax.ShapeDtypeStruct(t.shape, t.dtype) for t in arrs],
        scratch_shapes=[pltpu.SemaphoreType.DMA((n, 3)), pltpu.SemaphoreType.DMA((n, 3))],
    )(*arrs)


def pair_gather(bufs, *, name):
    n = len(bufs)

    def body(*refs):
        outs = refs[n:2 * n]
        send_sems, recv_sems = refs[2 * n:]
        x, y, c, _ = _place()
        copies = []
        for a in range(n):
            cp = pltpu.make_async_remote_copy(
                src_ref=outs[a].at[c], dst_ref=outs[a].at[c], send_sem=send_sems.at[a],
                recv_sem=recv_sems.at[a], device_id=(x, y, 1 - c), device_id_type=MESH)
            cp.start()
            copies.append(cp)
        for cp in copies:
            cp.wait()

    return pl.pallas_call(
        body, name=name,
        in_specs=[_ANY] * n, out_specs=[_ANY] * n,
        out_shape=[jax.ShapeDtypeStruct(t.shape, t.dtype) for t in bufs],
        input_output_aliases={a: a for a in range(n)},
        scratch_shapes=[pltpu.SemaphoreType.DMA((n,)), pltpu.SemaphoreType.DMA((n,))],
    )(*bufs)


def all_sum_small(v, *, name):
    m = v.shape[0]

    def body(v_ref, o_ref, buf, send_sems, recv_sems, local_sem):
        x, y, c, chips = _place()
        me, sibling = (x, y, c), (x, y, 1 - c)

        def rows(px, py, pc):
            return buf.at[pl.ds((4 * px + 2 * py + pc) * m, m), :]

        def copy(k, block, to, src=None):
            return pltpu.make_async_remote_copy(
                src_ref=rows(*block) if src is None else src, dst_ref=rows(*block),
                send_sem=send_sems.at[k], recv_sem=recv_sems.at[k], device_id=to, device_id_type=MESH)

        mine = pltpu.make_async_copy(v_ref, rows(*me), local_sem)
        mine.start()
        first = [copy(0, me, sibling, src=v_ref)]
        first += [copy(1 + j, me, (*chip, c), src=v_ref) for j, chip in enumerate(chips)]
        for cp in first:
            cp.start()
        passed = [copy(4 + j, (*chip, c), sibling) for j, chip in enumerate(chips)]
        for j, chip in enumerate(chips):
            copy(1 + j, (*chip, c), me).wait_recv()
            passed[j].start()
        copy(0, sibling, me).wait_recv()
        for j, chip in enumerate(chips):
            copy(4 + j, (*chip, 1 - c), me).wait_recv()
        for cp in first + passed:
            cp.wait_send()
        mine.wait()
        acc = buf[pl.ds(0, m), :]
        for k in range(1, 8):
            acc = acc + buf[pl.ds(k * m, m), :]
        o_ref[...] = acc

    vm = pl.BlockSpec(memory_space=pltpu.VMEM)
    return pl.pallas_call(
        body, name=name, in_specs=[vm], out_specs=vm,
        out_shape=jax.ShapeDtypeStruct((m, 128), F32),
        scratch_shapes=[pltpu.VMEM((8 * m, 128), F32), pltpu.SemaphoreType.DMA((7,)),
                        pltpu.SemaphoreType.DMA((7,)), pltpu.SemaphoreType.DMA],
    )(v)


def sum_halves(arr, recv, c_idx, *, name):
    _, _, r, c = arr.shape
    tr = r
    while tr * c * 4 > (1 << 20) and tr % 32 == 0:
        tr //= 2

    def body(c_ref, a_ref, b_ref, o_ref):
        o_ref[...] = (a_ref[...] + b_ref[...]).astype(BF16)

    return pl.pallas_call(
        body, name=name,
        grid_spec=pltpu.PrefetchScalarGridSpec(
            num_scalar_prefetch=1, grid=(N_SHARD, r // tr),
            in_specs=[pl.BlockSpec((None, None, tr, c), lambda j, i, cr: (j, cr[0], i, 0)),
                      pl.BlockSpec((None, tr, c), lambda j, i, cr: (j, i, 0))],
            out_specs=pl.BlockSpec((None, tr, c), lambda j, i, cr: (j, i, 0))),
        out_shape=jax.ShapeDtypeStruct((N_SHARD, r, c), BF16),
        compiler_params=_params("parallel", "parallel"),
    )(c_idx, arr, recv)


def sum_chips(recv, part, place, *, name):
    _, r, c = recv.shape
    tr = r
    while tr * c * 4 > (1 << 20) and tr % 32 == 0:
        tr //= 2

    def body(place_ref, a0, a1, a2, a3, p_ref, o_ref):
        me = place_ref[1]
        acc = None
        for k, a_ref in enumerate((a0, a1, a2, a3)):
            term = jnp.where(me == k, p_ref[...], a_ref[...]).astype(F32)
            acc = term if acc is None else acc + term
        o_ref[...] = acc

    def slab(k):
        return pl.BlockSpec((None, tr, c), lambda i, pr: (jnp.where(pr[1] == k, (k + 1) % N_SHARD, k), i, 0))

    return pl.pallas_call(
        body, name=name,
        grid_spec=pltpu.PrefetchScalarGridSpec(
            num_scalar_prefetch=1, grid=(r // tr,),
            in_specs=[slab(k) for k in range(N_SHARD)]
            + [pl.BlockSpec((None, tr, c), lambda i, pr: (pr[1], i, 0))],
            out_specs=pl.BlockSpec((None, tr, c), lambda i, pr: (pr[0], i, 0))),
        out_shape=jax.ShapeDtypeStruct((2, r, c), F32),
        compiler_params=_params("parallel"),
    )(place, recv, recv, recv, recv, part)


_WEIGHTS = ("ff1_norm", "ff1_wg", "ff1_wu", "ff1_wd", "mix_norm", "w_in", "sgu_ln_g", "sgu_ln_b", "sgu_w",
            "sgu_b", "lru_conv_w", "lru_conv_b", "lru_wa", "lru_ba", "lru_wx", "lru_bx", "lru_lambda",
            "gdn_conv_w", "gdn_a_log", "gdn_dt_bias", "gdn_norm_g", "pool_w", "pool_scale", "w_branch",
            "w_out", "ff2_norm", "ff2_wg", "ff2_wu", "ff2_wd", "final_norm")
_BIG = ("ff1_wg", "ff1_wu", "ff1_wd", "w_in", "w_branch", "w_out", "ff2_wg", "ff2_wu", "ff2_wd")
_SMALL = tuple(n for n in _WEIGHTS if n not in _BIG)
_CONV = ("lru_conv_w", "gdn_conv_w")


def _seg_rows(shape):
    return -(-math.prod(shape) // 1024) * 8


def _pack(arrs):
    segs = []
    for t in arrs:
        rows = _seg_rows(t.shape)
        flat = t.reshape(-1)
        segs.append(jnp.pad(flat, (0, rows * 128 - flat.shape[0])).reshape(rows, 128))
    return jnp.concatenate(segs, axis=0)


def _unpack(buf, shapes):
    out, o = [], 0
    for shp in shapes:
        rows = _seg_rows(shp)
        out.append(buf[o:o + rows].reshape(-1)[:math.prod(shp)].reshape(shp))
        o += rows
    return out


def _block_diag(w):
    h, n, _ = w.shape
    same = jnp.arange(h)[:, None, None, None] == jnp.arange(h)[None, None, :, None]
    return jnp.where(same, w[:, :, None, :], 0.0).reshape(h * n, h * n)


def _diag_blocks(m, h=8, n=64):
    return jnp.stack([m[i * n:(i + 1) * n, i * n:(i + 1) * n] for i in range(h)])


def _lane_row(v):
    return jnp.zeros((1, 128), F32).at[0, 4:8].set(v)


def _w_in_to_padded(raw):
    sem = jnp.concatenate([raw[k, :, :P_IN_SHARD] for k in range(N_SHARD)], axis=1)
    zero = jnp.zeros((sem.shape[0], C_GATE - C_BA - 8), sem.dtype)
    return jnp.concatenate([sem[:, 2048:4096], sem[:, 0:2048], sem[:, 4104:4616], sem[:, 4096:4104], zero,
                            sem[:, 4616:P_IN]], axis=1)


def _w_in_grad_to_shards(part):
    r = part.shape[1]
    dwp = part.transpose(1, 0, 2).reshape(r, P_PAD)
    sem = jnp.concatenate([dwp[:, C_AU:C_DX], dwp[:, 0:C_AU], dwp[:, C_BA:C_BA + 8], dwp[:, C_DX:C_BA],
                           dwp[:, C_GATE:]], axis=1)
    sh = sem.reshape(r, N_SHARD, P_IN_SHARD).transpose(1, 0, 2)
    return jnp.pad(sh, ((0, 0), (0, 0), (0, P_IN_SHARD_PAD - P_IN_SHARD)))


def kernel(x, ff1_norm, ff1_wg, ff1_wu, ff1_wd, mix_norm, w_in, sgu_ln_g, sgu_ln_b, sgu_w, sgu_b, lru_conv_w,
           lru_conv_b, lru_wa, lru_ba, lru_wx, lru_bx, lru_lambda, gdn_conv_w, gdn_a_log, gdn_dt_bias, gdn_norm_g,
           pool_w, pool_scale, w_branch, w_out, ff2_norm, ff2_wg, ff2_wu, ff2_wd, final_norm, loss_target, m_ff1_norm,
           m_ff1_wg, m_ff1_wu, m_ff1_wd, m_mix_norm, m_w_in, m_sgu_ln_g, m_sgu_ln_b, m_sgu_w, m_sgu_b, m_lru_conv_w,
           m_lru_conv_b, m_lru_wa, m_lru_ba, m_lru_wx, m_lru_bx, m_lru_lambda, m_gdn_conv_w, m_gdn_a_log, m_gdn_dt_bias,
           m_gdn_norm_g, m_pool_w, m_pool_scale, m_w_branch, m_w_out, m_ff2_norm, m_ff2_wg, m_ff2_wu, m_ff2_wd,
           m_final_norm, v_ff1_norm, v_ff1_wg, v_ff1_wu, v_ff1_wd, v_mix_norm, v_w_in, v_sgu_ln_g, v_sgu_ln_b, v_sgu_w,
           v_sgu_b, v_lru_conv_w, v_lru_conv_b, v_lru_wa, v_lru_ba, v_lru_wx, v_lru_bx, v_lru_lambda, v_gdn_conv_w,
           v_gdn_a_log, v_gdn_dt_bias, v_gdn_norm_g, v_pool_w, v_pool_scale, v_w_branch, v_w_out, v_ff2_norm, v_ff2_wg,
           v_ff2_wu, v_ff2_wd, v_final_norm):
    a = dict(locals())
    n_layer = ff1_norm.shape[0]
    d = D_MODEL
    x0 = x[0]
    chip = 2 * lax.axis_index("x") + lax.axis_index("y")
    c_idx = jnp.reshape(lax.axis_index("c"), (1,)).astype(jnp.int32)
    place = jnp.stack([lax.axis_index("c"), chip]).astype(jnp.int32)

    def own_slab(t, dtype=BF16):
        buf = lax.empty((N_SHARD,) + t.shape, dtype)
        return lax.dynamic_update_slice(buf, t.astype(dtype)[None], (chip,) + (0,) * t.ndim)

    def halves(t):
        return t.reshape((2, t.shape[0] // 2) + t.shape[1:])

    assert n_layer == 2
    w_in_loc = jnp.pad(w_in, ((0, 0), (0, 0), (0, P_IN_SHARD_PAD - P_IN_SHARD)))

    def layer_bufs(l):
        wgu = jnp.stack([ff1_wg[l], ff1_wu[l], ff2_wg[l], ff2_wu[l]]).reshape(2, 2, d, FF_SHARD)
        wd = jnp.stack([ff1_wd[l], ff2_wd[l]])
        return [own_slab(wgu), own_slab(wd), own_slab(halves(w_in_loc[l])), own_slab(halves(w_branch[l])),
                own_slab(halves(w_out[l]))]

    bufs = [layer_bufs(l) for l in range(n_layer)]
    *gathered0, g_lcw, g_gcw = gather_shards(bufs[0] + [own_slab(lru_conv_w, F32), own_slab(gdn_conv_w, F32)],
                                             name="gather_weights_0")
    first, bufs1 = lax.optimization_barrier((gathered0[0], bufs[1]))
    gathered = [[first] + gathered0[1:], None]
    in_flight, gather_sems, gather_token = gather_start(bufs1, name="gather_weights_1_start")

    def layer_params(l):
        g_wgu, g_wd, g_w_in, g_wb, g_wo = gathered[l]
        g_wgu = g_wgu.reshape(N_SHARD, 4, d, FF_SHARD)
        p = {}
        p["ff1"] = (g_wgu, g_wgu, g_wd, (0, 1, 0))
        p["ff2"] = (g_wgu, g_wgu, g_wd, (2, 3, 1))
        p["w_in"] = _w_in_to_padded(g_w_in.reshape(4, d, P_IN_SHARD_PAD))
        p["wb"] = g_wb.reshape(4, 4, BRANCH_W, d // 4).transpose(1, 2, 0, 3).reshape(4, BRANCH_W, d)
        p["wo"] = g_wo.reshape(d, d)
        p["lcw"] = g_lcw[:, l].transpose(1, 0, 2).reshape(4, BRANCH_W)
        p["gcw"] = g_gcw[:, l].transpose(1, 0, 2).reshape(4, 3 * BRANCH_W)
        p["lcb"] = lru_conv_b[l][None]
        p["dep1"] = gather_token if l == 0 else None
        p["bt"] = jnp.zeros((128, 128), F32).at[:, :4].set(sgu_b[l].T)
        p["wa"] = _block_diag(lru_wa[l])
        p["wx"] = _block_diag(lru_wx[l])
        p["alog"] = _lane_row(gdn_a_log[l])
        p["dt"] = _lane_row(gdn_dt_bias[l])
        return p

    saved = []
    xs = x0
    for l in range(n_layer):
        p = layer_params(l)
        sv = {"p": p, "x0": xs}
        x1 = ffn_fwd(xs, ff1_norm[l][None], *p["ff1"], tb=512, dep=p["dep1"], name=f"ffn1_fwd_{l}")
        h = norm_fwd(x1, mix_norm[l][None], tb=512, name=f"mix_norm_fwd_{l}")
        proj = matmul(h, p["w_in"], tm=1024, tn=1536, tk=1024, name=f"proj_{l}")
        ya = sgu_fwd(proj, sgu_ln_g[l][None], sgu_ln_b[l][None], sgu_w[l], p["bt"], tb=512, name=f"sgu_fwd_{l}")
        xc = conv_fwd(proj, C_BX, BRANCH_W, p["lcw"], p["lcb"], tb=512, name=f"lru_conv_fwd_{l}")
        yb, hl = lru_fwd(xc, proj, p["wa"], p["wx"], lru_ba[l][None], lru_bx[l][None], lru_lambda[l][None],
                         tb=256, name=f"lru_fwd_{l}")
        qkv = conv_fwd(proj, C_QKV, 3 * BRANCH_W, p["gcw"], jnp.zeros((1, 3 * BRANCH_W), F32), tb=512,
                       name=f"gdn_conv_fwd_{l}")
        yc, sall = gdn_fwd(qkv, proj, p["alog"], p["dt"], gdn_norm_g[l][None], cps=GDN_CPS, name=f"gdn_fwd_{l}")
        dep2 = None
        if l == 0:
            in_flight, gather_sems, dep2 = gather_pass(in_flight, gather_sems, yc, name="gather_weights_1_pass")
        yd = pool_fwd(proj, pool_w[l], pool_scale[l][None], tb=512, name=f"pool_fwd_{l}")
        ys = (ya, yb, yc, yd)
        x2, merged = merge_fwd(x1, ys, proj, p["wb"], p["wo"], tb=256, dep=dep2, name=f"merge_fwd_{l}")
        x3 = ffn_fwd(x2, ff2_norm[l][None], *p["ff2"], tb=512, name=f"ffn2_fwd_{l}")
        if l == 0:
            gathered[1] = gather_wait(in_flight, gather_sems, x3, name="gather_weights_1_wait")
        sv.update(x1=x1, h=h, proj=proj, xc=xc, hl=hl, qkv=qkv, sall=sall, ys=ys, merged=merged, x2=x2)
        saved.append(sv)
        xs = x3

    dx, d_final, loss_blk = loss_head(xs, final_norm[None], loss_target[0], tb=512, name="loss_head")
    loss = lax.psum(loss_blk[0, 0], ("x", "y", "c"))

    small_g = {n: [None] * n_layer for n in _SMALL if n != "final_norm"}
    big_g = [None] * n_layer
    per_layer = 7
    flight = None

    def pair_sums(bigs, recv, first):
        part = [sum_halves(t, r, c_idx, name=f"grad_pair_sum_{first + i}") for i, (t, r) in enumerate(zip(bigs, recv))]
        part[2] = _w_in_grad_to_shards(part[2])
        return part

    for l in reversed(range(n_layer)):
        sv = saved[l]
        p = sv["p"]
        proj = sv["proj"]
        dx2, dg_ff2, dab2, hid2, h2, dy2 = ffn_bwd(sv["x2"], dx, ff2_norm[l][None], *p["ff2"], tb=512,
                                                  dep=None if flight is None else flight[3],
                                                  name=f"ffn2_bwd_{l}")
        dwgu2 = matmul(h2, dab2, ta=True, tm=1024, tn=FF_SHARD, tk=TK_DW, name=f"ffn2_dwgu_{l}")
        dwd2 = matmul(hid2, dy2, ta=True, tm=FF_SHARD, tn=1024, tk=TK_DW, name=f"ffn2_dwd_{l}")
        if flight is not None:
            sent, recv = copies_wait(*flight[:3], dx2, _pair_plan, name="grad_pair_exchange_1_wait")
            part1 = pair_sums(sent, recv, per_layer)
            flight = copies_start(part1, [jax.ShapeDtypeStruct(t.shape, t.dtype) for t in part1], _chip_plan, 3,
                                  name="grad_chip_exchange_1_start")
        dgp, dbr, dya, dyb, dyc, dyd, dx2b = merge_bwd(dx2, sv["ys"], proj, p["wb"], p["wo"], tb=256,
                                                       dep=None if flight is None else flight[3],
                                                       name=f"merge_bwd_{l}")
        dwo = matmul(sv["merged"], dx2b, ta=True, tm=1024, tn=1024, tk=TK_DW, name=f"dwo_{l}")
        dwb = jnp.stack([matmul(sv["ys"][g], dbr[g], ta=True, tm=BRANCH_W, tn=1024, tk=TK_DW, name=f"dwb{g}_{l}")
                         for g in range(4)])
        du, dv, dlg, dlb, dws, dbt = sgu_bwd(proj, dya, sgu_ln_g[l][None], sgu_ln_b[l][None], sgu_w[l], p["bt"],
                                             tb=512, name=f"sgu_bwd_{l}")
        dxc, dgate, dwa, dwx, dba, dbx, dlam = lru_bwd(sv["xc"], proj, sv["hl"], dyb, p["wa"], p["wx"],
                                                       lru_ba[l][None], lru_bx[l][None], lru_lambda[l][None],
                                                       tb=256, name=f"lru_bwd_{l}")
        dbx_in, dcw_l = conv_bwd(proj, C_BX, BRANCH_W, dxc, p["lcw"], tb=512, name=f"lru_conv_bwd_{l}")
        dqkv, dz, dbeta, dal, ddt, dng = gdn_bwd(sv["qkv"], proj, sv["sall"], dyc, p["alog"], p["dt"],
                                                 gdn_norm_g[l][None], cps=GDN_CPS, name=f"gdn_bwd_{l}")
        dqkv_in, dcw_g = conv_bwd(proj, C_QKV, 3 * BRANCH_W, dqkv, p["gcw"], tb=512, name=f"gdn_conv_bwd_{l}")
        ddx, dwp, dsc = pool_bwd(proj, dyd, pool_w[l], pool_scale[l][None], tb=512, name=f"pool_bwd_{l}")
        s = dx.shape[0]
        dproj = jnp.concatenate([dqkv_in, dz, du, dv, dbx_in, dgate, ddx, dbeta,
                                 jnp.zeros((s, C_GATE - C_BA - 128), BF16), dgp], axis=1)
        dh = matmul(dproj, p["w_in"], tb=True, tm=1024, tn=1024, tk=1536, name=f"dh_mix_{l}")
        dw_in = matmul(sv["h"], dproj, ta=True, tm=1024, tn=1152, tk=TK_DW, col_groups=N_SHARD, name=f"dw_in_{l}")
        dx1, dg_mix = norm_bwd(sv["x1"], mix_norm[l][None], dh, dx2, tb=512, name=f"mix_norm_bwd_{l}")
        dx, dg_ff1, dab1, hid1, h1, dy1 = ffn_bwd(sv["x0"], dx1, ff1_norm[l][None], *p["ff1"],
                                                 tb=512, name=f"ffn1_bwd_{l}")
        dwgu1 = matmul(h1, dab1, ta=True, tm=1024, tn=FF_SHARD, tk=TK_DW, name=f"ffn1_dwgu_{l}")
        dwd1 = matmul(hid1, dy1, ta=True, tm=FF_SHARD, tn=1024, tk=TK_DW, name=f"ffn1_dwd_{l}")

        for n, g in (("ff1_norm", dg_ff1[0]), ("mix_norm", dg_mix[0]), ("sgu_ln_g", dlg[0]), ("sgu_ln_b", dlb[0]),
                     ("sgu_w", dws), ("sgu_b", dbt[:, :4].T), ("lru_conv_w", dcw_l[:4]), ("lru_conv_b", dcw_l[4]),
                     ("lru_wa", _diag_blocks(dwa)), ("lru_ba", dba[0]), ("lru_wx", _diag_blocks(dwx)),
                     ("lru_bx", dbx[0]), ("lru_lambda", dlam[0]), ("gdn_conv_w", dcw_g[:4]),
                     ("gdn_a_log", dal[0, 4:8]), ("gdn_dt_bias", ddt[0, 4:8]), ("gdn_norm_g", dng[0]),
                     ("pool_w", dwp), ("pool_scale", dsc[0]), ("ff2_norm", dg_ff2[0])):
            small_g[n][l] = g
        dwb_sh = dwb.reshape(4, BRANCH_W, N_SHARD, d // N_SHARD).transpose(2, 0, 1, 3)
        big_g[l] = [dwgu1, dwd1.reshape(N_SHARD, 2, FF_SHARD // 2, d),
                    dw_in.reshape(N_SHARD, 2, d // 2, P_PAD // N_SHARD),
                    dwb_sh.reshape(N_SHARD, 2, 2 * BRANCH_W, d // N_SHARD),
                    dwo.reshape(N_SHARD, 2, d // 8, d), dwgu2,
                    dwd2.reshape(N_SHARD, 2, FF_SHARD // 2, d)]
        if l == n_layer - 1:
            flight = copies_start(big_g[l], [jax.ShapeDtypeStruct((N_SHARD,) + t.shape[2:], t.dtype) for t in big_g[l]],
                                  _pair_plan, N_SHARD, name="grad_pair_exchange_1_start")

    part1, recv2 = copies_wait(*flight[:3], dx, _chip_plan, name="grad_chip_exchange_1_wait")
    red1 = [sum_chips(r, p, place, name=f"grad_chip_sum_{per_layer + i}") for i, (r, p) in enumerate(zip(recv2, part1))]
    recv = pair_send_halves(big_g[0], name="grad_pair_exchange_0")
    part0 = pair_sums(big_g[0], recv, 0)
    recv2 = chip_scatter(part0, name="grad_chip_exchange_0")
    red0 = [sum_chips(r, p, place, name=f"grad_chip_sum_{i}") for i, (r, p) in enumerate(zip(recv2, part0))]
    full = pair_gather(red0 + red1, name="grad_pair_gather")
    grads = {}

    def layers(k, f):
        return jnp.stack([f(full[l * per_layer + k]) for l in range(n_layer)])

    grads["ff1_wg"] = layers(0, lambda t: t[0])
    grads["ff1_wu"] = layers(0, lambda t: t[1])
    grads["ff1_wd"] = layers(1, lambda t: t.reshape(FF_SHARD, d))
    grads["w_in"] = layers(2, lambda t: t.reshape(d, P_IN_SHARD_PAD)[:, :P_IN_SHARD])
    grads["w_branch"] = layers(3, lambda t: t.reshape(4, BRANCH_W, d // N_SHARD))
    grads["w_out"] = layers(4, lambda t: t.reshape(d // N_SHARD, d))
    grads["ff2_wg"] = layers(5, lambda t: t[0])
    grads["ff2_wu"] = layers(5, lambda t: t[1])
    grads["ff2_wd"] = layers(6, lambda t: t.reshape(FF_SHARD, d))

    small_full = [jnp.stack(small_g[n]) for n in _SMALL if n != "final_norm"] + [d_final[0]]
    summed = _unpack(all_sum_small(_pack(small_full), name="grad_small_allreduce"), [t.shape for t in small_full])
    for n, g in zip(_SMALL, summed):
        if n in _CONV:
            width = a[n].shape[-1]
            g = lax.dynamic_slice_in_dim(g, chip * width, width, axis=2)
        grads[n] = g

    delta, new_m, new_v = {}, {}, {}
    for n in _BIG:
        delta[n], new_m[n], new_v[n] = adamw(a[n], grads[n], a["m_" + n], a["v_" + n], name=f"adamw_{n}")
    shapes = [a[n].shape for n in _SMALL]
    packed = [_pack([src[pre + n] for n in _SMALL]) for src, pre in ((a, ""), (grads, ""), (a, "m_"), (a, "v_"))]
    for store, buf in zip((delta, new_m, new_v), adamw(*packed, name="adamw_small")):
        store.update(zip(_SMALL, _unpack(buf, shapes)))

    return (loss, dx[None], *[grads[n] for n in _WEIGHTS], *[delta[n] for n in _WEIGHTS],
            *[new_m[n] for n in _WEIGHTS], *[new_v[n] for n in _WEIGHTS])
```

```python
import functools
import math

import jax
import jax.numpy as jnp
from jax import lax
from jax.experimental import pallas as pl
from jax.experimental.pallas import tpu as pltpu

F32 = jnp.float32
BF16 = jnp.bfloat16
MESH = pl.DeviceIdType.MESH

D_MODEL = 1024
N_SHARD = 4
D_FF = 2816
FF_SHARD = D_FF // N_SHARD
BRANCH_W = 512
CHUNK = 64
GDN_HEADS = 4
GDN_DK = 128
GDN_CPS = 4
TK_DW = 2048
LRU_C = 8.0
EPS = 1e-6
P_IN = 8712
P_IN_SHARD = P_IN // N_SHARD
P_IN_SHARD_PAD = 2304
P_PAD = 9216
C_QKV, C_Z, C_AU, C_AV, C_BX, C_BG, C_DX, C_BA, C_GATE = 0, 1536, 2048, 2560, 3072, 3584, 4096, 4608, 5120

ADAM_LR, ADAM_B1, ADAM_B2, ADAM_EPS, ADAM_WD, ADAM_STEP = 0.001, 0.9, 0.999, 1e-08, 0.01, 10


def _gelu(x):
    return 0.5 * x * (1.0 + jnp.tanh(0.7978845608028654 * (x + 0.044715 * (x * x * x))))


def _gelu_grad(x):
    u = 0.7978845608028654 * (x + 0.044715 * (x * x * x))
    t = jnp.tanh(u)
    return 0.5 * (1.0 + t) + 0.5 * x * (1.0 - t * t) * 0.7978845608028654 * (1.0 + 3.0 * 0.044715 * x * x)


def _silu(x):
    return x * jax.nn.sigmoid(x)


def _softplus(x):
    return jnp.maximum(x, 0.0) + jnp.log1p(jnp.exp(-jnp.abs(x)))


def _dg(a, b, ca, cb):
    return lax.dot_general(a.astype(BF16), b.astype(BF16), (((ca,), (cb,)), ((), ())),
                           preferred_element_type=F32)


def _dg3(a, b, ca, cb):
    a1 = a.astype(BF16)
    a2 = (a - a1.astype(F32)).astype(BF16)
    b1 = b.astype(BF16)
    b2 = (b - b1.astype(F32)).astype(BF16)
    dn = (((ca,), (cb,)), ((), ()))
    d = functools.partial(lax.dot_general, dimension_numbers=dn, preferred_element_type=F32)
    return d(a1, b1) + (d(a1, b2) + d(a2, b1))


def _make_mm(ca, cb, dot):
    @jax.custom_vjp
    def f(a, b):
        return dot(a, b, ca, cb)

    def fwd(a, b):
        return dot(a, b, ca, cb), (a, b)

    def bwd(res, g):
        a, b = res
        if (ca, cb) == (1, 0):
            return dot(g, b, 1, 1), dot(a, g, 0, 0)
        if (ca, cb) == (1, 1):
            return dot(g, b, 1, 0), dot(g, a, 0, 0)
        return dot(b, g, 1, 1), dot(a, g, 1, 0)

    f.defvjp(fwd, bwd)
    return f


mm_nn = _make_mm(1, 0, _dg)


def _bdg(a, b, ca, cb):
    return lax.dot_general(a.astype(BF16), b.astype(BF16), (((ca,), (cb,)), ((0,), (0,))),
                           preferred_element_type=F32)


def _split2(x):
    x1 = x.astype(BF16)
    return x1, (x - x1.astype(F32)).astype(BF16)


def _bdg3(a, b, ca, cb):
    a1, a2 = _split2(a)
    b1, b2 = _split2(b)
    d = functools.partial(lax.dot_general, dimension_numbers=(((ca,), (cb,)), ((0,), (0,))),
                          preferred_element_type=F32)
    return d(a1, b1) + (d(a1, b2) + d(a2, b1))


def _make_bmm(ca, cb):
    @jax.custom_vjp
    def f(a, b):
        return _bdg(a, b, ca, cb)

    def fwd(a, b):
        return _bdg(a, b, ca, cb), (a, b)

    def bwd(res, g):
        a, b = res
        if (ca, cb) == (2, 1):
            return _bdg(g, b, 2, 2), _bdg(a, g, 1, 1)
        if (ca, cb) == (2, 2):
            return _bdg(g, b, 2, 1), _bdg(g, a, 1, 1)
        return _bdg(b, g, 2, 2), _bdg(a, g, 2, 1)

    f.defvjp(fwd, bwd)
    return f


bmm_nn = _make_bmm(2, 1)
bmm_nt = _make_bmm(2, 2)
bmm_tn = _make_bmm(1, 1)


def _iota3(shape, dim):
    return lax.broadcasted_iota(jnp.int32, shape, dim)


@jax.custom_vjp
def _tri_inv(aw):
    b, n, _ = aw.shape
    shp = (b, n, 2 * n)
    left = _iota3(shp, 2) < n
    x = jnp.where(_iota3(shp, 1) == _iota3(shp, 2), 1.0, 0.0) - aw
    zero = jnp.zeros((b, n, 4 * n), BF16)
    k = 1
    while k < n:
        x1, x2 = _split2(x)
        lhs = jnp.concatenate([x1, x2], axis=1)
        rhs = jnp.concatenate([zero, jnp.concatenate([x1, x2], axis=2)], axis=1)
        r = lax.dot_general(lhs, rhs, (((2,), (1,)), ((0,), (0,))), preferred_element_type=F32)
        mx = (r[:, :n, :2 * n] + r[:, :n, 2 * n:]) + (r[:, n:, :2 * n] + r[:, n:, 2 * n:])
        x = jnp.where(left, x, 0.0) + mx
        k *= 2
    return jnp.where(left, x, 0.0)


def _tri_inv_fwd(aw):
    t = _tri_inv(aw)
    return t, t


def _tri_inv_bwd(t, dt):
    n = t.shape[1]
    x = _bdg3(t, dt, 1, 1)[:, :n]
    tp = jnp.concatenate([jnp.zeros_like(t), t], axis=1)
    return (-_bdg3(x, tp, 2, 2),)


_tri_inv.defvjp(_tri_inv_fwd, _tri_inv_bwd)


def _rows(shape):
    return lax.broadcasted_iota(jnp.int32, shape, 0)


def _lanes(shape):
    return lax.broadcasted_iota(jnp.int32, shape, 1)


def _col(x, j):
    return jnp.sum(jnp.where(_lanes(x.shape) == j, x, 0.0), axis=1, keepdims=True)


def _put_col(col, j, width):
    shape = (col.shape[0], width)
    return jnp.where(_lanes(shape) == j, jnp.broadcast_to(col, shape), 0.0)


VMEM_BYTES_V7X = 64 << 20


def _params(*sem, vmem_bytes=None):
    return pltpu.CompilerParams(dimension_semantics=sem, vmem_limit_bytes=vmem_bytes)


def matmul(a, b, *, ta=False, tb=False, tm, tn, tk, out_dtype=F32, col_groups=1, name):
    ba, bb = a.shape[:-2], b.shape[:-2]
    batch = ba if len(ba) >= len(bb) else bb
    assert ba in ((), batch) and bb in ((), batch)
    nb = len(batch)
    (m, k) = (a.shape[-1], a.shape[-2]) if ta else (a.shape[-2], a.shape[-1])
    (k2, n) = (b.shape[-1], b.shape[-2]) if tb else (b.shape[-2], b.shape[-1])
    tm, tn, tk = min(tm, m), min(tn, n), min(tk, k)
    assert k == k2 and m % tm == 0 and n % tn == 0 and k % tk == 0, (a.shape, b.shape, tm, tn, tk)
    nk = k // tk
    grid = batch + (m // tm, n // tn, nk)

    def a_map(*g):
        i, kk = g[nb], g[nb + 2]
        return (g[:nb] if ba else ()) + ((kk, i) if ta else (i, kk))

    def b_map(*g):
        j, kk = g[nb + 1], g[nb + 2]
        return (g[:nb] if bb else ()) + ((j, kk) if tb else (kk, j))

    per_group = n // tn // col_groups
    assert col_groups == 1 or (nb == 0 and per_group * col_groups * tn == n)

    def o_map(*g):
        if col_groups > 1:
            return (g[1] // per_group, g[0], g[1] % per_group)
        return g[:nb] + (g[nb], g[nb + 1])

    a_blk = (None,) * len(ba) + ((tk, tm) if ta else (tm, tk))
    b_blk = (None,) * len(bb) + ((tn, tk) if tb else (tk, tn))
    o_blk = (None,) * (nb + (col_groups > 1)) + (tm, tn)
    o_shape = (col_groups, m, n // col_groups) if col_groups > 1 else batch + (m, n)

    def body(a_ref, b_ref, o_ref, acc_ref):
        kk = pl.program_id(nb + 2)

        @pl.when(kk == 0)
        def _():
            acc_ref[...] = jnp.zeros_like(acc_ref)

        acc_ref[...] += _dg(a_ref[...], b_ref[...], 0 if ta else 1, 1 if tb else 0)

        @pl.when(kk == nk - 1)
        def _():
            o_ref[...] = acc_ref[...].astype(o_ref.dtype)

    return pl.pallas_call(
        body, name=name, grid=grid,
        in_specs=[pl.BlockSpec(a_blk, a_map), pl.BlockSpec(b_blk, b_map)],
        out_specs=pl.BlockSpec(o_blk, o_map),
        out_shape=jax.ShapeDtypeStruct(o_shape, out_dtype),
        scratch_shapes=[pltpu.VMEM((tm, tn), F32)],
        compiler_params=_params(*(("parallel",) * (nb + 2) + ("arbitrary",))),
    )(a, b)


def norm_fwd(x, g, *, tb, name):
    s, d = x.shape

    def body(x_ref, g_ref, h_ref):
        xv = x_ref[...]
        r = lax.rsqrt(jnp.mean(xv * xv, axis=-1, keepdims=True) + EPS)
        h_ref[...] = (xv * r * g_ref[...]).astype(BF16)

    return pl.pallas_call(
        body, name=name, grid=(s // tb,),
        in_specs=[pl.BlockSpec((tb, d), lambda i: (i, 0)), pl.BlockSpec((1, d), lambda i: (0, 0))],
        out_specs=pl.BlockSpec((tb, d), lambda i: (i, 0)),
        out_shape=jax.ShapeDtypeStruct((s, d), BF16),
        compiler_params=_params("parallel"),
    )(x, g)


def _rms_bwd(xv, gv, dh):
    r = lax.rsqrt(jnp.mean(xv * xv, axis=-1, keepdims=True) + EPS)
    xh = xv * r
    dxh = dh * gv
    dx = r * (dxh - xh * jnp.mean(dxh * xh, axis=-1, keepdims=True))
    return dx, jnp.sum(dh * xh, axis=0, keepdims=True)


def norm_bwd(x, g, dh, dres, *, tb, name):
    s, d = x.shape

    def body(x_ref, g_ref, dh_ref, dres_ref, dx_ref, dg_ref):
        dx, dg = _rms_bwd(x_ref[...], g_ref[...], dh_ref[...])
        dx_ref[...] = dres_ref[...] + dx

        @pl.when(pl.program_id(0) == 0)
        def _():
            dg_ref[...] = jnp.zeros_like(dg_ref)

        dg_ref[...] += dg

    row = pl.BlockSpec((tb, d), lambda i: (i, 0))
    one = pl.BlockSpec((1, d), lambda i: (0, 0))
    return pl.pallas_call(
        body, name=name, grid=(s // tb,),
        in_specs=[row, one, row, row], out_specs=[row, one],
        out_shape=[jax.ShapeDtypeStruct((s, d), F32), jax.ShapeDtypeStruct((1, d), F32)],
        compiler_params=_params("arbitrary"),
    )(x, g, dh, dres)


def _dep_operand(dep):
    return ([], []) if dep is None else ([_ANY], [dep])


def ffn_fwd(x, g, wg, wu, wd, layer, *, tb, dep=None, name):
    s, d = x.shape
    fc = wd.shape[-2]
    dep_specs, dep_args = _dep_operand(dep)

    def body(x_ref, g_ref, wg_ref, wu_ref, wd_ref, *rest):
        o_ref, h_scr, acc_scr = rest[len(dep_args):]
        j = pl.program_id(1)

        @pl.when(j == 0)
        def _():
            xv = x_ref[...]
            r = lax.rsqrt(jnp.mean(xv * xv, axis=-1, keepdims=True) + EPS)
            h_scr[...] = (xv * r * g_ref[...]).astype(BF16)
            acc_scr[...] = jnp.zeros_like(acc_scr)

        h = h_scr[...]
        a = _dg(h, wg_ref[...], 1, 0)
        b = _dg(h, wu_ref[...], 1, 0)
        acc_scr[...] += _dg(_silu(a) * b, wd_ref[...], 1, 0)

        @pl.when(j == N_SHARD - 1)
        def _():
            o_ref[...] = x_ref[...] + 0.5 * acc_scr[...]

    row = pl.BlockSpec((tb, d), lambda i, j: (i, 0))
    lg, lu, ld = layer
    return pl.pallas_call(
        body, name=name, grid=(s // tb, N_SHARD),
        in_specs=[row, pl.BlockSpec((1, d), lambda i, j: (0, 0)),
                  pl.BlockSpec((None, None, d, fc), lambda i, j: (j, lg, 0, 0)),
                  pl.BlockSpec((None, None, d, fc), lambda i, j: (j, lu, 0, 0)),
                  pl.BlockSpec((None, None, fc, d), lambda i, j: (j, ld, 0, 0))] + dep_specs,
        out_specs=row,
        out_shape=jax.ShapeDtypeStruct((s, d), F32),
        scratch_shapes=[pltpu.VMEM((tb, d), BF16), pltpu.VMEM((tb, d), F32)],
        compiler_params=_params("parallel", "arbitrary"),
    )(x, g, wg, wu, wd, *dep_args)


def ffn_bwd(x, dxo, g, wg, wu, wd, layer, *, tb, dep=None, name):
    s, d = x.shape
    fc = wd.shape[-2]
    dep_specs, dep_args = _dep_operand(dep)

    def body(x_ref, dxo_ref, g_ref, wg_ref, wu_ref, wd_ref, *rest):
        dx_ref, dg_ref, dab_ref, hid_ref, h_ref, dy_ref, dh_scr = rest[len(dep_args):]
        i, j = pl.program_id(0), pl.program_id(1)

        @pl.when(j == 0)
        def _():
            xv = x_ref[...]
            r = lax.rsqrt(jnp.mean(xv * xv, axis=-1, keepdims=True) + EPS)
            h_ref[...] = (xv * r * g_ref[...]).astype(BF16)
            dy_ref[...] = (0.5 * dxo_ref[...]).astype(BF16)
            dh_scr[...] = jnp.zeros_like(dh_scr)

        h = h_ref[...]
        a = _dg(h, wg_ref[...], 1, 0)
        b = _dg(h, wu_ref[...], 1, 0)
        sg = jax.nn.sigmoid(a)
        sl = a * sg
        dhid = _dg(dy_ref[...], wd_ref[...], 1, 1)
        da = (dhid * b * (sg * (1.0 + a * (1.0 - sg)))).astype(BF16)
        db = (dhid * sl).astype(BF16)
        dab_ref[0] = da
        dab_ref[1] = db
        hid_ref[...] = (sl * b).astype(BF16)
        dh_scr[...] += _dg(da, wg_ref[...], 1, 1) + _dg(db, wu_ref[...], 1, 1)

        @pl.when(jnp.logical_and(i == 0, j == 0))
        def _():
            dg_ref[...] = jnp.zeros_like(dg_ref)

        @pl.when(j == N_SHARD - 1)
        def _():
            dx, dg = _rms_bwd(x_ref[...], g_ref[...], dh_scr[...])
            dx_ref[...] = dxo_ref[...] + dx
            dg_ref[...] += dg

    row = pl.BlockSpec((tb, d), lambda i, j: (i, 0))
    one = pl.BlockSpec((1, d), lambda i, j: (0, 0))
    lg, lu, ld = layer
    return pl.pallas_call(
        body, name=name, grid=(s // tb, N_SHARD),
        in_specs=[row, row, one,
                  pl.BlockSpec((None, None, d, fc), lambda i, j: (j, lg, 0, 0)),
                  pl.BlockSpec((None, None, d, fc), lambda i, j: (j, lu, 0, 0)),
                  pl.BlockSpec((None, None, fc, d), lambda i, j: (j, ld, 0, 0))] + dep_specs,
        out_specs=[row, one,
                   pl.BlockSpec((None, 2, tb, fc), lambda i, j: (j, 0, i, 0)),
                   pl.BlockSpec((None, tb, fc), lambda i, j: (j, i, 0)),
                   row, row],
        out_shape=[jax.ShapeDtypeStruct((s, d), F32), jax.ShapeDtypeStruct((1, d), F32),
                   jax.ShapeDtypeStruct((N_SHARD, 2, s, fc), BF16),
                   jax.ShapeDtypeStruct((N_SHARD, s, fc), BF16),
                   jax.ShapeDtypeStruct((s, d), BF16), jax.ShapeDtypeStruct((s, d), BF16)],
        scratch_shapes=[pltpu.VMEM((tb, d), F32)],
        compiler_params=_params("arbitrary", "arbitrary",
                                vmem_bytes=min(VMEM_BYTES_V7X - (8 << 20),
                                               12 * d * fc + 44 * tb * d + 40 * tb * fc + (4 << 20))),
    )(x, dxo, g, wg, wu, wd, *dep_args)


def _pblk(tb, width, col0, rev_n=None):
    assert col0 % width == 0
    cb = col0 // width
    if rev_n is None:
        return pl.BlockSpec((tb, width), lambda i: (i, cb))
    return pl.BlockSpec((tb, width), lambda i: (rev_n - 1 - i, cb))


def _full(shape):
    return pl.BlockSpec(shape, lambda i: (0,) * len(shape))


def conv_fwd(src, col0, width, w, bias, *, tb, name):
    s = src.shape[0]
    cb = col0 // width

    def body(x_ref, p_ref, w_ref, b_ref, y_ref):
        i = pl.program_id(0)
        prev = jnp.where(i > 0, p_ref[...], 0.0)
        xe = jnp.concatenate([prev, x_ref[...]], axis=0)
        wv = w_ref[...]
        acc = b_ref[...] + wv[3:4] * xe[8:]
        for k in range(3):
            acc = acc + wv[k:k + 1] * pltpu.roll(xe, 3 - k, 0)[8:]
        y_ref[...] = acc

    return pl.pallas_call(
        body, name=name, grid=(s // tb,),
        in_specs=[_pblk(tb, width, col0),
                  pl.BlockSpec((8, width), lambda i: (jnp.maximum(i * (tb // 8) - 1, 0), cb)),
                  _full((4, width)), _full((1, width))],
        out_specs=pl.BlockSpec((tb, width), lambda i: (i, 0)),
        out_shape=jax.ShapeDtypeStruct((s, width), F32),
        compiler_params=_params("parallel"),
    )(src, src, w, bias)


def conv_bwd(src, col0, width, dpre, w, *, tb, name):
    s = src.shape[0]
    cb = col0 // width
    nt = s // tb

    def body(x_ref, p_ref, d_ref, n_ref, w_ref, dx_ref, dwb_ref):
        i = pl.program_id(0)
        d = d_ref[...]
        nxt = jnp.where(i < nt - 1, n_ref[...], 0.0)
        de = jnp.concatenate([d, nxt], axis=0)
        wv = w_ref[...]
        dx = wv[3:4] * d
        for k in range(3):
            dx = dx + wv[k:k + 1] * pltpu.roll(de, tb + 8 - (3 - k), 0)[:tb]
        dx_ref[...] = dx.astype(BF16)

        @pl.when(i == 0)
        def _():
            dwb_ref[...] = jnp.zeros_like(dwb_ref)

        prev = jnp.where(i > 0, p_ref[...], 0.0)
        xe = jnp.concatenate([prev, x_ref[...]], axis=0)
        dwb_ref[3:4, :] += jnp.sum(d * xe[8:], axis=0, keepdims=True)
        for k in range(3):
            dwb_ref[k:k + 1, :] += jnp.sum(d * pltpu.roll(xe, 3 - k, 0)[8:], axis=0, keepdims=True)
        dwb_ref[4:5, :] += jnp.sum(d, axis=0, keepdims=True)

    return pl.pallas_call(
        body, name=name, grid=(nt,),
        in_specs=[_pblk(tb, width, col0),
                  pl.BlockSpec((8, width), lambda i: (jnp.maximum(i * (tb // 8) - 1, 0), cb)),
                  pl.BlockSpec((tb, width), lambda i: (i, 0)),
                  pl.BlockSpec((8, width), lambda i: (jnp.minimum((i + 1) * (tb // 8), s // 8 - 1), 0)),
                  _full((4, width))],
        out_specs=[pl.BlockSpec((tb, width), lambda i: (i, 0)), _full((8, width))],
        out_shape=[jax.ShapeDtypeStruct((s, width), BF16), jax.ShapeDtypeStruct((8, width), F32)],
        compiler_params=_params("arbitrary"),
    )(src, src, dpre, dpre, w)


SGU_BLOCK = 128


def _sgu_pre(u, v, lg, lb):
    ug = _gelu(u)
    vg = _gelu(v)
    mu = jnp.mean(vg, axis=-1, keepdims=True)
    var = jnp.mean(jnp.square(vg - mu), axis=-1, keepdims=True)
    return ug, (vg - mu) * lax.rsqrt(var + EPS) * lg + lb


def _sgu_mask():
    shp = (SGU_BLOCK, SGU_BLOCK)
    return (_rows(shp) // CHUNK) >= (_lanes(shp) // CHUNK)


def sgu_fwd(proj, lg, lb, ws, bt, *, tb, name):
    s = proj.shape[0]
    w = BRANCH_W

    def body(u_ref, v_ref, lg_ref, lb_ref, w_ref, b_ref, y_ref):
        ug, vn = _sgu_pre(u_ref[...], v_ref[...], lg_ref[...], lb_ref[...])
        mask = _sgu_mask()
        for g in range(4):
            wm = jnp.where(mask, w_ref[g], 0.0)
            bcol = _col(b_ref[...], g)
            cs = slice(128 * g, 128 * g + 128)
            for n in range(tb // SGU_BLOCK):
                rs = slice(SGU_BLOCK * n, SGU_BLOCK * (n + 1))
                mixed = _dg(wm, vn[rs, cs], 1, 0) + bcol
                y_ref[rs, cs] = (ug[rs, cs] * mixed).astype(BF16)

    return pl.pallas_call(
        body, name=name, grid=(s // tb,),
        in_specs=[_pblk(tb, w, C_AU), _pblk(tb, w, C_AV), _full((1, w)), _full((1, w)),
                  _full((4, 128, 128)), _full((128, 128))],
        out_specs=pl.BlockSpec((tb, w), lambda i: (i, 0)),
        out_shape=jax.ShapeDtypeStruct((s, w), BF16),
        compiler_params=_params("parallel"),
    )(proj, proj, lg, lb, ws, bt)


def sgu_bwd(proj, dy, lg, lb, ws, bt, *, tb, name):
    s = proj.shape[0]
    w = BRANCH_W

    def body(u_ref, v_ref, dy_ref, lg_ref, lb_ref, w_ref, b_ref,
             du_ref, dv_ref, dlg_ref, dlb_ref, dw_ref, db_ref, dug_scr, dvn_scr):
        @pl.when(pl.program_id(0) == 0)
        def _():
            dlg_ref[...] = jnp.zeros_like(dlg_ref)
            dlb_ref[...] = jnp.zeros_like(dlb_ref)
            dw_ref[...] = jnp.zeros_like(dw_ref)
            db_ref[...] = jnp.zeros_like(db_ref)

        (ug, vn), vf = jax.vjp(_sgu_pre, u_ref[...], v_ref[...], lg_ref[...], lb_ref[...])
        dyv = dy_ref[...]
        mask = _sgu_mask()
        for g in range(4):
            wm = jnp.where(mask, w_ref[g], 0.0)
            bcol = _col(b_ref[...], g)
            cs = slice(128 * g, 128 * g + 128)
            dwg = jnp.zeros((SGU_BLOCK, SGU_BLOCK), F32)
            dbc = jnp.zeros((SGU_BLOCK, 1), F32)
            for n in range(tb // SGU_BLOCK):
                rs = slice(SGU_BLOCK * n, SGU_BLOCK * (n + 1))
                blk = vn[rs, cs]
                mixed = _dg(wm, blk, 1, 0) + bcol
                dmix = dyv[rs, cs] * ug[rs, cs]
                dug_scr[rs, cs] = dyv[rs, cs] * mixed
                dvn_scr[rs, cs] = _dg(wm, dmix, 0, 0)
                dwg = dwg + _dg(dmix, blk, 1, 1)
                dbc = dbc + jnp.sum(dmix, axis=1, keepdims=True)
            dw_ref[g] += jnp.where(mask, dwg, 0.0)
            db_ref[...] += _put_col(dbc, g, 128)
        du, dv, dlg, dlb = vf((dug_scr[...], dvn_scr[...]))
        du_ref[...] = du.astype(BF16)
        dv_ref[...] = dv.astype(BF16)
        dlg_ref[...] += dlg
        dlb_ref[...] += dlb

    row = pl.BlockSpec((tb, w), lambda i: (i, 0))
    return pl.pallas_call(
        body, name=name, grid=(s // tb,),
        in_specs=[_pblk(tb, w, C_AU), _pblk(tb, w, C_AV), row, _full((1, w)), _full((1, w)),
                  _full((4, 128, 128)), _full((128, 128))],
        out_specs=[row, row, _full((1, w)), _full((1, w)), _full((4, 128, 128)), _full((128, 128))],
        out_shape=[jax.ShapeDtypeStruct((s, w), BF16), jax.ShapeDtypeStruct((s, w), BF16),
                   jax.ShapeDtypeStruct((1, w), F32), jax.ShapeDtypeStruct((1, w), F32),
                   jax.ShapeDtypeStruct((4, 128, 128), F32), jax.ShapeDtypeStruct((128, 128), F32)],
        scratch_shapes=[pltpu.VMEM((tb, w), F32), pltpu.VMEM((tb, w), F32)],
        compiler_params=_params("arbitrary"),
    )(proj, proj, dy, lg, lb, ws, bt)


POOL_HALO = 16


def _pooled(xe, t0, tb):
    s2 = xe + pltpu.roll(xe, 1, 0)
    s4 = s2 + pltpu.roll(s2, 2, 0)
    s8 = s4 + pltpu.roll(s4, 4, 0)
    s16 = s8 + pltpu.roll(s8, 8, 0)
    t = (t0 + _rows((tb, 128))).astype(F32)
    outs = []
    for g, sw in enumerate((s2, s4, s8, s16)):
        cs = slice(128 * g, 128 * g + 128)
        cnt = jnp.minimum(t + 1.0, float(2 ** (g + 1)))
        outs.append(sw[POOL_HALO:, cs] / cnt - xe[POOL_HALO:, cs])
    return outs


def pool_fwd(proj, wp, scale, *, tb, name):
    s = proj.shape[0]
    w = BRANCH_W

    def body(x_ref, p_ref, w_ref, sc_ref, y_ref):
        i = pl.program_id(0)
        prev = jnp.where(i > 0, p_ref[...], 0.0)
        xe = jnp.concatenate([prev, x_ref[...]], axis=0)
        pooled = _pooled(xe, i * tb, tb)
        sc = sc_ref[...]
        for g in range(4):
            cs = slice(128 * g, 128 * g + 128)
            y_ref[:, cs] = (_dg(pooled[g], w_ref[g], 1, 0) * sc[:, cs]).astype(BF16)

    cb = C_DX // w
    return pl.pallas_call(
        body, name=name, grid=(s // tb,),
        in_specs=[_pblk(tb, w, C_DX),
                  pl.BlockSpec((POOL_HALO, w), lambda i: (jnp.maximum(i * (tb // POOL_HALO) - 1, 0), cb)),
                  _full((4, 128, 128)), _full((1, w))],
        out_specs=pl.BlockSpec((tb, w), lambda i: (i, 0)),
        out_shape=jax.ShapeDtypeStruct((s, w), BF16),
        compiler_params=_params("parallel"),
    )(proj, proj, wp, scale)


def pool_bwd(proj, dy, wp, scale, *, tb, name):
    s = proj.shape[0]
    w = BRANCH_W
    nt = s // tb
    te = tb + POOL_HALO

    def body(x_ref, p_ref, dy_ref, n_ref, w_ref, sc_ref, dx_ref, dw_ref, dsc_ref):
        i = pl.program_id(0)

        @pl.when(i == 0)
        def _():
            dw_ref[...] = jnp.zeros_like(dw_ref)
            dsc_ref[...] = jnp.zeros_like(dsc_ref)

        prev = jnp.where(i > 0, p_ref[...], 0.0)
        xe = jnp.concatenate([prev, x_ref[...]], axis=0)
        pooled = _pooled(xe, i * tb, tb)
        dyv = dy_ref[...]
        nxt = jnp.where(i < nt - 1, n_ref[...], 0.0)
        dye = jnp.concatenate([dyv, nxt], axis=0)
        sc = sc_ref[...]
        t = (i * tb + _rows((te, 128))).astype(F32)
        for g in range(4):
            cs = slice(128 * g, 128 * g + 128)
            win = 2 ** (g + 1)
            dpm = dye[:, cs] * sc[:, cs]
            dpool = _dg(dpm, w_ref[g], 1, 1)
            q = dpool / jnp.minimum(t + 1.0, float(win))
            r, sh = q, 1
            while sh < win:
                r = r + pltpu.roll(r, te - sh, 0)
                sh *= 2
            dx_ref[:, cs] = (r[:tb] - dpool[:tb]).astype(BF16)
            dw_ref[g] += _dg(pooled[g], dpm[:tb], 0, 0)
            dsc_ref[:, cs] += jnp.sum(dyv[:, cs] * _dg(pooled[g], w_ref[g], 1, 0), axis=0, keepdims=True)

    cb = C_DX // w
    row = pl.BlockSpec((tb, w), lambda i: (i, 0))
    return pl.pallas_call(
        body, name=name, grid=(nt,),
        in_specs=[_pblk(tb, w, C_DX),
                  pl.BlockSpec((POOL_HALO, w), lambda i: (jnp.maximum(i * (tb // POOL_HALO) - 1, 0), cb)),
                  row,
                  pl.BlockSpec((POOL_HALO, w),
                               lambda i: (jnp.minimum((i + 1) * (tb // POOL_HALO), s // POOL_HALO - 1), 0)),
                  _full((4, 128, 128)), _full((1, w))],
        out_specs=[row, _full((4, 128, 128)), _full((1, w))],
        out_shape=[jax.ShapeDtypeStruct((s, w), BF16), jax.ShapeDtypeStruct((4, 128, 128), F32),
                   jax.ShapeDtypeStruct((1, w), F32)],
        compiler_params=_params("arbitrary"),
    )(proj, proj, dy, dy, wp, scale)


def _neg_expm1(z):
    p = 1.0 + z * (1.0 / 9.0)
    for n in (8.0, 7.0, 6.0, 5.0, 4.0, 3.0, 2.0):
        p = 1.0 + z * (1.0 / n) * p
    return jnp.where(z > -0.5, -z * p, 1.0 - jnp.exp(z))


def _lru_gates(xc, wa, wx, ba, bx, lam):
    r = jax.nn.sigmoid(mm_nn(xc, wa) + ba)
    i = jax.nn.sigmoid(mm_nn(xc, wx) + bx)
    log_a = -LRU_C * r * _softplus(-lam)
    a = jnp.exp(log_a)
    return a, jnp.sqrt(_neg_expm1(2.0 * log_a)) * (i * xc)


def _scan_down(a, b):
    n = a.shape[0]
    row = _rows(a.shape)
    sh = 1
    while sh < n:
        keep = row >= sh
        a_sh = jnp.where(keep, pltpu.roll(a, sh, 0), 1.0)
        b_sh = jnp.where(keep, pltpu.roll(b, sh, 0), 0.0)
        b = a * b_sh + b
        a = a * a_sh
        sh *= 2
    return a, b


def _scan_up(a, b):
    n = a.shape[0]
    row = _rows(a.shape)
    sh = 1
    while sh < n:
        keep = row < n - sh
        a_sh = jnp.where(keep, pltpu.roll(a, n - sh, 0), 1.0)
        b_sh = jnp.where(keep, pltpu.roll(b, n - sh, 0), 0.0)
        b = a * b_sh + b
        a = a * a_sh
        sh *= 2
    return a, b


def lru_fwd(xc, proj, wa, wx, ba, bx, lam, *, tb, name):
    s = xc.shape[0]
    w = BRANCH_W

    def body(xc_ref, gate_ref, wa_ref, wx_ref, ba_ref, bx_ref, lam_ref, y_ref, h_ref, carry):
        @pl.when(pl.program_id(0) == 0)
        def _():
            carry[...] = jnp.zeros_like(carry)

        a, b = _lru_gates(xc_ref[...], wa_ref[...], wx_ref[...], ba_ref[...], bx_ref[...], lam_ref[...])
        pa, hb = _scan_down(a, b)
        h = pa * carry[...] + hb
        h_ref[...] = h
        carry[...] = h_ref[tb - 1:tb, :]
        y_ref[...] = (h * _gelu(gate_ref[...])).astype(BF16)

    row = pl.BlockSpec((tb, w), lambda i: (i, 0))
    return pl.pallas_call(
        body, name=name, grid=(s // tb,),
        in_specs=[row, _pblk(tb, w, C_BG), _full((w, w)), _full((w, w)),
                  _full((1, w)), _full((1, w)), _full((1, w))],
        out_specs=[row, row],
        out_shape=[jax.ShapeDtypeStruct((s, w), BF16), jax.ShapeDtypeStruct((s, w), F32)],
        scratch_shapes=[pltpu.VMEM((1, w), F32)],
        compiler_params=_params("arbitrary"),
    )(xc, proj, wa, wx, ba, bx, lam)


def lru_bwd(xc, proj, h, dy, wa, wx, ba, bx, lam, *, tb, name):
    s = xc.shape[0]
    w = BRANCH_W
    nt = s // tb

    def body(xc_ref, gate_ref, h_ref, hp_ref, dy_ref, wa_ref, wx_ref, ba_ref, bx_ref, lam_ref,
             dxc_ref, dgate_ref, dwa_ref, dwx_ref, dba_ref, dbx_ref, dlam_ref, carry):
        i = pl.program_id(0)

        @pl.when(i == 0)
        def _():
            carry[...] = jnp.zeros_like(carry)
            for r in (dwa_ref, dwx_ref, dba_ref, dbx_ref, dlam_ref):
                r[...] = jnp.zeros_like(r)

        (a, b), vf = jax.vjp(_lru_gates, xc_ref[...], wa_ref[...], wx_ref[...],
                             ba_ref[...], bx_ref[...], lam_ref[...])
        gate = gate_ref[...]
        dyv = dy_ref[...]
        hv = h_ref[...]
        row = _rows((tb, w))
        a_next = jnp.where(row < tb - 1, pltpu.roll(a, tb - 1, 0), 1.0)
        pa, xb = _scan_up(a_next, dyv * _gelu(gate))
        dh = xb + pa * carry[...]
        carry[...] = a[0:1, :] * dh[0:1, :]
        h_first = jnp.where(i < nt - 1, hp_ref[7:8, :], 0.0)
        h_prev = jnp.where(row >= 1, pltpu.roll(hv, 1, 0), h_first)
        dxc, dwa, dwx, dba, dbx, dlam = vf((dh * h_prev, dh))
        dxc_ref[...] = dxc
        dgate_ref[...] = (dyv * hv * _gelu_grad(gate)).astype(BF16)
        dwa_ref[...] += dwa
        dwx_ref[...] += dwx
        dba_ref[...] += dba
        dbx_ref[...] += dbx
        dlam_ref[...] += dlam

    row_spec = pl.BlockSpec((tb, w), lambda i: (nt - 1 - i, 0))
    hp_spec = pl.BlockSpec((8, w), lambda i: (jnp.maximum((nt - 1 - i) * (tb // 8) - 1, 0), 0))
    return pl.pallas_call(
        body, name=name, grid=(nt,),
        in_specs=[row_spec, _pblk(tb, w, C_BG, rev_n=nt), row_spec, hp_spec, row_spec,
                  _full((w, w)), _full((w, w)), _full((1, w)), _full((1, w)), _full((1, w))],
        out_specs=[row_spec, row_spec, _full((w, w)), _full((w, w)),
                   _full((1, w)), _full((1, w)), _full((1, w))],
        out_shape=[jax.ShapeDtypeStruct((s, w), F32), jax.ShapeDtypeStruct((s, w), BF16),
                   jax.ShapeDtypeStruct((w, w), F32), jax.ShapeDtypeStruct((w, w), F32),
                   jax.ShapeDtypeStruct((1, w), F32), jax.ShapeDtypeStruct((1, w), F32),
                   jax.ShapeDtypeStruct((1, w), F32)],
        scratch_shapes=[pltpu.VMEM((1, w), F32)],
        compiler_params=_params("arbitrary"),
    )(xc, proj, h, h, dy, wa, wx, ba, bx, lam)


def _gdn_g(ba, alog, dt):
    return -jnp.exp(alog) * _softplus(ba + dt)


def _gdn_local(qp, kp, vp, bpre, gc):
    b, c, dk = qp.shape
    q = _silu(qp)
    k = _silu(kp)
    v = _silu(vp)
    q = q * lax.rsqrt(jnp.sum(q * q, axis=-1, keepdims=True) + EPS) * (dk ** -0.5)
    k = k * lax.rsqrt(jnp.sum(k * k, axis=-1, keepdims=True) + EPS)
    beta = jax.nn.sigmoid(bpre)
    sq = (b, c, c)
    ii, jj = _iota3(sq, 1), _iota3(sq, 2)
    gr = jnp.sum(gc * (ii == jj).astype(F32), axis=1, keepdims=True)
    decay = jnp.exp(jnp.where(ii >= jj, gc - gr, -jnp.inf))
    wd = (b, c, 2 * c)
    wi, wj = _iota3(wd, 1), _iota3(wd, 2) - c
    gr_w = jnp.sum(gc * (wi == wj).astype(F32), axis=1, keepdims=True)
    decay_w = jnp.exp(jnp.where(jnp.logical_and(wj >= 0, wi > wj), gc - gr_w, -jnp.inf))
    kb = k * beta
    pad = jnp.zeros((b, c, dk), F32)
    t = _tri_inv(bmm_nt(kb, jnp.concatenate([pad, k], axis=1)) * decay_w)
    u = bmm_nn(t, jnp.concatenate([v * beta, pad], axis=1))
    wk = bmm_nn(t, jnp.concatenate([kb * jnp.exp(gc), pad], axis=1))
    attn = bmm_nt(q, k) * decay
    g_last = jnp.sum(jnp.where(_iota3((b, c, 1), 1) == c - 1, gc, 0.0), axis=1, keepdims=True)
    return u, wk, attn, q * jnp.exp(gc), k * jnp.exp(g_last - gc), jnp.exp(g_last)


def _gdn_recur(s, u, wk, attn, qg, kd, eg, zz, ng):
    v_new = u - bmm_nn(wk, s)
    o = bmm_nn(qg, s) + bmm_nn(attn, v_new)
    s_out = s * eg + bmm_tn(kd, v_new)
    o = o * lax.rsqrt(jnp.mean(o * o, axis=-1, keepdims=True) + EPS) * ng
    return o * _silu(zz), s_out


def _gdn_inputs(qkv_ref, ba, gcums, cps):
    c, hd = CHUNK, GDN_DK
    qs, ks, vs, bs, gs = [], [], [], [], []
    for ci in range(cps):
        rs = slice(c * ci, c * (ci + 1))
        for h in range(GDN_HEADS):
            qs.append(qkv_ref[rs, hd * h:hd * (h + 1)])
            ks.append(qkv_ref[rs, 512 + hd * h:512 + hd * (h + 1)])
            vs.append(qkv_ref[rs, 1024 + hd * h:1024 + hd * (h + 1)])
            bs.append(_col(ba[rs], h))
            gs.append(_col(gcums[ci], 4 + h))
    return tuple(jnp.stack(t) for t in (qs, ks, vs, bs, gs))


def _chunk_cumsum(g, transpose=False):
    c = CHUNK
    tri = (_rows((c, c)) >= _lanes((c, c))).astype(F32)
    return _dg3(tri, g, 0 if transpose else 1, 0)


def gdn_fwd(qkv, proj, alog, dt, ng, *, cps, name):
    s = qkv.shape[0]
    c = CHUNK
    nc = s // c
    hd = GDN_DK
    tb = c * cps

    def body(qkv_ref, z_ref, ba_ref, al_ref, dt_ref, ng_ref, y_ref, sall_ref, s_scr):
        @pl.when(pl.program_id(0) == 0)
        def _():
            s_scr[...] = jnp.zeros_like(s_scr)

        nh = GDN_HEADS
        ba = ba_ref[...]
        g = _gdn_g(ba, al_ref[...], dt_ref[...])
        gcums = [_chunk_cumsum(g[c * ci:c * (ci + 1)]) for ci in range(cps)]
        loc = _gdn_local(*_gdn_inputs(qkv_ref, ba, gcums, cps))
        for ci in range(cps):
            rs = slice(c * ci, c * (ci + 1))
            sl = slice(nh * ci, nh * (ci + 1))
            s_in = s_scr[...]
            sall_ref[ci] = s_in
            zz = jnp.stack([z_ref[rs, hd * h:hd * (h + 1)] for h in range(nh)])
            y, s_out = _gdn_recur(s_in, *(t[sl] for t in loc), zz, ng_ref[...])
            s_scr[...] = s_out
            for h in range(nh):
                y_ref[rs, hd * h:hd * (h + 1)] = y[h].astype(BF16)

    return pl.pallas_call(
        body, name=name, grid=(nc // cps,),
        in_specs=[pl.BlockSpec((tb, 1536), lambda i: (i, 0)), _pblk(tb, 512, C_Z), _pblk(tb, 128, C_BA),
                  _full((1, 128)), _full((1, 128)), _full((1, 128))],
        out_specs=[pl.BlockSpec((tb, 512), lambda i: (i, 0)),
                   pl.BlockSpec((cps, GDN_HEADS, hd, hd), lambda i: (i, 0, 0, 0))],
        out_shape=[jax.ShapeDtypeStruct((s, 512), BF16), jax.ShapeDtypeStruct((nc, GDN_HEADS, hd, hd), F32)],
        scratch_shapes=[pltpu.VMEM((GDN_HEADS, hd, hd), F32)],
        compiler_params=_params("arbitrary"),
    )(qkv, proj, proj, alog, dt, ng)


def gdn_bwd(qkv, proj, sall, dy, alog, dt, ng, *, cps, name):
    s = qkv.shape[0]
    c = CHUNK
    nc = s // c
    hd = GDN_DK
    tb = c * cps
    nt = nc // cps

    def body(qkv_ref, z_ref, ba_ref, sall_ref, dy_ref, al_ref, dt_ref, ng_ref,
             dqkv_ref, dz_ref, dba_ref, dal_ref, ddt_ref, dng_ref, ds_scr):
        @pl.when(pl.program_id(0) == 0)
        def _():
            ds_scr[...] = jnp.zeros_like(ds_scr)
            dal_ref[...] = jnp.zeros_like(dal_ref)
            ddt_ref[...] = jnp.zeros_like(ddt_ref)
            dng_ref[...] = jnp.zeros_like(dng_ref)

        nh = GDN_HEADS
        ba = ba_ref[...]
        g, gvf = jax.vjp(_gdn_g, ba, al_ref[...], dt_ref[...])
        gcums = [_chunk_cumsum(g[c * ci:c * (ci + 1)]) for ci in range(cps)]
        loc, loc_vf = jax.vjp(_gdn_local, *_gdn_inputs(qkv_ref, ba, gcums, cps))
        d_loc = [None] * cps
        dng = jnp.zeros((1, 128), F32)
        for ci in reversed(range(cps)):
            rs = slice(c * ci, c * (ci + 1))
            sl = slice(nh * ci, nh * (ci + 1))
            zz = jnp.stack([z_ref[rs, hd * h:hd * (h + 1)] for h in range(nh)])
            dyy = jnp.stack([dy_ref[rs, hd * h:hd * (h + 1)] for h in range(nh)])
            _, rvf = jax.vjp(_gdn_recur, sall_ref[ci], *(t[sl] for t in loc), zz, ng_ref[...])
            ds, du, dwk, dattn, dqg, dkd, deg, dzz, dn = rvf((dyy, ds_scr[...]))
            ds_scr[...] = ds
            d_loc[ci] = (du, dwk, dattn, dqg, dkd, deg)
            dng = dng + dn
            for h in range(nh):
                dz_ref[rs, hd * h:hd * (h + 1)] = dzz[h].astype(BF16)
        dq, dk, dv, dbp, dgc = loc_vf(tuple(jnp.concatenate([d_loc[ci][j] for ci in range(cps)], axis=0)
                                            for j in range(6)))
        dgs, dbas = [], []
        for ci in range(cps):
            rs = slice(c * ci, c * (ci + 1))
            dgcum = jnp.zeros((c, 128), F32)
            dba = jnp.zeros((c, 128), F32)
            for h in range(nh):
                b = nh * ci + h
                dqkv_ref[rs, hd * h:hd * (h + 1)] = dq[b]
                dqkv_ref[rs, 512 + hd * h:512 + hd * (h + 1)] = dk[b]
                dqkv_ref[rs, 1024 + hd * h:1024 + hd * (h + 1)] = dv[b]
                dgcum = dgcum + _put_col(dgc[b], 4 + h, 128)
                dba = dba + _put_col(dbp[b], h, 128)
            dgs.append(_chunk_cumsum(dgcum, transpose=True))
            dbas.append(dba)
        dba2, dal, ddt = gvf(jnp.concatenate(dgs, axis=0))
        dba_ref[...] = (jnp.concatenate(dbas, axis=0) + dba2).astype(BF16)
        dal_ref[...] += dal
        ddt_ref[...] += ddt
        dng_ref[...] += dng

    def rev(width):
        return pl.BlockSpec((tb, width), lambda i: (nt - 1 - i, 0))

    return pl.pallas_call(
        body, name=name, grid=(nt,),
        in_specs=[rev(1536), _pblk(tb, 512, C_Z, rev_n=nt), _pblk(tb, 128, C_BA, rev_n=nt),
                  pl.BlockSpec((cps, GDN_HEADS, hd, hd), lambda i: (nt - 1 - i, 0, 0, 0)), rev(512),
                  _full((1, 128)), _full((1, 128)), _full((1, 128))],
        out_specs=[rev(1536), rev(512), rev(128), _full((1, 128)), _full((1, 128)), _full((1, 128))],
        out_shape=[jax.ShapeDtypeStruct((s, 1536), F32), jax.ShapeDtypeStruct((s, 512), BF16),
                   jax.ShapeDtypeStruct((s, 128), BF16), jax.ShapeDtypeStruct((1, 128), F32),
                   jax.ShapeDtypeStruct((1, 128), F32), jax.ShapeDtypeStruct((1, 128), F32)],
        scratch_shapes=[pltpu.VMEM((GDN_HEADS, hd, hd), F32)],
        compiler_params=_params("arbitrary"),
    )(qkv, proj, proj, sall, dy, alog, dt, ng)


def merge_fwd(x, ys, proj, wb, wo, *, tb, dep=None, name):
    s, d = x.shape
    dep_specs, dep_args = _dep_operand(dep)

    def body(x_ref, ya, yb, yc, yd, g0, g1, g2, g3, wb_ref, wo_ref, *rest):
        o_ref, m_ref = rest[len(dep_args):]
        m = jnp.zeros((tb, d), F32)
        for g, (y_ref, gp_ref) in enumerate(zip((ya, yb, yc, yd), (g0, g1, g2, g3))):
            m = m + jax.nn.sigmoid(gp_ref[...]) * _dg(y_ref[...], wb_ref[g], 1, 0)
        mb = m.astype(BF16)
        m_ref[...] = mb
        o_ref[...] = x_ref[...] + _dg(mb, wo_ref[...], 1, 0)

    row = pl.BlockSpec((tb, d), lambda i: (i, 0))
    yrow = pl.BlockSpec((tb, BRANCH_W), lambda i: (i, 0))
    return pl.pallas_call(
        body, name=name, grid=(s // tb,),
        in_specs=[row] + [yrow] * 4 + [_pblk(tb, d, C_GATE + d * g) for g in range(4)]
        + [_full((4, BRANCH_W, d)), _full((d, d))] + dep_specs,
        out_specs=[row, row],
        out_shape=[jax.ShapeDtypeStruct((s, d), F32), jax.ShapeDtypeStruct((s, d), BF16)],
        compiler_params=_params("parallel"),
    )(x, *ys, proj, proj, proj, proj, wb, wo, *dep_args)


def merge_bwd(dx, ys, proj, wb, wo, *, tb, dep=None, name):
    s, d = dx.shape
    dep_specs, dep_args = _dep_operand(dep)

    def body(dx_ref, ya, yb, yc, yd, g0, g1, g2, g3, wb_ref, wo_ref, *rest):
        dgp_ref, dbr_ref, da, db, dc, dd, dxb_ref = rest[len(dep_args):]
        dxb = dx_ref[...].astype(BF16)
        dxb_ref[...] = dxb
        dm = _dg(dxb, wo_ref[...], 1, 1)
        for g, (y_ref, gp_ref, dy_ref) in enumerate(zip((ya, yb, yc, yd), (g0, g1, g2, g3), (da, db, dc, dd))):
            br = _dg(y_ref[...], wb_ref[g], 1, 0)
            sg = jax.nn.sigmoid(gp_ref[...])
            dgp_ref[:, d * g:d * (g + 1)] = (dm * br * sg * (1.0 - sg)).astype(BF16)
            dbr = (dm * sg).astype(BF16)
            dbr_ref[g] = dbr
            dy_ref[...] = _dg(dbr, wb_ref[g], 1, 1)

    row = pl.BlockSpec((tb, d), lambda i: (i, 0))
    yrow = pl.BlockSpec((tb, BRANCH_W), lambda i: (i, 0))
    return pl.pallas_call(
        body, name=name, grid=(s // tb,),
        in_specs=[row] + [yrow] * 4 + [_pblk(tb, d, C_GATE + d * g) for g in range(4)]
        + [_full((4, BRANCH_W, d)), _full((d, d))] + dep_specs,
        out_specs=[pl.BlockSpec((tb, 4 * d), lambda i: (i, 0)), pl.BlockSpec((4, tb, d), lambda i: (0, i, 0)),
                   yrow, yrow, yrow, yrow, row],
        out_shape=[jax.ShapeDtypeStruct((s, 4 * d), BF16), jax.ShapeDtypeStruct((4, s, d), BF16)]
        + [jax.ShapeDtypeStruct((s, BRANCH_W), F32)] * 4 + [jax.ShapeDtypeStruct((s, d), BF16)],
        compiler_params=_params("parallel"),
    )(dx, *ys, proj, proj, proj, proj, wb, wo, *dep_args)


def loss_head(x, g, target, *, tb, name):
    s, d = x.shape

    def body(x_ref, g_ref, t_ref, dx_ref, dg_ref, loss_ref):
        @pl.when(pl.program_id(0) == 0)
        def _():
            dg_ref[...] = jnp.zeros_like(dg_ref)
            loss_ref[...] = jnp.zeros_like(loss_ref)

        xv, gv = x_ref[...], g_ref[...]
        r = lax.rsqrt(jnp.mean(xv * xv, axis=-1, keepdims=True) + EPS)
        err = xv * r * gv - t_ref[...]
        loss_ref[...] += 0.5 * jnp.sum(jnp.mean(err * err, axis=-1, keepdims=True), axis=0, keepdims=True)
        dx, dg = _rms_bwd(xv, gv, err * (1.0 / d))
        dx_ref[...] = dx
        dg_ref[...] += dg

    row = pl.BlockSpec((tb, d), lambda i: (i, 0))
    return pl.pallas_call(
        body, name=name, grid=(s // tb,),
        in_specs=[row, _full((1, d)), row],
        out_specs=[row, _full((1, d)), _full((1, 128))],
        out_shape=[jax.ShapeDtypeStruct((s, d), F32), jax.ShapeDtypeStruct((1, d), F32),
                   jax.ShapeDtypeStruct((1, 128), F32)],
        compiler_params=_params("arbitrary"),
    )(x, g, target)


def adamw(w, g, m, v, *, name):
    shape = w.shape
    c = shape[-1]
    r = math.prod(shape[:-1])
    tr = r
    while tr * c * 4 > (1 << 20) and tr % 16 == 0:
        tr //= 2
    c1 = 1.0 / (1.0 - ADAM_B1 ** ADAM_STEP)
    c2 = 1.0 / (1.0 - ADAM_B2 ** ADAM_STEP)

    def body(w_ref, g_ref, m_ref, v_ref, d_ref, nm_ref, nv_ref):
        gv = g_ref[...]
        nm = ADAM_B1 * m_ref[...] + (1.0 - ADAM_B1) * gv
        nv = ADAM_B2 * v_ref[...] + (1.0 - ADAM_B2) * (gv * gv)
        nm_ref[...] = nm
        nv_ref[...] = nv
        d_ref[...] = -ADAM_LR * ((nm * c1) / (jnp.sqrt(nv * c2) + ADAM_EPS) + ADAM_WD * w_ref[...])

    blk = pl.BlockSpec((tr, c), lambda i: (i, 0))
    outs = pl.pallas_call(
        body, name=name, grid=(r // tr,),
        in_specs=[blk] * 4, out_specs=[blk] * 3,
        out_shape=[jax.ShapeDtypeStruct((r, c), F32)] * 3,
        compiler_params=_params("parallel"),
    )(*(t.reshape(r, c) for t in (w, g, m, v)))
    return tuple(o.reshape(shape) for o in outs)


_ANY = pl.BlockSpec(memory_space=pl.ANY)


def _place():
    x, y, c = lax.axis_index("x"), lax.axis_index("y"), lax.axis_index("c")
    return x, y, c, [(1 - x, y), (x, 1 - y), (1 - x, 1 - y)]


def gather_shards(bufs, *, name):
    n = len(bufs)

    def body(*refs):
        outs = refs[n:2 * n]
        ici_send, ici_recv, d2d_send, d2d_recv = refs[2 * n:]
        x, y, c, chips = _place()
        me = 2 * x + y

        def ici(a, j, slab, to):
            return pltpu.make_async_remote_copy(
                src_ref=outs[a].at[slab, c], dst_ref=outs[a].at[slab, c], send_sem=ici_send.at[a, j],
                recv_sem=ici_recv.at[a, j], device_id=to, device_id_type=MESH)

        def d2d(a, j, slab, half):
            return pltpu.make_async_remote_copy(
                src_ref=outs[a].at[slab, half], dst_ref=outs[a].at[slab, half], send_sem=d2d_send.at[a, j],
                recv_sem=d2d_recv.at[a, j], device_id=(x, y, 1 - c), device_id_type=MESH)

        sends = []
        for a in range(n):
            for j, (px, py) in enumerate(chips):
                cp = ici(a, j, me, (px, py, c))
                cp.start()
                sends.append(cp)
        for a in range(n):
            for j, (px, py) in enumerate(chips):
                ici(a, j, 2 * px + py, (px, py, c)).wait_recv()
                cp = d2d(a, j, 2 * px + py, c)
                cp.start()
                sends.append(cp)
        for a in range(n):
            for j, (px, py) in enumerate(chips):
                d2d(a, j, 2 * px + py, 1 - c).wait_recv()
        for cp in sends:
            cp.wait_send()

    return pl.pallas_call(
        body, name=name,
        in_specs=[_ANY] * n, out_specs=[_ANY] * n,
        out_shape=[jax.ShapeDtypeStruct(t.shape, t.dtype) for t in bufs],
        input_output_aliases={a: a for a in range(n)},
        scratch_shapes=[pltpu.SemaphoreType.DMA((n, 3))] * 4,
    )(*bufs)


_SEM = pl.BlockSpec(memory_space=pltpu.SEMAPHORE)
_VMEM = pl.BlockSpec(memory_space=pltpu.VMEM)
_EFFECT = pltpu.SideEffectType.DATAFLOW_SIDE_EFFECTING
_TOKEN = jax.ShapeDtypeStruct((8, 128), F32)


def _gather_copies(outs, sems_ici, sems_d2d):
    x, y, c, chips = _place()

    def ici(a, j, slab, to):
        return pltpu.make_async_remote_copy(
            src_ref=outs[a].at[slab, c], dst_ref=outs[a].at[slab, c], send_sem=sems_ici[0][3 * a + j],
            recv_sem=sems_ici[1][3 * a + j], device_id=to, device_id_type=MESH)

    def d2d(a, j, slab, half):
        return pltpu.make_async_remote_copy(
            src_ref=outs[a].at[slab, half], dst_ref=outs[a].at[slab, half], send_sem=sems_d2d[0][3 * a + j],
            recv_sem=sems_d2d[1][3 * a + j], device_id=(x, y, 1 - c), device_id_type=MESH)

    return x, y, c, chips, ici, d2d


def _async_call(body, bufs, sems_in, after, sems_out, token, *, name):
    n = len(bufs)
    n_out = 6 * n if sems_out else 0
    extra = [] if after is None else [after]
    return pl.pallas_call(
        body, name=name,
        in_specs=[_ANY] * n + [_SEM] * len(sems_in) + [_ANY] * len(extra),
        out_specs=[_ANY] * n + [_SEM] * n_out + [_VMEM] * token,
        out_shape=[jax.ShapeDtypeStruct(t.shape, t.dtype) for t in bufs]
        + [pltpu.SemaphoreType.DMA(())] * n_out + [_TOKEN] * token,
        input_output_aliases={a: a for a in range(n)},
        compiler_params=pltpu.CompilerParams(has_side_effects=_EFFECT),
    )(*bufs, *sems_in, *extra)


def gather_start(bufs, *, name):
    n = len(bufs)

    def body(*refs):
        outs = refs[n:2 * n]
        sems, token = refs[2 * n:8 * n], refs[8 * n]
        x, y, c, chips, ici, _ = _gather_copies(outs, (sems[:3 * n], sems[3 * n:]), None)
        for a in range(n):
            for j, (px, py) in enumerate(chips):
                ici(a, j, 2 * x + y, (px, py, c)).start()
        token[...] = jnp.zeros_like(token)

    res = _async_call(body, bufs, [], None, True, True, name=name)
    return res[:n], res[n:7 * n], res[7 * n]


def gather_pass(bufs, sems, after, *, name):
    n = len(bufs)

    def body(*refs):
        s_in = refs[n:7 * n]
        outs = refs[7 * n + 1:8 * n + 1]
        s_out, token = refs[8 * n + 1:14 * n + 1], refs[14 * n + 1]
        x, y, c, chips, ici, d2d = _gather_copies(outs, (s_in[:3 * n], s_in[3 * n:]),
                                                  (s_out[:3 * n], s_out[3 * n:]))
        for a in range(n):
            for j, (px, py) in enumerate(chips):
                ici(a, j, 2 * px + py, (px, py, c)).wait_recv()
                d2d(a, j, 2 * px + py, c).start()
        for a in range(n):
            for j, (px, py) in enumerate(chips):
                ici(a, j, 2 * x + y, (px, py, c)).wait_send()
        token[...] = jnp.zeros_like(token)

    res = _async_call(body, bufs, list(sems), after, True, True, name=name)
    return res[:n], res[n:7 * n], res[7 * n]


def gather_wait(bufs, sems, after, *, name):
    n = len(bufs)

    def body(*refs):
        s_in = refs[n:7 * n]
        outs = refs[7 * n + 1:]
        x, y, c, chips, _, d2d = _gather_copies(outs, None, (s_in[:3 * n], s_in[3 * n:]))
        for a in range(n):
            for j, (px, py) in enumerate(chips):
                d2d(a, j, 2 * px + py, 1 - c).wait_recv()
                d2d(a, j, 2 * px + py, c).wait_send()

    return _async_call(body, bufs, list(sems), after, False, False, name=name)


def _pair_plan(srcs, lands):
    x, y, c, _ = _place()
    return [(srcs[a].at[j, 1 - c], lands[a].at[j], (x, y, 1 - c))
            for a in range(len(srcs)) for j in range(N_SHARD)]


def _chip_plan(srcs, lands):
    x, y, c, chips = _place()
    return [(srcs[a].at[2 * px + py], lands[a].at[2 * x + y], (px, py, c))
            for a in range(len(srcs)) for (px, py) in chips]


def copies_start(arrs, land_shapes, plan, per_array, *, name):
    n = len(arrs)
    k = per_array * n

    def body(*refs):
        lands = refs[2 * n:3 * n]
        sems, token = refs[3 * n:3 * n + 2 * k], refs[3 * n + 2 * k]
        for i, (src, dst, dev) in enumerate(plan(refs[n:2 * n], lands)):
            pltpu.make_async_remote_copy(src_ref=src, dst_ref=dst, send_sem=sems[i], recv_sem=sems[k + i],
                                         device_id=dev, device_id_type=MESH).start()
        token[...] = jnp.zeros_like(token)

    res = pl.pallas_call(
        body, name=name,
        in_specs=[_ANY] * n,
        out_specs=[_ANY] * (2 * n) + [_SEM] * (2 * k) + [_VMEM],
        out_shape=[jax.ShapeDtypeStruct(t.shape, t.dtype) for t in arrs] + list(land_shapes)
        + [pltpu.SemaphoreType.DMA(())] * (2 * k) + [_TOKEN],
        input_output_aliases={a: a for a in range(n)},
        compiler_params=pltpu.CompilerParams(has_side_effects=_EFFECT),
    )(*arrs)
    return res[:n], res[n:2 * n], res[2 * n:2 * n + 2 * k], res[2 * n + 2 * k]


def copies_wait(arrs, lands, sems, after, plan, *, name):
    n = len(arrs)
    k = len(sems) // 2

    def body(*refs):
        s_in = refs[2 * n:2 * n + 2 * k]
        outs = refs[2 * n + 2 * k + 1:]
        for i, (src, dst, dev) in enumerate(plan(outs[:n], outs[n:])):
            cp = pltpu.make_async_remote_copy(src_ref=src, dst_ref=dst, send_sem=s_in[i], recv_sem=s_in[k + i],
                                              device_id=dev, device_id_type=MESH)
            cp.wait_send()
            cp.wait_recv()

    res = pl.pallas_call(
        body, name=name,
        in_specs=[_ANY] * (2 * n) + [_SEM] * (2 * k) + [_ANY],
        out_specs=[_ANY] * (2 * n),
        out_shape=[jax.ShapeDtypeStruct(t.shape, t.dtype) for t in list(arrs) + list(lands)],
        input_output_aliases={a: a for a in range(2 * n)},
        compiler_params=pltpu.CompilerParams(has_side_effects=_EFFECT),
    )(*arrs, *lands, *sems, after)
    return res[:n], res[n:]


def pair_send_halves(arrs, *, name):
    n = len(arrs)

    def body(*refs):
        ins, outs = refs[:n], refs[n:2 * n]
        send_sems, recv_sems = refs[2 * n:]
        x, y, c, _ = _place()
        copies = []
        for a in range(n):
            for j in range(N_SHARD):
                cp = pltpu.make_async_remote_copy(
                    src_ref=ins[a].at[j, 1 - c], dst_ref=outs[a].at[j], send_sem=send_sems.at[a, j],
                    recv_sem=recv_sems.at[a, j], device_id=(x, y, 1 - c), device_id_type=MESH)
                cp.start()
                copies.append(cp)
        for cp in copies:
            cp.wait()

    return pl.pallas_call(
        body, name=name,
        in_specs=[_ANY] * n, out_specs=[_ANY] * n,
        out_shape=[jax.ShapeDtypeStruct((N_SHARD,) + t.shape[2:], t.dtype) for t in arrs],
        scratch_shapes=[pltpu.SemaphoreType.DMA((n, N_SHARD)), pltpu.SemaphoreType.DMA((n, N_SHARD))],
    )(*arrs)


def chip_scatter(arrs, *, name):
    n = len(arrs)

    def body(*refs):
        ins, outs = refs[:n], refs[n:2 * n]
        send_sems, recv_sems = refs[2 * n:]
        x, y, c, chips = _place()
        me = 2 * x + y
        copies = []
        for a in range(n):
            for j, (px, py) in enumerate(chips):
                cp = pltpu.make_async_remote_copy(
                    src_ref=ins[a].at[2 * px + py], dst_ref=outs[a].at[me], send_sem=send_sems.at[a, j],
                    recv_sem=recv_sems.at[a, j], device_id=(px, py, c), device_id_type=MESH)
                cp.start()
                copies.append(cp)
        for cp in copies:
            cp.wait()

    return pl.pallas_call(
        body, name=name,
        in_specs=[_ANY] * n, out_specs=[_ANY] * n,
        out_shape=[jax.ShapeDtypeStruct(t.shape, t.dtype) for t in arrs],
        scratch_shapes=[pltpu.SemaphoreType.DMA((n, 3)), pltpu.SemaphoreType.DMA((n, 3))],
    )(*arrs)


def pair_gather(bufs, *, name):
    n = len(bufs)

    def body(*refs):
        outs = refs[n:2 * n]
        send_sems, recv_sems = refs[2 * n:]
        x, y, c, _ = _place()
        copies = []
        for a in range(n):
            cp = pltpu.make_async_remote_copy(
                src_ref=outs[a].at[c], dst_ref=outs[a].at[c], send_sem=send_sems.at[a],
                recv_sem=recv_sems.at[a], device_id=(x, y, 1 - c), device_id_type=MESH)
            cp.start()
            copies.append(cp)
        for cp in copies:
            cp.wait()

    return pl.pallas_call(
        body, name=name,
        in_specs=[_ANY] * n, out_specs=[_ANY] * n,
        out_shape=[jax.ShapeDtypeStruct(t.shape, t.dtype) for t in bufs],
        input_output_aliases={a: a for a in range(n)},
        scratch_shapes=[pltpu.SemaphoreType.DMA((n,)), pltpu.SemaphoreType.DMA((n,))],
    )(*bufs)


def all_sum_small(v, *, name):
    m = v.shape[0]

    def body(v_ref, o_ref, buf, send_sems, recv_sems, local_sem):
        x, y, c, chips = _place()
        me, sibling = (x, y, c), (x, y, 1 - c)

        def rows(px, py, pc):
            return buf.at[pl.ds((4 * px + 2 * py + pc) * m, m), :]

        def copy(k, block, to, src=None):
            return pltpu.make_async_remote_copy(
                src_ref=rows(*block) if src is None else src, dst_ref=rows(*block),
                send_sem=send_sems.at[k], recv_sem=recv_sems.at[k], device_id=to, device_id_type=MESH)

        mine = pltpu.make_async_copy(v_ref, rows(*me), local_sem)
        mine.start()
        first = [copy(0, me, sibling, src=v_ref)]
        first += [copy(1 + j, me, (*chip, c), src=v_ref) for j, chip in enumerate(chips)]
        for cp in first:
            cp.start()
        passed = [copy(4 + j, (*chip, c), sibling) for j, chip in enumerate(chips)]
        for j, chip in enumerate(chips):
            copy(1 + j, (*chip, c), me).wait_recv()
            passed[j].start()
        copy(0, sibling, me).wait_recv()
        for j, chip in enumerate(chips):
            copy(4 + j, (*chip, 1 - c), me).wait_recv()
        for cp in first + passed:
            cp.wait_send()
        mine.wait()
        acc = buf[pl.ds(0, m), :]
        for k in range(1, 8):
            acc = acc + buf[pl.ds(k * m, m), :]
        o_ref[...] = acc

    vm = pl.BlockSpec(memory_space=pltpu.VMEM)
    return pl.pallas_call(
        body, name=name, in_specs=[vm], out_specs=vm,
        out_shape=jax.ShapeDtypeStruct((m, 128), F32),
        scratch_shapes=[pltpu.VMEM((8 * m, 128), F32), pltpu.SemaphoreType.DMA((7,)),
                        pltpu.SemaphoreType.DMA((7,)), pltpu.SemaphoreType.DMA],
    )(v)


def sum_halves(arr, recv, c_idx, *, name):
    _, _, r, c = arr.shape
    tr = r
    while tr * c * 4 > (1 << 20) and tr % 32 == 0:
        tr //= 2

    def body(c_ref, a_ref, b_ref, o_ref):
        o_ref[...] = (a_ref[...] + b_ref[...]).astype(BF16)

    return pl.pallas_call(
        body, name=name,
        grid_spec=pltpu.PrefetchScalarGridSpec(
            num_scalar_prefetch=1, grid=(N_SHARD, r // tr),
            in_specs=[pl.BlockSpec((None, None, tr, c), lambda j, i, cr: (j, cr[0], i, 0)),
                      pl.BlockSpec((None, tr, c), lambda j, i, cr: (j, i, 0))],
            out_specs=pl.BlockSpec((None, tr, c), lambda j, i, cr: (j, i, 0))),
        out_shape=jax.ShapeDtypeStruct((N_SHARD, r, c), BF16),
        compiler_params=_params("parallel", "parallel"),
    )(c_idx, arr, recv)


def sum_chips(recv, part, place, *, name):
    _, r, c = recv.shape
    tr = r
    while tr * c * 4 > (1 << 20) and tr % 32 == 0:
        tr //= 2

    def body(place_ref, a0, a1, a2, a3, p_ref, o_ref):
        me = place_ref[1]
        acc = None
        for k, a_ref in enumerate((a0, a1, a2, a3)):
            term = jnp.where(me == k, p_ref[...], a_ref[...]).astype(F32)
            acc = term if acc is None else acc + term
        o_ref[...] = acc

    def slab(k):
        return pl.BlockSpec((None, tr, c), lambda i, pr: (jnp.where(pr[1] == k, (k + 1) % N_SHARD, k), i, 0))

    return pl.pallas_call(
        body, name=name,
        grid_spec=pltpu.PrefetchScalarGridSpec(
            num_scalar_prefetch=1, grid=(r // tr,),
            in_specs=[slab(k) for k in range(N_SHARD)]
            + [pl.BlockSpec((None, tr, c), lambda i, pr: (pr[1], i, 0))],
            out_specs=pl.BlockSpec((None, tr, c), lambda i, pr: (pr[0], i, 0))),
        out_shape=jax.ShapeDtypeStruct((2, r, c), F32),
        compiler_params=_params("parallel"),
    )(place, recv, recv, recv, recv, part)


_WEIGHTS = ("ff1_norm", "ff1_wg", "ff1_wu", "ff1_wd", "mix_norm", "w_in", "sgu_ln_g", "sgu_ln_b", "sgu_w",
            "sgu_b", "lru_conv_w", "lru_conv_b", "lru_wa", "lru_ba", "lru_wx", "lru_bx", "lru_lambda",
            "gdn_conv_w", "gdn_a_log", "gdn_dt_bias", "gdn_norm_g", "pool_w", "pool_scale", "w_branch",
            "w_out", "ff2_norm", "ff2_wg", "ff2_wu", "ff2_wd", "final_norm")
_BIG = ("ff1_wg", "ff1_wu", "ff1_wd", "w_in", "w_branch", "w_out", "ff2_wg", "ff2_wu", "ff2_wd")
_SMALL = tuple(n for n in _WEIGHTS if n not in _BIG)
_CONV = ("lru_conv_w", "gdn_conv_w")


def _seg_rows(shape):
    return -(-math.prod(shape) // 1024) * 8


def _pack(arrs):
    segs = []
    for t in arrs:
        rows = _seg_rows(t.shape)
        flat = t.reshape(-1)
        segs.append(jnp.pad(flat, (0, rows * 128 - flat.shape[0])).reshape(rows, 128))
    return jnp.concatenate(segs, axis=0)


def _unpack(buf, shapes):
    out, o = [], 0
    for shp in shapes:
        rows = _seg_rows(shp)
        out.append(buf[o:o + rows].reshape(-1)[:math.prod(shp)].reshape(shp))
        o += rows
    return out


def _block_diag(w):
    h, n, _ = w.shape
    same = jnp.arange(h)[:, None, None, None] == jnp.arange(h)[None, None, :, None]
    return jnp.where(same, w[:, :, None, :], 0.0).reshape(h * n, h * n)


def _diag_blocks(m, h=8, n=64):
    return jnp.stack([m[i * n:(i + 1) * n, i * n:(i + 1) * n] for i in range(h)])


def _lane_row(v):
    return jnp.zeros((1, 128), F32).at[0, 4:8].set(v)


def _w_in_to_padded(raw):
    sem = jnp.concatenate([raw[k, :, :P_IN_SHARD] for k in range(N_SHARD)], axis=1)
    zero = jnp.zeros((sem.shape[0], C_GATE - C_BA - 8), sem.dtype)
    return jnp.concatenate([sem[:, 2048:4096], sem[:, 0:2048], sem[:, 4104:4616], sem[:, 4096:4104], zero,
                            sem[:, 4616:P_IN]], axis=1)


def _w_in_grad_to_shards(part):
    r = part.shape[1]
    dwp = part.transpose(1, 0, 2).reshape(r, P_PAD)
    sem = jnp.concatenate([dwp[:, C_AU:C_DX], dwp[:, 0:C_AU], dwp[:, C_BA:C_BA + 8], dwp[:, C_DX:C_BA],
                           dwp[:, C_GATE:]], axis=1)
    sh = sem.reshape(r, N_SHARD, P_IN_SHARD).transpose(1, 0, 2)
    return jnp.pad(sh, ((0, 0), (0, 0), (0, P_IN_SHARD_PAD - P_IN_SHARD)))


def kernel(x, ff1_norm, ff1_wg, ff1_wu, ff1_wd, mix_norm, w_in, sgu_ln_g, sgu_ln_b, sgu_w, sgu_b, lru_conv_w,
           lru_conv_b, lru_wa, lru_ba, lru_wx, lru_bx, lru_lambda, gdn_conv_w, gdn_a_log, gdn_dt_bias, gdn_norm_g,
           pool_w, pool_scale, w_branch, w_out, ff2_norm, ff2_wg, ff2_wu, ff2_wd, final_norm, loss_target, m_ff1_norm,
           m_ff1_wg, m_ff1_wu, m_ff1_wd, m_mix_norm, m_w_in, m_sgu_ln_g, m_sgu_ln_b, m_sgu_w, m_sgu_b, m_lru_conv_w,
           m_lru_conv_b, m_lru_wa, m_lru_ba, m_lru_wx, m_lru_bx, m_lru_lambda, m_gdn_conv_w, m_gdn_a_log, m_gdn_dt_bias,
           m_gdn_norm_g, m_pool_w, m_pool_scale, m_w_branch, m_w_out, m_ff2_norm, m_ff2_wg, m_ff2_wu, m_ff2_wd,
           m_final_norm, v_ff1_norm, v_ff1_wg, v_ff1_wu, v_ff1_wd, v_mix_norm, v_w_in, v_sgu_ln_g, v_sgu_ln_b, v_sgu_w,
           v_sgu_b, v_lru_conv_w, v_lru_conv_b, v_lru_wa, v_lru_ba, v_lru_wx, v_lru_bx, v_lru_lambda, v_gdn_conv_w,
           v_gdn_a_log, v_gdn_dt_bias, v_gdn_norm_g, v_pool_w, v_pool_scale, v_w_branch, v_w_out, v_ff2_norm, v_ff2_wg,
           v_ff2_wu, v_ff2_wd, v_final_norm):
    a = dict(locals())
    n_layer = ff1_norm.shape[0]
    d = D_MODEL
    x0 = x[0]
    chip = 2 * lax.axis_index("x") + lax.axis_index("y")
    c_idx = jnp.reshape(lax.axis_index("c"), (1,)).astype(jnp.int32)
    place = jnp.stack([lax.axis_index("c"), chip]).astype(jnp.int32)

    def own_slab(t, dtype=BF16):
        buf = lax.empty((N_SHARD,) + t.shape, dtype)
        return lax.dynamic_update_slice(buf, t.astype(dtype)[None], (chip,) + (0,) * t.ndim)

    def halves(t):
        return t.reshape((2, t.shape[0] // 2) + t.shape[1:])

    assert n_layer == 2
    w_in_loc = jnp.pad(w_in, ((0, 0), (0, 0), (0, P_IN_SHARD_PAD - P_IN_SHARD)))

    def layer_bufs(l):
        wgu = jnp.stack([ff1_wg[l], ff1_wu[l], ff2_wg[l], ff2_wu[l]]).reshape(2, 2, d, FF_SHARD)
        wd = jnp.stack([ff1_wd[l], ff2_wd[l]])
        return [own_slab(wgu), own_slab(wd), own_slab(halves(w_in_loc[l])), own_slab(halves(w_branch[l])),
                own_slab(halves(w_out[l]))]

    bufs = [layer_bufs(l) for l in range(n_layer)]
    *gathered0, g_lcw, g_gcw = gather_shards(bufs[0] + [own_slab(lru_conv_w, F32), own_slab(gdn_conv_w, F32)],
                                             name="gather_weights_0")
    first, bufs1 = lax.optimization_barrier((gathered0[0], bufs[1]))
    gathered = [[first] + gathered0[1:], None]
    in_flight, gather_sems, gather_token = gather_start(bufs1, name="gather_weights_1_start")

    def layer_params(l):
        g_wgu, g_wd, g_w_in, g_wb, g_wo = gathered[l]
        g_wgu = g_wgu.reshape(N_SHARD, 4, d, FF_SHARD)
        p = {}
        p["ff1"] = (g_wgu, g_wgu, g_wd, (0, 1, 0))
        p["ff2"] = (g_wgu, g_wgu, g_wd, (2, 3, 1))
        p["w_in"] = _w_in_to_padded(g_w_in.reshape(4, d, P_IN_SHARD_PAD))
        p["wb"] = g_wb.reshape(4, 4, BRANCH_W, d // 4).transpose(1, 2, 0, 3).reshape(4, BRANCH_W, d)
        p["wo"] = g_wo.reshape(d, d)
        p["lcw"] = g_lcw[:, l].transpose(1, 0, 2).reshape(4, BRANCH_W)
        p["gcw"] = g_gcw[:, l].transpose(1, 0, 2).reshape(4, 3 * BRANCH_W)
        p["lcb"] = lru_conv_b[l][None]
        p["dep1"] = gather_token if l == 0 else None
        p["bt"] = jnp.zeros((128, 128), F32).at[:, :4].set(sgu_b[l].T)
        p["wa"] = _block_diag(lru_wa[l])
        p["wx"] = _block_diag(lru_wx[l])
        p["alog"] = _lane_row(gdn_a_log[l])
        p["dt"] = _lane_row(gdn_dt_bias[l])
        return p

    saved = []
    xs = x0
    for l in range(n_layer):
        p = layer_params(l)
        sv = {"p": p, "x0": xs}
        x1 = ffn_fwd(xs, ff1_norm[l][None], *p["ff1"], tb=512, dep=p["dep1"], name=f"ffn1_fwd_{l}")
        h = norm_fwd(x1, mix_norm[l][None], tb=512, name=f"mix_norm_fwd_{l}")
        proj = matmul(h, p["w_in"], tm=1024, tn=1536, tk=1024, name=f"proj_{l}")
        ya = sgu_fwd(proj, sgu_ln_g[l][None], sgu_ln_b[l][None], sgu_w[l], p["bt"], tb=512, name=f"sgu_fwd_{l}")
        xc = conv_fwd(proj, C_BX, BRANCH_W, p["lcw"], p["lcb"], tb=512, name=f"lru_conv_fwd_{l}")
        yb, hl = lru_fwd(xc, proj, p["wa"], p["wx"], lru_ba[l][None], lru_bx[l][None], lru_lambda[l][None],
                         tb=256, name=f"lru_fwd_{l}")
        qkv = conv_fwd(proj, C_QKV, 3 * BRANCH_W, p["gcw"], jnp.zeros((1, 3 * BRANCH_W), F32), tb=512,
                       name=f"gdn_conv_fwd_{l}")
        yc, sall = gdn_fwd(qkv, proj, p["alog"], p["dt"], gdn_norm_g[l][None], cps=GDN_CPS, name=f"gdn_fwd_{l}")
        dep2 = None
        if l == 0:
            in_flight, gather_sems, dep2 = gather_pass(in_flight, gather_sems, yc, name="gather_weights_1_pass")
        yd = pool_fwd(proj, pool_w[l], pool_scale[l][None], tb=512, name=f"pool_fwd_{l}")
        ys = (ya, yb, yc, yd)
        x2, merged = merge_fwd(x1, ys, proj, p["wb"], p["wo"], tb=256, dep=dep2, name=f"merge_fwd_{l}")
        x3 = ffn_fwd(x2, ff2_norm[l][None], *p["ff2"], tb=512, name=f"ffn2_fwd_{l}")
        if l == 0:
            gathered[1] = gather_wait(in_flight, gather_sems, x3, name="gather_weights_1_wait")
        sv.update(x1=x1, h=h, proj=proj, xc=xc, hl=hl, qkv=qkv, sall=sall, ys=ys, merged=merged, x2=x2)
        saved.append(sv)
        xs = x3

    dx, d_final, loss_blk = loss_head(xs, final_norm[None], loss_target[0], tb=512, name="loss_head")
    loss = lax.psum(loss_blk[0, 0], ("x", "y", "c"))

    small_g = {n: [None] * n_layer for n in _SMALL if n != "final_norm"}
    big_g = [None] * n_layer
    per_layer = 7
    flight = None

    def tie(t, token):
        return lax.optimization_barrier((t, token))[0]

    def pair_sums(bigs, recv, tag, w_in_at):
        part = [sum_halves(t, r, c_idx, name=f"grad_pair_sum_{tag}_{i}") for i, (t, r) in enumerate(zip(bigs, recv))]
        if w_in_at is not None:
            part[w_in_at] = _w_in_grad_to_shards(part[w_in_at])
        return part

    def pair_start(arrs, tag):
        return copies_start(arrs, [jax.ShapeDtypeStruct((N_SHARD,) + t.shape[2:], t.dtype) for t in arrs],
                            _pair_plan, N_SHARD, name=f"grad_pair_exchange_{tag}_start")

    def pair_to_chip(fl, after, tag, w_in_at=None):
        sent, recv = copies_wait(*fl[:3], after, _pair_plan, name=f"grad_pair_exchange_{tag}_wait")
        part = pair_sums(sent, recv, tag, w_in_at)
        return copies_start(part, [jax.ShapeDtypeStruct(t.shape, t.dtype) for t in part], _chip_plan, 3,
                            name=f"grad_chip_exchange_{tag}_start")

    def chip_finish(fl, after, tag):
        part, recv = copies_wait(*fl[:3], after, _chip_plan, name=f"grad_chip_exchange_{tag}_wait")
        return [sum_chips(r, q, place, name=f"grad_chip_sum_{tag}_{i}") for i, (r, q) in enumerate(zip(recv, part))]

    early = mid = None

    for l in reversed(range(n_layer)):
        sv = saved[l]
        p = sv["p"]
        proj = sv["proj"]
        dx2, dg_ff2, dab2, hid2, h2, dy2 = ffn_bwd(sv["x2"], dx, ff2_norm[l][None], *p["ff2"], tb=512,
                                                  dep=None if flight is None else flight[3],
                                                  name=f"ffn2_bwd_{l}")
        dwgu2 = matmul(h2, dab2, ta=True, tm=1024, tn=FF_SHARD, tk=TK_DW, name=f"ffn2_dwgu_{l}")
        dwd2 = matmul(hid2, dy2, ta=True, tm=FF_SHARD, tn=1024, tk=TK_DW, name=f"ffn2_dwd_{l}")
        if flight is not None:
            flight = pair_to_chip(flight, dx2, "1", w_in_at=2)
        dgp, dbr, dya, dyb, dyc, dyd, dx2b = merge_bwd(dx2, sv["ys"], proj, p["wb"], p["wo"], tb=256,
                                                       dep=None if flight is None else flight[3],
                                                       name=f"merge_bwd_{l}")
        dwo = matmul(sv["merged"], dx2b, ta=True, tm=1024, tn=1024, tk=TK_DW, name=f"dwo_{l}")
        dwb = jnp.stack([matmul(sv["ys"][g], dbr[g], ta=True, tm=BRANCH_W, tn=1024, tk=TK_DW, name=f"dwb{g}_{l}")
                         for g in range(4)])
        dwb_sh = dwb.reshape(4, BRANCH_W, N_SHARD, d // N_SHARD).transpose(2, 0, 1, 3)
        g_late = [dwgu2, dwd2.reshape(N_SHARD, 2, FF_SHARD // 2, d),
                  dwb_sh.reshape(N_SHARD, 2, 2 * BRANCH_W, d // N_SHARD), dwo.reshape(N_SHARD, 2, d // 8, d)]
        if l == 0:
            early = pair_start(g_late, "0a")
            dya = tie(dya, early[3])
        du, dv, dlg, dlb, dws, dbt = sgu_bwd(proj, dya, sgu_ln_g[l][None], sgu_ln_b[l][None], sgu_w[l], p["bt"],
                                             tb=512, name=f"sgu_bwd_{l}")
        dxc, dgate, dwa, dwx, dba, dbx, dlam = lru_bwd(sv["xc"], proj, sv["hl"], dyb, p["wa"], p["wx"],
                                                       lru_ba[l][None], lru_bx[l][None], lru_lambda[l][None],
                                                       tb=256, name=f"lru_bwd_{l}")
        dbx_in, dcw_l = conv_bwd(proj, C_BX, BRANCH_W, dxc, p["lcw"], tb=512, name=f"lru_conv_bwd_{l}")
        dqkv, dz, dbeta, dal, ddt, dng = gdn_bwd(sv["qkv"], proj, sv["sall"], dyc, p["alog"], p["dt"],
                                                 gdn_norm_g[l][None], cps=GDN_CPS, name=f"gdn_bwd_{l}")
        dqkv_in, dcw_g = conv_bwd(proj, C_QKV, 3 * BRANCH_W, dqkv, p["gcw"], tb=512, name=f"gdn_conv_bwd_{l}")
        ddx, dwp, dsc = pool_bwd(proj, dyd, pool_w[l], pool_scale[l][None], tb=512, name=f"pool_bwd_{l}")
        if l == 0:
            early = pair_to_chip(early, ddx, "0a")
            dgp = tie(dgp, early[3])
        s = dx.shape[0]
        dproj = jnp.concatenate([dqkv_in, dz, du, dv, dbx_in, dgate, ddx, dbeta,
                                 jnp.zeros((s, C_GATE - C_BA - 128), BF16), dgp], axis=1)
        dh = matmul(dproj, p["w_in"], tb=True, tm=1024, tn=1024, tk=1536, name=f"dh_mix_{l}")
        dw_in = matmul(sv["h"], dproj, ta=True, tm=1024, tn=1152, tk=TK_DW, col_groups=N_SHARD, name=f"dw_in_{l}")
        dw_in = dw_in.reshape(N_SHARD, 2, d // 2, P_PAD // N_SHARD)
        if l == 0:
            mid = pair_start([dw_in], "0b")
            dh = tie(dh, mid[3])
        dx1, dg_mix = norm_bwd(sv["x1"], mix_norm[l][None], dh, dx2, tb=512, name=f"mix_norm_bwd_{l}")
        dx, dg_ff1, dab1, hid1, h1, dy1 = ffn_bwd(sv["x0"], dx1, ff1_norm[l][None], *p["ff1"],
                                                 tb=512, name=f"ffn1_bwd_{l}")
        if l == 0:
            mid = pair_to_chip(mid, dx, "0b", w_in_at=0)
            h1 = tie(h1, mid[3])
        dwgu1 = matmul(h1, dab1, ta=True, tm=1024, tn=FF_SHARD, tk=TK_DW, name=f"ffn1_dwgu_{l}")
        dwd1 = matmul(hid1, dy1, ta=True, tm=FF_SHARD, tn=1024, tk=TK_DW, name=f"ffn1_dwd_{l}")

        for n, g in (("ff1_norm", dg_ff1[0]), ("mix_norm", dg_mix[0]), ("sgu_ln_g", dlg[0]), ("sgu_ln_b", dlb[0]),
                     ("sgu_w", dws), ("sgu_b", dbt[:, :4].T), ("lru_conv_w", dcw_l[:4]), ("lru_conv_b", dcw_l[4]),
                     ("lru_wa", _diag_blocks(dwa)), ("lru_ba", dba[0]), ("lru_wx", _diag_blocks(dwx)),
                     ("lru_bx", dbx[0]), ("lru_lambda", dlam[0]), ("gdn_conv_w", dcw_g[:4]),
                     ("gdn_a_log", dal[0, 4:8]), ("gdn_dt_bias", ddt[0, 4:8]), ("gdn_norm_g", dng[0]),
                     ("pool_w", dwp), ("pool_scale", dsc[0]), ("ff2_norm", dg_ff2[0])):
            small_g[n][l] = g
        big_g[l] = [dwgu1, dwd1.reshape(N_SHARD, 2, FF_SHARD // 2, d), dw_in, g_late[2], g_late[3], g_late[0], g_late[1]]
        if l == n_layer - 1:
            flight = pair_start(big_g[l], "1")

    last = big_g[0][:2]
    recv = pair_send_halves(last, name="grad_pair_exchange_0c")
    part = pair_sums(last, recv, "0c", None)
    recv = chip_scatter(part, name="grad_chip_exchange_0c")
    red_c = [sum_chips(r, q, place, name=f"grad_chip_sum_0c_{i}") for i, (r, q) in enumerate(zip(recv, part))]
    red1 = chip_finish(flight, red_c[0], "1")
    red_a = chip_finish(early, red_c[0], "0a")
    red_b = chip_finish(mid, red_c[0], "0b")
    full = pair_gather(red_c + red_b + [red_a[2], red_a[3], red_a[0], red_a[1]] + red1, name="grad_pair_gather")
    grads = {}

    def layers(k, f):
        return jnp.stack([f(full[l * per_layer + k]) for l in range(n_layer)])

    grads["ff1_wg"] = layers(0, lambda t: t[0])
    grads["ff1_wu"] = layers(0, lambda t: t[1])
    grads["ff1_wd"] = layers(1, lambda t: t.reshape(FF_SHARD, d))
    grads["w_in"] = layers(2, lambda t: t.reshape(d, P_IN_SHARD_PAD)[:, :P_IN_SHARD])
    grads["w_branch"] = layers(3, lambda t: t.reshape(4, BRANCH_W, d // N_SHARD))
    grads["w_out"] = layers(4, lambda t: t.reshape(d // N_SHARD, d))
    grads["ff2_wg"] = layers(5, lambda t: t[0])
    grads["ff2_wu"] = layers(5, lambda t: t[1])
    grads["ff2_wd"] = layers(6, lambda t: t.reshape(FF_SHARD, d))

    small_full = [jnp.stack(small_g[n]) for n in _SMALL if n != "final_norm"] + [d_final[0]]
    summed = _unpack(all_sum_small(_pack(small_full), name="grad_small_allreduce"), [t.shape for t in small_full])
    for n, g in zip(_SMALL, summed):
        if n in _CONV:
            width = a[n].shape[-1]
            g = lax.dynamic_slice_in_dim(g, chip * width, width, axis=2)
        grads[n] = g

    delta, new_m, new_v = {}, {}, {}
    for n in _BIG:
        delta[n], new_m[n], new_v[n] = adamw(a[n], grads[n], a["m_" + n], a["v_" + n], name=f"adamw_{n}")
    shapes = [a[n].shape for n in _SMALL]
    packed = [_pack([src[pre + n] for n in _SMALL]) for src, pre in ((a, ""), (grads, ""), (a, "m_"), (a, "v_"))]
    for store, buf in zip((delta, new_m, new_v), adamw(*packed, name="adamw_small")):
        store.update(zip(_SMALL, _unpack(buf, shapes)))

    return (loss, dx[None], *[grads[n] for n in _WEIGHTS], *[delta[n] for n in _WEIGHTS],
            *[new_m[n] for n in _WEIGHTS], *[new_v[n] for n in _WEIGHTS])
```

```python
import functools
import math

import jax
import jax.numpy as jnp
from jax import lax
from jax.experimental import pallas as pl
from jax.experimental.pallas import tpu as pltpu

F32 = jnp.float32
BF16 = jnp.bfloat16
MESH = pl.DeviceIdType.MESH

D_MODEL = 1024
N_SHARD = 4
D_FF = 2816
FF_SHARD = D_FF // N_SHARD
BRANCH_W = 512
CHUNK = 64
GDN_HEADS = 4
GDN_DK = 128
GDN_CPS = 4
TK_DW = 2048
LRU_C = 8.0
EPS = 1e-6
P_IN = 8712
P_IN_SHARD = P_IN // N_SHARD
P_IN_SHARD_PAD = 2304
P_PAD = 9216
C_QKV, C_Z, C_AU, C_AV, C_BX, C_BG, C_DX, C_BA, C_GATE = 0, 1536, 2048, 2560, 3072, 3584, 4096, 4608, 5120

ADAM_LR, ADAM_B1, ADAM_B2, ADAM_EPS, ADAM_WD, ADAM_STEP = 0.001, 0.9, 0.999, 1e-08, 0.01, 10


def _gelu(x):
    return 0.5 * x * (1.0 + jnp.tanh(0.7978845608028654 * (x + 0.044715 * (x * x * x))))


def _gelu_grad(x):
    u = 0.7978845608028654 * (x + 0.044715 * (x * x * x))
    t = jnp.tanh(u)
    return 0.5 * (1.0 + t) + 0.5 * x * (1.0 - t * t) * 0.7978845608028654 * (1.0 + 3.0 * 0.044715 * x * x)


def _silu(x):
    return x * jax.nn.sigmoid(x)


def _softplus(x):
    return jnp.maximum(x, 0.0) + jnp.log1p(jnp.exp(-jnp.abs(x)))


def _dg(a, b, ca, cb):
    return lax.dot_general(a.astype(BF16), b.astype(BF16), (((ca,), (cb,)), ((), ())),
                           preferred_element_type=F32)


def _dg3(a, b, ca, cb):
    a1 = a.astype(BF16)
    a2 = (a - a1.astype(F32)).astype(BF16)
    b1 = b.astype(BF16)
    b2 = (b - b1.astype(F32)).astype(BF16)
    dn = (((ca,), (cb,)), ((), ()))
    d = functools.partial(lax.dot_general, dimension_numbers=dn, preferred_element_type=F32)
    return d(a1, b1) + (d(a1, b2) + d(a2, b1))


def _make_mm(ca, cb, dot):
    @jax.custom_vjp
    def f(a, b):
        return dot(a, b, ca, cb)

    def fwd(a, b):
        return dot(a, b, ca, cb), (a, b)

    def bwd(res, g):
        a, b = res
        if (ca, cb) == (1, 0):
            return dot(g, b, 1, 1), dot(a, g, 0, 0)
        if (ca, cb) == (1, 1):
            return dot(g, b, 1, 0), dot(g, a, 0, 0)
        return dot(b, g, 1, 1), dot(a, g, 1, 0)

    f.defvjp(fwd, bwd)
    return f


mm_nn = _make_mm(1, 0, _dg)


def _bdg(a, b, ca, cb):
    return lax.dot_general(a.astype(BF16), b.astype(BF16), (((ca,), (cb,)), ((0,), (0,))),
                           preferred_element_type=F32)


def _split2(x):
    x1 = x.astype(BF16)
    return x1, (x - x1.astype(F32)).astype(BF16)


def _bdg3(a, b, ca, cb):
    a1, a2 = _split2(a)
    b1, b2 = _split2(b)
    d = functools.partial(lax.dot_general, dimension_numbers=(((ca,), (cb,)), ((0,), (0,))),
                          preferred_element_type=F32)
    return d(a1, b1) + (d(a1, b2) + d(a2, b1))


def _make_bmm(ca, cb):
    @jax.custom_vjp
    def f(a, b):
        return _bdg(a, b, ca, cb)

    def fwd(a, b):
        return _bdg(a, b, ca, cb), (a, b)

    def bwd(res, g):
        a, b = res
        if (ca, cb) == (2, 1):
            return _bdg(g, b, 2, 2), _bdg(a, g, 1, 1)
        if (ca, cb) == (2, 2):
            return _bdg(g, b, 2, 1), _bdg(g, a, 1, 1)
        return _bdg(b, g, 2, 2), _bdg(a, g, 2, 1)

    f.defvjp(fwd, bwd)
    return f


bmm_nn = _make_bmm(2, 1)
bmm_nt = _make_bmm(2, 2)
bmm_tn = _make_bmm(1, 1)


def _iota3(shape, dim):
    return lax.broadcasted_iota(jnp.int32, shape, dim)


@jax.custom_vjp
def _tri_inv(aw):
    b, n, _ = aw.shape
    shp = (b, n, 2 * n)
    left = _iota3(shp, 2) < n
    x = jnp.where(_iota3(shp, 1) == _iota3(shp, 2), 1.0, 0.0) - aw
    zero = jnp.zeros((b, n, 4 * n), BF16)
    k = 1
    while k < n:
        x1, x2 = _split2(x)
        lhs = jnp.concatenate([x1, x2], axis=1)
        rhs = jnp.concatenate([zero, jnp.concatenate([x1, x2], axis=2)], axis=1)
        r = lax.dot_general(lhs, rhs, (((2,), (1,)), ((0,), (0,))), preferred_element_type=F32)
        mx = (r[:, :n, :2 * n] + r[:, :n, 2 * n:]) + (r[:, n:, :2 * n] + r[:, n:, 2 * n:])
        x = jnp.where(left, x, 0.0) + mx
        k *= 2
    return jnp.where(left, x, 0.0)


def _tri_inv_fwd(aw):
    t = _tri_inv(aw)
    return t, t


def _tri_inv_bwd(t, dt):
    n = t.shape[1]
    x = _bdg3(t, dt, 1, 1)[:, :n]
    tp = jnp.concatenate([jnp.zeros_like(t), t], axis=1)
    return (-_bdg3(x, tp, 2, 2),)


_tri_inv.defvjp(_tri_inv_fwd, _tri_inv_bwd)


def _rows(shape):
    return lax.broadcasted_iota(jnp.int32, shape, 0)


def _lanes(shape):
    return lax.broadcasted_iota(jnp.int32, shape, 1)


def _col(x, j):
    return jnp.sum(jnp.where(_lanes(x.shape) == j, x, 0.0), axis=1, keepdims=True)


def _put_col(col, j, width):
    shape = (col.shape[0], width)
    return jnp.where(_lanes(shape) == j, jnp.broadcast_to(col, shape), 0.0)


VMEM_BYTES_V7X = 64 << 20


def _params(*sem, vmem_bytes=None):
    return pltpu.CompilerParams(dimension_semantics=sem, vmem_limit_bytes=vmem_bytes)


def matmul(a, b, *, ta=False, tb=False, tm, tn, tk, out_dtype=F32, col_groups=1, dep=None, name):
    ba, bb = a.shape[:-2], b.shape[:-2]
    batch = ba if len(ba) >= len(bb) else bb
    assert ba in ((), batch) and bb in ((), batch)
    nb = len(batch)
    (m, k) = (a.shape[-1], a.shape[-2]) if ta else (a.shape[-2], a.shape[-1])
    (k2, n) = (b.shape[-1], b.shape[-2]) if tb else (b.shape[-2], b.shape[-1])
    tm, tn, tk = min(tm, m), min(tn, n), min(tk, k)
    assert k == k2 and m % tm == 0 and n % tn == 0 and k % tk == 0, (a.shape, b.shape, tm, tn, tk)
    nk = k // tk
    grid = batch + (m // tm, n // tn, nk)

    def a_map(*g):
        i, kk = g[nb], g[nb + 2]
        return (g[:nb] if ba else ()) + ((kk, i) if ta else (i, kk))

    def b_map(*g):
        j, kk = g[nb + 1], g[nb + 2]
        return (g[:nb] if bb else ()) + ((j, kk) if tb else (kk, j))

    per_group = n // tn // col_groups
    assert col_groups == 1 or (nb == 0 and per_group * col_groups * tn == n)

    def o_map(*g):
        if col_groups > 1:
            return (g[1] // per_group, g[0], g[1] % per_group)
        return g[:nb] + (g[nb], g[nb + 1])

    a_blk = (None,) * len(ba) + ((tk, tm) if ta else (tm, tk))
    b_blk = (None,) * len(bb) + ((tn, tk) if tb else (tk, tn))
    o_blk = (None,) * (nb + (col_groups > 1)) + (tm, tn)
    o_shape = (col_groups, m, n // col_groups) if col_groups > 1 else batch + (m, n)

    dep_specs, dep_args = _dep_operand(dep)

    def body(a_ref, b_ref, *rest):
        o_ref, acc_ref = rest[len(dep_args):]
        kk = pl.program_id(nb + 2)

        @pl.when(kk == 0)
        def _():
            acc_ref[...] = jnp.zeros_like(acc_ref)

        acc_ref[...] += _dg(a_ref[...], b_ref[...], 0 if ta else 1, 1 if tb else 0)

        @pl.when(kk == nk - 1)
        def _():
            o_ref[...] = acc_ref[...].astype(o_ref.dtype)

    return pl.pallas_call(
        body, name=name, grid=grid,
        in_specs=[pl.BlockSpec(a_blk, a_map), pl.BlockSpec(b_blk, b_map)] + dep_specs,
        out_specs=pl.BlockSpec(o_blk, o_map),
        out_shape=jax.ShapeDtypeStruct(o_shape, out_dtype),
        scratch_shapes=[pltpu.VMEM((tm, tn), F32)],
        compiler_params=_params(*(("parallel",) * (nb + 2) + ("arbitrary",))),
    )(a, b, *dep_args)


def norm_fwd(x, g, *, tb, name):
    s, d = x.shape

    def body(x_ref, g_ref, h_ref):
        xv = x_ref[...]
        r = lax.rsqrt(jnp.mean(xv * xv, axis=-1, keepdims=True) + EPS)
        h_ref[...] = (xv * r * g_ref[...]).astype(BF16)

    return pl.pallas_call(
        body, name=name, grid=(s // tb,),
        in_specs=[pl.BlockSpec((tb, d), lambda i: (i, 0)), pl.BlockSpec((1, d), lambda i: (0, 0))],
        out_specs=pl.BlockSpec((tb, d), lambda i: (i, 0)),
        out_shape=jax.ShapeDtypeStruct((s, d), BF16),
        compiler_params=_params("parallel"),
    )(x, g)


def _rms_bwd(xv, gv, dh):
    r = lax.rsqrt(jnp.mean(xv * xv, axis=-1, keepdims=True) + EPS)
    xh = xv * r
    dxh = dh * gv
    dx = r * (dxh - xh * jnp.mean(dxh * xh, axis=-1, keepdims=True))
    return dx, jnp.sum(dh * xh, axis=0, keepdims=True)


def norm_bwd(x, g, dh, dres, *, tb, dep=None, name):
    s, d = x.shape
    dep_specs, dep_args = _dep_operand(dep)

    def body(x_ref, g_ref, dh_ref, dres_ref, *rest):
        dx_ref, dg_ref = rest[len(dep_args):]
        dx, dg = _rms_bwd(x_ref[...], g_ref[...], dh_ref[...])
        dx_ref[...] = dres_ref[...] + dx

        @pl.when(pl.program_id(0) == 0)
        def _():
            dg_ref[...] = jnp.zeros_like(dg_ref)

        dg_ref[...] += dg

    row = pl.BlockSpec((tb, d), lambda i: (i, 0))
    one = pl.BlockSpec((1, d), lambda i: (0, 0))
    return pl.pallas_call(
        body, name=name, grid=(s // tb,),
        in_specs=[row, one, row, row] + dep_specs, out_specs=[row, one],
        out_shape=[jax.ShapeDtypeStruct((s, d), F32), jax.ShapeDtypeStruct((1, d), F32)],
        compiler_params=_params("arbitrary"),
    )(x, g, dh, dres, *dep_args)


def _dep_operand(dep):
    return ([], []) if dep is None else ([_ANY], [dep])


def ffn_fwd(x, g, wg, wu, wd, layer, *, tb, dep=None, name):
    s, d = x.shape
    fc = wd.shape[-2]
    dep_specs, dep_args = _dep_operand(dep)

    def body(x_ref, g_ref, wg_ref, wu_ref, wd_ref, *rest):
        o_ref, h_scr, acc_scr = rest[len(dep_args):]
        j = pl.program_id(1)

        @pl.when(j == 0)
        def _():
            xv = x_ref[...]
            r = lax.rsqrt(jnp.mean(xv * xv, axis=-1, keepdims=True) + EPS)
            h_scr[...] = (xv * r * g_ref[...]).astype(BF16)
            acc_scr[...] = jnp.zeros_like(acc_scr)

        h = h_scr[...]
        a = _dg(h, wg_ref[...], 1, 0)
        b = _dg(h, wu_ref[...], 1, 0)
        acc_scr[...] += _dg(_silu(a) * b, wd_ref[...], 1, 0)

        @pl.when(j == N_SHARD - 1)
        def _():
            o_ref[...] = x_ref[...] + 0.5 * acc_scr[...]

    row = pl.BlockSpec((tb, d), lambda i, j: (i, 0))
    lg, lu, ld = layer
    return pl.pallas_call(
        body, name=name, grid=(s // tb, N_SHARD),
        in_specs=[row, pl.BlockSpec((1, d), lambda i, j: (0, 0)),
                  pl.BlockSpec((None, None, d, fc), lambda i, j: (j, lg, 0, 0)),
                  pl.BlockSpec((None, None, d, fc), lambda i, j: (j, lu, 0, 0)),
                  pl.BlockSpec((None, None, fc, d), lambda i, j: (j, ld, 0, 0))] + dep_specs,
        out_specs=row,
        out_shape=jax.ShapeDtypeStruct((s, d), F32),
        scratch_shapes=[pltpu.VMEM((tb, d), BF16), pltpu.VMEM((tb, d), F32)],
        compiler_params=_params("parallel", "arbitrary"),
    )(x, g, wg, wu, wd, *dep_args)


def ffn_bwd(x, dxo, g, wg, wu, wd, layer, *, tb, dep=None, name):
    s, d = x.shape
    fc = wd.shape[-2]
    dep_specs, dep_args = _dep_operand(dep)

    def body(x_ref, dxo_ref, g_ref, wg_ref, wu_ref, wd_ref, *rest):
        dx_ref, dg_ref, dab_ref, hid_ref, h_ref, dy_ref, dh_scr = rest[len(dep_args):]
        i, j = pl.program_id(0), pl.program_id(1)

        @pl.when(j == 0)
        def _():
            xv = x_ref[...]
            r = lax.rsqrt(jnp.mean(xv * xv, axis=-1, keepdims=True) + EPS)
            h_ref[...] = (xv * r * g_ref[...]).astype(BF16)
            dy_ref[...] = (0.5 * dxo_ref[...]).astype(BF16)
            dh_scr[...] = jnp.zeros_like(dh_scr)

        h = h_ref[...]
        a = _dg(h, wg_ref[...], 1, 0)
        b = _dg(h, wu_ref[...], 1, 0)
        sg = jax.nn.sigmoid(a)
        sl = a * sg
        dhid = _dg(dy_ref[...], wd_ref[...], 1, 1)
        da = (dhid * b * (sg * (1.0 + a * (1.0 - sg)))).astype(BF16)
        db = (dhid * sl).astype(BF16)
        dab_ref[0] = da
        dab_ref[1] = db
        hid_ref[...] = (sl * b).astype(BF16)
        dh_scr[...] += _dg(da, wg_ref[...], 1, 1) + _dg(db, wu_ref[...], 1, 1)

        @pl.when(jnp.logical_and(i == 0, j == 0))
        def _():
            dg_ref[...] = jnp.zeros_like(dg_ref)

        @pl.when(j == N_SHARD - 1)
        def _():
            dx, dg = _rms_bwd(x_ref[...], g_ref[...], dh_scr[...])
            dx_ref[...] = dxo_ref[...] + dx
            dg_ref[...] += dg

    row = pl.BlockSpec((tb, d), lambda i, j: (i, 0))
    one = pl.BlockSpec((1, d), lambda i, j: (0, 0))
    lg, lu, ld = layer
    return pl.pallas_call(
        body, name=name, grid=(s // tb, N_SHARD),
        in_specs=[row, row, one,
                  pl.BlockSpec((None, None, d, fc), lambda i, j: (j, lg, 0, 0)),
                  pl.BlockSpec((None, None, d, fc), lambda i, j: (j, lu, 0, 0)),
                  pl.BlockSpec((None, None, fc, d), lambda i, j: (j, ld, 0, 0))] + dep_specs,
        out_specs=[row, one,
                   pl.BlockSpec((None, 2, tb, fc), lambda i, j: (j, 0, i, 0)),
                   pl.BlockSpec((None, tb, fc), lambda i, j: (j, i, 0)),
                   row, row],
        out_shape=[jax.ShapeDtypeStruct((s, d), F32), jax.ShapeDtypeStruct((1, d), F32),
                   jax.ShapeDtypeStruct((N_SHARD, 2, s, fc), BF16),
                   jax.ShapeDtypeStruct((N_SHARD, s, fc), BF16),
                   jax.ShapeDtypeStruct((s, d), BF16), jax.ShapeDtypeStruct((s, d), BF16)],
        scratch_shapes=[pltpu.VMEM((tb, d), F32)],
        compiler_params=_params("arbitrary", "arbitrary",
                                vmem_bytes=min(VMEM_BYTES_V7X - (8 << 20),
                                               12 * d * fc + 44 * tb * d + 40 * tb * fc + (4 << 20))),
    )(x, dxo, g, wg, wu, wd, *dep_args)


def _pblk(tb, width, col0, rev_n=None):
    assert col0 % width == 0
    cb = col0 // width
    if rev_n is None:
        return pl.BlockSpec((tb, width), lambda i: (i, cb))
    return pl.BlockSpec((tb, width), lambda i: (rev_n - 1 - i, cb))


def _full(shape):
    return pl.BlockSpec(shape, lambda i: (0,) * len(shape))


def conv_fwd(src, col0, width, w, bias, *, tb, name):
    s = src.shape[0]
    cb = col0 // width

    def body(x_ref, p_ref, w_ref, b_ref, y_ref):
        i = pl.program_id(0)
        prev = jnp.where(i > 0, p_ref[...], 0.0)
        xe = jnp.concatenate([prev, x_ref[...]], axis=0)
        wv = w_ref[...]
        acc = b_ref[...] + wv[3:4] * xe[8:]
        for k in range(3):
            acc = acc + wv[k:k + 1] * pltpu.roll(xe, 3 - k, 0)[8:]
        y_ref[...] = acc

    return pl.pallas_call(
        body, name=name, grid=(s // tb,),
        in_specs=[_pblk(tb, width, col0),
                  pl.BlockSpec((8, width), lambda i: (jnp.maximum(i * (tb // 8) - 1, 0), cb)),
                  _full((4, width)), _full((1, width))],
        out_specs=pl.BlockSpec((tb, width), lambda i: (i, 0)),
        out_shape=jax.ShapeDtypeStruct((s, width), F32),
        compiler_params=_params("parallel"),
    )(src, src, w, bias)


def conv_bwd(src, col0, width, dpre, w, *, tb, name):
    s = src.shape[0]
    cb = col0 // width
    nt = s // tb

    def body(x_ref, p_ref, d_ref, n_ref, w_ref, dx_ref, dwb_ref):
        i = pl.program_id(0)
        d = d_ref[...]
        nxt = jnp.where(i < nt - 1, n_ref[...], 0.0)
        de = jnp.concatenate([d, nxt], axis=0)
        wv = w_ref[...]
        dx = wv[3:4] * d
        for k in range(3):
            dx = dx + wv[k:k + 1] * pltpu.roll(de, tb + 8 - (3 - k), 0)[:tb]
        dx_ref[...] = dx.astype(BF16)

        @pl.when(i == 0)
        def _():
            dwb_ref[...] = jnp.zeros_like(dwb_ref)

        prev = jnp.where(i > 0, p_ref[...], 0.0)
        xe = jnp.concatenate([prev, x_ref[...]], axis=0)
        dwb_ref[3:4, :] += jnp.sum(d * xe[8:], axis=0, keepdims=True)
        for k in range(3):
            dwb_ref[k:k + 1, :] += jnp.sum(d * pltpu.roll(xe, 3 - k, 0)[8:], axis=0, keepdims=True)
        dwb_ref[4:5, :] += jnp.sum(d, axis=0, keepdims=True)

    return pl.pallas_call(
        body, name=name, grid=(nt,),
        in_specs=[_pblk(tb, width, col0),
                  pl.BlockSpec((8, width), lambda i: (jnp.maximum(i * (tb // 8) - 1, 0), cb)),
                  pl.BlockSpec((tb, width), lambda i: (i, 0)),
                  pl.BlockSpec((8, width), lambda i: (jnp.minimum((i + 1) * (tb // 8), s // 8 - 1), 0)),
                  _full((4, width))],
        out_specs=[pl.BlockSpec((tb, width), lambda i: (i, 0)), _full((8, width))],
        out_shape=[jax.ShapeDtypeStruct((s, width), BF16), jax.ShapeDtypeStruct((8, width), F32)],
        compiler_params=_params("arbitrary"),
    )(src, src, dpre, dpre, w)


SGU_BLOCK = 128


def _sgu_pre(u, v, lg, lb):
    ug = _gelu(u)
    vg = _gelu(v)
    mu = jnp.mean(vg, axis=-1, keepdims=True)
    var = jnp.mean(jnp.square(vg - mu), axis=-1, keepdims=True)
    return ug, (vg - mu) * lax.rsqrt(var + EPS) * lg + lb


def _sgu_mask():
    shp = (SGU_BLOCK, SGU_BLOCK)
    return (_rows(shp) // CHUNK) >= (_lanes(shp) // CHUNK)


def sgu_fwd(proj, lg, lb, ws, bt, *, tb, name):
    s = proj.shape[0]
    w = BRANCH_W

    def body(u_ref, v_ref, lg_ref, lb_ref, w_ref, b_ref, y_ref):
        ug, vn = _sgu_pre(u_ref[...], v_ref[...], lg_ref[...], lb_ref[...])
        mask = _sgu_mask()
        for g in range(4):
            wm = jnp.where(mask, w_ref[g], 0.0)
            bcol = _col(b_ref[...], g)
            cs = slice(128 * g, 128 * g + 128)
            for n in range(tb // SGU_BLOCK):
                rs = slice(SGU_BLOCK * n, SGU_BLOCK * (n + 1))
                mixed = _dg(wm, vn[rs, cs], 1, 0) + bcol
                y_ref[rs, cs] = (ug[rs, cs] * mixed).astype(BF16)

    return pl.pallas_call(
        body, name=name, grid=(s // tb,),
        in_specs=[_pblk(tb, w, C_AU), _pblk(tb, w, C_AV), _full((1, w)), _full((1, w)),
                  _full((4, 128, 128)), _full((128, 128))],
        out_specs=pl.BlockSpec((tb, w), lambda i: (i, 0)),
        out_shape=jax.ShapeDtypeStruct((s, w), BF16),
        compiler_params=_params("parallel"),
    )(proj, proj, lg, lb, ws, bt)


def sgu_bwd(proj, dy, lg, lb, ws, bt, *, tb, dep=None, name):
    s = proj.shape[0]
    w = BRANCH_W
    dep_specs, dep_args = _dep_operand(dep)

    def body(u_ref, v_ref, dy_ref, lg_ref, lb_ref, w_ref, b_ref, *rest):
        du_ref, dv_ref, dlg_ref, dlb_ref, dw_ref, db_ref, dug_scr, dvn_scr = rest[len(dep_args):]

        @pl.when(pl.program_id(0) == 0)
        def _():
            dlg_ref[...] = jnp.zeros_like(dlg_ref)
            dlb_ref[...] = jnp.zeros_like(dlb_ref)
            dw_ref[...] = jnp.zeros_like(dw_ref)
            db_ref[...] = jnp.zeros_like(db_ref)

        (ug, vn), vf = jax.vjp(_sgu_pre, u_ref[...], v_ref[...], lg_ref[...], lb_ref[...])
        dyv = dy_ref[...]
        mask = _sgu_mask()
        for g in range(4):
            wm = jnp.where(mask, w_ref[g], 0.0)
            bcol = _col(b_ref[...], g)
            cs = slice(128 * g, 128 * g + 128)
            dwg = jnp.zeros((SGU_BLOCK, SGU_BLOCK), F32)
            dbc = jnp.zeros((SGU_BLOCK, 1), F32)
            for n in range(tb // SGU_BLOCK):
                rs = slice(SGU_BLOCK * n, SGU_BLOCK * (n + 1))
                blk = vn[rs, cs]
                mixed = _dg(wm, blk, 1, 0) + bcol
                dmix = dyv[rs, cs] * ug[rs, cs]
                dug_scr[rs, cs] = dyv[rs, cs] * mixed
                dvn_scr[rs, cs] = _dg(wm, dmix, 0, 0)
                dwg = dwg + _dg(dmix, blk, 1, 1)
                dbc = dbc + jnp.sum(dmix, axis=1, keepdims=True)
            dw_ref[g] += jnp.where(mask, dwg, 0.0)
            db_ref[...] += _put_col(dbc, g, 128)
        du, dv, dlg, dlb = vf((dug_scr[...], dvn_scr[...]))
        du_ref[...] = du.astype(BF16)
        dv_ref[...] = dv.astype(BF16)
        dlg_ref[...] += dlg
        dlb_ref[...] += dlb

    row = pl.BlockSpec((tb, w), lambda i: (i, 0))
    return pl.pallas_call(
        body, name=name, grid=(s // tb,),
        in_specs=[_pblk(tb, w, C_AU), _pblk(tb, w, C_AV), row, _full((1, w)), _full((1, w)),
                  _full((4, 128, 128)), _full((128, 128))] + dep_specs,
        out_specs=[row, row, _full((1, w)), _full((1, w)), _full((4, 128, 128)), _full((128, 128))],
        out_shape=[jax.ShapeDtypeStruct((s, w), BF16), jax.ShapeDtypeStruct((s, w), BF16),
                   jax.ShapeDtypeStruct((1, w), F32), jax.ShapeDtypeStruct((1, w), F32),
                   jax.ShapeDtypeStruct((4, 128, 128), F32), jax.ShapeDtypeStruct((128, 128), F32)],
        scratch_shapes=[pltpu.VMEM((tb, w), F32), pltpu.VMEM((tb, w), F32)],
        compiler_params=_params("arbitrary"),
    )(proj, proj, dy, lg, lb, ws, bt, *dep_args)


POOL_HALO = 16


def _pooled(xe, t0, tb):
    s2 = xe + pltpu.roll(xe, 1, 0)
    s4 = s2 + pltpu.roll(s2, 2, 0)
    s8 = s4 + pltpu.roll(s4, 4, 0)
    s16 = s8 + pltpu.roll(s8, 8, 0)
    t = (t0 + _rows((tb, 128))).astype(F32)
    outs = []
    for g, sw in enumerate((s2, s4, s8, s16)):
        cs = slice(128 * g, 128 * g + 128)
        cnt = jnp.minimum(t + 1.0, float(2 ** (g + 1)))
        outs.append(sw[POOL_HALO:, cs] / cnt - xe[POOL_HALO:, cs])
    return outs


def pool_fwd(proj, wp, scale, *, tb, name):
    s = proj.shape[0]
    w = BRANCH_W

    def body(x_ref, p_ref, w_ref, sc_ref, y_ref):
        i = pl.program_id(0)
        prev = jnp.where(i > 0, p_ref[...], 0.0)
        xe = jnp.concatenate([prev, x_ref[...]], axis=0)
        pooled = _pooled(xe, i * tb, tb)
        sc = sc_ref[...]
        for g in range(4):
            cs = slice(128 * g, 128 * g + 128)
            y_ref[:, cs] = (_dg(pooled[g], w_ref[g], 1, 0) * sc[:, cs]).astype(BF16)

    cb = C_DX // w
    return pl.pallas_call(
        body, name=name, grid=(s // tb,),
        in_specs=[_pblk(tb, w, C_DX),
                  pl.BlockSpec((POOL_HALO, w), lambda i: (jnp.maximum(i * (tb // POOL_HALO) - 1, 0), cb)),
                  _full((4, 128, 128)), _full((1, w))],
        out_specs=pl.BlockSpec((tb, w), lambda i: (i, 0)),
        out_shape=jax.ShapeDtypeStruct((s, w), BF16),
        compiler_params=_params("parallel"),
    )(proj, proj, wp, scale)


def pool_bwd(proj, dy, wp, scale, *, tb, name):
    s = proj.shape[0]
    w = BRANCH_W
    nt = s // tb
    te = tb + POOL_HALO

    def body(x_ref, p_ref, dy_ref, n_ref, w_ref, sc_ref, dx_ref, dw_ref, dsc_ref):
        i = pl.program_id(0)

        @pl.when(i == 0)
        def _():
            dw_ref[...] = jnp.zeros_like(dw_ref)
            dsc_ref[...] = jnp.zeros_like(dsc_ref)

        prev = jnp.where(i > 0, p_ref[...], 0.0)
        xe = jnp.concatenate([prev, x_ref[...]], axis=0)
        pooled = _pooled(xe, i * tb, tb)
        dyv = dy_ref[...]
        nxt = jnp.where(i < nt - 1, n_ref[...], 0.0)
        dye = jnp.concatenate([dyv, nxt], axis=0)
        sc = sc_ref[...]
        t = (i * tb + _rows((te, 128))).astype(F32)
        for g in range(4):
            cs = slice(128 * g, 128 * g + 128)
            win = 2 ** (g + 1)
            dpm = dye[:, cs] * sc[:, cs]
            dpool = _dg(dpm, w_ref[g], 1, 1)
            q = dpool / jnp.minimum(t + 1.0, float(win))
            r, sh = q, 1
            while sh < win:
                r = r + pltpu.roll(r, te - sh, 0)
                sh *= 2
            dx_ref[:, cs] = (r[:tb] - dpool[:tb]).astype(BF16)
            dw_ref[g] += _dg(pooled[g], dpm[:tb], 0, 0)
            dsc_ref[:, cs] += jnp.sum(dyv[:, cs] * _dg(pooled[g], w_ref[g], 1, 0), axis=0, keepdims=True)

    cb = C_DX // w
    row = pl.BlockSpec((tb, w), lambda i: (i, 0))
    return pl.pallas_call(
        body, name=name, grid=(nt,),
        in_specs=[_pblk(tb, w, C_DX),
                  pl.BlockSpec((POOL_HALO, w), lambda i: (jnp.maximum(i * (tb // POOL_HALO) - 1, 0), cb)),
                  row,
                  pl.BlockSpec((POOL_HALO, w),
                               lambda i: (jnp.minimum((i + 1) * (tb // POOL_HALO), s // POOL_HALO - 1), 0)),
                  _full((4, 128, 128)), _full((1, w))],
        out_specs=[row, _full((4, 128, 128)), _full((1, w))],
        out_shape=[jax.ShapeDtypeStruct((s, w), BF16), jax.ShapeDtypeStruct((4, 128, 128), F32),
                   jax.ShapeDtypeStruct((1, w), F32)],
        compiler_params=_params("arbitrary"),
    )(proj, proj, dy, dy, wp, scale)


def _neg_expm1(z):
    p = 1.0 + z * (1.0 / 9.0)
    for n in (8.0, 7.0, 6.0, 5.0, 4.0, 3.0, 2.0):
        p = 1.0 + z * (1.0 / n) * p
    return jnp.where(z > -0.5, -z * p, 1.0 - jnp.exp(z))


def _lru_gates(xc, wa, wx, ba, bx, lam):
    r = jax.nn.sigmoid(mm_nn(xc, wa) + ba)
    i = jax.nn.sigmoid(mm_nn(xc, wx) + bx)
    log_a = -LRU_C * r * _softplus(-lam)
    a = jnp.exp(log_a)
    return a, jnp.sqrt(_neg_expm1(2.0 * log_a)) * (i * xc)


def _scan_down(a, b):
    n = a.shape[0]
    row = _rows(a.shape)
    sh = 1
    while sh < n:
        keep = row >= sh
        a_sh = jnp.where(keep, pltpu.roll(a, sh, 0), 1.0)
        b_sh = jnp.where(keep, pltpu.roll(b, sh, 0), 0.0)
        b = a * b_sh + b
        a = a * a_sh
        sh *= 2
    return a, b


def _scan_up(a, b):
    n = a.shape[0]
    row = _rows(a.shape)
    sh = 1
    while sh < n:
        keep = row < n - sh
        a_sh = jnp.where(keep, pltpu.roll(a, n - sh, 0), 1.0)
        b_sh = jnp.where(keep, pltpu.roll(b, n - sh, 0), 0.0)
        b = a * b_sh + b
        a = a * a_sh
        sh *= 2
    return a, b


def lru_fwd(xc, proj, wa, wx, ba, bx, lam, *, tb, name):
    s = xc.shape[0]
    w = BRANCH_W

    def body(xc_ref, gate_ref, wa_ref, wx_ref, ba_ref, bx_ref, lam_ref, y_ref, h_ref, carry):
        @pl.when(pl.program_id(0) == 0)
        def _():
            carry[...] = jnp.zeros_like(carry)

        a, b = _lru_gates(xc_ref[...], wa_ref[...], wx_ref[...], ba_ref[...], bx_ref[...], lam_ref[...])
        pa, hb = _scan_down(a, b)
        h = pa * carry[...] + hb
        h_ref[...] = h
        carry[...] = h_ref[tb - 1:tb, :]
        y_ref[...] = (h * _gelu(gate_ref[...])).astype(BF16)

    row = pl.BlockSpec((tb, w), lambda i: (i, 0))
    return pl.pallas_call(
        body, name=name, grid=(s // tb,),
        in_specs=[row, _pblk(tb, w, C_BG), _full((w, w)), _full((w, w)),
                  _full((1, w)), _full((1, w)), _full((1, w))],
        out_specs=[row, row],
        out_shape=[jax.ShapeDtypeStruct((s, w), BF16), jax.ShapeDtypeStruct((s, w), F32)],
        scratch_shapes=[pltpu.VMEM((1, w), F32)],
        compiler_params=_params("arbitrary"),
    )(xc, proj, wa, wx, ba, bx, lam)


def lru_bwd(xc, proj, h, dy, wa, wx, ba, bx, lam, *, tb, name):
    s = xc.shape[0]
    w = BRANCH_W
    nt = s // tb

    def body(xc_ref, gate_ref, h_ref, hp_ref, dy_ref, wa_ref, wx_ref, ba_ref, bx_ref, lam_ref,
             dxc_ref, dgate_ref, dwa_ref, dwx_ref, dba_ref, dbx_ref, dlam_ref, carry):
        i = pl.program_id(0)

        @pl.when(i == 0)
        def _():
            carry[...] = jnp.zeros_like(carry)
            for r in (dwa_ref, dwx_ref, dba_ref, dbx_ref, dlam_ref):
                r[...] = jnp.zeros_like(r)

        (a, b), vf = jax.vjp(_lru_gates, xc_ref[...], wa_ref[...], wx_ref[...],
                             ba_ref[...], bx_ref[...], lam_ref[...])
        gate = gate_ref[...]
        dyv = dy_ref[...]
        hv = h_ref[...]
        row = _rows((tb, w))
        a_next = jnp.where(row < tb - 1, pltpu.roll(a, tb - 1, 0), 1.0)
        pa, xb = _scan_up(a_next, dyv * _gelu(gate))
        dh = xb + pa * carry[...]
        carry[...] = a[0:1, :] * dh[0:1, :]
        h_first = jnp.where(i < nt - 1, hp_ref[7:8, :], 0.0)
        h_prev = jnp.where(row >= 1, pltpu.roll(hv, 1, 0), h_first)
        dxc, dwa, dwx, dba, dbx, dlam = vf((dh * h_prev, dh))
        dxc_ref[...] = dxc
        dgate_ref[...] = (dyv * hv * _gelu_grad(gate)).astype(BF16)
        dwa_ref[...] += dwa
        dwx_ref[...] += dwx
        dba_ref[...] += dba
        dbx_ref[...] += dbx
        dlam_ref[...] += dlam

    row_spec = pl.BlockSpec((tb, w), lambda i: (nt - 1 - i, 0))
    hp_spec = pl.BlockSpec((8, w), lambda i: (jnp.maximum((nt - 1 - i) * (tb // 8) - 1, 0), 0))
    return pl.pallas_call(
        body, name=name, grid=(nt,),
        in_specs=[row_spec, _pblk(tb, w, C_BG, rev_n=nt), row_spec, hp_spec, row_spec,
                  _full((w, w)), _full((w, w)), _full((1, w)), _full((1, w)), _full((1, w))],
        out_specs=[row_spec, row_spec, _full((w, w)), _full((w, w)),
                   _full((1, w)), _full((1, w)), _full((1, w))],
        out_shape=[jax.ShapeDtypeStruct((s, w), F32), jax.ShapeDtypeStruct((s, w), BF16),
                   jax.ShapeDtypeStruct((w, w), F32), jax.ShapeDtypeStruct((w, w), F32),
                   jax.ShapeDtypeStruct((1, w), F32), jax.ShapeDtypeStruct((1, w), F32),
                   jax.ShapeDtypeStruct((1, w), F32)],
        scratch_shapes=[pltpu.VMEM((1, w), F32)],
        compiler_params=_params("arbitrary"),
    )(xc, proj, h, h, dy, wa, wx, ba, bx, lam)


def _gdn_g(ba, alog, dt):
    return -jnp.exp(alog) * _softplus(ba + dt)


def _gdn_local(qp, kp, vp, bpre, gc):
    b, c, dk = qp.shape
    q = _silu(qp)
    k = _silu(kp)
    v = _silu(vp)
    q = q * lax.rsqrt(jnp.sum(q * q, axis=-1, keepdims=True) + EPS) * (dk ** -0.5)
    k = k * lax.rsqrt(jnp.sum(k * k, axis=-1, keepdims=True) + EPS)
    beta = jax.nn.sigmoid(bpre)
    sq = (b, c, c)
    ii, jj = _iota3(sq, 1), _iota3(sq, 2)
    gr = jnp.sum(gc * (ii == jj).astype(F32), axis=1, keepdims=True)
    decay = jnp.exp(jnp.where(ii >= jj, gc - gr, -jnp.inf))
    wd = (b, c, 2 * c)
    wi, wj = _iota3(wd, 1), _iota3(wd, 2) - c
    gr_w = jnp.sum(gc * (wi == wj).astype(F32), axis=1, keepdims=True)
    decay_w = jnp.exp(jnp.where(jnp.logical_and(wj >= 0, wi > wj), gc - gr_w, -jnp.inf))
    kb = k * beta
    pad = jnp.zeros((b, c, dk), F32)
    t = _tri_inv(bmm_nt(kb, jnp.concatenate([pad, k], axis=1)) * decay_w)
    u = bmm_nn(t, jnp.concatenate([v * beta, pad], axis=1))
    wk = bmm_nn(t, jnp.concatenate([kb * jnp.exp(gc), pad], axis=1))
    attn = bmm_nt(q, k) * decay
    g_last = jnp.sum(jnp.where(_iota3((b, c, 1), 1) == c - 1, gc, 0.0), axis=1, keepdims=True)
    return u, wk, attn, q * jnp.exp(gc), k * jnp.exp(g_last - gc), jnp.exp(g_last)


def _gdn_recur(s, u, wk, attn, qg, kd, eg, zz, ng):
    v_new = u - bmm_nn(wk, s)
    o = bmm_nn(qg, s) + bmm_nn(attn, v_new)
    s_out = s * eg + bmm_tn(kd, v_new)
    o = o * lax.rsqrt(jnp.mean(o * o, axis=-1, keepdims=True) + EPS) * ng
    return o * _silu(zz), s_out


def _gdn_inputs(qkv_ref, ba, gcums, cps):
    c, hd = CHUNK, GDN_DK
    qs, ks, vs, bs, gs = [], [], [], [], []
    for ci in range(cps):
        rs = slice(c * ci, c * (ci + 1))
        for h in range(GDN_HEADS):
            qs.append(qkv_ref[rs, hd * h:hd * (h + 1)])
            ks.append(qkv_ref[rs, 512 + hd * h:512 + hd * (h + 1)])
            vs.append(qkv_ref[rs, 1024 + hd * h:1024 + hd * (h + 1)])
            bs.append(_col(ba[rs], h))
            gs.append(_col(gcums[ci], 4 + h))
    return tuple(jnp.stack(t) for t in (qs, ks, vs, bs, gs))


def _chunk_cumsum(g, transpose=False):
    c = CHUNK
    tri = (_rows((c, c)) >= _lanes((c, c))).astype(F32)
    return _dg3(tri, g, 0 if transpose else 1, 0)


def gdn_fwd(qkv, proj, alog, dt, ng, *, cps, name):
    s = qkv.shape[0]
    c = CHUNK
    nc = s // c
    hd = GDN_DK
    tb = c * cps

    def body(qkv_ref, z_ref, ba_ref, al_ref, dt_ref, ng_ref, y_ref, sall_ref, s_scr):
        @pl.when(pl.program_id(0) == 0)
        def _():
            s_scr[...] = jnp.zeros_like(s_scr)

        nh = GDN_HEADS
        ba = ba_ref[...]
        g = _gdn_g(ba, al_ref[...], dt_ref[...])
        gcums = [_chunk_cumsum(g[c * ci:c * (ci + 1)]) for ci in range(cps)]
        loc = _gdn_local(*_gdn_inputs(qkv_ref, ba, gcums, cps))
        for ci in range(cps):
            rs = slice(c * ci, c * (ci + 1))
            sl = slice(nh * ci, nh * (ci + 1))
            s_in = s_scr[...]
            sall_ref[ci] = s_in
            zz = jnp.stack([z_ref[rs, hd * h:hd * (h + 1)] for h in range(nh)])
            y, s_out = _gdn_recur(s_in, *(t[sl] for t in loc), zz, ng_ref[...])
            s_scr[...] = s_out
            for h in range(nh):
                y_ref[rs, hd * h:hd * (h + 1)] = y[h].astype(BF16)

    return pl.pallas_call(
        body, name=name, grid=(nc // cps,),
        in_specs=[pl.BlockSpec((tb, 1536), lambda i: (i, 0)), _pblk(tb, 512, C_Z), _pblk(tb, 128, C_BA),
                  _full((1, 128)), _full((1, 128)), _full((1, 128))],
        out_specs=[pl.BlockSpec((tb, 512), lambda i: (i, 0)),
                   pl.BlockSpec((cps, GDN_HEADS, hd, hd), lambda i: (i, 0, 0, 0))],
        out_shape=[jax.ShapeDtypeStruct((s, 512), BF16), jax.ShapeDtypeStruct((nc, GDN_HEADS, hd, hd), F32)],
        scratch_shapes=[pltpu.VMEM((GDN_HEADS, hd, hd), F32)],
        compiler_params=_params("arbitrary"),
    )(qkv, proj, proj, alog, dt, ng)


def gdn_bwd(qkv, proj, sall, dy, alog, dt, ng, *, cps, name):
    s = qkv.shape[0]
    c = CHUNK
    nc = s // c
    hd = GDN_DK
    tb = c * cps
    nt = nc // cps

    def body(qkv_ref, z_ref, ba_ref, sall_ref, dy_ref, al_ref, dt_ref, ng_ref,
             dqkv_ref, dz_ref, dba_ref, dal_ref, ddt_ref, dng_ref, ds_scr):
        @pl.when(pl.program_id(0) == 0)
        def _():
            ds_scr[...] = jnp.zeros_like(ds_scr)
            dal_ref[...] = jnp.zeros_like(dal_ref)
            ddt_ref[...] = jnp.zeros_like(ddt_ref)
            dng_ref[...] = jnp.zeros_like(dng_ref)

        nh = GDN_HEADS
        ba = ba_ref[...]
        g, gvf = jax.vjp(_gdn_g, ba, al_ref[...], dt_ref[...])
        gcums = [_chunk_cumsum(g[c * ci:c * (ci + 1)]) for ci in range(cps)]
        loc, loc_vf = jax.vjp(_gdn_local, *_gdn_inputs(qkv_ref, ba, gcums, cps))
        d_loc = [None] * cps
        dng = jnp.zeros((1, 128), F32)
        for ci in reversed(range(cps)):
            rs = slice(c * ci, c * (ci + 1))
            sl = slice(nh * ci, nh * (ci + 1))
            zz = jnp.stack([z_ref[rs, hd * h:hd * (h + 1)] for h in range(nh)])
            dyy = jnp.stack([dy_ref[rs, hd * h:hd * (h + 1)] for h in range(nh)])
            _, rvf = jax.vjp(_gdn_recur, sall_ref[ci], *(t[sl] for t in loc), zz, ng_ref[...])
            ds, du, dwk, dattn, dqg, dkd, deg, dzz, dn = rvf((dyy, ds_scr[...]))
            ds_scr[...] = ds
            d_loc[ci] = (du, dwk, dattn, dqg, dkd, deg)
            dng = dng + dn
            for h in range(nh):
                dz_ref[rs, hd * h:hd * (h + 1)] = dzz[h].astype(BF16)
        dq, dk, dv, dbp, dgc = loc_vf(tuple(jnp.concatenate([d_loc[ci][j] for ci in range(cps)], axis=0)
                                            for j in range(6)))
        dgs, dbas = [], []
        for ci in range(cps):
            rs = slice(c * ci, c * (ci + 1))
            dgcum = jnp.zeros((c, 128), F32)
            dba = jnp.zeros((c, 128), F32)
            for h in range(nh):
                b = nh * ci + h
                dqkv_ref[rs, hd * h:hd * (h + 1)] = dq[b]
                dqkv_ref[rs, 512 + hd * h:512 + hd * (h + 1)] = dk[b]
                dqkv_ref[rs, 1024 + hd * h:1024 + hd * (h + 1)] = dv[b]
                dgcum = dgcum + _put_col(dgc[b], 4 + h, 128)
                dba = dba + _put_col(dbp[b], h, 128)
            dgs.append(_chunk_cumsum(dgcum, transpose=True))
            dbas.append(dba)
        dba2, dal, ddt = gvf(jnp.concatenate(dgs, axis=0))
        dba_ref[...] = (jnp.concatenate(dbas, axis=0) + dba2).astype(BF16)
        dal_ref[...] += dal
        ddt_ref[...] += ddt
        dng_ref[...] += dng

    def rev(width):
        return pl.BlockSpec((tb, width), lambda i: (nt - 1 - i, 0))

    return pl.pallas_call(
        body, name=name, grid=(nt,),
        in_specs=[rev(1536), _pblk(tb, 512, C_Z, rev_n=nt), _pblk(tb, 128, C_BA, rev_n=nt),
                  pl.BlockSpec((cps, GDN_HEADS, hd, hd), lambda i: (nt - 1 - i, 0, 0, 0)), rev(512),
                  _full((1, 128)), _full((1, 128)), _full((1, 128))],
        out_specs=[rev(1536), rev(512), rev(128), _full((1, 128)), _full((1, 128)), _full((1, 128))],
        out_shape=[jax.ShapeDtypeStruct((s, 1536), F32), jax.ShapeDtypeStruct((s, 512), BF16),
                   jax.ShapeDtypeStruct((s, 128), BF16), jax.ShapeDtypeStruct((1, 128), F32),
                   jax.ShapeDtypeStruct((1, 128), F32), jax.ShapeDtypeStruct((1, 128), F32)],
        scratch_shapes=[pltpu.VMEM((GDN_HEADS, hd, hd), F32)],
        compiler_params=_params("arbitrary"),
    )(qkv, proj, proj, sall, dy, alog, dt, ng)


def merge_fwd(x, ys, proj, wb, wo, *, tb, dep=None, name):
    s, d = x.shape
    dep_specs, dep_args = _dep_operand(dep)

    def body(x_ref, ya, yb, yc, yd, g0, g1, g2, g3, wb_ref, wo_ref, *rest):
        o_ref, m_ref = rest[len(dep_args):]
        m = jnp.zeros((tb, d), F32)
        for g, (y_ref, gp_ref) in enumerate(zip((ya, yb, yc, yd), (g0, g1, g2, g3))):
            m = m + jax.nn.sigmoid(gp_ref[...]) * _dg(y_ref[...], wb_ref[g], 1, 0)
        mb = m.astype(BF16)
        m_ref[...] = mb
        o_ref[...] = x_ref[...] + _dg(mb, wo_ref[...], 1, 0)

    row = pl.BlockSpec((tb, d), lambda i: (i, 0))
    yrow = pl.BlockSpec((tb, BRANCH_W), lambda i: (i, 0))
    return pl.pallas_call(
        body, name=name, grid=(s // tb,),
        in_specs=[row] + [yrow] * 4 + [_pblk(tb, d, C_GATE + d * g) for g in range(4)]
        + [_full((4, BRANCH_W, d)), _full((d, d))] + dep_specs,
        out_specs=[row, row],
        out_shape=[jax.ShapeDtypeStruct((s, d), F32), jax.ShapeDtypeStruct((s, d), BF16)],
        compiler_params=_params("parallel"),
    )(x, *ys, proj, proj, proj, proj, wb, wo, *dep_args)


def merge_bwd(dx, ys, proj, wb, wo, *, tb, dep=None, name):
    s, d = dx.shape
    dep_specs, dep_args = _dep_operand(dep)

    def body(dx_ref, ya, yb, yc, yd, g0, g1, g2, g3, wb_ref, wo_ref, *rest):
        dgp_ref, dbr_ref, da, db, dc, dd, dxb_ref = rest[len(dep_args):]
        dxb = dx_ref[...].astype(BF16)
        dxb_ref[...] = dxb
        dm = _dg(dxb, wo_ref[...], 1, 1)
        for g, (y_ref, gp_ref, dy_ref) in enumerate(zip((ya, yb, yc, yd), (g0, g1, g2, g3), (da, db, dc, dd))):
            br = _dg(y_ref[...], wb_ref[g], 1, 0)
            sg = jax.nn.sigmoid(gp_ref[...])
            dgp_ref[:, d * g:d * (g + 1)] = (dm * br * sg * (1.0 - sg)).astype(BF16)
            dbr = (dm * sg).astype(BF16)
            dbr_ref[g] = dbr
            dy_ref[...] = _dg(dbr, wb_ref[g], 1, 1)

    row = pl.BlockSpec((tb, d), lambda i: (i, 0))
    yrow = pl.BlockSpec((tb, BRANCH_W), lambda i: (i, 0))
    return pl.pallas_call(
        body, name=name, grid=(s // tb,),
        in_specs=[row] + [yrow] * 4 + [_pblk(tb, d, C_GATE + d * g) for g in range(4)]
        + [_full((4, BRANCH_W, d)), _full((d, d))] + dep_specs,
        out_specs=[pl.BlockSpec((tb, 4 * d), lambda i: (i, 0)), pl.BlockSpec((4, tb, d), lambda i: (0, i, 0)),
                   yrow, yrow, yrow, yrow, row],
        out_shape=[jax.ShapeDtypeStruct((s, 4 * d), BF16), jax.ShapeDtypeStruct((4, s, d), BF16)]
        + [jax.ShapeDtypeStruct((s, BRANCH_W), F32)] * 4 + [jax.ShapeDtypeStruct((s, d), BF16)],
        compiler_params=_params("parallel"),
    )(dx, *ys, proj, proj, proj, proj, wb, wo, *dep_args)


def loss_head(x, g, target, *, tb, name):
    s, d = x.shape

    def body(x_ref, g_ref, t_ref, dx_ref, dg_ref, loss_ref):
        @pl.when(pl.program_id(0) == 0)
        def _():
            dg_ref[...] = jnp.zeros_like(dg_ref)
            loss_ref[...] = jnp.zeros_like(loss_ref)

        xv, gv = x_ref[...], g_ref[...]
        r = lax.rsqrt(jnp.mean(xv * xv, axis=-1, keepdims=True) + EPS)
        err = xv * r * gv - t_ref[...]
        loss_ref[...] += 0.5 * jnp.sum(jnp.mean(err * err, axis=-1, keepdims=True), axis=0, keepdims=True)
        dx, dg = _rms_bwd(xv, gv, err * (1.0 / d))
        dx_ref[...] = dx
        dg_ref[...] += dg

    row = pl.BlockSpec((tb, d), lambda i: (i, 0))
    return pl.pallas_call(
        body, name=name, grid=(s // tb,),
        in_specs=[row, _full((1, d)), row],
        out_specs=[row, _full((1, d)), _full((1, 128))],
        out_shape=[jax.ShapeDtypeStruct((s, d), F32), jax.ShapeDtypeStruct((1, d), F32),
                   jax.ShapeDtypeStruct((1, 128), F32)],
        compiler_params=_params("arbitrary"),
    )(x, g, target)


def adamw(w, g, m, v, *, name):
    shape = w.shape
    lead, (r, c) = shape[:-2], shape[-2:]
    tr = r
    while tr * c * 4 > (1 << 20) and tr % 16 == 0:
        tr //= 2
    c1 = 1.0 / (1.0 - ADAM_B1 ** ADAM_STEP)
    c2 = 1.0 / (1.0 - ADAM_B2 ** ADAM_STEP)

    def body(w_ref, g_ref, m_ref, v_ref, d_ref, nm_ref, nv_ref):
        gv = g_ref[...]
        nm = ADAM_B1 * m_ref[...] + (1.0 - ADAM_B1) * gv
        nv = ADAM_B2 * v_ref[...] + (1.0 - ADAM_B2) * (gv * gv)
        nm_ref[...] = nm
        nv_ref[...] = nv
        d_ref[...] = -ADAM_LR * ((nm * c1) / (jnp.sqrt(nv * c2) + ADAM_EPS) + ADAM_WD * w_ref[...])

    blk = pl.BlockSpec((None,) * len(lead) + (tr, c), lambda *i: i + (0,))
    return tuple(pl.pallas_call(
        body, name=name, grid=lead + (r // tr,),
        in_specs=[blk] * 4, out_specs=[blk] * 3,
        out_shape=[jax.ShapeDtypeStruct(shape, F32)] * 3,
        compiler_params=_params(*(("parallel",) * (len(lead) + 1))),
    )(w, g, m, v))


_ANY = pl.BlockSpec(memory_space=pl.ANY)


def _place():
    x, y, c = lax.axis_index("x"), lax.axis_index("y"), lax.axis_index("c")
    return x, y, c, [(1 - x, y), (x, 1 - y), (1 - x, 1 - y)]


def gather_shards(bufs, *, name):
    n = len(bufs)

    def body(*refs):
        outs = refs[n:2 * n]
        ici_send, ici_recv, d2d_send, d2d_recv = refs[2 * n:]
        x, y, c, chips = _place()
        me = 2 * x + y

        def ici(a, j, slab, to):
            return pltpu.make_async_remote_copy(
                src_ref=outs[a].at[slab, c], dst_ref=outs[a].at[slab, c], send_sem=ici_send.at[a, j],
                recv_sem=ici_recv.at[a, j], device_id=to, device_id_type=MESH)

        def d2d(a, j, slab, half):
            return pltpu.make_async_remote_copy(
                src_ref=outs[a].at[slab, half], dst_ref=outs[a].at[slab, half], send_sem=d2d_send.at[a, j],
                recv_sem=d2d_recv.at[a, j], device_id=(x, y, 1 - c), device_id_type=MESH)

        sends = []
        for a in range(n):
            for j, (px, py) in enumerate(chips):
                cp = ici(a, j, me, (px, py, c))
                cp.start()
                sends.append(cp)
        for a in range(n):
            for j, (px, py) in enumerate(chips):
                ici(a, j, 2 * px + py, (px, py, c)).wait_recv()
                cp = d2d(a, j, 2 * px + py, c)
                cp.start()
                sends.append(cp)
        for a in range(n):
            for j, (px, py) in enumerate(chips):
                d2d(a, j, 2 * px + py, 1 - c).wait_recv()
        for cp in sends:
            cp.wait_send()

    return pl.pallas_call(
        body, name=name,
        in_specs=[_ANY] * n, out_specs=[_ANY] * n,
        out_shape=[jax.ShapeDtypeStruct(t.shape, t.dtype) for t in bufs],
        input_output_aliases={a: a for a in range(n)},
        scratch_shapes=[pltpu.SemaphoreType.DMA((n, 3))] * 4,
    )(*bufs)


_SEM = pl.BlockSpec(memory_space=pltpu.SEMAPHORE)
_VMEM = pl.BlockSpec(memory_space=pltpu.VMEM)
_EFFECT = pltpu.SideEffectType.DATAFLOW_SIDE_EFFECTING
_TOKEN = jax.ShapeDtypeStruct((8, 128), F32)


def _gather_copies(outs, sems_ici, sems_d2d):
    x, y, c, chips = _place()

    def ici(a, j, slab, to):
        return pltpu.make_async_remote_copy(
            src_ref=outs[a].at[slab, c], dst_ref=outs[a].at[slab, c], send_sem=sems_ici[0][3 * a + j],
            recv_sem=sems_ici[1][3 * a + j], device_id=to, device_id_type=MESH)

    def d2d(a, j, slab, half):
        return pltpu.make_async_remote_copy(
            src_ref=outs[a].at[slab, half], dst_ref=outs[a].at[slab, half], send_sem=sems_d2d[0][3 * a + j],
            recv_sem=sems_d2d[1][3 * a + j], device_id=(x, y, 1 - c), device_id_type=MESH)

    return x, y, c, chips, ici, d2d


def _async_call(body, bufs, sems_in, after, sems_out, token, *, name):
    n = len(bufs)
    n_out = 6 * n if sems_out else 0
    extra = [] if after is None else [after]
    return pl.pallas_call(
        body, name=name,
        in_specs=[_ANY] * n + [_SEM] * len(sems_in) + [_ANY] * len(extra),
        out_specs=[_ANY] * n + [_SEM] * n_out + [_VMEM] * token,
        out_shape=[jax.ShapeDtypeStruct(t.shape, t.dtype) for t in bufs]
        + [pltpu.SemaphoreType.DMA(())] * n_out + [_TOKEN] * token,
        input_output_aliases={a: a for a in range(n)},
        compiler_params=pltpu.CompilerParams(has_side_effects=_EFFECT),
    )(*bufs, *sems_in, *extra)


def gather_start(bufs, *, name):
    n = len(bufs)

    def body(*refs):
        outs = refs[n:2 * n]
        sems, token = refs[2 * n:8 * n], refs[8 * n]
        x, y, c, chips, ici, _ = _gather_copies(outs, (sems[:3 * n], sems[3 * n:]), None)
        for a in range(n):
            for j, (px, py) in enumerate(chips):
                ici(a, j, 2 * x + y, (px, py, c)).start()
        token[...] = jnp.zeros_like(token)

    res = _async_call(body, bufs, [], None, True, True, name=name)
    return res[:n], res[n:7 * n], res[7 * n]


def gather_pass(bufs, sems, after, *, name):
    n = len(bufs)

    def body(*refs):
        s_in = refs[n:7 * n]
        outs = refs[7 * n + 1:8 * n + 1]
        s_out, token = refs[8 * n + 1:14 * n + 1], refs[14 * n + 1]
        x, y, c, chips, ici, d2d = _gather_copies(outs, (s_in[:3 * n], s_in[3 * n:]),
                                                  (s_out[:3 * n], s_out[3 * n:]))
        for a in range(n):
            for j, (px, py) in enumerate(chips):
                ici(a, j, 2 * px + py, (px, py, c)).wait_recv()
                d2d(a, j, 2 * px + py, c).start()
        for a in range(n):
            for j, (px, py) in enumerate(chips):
                ici(a, j, 2 * x + y, (px, py, c)).wait_send()
        token[...] = jnp.zeros_like(token)

    res = _async_call(body, bufs, list(sems), after, True, True, name=name)
    return res[:n], res[n:7 * n], res[7 * n]


def gather_wait(bufs, sems, after, *, name):
    n = len(bufs)

    def body(*refs):
        s_in = refs[n:7 * n]
        outs = refs[7 * n + 1:]
        x, y, c, chips, _, d2d = _gather_copies(outs, None, (s_in[:3 * n], s_in[3 * n:]))
        for a in range(n):
            for j, (px, py) in enumerate(chips):
                d2d(a, j, 2 * px + py, 1 - c).wait_recv()
                d2d(a, j, 2 * px + py, c).wait_send()

    return _async_call(body, bufs, list(sems), after, False, False, name=name)


def _pair_plan(srcs, lands):
    x, y, c, _ = _place()
    return [(srcs[a].at[j, 1 - c], lands[a].at[j], (x, y, 1 - c))
            for a in range(len(srcs)) for j in range(N_SHARD)]


def _chip_plan(srcs, lands):
    x, y, c, chips = _place()
    return [(srcs[a].at[2 * px + py], lands[a].at[2 * x + y], (px, py, c))
            for a in range(len(srcs)) for (px, py) in chips]


def copies_start(arrs, land_shapes, plan, per_array, *, name):
    n = len(arrs)
    k = per_array * n

    def body(*refs):
        lands = refs[2 * n:3 * n]
        sems, token = refs[3 * n:3 * n + 2 * k], refs[3 * n + 2 * k]
        for i, (src, dst, dev) in enumerate(plan(refs[n:2 * n], lands)):
            pltpu.make_async_remote_copy(src_ref=src, dst_ref=dst, send_sem=sems[i], recv_sem=sems[k + i],
                                         device_id=dev, device_id_type=MESH).start()
        token[...] = jnp.zeros_like(token)

    res = pl.pallas_call(
        body, name=name,
        in_specs=[_ANY] * n,
        out_specs=[_ANY] * (2 * n) + [_SEM] * (2 * k) + [_VMEM],
        out_shape=[jax.ShapeDtypeStruct(t.shape, t.dtype) for t in arrs] + list(land_shapes)
        + [pltpu.SemaphoreType.DMA(())] * (2 * k) + [_TOKEN],
        input_output_aliases={a: a for a in range(n)},
        compiler_params=pltpu.CompilerParams(has_side_effects=_EFFECT),
    )(*arrs)
    return res[:n], res[n:2 * n], res[2 * n:2 * n + 2 * k], res[2 * n + 2 * k]


def copies_wait(arrs, lands, sems, after, plan, *, name):
    n = len(arrs)
    k = len(sems) // 2

    def body(*refs):
        s_in = refs[2 * n:2 * n + 2 * k]
        outs = refs[2 * n + 2 * k + 1:]
        for i, (src, dst, dev) in enumerate(plan(outs[:n], outs[n:])):
            cp = pltpu.make_async_remote_copy(src_ref=src, dst_ref=dst, send_sem=s_in[i], recv_sem=s_in[k + i],
                                              device_id=dev, device_id_type=MESH)
            cp.wait_send()
            cp.wait_recv()

    res = pl.pallas_call(
        body, name=name,
        in_specs=[_ANY] * (2 * n) + [_SEM] * (2 * k) + [_ANY],
        out_specs=[_ANY] * (2 * n),
        out_shape=[jax.ShapeDtypeStruct(t.shape, t.dtype) for t in list(arrs) + list(lands)],
        input_output_aliases={a: a for a in range(2 * n)},
        compiler_params=pltpu.CompilerParams(has_side_effects=_EFFECT),
    )(*arrs, *lands, *sems, after)
    return res[:n], res[n:]


def pair_send_halves(arrs, *, name):
    n = len(arrs)

    def body(*refs):
        ins, outs = refs[:n], refs[n:2 * n]
        send_sems, recv_sems = refs[2 * n:]
        x, y, c, _ = _place()
        copies = []
        for a in range(n):
            for j in range(N_SHARD):
                cp = pltpu.make_async_remote_copy(
                    src_ref=ins[a].at[j, 1 - c], dst_ref=outs[a].at[j], send_sem=send_sems.at[a, j],
                    recv_sem=recv_sems.at[a, j], device_id=(x, y, 1 - c), device_id_type=MESH)
                cp.start()
                copies.append(cp)
        for cp in copies:
            cp.wait()

    return pl.pallas_call(
        body, name=name,
        in_specs=[_ANY] * n, out_specs=[_ANY] * n,
        out_shape=[jax.ShapeDtypeStruct((N_SHARD,) + t.shape[2:], t.dtype) for t in arrs],
        scratch_shapes=[pltpu.SemaphoreType.DMA((n, N_SHARD)), pltpu.SemaphoreType.DMA((n, N_SHARD))],
    )(*arrs)


def chip_scatter(arrs, *, name):
    n = len(arrs)

    def body(*refs):
        ins, outs = refs[:n], refs[n:2 * n]
        send_sems, recv_sems = refs[2 * n:]
        x, y, c, chips = _place()
        me = 2 * x + y
        copies = []
        for a in range(n):
            for j, (px, py) in enumerate(chips):
                cp = pltpu.make_async_remote_copy(
                    src_ref=ins[a].at[2 * px + py], dst_ref=outs[a].at[me], send_sem=send_sems.at[a, j],
                    recv_sem=recv_sems.at[a, j], device_id=(px, py, c), device_id_type=MESH)
                cp.start()
                copies.append(cp)
        for cp in copies:
            cp.wait()

    return pl.pallas_call(
        body, name=name,
        in_specs=[_ANY] * n, out_specs=[_ANY] * n,
        out_shape=[jax.ShapeDtypeStruct(t.shape, t.dtype) for t in arrs],
        scratch_shapes=[pltpu.SemaphoreType.DMA((n, 3)), pltpu.SemaphoreType.DMA((n, 3))],
    )(*arrs)


def pair_gather(bufs, *, name):
    n = len(bufs)

    def body(*refs):
        outs = refs[n:2 * n]
        send_sems, recv_sems = refs[2 * n:]
        x, y, c, _ = _place()
        copies = []
        for a in range(n):
            cp = pltpu.make_async_remote_copy(
                src_ref=outs[a].at[c], dst_ref=outs[a].at[c], send_sem=send_sems.at[a],
                recv_sem=recv_sems.at[a], device_id=(x, y, 1 - c), device_id_type=MESH)
            cp.start()
            copies.append(cp)
        for cp in copies:
            cp.wait()

    return pl.pallas_call(
        body, name=name,
        in_specs=[_ANY] * n, out_specs=[_ANY] * n,
        out_shape=[jax.ShapeDtypeStruct(t.shape, t.dtype) for t in bufs],
        input_output_aliases={a: a for a in range(n)},
        scratch_shapes=[pltpu.SemaphoreType.DMA((n,)), pltpu.SemaphoreType.DMA((n,))],
    )(*bufs)


def all_sum_small(v, *, name):
    m = v.shape[0]

    def body(v_ref, o_ref, buf, send_sems, recv_sems, local_sem):
        x, y, c, chips = _place()
        me, sibling = (x, y, c), (x, y, 1 - c)

        def rows(px, py, pc):
            return buf.at[pl.ds((4 * px + 2 * py + pc) * m, m), :]

        def copy(k, block, to, src=None):
            return pltpu.make_async_remote_copy(
                src_ref=rows(*block) if src is None else src, dst_ref=rows(*block),
                send_sem=send_sems.at[k], recv_sem=recv_sems.at[k], device_id=to, device_id_type=MESH)

        mine = pltpu.make_async_copy(v_ref, rows(*me), local_sem)
        mine.start()
        first = [copy(0, me, sibling, src=v_ref)]
        first += [copy(1 + j, me, (*chip, c), src=v_ref) for j, chip in enumerate(chips)]
        for cp in first:
            cp.start()
        passed = [copy(4 + j, (*chip, c), sibling) for j, chip in enumerate(chips)]
        for j, chip in enumerate(chips):
            copy(1 + j, (*chip, c), me).wait_recv()
            passed[j].start()
        copy(0, sibling, me).wait_recv()
        for j, chip in enumerate(chips):
            copy(4 + j, (*chip, 1 - c), me).wait_recv()
        for cp in first + passed:
            cp.wait_send()
        mine.wait()
        acc = buf[pl.ds(0, m), :]
        for k in range(1, 8):
            acc = acc + buf[pl.ds(k * m, m), :]
        o_ref[...] = acc

    vm = pl.BlockSpec(memory_space=pltpu.VMEM)
    return pl.pallas_call(
        body, name=name, in_specs=[vm], out_specs=vm,
        out_shape=jax.ShapeDtypeStruct((m, 128), F32),
        scratch_shapes=[pltpu.VMEM((8 * m, 128), F32), pltpu.SemaphoreType.DMA((7,)),
                        pltpu.SemaphoreType.DMA((7,)), pltpu.SemaphoreType.DMA],
    )(v)


def sum_halves(arr, recv, c_idx, *, name):
    _, _, r, c = arr.shape
    tr = r
    while tr * c * 4 > (1 << 20) and tr % 32 == 0:
        tr //= 2

    def body(c_ref, a_ref, b_ref, o_ref):
        o_ref[...] = (a_ref[...] + b_ref[...]).astype(BF16)

    return pl.pallas_call(
        body, name=name,
        grid_spec=pltpu.PrefetchScalarGridSpec(
            num_scalar_prefetch=1, grid=(N_SHARD, r // tr),
            in_specs=[pl.BlockSpec((None, None, tr, c), lambda j, i, cr: (j, cr[0], i, 0)),
                      pl.BlockSpec((None, tr, c), lambda j, i, cr: (j, i, 0))],
            out_specs=pl.BlockSpec((None, tr, c), lambda j, i, cr: (j, i, 0))),
        out_shape=jax.ShapeDtypeStruct((N_SHARD, r, c), BF16),
        compiler_params=_params("parallel", "parallel"),
    )(c_idx, arr, recv)


def sum_chips(recv, part, place, *, name):
    _, r, c = recv.shape
    tr = r
    while tr * c * 4 > (1 << 20) and tr % 32 == 0:
        tr //= 2

    def body(place_ref, a0, a1, a2, a3, p_ref, o_ref):
        me = place_ref[1]
        acc = None
        for k, a_ref in enumerate((a0, a1, a2, a3)):
            term = jnp.where(me == k, p_ref[...], a_ref[...]).astype(F32)
            acc = term if acc is None else acc + term
        o_ref[...] = acc

    def slab(k):
        return pl.BlockSpec((None, tr, c), lambda i, pr: (jnp.where(pr[1] == k, (k + 1) % N_SHARD, k), i, 0))

    return pl.pallas_call(
        body, name=name,
        grid_spec=pltpu.PrefetchScalarGridSpec(
            num_scalar_prefetch=1, grid=(r // tr,),
            in_specs=[slab(k) for k in range(N_SHARD)]
            + [pl.BlockSpec((None, tr, c), lambda i, pr: (pr[1], i, 0))],
            out_specs=pl.BlockSpec((None, tr, c), lambda i, pr: (pr[0], i, 0))),
        out_shape=jax.ShapeDtypeStruct((2, r, c), F32),
        compiler_params=_params("parallel"),
    )(place, recv, recv, recv, recv, part)


_WEIGHTS = ("ff1_norm", "ff1_wg", "ff1_wu", "ff1_wd", "mix_norm", "w_in", "sgu_ln_g", "sgu_ln_b", "sgu_w",
            "sgu_b", "lru_conv_w", "lru_conv_b", "lru_wa", "lru_ba", "lru_wx", "lru_bx", "lru_lambda",
            "gdn_conv_w", "gdn_a_log", "gdn_dt_bias", "gdn_norm_g", "pool_w", "pool_scale", "w_branch",
            "w_out", "ff2_norm", "ff2_wg", "ff2_wu", "ff2_wd", "final_norm")
_BIG = ("ff1_wg", "ff1_wu", "ff1_wd", "w_in", "w_branch", "w_out", "ff2_wg", "ff2_wu", "ff2_wd")
_SMALL = tuple(n for n in _WEIGHTS if n not in _BIG)
_CONV = ("lru_conv_w", "gdn_conv_w")


def _seg_rows(shape):
    return -(-math.prod(shape) // 1024) * 8


def _pack(arrs):
    segs = []
    for t in arrs:
        rows = _seg_rows(t.shape)
        flat = t.reshape(-1)
        segs.append(jnp.pad(flat, (0, rows * 128 - flat.shape[0])).reshape(rows, 128))
    return jnp.concatenate(segs, axis=0)


def _unpack(buf, shapes):
    out, o = [], 0
    for shp in shapes:
        rows = _seg_rows(shp)
        out.append(buf[o:o + rows].reshape(-1)[:math.prod(shp)].reshape(shp))
        o += rows
    return out


def _block_diag(w):
    h, n, _ = w.shape
    same = jnp.arange(h)[:, None, None, None] == jnp.arange(h)[None, None, :, None]
    return jnp.where(same, w[:, :, None, :], 0.0).reshape(h * n, h * n)


def _diag_blocks(m, h=8, n=64):
    return jnp.stack([m[i * n:(i + 1) * n, i * n:(i + 1) * n] for i in range(h)])


def _lane_row(v):
    return jnp.zeros((1, 128), F32).at[0, 4:8].set(v)


def _w_in_to_padded(raw):
    sem = jnp.concatenate([raw[k, :, :P_IN_SHARD] for k in range(N_SHARD)], axis=1)
    zero = jnp.zeros((sem.shape[0], C_GATE - C_BA - 8), sem.dtype)
    return jnp.concatenate([sem[:, 2048:4096], sem[:, 0:2048], sem[:, 4104:4616], sem[:, 4096:4104], zero,
                            sem[:, 4616:P_IN]], axis=1)


def _w_in_grad_to_shards(part):
    r = part.shape[1]
    dwp = part.transpose(1, 0, 2).reshape(r, P_PAD)
    sem = jnp.concatenate([dwp[:, C_AU:C_DX], dwp[:, 0:C_AU], dwp[:, C_BA:C_BA + 8], dwp[:, C_DX:C_BA],
                           dwp[:, C_GATE:]], axis=1)
    sh = sem.reshape(r, N_SHARD, P_IN_SHARD).transpose(1, 0, 2)
    return jnp.pad(sh, ((0, 0), (0, 0), (0, P_IN_SHARD_PAD - P_IN_SHARD)))


def kernel(x, ff1_norm, ff1_wg, ff1_wu, ff1_wd, mix_norm, w_in, sgu_ln_g, sgu_ln_b, sgu_w, sgu_b, lru_conv_w,
           lru_conv_b, lru_wa, lru_ba, lru_wx, lru_bx, lru_lambda, gdn_conv_w, gdn_a_log, gdn_dt_bias, gdn_norm_g,
           pool_w, pool_scale, w_branch, w_out, ff2_norm, ff2_wg, ff2_wu, ff2_wd, final_norm, loss_target, m_ff1_norm,
           m_ff1_wg, m_ff1_wu, m_ff1_wd, m_mix_norm, m_w_in, m_sgu_ln_g, m_sgu_ln_b, m_sgu_w, m_sgu_b, m_lru_conv_w,
           m_lru_conv_b, m_lru_wa, m_lru_ba, m_lru_wx, m_lru_bx, m_lru_lambda, m_gdn_conv_w, m_gdn_a_log, m_gdn_dt_bias,
           m_gdn_norm_g, m_pool_w, m_pool_scale, m_w_branch, m_w_out, m_ff2_norm, m_ff2_wg, m_ff2_wu, m_ff2_wd,
           m_final_norm, v_ff1_norm, v_ff1_wg, v_ff1_wu, v_ff1_wd, v_mix_norm, v_w_in, v_sgu_ln_g, v_sgu_ln_b, v_sgu_w,
           v_sgu_b, v_lru_conv_w, v_lru_conv_b, v_lru_wa, v_lru_ba, v_lru_wx, v_lru_bx, v_lru_lambda, v_gdn_conv_w,
           v_gdn_a_log, v_gdn_dt_bias, v_gdn_norm_g, v_pool_w, v_pool_scale, v_w_branch, v_w_out, v_ff2_norm, v_ff2_wg,
           v_ff2_wu, v_ff2_wd, v_final_norm):
    a = dict(locals())
    n_layer = ff1_norm.shape[0]
    d = D_MODEL
    x0 = x[0]
    chip = 2 * lax.axis_index("x") + lax.axis_index("y")
    c_idx = jnp.reshape(lax.axis_index("c"), (1,)).astype(jnp.int32)
    place = jnp.stack([lax.axis_index("c"), chip]).astype(jnp.int32)

    def own_slab(t, dtype=BF16):
        buf = lax.empty((N_SHARD,) + t.shape, dtype)
        return lax.dynamic_update_slice(buf, t.astype(dtype)[None], (chip,) + (0,) * t.ndim)

    def halves(t):
        return t.reshape((2, t.shape[0] // 2) + t.shape[1:])

    assert n_layer == 2
    w_in_loc = jnp.pad(w_in, ((0, 0), (0, 0), (0, P_IN_SHARD_PAD - P_IN_SHARD)))

    def layer_bufs(l):
        wgu = jnp.stack([ff1_wg[l], ff1_wu[l], ff2_wg[l], ff2_wu[l]]).reshape(2, 2, d, FF_SHARD)
        wd = jnp.stack([ff1_wd[l], ff2_wd[l]])
        return [own_slab(wgu), own_slab(wd), own_slab(halves(w_in_loc[l])), own_slab(halves(w_branch[l])),
                own_slab(halves(w_out[l]))]

    bufs = [layer_bufs(l) for l in range(n_layer)]
    *gathered0, g_lcw, g_gcw = gather_shards(bufs[0] + [own_slab(lru_conv_w, F32), own_slab(gdn_conv_w, F32)],
                                             name="gather_weights_0")
    first, bufs1 = lax.optimization_barrier((gathered0[0], bufs[1]))
    gathered = [[first] + gathered0[1:], None]
    in_flight, gather_sems, gather_token = gather_start(bufs1, name="gather_weights_1_start")

    def layer_params(l):
        g_wgu, g_wd, g_w_in, g_wb, g_wo = gathered[l]
        g_wgu = g_wgu.reshape(N_SHARD, 4, d, FF_SHARD)
        p = {}
        p["ff1"] = (g_wgu, g_wgu, g_wd, (0, 1, 0))
        p["ff2"] = (g_wgu, g_wgu, g_wd, (2, 3, 1))
        p["w_in"] = _w_in_to_padded(g_w_in.reshape(4, d, P_IN_SHARD_PAD))
        p["wb"] = g_wb.reshape(4, 4, BRANCH_W, d // 4).transpose(1, 2, 0, 3).reshape(4, BRANCH_W, d)
        p["wo"] = g_wo.reshape(d, d)
        p["lcw"] = g_lcw[:, l].transpose(1, 0, 2).reshape(4, BRANCH_W)
        p["gcw"] = g_gcw[:, l].transpose(1, 0, 2).reshape(4, 3 * BRANCH_W)
        p["lcb"] = lru_conv_b[l][None]
        p["dep1"] = gather_token if l == 0 else None
        p["bt"] = jnp.zeros((128, 128), F32).at[:, :4].set(sgu_b[l].T)
        p["wa"] = _block_diag(lru_wa[l])
        p["wx"] = _block_diag(lru_wx[l])
        p["alog"] = _lane_row(gdn_a_log[l])
        p["dt"] = _lane_row(gdn_dt_bias[l])
        return p

    saved = []
    xs = x0
    for l in range(n_layer):
        p = layer_params(l)
        sv = {"p": p, "x0": xs}
        x1 = ffn_fwd(xs, ff1_norm[l][None], *p["ff1"], tb=512, dep=p["dep1"], name=f"ffn1_fwd_{l}")
        h = norm_fwd(x1, mix_norm[l][None], tb=512, name=f"mix_norm_fwd_{l}")
        proj = matmul(h, p["w_in"], tm=1024, tn=1536, tk=1024, name=f"proj_{l}")
        ya = sgu_fwd(proj, sgu_ln_g[l][None], sgu_ln_b[l][None], sgu_w[l], p["bt"], tb=512, name=f"sgu_fwd_{l}")
        xc = conv_fwd(proj, C_BX, BRANCH_W, p["lcw"], p["lcb"], tb=512, name=f"lru_conv_fwd_{l}")
        yb, hl = lru_fwd(xc, proj, p["wa"], p["wx"], lru_ba[l][None], lru_bx[l][None], lru_lambda[l][None],
                         tb=256, name=f"lru_fwd_{l}")
        qkv = conv_fwd(proj, C_QKV, 3 * BRANCH_W, p["gcw"], jnp.zeros((1, 3 * BRANCH_W), F32), tb=512,
                       name=f"gdn_conv_fwd_{l}")
        yc, sall = gdn_fwd(qkv, proj, p["alog"], p["dt"], gdn_norm_g[l][None], cps=GDN_CPS, name=f"gdn_fwd_{l}")
        dep2 = None
        if l == 0:
            in_flight, gather_sems, dep2 = gather_pass(in_flight, gather_sems, yc, name="gather_weights_1_pass")
        yd = pool_fwd(proj, pool_w[l], pool_scale[l][None], tb=512, name=f"pool_fwd_{l}")
        ys = (ya, yb, yc, yd)
        x2, merged = merge_fwd(x1, ys, proj, p["wb"], p["wo"], tb=256, dep=dep2, name=f"merge_fwd_{l}")
        x3 = ffn_fwd(x2, ff2_norm[l][None], *p["ff2"], tb=512, name=f"ffn2_fwd_{l}")
        if l == 0:
            gathered[1] = gather_wait(in_flight, gather_sems, x3, name="gather_weights_1_wait")
        sv.update(x1=x1, h=h, proj=proj, xc=xc, hl=hl, qkv=qkv, sall=sall, ys=ys, merged=merged, x2=x2)
        saved.append(sv)
        xs = x3

    dx, d_final, loss_blk = loss_head(xs, final_norm[None], loss_target[0], tb=512, name="loss_head")
    loss = lax.psum(loss_blk[0, 0], ("x", "y", "c"))

    small_g = {n: [None] * n_layer for n in _SMALL if n != "final_norm"}
    big_g = [None] * n_layer
    per_layer = 7
    flight = None

    def pair_sums(bigs, recv, tag, w_in_at):
        part = [sum_halves(t, r, c_idx, name=f"grad_pair_sum_{tag}_{i}") for i, (t, r) in enumerate(zip(bigs, recv))]
        if w_in_at is not None:
            part[w_in_at] = _w_in_grad_to_shards(part[w_in_at])
        return part

    def pair_start(arrs, tag):
        return copies_start(arrs, [jax.ShapeDtypeStruct((N_SHARD,) + t.shape[2:], t.dtype) for t in arrs],
                            _pair_plan, N_SHARD, name=f"grad_pair_exchange_{tag}_start")

    def pair_to_chip(fl, after, tag, w_in_at=None):
        sent, recv = copies_wait(*fl[:3], after, _pair_plan, name=f"grad_pair_exchange_{tag}_wait")
        part = pair_sums(sent, recv, tag, w_in_at)
        return copies_start(part, [jax.ShapeDtypeStruct(t.shape, t.dtype) for t in part], _chip_plan, 3,
                            name=f"grad_chip_exchange_{tag}_start")

    def chip_finish(fl, after, tag):
        part, recv = copies_wait(*fl[:3], after, _chip_plan, name=f"grad_chip_exchange_{tag}_wait")
        return [sum_chips(r, q, place, name=f"grad_chip_sum_{tag}_{i}") for i, (r, q) in enumerate(zip(recv, part))]

    early = mid = None

    for l in reversed(range(n_layer)):
        sv = saved[l]
        p = sv["p"]
        proj = sv["proj"]
        dx2, dg_ff2, dab2, hid2, h2, dy2 = ffn_bwd(sv["x2"], dx, ff2_norm[l][None], *p["ff2"], tb=512,
                                                  dep=None if flight is None else flight[3],
                                                  name=f"ffn2_bwd_{l}")
        dwgu2 = matmul(h2, dab2, ta=True, tm=1024, tn=FF_SHARD, tk=TK_DW, name=f"ffn2_dwgu_{l}")
        dwd2 = matmul(hid2, dy2, ta=True, tm=FF_SHARD, tn=1024, tk=TK_DW, name=f"ffn2_dwd_{l}")
        if flight is not None:
            flight = pair_to_chip(flight, dx2, "1", w_in_at=2)
        dgp, dbr, dya, dyb, dyc, dyd, dx2b = merge_bwd(dx2, sv["ys"], proj, p["wb"], p["wo"], tb=256,
                                                       dep=None if flight is None else flight[3],
                                                       name=f"merge_bwd_{l}")
        dwo = matmul(sv["merged"], dx2b, ta=True, tm=1024, tn=1024, tk=TK_DW, name=f"dwo_{l}")
        dwb = jnp.stack([matmul(sv["ys"][g], dbr[g], ta=True, tm=BRANCH_W, tn=1024, tk=TK_DW, name=f"dwb{g}_{l}")
                         for g in range(4)])
        dwb_sh = dwb.reshape(4, BRANCH_W, N_SHARD, d // N_SHARD).transpose(2, 0, 1, 3)
        g_late = [dwgu2, dwd2.reshape(N_SHARD, 2, FF_SHARD // 2, d),
                  dwb_sh.reshape(N_SHARD, 2, 2 * BRANCH_W, d // N_SHARD), dwo.reshape(N_SHARD, 2, d // 8, d)]
        tok = None
        if l == 0:
            early = pair_start(g_late, "0a")
            tok = early[3]
        du, dv, dlg, dlb, dws, dbt = sgu_bwd(proj, dya, sgu_ln_g[l][None], sgu_ln_b[l][None], sgu_w[l], p["bt"],
                                             tb=512, dep=tok, name=f"sgu_bwd_{l}")
        dxc, dgate, dwa, dwx, dba, dbx, dlam = lru_bwd(sv["xc"], proj, sv["hl"], dyb, p["wa"], p["wx"],
                                                       lru_ba[l][None], lru_bx[l][None], lru_lambda[l][None],
                                                       tb=256, name=f"lru_bwd_{l}")
        dbx_in, dcw_l = conv_bwd(proj, C_BX, BRANCH_W, dxc, p["lcw"], tb=512, name=f"lru_conv_bwd_{l}")
        dqkv, dz, dbeta, dal, ddt, dng = gdn_bwd(sv["qkv"], proj, sv["sall"], dyc, p["alog"], p["dt"],
                                                 gdn_norm_g[l][None], cps=GDN_CPS, name=f"gdn_bwd_{l}")
        dqkv_in, dcw_g = conv_bwd(proj, C_QKV, 3 * BRANCH_W, dqkv, p["gcw"], tb=512, name=f"gdn_conv_bwd_{l}")
        ddx, dwp, dsc = pool_bwd(proj, dyd, pool_w[l], pool_scale[l][None], tb=512, name=f"pool_bwd_{l}")
        if l == 0:
            early = pair_to_chip(early, ddx, "0a")
            tok = early[3]
        s = dx.shape[0]
        dproj = jnp.concatenate([dqkv_in, dz, du, dv, dbx_in, dgate, ddx, dbeta,
                                 jnp.zeros((s, C_GATE - C_BA - 128), BF16), dgp], axis=1)
        dh = matmul(dproj, p["w_in"], tb=True, tm=1024, tn=1024, tk=1536, dep=tok, name=f"dh_mix_{l}")
        dw_in = matmul(sv["h"], dproj, ta=True, tm=1024, tn=1152, tk=TK_DW, col_groups=N_SHARD, name=f"dw_in_{l}")
        dw_in = dw_in.reshape(N_SHARD, 2, d // 2, P_PAD // N_SHARD)
        if l == 0:
            mid = pair_start([dw_in], "0b")
            tok = mid[3]
        dx1, dg_mix = norm_bwd(sv["x1"], mix_norm[l][None], dh, dx2, tb=512, dep=tok, name=f"mix_norm_bwd_{l}")
        dx, dg_ff1, dab1, hid1, h1, dy1 = ffn_bwd(sv["x0"], dx1, ff1_norm[l][None], *p["ff1"],
                                                 tb=512, name=f"ffn1_bwd_{l}")
        if l == 0:
            mid = pair_to_chip(mid, dx, "0b", w_in_at=0)
            tok = mid[3]
        dwgu1 = matmul(h1, dab1, ta=True, tm=1024, tn=FF_SHARD, tk=TK_DW, dep=tok, name=f"ffn1_dwgu_{l}")
        dwd1 = matmul(hid1, dy1, ta=True, tm=FF_SHARD, tn=1024, tk=TK_DW, name=f"ffn1_dwd_{l}")

        for n, g in (("ff1_norm", dg_ff1[0]), ("mix_norm", dg_mix[0]), ("sgu_ln_g", dlg[0]), ("sgu_ln_b", dlb[0]),
                     ("sgu_w", dws), ("sgu_b", dbt[:, :4].T), ("lru_conv_w", dcw_l[:4]), ("lru_conv_b", dcw_l[4]),
                     ("lru_wa", _diag_blocks(dwa)), ("lru_ba", dba[0]), ("lru_wx", _diag_blocks(dwx)),
                     ("lru_bx", dbx[0]), ("lru_lambda", dlam[0]), ("gdn_conv_w", dcw_g[:4]),
                     ("gdn_a_log", dal[0, 4:8]), ("gdn_dt_bias", ddt[0, 4:8]), ("gdn_norm_g", dng[0]),
                     ("pool_w", dwp), ("pool_scale", dsc[0]), ("ff2_norm", dg_ff2[0])):
            small_g[n][l] = g
        big_g[l] = [dwgu1, dwd1.reshape(N_SHARD, 2, FF_SHARD // 2, d), dw_in, g_late[2], g_late[3], g_late[0], g_late[1]]
        if l == n_layer - 1:
            flight = pair_start(big_g[l], "1")

    last = big_g[0][:2]
    recv = pair_send_halves(last, name="grad_pair_exchange_0c")
    part = pair_sums(last, recv, "0c", None)
    recv = chip_scatter(part, name="grad_chip_exchange_0c")
    red_c = [sum_chips(r, q, place, name=f"grad_chip_sum_0c_{i}") for i, (r, q) in enumerate(zip(recv, part))]
    red1 = chip_finish(flight, red_c[0], "1")
    red_a = chip_finish(early, red_c[0], "0a")
    red_b = chip_finish(mid, red_c[0], "0b")
    full = pair_gather(red_c + red_b + [red_a[2], red_a[3], red_a[0], red_a[1]] + red1, name="grad_pair_gather")
    grads = {}

    def layers(k, f):
        return jnp.stack([f(full[l * per_layer + k]) for l in range(n_layer)])

    grads["ff1_wg"] = layers(0, lambda t: t[0])
    grads["ff1_wu"] = layers(0, lambda t: t[1])
    grads["ff1_wd"] = layers(1, lambda t: t.reshape(FF_SHARD, d))
    grads["w_in"] = layers(2, lambda t: t.reshape(d, P_IN_SHARD_PAD)[:, :P_IN_SHARD])
    grads["w_branch"] = layers(3, lambda t: t.reshape(4, BRANCH_W, d // N_SHARD))
    grads["w_out"] = layers(4, lambda t: t.reshape(d // N_SHARD, d))
    grads["ff2_wg"] = layers(5, lambda t: t[0])
    grads["ff2_wu"] = layers(5, lambda t: t[1])
    grads["ff2_wd"] = layers(6, lambda t: t.reshape(FF_SHARD, d))

    small_full = [jnp.stack(small_g[n]) for n in _SMALL if n != "final_norm"] + [d_final[0]]
    summed = _unpack(all_sum_small(_pack(small_full), name="grad_small_allreduce"), [t.shape for t in small_full])
    for n, g in zip(_SMALL, summed):
        if n in _CONV:
            width = a[n].shape[-1]
            g = lax.dynamic_slice_in_dim(g, chip * width, width, axis=2)
        grads[n] = g

    delta, new_m, new_v = {}, {}, {}
    for n in _BIG:
        delta[n], new_m[n], new_v[n] = adamw(a[n], grads[n], a["m_" + n], a["v_" + n], name=f"adamw_{n}")
    shapes = [a[n].shape for n in _SMALL]
    packed = [_pack([src[pre + n] for n in _SMALL]) for src, pre in ((a, ""), (grads, ""), (a, "m_"), (a, "v_"))]
    for store, buf in zip((delta, new_m, new_v), adamw(*packed, name="adamw_small")):
        store.update(zip(_SMALL, _unpack(buf, shapes)))

    return (loss, dx[None], *[grads[n] for n in _WEIGHTS], *[delta[n] for n in _WEIGHTS],
            *[new_m[n] for n in _WEIGHTS], *[new_v[n] for n in _WEIGHTS])
```

```python
import functools
import math

import jax
import jax.numpy as jnp
from jax import lax
from jax.experimental import pallas as pl
from jax.experimental.pallas import tpu as pltpu

F32 = jnp.float32
BF16 = jnp.bfloat16
MESH = pl.DeviceIdType.MESH

D_MODEL = 1024
N_SHARD = 4
D_FF = 2816
FF_SHARD = D_FF // N_SHARD
BRANCH_W = 512
CHUNK = 64
GDN_HEADS = 4
GDN_DK = 128
GDN_CPS = 4
TK_DW = 2048
LRU_C = 8.0
EPS = 1e-6
P_IN = 8712
P_IN_SHARD = P_IN // N_SHARD
P_IN_SHARD_PAD = 2304
P_PAD = 9216
C_QKV, C_Z, C_AU, C_AV, C_BX, C_BG, C_DX, C_BA, C_GATE = 0, 1536, 2048, 2560, 3072, 3584, 4096, 4608, 5120

ADAM_LR, ADAM_B1, ADAM_B2, ADAM_EPS, ADAM_WD, ADAM_STEP = 0.001, 0.9, 0.999, 1e-08, 0.01, 10


def _gelu(x):
    return 0.5 * x * (1.0 + jnp.tanh(0.7978845608028654 * (x + 0.044715 * (x * x * x))))


def _gelu_grad(x):
    u = 0.7978845608028654 * (x + 0.044715 * (x * x * x))
    t = jnp.tanh(u)
    return 0.5 * (1.0 + t) + 0.5 * x * (1.0 - t * t) * 0.7978845608028654 * (1.0 + 3.0 * 0.044715 * x * x)


def _silu(x):
    return x * jax.nn.sigmoid(x)


def _softplus(x):
    return jnp.maximum(x, 0.0) + jnp.log1p(jnp.exp(-jnp.abs(x)))


def _dg(a, b, ca, cb):
    return lax.dot_general(a.astype(BF16), b.astype(BF16), (((ca,), (cb,)), ((), ())),
                           preferred_element_type=F32)


def _dg3(a, b, ca, cb):
    a1 = a.astype(BF16)
    a2 = (a - a1.astype(F32)).astype(BF16)
    b1 = b.astype(BF16)
    b2 = (b - b1.astype(F32)).astype(BF16)
    dn = (((ca,), (cb,)), ((), ()))
    d = functools.partial(lax.dot_general, dimension_numbers=dn, preferred_element_type=F32)
    return d(a1, b1) + (d(a1, b2) + d(a2, b1))


def _make_mm(ca, cb, dot):
    @jax.custom_vjp
    def f(a, b):
        return dot(a, b, ca, cb)

    def fwd(a, b):
        return dot(a, b, ca, cb), (a, b)

    def bwd(res, g):
        a, b = res
        if (ca, cb) == (1, 0):
            return dot(g, b, 1, 1), dot(a, g, 0, 0)
        if (ca, cb) == (1, 1):
            return dot(g, b, 1, 0), dot(g, a, 0, 0)
        return dot(b, g, 1, 1), dot(a, g, 1, 0)

    f.defvjp(fwd, bwd)
    return f


mm_nn = _make_mm(1, 0, _dg)


def _bdg(a, b, ca, cb):
    return lax.dot_general(a.astype(BF16), b.astype(BF16), (((ca,), (cb,)), ((0,), (0,))),
                           preferred_element_type=F32)


def _split2(x):
    x1 = x.astype(BF16)
    return x1, (x - x1.astype(F32)).astype(BF16)


def _bdg3(a, b, ca, cb):
    a1, a2 = _split2(a)
    b1, b2 = _split2(b)
    d = functools.partial(lax.dot_general, dimension_numbers=(((ca,), (cb,)), ((0,), (0,))),
                          preferred_element_type=F32)
    return d(a1, b1) + (d(a1, b2) + d(a2, b1))


def _make_bmm(ca, cb):
    @jax.custom_vjp
    def f(a, b):
        return _bdg(a, b, ca, cb)

    def fwd(a, b):
        return _bdg(a, b, ca, cb), (a, b)

    def bwd(res, g):
        a, b = res
        if (ca, cb) == (2, 1):
            return _bdg(g, b, 2, 2), _bdg(a, g, 1, 1)
        if (ca, cb) == (2, 2):
            return _bdg(g, b, 2, 1), _bdg(g, a, 1, 1)
        return _bdg(b, g, 2, 2), _bdg(a, g, 2, 1)

    f.defvjp(fwd, bwd)
    return f


bmm_nn = _make_bmm(2, 1)
bmm_nt = _make_bmm(2, 2)
bmm_tn = _make_bmm(1, 1)


def _iota3(shape, dim):
    return lax.broadcasted_iota(jnp.int32, shape, dim)


@jax.custom_vjp
def _tri_inv(aw):
    b, n, _ = aw.shape
    shp = (b, n, 2 * n)
    left = _iota3(shp, 2) < n
    x = jnp.where(_iota3(shp, 1) == _iota3(shp, 2), 1.0, 0.0) - aw
    zero = jnp.zeros((b, n, 4 * n), BF16)
    k = 1
    while k < n:
        x1, x2 = _split2(x)
        lhs = jnp.concatenate([x1, x2], axis=1)
        rhs = jnp.concatenate([zero, jnp.concatenate([x1, x2], axis=2)], axis=1)
        r = lax.dot_general(lhs, rhs, (((2,), (1,)), ((0,), (0,))), preferred_element_type=F32)
        mx = (r[:, :n, :2 * n] + r[:, :n, 2 * n:]) + (r[:, n:, :2 * n] + r[:, n:, 2 * n:])
        x = jnp.where(left, x, 0.0) + mx
        k *= 2
    return jnp.where(left, x, 0.0)


def _tri_inv_fwd(aw):
    t = _tri_inv(aw)
    return t, t


def _tri_inv_bwd(t, dt):
    n = t.shape[1]
    x = _bdg3(t, dt, 1, 1)[:, :n]
    tp = jnp.concatenate([jnp.zeros_like(t), t], axis=1)
    return (-_bdg3(x, tp, 2, 2),)


_tri_inv.defvjp(_tri_inv_fwd, _tri_inv_bwd)


def _rows(shape):
    return lax.broadcasted_iota(jnp.int32, shape, 0)


def _lanes(shape):
    return lax.broadcasted_iota(jnp.int32, shape, 1)


def _col(x, j):
    return jnp.sum(jnp.where(_lanes(x.shape) == j, x, 0.0), axis=1, keepdims=True)


def _put_col(col, j, width):
    shape = (col.shape[0], width)
    return jnp.where(_lanes(shape) == j, jnp.broadcast_to(col, shape), 0.0)


VMEM_BYTES_V7X = 64 << 20


def _params(*sem, vmem_bytes=None):
    return pltpu.CompilerParams(dimension_semantics=sem, vmem_limit_bytes=vmem_bytes)


def matmul(a, b, *, ta=False, tb=False, tm, tn, tk, out_dtype=F32, col_groups=1, dep=None, name):
    ba, bb = a.shape[:-2], b.shape[:-2]
    batch = ba if len(ba) >= len(bb) else bb
    assert ba in ((), batch) and bb in ((), batch)
    nb = len(batch)
    (m, k) = (a.shape[-1], a.shape[-2]) if ta else (a.shape[-2], a.shape[-1])
    (k2, n) = (b.shape[-1], b.shape[-2]) if tb else (b.shape[-2], b.shape[-1])
    tm, tn, tk = min(tm, m), min(tn, n), min(tk, k)
    assert k == k2 and m % tm == 0 and n % tn == 0 and k % tk == 0, (a.shape, b.shape, tm, tn, tk)
    nk = k // tk
    grid = batch + (m // tm, n // tn, nk)

    def a_map(*g):
        i, kk = g[nb], g[nb + 2]
        return (g[:nb] if ba else ()) + ((kk, i) if ta else (i, kk))

    def b_map(*g):
        j, kk = g[nb + 1], g[nb + 2]
        return (g[:nb] if bb else ()) + ((j, kk) if tb else (kk, j))

    per_group = n // tn // col_groups
    assert col_groups == 1 or (nb == 0 and per_group * col_groups * tn == n)

    def o_map(*g):
        if col_groups > 1:
            return (g[1] // per_group, g[0], g[1] % per_group)
        return g[:nb] + (g[nb], g[nb + 1])

    a_blk = (None,) * len(ba) + ((tk, tm) if ta else (tm, tk))
    b_blk = (None,) * len(bb) + ((tn, tk) if tb else (tk, tn))
    o_blk = (None,) * (nb + (col_groups > 1)) + (tm, tn)
    o_shape = (col_groups, m, n // col_groups) if col_groups > 1 else batch + (m, n)

    dep_specs, dep_args = _dep_operand(dep)

    def body(a_ref, b_ref, *rest):
        o_ref, acc_ref = rest[len(dep_args):]
        kk = pl.program_id(nb + 2)

        @pl.when(kk == 0)
        def _():
            acc_ref[...] = jnp.zeros_like(acc_ref)

        acc_ref[...] += _dg(a_ref[...], b_ref[...], 0 if ta else 1, 1 if tb else 0)

        @pl.when(kk == nk - 1)
        def _():
            o_ref[...] = acc_ref[...].astype(o_ref.dtype)

    return pl.pallas_call(
        body, name=name, grid=grid,
        in_specs=[pl.BlockSpec(a_blk, a_map), pl.BlockSpec(b_blk, b_map)] + dep_specs,
        out_specs=pl.BlockSpec(o_blk, o_map),
        out_shape=jax.ShapeDtypeStruct(o_shape, out_dtype),
        scratch_shapes=[pltpu.VMEM((tm, tn), F32)],
        compiler_params=_params(*(("parallel",) * (nb + 2) + ("arbitrary",))),
    )(a, b, *dep_args)


def norm_fwd(x, g, *, tb, name):
    s, d = x.shape

    def body(x_ref, g_ref, h_ref):
        xv = x_ref[...]
        r = lax.rsqrt(jnp.mean(xv * xv, axis=-1, keepdims=True) + EPS)
        h_ref[...] = (xv * r * g_ref[...]).astype(BF16)

    return pl.pallas_call(
        body, name=name, grid=(s // tb,),
        in_specs=[pl.BlockSpec((tb, d), lambda i: (i, 0)), pl.BlockSpec((1, d), lambda i: (0, 0))],
        out_specs=pl.BlockSpec((tb, d), lambda i: (i, 0)),
        out_shape=jax.ShapeDtypeStruct((s, d), BF16),
        compiler_params=_params("parallel"),
    )(x, g)


def _rms_bwd(xv, gv, dh):
    r = lax.rsqrt(jnp.mean(xv * xv, axis=-1, keepdims=True) + EPS)
    xh = xv * r
    dxh = dh * gv
    dx = r * (dxh - xh * jnp.mean(dxh * xh, axis=-1, keepdims=True))
    return dx, jnp.sum(dh * xh, axis=0, keepdims=True)


def norm_bwd(x, g, dh, dres, *, tb, dep=None, name):
    s, d = x.shape
    dep_specs, dep_args = _dep_operand(dep)

    def body(x_ref, g_ref, dh_ref, dres_ref, *rest):
        dx_ref, dg_ref = rest[len(dep_args):]
        dx, dg = _rms_bwd(x_ref[...], g_ref[...], dh_ref[...])
        dx_ref[...] = dres_ref[...] + dx

        @pl.when(pl.program_id(0) == 0)
        def _():
            dg_ref[...] = jnp.zeros_like(dg_ref)

        dg_ref[...] += dg

    row = pl.BlockSpec((tb, d), lambda i: (i, 0))
    one = pl.BlockSpec((1, d), lambda i: (0, 0))
    return pl.pallas_call(
        body, name=name, grid=(s // tb,),
        in_specs=[row, one, row, row] + dep_specs, out_specs=[row, one],
        out_shape=[jax.ShapeDtypeStruct((s, d), F32), jax.ShapeDtypeStruct((1, d), F32)],
        compiler_params=_params("arbitrary"),
    )(x, g, dh, dres, *dep_args)


def _dep_operand(dep):
    return ([], []) if dep is None else ([_ANY], [dep])


def ffn_fwd(x, g, wg, wu, wd, layer, *, tb, dep=None, name):
    s, d = x.shape
    fc = wd.shape[-2]
    dep_specs, dep_args = _dep_operand(dep)

    def body(x_ref, g_ref, wg_ref, wu_ref, wd_ref, *rest):
        o_ref, h_scr, acc_scr = rest[len(dep_args):]
        j = pl.program_id(1)

        @pl.when(j == 0)
        def _():
            xv = x_ref[...]
            r = lax.rsqrt(jnp.mean(xv * xv, axis=-1, keepdims=True) + EPS)
            h_scr[...] = (xv * r * g_ref[...]).astype(BF16)
            acc_scr[...] = jnp.zeros_like(acc_scr)

        h = h_scr[...]
        a = _dg(h, wg_ref[...], 1, 0)
        b = _dg(h, wu_ref[...], 1, 0)
        acc_scr[...] += _dg(_silu(a) * b, wd_ref[...], 1, 0)

        @pl.when(j == N_SHARD - 1)
        def _():
            o_ref[...] = x_ref[...] + 0.5 * acc_scr[...]

    row = pl.BlockSpec((tb, d), lambda i, j: (i, 0))
    lg, lu, ld = layer
    return pl.pallas_call(
        body, name=name, grid=(s // tb, N_SHARD),
        in_specs=[row, pl.BlockSpec((1, d), lambda i, j: (0, 0)),
                  pl.BlockSpec((None, None, d, fc), lambda i, j: (j, lg, 0, 0)),
                  pl.BlockSpec((None, None, d, fc), lambda i, j: (j, lu, 0, 0)),
                  pl.BlockSpec((None, None, fc, d), lambda i, j: (j, ld, 0, 0))] + dep_specs,
        out_specs=row,
        out_shape=jax.ShapeDtypeStruct((s, d), F32),
        scratch_shapes=[pltpu.VMEM((tb, d), BF16), pltpu.VMEM((tb, d), F32)],
        compiler_params=_params("parallel", "arbitrary"),
    )(x, g, wg, wu, wd, *dep_args)


def ffn_bwd(x, dxo, g, wg, wu, wd, layer, *, tb, dep=None, name):
    s, d = x.shape
    fc = wd.shape[-2]
    dep_specs, dep_args = _dep_operand(dep)

    def body(x_ref, dxo_ref, g_ref, wg_ref, wu_ref, wd_ref, *rest):
        dx_ref, dg_ref, dab_ref, hid_ref, h_ref, dy_ref, dh_scr = rest[len(dep_args):]
        i, j = pl.program_id(0), pl.program_id(1)

        @pl.when(j == 0)
        def _():
            xv = x_ref[...]
            r = lax.rsqrt(jnp.mean(xv * xv, axis=-1, keepdims=True) + EPS)
            h_ref[...] = (xv * r * g_ref[...]).astype(BF16)
            dy_ref[...] = (0.5 * dxo_ref[...]).astype(BF16)
            dh_scr[...] = jnp.zeros_like(dh_scr)

        h = h_ref[...]
        a = _dg(h, wg_ref[...], 1, 0)
        b = _dg(h, wu_ref[...], 1, 0)
        sg = jax.nn.sigmoid(a)
        sl = a * sg
        dhid = _dg(dy_ref[...], wd_ref[...], 1, 1)
        da = (dhid * b * (sg * (1.0 + a * (1.0 - sg)))).astype(BF16)
        db = (dhid * sl).astype(BF16)
        dab_ref[0] = da
        dab_ref[1] = db
        hid_ref[...] = (sl * b).astype(BF16)
        dh_scr[...] += _dg(da, wg_ref[...], 1, 1) + _dg(db, wu_ref[...], 1, 1)

        @pl.when(jnp.logical_and(i == 0, j == 0))
        def _():
            dg_ref[...] = jnp.zeros_like(dg_ref)

        @pl.when(j == N_SHARD - 1)
        def _():
            dx, dg = _rms_bwd(x_ref[...], g_ref[...], dh_scr[...])
            dx_ref[...] = dxo_ref[...] + dx
            dg_ref[...] += dg

    row = pl.BlockSpec((tb, d), lambda i, j: (i, 0))
    one = pl.BlockSpec((1, d), lambda i, j: (0, 0))
    lg, lu, ld = layer
    return pl.pallas_call(
        body, name=name, grid=(s // tb, N_SHARD),
        in_specs=[row, row, one,
                  pl.BlockSpec((None, None, d, fc), lambda i, j: (j, lg, 0, 0)),
                  pl.BlockSpec((None, None, d, fc), lambda i, j: (j, lu, 0, 0)),
                  pl.BlockSpec((None, None, fc, d), lambda i, j: (j, ld, 0, 0))] + dep_specs,
        out_specs=[row, one,
                   pl.BlockSpec((None, 2, tb, fc), lambda i, j: (j, 0, i, 0)),
                   pl.BlockSpec((None, tb, fc), lambda i, j: (j, i, 0)),
                   row, row],
        out_shape=[jax.ShapeDtypeStruct((s, d), F32), jax.ShapeDtypeStruct((1, d), F32),
                   jax.ShapeDtypeStruct((N_SHARD, 2, s, fc), BF16),
                   jax.ShapeDtypeStruct((N_SHARD, s, fc), BF16),
                   jax.ShapeDtypeStruct((s, d), BF16), jax.ShapeDtypeStruct((s, d), BF16)],
        scratch_shapes=[pltpu.VMEM((tb, d), F32)],
        compiler_params=_params("arbitrary", "arbitrary",
                                vmem_bytes=min(VMEM_BYTES_V7X - (8 << 20),
                                               12 * d * fc + 44 * tb * d + 40 * tb * fc + (4 << 20))),
    )(x, dxo, g, wg, wu, wd, *dep_args)


def _pblk(tb, width, col0, rev_n=None):
    assert col0 % width == 0
    cb = col0 // width
    if rev_n is None:
        return pl.BlockSpec((tb, width), lambda i: (i, cb))
    return pl.BlockSpec((tb, width), lambda i: (rev_n - 1 - i, cb))


def _full(shape):
    return pl.BlockSpec(shape, lambda i: (0,) * len(shape))


def conv_fwd(src, col0, width, w, bias, *, tb, name):
    s = src.shape[0]
    cb = col0 // width

    def body(x_ref, p_ref, w_ref, b_ref, y_ref):
        i = pl.program_id(0)
        prev = jnp.where(i > 0, p_ref[...], 0.0)
        xe = jnp.concatenate([prev, x_ref[...]], axis=0)
        wv = w_ref[...]
        acc = b_ref[...] + wv[3:4] * xe[8:]
        for k in range(3):
            acc = acc + wv[k:k + 1] * pltpu.roll(xe, 3 - k, 0)[8:]
        y_ref[...] = acc

    return pl.pallas_call(
        body, name=name, grid=(s // tb,),
        in_specs=[_pblk(tb, width, col0),
                  pl.BlockSpec((8, width), lambda i: (jnp.maximum(i * (tb // 8) - 1, 0), cb)),
                  _full((4, width)), _full((1, width))],
        out_specs=pl.BlockSpec((tb, width), lambda i: (i, 0)),
        out_shape=jax.ShapeDtypeStruct((s, width), F32),
        compiler_params=_params("parallel"),
    )(src, src, w, bias)


def conv_bwd(src, col0, width, dpre, w, *, tb, name):
    s = src.shape[0]
    cb = col0 // width
    nt = s // tb

    def body(x_ref, p_ref, d_ref, n_ref, w_ref, dx_ref, dwb_ref):
        i = pl.program_id(0)
        d = d_ref[...]
        nxt = jnp.where(i < nt - 1, n_ref[...], 0.0)
        de = jnp.concatenate([d, nxt], axis=0)
        wv = w_ref[...]
        dx = wv[3:4] * d
        for k in range(3):
            dx = dx + wv[k:k + 1] * pltpu.roll(de, tb + 8 - (3 - k), 0)[:tb]
        dx_ref[...] = dx.astype(BF16)

        @pl.when(i == 0)
        def _():
            dwb_ref[...] = jnp.zeros_like(dwb_ref)

        prev = jnp.where(i > 0, p_ref[...], 0.0)
        xe = jnp.concatenate([prev, x_ref[...]], axis=0)
        dwb_ref[3:4, :] += jnp.sum(d * xe[8:], axis=0, keepdims=True)
        for k in range(3):
            dwb_ref[k:k + 1, :] += jnp.sum(d * pltpu.roll(xe, 3 - k, 0)[8:], axis=0, keepdims=True)
        dwb_ref[4:5, :] += jnp.sum(d, axis=0, keepdims=True)

    return pl.pallas_call(
        body, name=name, grid=(nt,),
        in_specs=[_pblk(tb, width, col0),
                  pl.BlockSpec((8, width), lambda i: (jnp.maximum(i * (tb // 8) - 1, 0), cb)),
                  pl.BlockSpec((tb, width), lambda i: (i, 0)),
                  pl.BlockSpec((8, width), lambda i: (jnp.minimum((i + 1) * (tb // 8), s // 8 - 1), 0)),
                  _full((4, width))],
        out_specs=[pl.BlockSpec((tb, width), lambda i: (i, 0)), _full((8, width))],
        out_shape=[jax.ShapeDtypeStruct((s, width), BF16), jax.ShapeDtypeStruct((8, width), F32)],
        compiler_params=_params("arbitrary"),
    )(src, src, dpre, dpre, w)


SGU_BLOCK = 128


def _sgu_pre(u, v, lg, lb):
    ug = _gelu(u)
    vg = _gelu(v)
    mu = jnp.mean(vg, axis=-1, keepdims=True)
    var = jnp.mean(jnp.square(vg - mu), axis=-1, keepdims=True)
    return ug, (vg - mu) * lax.rsqrt(var + EPS) * lg + lb


def _sgu_mask():
    shp = (SGU_BLOCK, SGU_BLOCK)
    return (_rows(shp) // CHUNK) >= (_lanes(shp) // CHUNK)


def sgu_fwd(proj, lg, lb, ws, bt, *, tb, name):
    s = proj.shape[0]
    w = BRANCH_W

    def body(u_ref, v_ref, lg_ref, lb_ref, w_ref, b_ref, y_ref):
        ug, vn = _sgu_pre(u_ref[...], v_ref[...], lg_ref[...], lb_ref[...])
        mask = _sgu_mask()
        for g in range(4):
            wm = jnp.where(mask, w_ref[g], 0.0)
            bcol = _col(b_ref[...], g)
            cs = slice(128 * g, 128 * g + 128)
            for n in range(tb // SGU_BLOCK):
                rs = slice(SGU_BLOCK * n, SGU_BLOCK * (n + 1))
                mixed = _dg(wm, vn[rs, cs], 1, 0) + bcol
                y_ref[rs, cs] = (ug[rs, cs] * mixed).astype(BF16)

    return pl.pallas_call(
        body, name=name, grid=(s // tb,),
        in_specs=[_pblk(tb, w, C_AU), _pblk(tb, w, C_AV), _full((1, w)), _full((1, w)),
                  _full((4, 128, 128)), _full((128, 128))],
        out_specs=pl.BlockSpec((tb, w), lambda i: (i, 0)),
        out_shape=jax.ShapeDtypeStruct((s, w), BF16),
        compiler_params=_params("parallel"),
    )(proj, proj, lg, lb, ws, bt)


def sgu_bwd(proj, dy, lg, lb, ws, bt, *, tb, dep=None, name):
    s = proj.shape[0]
    w = BRANCH_W
    dep_specs, dep_args = _dep_operand(dep)

    def body(u_ref, v_ref, dy_ref, lg_ref, lb_ref, w_ref, b_ref, *rest):
        du_ref, dv_ref, dlg_ref, dlb_ref, dw_ref, db_ref, dug_scr, dvn_scr = rest[len(dep_args):]

        @pl.when(pl.program_id(0) == 0)
        def _():
            dlg_ref[...] = jnp.zeros_like(dlg_ref)
            dlb_ref[...] = jnp.zeros_like(dlb_ref)
            dw_ref[...] = jnp.zeros_like(dw_ref)
            db_ref[...] = jnp.zeros_like(db_ref)

        (ug, vn), vf = jax.vjp(_sgu_pre, u_ref[...], v_ref[...], lg_ref[...], lb_ref[...])
        dyv = dy_ref[...]
        mask = _sgu_mask()
        for g in range(4):
            wm = jnp.where(mask, w_ref[g], 0.0)
            bcol = _col(b_ref[...], g)
            cs = slice(128 * g, 128 * g + 128)
            dwg = jnp.zeros((SGU_BLOCK, SGU_BLOCK), F32)
            dbc = jnp.zeros((SGU_BLOCK, 1), F32)
            for n in range(tb // SGU_BLOCK):
                rs = slice(SGU_BLOCK * n, SGU_BLOCK * (n + 1))
                blk = vn[rs, cs]
                mixed = _dg(wm, blk, 1, 0) + bcol
                dmix = dyv[rs, cs] * ug[rs, cs]
                dug_scr[rs, cs] = dyv[rs, cs] * mixed
                dvn_scr[rs, cs] = _dg(wm, dmix, 0, 0)
                dwg = dwg + _dg(dmix, blk, 1, 1)
                dbc = dbc + jnp.sum(dmix, axis=1, keepdims=True)
            dw_ref[g] += jnp.where(mask, dwg, 0.0)
            db_ref[...] += _put_col(dbc, g, 128)
        du, dv, dlg, dlb = vf((dug_scr[...], dvn_scr[...]))
        du_ref[...] = du.astype(BF16)
        dv_ref[...] = dv.astype(BF16)
        dlg_ref[...] += dlg
        dlb_ref[...] += dlb

    row = pl.BlockSpec((tb, w), lambda i: (i, 0))
    return pl.pallas_call(
        body, name=name, grid=(s // tb,),
        in_specs=[_pblk(tb, w, C_AU), _pblk(tb, w, C_AV), row, _full((1, w)), _full((1, w)),
                  _full((4, 128, 128)), _full((128, 128))] + dep_specs,
        out_specs=[row, row, _full((1, w)), _full((1, w)), _full((4, 128, 128)), _full((128, 128))],
        out_shape=[jax.ShapeDtypeStruct((s, w), BF16), jax.ShapeDtypeStruct((s, w), BF16),
                   jax.ShapeDtypeStruct((1, w), F32), jax.ShapeDtypeStruct((1, w), F32),
                   jax.ShapeDtypeStruct((4, 128, 128), F32), jax.ShapeDtypeStruct((128, 128), F32)],
        scratch_shapes=[pltpu.VMEM((tb, w), F32), pltpu.VMEM((tb, w), F32)],
        compiler_params=_params("arbitrary"),
    )(proj, proj, dy, lg, lb, ws, bt, *dep_args)


POOL_HALO = 16


def _pooled(xe, t0, tb):
    s2 = xe + pltpu.roll(xe, 1, 0)
    s4 = s2 + pltpu.roll(s2, 2, 0)
    s8 = s4 + pltpu.roll(s4, 4, 0)
    s16 = s8 + pltpu.roll(s8, 8, 0)
    t = (t0 + _rows((tb, 128))).astype(F32)
    outs = []
    for g, sw in enumerate((s2, s4, s8, s16)):
        cs = slice(128 * g, 128 * g + 128)
        cnt = jnp.minimum(t + 1.0, float(2 ** (g + 1)))
        outs.append(sw[POOL_HALO:, cs] / cnt - xe[POOL_HALO:, cs])
    return outs


def pool_fwd(proj, wp, scale, *, tb, name):
    s = proj.shape[0]
    w = BRANCH_W

    def body(x_ref, p_ref, w_ref, sc_ref, y_ref):
        i = pl.program_id(0)
        prev = jnp.where(i > 0, p_ref[...], 0.0)
        xe = jnp.concatenate([prev, x_ref[...]], axis=0)
        pooled = _pooled(xe, i * tb, tb)
        sc = sc_ref[...]
        for g in range(4):
            cs = slice(128 * g, 128 * g + 128)
            y_ref[:, cs] = (_dg(pooled[g], w_ref[g], 1, 0) * sc[:, cs]).astype(BF16)

    cb = C_DX // w
    return pl.pallas_call(
        body, name=name, grid=(s // tb,),
        in_specs=[_pblk(tb, w, C_DX),
                  pl.BlockSpec((POOL_HALO, w), lambda i: (jnp.maximum(i * (tb // POOL_HALO) - 1, 0), cb)),
                  _full((4, 128, 128)), _full((1, w))],
        out_specs=pl.BlockSpec((tb, w), lambda i: (i, 0)),
        out_shape=jax.ShapeDtypeStruct((s, w), BF16),
        compiler_params=_params("parallel"),
    )(proj, proj, wp, scale)


def pool_bwd(proj, dy, wp, scale, *, tb, name):
    s = proj.shape[0]
    w = BRANCH_W
    nt = s // tb
    te = tb + POOL_HALO

    def body(x_ref, p_ref, dy_ref, n_ref, w_ref, sc_ref, dx_ref, dw_ref, dsc_ref):
        i = pl.program_id(0)

        @pl.when(i == 0)
        def _():
            dw_ref[...] = jnp.zeros_like(dw_ref)
            dsc_ref[...] = jnp.zeros_like(dsc_ref)

        prev = jnp.where(i > 0, p_ref[...], 0.0)
        xe = jnp.concatenate([prev, x_ref[...]], axis=0)
        pooled = _pooled(xe, i * tb, tb)
        dyv = dy_ref[...]
        nxt = jnp.where(i < nt - 1, n_ref[...], 0.0)
        dye = jnp.concatenate([dyv, nxt], axis=0)
        sc = sc_ref[...]
        t = (i * tb + _rows((te, 128))).astype(F32)
        for g in range(4):
            cs = slice(128 * g, 128 * g + 128)
            win = 2 ** (g + 1)
            dpm = dye[:, cs] * sc[:, cs]
            dpool = _dg(dpm, w_ref[g], 1, 1)
            q = dpool / jnp.minimum(t + 1.0, float(win))
            r, sh = q, 1
            while sh < win:
                r = r + pltpu.roll(r, te - sh, 0)
                sh *= 2
            dx_ref[:, cs] = (r[:tb] - dpool[:tb]).astype(BF16)
            dw_ref[g] += _dg(pooled[g], dpm[:tb], 0, 0)
            dsc_ref[:, cs] += jnp.sum(dyv[:, cs] * _dg(pooled[g], w_ref[g], 1, 0), axis=0, keepdims=True)

    cb = C_DX // w
    row = pl.BlockSpec((tb, w), lambda i: (i, 0))
    return pl.pallas_call(
        body, name=name, grid=(nt,),
        in_specs=[_pblk(tb, w, C_DX),
                  pl.BlockSpec((POOL_HALO, w), lambda i: (jnp.maximum(i * (tb // POOL_HALO) - 1, 0), cb)),
                  row,
                  pl.BlockSpec((POOL_HALO, w),
                               lambda i: (jnp.minimum((i + 1) * (tb // POOL_HALO), s // POOL_HALO - 1), 0)),
                  _full((4, 128, 128)), _full((1, w))],
        out_specs=[row, _full((4, 128, 128)), _full((1, w))],
        out_shape=[jax.ShapeDtypeStruct((s, w), BF16), jax.ShapeDtypeStruct((4, 128, 128), F32),
                   jax.ShapeDtypeStruct((1, w), F32)],
        compiler_params=_params("arbitrary"),
    )(proj, proj, dy, dy, wp, scale)


def _neg_expm1(z):
    p = 1.0 + z * (1.0 / 9.0)
    for n in (8.0, 7.0, 6.0, 5.0, 4.0, 3.0, 2.0):
        p = 1.0 + z * (1.0 / n) * p
    return jnp.where(z > -0.5, -z * p, 1.0 - jnp.exp(z))


def _lru_gates(xc, wa, wx, ba, bx, lam):
    r = jax.nn.sigmoid(mm_nn(xc, wa) + ba)
    i = jax.nn.sigmoid(mm_nn(xc, wx) + bx)
    log_a = -LRU_C * r * _softplus(-lam)
    a = jnp.exp(log_a)
    return a, jnp.sqrt(_neg_expm1(2.0 * log_a)) * (i * xc)


def _scan_down(a, b):
    n = a.shape[0]
    row = _rows(a.shape)
    sh = 1
    while sh < n:
        keep = row >= sh
        a_sh = jnp.where(keep, pltpu.roll(a, sh, 0), 1.0)
        b_sh = jnp.where(keep, pltpu.roll(b, sh, 0), 0.0)
        b = a * b_sh + b
        a = a * a_sh
        sh *= 2
    return a, b


def _scan_up(a, b):
    n = a.shape[0]
    row = _rows(a.shape)
    sh = 1
    while sh < n:
        keep = row < n - sh
        a_sh = jnp.where(keep, pltpu.roll(a, n - sh, 0), 1.0)
        b_sh = jnp.where(keep, pltpu.roll(b, n - sh, 0), 0.0)
        b = a * b_sh + b
        a = a * a_sh
        sh *= 2
    return a, b


def lru_fwd(xc, proj, wa, wx, ba, bx, lam, *, tb, name):
    s = xc.shape[0]
    w = BRANCH_W

    def body(xc_ref, gate_ref, wa_ref, wx_ref, ba_ref, bx_ref, lam_ref, y_ref, h_ref, carry):
        @pl.when(pl.program_id(0) == 0)
        def _():
            carry[...] = jnp.zeros_like(carry)

        a, b = _lru_gates(xc_ref[...], wa_ref[...], wx_ref[...], ba_ref[...], bx_ref[...], lam_ref[...])
        pa, hb = _scan_down(a, b)
        h = pa * carry[...] + hb
        h_ref[...] = h
        carry[...] = h_ref[tb - 1:tb, :]
        y_ref[...] = (h * _gelu(gate_ref[...])).astype(BF16)

    row = pl.BlockSpec((tb, w), lambda i: (i, 0))
    return pl.pallas_call(
        body, name=name, grid=(s // tb,),
        in_specs=[row, _pblk(tb, w, C_BG), _full((w, w)), _full((w, w)),
                  _full((1, w)), _full((1, w)), _full((1, w))],
        out_specs=[row, row],
        out_shape=[jax.ShapeDtypeStruct((s, w), BF16), jax.ShapeDtypeStruct((s, w), F32)],
        scratch_shapes=[pltpu.VMEM((1, w), F32)],
        compiler_params=_params("arbitrary"),
    )(xc, proj, wa, wx, ba, bx, lam)


def lru_bwd(xc, proj, h, dy, wa, wx, ba, bx, lam, *, tb, name):
    s = xc.shape[0]
    w = BRANCH_W
    nt = s // tb

    def body(xc_ref, gate_ref, h_ref, hp_ref, dy_ref, wa_ref, wx_ref, ba_ref, bx_ref, lam_ref,
             dxc_ref, dgate_ref, dwa_ref, dwx_ref, dba_ref, dbx_ref, dlam_ref, carry):
        i = pl.program_id(0)

        @pl.when(i == 0)
        def _():
            carry[...] = jnp.zeros_like(carry)
            for r in (dwa_ref, dwx_ref, dba_ref, dbx_ref, dlam_ref):
                r[...] = jnp.zeros_like(r)

        (a, b), vf = jax.vjp(_lru_gates, xc_ref[...], wa_ref[...], wx_ref[...],
                             ba_ref[...], bx_ref[...], lam_ref[...])
        gate = gate_ref[...]
        dyv = dy_ref[...]
        hv = h_ref[...]
        row = _rows((tb, w))
        a_next = jnp.where(row < tb - 1, pltpu.roll(a, tb - 1, 0), 1.0)
        pa, xb = _scan_up(a_next, dyv * _gelu(gate))
        dh = xb + pa * carry[...]
        carry[...] = a[0:1, :] * dh[0:1, :]
        h_first = jnp.where(i < nt - 1, hp_ref[7:8, :], 0.0)
        h_prev = jnp.where(row >= 1, pltpu.roll(hv, 1, 0), h_first)
        dxc, dwa, dwx, dba, dbx, dlam = vf((dh * h_prev, dh))
        dxc_ref[...] = dxc
        dgate_ref[...] = (dyv * hv * _gelu_grad(gate)).astype(BF16)
        dwa_ref[...] += dwa
        dwx_ref[...] += dwx
        dba_ref[...] += dba
        dbx_ref[...] += dbx
        dlam_ref[...] += dlam

    row_spec = pl.BlockSpec((tb, w), lambda i: (nt - 1 - i, 0))
    hp_spec = pl.BlockSpec((8, w), lambda i: (jnp.maximum((nt - 1 - i) * (tb // 8) - 1, 0), 0))
    return pl.pallas_call(
        body, name=name, grid=(nt,),
        in_specs=[row_spec, _pblk(tb, w, C_BG, rev_n=nt), row_spec, hp_spec, row_spec,
                  _full((w, w)), _full((w, w)), _full((1, w)), _full((1, w)), _full((1, w))],
        out_specs=[row_spec, row_spec, _full((w, w)), _full((w, w)),
                   _full((1, w)), _full((1, w)), _full((1, w))],
        out_shape=[jax.ShapeDtypeStruct((s, w), F32), jax.ShapeDtypeStruct((s, w), BF16),
                   jax.ShapeDtypeStruct((w, w), F32), jax.ShapeDtypeStruct((w, w), F32),
                   jax.ShapeDtypeStruct((1, w), F32), jax.ShapeDtypeStruct((1, w), F32),
                   jax.ShapeDtypeStruct((1, w), F32)],
        scratch_shapes=[pltpu.VMEM((1, w), F32)],
        compiler_params=_params("arbitrary"),
    )(xc, proj, h, h, dy, wa, wx, ba, bx, lam)


def _gdn_g(ba, alog, dt):
    return -jnp.exp(alog) * _softplus(ba + dt)


def _gdn_local(qp, kp, vp, bpre, gc):
    b, c, dk = qp.shape
    q = _silu(qp)
    k = _silu(kp)
    v = _silu(vp)
    q = q * lax.rsqrt(jnp.sum(q * q, axis=-1, keepdims=True) + EPS) * (dk ** -0.5)
    k = k * lax.rsqrt(jnp.sum(k * k, axis=-1, keepdims=True) + EPS)
    beta = jax.nn.sigmoid(bpre)
    sq = (b, c, c)
    ii, jj = _iota3(sq, 1), _iota3(sq, 2)
    gr = jnp.sum(gc * (ii == jj).astype(F32), axis=1, keepdims=True)
    decay = jnp.exp(jnp.where(ii >= jj, gc - gr, -jnp.inf))
    wd = (b, c, 2 * c)
    wi, wj = _iota3(wd, 1), _iota3(wd, 2) - c
    gr_w = jnp.sum(gc * (wi == wj).astype(F32), axis=1, keepdims=True)
    decay_w = jnp.exp(jnp.where(jnp.logical_and(wj >= 0, wi > wj), gc - gr_w, -jnp.inf))
    kb = k * beta
    pad = jnp.zeros((b, c, dk), F32)
    t = _tri_inv(bmm_nt(kb, jnp.concatenate([pad, k], axis=1)) * decay_w)
    u = bmm_nn(t, jnp.concatenate([v * beta, pad], axis=1))
    wk = bmm_nn(t, jnp.concatenate([kb * jnp.exp(gc), pad], axis=1))
    attn = bmm_nt(q, k) * decay
    g_last = jnp.sum(jnp.where(_iota3((b, c, 1), 1) == c - 1, gc, 0.0), axis=1, keepdims=True)
    return u, wk, attn, q * jnp.exp(gc), k * jnp.exp(g_last - gc), jnp.exp(g_last)


def _gdn_recur(s, u, wk, attn, qg, kd, eg, zz, ng):
    v_new = u - bmm_nn(wk, s)
    o = bmm_nn(qg, s) + bmm_nn(attn, v_new)
    s_out = s * eg + bmm_tn(kd, v_new)
    o = o * lax.rsqrt(jnp.mean(o * o, axis=-1, keepdims=True) + EPS) * ng
    return o * _silu(zz), s_out


def _gdn_inputs(qkv_ref, ba, gcums, cps):
    c, hd = CHUNK, GDN_DK
    qs, ks, vs, bs, gs = [], [], [], [], []
    for ci in range(cps):
        rs = slice(c * ci, c * (ci + 1))
        for h in range(GDN_HEADS):
            qs.append(qkv_ref[rs, hd * h:hd * (h + 1)])
            ks.append(qkv_ref[rs, 512 + hd * h:512 + hd * (h + 1)])
            vs.append(qkv_ref[rs, 1024 + hd * h:1024 + hd * (h + 1)])
            bs.append(_col(ba[rs], h))
            gs.append(_col(gcums[ci], 4 + h))
    return tuple(jnp.stack(t) for t in (qs, ks, vs, bs, gs))


def _chunk_cumsum(g, transpose=False):
    c = CHUNK
    tri = (_rows((c, c)) >= _lanes((c, c))).astype(F32)
    return _dg3(tri, g, 0 if transpose else 1, 0)


def gdn_fwd(qkv, proj, alog, dt, ng, *, cps, name):
    s = qkv.shape[0]
    c = CHUNK
    nc = s // c
    hd = GDN_DK
    tb = c * cps

    def body(qkv_ref, z_ref, ba_ref, al_ref, dt_ref, ng_ref, y_ref, sall_ref, s_scr):
        @pl.when(pl.program_id(0) == 0)
        def _():
            s_scr[...] = jnp.zeros_like(s_scr)

        nh = GDN_HEADS
        ba = ba_ref[...]
        g = _gdn_g(ba, al_ref[...], dt_ref[...])
        gcums = [_chunk_cumsum(g[c * ci:c * (ci + 1)]) for ci in range(cps)]
        loc = _gdn_local(*_gdn_inputs(qkv_ref, ba, gcums, cps))
        for ci in range(cps):
            rs = slice(c * ci, c * (ci + 1))
            sl = slice(nh * ci, nh * (ci + 1))
            s_in = s_scr[...]
            sall_ref[ci] = s_in
            zz = jnp.stack([z_ref[rs, hd * h:hd * (h + 1)] for h in range(nh)])
            y, s_out = _gdn_recur(s_in, *(t[sl] for t in loc), zz, ng_ref[...])
            s_scr[...] = s_out
            for h in range(nh):
                y_ref[rs, hd * h:hd * (h + 1)] = y[h].astype(BF16)

    return pl.pallas_call(
        body, name=name, grid=(nc // cps,),
        in_specs=[pl.BlockSpec((tb, 1536), lambda i: (i, 0)), _pblk(tb, 512, C_Z), _pblk(tb, 128, C_BA),
                  _full((1, 128)), _full((1, 128)), _full((1, 128))],
        out_specs=[pl.BlockSpec((tb, 512), lambda i: (i, 0)),
                   pl.BlockSpec((cps, GDN_HEADS, hd, hd), lambda i: (i, 0, 0, 0))],
        out_shape=[jax.ShapeDtypeStruct((s, 512), BF16), jax.ShapeDtypeStruct((nc, GDN_HEADS, hd, hd), F32)],
        scratch_shapes=[pltpu.VMEM((GDN_HEADS, hd, hd), F32)],
        compiler_params=_params("arbitrary"),
    )(qkv, proj, proj, alog, dt, ng)


def gdn_bwd(qkv, proj, sall, dy, alog, dt, ng, *, cps, name):
    s = qkv.shape[0]
    c = CHUNK
    nc = s // c
    hd = GDN_DK
    tb = c * cps
    nt = nc // cps

    def body(qkv_ref, z_ref, ba_ref, sall_ref, dy_ref, al_ref, dt_ref, ng_ref,
             dqkv_ref, dz_ref, dba_ref, dal_ref, ddt_ref, dng_ref, ds_scr):
        @pl.when(pl.program_id(0) == 0)
        def _():
            ds_scr[...] = jnp.zeros_like(ds_scr)
            dal_ref[...] = jnp.zeros_like(dal_ref)
            ddt_ref[...] = jnp.zeros_like(ddt_ref)
            dng_ref[...] = jnp.zeros_like(dng_ref)

        nh = GDN_HEADS
        ba = ba_ref[...]
        g, gvf = jax.vjp(_gdn_g, ba, al_ref[...], dt_ref[...])
        gcums = [_chunk_cumsum(g[c * ci:c * (ci + 1)]) for ci in range(cps)]
        loc, loc_vf = jax.vjp(_gdn_local, *_gdn_inputs(qkv_ref, ba, gcums, cps))
        d_loc = [None] * cps
        dng = jnp.zeros((1, 128), F32)
        for ci in reversed(range(cps)):
            rs = slice(c * ci, c * (ci + 1))
            sl = slice(nh * ci, nh * (ci + 1))
            zz = jnp.stack([z_ref[rs, hd * h:hd * (h + 1)] for h in range(nh)])
            dyy = jnp.stack([dy_ref[rs, hd * h:hd * (h + 1)] for h in range(nh)])
            _, rvf = jax.vjp(_gdn_recur, sall_ref[ci], *(t[sl] for t in loc), zz, ng_ref[...])
            ds, du, dwk, dattn, dqg, dkd, deg, dzz, dn = rvf((dyy, ds_scr[...]))
            ds_scr[...] = ds
            d_loc[ci] = (du, dwk, dattn, dqg, dkd, deg)
            dng = dng + dn
            for h in range(nh):
                dz_ref[rs, hd * h:hd * (h + 1)] = dzz[h].astype(BF16)
        dq, dk, dv, dbp, dgc = loc_vf(tuple(jnp.concatenate([d_loc[ci][j] for ci in range(cps)], axis=0)
                                            for j in range(6)))
        dgs, dbas = [], []
        for ci in range(cps):
            rs = slice(c * ci, c * (ci + 1))
            dgcum = jnp.zeros((c, 128), F32)
            dba = jnp.zeros((c, 128), F32)
            for h in range(nh):
                b = nh * ci + h
                dqkv_ref[rs, hd * h:hd * (h + 1)] = dq[b]
                dqkv_ref[rs, 512 + hd * h:512 + hd * (h + 1)] = dk[b]
                dqkv_ref[rs, 1024 + hd * h:1024 + hd * (h + 1)] = dv[b]
                dgcum = dgcum + _put_col(dgc[b], 4 + h, 128)
                dba = dba + _put_col(dbp[b], h, 128)
            dgs.append(_chunk_cumsum(dgcum, transpose=True))
            dbas.append(dba)
        dba2, dal, ddt = gvf(jnp.concatenate(dgs, axis=0))
        dba_ref[...] = (jnp.concatenate(dbas, axis=0) + dba2).astype(BF16)
        dal_ref[...] += dal
        ddt_ref[...] += ddt
        dng_ref[...] += dng

    def rev(width):
        return pl.BlockSpec((tb, width), lambda i: (nt - 1 - i, 0))

    return pl.pallas_call(
        body, name=name, grid=(nt,),
        in_specs=[rev(1536), _pblk(tb, 512, C_Z, rev_n=nt), _pblk(tb, 128, C_BA, rev_n=nt),
                  pl.BlockSpec((cps, GDN_HEADS, hd, hd), lambda i: (nt - 1 - i, 0, 0, 0)), rev(512),
                  _full((1, 128)), _full((1, 128)), _full((1, 128))],
        out_specs=[rev(1536), rev(512), rev(128), _full((1, 128)), _full((1, 128)), _full((1, 128))],
        out_shape=[jax.ShapeDtypeStruct((s, 1536), F32), jax.ShapeDtypeStruct((s, 512), BF16),
                   jax.ShapeDtypeStruct((s, 128), BF16), jax.ShapeDtypeStruct((1, 128), F32),
                   jax.ShapeDtypeStruct((1, 128), F32), jax.ShapeDtypeStruct((1, 128), F32)],
        scratch_shapes=[pltpu.VMEM((GDN_HEADS, hd, hd), F32)],
        compiler_params=_params("arbitrary"),
    )(qkv, proj, proj, sall, dy, alog, dt, ng)


def merge_fwd(x, ys, proj, wb, wo, *, tb, dep=None, name):
    s, d = x.shape
    dep_specs, dep_args = _dep_operand(dep)

    def body(x_ref, ya, yb, yc, yd, g0, g1, g2, g3, wb_ref, wo_ref, *rest):
        o_ref, m_ref = rest[len(dep_args):]
        m = jnp.zeros((tb, d), F32)
        for g, (y_ref, gp_ref) in enumerate(zip((ya, yb, yc, yd), (g0, g1, g2, g3))):
            m = m + jax.nn.sigmoid(gp_ref[...]) * _dg(y_ref[...], wb_ref[g], 1, 0)
        mb = m.astype(BF16)
        m_ref[...] = mb
        o_ref[...] = x_ref[...] + _dg(mb, wo_ref[...], 1, 0)

    row = pl.BlockSpec((tb, d), lambda i: (i, 0))
    yrow = pl.BlockSpec((tb, BRANCH_W), lambda i: (i, 0))
    return pl.pallas_call(
        body, name=name, grid=(s // tb,),
        in_specs=[row] + [yrow] * 4 + [_pblk(tb, d, C_GATE + d * g) for g in range(4)]
        + [_full((4, BRANCH_W, d)), _full((d, d))] + dep_specs,
        out_specs=[row, row],
        out_shape=[jax.ShapeDtypeStruct((s, d), F32), jax.ShapeDtypeStruct((s, d), BF16)],
        compiler_params=_params("parallel"),
    )(x, *ys, proj, proj, proj, proj, wb, wo, *dep_args)


def merge_bwd(dx, ys, proj, wb, wo, *, tb, dep=None, name):
    s, d = dx.shape
    dep_specs, dep_args = _dep_operand(dep)

    def body(dx_ref, ya, yb, yc, yd, g0, g1, g2, g3, wb_ref, wo_ref, *rest):
        dgp_ref, dbr_ref, da, db, dc, dd, dxb_ref = rest[len(dep_args):]
        dxb = dx_ref[...].astype(BF16)
        dxb_ref[...] = dxb
        dm = _dg(dxb, wo_ref[...], 1, 1)
        for g, (y_ref, gp_ref, dy_ref) in enumerate(zip((ya, yb, yc, yd), (g0, g1, g2, g3), (da, db, dc, dd))):
            br = _dg(y_ref[...], wb_ref[g], 1, 0)
            sg = jax.nn.sigmoid(gp_ref[...])
            dgp_ref[:, d * g:d * (g + 1)] = (dm * br * sg * (1.0 - sg)).astype(BF16)
            dbr = (dm * sg).astype(BF16)
            dbr_ref[g] = dbr
            dy_ref[...] = _dg(dbr, wb_ref[g], 1, 1)

    row = pl.BlockSpec((tb, d), lambda i: (i, 0))
    yrow = pl.BlockSpec((tb, BRANCH_W), lambda i: (i, 0))
    return pl.pallas_call(
        body, name=name, grid=(s // tb,),
        in_specs=[row] + [yrow] * 4 + [_pblk(tb, d, C_GATE + d * g) for g in range(4)]
        + [_full((4, BRANCH_W, d)), _full((d, d))] + dep_specs,
        out_specs=[pl.BlockSpec((tb, 4 * d), lambda i: (i, 0)), pl.BlockSpec((4, tb, d), lambda i: (0, i, 0)),
                   yrow, yrow, yrow, yrow, row],
        out_shape=[jax.ShapeDtypeStruct((s, 4 * d), BF16), jax.ShapeDtypeStruct((4, s, d), BF16)]
        + [jax.ShapeDtypeStruct((s, BRANCH_W), F32)] * 4 + [jax.ShapeDtypeStruct((s, d), BF16)],
        compiler_params=_params("parallel"),
    )(dx, *ys, proj, proj, proj, proj, wb, wo, *dep_args)


def loss_head(x, g, target, *, tb, name):
    s, d = x.shape

    def body(x_ref, g_ref, t_ref, dx_ref, dg_ref, loss_ref):
        @pl.when(pl.program_id(0) == 0)
        def _():
            dg_ref[...] = jnp.zeros_like(dg_ref)
            loss_ref[...] = jnp.zeros_like(loss_ref)

        xv, gv = x_ref[...], g_ref[...]
        r = lax.rsqrt(jnp.mean(xv * xv, axis=-1, keepdims=True) + EPS)
        err = xv * r * gv - t_ref[...]
        loss_ref[...] += 0.5 * jnp.sum(jnp.mean(err * err, axis=-1, keepdims=True), axis=0, keepdims=True)
        dx, dg = _rms_bwd(xv, gv, err * (1.0 / d))
        dx_ref[...] = dx
        dg_ref[...] += dg

    row = pl.BlockSpec((tb, d), lambda i: (i, 0))
    return pl.pallas_call(
        body, name=name, grid=(s // tb,),
        in_specs=[row, _full((1, d)), row],
        out_specs=[row, _full((1, d)), _full((1, 128))],
        out_shape=[jax.ShapeDtypeStruct((s, d), F32), jax.ShapeDtypeStruct((1, d), F32),
                   jax.ShapeDtypeStruct((1, 128), F32)],
        compiler_params=_params("arbitrary"),
    )(x, g, target)


def adamw(w, g, m, v, *, name):
    shape = w.shape
    lead, (r, c) = shape[:-2], shape[-2:]
    tr = r
    while tr * c * 4 > (1 << 20) and tr % 16 == 0:
        tr //= 2
    c1 = 1.0 / (1.0 - ADAM_B1 ** ADAM_STEP)
    c2 = 1.0 / (1.0 - ADAM_B2 ** ADAM_STEP)

    def body(w_ref, g_ref, m_ref, v_ref, d_ref, nm_ref, nv_ref):
        gv = g_ref[...]
        nm = ADAM_B1 * m_ref[...] + (1.0 - ADAM_B1) * gv
        nv = ADAM_B2 * v_ref[...] + (1.0 - ADAM_B2) * (gv * gv)
        nm_ref[...] = nm
        nv_ref[...] = nv
        d_ref[...] = -ADAM_LR * ((nm * c1) / (jnp.sqrt(nv * c2) + ADAM_EPS) + ADAM_WD * w_ref[...])

    blk = pl.BlockSpec((None,) * len(lead) + (tr, c), lambda *i: i + (0,))
    return tuple(pl.pallas_call(
        body, name=name, grid=lead + (r // tr,),
        in_specs=[blk] * 4, out_specs=[blk] * 3,
        out_shape=[jax.ShapeDtypeStruct(shape, F32)] * 3,
        compiler_params=_params(*(("parallel",) * (len(lead) + 1))),
    )(w, g, m, v))


_ANY = pl.BlockSpec(memory_space=pl.ANY)


def _place():
    x, y, c = lax.axis_index("x"), lax.axis_index("y"), lax.axis_index("c")
    return x, y, c, [(1 - x, y), (x, 1 - y), (1 - x, 1 - y)]


def gather_shards(bufs, *, name):
    n = len(bufs)

    def body(*refs):
        outs = refs[n:2 * n]
        ici_send, ici_recv, d2d_send, d2d_recv = refs[2 * n:]
        x, y, c, chips = _place()
        me = 2 * x + y

        def ici(a, j, slab, to):
            return pltpu.make_async_remote_copy(
                src_ref=outs[a].at[slab, c], dst_ref=outs[a].at[slab, c], send_sem=ici_send.at[a, j],
                recv_sem=ici_recv.at[a, j], device_id=to, device_id_type=MESH)

        def d2d(a, j, slab, half):
            return pltpu.make_async_remote_copy(
                src_ref=outs[a].at[slab, half], dst_ref=outs[a].at[slab, half], send_sem=d2d_send.at[a, j],
                recv_sem=d2d_recv.at[a, j], device_id=(x, y, 1 - c), device_id_type=MESH)

        sends = []
        for a in range(n):
            for j, (px, py) in enumerate(chips):
                cp = ici(a, j, me, (px, py, c))
                cp.start()
                sends.append(cp)
        for a in range(n):
            for j, (px, py) in enumerate(chips):
                ici(a, j, 2 * px + py, (px, py, c)).wait_recv()
                cp = d2d(a, j, 2 * px + py, c)
                cp.start()
                sends.append(cp)
        for a in range(n):
            for j, (px, py) in enumerate(chips):
                d2d(a, j, 2 * px + py, 1 - c).wait_recv()
        for cp in sends:
            cp.wait_send()

    return pl.pallas_call(
        body, name=name,
        in_specs=[_ANY] * n, out_specs=[_ANY] * n,
        out_shape=[jax.ShapeDtypeStruct(t.shape, t.dtype) for t in bufs],
        input_output_aliases={a: a for a in range(n)},
        scratch_shapes=[pltpu.SemaphoreType.DMA((n, 3))] * 4,
    )(*bufs)


_SEM = pl.BlockSpec(memory_space=pltpu.SEMAPHORE)
_VMEM = pl.BlockSpec(memory_space=pltpu.VMEM)
_EFFECT = pltpu.SideEffectType.DATAFLOW_SIDE_EFFECTING
_TOKEN = jax.ShapeDtypeStruct((8, 128), F32)


def _gather_copies(outs, sems_ici, sems_d2d):
    x, y, c, chips = _place()

    def ici(a, j, slab, to):
        return pltpu.make_async_remote_copy(
            src_ref=outs[a].at[slab, c], dst_ref=outs[a].at[slab, c], send_sem=sems_ici[0][3 * a + j],
            recv_sem=sems_ici[1][3 * a + j], device_id=to, device_id_type=MESH)

    def d2d(a, j, slab, half):
        return pltpu.make_async_remote_copy(
            src_ref=outs[a].at[slab, half], dst_ref=outs[a].at[slab, half], send_sem=sems_d2d[0][3 * a + j],
            recv_sem=sems_d2d[1][3 * a + j], device_id=(x, y, 1 - c), device_id_type=MESH)

    return x, y, c, chips, ici, d2d


def _async_call(body, bufs, sems_in, after, sems_out, token, *, name):
    n = len(bufs)
    n_out = 6 * n if sems_out else 0
    extra = [] if after is None else [after]
    return pl.pallas_call(
        body, name=name,
        in_specs=[_ANY] * n + [_SEM] * len(sems_in) + [_ANY] * len(extra),
        out_specs=[_ANY] * n + [_SEM] * n_out + [_VMEM] * token,
        out_shape=[jax.ShapeDtypeStruct(t.shape, t.dtype) for t in bufs]
        + [pltpu.SemaphoreType.DMA(())] * n_out + [_TOKEN] * token,
        input_output_aliases={a: a for a in range(n)},
        compiler_params=pltpu.CompilerParams(has_side_effects=_EFFECT),
    )(*bufs, *sems_in, *extra)


def gather_start(bufs, *, name):
    n = len(bufs)

    def body(*refs):
        outs = refs[n:2 * n]
        sems, token = refs[2 * n:8 * n], refs[8 * n]
        x, y, c, chips, ici, _ = _gather_copies(outs, (sems[:3 * n], sems[3 * n:]), None)
        for a in range(n):
            for j, (px, py) in enumerate(chips):
                ici(a, j, 2 * x + y, (px, py, c)).start()
        token[...] = jnp.zeros_like(token)

    res = _async_call(body, bufs, [], None, True, True, name=name)
    return res[:n], res[n:7 * n], res[7 * n]


def gather_pass(bufs, sems, after, *, name):
    n = len(bufs)

    def body(*refs):
        s_in = refs[n:7 * n]
        outs = refs[7 * n + 1:8 * n + 1]
        s_out, token = refs[8 * n + 1:14 * n + 1], refs[14 * n + 1]
        x, y, c, chips, ici, d2d = _gather_copies(outs, (s_in[:3 * n], s_in[3 * n:]),
                                                  (s_out[:3 * n], s_out[3 * n:]))
        for a in range(n):
            for j, (px, py) in enumerate(chips):
                ici(a, j, 2 * px + py, (px, py, c)).wait_recv()
                d2d(a, j, 2 * px + py, c).start()
        for a in range(n):
            for j, (px, py) in enumerate(chips):
                ici(a, j, 2 * x + y, (px, py, c)).wait_send()
        token[...] = jnp.zeros_like(token)

    res = _async_call(body, bufs, list(sems), after, True, True, name=name)
    return res[:n], res[n:7 * n], res[7 * n]


def gather_wait(bufs, sems, after, *, name):
    n = len(bufs)

    def body(*refs):
        s_in = refs[n:7 * n]
        outs = refs[7 * n + 1:]
        x, y, c, chips, _, d2d = _gather_copies(outs, None, (s_in[:3 * n], s_in[3 * n:]))
        for a in range(n):
            for j, (px, py) in enumerate(chips):
                d2d(a, j, 2 * px + py, 1 - c).wait_recv()
                d2d(a, j, 2 * px + py, c).wait_send()

    return _async_call(body, bufs, list(sems), after, False, False, name=name)


def _pair_plan(srcs, lands):
    x, y, c, _ = _place()
    return [(srcs[a].at[j, 1 - c], lands[a].at[j], (x, y, 1 - c))
            for a in range(len(srcs)) for j in range(N_SHARD)]


def _chip_plan(srcs, lands):
    x, y, c, chips = _place()
    return [(srcs[a].at[2 * px + py], lands[a].at[2 * x + y], (px, py, c))
            for a in range(len(srcs)) for (px, py) in chips]


def copies_start(arrs, land_shapes, plan, per_array, *, name):
    n = len(arrs)
    k = per_array * n

    def body(*refs):
        lands = refs[2 * n:3 * n]
        sems, token = refs[3 * n:3 * n + 2 * k], refs[3 * n + 2 * k]
        for i, (src, dst, dev) in enumerate(plan(refs[n:2 * n], lands)):
            pltpu.make_async_remote_copy(src_ref=src, dst_ref=dst, send_sem=sems[i], recv_sem=sems[k + i],
                                         device_id=dev, device_id_type=MESH).start()
        token[...] = jnp.zeros_like(token)

    res = pl.pallas_call(
        body, name=name,
        in_specs=[_ANY] * n,
        out_specs=[_ANY] * (2 * n) + [_SEM] * (2 * k) + [_VMEM],
        out_shape=[jax.ShapeDtypeStruct(t.shape, t.dtype) for t in arrs] + list(land_shapes)
        + [pltpu.SemaphoreType.DMA(())] * (2 * k) + [_TOKEN],
        input_output_aliases={a: a for a in range(n)},
        compiler_params=pltpu.CompilerParams(has_side_effects=_EFFECT),
    )(*arrs)
    return res[:n], res[n:2 * n], res[2 * n:2 * n + 2 * k], res[2 * n + 2 * k]


def copies_wait(arrs, lands, sems, after, plan, *, name):
    n = len(arrs)
    k = len(sems) // 2

    def body(*refs):
        s_in = refs[2 * n:2 * n + 2 * k]
        outs = refs[2 * n + 2 * k + 1:]
        for i, (src, dst, dev) in enumerate(plan(outs[:n], outs[n:])):
            cp = pltpu.make_async_remote_copy(src_ref=src, dst_ref=dst, send_sem=s_in[i], recv_sem=s_in[k + i],
                                              device_id=dev, device_id_type=MESH)
            cp.wait_send()
            cp.wait_recv()

    res = pl.pallas_call(
        body, name=name,
        in_specs=[_ANY] * (2 * n) + [_SEM] * (2 * k) + [_ANY],
        out_specs=[_ANY] * (2 * n),
        out_shape=[jax.ShapeDtypeStruct(t.shape, t.dtype) for t in list(arrs) + list(lands)],
        input_output_aliases={a: a for a in range(2 * n)},
        compiler_params=pltpu.CompilerParams(has_side_effects=_EFFECT),
    )(*arrs, *lands, *sems, after)
    return res[:n], res[n:]


def pair_send_halves(arrs, *, name):
    n = len(arrs)

    def body(*refs):
        ins, outs = refs[:n], refs[n:2 * n]
        send_sems, recv_sems = refs[2 * n:]
        x, y, c, _ = _place()
        copies = []
        for a in range(n):
            for j in range(N_SHARD):
                cp = pltpu.make_async_remote_copy(
                    src_ref=ins[a].at[j, 1 - c], dst_ref=outs[a].at[j], send_sem=send_sems.at[a, j],
                    recv_sem=recv_sems.at[a, j], device_id=(x, y, 1 - c), device_id_type=MESH)
                cp.start()
                copies.append(cp)
        for cp in copies:
            cp.wait()

    return pl.pallas_call(
        body, name=name,
        in_specs=[_ANY] * n, out_specs=[_ANY] * n,
        out_shape=[jax.ShapeDtypeStruct((N_SHARD,) + t.shape[2:], t.dtype) for t in arrs],
        scratch_shapes=[pltpu.SemaphoreType.DMA((n, N_SHARD)), pltpu.SemaphoreType.DMA((n, N_SHARD))],
    )(*arrs)


def chip_scatter(arrs, *, name):
    n = len(arrs)

    def body(*refs):
        ins, outs = refs[:n], refs[n:2 * n]
        send_sems, recv_sems = refs[2 * n:]
        x, y, c, chips = _place()
        me = 2 * x + y
        copies = []
        for a in range(n):
            for j, (px, py) in enumerate(chips):
                cp = pltpu.make_async_remote_copy(
                    src_ref=ins[a].at[2 * px + py], dst_ref=outs[a].at[me], send_sem=send_sems.at[a, j],
                    recv_sem=recv_sems.at[a, j], device_id=(px, py, c), device_id_type=MESH)
                cp.start()
                copies.append(cp)
        for cp in copies:
            cp.wait()

    return pl.pallas_call(
        body, name=name,
        in_specs=[_ANY] * n, out_specs=[_ANY] * n,
        out_shape=[jax.ShapeDtypeStruct(t.shape, t.dtype) for t in arrs],
        scratch_shapes=[pltpu.SemaphoreType.DMA((n, 3)), pltpu.SemaphoreType.DMA((n, 3))],
    )(*arrs)


def pair_gather(bufs, *, name):
    n = len(bufs)

    def body(*refs):
        outs = refs[n:2 * n]
        send_sems, recv_sems = refs[2 * n:]
        x, y, c, _ = _place()
        copies = []
        for a in range(n):
            cp = pltpu.make_async_remote_copy(
                src_ref=outs[a].at[c], dst_ref=outs[a].at[c], send_sem=send_sems.at[a],
                recv_sem=recv_sems.at[a], device_id=(x, y, 1 - c), device_id_type=MESH)
            cp.start()
            copies.append(cp)
        for cp in copies:
            cp.wait()

    return pl.pallas_call(
        body, name=name,
        in_specs=[_ANY] * n, out_specs=[_ANY] * n,
        out_shape=[jax.ShapeDtypeStruct(t.shape, t.dtype) for t in bufs],
        input_output_aliases={a: a for a in range(n)},
        scratch_shapes=[pltpu.SemaphoreType.DMA((n,)), pltpu.SemaphoreType.DMA((n,))],
    )(*bufs)


def all_sum_small(v, *, name):
    m = v.shape[0]

    def body(v_ref, o_ref, buf, send_sems, recv_sems, local_sem):
        x, y, c, chips = _place()
        me, sibling = (x, y, c), (x, y, 1 - c)

        def rows(px, py, pc):
            return buf.at[pl.ds((4 * px + 2 * py + pc) * m, m), :]

        def copy(k, block, to, src=None):
            return pltpu.make_async_remote_copy(
                src_ref=rows(*block) if src is None else src, dst_ref=rows(*block),
                send_sem=send_sems.at[k], recv_sem=recv_sems.at[k], device_id=to, device_id_type=MESH)

        mine = pltpu.make_async_copy(v_ref, rows(*me), local_sem)
        mine.start()
        first = [copy(0, me, sibling, src=v_ref)]
        first += [copy(1 + j, me, (*chip, c), src=v_ref) for j, chip in enumerate(chips)]
        for cp in first:
            cp.start()
        passed = [copy(4 + j, (*chip, c), sibling) for j, chip in enumerate(chips)]
        for j, chip in enumerate(chips):
            copy(1 + j, (*chip, c), me).wait_recv()
            passed[j].start()
        copy(0, sibling, me).wait_recv()
        for j, chip in enumerate(chips):
            copy(4 + j, (*chip, 1 - c), me).wait_recv()
        for cp in first + passed:
            cp.wait_send()
        mine.wait()
        acc = buf[pl.ds(0, m), :]
        for k in range(1, 8):
            acc = acc + buf[pl.ds(k * m, m), :]
        o_ref[...] = acc

    vm = pl.BlockSpec(memory_space=pltpu.VMEM)
    return pl.pallas_call(
        body, name=name, in_specs=[vm], out_specs=vm,
        out_shape=jax.ShapeDtypeStruct((m, 128), F32),
        scratch_shapes=[pltpu.VMEM((8 * m, 128), F32), pltpu.SemaphoreType.DMA((7,)),
                        pltpu.SemaphoreType.DMA((7,)), pltpu.SemaphoreType.DMA],
    )(v)


def pack_slab(srcs, blocks, maps, n_slots, slot_shape, place, *, name):
    k = len(srcs)
    per = n_slots // k

    def body(place_ref, *refs):
        o_ref = refs[k]
        s = pl.program_id(0)
        for i in range(k):
            @pl.when(jnp.logical_and(s >= i * per, s < (i + 1) * per))
            def _():
                o_ref[...] = refs[i][...].astype(BF16)

    return pl.pallas_call(
        body, name=name,
        grid_spec=pltpu.PrefetchScalarGridSpec(
            num_scalar_prefetch=1, grid=(n_slots,),
            in_specs=[pl.BlockSpec(blk, functools.partial(lambda s, pr, m: m(s), m=m)) for blk, m in zip(blocks, maps)],
            out_specs=pl.BlockSpec((None, None) + slot_shape, lambda s, pr: (pr[1], s, 0, 0))),
        out_shape=jax.ShapeDtypeStruct((N_SHARD, n_slots) + slot_shape, BF16),
        compiler_params=_params("arbitrary"),
    )(place, *srcs)


def sum_halves(arr, recv, c_idx, *, name):
    _, _, r, c = arr.shape
    tr = r
    while tr * c * 4 > (1 << 20) and tr % 32 == 0:
        tr //= 2

    def body(c_ref, a_ref, b_ref, o_ref):
        o_ref[...] = (a_ref[...] + b_ref[...]).astype(BF16)

    return pl.pallas_call(
        body, name=name,
        grid_spec=pltpu.PrefetchScalarGridSpec(
            num_scalar_prefetch=1, grid=(N_SHARD, r // tr),
            in_specs=[pl.BlockSpec((None, None, tr, c), lambda j, i, cr: (j, cr[0], i, 0)),
                      pl.BlockSpec((None, tr, c), lambda j, i, cr: (j, i, 0))],
            out_specs=pl.BlockSpec((None, tr, c), lambda j, i, cr: (j, i, 0))),
        out_shape=jax.ShapeDtypeStruct((N_SHARD, r, c), BF16),
        compiler_params=_params("parallel", "parallel"),
    )(c_idx, arr, recv)


def sum_chips(recv, part, place, *, name):
    _, r, c = recv.shape
    tr = r
    while tr * c * 4 > (1 << 20) and tr % 32 == 0:
        tr //= 2

    def body(place_ref, a0, a1, a2, a3, p_ref, o_ref):
        me = place_ref[1]
        acc = None
        for k, a_ref in enumerate((a0, a1, a2, a3)):
            term = jnp.where(me == k, p_ref[...], a_ref[...]).astype(F32)
            acc = term if acc is None else acc + term
        o_ref[...] = acc

    def slab(k):
        return pl.BlockSpec((None, tr, c), lambda i, pr: (jnp.where(pr[1] == k, (k + 1) % N_SHARD, k), i, 0))

    return pl.pallas_call(
        body, name=name,
        grid_spec=pltpu.PrefetchScalarGridSpec(
            num_scalar_prefetch=1, grid=(r // tr,),
            in_specs=[slab(k) for k in range(N_SHARD)]
            + [pl.BlockSpec((None, tr, c), lambda i, pr: (pr[1], i, 0))],
            out_specs=pl.BlockSpec((None, tr, c), lambda i, pr: (pr[0], i, 0))),
        out_shape=jax.ShapeDtypeStruct((2, r, c), F32),
        compiler_params=_params("parallel"),
    )(place, recv, recv, recv, recv, part)


_WEIGHTS = ("ff1_norm", "ff1_wg", "ff1_wu", "ff1_wd", "mix_norm", "w_in", "sgu_ln_g", "sgu_ln_b", "sgu_w",
            "sgu_b", "lru_conv_w", "lru_conv_b", "lru_wa", "lru_ba", "lru_wx", "lru_bx", "lru_lambda",
            "gdn_conv_w", "gdn_a_log", "gdn_dt_bias", "gdn_norm_g", "pool_w", "pool_scale", "w_branch",
            "w_out", "ff2_norm", "ff2_wg", "ff2_wu", "ff2_wd", "final_norm")
_BIG = ("ff1_wg", "ff1_wu", "ff1_wd", "w_in", "w_branch", "w_out", "ff2_wg", "ff2_wu", "ff2_wd")
_SMALL = tuple(n for n in _WEIGHTS if n not in _BIG)
_CONV = ("lru_conv_w", "gdn_conv_w")


def _seg_rows(shape):
    return -(-math.prod(shape) // 1024) * 8


def _pack(arrs):
    segs = []
    for t in arrs:
        rows = _seg_rows(t.shape)
        flat = t.reshape(-1)
        segs.append(jnp.pad(flat, (0, rows * 128 - flat.shape[0])).reshape(rows, 128))
    return jnp.concatenate(segs, axis=0)


def _unpack(buf, shapes):
    out, o = [], 0
    for shp in shapes:
        rows = _seg_rows(shp)
        out.append(buf[o:o + rows].reshape(-1)[:math.prod(shp)].reshape(shp))
        o += rows
    return out


def _block_diag(w):
    h, n, _ = w.shape
    same = jnp.arange(h)[:, None, None, None] == jnp.arange(h)[None, None, :, None]
    return jnp.where(same, w[:, :, None, :], 0.0).reshape(h * n, h * n)


def _diag_blocks(m, h=8, n=64):
    return jnp.stack([m[i * n:(i + 1) * n, i * n:(i + 1) * n] for i in range(h)])


def _lane_row(v):
    return jnp.zeros((1, 128), F32).at[0, 4:8].set(v)


def _w_in_to_padded(raw):
    sem = jnp.concatenate([raw[k, :, :P_IN_SHARD] for k in range(N_SHARD)], axis=1)
    zero = jnp.zeros((sem.shape[0], C_GATE - C_BA - 8), sem.dtype)
    return jnp.concatenate([sem[:, 2048:4096], sem[:, 0:2048], sem[:, 4104:4616], sem[:, 4096:4104], zero,
                            sem[:, 4616:P_IN]], axis=1)


def _w_in_grad_to_shards(part):
    r = part.shape[1]
    dwp = part.transpose(1, 0, 2).reshape(r, P_PAD)
    sem = jnp.concatenate([dwp[:, C_AU:C_DX], dwp[:, 0:C_AU], dwp[:, C_BA:C_BA + 8], dwp[:, C_DX:C_BA],
                           dwp[:, C_GATE:]], axis=1)
    sh = sem.reshape(r, N_SHARD, P_IN_SHARD).transpose(1, 0, 2)
    return jnp.pad(sh, ((0, 0), (0, 0), (0, P_IN_SHARD_PAD - P_IN_SHARD)))


def kernel(x, ff1_norm, ff1_wg, ff1_wu, ff1_wd, mix_norm, w_in, sgu_ln_g, sgu_ln_b, sgu_w, sgu_b, lru_conv_w,
           lru_conv_b, lru_wa, lru_ba, lru_wx, lru_bx, lru_lambda, gdn_conv_w, gdn_a_log, gdn_dt_bias, gdn_norm_g,
           pool_w, pool_scale, w_branch, w_out, ff2_norm, ff2_wg, ff2_wu, ff2_wd, final_norm, loss_target, m_ff1_norm,
           m_ff1_wg, m_ff1_wu, m_ff1_wd, m_mix_norm, m_w_in, m_sgu_ln_g, m_sgu_ln_b, m_sgu_w, m_sgu_b, m_lru_conv_w,
           m_lru_conv_b, m_lru_wa, m_lru_ba, m_lru_wx, m_lru_bx, m_lru_lambda, m_gdn_conv_w, m_gdn_a_log, m_gdn_dt_bias,
           m_gdn_norm_g, m_pool_w, m_pool_scale, m_w_branch, m_w_out, m_ff2_norm, m_ff2_wg, m_ff2_wu, m_ff2_wd,
           m_final_norm, v_ff1_norm, v_ff1_wg, v_ff1_wu, v_ff1_wd, v_mix_norm, v_w_in, v_sgu_ln_g, v_sgu_ln_b, v_sgu_w,
           v_sgu_b, v_lru_conv_w, v_lru_conv_b, v_lru_wa, v_lru_ba, v_lru_wx, v_lru_bx, v_lru_lambda, v_gdn_conv_w,
           v_gdn_a_log, v_gdn_dt_bias, v_gdn_norm_g, v_pool_w, v_pool_scale, v_w_branch, v_w_out, v_ff2_norm, v_ff2_wg,
           v_ff2_wu, v_ff2_wd, v_final_norm):
    a = dict(locals())
    n_layer = ff1_norm.shape[0]
    d = D_MODEL
    x0 = x[0]
    chip = 2 * lax.axis_index("x") + lax.axis_index("y")
    c_idx = jnp.reshape(lax.axis_index("c"), (1,)).astype(jnp.int32)
    place = jnp.stack([lax.axis_index("c"), chip]).astype(jnp.int32)

    def own_slab(t, dtype=BF16):
        buf = lax.empty((N_SHARD,) + t.shape, dtype)
        return lax.dynamic_update_slice(buf, t.astype(dtype)[None], (chip,) + (0,) * t.ndim)

    assert n_layer == 2

    def layer_bufs(l):
        fc = FF_SHARD
        whole = lambda s: (l, 0, 0)
        wgu = pack_slab([ff1_wg, ff1_wu, ff2_wg, ff2_wu], [(None, d, fc)] * 4, [whole] * 4, 4, (d, fc), place,
                        name=f"pack_wgu_{l}")
        wd = pack_slab([ff1_wd, ff2_wd], [(None, fc, d)] * 2, [whole] * 2, 2, (fc, d), place, name=f"pack_wd_{l}")
        win = pack_slab([w_in], [(None, d // 2, P_IN_SHARD_PAD)], [lambda s: (l, s, 0)], 2,
                        (d // 2, P_IN_SHARD_PAD), place, name=f"pack_w_in_{l}")
        wb = pack_slab([w_branch], [(None, None, BRANCH_W, d // 4)], [lambda s: (l, s, 0, 0)], 4,
                       (BRANCH_W, d // 4), place, name=f"pack_wb_{l}")
        wo = pack_slab([w_out], [(None, d // 8, d)], [lambda s: (l, s, 0)], 2, (d // 8, d), place,
                       name=f"pack_wo_{l}")
        return [wgu.reshape(N_SHARD, 2, 2, d, fc), wd, win, wb.reshape(N_SHARD, 2, 2, BRANCH_W, d // 4), wo]

    bufs = [layer_bufs(l) for l in range(n_layer)]
    *gathered0, g_lcw, g_gcw = gather_shards(bufs[0] + [own_slab(lru_conv_w, F32), own_slab(gdn_conv_w, F32)],
                                             name="gather_weights_0")
    first, bufs1 = lax.optimization_barrier((gathered0[0], bufs[1]))
    gathered = [[first] + gathered0[1:], None]
    in_flight, gather_sems, gather_token = gather_start(bufs1, name="gather_weights_1_start")

    def layer_params(l):
        g_wgu, g_wd, g_w_in, g_wb, g_wo = gathered[l]
        g_wgu = g_wgu.reshape(N_SHARD, 4, d, FF_SHARD)
        p = {}
        p["ff1"] = (g_wgu, g_wgu, g_wd, (0, 1, 0))
        p["ff2"] = (g_wgu, g_wgu, g_wd, (2, 3, 1))
        p["w_in"] = _w_in_to_padded(g_w_in.reshape(4, d, P_IN_SHARD_PAD))
        p["wb"] = g_wb.reshape(4, 4, BRANCH_W, d // 4).transpose(1, 2, 0, 3).reshape(4, BRANCH_W, d)
        p["wo"] = g_wo.reshape(d, d)
        p["lcw"] = g_lcw[:, l].transpose(1, 0, 2).reshape(4, BRANCH_W)
        p["gcw"] = g_gcw[:, l].transpose(1, 0, 2).reshape(4, 3 * BRANCH_W)
        p["lcb"] = lru_conv_b[l][None]
        p["dep1"] = gather_token if l == 0 else None
        p["bt"] = jnp.zeros((128, 128), F32).at[:, :4].set(sgu_b[l].T)
        p["wa"] = _block_diag(lru_wa[l])
        p["wx"] = _block_diag(lru_wx[l])
        p["alog"] = _lane_row(gdn_a_log[l])
        p["dt"] = _lane_row(gdn_dt_bias[l])
        return p

    saved = []
    xs = x0
    for l in range(n_layer):
        p = layer_params(l)
        sv = {"p": p, "x0": xs}
        x1 = ffn_fwd(xs, ff1_norm[l][None], *p["ff1"], tb=512, dep=p["dep1"], name=f"ffn1_fwd_{l}")
        h = norm_fwd(x1, mix_norm[l][None], tb=512, name=f"mix_norm_fwd_{l}")
        proj = matmul(h, p["w_in"], tm=1024, tn=1536, tk=1024, name=f"proj_{l}")
        ya = sgu_fwd(proj, sgu_ln_g[l][None], sgu_ln_b[l][None], sgu_w[l], p["bt"], tb=512, name=f"sgu_fwd_{l}")
        xc = conv_fwd(proj, C_BX, BRANCH_W, p["lcw"], p["lcb"], tb=512, name=f"lru_conv_fwd_{l}")
        yb, hl = lru_fwd(xc, proj, p["wa"], p["wx"], lru_ba[l][None], lru_bx[l][None], lru_lambda[l][None],
                         tb=256, name=f"lru_fwd_{l}")
        qkv = conv_fwd(proj, C_QKV, 3 * BRANCH_W, p["gcw"], jnp.zeros((1, 3 * BRANCH_W), F32), tb=512,
                       name=f"gdn_conv_fwd_{l}")
        yc, sall = gdn_fwd(qkv, proj, p["alog"], p["dt"], gdn_norm_g[l][None], cps=GDN_CPS, name=f"gdn_fwd_{l}")
        dep2 = None
        if l == 0:
            in_flight, gather_sems, dep2 = gather_pass(in_flight, gather_sems, yc, name="gather_weights_1_pass")
        yd = pool_fwd(proj, pool_w[l], pool_scale[l][None], tb=512, name=f"pool_fwd_{l}")
        ys = (ya, yb, yc, yd)
        x2, merged = merge_fwd(x1, ys, proj, p["wb"], p["wo"], tb=256, dep=dep2, name=f"merge_fwd_{l}")
        x3 = ffn_fwd(x2, ff2_norm[l][None], *p["ff2"], tb=512, name=f"ffn2_fwd_{l}")
        if l == 0:
            gathered[1] = gather_wait(in_flight, gather_sems, x3, name="gather_weights_1_wait")
        sv.update(x1=x1, h=h, proj=proj, xc=xc, hl=hl, qkv=qkv, sall=sall, ys=ys, merged=merged, x2=x2)
        saved.append(sv)
        xs = x3

    dx, d_final, loss_blk = loss_head(xs, final_norm[None], loss_target[0], tb=512, name="loss_head")
    loss = lax.psum(loss_blk[0, 0], ("x", "y", "c"))

    small_g = {n: [None] * n_layer for n in _SMALL if n != "final_norm"}
    big_g = [None] * n_layer
    per_layer = 7
    flight = None

    def pair_sums(bigs, recv, tag, w_in_at):
        part = [sum_halves(t, r, c_idx, name=f"grad_pair_sum_{tag}_{i}") for i, (t, r) in enumerate(zip(bigs, recv))]
        if w_in_at is not None:
            part[w_in_at] = _w_in_grad_to_shards(part[w_in_at])
        return part

    def pair_start(arrs, tag):
        return copies_start(arrs, [jax.ShapeDtypeStruct((N_SHARD,) + t.shape[2:], t.dtype) for t in arrs],
                            _pair_plan, N_SHARD, name=f"grad_pair_exchange_{tag}_start")

    def pair_to_chip(fl, after, tag, w_in_at=None):
        sent, recv = copies_wait(*fl[:3], after, _pair_plan, name=f"grad_pair_exchange_{tag}_wait")
        part = pair_sums(sent, recv, tag, w_in_at)
        return copies_start(part, [jax.ShapeDtypeStruct(t.shape, t.dtype) for t in part], _chip_plan, 3,
                            name=f"grad_chip_exchange_{tag}_start")

    def chip_finish(fl, after, tag):
        part, recv = copies_wait(*fl[:3], after, _chip_plan, name=f"grad_chip_exchange_{tag}_wait")
        return [sum_chips(r, q, place, name=f"grad_chip_sum_{tag}_{i}") for i, (r, q) in enumerate(zip(recv, part))]

    early = mid = None

    for l in reversed(range(n_layer)):
        sv = saved[l]
        p = sv["p"]
        proj = sv["proj"]
        dx2, dg_ff2, dab2, hid2, h2, dy2 = ffn_bwd(sv["x2"], dx, ff2_norm[l][None], *p["ff2"], tb=512,
                                                  dep=None if flight is None else flight[3],
                                                  name=f"ffn2_bwd_{l}")
        dwgu2 = matmul(h2, dab2, ta=True, tm=1024, tn=FF_SHARD, tk=TK_DW, name=f"ffn2_dwgu_{l}")
        dwd2 = matmul(hid2, dy2, ta=True, tm=FF_SHARD, tn=1024, tk=TK_DW, name=f"ffn2_dwd_{l}")
        if flight is not None:
            flight = pair_to_chip(flight, dx2, "1", w_in_at=2)
        dgp, dbr, dya, dyb, dyc, dyd, dx2b = merge_bwd(dx2, sv["ys"], proj, p["wb"], p["wo"], tb=256,
                                                       dep=None if flight is None else flight[3],
                                                       name=f"merge_bwd_{l}")
        dwo = matmul(sv["merged"], dx2b, ta=True, tm=1024, tn=1024, tk=TK_DW, name=f"dwo_{l}")
        dwb = jnp.stack([matmul(sv["ys"][g], dbr[g], ta=True, tm=BRANCH_W, tn=1024, tk=TK_DW, name=f"dwb{g}_{l}")
                         for g in range(4)])
        dwb_sh = dwb.reshape(4, BRANCH_W, N_SHARD, d // N_SHARD).transpose(2, 0, 1, 3)
        g_late = [dwgu2, dwd2.reshape(N_SHARD, 2, FF_SHARD // 2, d),
                  dwb_sh.reshape(N_SHARD, 2, 2 * BRANCH_W, d // N_SHARD), dwo.reshape(N_SHARD, 2, d // 8, d)]
        tok = None
        if l == 0:
            early = pair_start(g_late, "0a")
            tok = early[3]
        du, dv, dlg, dlb, dws, dbt = sgu_bwd(proj, dya, sgu_ln_g[l][None], sgu_ln_b[l][None], sgu_w[l], p["bt"],
                                             tb=512, dep=tok, name=f"sgu_bwd_{l}")
        dxc, dgate, dwa, dwx, dba, dbx, dlam = lru_bwd(sv["xc"], proj, sv["hl"], dyb, p["wa"], p["wx"],
                                                       lru_ba[l][None], lru_bx[l][None], lru_lambda[l][None],
                                                       tb=256, name=f"lru_bwd_{l}")
        dbx_in, dcw_l = conv_bwd(proj, C_BX, BRANCH_W, dxc, p["lcw"], tb=512, name=f"lru_conv_bwd_{l}")
        dqkv, dz, dbeta, dal, ddt, dng = gdn_bwd(sv["qkv"], proj, sv["sall"], dyc, p["alog"], p["dt"],
                                                 gdn_norm_g[l][None], cps=GDN_CPS, name=f"gdn_bwd_{l}")
        dqkv_in, dcw_g = conv_bwd(proj, C_QKV, 3 * BRANCH_W, dqkv, p["gcw"], tb=512, name=f"gdn_conv_bwd_{l}")
        ddx, dwp, dsc = pool_bwd(proj, dyd, pool_w[l], pool_scale[l][None], tb=512, name=f"pool_bwd_{l}")
        if l == 0:
            early = pair_to_chip(early, ddx, "0a")
            tok = early[3]
        s = dx.shape[0]
        dproj = jnp.concatenate([dqkv_in, dz, du, dv, dbx_in, dgate, ddx, dbeta,
                                 jnp.zeros((s, C_GATE - C_BA - 128), BF16), dgp], axis=1)
        dh = matmul(dproj, p["w_in"], tb=True, tm=1024, tn=1024, tk=1536, dep=tok, name=f"dh_mix_{l}")
        dw_in = matmul(sv["h"], dproj, ta=True, tm=1024, tn=1152, tk=TK_DW, col_groups=N_SHARD, name=f"dw_in_{l}")
        dw_in = dw_in.reshape(N_SHARD, 2, d // 2, P_PAD // N_SHARD)
        if l == 0:
            mid = pair_start([dw_in], "0b")
            tok = mid[3]
        dx1, dg_mix = norm_bwd(sv["x1"], mix_norm[l][None], dh, dx2, tb=512, dep=tok, name=f"mix_norm_bwd_{l}")
        dx, dg_ff1, dab1, hid1, h1, dy1 = ffn_bwd(sv["x0"], dx1, ff1_norm[l][None], *p["ff1"],
                                                 tb=512, name=f"ffn1_bwd_{l}")
        if l == 0:
            mid = pair_to_chip(mid, dx, "0b", w_in_at=0)
            tok = mid[3]
        dwgu1 = matmul(h1, dab1, ta=True, tm=1024, tn=FF_SHARD, tk=TK_DW, dep=tok, name=f"ffn1_dwgu_{l}")
        dwd1 = matmul(hid1, dy1, ta=True, tm=FF_SHARD, tn=1024, tk=TK_DW, name=f"ffn1_dwd_{l}")

        for n, g in (("ff1_norm", dg_ff1[0]), ("mix_norm", dg_mix[0]), ("sgu_ln_g", dlg[0]), ("sgu_ln_b", dlb[0]),
                     ("sgu_w", dws), ("sgu_b", dbt[:, :4].T), ("lru_conv_w", dcw_l[:4]), ("lru_conv_b", dcw_l[4]),
                     ("lru_wa", _diag_blocks(dwa)), ("lru_ba", dba[0]), ("lru_wx", _diag_blocks(dwx)),
                     ("lru_bx", dbx[0]), ("lru_lambda", dlam[0]), ("gdn_conv_w", dcw_g[:4]),
                     ("gdn_a_log", dal[0, 4:8]), ("gdn_dt_bias", ddt[0, 4:8]), ("gdn_norm_g", dng[0]),
                     ("pool_w", dwp), ("pool_scale", dsc[0]), ("ff2_norm", dg_ff2[0])):
            small_g[n][l] = g
        big_g[l] = [dwgu1, dwd1.reshape(N_SHARD, 2, FF_SHARD // 2, d), dw_in, g_late[2], g_late[3], g_late[0], g_late[1]]
        if l == n_layer - 1:
            flight = pair_start(big_g[l], "1")

    last = big_g[0][:2]
    recv = pair_send_halves(last, name="grad_pair_exchange_0c")
    part = pair_sums(last, recv, "0c", None)
    recv = chip_scatter(part, name="grad_chip_exchange_0c")
    red_c = [sum_chips(r, q, place, name=f"grad_chip_sum_0c_{i}") for i, (r, q) in enumerate(zip(recv, part))]
    red1 = chip_finish(flight, red_c[0], "1")
    red_a = chip_finish(early, red_c[0], "0a")
    red_b = chip_finish(mid, red_c[0], "0b")
    full = pair_gather(red_c + red_b + [red_a[2], red_a[3], red_a[0], red_a[1]] + red1, name="grad_pair_gather")
    grads = {}

    def layers(k, f):
        return jnp.stack([f(full[l * per_layer + k]) for l in range(n_layer)])

    grads["ff1_wg"] = layers(0, lambda t: t[0])
    grads["ff1_wu"] = layers(0, lambda t: t[1])
    grads["ff1_wd"] = layers(1, lambda t: t.reshape(FF_SHARD, d))
    grads["w_in"] = layers(2, lambda t: t.reshape(d, P_IN_SHARD_PAD)[:, :P_IN_SHARD])
    grads["w_branch"] = layers(3, lambda t: t.reshape(4, BRANCH_W, d // N_SHARD))
    grads["w_out"] = layers(4, lambda t: t.reshape(d // N_SHARD, d))
    grads["ff2_wg"] = layers(5, lambda t: t[0])
    grads["ff2_wu"] = layers(5, lambda t: t[1])
    grads["ff2_wd"] = layers(6, lambda t: t.reshape(FF_SHARD, d))

    small_full = [jnp.stack(small_g[n]) for n in _SMALL if n != "final_norm"] + [d_final[0]]
    summed = _unpack(all_sum_small(_pack(small_full), name="grad_small_allreduce"), [t.shape for t in small_full])
    for n, g in zip(_SMALL, summed):
        if n in _CONV:
            width = a[n].shape[-1]
            g = lax.dynamic_slice_in_dim(g, chip * width, width, axis=2)
        grads[n] = g

    delta, new_m, new_v = {}, {}, {}
    for n in _BIG:
        delta[n], new_m[n], new_v[n] = adamw(a[n], grads[n], a["m_" + n], a["v_" + n], name=f"adamw_{n}")
    shapes = [a[n].shape for n in _SMALL]
    packed = [_pack([src[pre + n] for n in _SMALL]) for src, pre in ((a, ""), (grads, ""), (a, "m_"), (a, "v_"))]
    for store, buf in zip((delta, new_m, new_v), adamw(*packed, name="adamw_small")):
        store.update(zip(_SMALL, _unpack(buf, shapes)))

    return (loss, dx[None], *[grads[n] for n in _WEIGHTS], *[delta[n] for n in _WEIGHTS],
            *[new_m[n] for n in _WEIGHTS], *[new_v[n] for n in _WEIGHTS])
```

```python
import functools
import math

import jax
import jax.numpy as jnp
from jax import lax
from jax.experimental import pallas as pl
from jax.experimental.pallas import tpu as pltpu

F32 = jnp.float32
BF16 = jnp.bfloat16
MESH = pl.DeviceIdType.MESH

D_MODEL = 1024
N_SHARD = 4
D_FF = 2816
FF_SHARD = D_FF // N_SHARD
BRANCH_W = 512
CHUNK = 64
GDN_HEADS = 4
GDN_DK = 128
GDN_CPS = 4
TK_DW = 2048
LRU_C = 8.0
EPS = 1e-6
P_IN = 8712
P_IN_SHARD = P_IN // N_SHARD
P_IN_SHARD_PAD = 2304
P_PAD = 9216
C_QKV, C_Z, C_AU, C_AV, C_BX, C_BG, C_DX, C_BA, C_GATE = 0, 1536, 2048, 2560, 3072, 3584, 4096, 4608, 5120

ADAM_LR, ADAM_B1, ADAM_B2, ADAM_EPS, ADAM_WD, ADAM_STEP = 0.001, 0.9, 0.999, 1e-08, 0.01, 10


def _gelu(x):
    return 0.5 * x * (1.0 + jnp.tanh(0.7978845608028654 * (x + 0.044715 * (x * x * x))))


def _gelu_grad(x):
    u = 0.7978845608028654 * (x + 0.044715 * (x * x * x))
    t = jnp.tanh(u)
    return 0.5 * (1.0 + t) + 0.5 * x * (1.0 - t * t) * 0.7978845608028654 * (1.0 + 3.0 * 0.044715 * x * x)


def _silu(x):
    return x * jax.nn.sigmoid(x)


def _softplus(x):
    return jnp.maximum(x, 0.0) + jnp.log1p(jnp.exp(-jnp.abs(x)))


def _dg(a, b, ca, cb):
    return lax.dot_general(a.astype(BF16), b.astype(BF16), (((ca,), (cb,)), ((), ())),
                           preferred_element_type=F32)


def _dg3(a, b, ca, cb):
    a1 = a.astype(BF16)
    a2 = (a - a1.astype(F32)).astype(BF16)
    b1 = b.astype(BF16)
    b2 = (b - b1.astype(F32)).astype(BF16)
    dn = (((ca,), (cb,)), ((), ()))
    d = functools.partial(lax.dot_general, dimension_numbers=dn, preferred_element_type=F32)
    return d(a1, b1) + (d(a1, b2) + d(a2, b1))


def _make_mm(ca, cb, dot):
    @jax.custom_vjp
    def f(a, b):
        return dot(a, b, ca, cb)

    def fwd(a, b):
        return dot(a, b, ca, cb), (a, b)

    def bwd(res, g):
        a, b = res
        if (ca, cb) == (1, 0):
            return dot(g, b, 1, 1), dot(a, g, 0, 0)
        if (ca, cb) == (1, 1):
            return dot(g, b, 1, 0), dot(g, a, 0, 0)
        return dot(b, g, 1, 1), dot(a, g, 1, 0)

    f.defvjp(fwd, bwd)
    return f


mm_nn = _make_mm(1, 0, _dg)


def _bdg(a, b, ca, cb):
    return lax.dot_general(a.astype(BF16), b.astype(BF16), (((ca,), (cb,)), ((0,), (0,))),
                           preferred_element_type=F32)


def _split2(x):
    x1 = x.astype(BF16)
    return x1, (x - x1.astype(F32)).astype(BF16)


def _bdg3(a, b, ca, cb):
    a1, a2 = _split2(a)
    b1, b2 = _split2(b)
    d = functools.partial(lax.dot_general, dimension_numbers=(((ca,), (cb,)), ((0,), (0,))),
                          preferred_element_type=F32)
    return d(a1, b1) + (d(a1, b2) + d(a2, b1))


def _make_bmm(ca, cb):
    @jax.custom_vjp
    def f(a, b):
        return _bdg(a, b, ca, cb)

    def fwd(a, b):
        return _bdg(a, b, ca, cb), (a, b)

    def bwd(res, g):
        a, b = res
        if (ca, cb) == (2, 1):
            return _bdg(g, b, 2, 2), _bdg(a, g, 1, 1)
        if (ca, cb) == (2, 2):
            return _bdg(g, b, 2, 1), _bdg(g, a, 1, 1)
        return _bdg(b, g, 2, 2), _bdg(a, g, 2, 1)

    f.defvjp(fwd, bwd)
    return f


bmm_nn = _make_bmm(2, 1)
bmm_nt = _make_bmm(2, 2)
bmm_tn = _make_bmm(1, 1)


def _iota3(shape, dim):
    return lax.broadcasted_iota(jnp.int32, shape, dim)


@jax.custom_vjp
def _tri_inv(aw):
    b, n, _ = aw.shape
    shp = (b, n, 2 * n)
    left = _iota3(shp, 2) < n
    x = jnp.where(_iota3(shp, 1) == _iota3(shp, 2), 1.0, 0.0) - aw
    zero = jnp.zeros((b, n, 4 * n), BF16)
    k = 1
    while k < n:
        x1, x2 = _split2(x)
        lhs = jnp.concatenate([x1, x2], axis=1)
        rhs = jnp.concatenate([zero, jnp.concatenate([x1, x2], axis=2)], axis=1)
        r = lax.dot_general(lhs, rhs, (((2,), (1,)), ((0,), (0,))), preferred_element_type=F32)
        mx = (r[:, :n, :2 * n] + r[:, :n, 2 * n:]) + (r[:, n:, :2 * n] + r[:, n:, 2 * n:])
        x = jnp.where(left, x, 0.0) + mx
        k *= 2
    return jnp.where(left, x, 0.0)


def _tri_inv_fwd(aw):
    t = _tri_inv(aw)
    return t, t


def _tri_inv_bwd(t, dt):
    n = t.shape[1]
    x = _bdg3(t, dt, 1, 1)[:, :n]
    tp = jnp.concatenate([jnp.zeros_like(t), t], axis=1)
    return (-_bdg3(x, tp, 2, 2),)


_tri_inv.defvjp(_tri_inv_fwd, _tri_inv_bwd)


def _rows(shape):
    return lax.broadcasted_iota(jnp.int32, shape, 0)


def _lanes(shape):
    return lax.broadcasted_iota(jnp.int32, shape, 1)


def _col(x, j):
    return jnp.sum(jnp.where(_lanes(x.shape) == j, x, 0.0), axis=1, keepdims=True)


def _put_col(col, j, width):
    shape = (col.shape[0], width)
    return jnp.where(_lanes(shape) == j, jnp.broadcast_to(col, shape), 0.0)


VMEM_BYTES_V7X = 64 << 20


def _params(*sem, vmem_bytes=None):
    return pltpu.CompilerParams(dimension_semantics=sem, vmem_limit_bytes=vmem_bytes)


def matmul(a, b, *, ta=False, tb=False, tm, tn, tk, out_dtype=F32, col_groups=1, dep=None, name):
    ba, bb = a.shape[:-2], b.shape[:-2]
    batch = ba if len(ba) >= len(bb) else bb
    assert ba in ((), batch) and bb in ((), batch)
    nb = len(batch)
    (m, k) = (a.shape[-1], a.shape[-2]) if ta else (a.shape[-2], a.shape[-1])
    (k2, n) = (b.shape[-1], b.shape[-2]) if tb else (b.shape[-2], b.shape[-1])
    tm, tn, tk = min(tm, m), min(tn, n), min(tk, k)
    assert k == k2 and m % tm == 0 and n % tn == 0 and k % tk == 0, (a.shape, b.shape, tm, tn, tk)
    nk = k // tk
    grid = batch + (m // tm, n // tn, nk)

    def a_map(*g):
        i, kk = g[nb], g[nb + 2]
        return (g[:nb] if ba else ()) + ((kk, i) if ta else (i, kk))

    def b_map(*g):
        j, kk = g[nb + 1], g[nb + 2]
        return (g[:nb] if bb else ()) + ((j, kk) if tb else (kk, j))

    per_group = n // tn // col_groups
    assert col_groups == 1 or (nb == 0 and per_group * col_groups * tn == n)

    def o_map(*g):
        if col_groups > 1:
            return (g[1] // per_group, g[0], g[1] % per_group)
        return g[:nb] + (g[nb], g[nb + 1])

    a_blk = (None,) * len(ba) + ((tk, tm) if ta else (tm, tk))
    b_blk = (None,) * len(bb) + ((tn, tk) if tb else (tk, tn))
    o_blk = (None,) * (nb + (col_groups > 1)) + (tm, tn)
    o_shape = (col_groups, m, n // col_groups) if col_groups > 1 else batch + (m, n)

    dep_specs, dep_args = _dep_operand(dep)

    def body(a_ref, b_ref, *rest):
        o_ref, acc_ref = rest[len(dep_args):]
        kk = pl.program_id(nb + 2)

        @pl.when(kk == 0)
        def _():
            acc_ref[...] = jnp.zeros_like(acc_ref)

        acc_ref[...] += _dg(a_ref[...], b_ref[...], 0 if ta else 1, 1 if tb else 0)

        @pl.when(kk == nk - 1)
        def _():
            o_ref[...] = acc_ref[...].astype(o_ref.dtype)

    return pl.pallas_call(
        body, name=name, grid=grid,
        in_specs=[pl.BlockSpec(a_blk, a_map), pl.BlockSpec(b_blk, b_map)] + dep_specs,
        out_specs=pl.BlockSpec(o_blk, o_map),
        out_shape=jax.ShapeDtypeStruct(o_shape, out_dtype),
        scratch_shapes=[pltpu.VMEM((tm, tn), F32)],
        compiler_params=_params(*(("parallel",) * (nb + 2) + ("arbitrary",))),
    )(a, b, *dep_args)


def norm_fwd(x, g, *, tb, name):
    s, d = x.shape

    def body(x_ref, g_ref, h_ref):
        xv = x_ref[...]
        r = lax.rsqrt(jnp.mean(xv * xv, axis=-1, keepdims=True) + EPS)
        h_ref[...] = (xv * r * g_ref[...]).astype(BF16)

    return pl.pallas_call(
        body, name=name, grid=(s // tb,),
        in_specs=[pl.BlockSpec((tb, d), lambda i: (i, 0)), pl.BlockSpec((1, d), lambda i: (0, 0))],
        out_specs=pl.BlockSpec((tb, d), lambda i: (i, 0)),
        out_shape=jax.ShapeDtypeStruct((s, d), BF16),
        compiler_params=_params("parallel"),
    )(x, g)


def _rms_bwd(xv, gv, dh):
    r = lax.rsqrt(jnp.mean(xv * xv, axis=-1, keepdims=True) + EPS)
    xh = xv * r
    dxh = dh * gv
    dx = r * (dxh - xh * jnp.mean(dxh * xh, axis=-1, keepdims=True))
    return dx, jnp.sum(dh * xh, axis=0, keepdims=True)


def norm_bwd(x, g, dh, dres, *, tb, dep=None, name):
    s, d = x.shape
    dep_specs, dep_args = _dep_operand(dep)

    def body(x_ref, g_ref, dh_ref, dres_ref, *rest):
        dx_ref, dg_ref = rest[len(dep_args):]
        dx, dg = _rms_bwd(x_ref[...], g_ref[...], dh_ref[...])
        dx_ref[...] = dres_ref[...] + dx

        @pl.when(pl.program_id(0) == 0)
        def _():
            dg_ref[...] = jnp.zeros_like(dg_ref)

        dg_ref[...] += dg

    row = pl.BlockSpec((tb, d), lambda i: (i, 0))
    one = pl.BlockSpec((1, d), lambda i: (0, 0))
    return pl.pallas_call(
        body, name=name, grid=(s // tb,),
        in_specs=[row, one, row, row] + dep_specs, out_specs=[row, one],
        out_shape=[jax.ShapeDtypeStruct((s, d), F32), jax.ShapeDtypeStruct((1, d), F32)],
        compiler_params=_params("arbitrary"),
    )(x, g, dh, dres, *dep_args)


def _dep_operand(dep):
    return ([], []) if dep is None else ([_ANY], [dep])


def ffn_fwd(x, g, wg, wu, wd, layer, *, tb, dep=None, name):
    s, d = x.shape
    fc = wd.shape[-2]
    dep_specs, dep_args = _dep_operand(dep)

    def body(x_ref, g_ref, wg_ref, wu_ref, wd_ref, *rest):
        o_ref, h_scr, acc_scr = rest[len(dep_args):]
        j = pl.program_id(1)

        @pl.when(j == 0)
        def _():
            xv = x_ref[...]
            r = lax.rsqrt(jnp.mean(xv * xv, axis=-1, keepdims=True) + EPS)
            h_scr[...] = (xv * r * g_ref[...]).astype(BF16)
            acc_scr[...] = jnp.zeros_like(acc_scr)

        h = h_scr[...]
        a = _dg(h, wg_ref[...], 1, 0)
        b = _dg(h, wu_ref[...], 1, 0)
        acc_scr[...] += _dg(_silu(a) * b, wd_ref[...], 1, 0)

        @pl.when(j == N_SHARD - 1)
        def _():
            o_ref[...] = x_ref[...] + 0.5 * acc_scr[...]

    row = pl.BlockSpec((tb, d), lambda i, j: (i, 0))
    lg, lu, ld = layer
    return pl.pallas_call(
        body, name=name, grid=(s // tb, N_SHARD),
        in_specs=[row, pl.BlockSpec((1, d), lambda i, j: (0, 0)),
                  pl.BlockSpec((None, None, d, fc), lambda i, j: (j, lg, 0, 0)),
                  pl.BlockSpec((None, None, d, fc), lambda i, j: (j, lu, 0, 0)),
                  pl.BlockSpec((None, None, fc, d), lambda i, j: (j, ld, 0, 0))] + dep_specs,
        out_specs=row,
        out_shape=jax.ShapeDtypeStruct((s, d), F32),
        scratch_shapes=[pltpu.VMEM((tb, d), BF16), pltpu.VMEM((tb, d), F32)],
        compiler_params=_params("parallel", "arbitrary"),
    )(x, g, wg, wu, wd, *dep_args)


def ffn_bwd(x, dxo, g, wg, wu, wd, layer, *, tb, dep=None, name):
    s, d = x.shape
    fc = wd.shape[-2]
    dep_specs, dep_args = _dep_operand(dep)

    def body(x_ref, dxo_ref, g_ref, wg_ref, wu_ref, wd_ref, *rest):
        dx_ref, dg_ref, dab_ref, hid_ref, h_ref, dy_ref, dh_scr = rest[len(dep_args):]
        i, j = pl.program_id(0), pl.program_id(1)

        @pl.when(j == 0)
        def _():
            xv = x_ref[...]
            r = lax.rsqrt(jnp.mean(xv * xv, axis=-1, keepdims=True) + EPS)
            h_ref[...] = (xv * r * g_ref[...]).astype(BF16)
            dy_ref[...] = (0.5 * dxo_ref[...]).astype(BF16)
            dh_scr[...] = jnp.zeros_like(dh_scr)

        h = h_ref[...]
        a = _dg(h, wg_ref[...], 1, 0)
        b = _dg(h, wu_ref[...], 1, 0)
        sg = jax.nn.sigmoid(a)
        sl = a * sg
        dhid = _dg(dy_ref[...], wd_ref[...], 1, 1)
        da = (dhid * b * (sg * (1.0 + a * (1.0 - sg)))).astype(BF16)
        db = (dhid * sl).astype(BF16)
        dab_ref[0] = da
        dab_ref[1] = db
        hid_ref[...] = (sl * b).astype(BF16)
        dh_scr[...] += _dg(da, wg_ref[...], 1, 1) + _dg(db, wu_ref[...], 1, 1)

        @pl.when(jnp.logical_and(i == 0, j == 0))
        def _():
            dg_ref[...] = jnp.zeros_like(dg_ref)

        @pl.when(j == N_SHARD - 1)
        def _():
            dx, dg = _rms_bwd(x_ref[...], g_ref[...], dh_scr[...])
            dx_ref[...] = dxo_ref[...] + dx
            dg_ref[...] += dg

    row = pl.BlockSpec((tb, d), lambda i, j: (i, 0))
    one = pl.BlockSpec((1, d), lambda i, j: (0, 0))
    lg, lu, ld = layer
    return pl.pallas_call(
        body, name=name, grid=(s // tb, N_SHARD),
        in_specs=[row, row, one,
                  pl.BlockSpec((None, None, d, fc), lambda i, j: (j, lg, 0, 0)),
                  pl.BlockSpec((None, None, d, fc), lambda i, j: (j, lu, 0, 0)),
                  pl.BlockSpec((None, None, fc, d), lambda i, j: (j, ld, 0, 0))] + dep_specs,
        out_specs=[row, one,
                   pl.BlockSpec((None, 2, tb, fc), lambda i, j: (j, 0, i, 0)),
                   pl.BlockSpec((None, tb, fc), lambda i, j: (j, i, 0)),
                   row, row],
        out_shape=[jax.ShapeDtypeStruct((s, d), F32), jax.ShapeDtypeStruct((1, d), F32),
                   jax.ShapeDtypeStruct((N_SHARD, 2, s, fc), BF16),
                   jax.ShapeDtypeStruct((N_SHARD, s, fc), BF16),
                   jax.ShapeDtypeStruct((s, d), BF16), jax.ShapeDtypeStruct((s, d), BF16)],
        scratch_shapes=[pltpu.VMEM((tb, d), F32)],
        compiler_params=_params("arbitrary", "arbitrary",
                                vmem_bytes=min(VMEM_BYTES_V7X - (8 << 20),
                                               12 * d * fc + 44 * tb * d + 40 * tb * fc + (4 << 20))),
    )(x, dxo, g, wg, wu, wd, *dep_args)


def _pblk(tb, width, col0, rev_n=None):
    assert col0 % width == 0
    cb = col0 // width
    if rev_n is None:
        return pl.BlockSpec((tb, width), lambda i: (i, cb))
    return pl.BlockSpec((tb, width), lambda i: (rev_n - 1 - i, cb))


def _full(shape):
    return pl.BlockSpec(shape, lambda i: (0,) * len(shape))


def conv_fwd(src, col0, width, w, bias, *, tb, name):
    s = src.shape[0]
    cb = col0 // width

    def body(x_ref, p_ref, w_ref, b_ref, y_ref):
        i = pl.program_id(0)
        prev = jnp.where(i > 0, p_ref[...], 0.0)
        xe = jnp.concatenate([prev, x_ref[...]], axis=0)
        wv = w_ref[...]
        acc = b_ref[...] + wv[3:4] * xe[8:]
        for k in range(3):
            acc = acc + wv[k:k + 1] * pltpu.roll(xe, 3 - k, 0)[8:]
        y_ref[...] = acc

    return pl.pallas_call(
        body, name=name, grid=(s // tb,),
        in_specs=[_pblk(tb, width, col0),
                  pl.BlockSpec((8, width), lambda i: (jnp.maximum(i * (tb // 8) - 1, 0), cb)),
                  _full((4, width)), _full((1, width))],
        out_specs=pl.BlockSpec((tb, width), lambda i: (i, 0)),
        out_shape=jax.ShapeDtypeStruct((s, width), F32),
        compiler_params=_params("parallel"),
    )(src, src, w, bias)


def conv_bwd(src, col0, width, dpre, w, *, tb, name):
    s = src.shape[0]
    cb = col0 // width
    nt = s // tb

    def body(x_ref, p_ref, d_ref, n_ref, w_ref, dx_ref, dwb_ref):
        i = pl.program_id(0)
        d = d_ref[...]
        nxt = jnp.where(i < nt - 1, n_ref[...], 0.0)
        de = jnp.concatenate([d, nxt], axis=0)
        wv = w_ref[...]
        dx = wv[3:4] * d
        for k in range(3):
            dx = dx + wv[k:k + 1] * pltpu.roll(de, tb + 8 - (3 - k), 0)[:tb]
        dx_ref[...] = dx.astype(BF16)

        @pl.when(i == 0)
        def _():
            dwb_ref[...] = jnp.zeros_like(dwb_ref)

        prev = jnp.where(i > 0, p_ref[...], 0.0)
        xe = jnp.concatenate([prev, x_ref[...]], axis=0)
        dwb_ref[3:4, :] += jnp.sum(d * xe[8:], axis=0, keepdims=True)
        for k in range(3):
            dwb_ref[k:k + 1, :] += jnp.sum(d * pltpu.roll(xe, 3 - k, 0)[8:], axis=0, keepdims=True)
        dwb_ref[4:5, :] += jnp.sum(d, axis=0, keepdims=True)

    return pl.pallas_call(
        body, name=name, grid=(nt,),
        in_specs=[_pblk(tb, width, col0),
                  pl.BlockSpec((8, width), lambda i: (jnp.maximum(i * (tb // 8) - 1, 0), cb)),
                  pl.BlockSpec((tb, width), lambda i: (i, 0)),
                  pl.BlockSpec((8, width), lambda i: (jnp.minimum((i + 1) * (tb // 8), s // 8 - 1), 0)),
                  _full((4, width))],
        out_specs=[pl.BlockSpec((tb, width), lambda i: (i, 0)), _full((8, width))],
        out_shape=[jax.ShapeDtypeStruct((s, width), BF16), jax.ShapeDtypeStruct((8, width), F32)],
        compiler_params=_params("arbitrary"),
    )(src, src, dpre, dpre, w)


SGU_BLOCK = 128


def _sgu_pre(u, v, lg, lb):
    ug = _gelu(u)
    vg = _gelu(v)
    mu = jnp.mean(vg, axis=-1, keepdims=True)
    var = jnp.mean(jnp.square(vg - mu), axis=-1, keepdims=True)
    return ug, (vg - mu) * lax.rsqrt(var + EPS) * lg + lb


def _sgu_mask():
    shp = (SGU_BLOCK, SGU_BLOCK)
    return (_rows(shp) // CHUNK) >= (_lanes(shp) // CHUNK)


def sgu_fwd(proj, lg, lb, ws, bt, *, tb, name):
    s = proj.shape[0]
    w = BRANCH_W

    def body(u_ref, v_ref, lg_ref, lb_ref, w_ref, b_ref, y_ref):
        ug, vn = _sgu_pre(u_ref[...], v_ref[...], lg_ref[...], lb_ref[...])
        mask = _sgu_mask()
        for g in range(4):
            wm = jnp.where(mask, w_ref[g], 0.0)
            bcol = _col(b_ref[...], g)
            cs = slice(128 * g, 128 * g + 128)
            for n in range(tb // SGU_BLOCK):
                rs = slice(SGU_BLOCK * n, SGU_BLOCK * (n + 1))
                mixed = _dg(wm, vn[rs, cs], 1, 0) + bcol
                y_ref[rs, cs] = (ug[rs, cs] * mixed).astype(BF16)

    return pl.pallas_call(
        body, name=name, grid=(s // tb,),
        in_specs=[_pblk(tb, w, C_AU), _pblk(tb, w, C_AV), _full((1, w)), _full((1, w)),
                  _full((4, 128, 128)), _full((128, 128))],
        out_specs=pl.BlockSpec((tb, w), lambda i: (i, 0)),
        out_shape=jax.ShapeDtypeStruct((s, w), BF16),
        compiler_params=_params("parallel"),
    )(proj, proj, lg, lb, ws, bt)


def sgu_bwd(proj, dy, lg, lb, ws, bt, *, tb, dep=None, name):
    s = proj.shape[0]
    w = BRANCH_W
    dep_specs, dep_args = _dep_operand(dep)

    def body(u_ref, v_ref, dy_ref, lg_ref, lb_ref, w_ref, b_ref, *rest):
        du_ref, dv_ref, dlg_ref, dlb_ref, dw_ref, db_ref, dug_scr, dvn_scr = rest[len(dep_args):]

        @pl.when(pl.program_id(0) == 0)
        def _():
            dlg_ref[...] = jnp.zeros_like(dlg_ref)
            dlb_ref[...] = jnp.zeros_like(dlb_ref)
            dw_ref[...] = jnp.zeros_like(dw_ref)
            db_ref[...] = jnp.zeros_like(db_ref)

        (ug, vn), vf = jax.vjp(_sgu_pre, u_ref[...], v_ref[...], lg_ref[...], lb_ref[...])
        dyv = dy_ref[...]
        mask = _sgu_mask()
        for g in range(4):
            wm = jnp.where(mask, w_ref[g], 0.0)
            bcol = _col(b_ref[...], g)
            cs = slice(128 * g, 128 * g + 128)
            dwg = jnp.zeros((SGU_BLOCK, SGU_BLOCK), F32)
            dbc = jnp.zeros((SGU_BLOCK, 1), F32)
            for n in range(tb // SGU_BLOCK):
                rs = slice(SGU_BLOCK * n, SGU_BLOCK * (n + 1))
                blk = vn[rs, cs]
                mixed = _dg(wm, blk, 1, 0) + bcol
                dmix = dyv[rs, cs] * ug[rs, cs]
                dug_scr[rs, cs] = dyv[rs, cs] * mixed
                dvn_scr[rs, cs] = _dg(wm, dmix, 0, 0)
                dwg = dwg + _dg(dmix, blk, 1, 1)
                dbc = dbc + jnp.sum(dmix, axis=1, keepdims=True)
            dw_ref[g] += jnp.where(mask, dwg, 0.0)
            db_ref[...] += _put_col(dbc, g, 128)
        du, dv, dlg, dlb = vf((dug_scr[...], dvn_scr[...]))
        du_ref[...] = du.astype(BF16)
        dv_ref[...] = dv.astype(BF16)
        dlg_ref[...] += dlg
        dlb_ref[...] += dlb

    row = pl.BlockSpec((tb, w), lambda i: (i, 0))
    return pl.pallas_call(
        body, name=name, grid=(s // tb,),
        in_specs=[_pblk(tb, w, C_AU), _pblk(tb, w, C_AV), row, _full((1, w)), _full((1, w)),
                  _full((4, 128, 128)), _full((128, 128))] + dep_specs,
        out_specs=[row, row, _full((1, w)), _full((1, w)), _full((4, 128, 128)), _full((128, 128))],
        out_shape=[jax.ShapeDtypeStruct((s, w), BF16), jax.ShapeDtypeStruct((s, w), BF16),
                   jax.ShapeDtypeStruct((1, w), F32), jax.ShapeDtypeStruct((1, w), F32),
                   jax.ShapeDtypeStruct((4, 128, 128), F32), jax.ShapeDtypeStruct((128, 128), F32)],
        scratch_shapes=[pltpu.VMEM((tb, w), F32), pltpu.VMEM((tb, w), F32)],
        compiler_params=_params("arbitrary"),
    )(proj, proj, dy, lg, lb, ws, bt, *dep_args)


POOL_HALO = 16


def _pooled(xe, t0, tb):
    s2 = xe + pltpu.roll(xe, 1, 0)
    s4 = s2 + pltpu.roll(s2, 2, 0)
    s8 = s4 + pltpu.roll(s4, 4, 0)
    s16 = s8 + pltpu.roll(s8, 8, 0)
    t = (t0 + _rows((tb, 128))).astype(F32)
    outs = []
    for g, sw in enumerate((s2, s4, s8, s16)):
        cs = slice(128 * g, 128 * g + 128)
        cnt = jnp.minimum(t + 1.0, float(2 ** (g + 1)))
        outs.append(sw[POOL_HALO:, cs] / cnt - xe[POOL_HALO:, cs])
    return outs


def pool_fwd(proj, wp, scale, *, tb, name):
    s = proj.shape[0]
    w = BRANCH_W

    def body(x_ref, p_ref, w_ref, sc_ref, y_ref):
        i = pl.program_id(0)
        prev = jnp.where(i > 0, p_ref[...], 0.0)
        xe = jnp.concatenate([prev, x_ref[...]], axis=0)
        pooled = _pooled(xe, i * tb, tb)
        sc = sc_ref[...]
        for g in range(4):
            cs = slice(128 * g, 128 * g + 128)
            y_ref[:, cs] = (_dg(pooled[g], w_ref[g], 1, 0) * sc[:, cs]).astype(BF16)

    cb = C_DX // w
    return pl.pallas_call(
        body, name=name, grid=(s // tb,),
        in_specs=[_pblk(tb, w, C_DX),
                  pl.BlockSpec((POOL_HALO, w), lambda i: (jnp.maximum(i * (tb // POOL_HALO) - 1, 0), cb)),
                  _full((4, 128, 128)), _full((1, w))],
        out_specs=pl.BlockSpec((tb, w), lambda i: (i, 0)),
        out_shape=jax.ShapeDtypeStruct((s, w), BF16),
        compiler_params=_params("parallel"),
    )(proj, proj, wp, scale)


def pool_bwd(proj, dy, wp, scale, *, tb, name):
    s = proj.shape[0]
    w = BRANCH_W
    nt = s // tb
    te = tb + POOL_HALO

    def body(x_ref, p_ref, dy_ref, n_ref, w_ref, sc_ref, dx_ref, dw_ref, dsc_ref):
        i = pl.program_id(0)

        @pl.when(i == 0)
        def _():
            dw_ref[...] = jnp.zeros_like(dw_ref)
            dsc_ref[...] = jnp.zeros_like(dsc_ref)

        prev = jnp.where(i > 0, p_ref[...], 0.0)
        xe = jnp.concatenate([prev, x_ref[...]], axis=0)
        pooled = _pooled(xe, i * tb, tb)
        dyv = dy_ref[...]
        nxt = jnp.where(i < nt - 1, n_ref[...], 0.0)
        dye = jnp.concatenate([dyv, nxt], axis=0)
        sc = sc_ref[...]
        t = (i * tb + _rows((te, 128))).astype(F32)
        for g in range(4):
            cs = slice(128 * g, 128 * g + 128)
            win = 2 ** (g + 1)
            dpm = dye[:, cs] * sc[:, cs]
            dpool = _dg(dpm, w_ref[g], 1, 1)
            q = dpool / jnp.minimum(t + 1.0, float(win))
            r, sh = q, 1
            while sh < win:
                r = r + pltpu.roll(r, te - sh, 0)
                sh *= 2
            dx_ref[:, cs] = (r[:tb] - dpool[:tb]).astype(BF16)
            dw_ref[g] += _dg(pooled[g], dpm[:tb], 0, 0)
            dsc_ref[:, cs] += jnp.sum(dyv[:, cs] * _dg(pooled[g], w_ref[g], 1, 0), axis=0, keepdims=True)

    cb = C_DX // w
    row = pl.BlockSpec((tb, w), lambda i: (i, 0))
    return pl.pallas_call(
        body, name=name, grid=(nt,),
        in_specs=[_pblk(tb, w, C_DX),
                  pl.BlockSpec((POOL_HALO, w), lambda i: (jnp.maximum(i * (tb // POOL_HALO) - 1, 0), cb)),
                  row,
                  pl.BlockSpec((POOL_HALO, w),
                               lambda i: (jnp.minimum((i + 1) * (tb // POOL_HALO), s // POOL_HALO - 1), 0)),
                  _full((4, 128, 128)), _full((1, w))],
        out_specs=[row, _full((4, 128, 128)), _full((1, w))],
        out_shape=[jax.ShapeDtypeStruct((s, w), BF16), jax.ShapeDtypeStruct((4, 128, 128), F32),
                   jax.ShapeDtypeStruct((1, w), F32)],
        compiler_params=_params("arbitrary"),
    )(proj, proj, dy, dy, wp, scale)


def _neg_expm1(z):
    p = 1.0 + z * (1.0 / 9.0)
    for n in (8.0, 7.0, 6.0, 5.0, 4.0, 3.0, 2.0):
        p = 1.0 + z * (1.0 / n) * p
    return jnp.where(z > -0.5, -z * p, 1.0 - jnp.exp(z))


def _lru_gates(xc, wa, wx, ba, bx, lam):
    r = jax.nn.sigmoid(mm_nn(xc, wa) + ba)
    i = jax.nn.sigmoid(mm_nn(xc, wx) + bx)
    log_a = -LRU_C * r * _softplus(-lam)
    a = jnp.exp(log_a)
    return a, jnp.sqrt(_neg_expm1(2.0 * log_a)) * (i * xc)


def _scan_down(a, b):
    n = a.shape[0]
    row = _rows(a.shape)
    sh = 1
    while sh < n:
        keep = row >= sh
        a_sh = jnp.where(keep, pltpu.roll(a, sh, 0), 1.0)
        b_sh = jnp.where(keep, pltpu.roll(b, sh, 0), 0.0)
        b = a * b_sh + b
        a = a * a_sh
        sh *= 2
    return a, b


def _scan_up(a, b):
    n = a.shape[0]
    row = _rows(a.shape)
    sh = 1
    while sh < n:
        keep = row < n - sh
        a_sh = jnp.where(keep, pltpu.roll(a, n - sh, 0), 1.0)
        b_sh = jnp.where(keep, pltpu.roll(b, n - sh, 0), 0.0)
        b = a * b_sh + b
        a = a * a_sh
        sh *= 2
    return a, b


def lru_fwd(xc, proj, wa, wx, ba, bx, lam, *, tb, name):
    s = xc.shape[0]
    w = BRANCH_W

    def body(xc_ref, gate_ref, wa_ref, wx_ref, ba_ref, bx_ref, lam_ref, y_ref, h_ref, carry):
        @pl.when(pl.program_id(0) == 0)
        def _():
            carry[...] = jnp.zeros_like(carry)

        a, b = _lru_gates(xc_ref[...], wa_ref[...], wx_ref[...], ba_ref[...], bx_ref[...], lam_ref[...])
        pa, hb = _scan_down(a, b)
        h = pa * carry[...] + hb
        h_ref[...] = h
        carry[...] = h_ref[tb - 1:tb, :]
        y_ref[...] = (h * _gelu(gate_ref[...])).astype(BF16)

    row = pl.BlockSpec((tb, w), lambda i: (i, 0))
    return pl.pallas_call(
        body, name=name, grid=(s // tb,),
        in_specs=[row, _pblk(tb, w, C_BG), _full((w, w)), _full((w, w)),
                  _full((1, w)), _full((1, w)), _full((1, w))],
        out_specs=[row, row],
        out_shape=[jax.ShapeDtypeStruct((s, w), BF16), jax.ShapeDtypeStruct((s, w), F32)],
        scratch_shapes=[pltpu.VMEM((1, w), F32)],
        compiler_params=_params("arbitrary"),
    )(xc, proj, wa, wx, ba, bx, lam)


def lru_bwd(xc, proj, h, dy, wa, wx, ba, bx, lam, *, tb, name):
    s = xc.shape[0]
    w = BRANCH_W
    nt = s // tb

    def body(xc_ref, gate_ref, h_ref, hp_ref, dy_ref, wa_ref, wx_ref, ba_ref, bx_ref, lam_ref,
             dxc_ref, dgate_ref, dwa_ref, dwx_ref, dba_ref, dbx_ref, dlam_ref, carry):
        i = pl.program_id(0)

        @pl.when(i == 0)
        def _():
            carry[...] = jnp.zeros_like(carry)
            for r in (dwa_ref, dwx_ref, dba_ref, dbx_ref, dlam_ref):
                r[...] = jnp.zeros_like(r)

        (a, b), vf = jax.vjp(_lru_gates, xc_ref[...], wa_ref[...], wx_ref[...],
                             ba_ref[...], bx_ref[...], lam_ref[...])
        gate = gate_ref[...]
        dyv = dy_ref[...]
        hv = h_ref[...]
        row = _rows((tb, w))
        a_next = jnp.where(row < tb - 1, pltpu.roll(a, tb - 1, 0), 1.0)
        pa, xb = _scan_up(a_next, dyv * _gelu(gate))
        dh = xb + pa * carry[...]
        carry[...] = a[0:1, :] * dh[0:1, :]
        h_first = jnp.where(i < nt - 1, hp_ref[7:8, :], 0.0)
        h_prev = jnp.where(row >= 1, pltpu.roll(hv, 1, 0), h_first)
        dxc, dwa, dwx, dba, dbx, dlam = vf((dh * h_prev, dh))
        dxc_ref[...] = dxc
        dgate_ref[...] = (dyv * hv * _gelu_grad(gate)).astype(BF16)
        dwa_ref[...] += dwa
        dwx_ref[...] += dwx
        dba_ref[...] += dba
        dbx_ref[...] += dbx
        dlam_ref[...] += dlam

    row_spec = pl.BlockSpec((tb, w), lambda i: (nt - 1 - i, 0))
    hp_spec = pl.BlockSpec((8, w), lambda i: (jnp.maximum((nt - 1 - i) * (tb // 8) - 1, 0), 0))
    return pl.pallas_call(
        body, name=name, grid=(nt,),
        in_specs=[row_spec, _pblk(tb, w, C_BG, rev_n=nt), row_spec, hp_spec, row_spec,
                  _full((w, w)), _full((w, w)), _full((1, w)), _full((1, w)), _full((1, w))],
        out_specs=[row_spec, row_spec, _full((w, w)), _full((w, w)),
                   _full((1, w)), _full((1, w)), _full((1, w))],
        out_shape=[jax.ShapeDtypeStruct((s, w), F32), jax.ShapeDtypeStruct((s, w), BF16),
                   jax.ShapeDtypeStruct((w, w), F32), jax.ShapeDtypeStruct((w, w), F32),
                   jax.ShapeDtypeStruct((1, w), F32), jax.ShapeDtypeStruct((1, w), F32),
                   jax.ShapeDtypeStruct((1, w), F32)],
        scratch_shapes=[pltpu.VMEM((1, w), F32)],
        compiler_params=_params("arbitrary"),
    )(xc, proj, h, h, dy, wa, wx, ba, bx, lam)


def _gdn_g(ba, alog, dt):
    return -jnp.exp(alog) * _softplus(ba + dt)


def _gdn_local(qp, kp, vp, bpre, gc):
    b, c, dk = qp.shape
    q = _silu(qp)
    k = _silu(kp)
    v = _silu(vp)
    q = q * lax.rsqrt(jnp.sum(q * q, axis=-1, keepdims=True) + EPS) * (dk ** -0.5)
    k = k * lax.rsqrt(jnp.sum(k * k, axis=-1, keepdims=True) + EPS)
    beta = jax.nn.sigmoid(bpre)
    sq = (b, c, c)
    ii, jj = _iota3(sq, 1), _iota3(sq, 2)
    gr = jnp.sum(gc * (ii == jj).astype(F32), axis=1, keepdims=True)
    decay = jnp.exp(jnp.where(ii >= jj, gc - gr, -jnp.inf))
    wd = (b, c, 2 * c)
    wi, wj = _iota3(wd, 1), _iota3(wd, 2) - c
    gr_w = jnp.sum(gc * (wi == wj).astype(F32), axis=1, keepdims=True)
    decay_w = jnp.exp(jnp.where(jnp.logical_and(wj >= 0, wi > wj), gc - gr_w, -jnp.inf))
    kb = k * beta
    pad = jnp.zeros((b, c, dk), F32)
    t = _tri_inv(bmm_nt(kb, jnp.concatenate([pad, k], axis=1)) * decay_w)
    u = bmm_nn(t, jnp.concatenate([v * beta, pad], axis=1))
    wk = bmm_nn(t, jnp.concatenate([kb * jnp.exp(gc), pad], axis=1))
    attn = bmm_nt(q, k) * decay
    g_last = jnp.sum(jnp.where(_iota3((b, c, 1), 1) == c - 1, gc, 0.0), axis=1, keepdims=True)
    return u, wk, attn, q * jnp.exp(gc), k * jnp.exp(g_last - gc), jnp.exp(g_last)


def _gdn_recur(s, u, wk, attn, qg, kd, eg, zz, ng):
    v_new = u - bmm_nn(wk, s)
    o = bmm_nn(qg, s) + bmm_nn(attn, v_new)
    s_out = s * eg + bmm_tn(kd, v_new)
    o = o * lax.rsqrt(jnp.mean(o * o, axis=-1, keepdims=True) + EPS) * ng
    return o * _silu(zz), s_out


def _gdn_inputs(qkv_ref, ba, gcums, cps):
    c, hd = CHUNK, GDN_DK
    qs, ks, vs, bs, gs = [], [], [], [], []
    for ci in range(cps):
        rs = slice(c * ci, c * (ci + 1))
        for h in range(GDN_HEADS):
            qs.append(qkv_ref[rs, hd * h:hd * (h + 1)])
            ks.append(qkv_ref[rs, 512 + hd * h:512 + hd * (h + 1)])
            vs.append(qkv_ref[rs, 1024 + hd * h:1024 + hd * (h + 1)])
            bs.append(_col(ba[rs], h))
            gs.append(_col(gcums[ci], 4 + h))
    return tuple(jnp.stack(t) for t in (qs, ks, vs, bs, gs))


def _chunk_cumsum(g, transpose=False):
    c = CHUNK
    tri = (_rows((c, c)) >= _lanes((c, c))).astype(F32)
    return _dg3(tri, g, 0 if transpose else 1, 0)


def gdn_fwd(qkv, proj, alog, dt, ng, *, cps, name):
    s = qkv.shape[0]
    c = CHUNK
    nc = s // c
    hd = GDN_DK
    tb = c * cps

    def body(qkv_ref, z_ref, ba_ref, al_ref, dt_ref, ng_ref, y_ref, sall_ref, s_scr):
        @pl.when(pl.program_id(0) == 0)
        def _():
            s_scr[...] = jnp.zeros_like(s_scr)

        nh = GDN_HEADS
        ba = ba_ref[...]
        g = _gdn_g(ba, al_ref[...], dt_ref[...])
        gcums = [_chunk_cumsum(g[c * ci:c * (ci + 1)]) for ci in range(cps)]
        loc = _gdn_local(*_gdn_inputs(qkv_ref, ba, gcums, cps))
        for ci in range(cps):
            rs = slice(c * ci, c * (ci + 1))
            sl = slice(nh * ci, nh * (ci + 1))
            s_in = s_scr[...]
            sall_ref[ci] = s_in
            zz = jnp.stack([z_ref[rs, hd * h:hd * (h + 1)] for h in range(nh)])
            y, s_out = _gdn_recur(s_in, *(t[sl] for t in loc), zz, ng_ref[...])
            s_scr[...] = s_out
            for h in range(nh):
                y_ref[rs, hd * h:hd * (h + 1)] = y[h].astype(BF16)

    return pl.pallas_call(
        body, name=name, grid=(nc // cps,),
        in_specs=[pl.BlockSpec((tb, 1536), lambda i: (i, 0)), _pblk(tb, 512, C_Z), _pblk(tb, 128, C_BA),
                  _full((1, 128)), _full((1, 128)), _full((1, 128))],
        out_specs=[pl.BlockSpec((tb, 512), lambda i: (i, 0)),
                   pl.BlockSpec((cps, GDN_HEADS, hd, hd), lambda i: (i, 0, 0, 0))],
        out_shape=[jax.ShapeDtypeStruct((s, 512), BF16), jax.ShapeDtypeStruct((nc, GDN_HEADS, hd, hd), F32)],
        scratch_shapes=[pltpu.VMEM((GDN_HEADS, hd, hd), F32)],
        compiler_params=_params("arbitrary"),
    )(qkv, proj, proj, alog, dt, ng)


def gdn_bwd(qkv, proj, sall, dy, alog, dt, ng, *, cps, name):
    s = qkv.shape[0]
    c = CHUNK
    nc = s // c
    hd = GDN_DK
    tb = c * cps
    nt = nc // cps

    def body(qkv_ref, z_ref, ba_ref, sall_ref, dy_ref, al_ref, dt_ref, ng_ref,
             dqkv_ref, dz_ref, dba_ref, dal_ref, ddt_ref, dng_ref, ds_scr):
        @pl.when(pl.program_id(0) == 0)
        def _():
            ds_scr[...] = jnp.zeros_like(ds_scr)
            dal_ref[...] = jnp.zeros_like(dal_ref)
            ddt_ref[...] = jnp.zeros_like(ddt_ref)
            dng_ref[...] = jnp.zeros_like(dng_ref)

        nh = GDN_HEADS
        ba = ba_ref[...]
        g, gvf = jax.vjp(_gdn_g, ba, al_ref[...], dt_ref[...])
        gcums = [_chunk_cumsum(g[c * ci:c * (ci + 1)]) for ci in range(cps)]
        loc, loc_vf = jax.vjp(_gdn_local, *_gdn_inputs(qkv_ref, ba, gcums, cps))
        d_loc = [None] * cps
        dng = jnp.zeros((1, 128), F32)
        for ci in reversed(range(cps)):
            rs = slice(c * ci, c * (ci + 1))
            sl = slice(nh * ci, nh * (ci + 1))
            zz = jnp.stack([z_ref[rs, hd * h:hd * (h + 1)] for h in range(nh)])
            dyy = jnp.stack([dy_ref[rs, hd * h:hd * (h + 1)] for h in range(nh)])
            _, rvf = jax.vjp(_gdn_recur, sall_ref[ci], *(t[sl] for t in loc), zz, ng_ref[...])
            ds, du, dwk, dattn, dqg, dkd, deg, dzz, dn = rvf((dyy, ds_scr[...]))
            ds_scr[...] = ds
            d_loc[ci] = (du, dwk, dattn, dqg, dkd, deg)
            dng = dng + dn
            for h in range(nh):
                dz_ref[rs, hd * h:hd * (h + 1)] = dzz[h].astype(BF16)
        dq, dk, dv, dbp, dgc = loc_vf(tuple(jnp.concatenate([d_loc[ci][j] for ci in range(cps)], axis=0)
                                            for j in range(6)))
        dgs, dbas = [], []
        for ci in range(cps):
            rs = slice(c * ci, c * (ci + 1))
            dgcum = jnp.zeros((c, 128), F32)
            dba = jnp.zeros((c, 128), F32)
            for h in range(nh):
                b = nh * ci + h
                dqkv_ref[rs, hd * h:hd * (h + 1)] = dq[b]
                dqkv_ref[rs, 512 + hd * h:512 + hd * (h + 1)] = dk[b]
                dqkv_ref[rs, 1024 + hd * h:1024 + hd * (h + 1)] = dv[b]
                dgcum = dgcum + _put_col(dgc[b], 4 + h, 128)
                dba = dba + _put_col(dbp[b], h, 128)
            dgs.append(_chunk_cumsum(dgcum, transpose=True))
            dbas.append(dba)
        dba2, dal, ddt = gvf(jnp.concatenate(dgs, axis=0))
        dba_ref[...] = (jnp.concatenate(dbas, axis=0) + dba2).astype(BF16)
        dal_ref[...] += dal
        ddt_ref[...] += ddt
        dng_ref[...] += dng

    def rev(width):
        return pl.BlockSpec((tb, width), lambda i: (nt - 1 - i, 0))

    return pl.pallas_call(
        body, name=name, grid=(nt,),
        in_specs=[rev(1536), _pblk(tb, 512, C_Z, rev_n=nt), _pblk(tb, 128, C_BA, rev_n=nt),
                  pl.BlockSpec((cps, GDN_HEADS, hd, hd), lambda i: (nt - 1 - i, 0, 0, 0)), rev(512),
                  _full((1, 128)), _full((1, 128)), _full((1, 128))],
        out_specs=[rev(1536), rev(512), rev(128), _full((1, 128)), _full((1, 128)), _full((1, 128))],
        out_shape=[jax.ShapeDtypeStruct((s, 1536), F32), jax.ShapeDtypeStruct((s, 512), BF16),
                   jax.ShapeDtypeStruct((s, 128), BF16), jax.ShapeDtypeStruct((1, 128), F32),
                   jax.ShapeDtypeStruct((1, 128), F32), jax.ShapeDtypeStruct((1, 128), F32)],
        scratch_shapes=[pltpu.VMEM((GDN_HEADS, hd, hd), F32)],
        compiler_params=_params("arbitrary"),
    )(qkv, proj, proj, sall, dy, alog, dt, ng)


def merge_fwd(x, ys, proj, wb, wo, *, tb, dep=None, name):
    s, d = x.shape
    dep_specs, dep_args = _dep_operand(dep)

    def body(x_ref, ya, yb, yc, yd, g0, g1, g2, g3, wb_ref, wo_ref, *rest):
        o_ref, m_ref = rest[len(dep_args):]
        m = jnp.zeros((tb, d), F32)
        for g, (y_ref, gp_ref) in enumerate(zip((ya, yb, yc, yd), (g0, g1, g2, g3))):
            m = m + jax.nn.sigmoid(gp_ref[...]) * _dg(y_ref[...], wb_ref[g], 1, 0)
        mb = m.astype(BF16)
        m_ref[...] = mb
        o_ref[...] = x_ref[...] + _dg(mb, wo_ref[...], 1, 0)

    row = pl.BlockSpec((tb, d), lambda i: (i, 0))
    yrow = pl.BlockSpec((tb, BRANCH_W), lambda i: (i, 0))
    return pl.pallas_call(
        body, name=name, grid=(s // tb,),
        in_specs=[row] + [yrow] * 4 + [_pblk(tb, d, C_GATE + d * g) for g in range(4)]
        + [_full((4, BRANCH_W, d)), _full((d, d))] + dep_specs,
        out_specs=[row, row],
        out_shape=[jax.ShapeDtypeStruct((s, d), F32), jax.ShapeDtypeStruct((s, d), BF16)],
        compiler_params=_params("parallel"),
    )(x, *ys, proj, proj, proj, proj, wb, wo, *dep_args)


def merge_bwd(dx, ys, proj, wb, wo, *, tb, dep=None, name):
    s, d = dx.shape
    dep_specs, dep_args = _dep_operand(dep)

    def body(dx_ref, ya, yb, yc, yd, g0, g1, g2, g3, wb_ref, wo_ref, *rest):
        dgp_ref, dbr_ref, da, db, dc, dd, dxb_ref = rest[len(dep_args):]
        dxb = dx_ref[...].astype(BF16)
        dxb_ref[...] = dxb
        dm = _dg(dxb, wo_ref[...], 1, 1)
        for g, (y_ref, gp_ref, dy_ref) in enumerate(zip((ya, yb, yc, yd), (g0, g1, g2, g3), (da, db, dc, dd))):
            br = _dg(y_ref[...], wb_ref[g], 1, 0)
            sg = jax.nn.sigmoid(gp_ref[...])
            dgp_ref[:, d * g:d * (g + 1)] = (dm * br * sg * (1.0 - sg)).astype(BF16)
            dbr = (dm * sg).astype(BF16)
            dbr_ref[g] = dbr
            dy_ref[...] = _dg(dbr, wb_ref[g], 1, 1)

    row = pl.BlockSpec((tb, d), lambda i: (i, 0))
    yrow = pl.BlockSpec((tb, BRANCH_W), lambda i: (i, 0))
    return pl.pallas_call(
        body, name=name, grid=(s // tb,),
        in_specs=[row] + [yrow] * 4 + [_pblk(tb, d, C_GATE + d * g) for g in range(4)]
        + [_full((4, BRANCH_W, d)), _full((d, d))] + dep_specs,
        out_specs=[pl.BlockSpec((tb, 4 * d), lambda i: (i, 0)), pl.BlockSpec((4, tb, d), lambda i: (0, i, 0)),
                   yrow, yrow, yrow, yrow, row],
        out_shape=[jax.ShapeDtypeStruct((s, 4 * d), BF16), jax.ShapeDtypeStruct((4, s, d), BF16)]
        + [jax.ShapeDtypeStruct((s, BRANCH_W), F32)] * 4 + [jax.ShapeDtypeStruct((s, d), BF16)],
        compiler_params=_params("parallel"),
    )(dx, *ys, proj, proj, proj, proj, wb, wo, *dep_args)


def loss_head(x, g, target, *, tb, name):
    s, d = x.shape

    def body(x_ref, g_ref, t_ref, dx_ref, dg_ref, loss_ref):
        @pl.when(pl.program_id(0) == 0)
        def _():
            dg_ref[...] = jnp.zeros_like(dg_ref)
            loss_ref[...] = jnp.zeros_like(loss_ref)

        xv, gv = x_ref[...], g_ref[...]
        r = lax.rsqrt(jnp.mean(xv * xv, axis=-1, keepdims=True) + EPS)
        err = xv * r * gv - t_ref[...]
        loss_ref[...] += 0.5 * jnp.sum(jnp.mean(err * err, axis=-1, keepdims=True), axis=0, keepdims=True)
        dx, dg = _rms_bwd(xv, gv, err * (1.0 / d))
        dx_ref[...] = dx
        dg_ref[...] += dg

    row = pl.BlockSpec((tb, d), lambda i: (i, 0))
    return pl.pallas_call(
        body, name=name, grid=(s // tb,),
        in_specs=[row, _full((1, d)), row],
        out_specs=[row, _full((1, d)), _full((1, 128))],
        out_shape=[jax.ShapeDtypeStruct((s, d), F32), jax.ShapeDtypeStruct((1, d), F32),
                   jax.ShapeDtypeStruct((1, 128), F32)],
        compiler_params=_params("arbitrary"),
    )(x, g, target)


def adamw(w, g, m, v, *, name):
    shape = w.shape
    lead, (r, c) = shape[:-2], shape[-2:]
    tr = r
    while tr * c * 4 > (1 << 20) and tr % 16 == 0:
        tr //= 2
    c1 = 1.0 / (1.0 - ADAM_B1 ** ADAM_STEP)
    c2 = 1.0 / (1.0 - ADAM_B2 ** ADAM_STEP)

    def body(w_ref, g_ref, m_ref, v_ref, d_ref, nm_ref, nv_ref):
        gv = g_ref[...]
        nm = ADAM_B1 * m_ref[...] + (1.0 - ADAM_B1) * gv
        nv = ADAM_B2 * v_ref[...] + (1.0 - ADAM_B2) * (gv * gv)
        nm_ref[...] = nm
        nv_ref[...] = nv
        d_ref[...] = -ADAM_LR * ((nm * c1) / (jnp.sqrt(nv * c2) + ADAM_EPS) + ADAM_WD * w_ref[...])

    blk = pl.BlockSpec((None,) * len(lead) + (tr, c), lambda *i: i + (0,))
    return tuple(pl.pallas_call(
        body, name=name, grid=lead + (r // tr,),
        in_specs=[blk] * 4, out_specs=[blk] * 3,
        out_shape=[jax.ShapeDtypeStruct(shape, F32)] * 3,
        compiler_params=_params(*(("parallel",) * (len(lead) + 1))),
    )(w, g, m, v))


_ANY = pl.BlockSpec(memory_space=pl.ANY)


def _place():
    x, y, c = lax.axis_index("x"), lax.axis_index("y"), lax.axis_index("c")
    return x, y, c, [(1 - x, y), (x, 1 - y), (1 - x, 1 - y)]


def gather_shards(bufs, *, name):
    n = len(bufs)

    def body(*refs):
        outs = refs[n:2 * n]
        ici_send, ici_recv, d2d_send, d2d_recv = refs[2 * n:]
        x, y, c, chips = _place()
        me = 2 * x + y

        def ici(a, j, slab, to):
            return pltpu.make_async_remote_copy(
                src_ref=outs[a].at[slab, c], dst_ref=outs[a].at[slab, c], send_sem=ici_send.at[a, j],
                recv_sem=ici_recv.at[a, j], device_id=to, device_id_type=MESH)

        def d2d(a, j, slab, half):
            return pltpu.make_async_remote_copy(
                src_ref=outs[a].at[slab, half], dst_ref=outs[a].at[slab, half], send_sem=d2d_send.at[a, j],
                recv_sem=d2d_recv.at[a, j], device_id=(x, y, 1 - c), device_id_type=MESH)

        sends = []
        for a in range(n):
            for j, (px, py) in enumerate(chips):
                cp = ici(a, j, me, (px, py, c))
                cp.start()
                sends.append(cp)
        for a in range(n):
            for j, (px, py) in enumerate(chips):
                ici(a, j, 2 * px + py, (px, py, c)).wait_recv()
                cp = d2d(a, j, 2 * px + py, c)
                cp.start()
                sends.append(cp)
        for a in range(n):
            for j, (px, py) in enumerate(chips):
                d2d(a, j, 2 * px + py, 1 - c).wait_recv()
        for cp in sends:
            cp.wait_send()

    return pl.pallas_call(
        body, name=name,
        in_specs=[_ANY] * n, out_specs=[_ANY] * n,
        out_shape=[jax.ShapeDtypeStruct(t.shape, t.dtype) for t in bufs],
        input_output_aliases={a: a for a in range(n)},
        scratch_shapes=[pltpu.SemaphoreType.DMA((n, 3))] * 4,
    )(*bufs)


_SEM = pl.BlockSpec(memory_space=pltpu.SEMAPHORE)
_VMEM = pl.BlockSpec(memory_space=pltpu.VMEM)
_EFFECT = pltpu.SideEffectType.DATAFLOW_SIDE_EFFECTING
_TOKEN = jax.ShapeDtypeStruct((8, 128), F32)


def _gather_copies(outs, sems_ici, sems_d2d):
    x, y, c, chips = _place()

    def ici(a, j, slab, to):
        return pltpu.make_async_remote_copy(
            src_ref=outs[a].at[slab, c], dst_ref=outs[a].at[slab, c], send_sem=sems_ici[0][3 * a + j],
            recv_sem=sems_ici[1][3 * a + j], device_id=to, device_id_type=MESH)

    def d2d(a, j, slab, half):
        return pltpu.make_async_remote_copy(
            src_ref=outs[a].at[slab, half], dst_ref=outs[a].at[slab, half], send_sem=sems_d2d[0][3 * a + j],
            recv_sem=sems_d2d[1][3 * a + j], device_id=(x, y, 1 - c), device_id_type=MESH)

    return x, y, c, chips, ici, d2d


def _async_call(body, bufs, sems_in, after, sems_out, token, *, name):
    n = len(bufs)
    n_out = 6 * n if sems_out else 0
    extra = [] if after is None else [after]
    return pl.pallas_call(
        body, name=name,
        in_specs=[_ANY] * n + [_SEM] * len(sems_in) + [_ANY] * len(extra),
        out_specs=[_ANY] * n + [_SEM] * n_out + [_VMEM] * token,
        out_shape=[jax.ShapeDtypeStruct(t.shape, t.dtype) for t in bufs]
        + [pltpu.SemaphoreType.DMA(())] * n_out + [_TOKEN] * token,
        input_output_aliases={a: a for a in range(n)},
        compiler_params=pltpu.CompilerParams(has_side_effects=_EFFECT),
    )(*bufs, *sems_in, *extra)


def gather_start(bufs, *, name):
    n = len(bufs)

    def body(*refs):
        outs = refs[n:2 * n]
        sems, token = refs[2 * n:8 * n], refs[8 * n]
        x, y, c, chips, ici, _ = _gather_copies(outs, (sems[:3 * n], sems[3 * n:]), None)
        for a in range(n):
            for j, (px, py) in enumerate(chips):
                ici(a, j, 2 * x + y, (px, py, c)).start()
        token[...] = jnp.zeros_like(token)

    res = _async_call(body, bufs, [], None, True, True, name=name)
    return res[:n], res[n:7 * n], res[7 * n]


def gather_pass(bufs, sems, after, *, name):
    n = len(bufs)

    def body(*refs):
        s_in = refs[n:7 * n]
        outs = refs[7 * n + 1:8 * n + 1]
        s_out, token = refs[8 * n + 1:14 * n + 1], refs[14 * n + 1]
        x, y, c, chips, ici, d2d = _gather_copies(outs, (s_in[:3 * n], s_in[3 * n:]),
                                                  (s_out[:3 * n], s_out[3 * n:]))
        for a in range(n):
            for j, (px, py) in enumerate(chips):
                ici(a, j, 2 * px + py, (px, py, c)).wait_recv()
                d2d(a, j, 2 * px + py, c).start()
        for a in range(n):
            for j, (px, py) in enumerate(chips):
                ici(a, j, 2 * x + y, (px, py, c)).wait_send()
        token[...] = jnp.zeros_like(token)

    res = _async_call(body, bufs, list(sems), after, True, True, name=name)
    return res[:n], res[n:7 * n], res[7 * n]


def gather_wait(bufs, sems, after, *, name):
    n = len(bufs)

    def body(*refs):
        s_in = refs[n:7 * n]
        outs = refs[7 * n + 1:]
        x, y, c, chips, _, d2d = _gather_copies(outs, None, (s_in[:3 * n], s_in[3 * n:]))
        for a in range(n):
            for j, (px, py) in enumerate(chips):
                d2d(a, j, 2 * px + py, 1 - c).wait_recv()
                d2d(a, j, 2 * px + py, c).wait_send()

    return _async_call(body, bufs, list(sems), after, False, False, name=name)


def _pair_plan(srcs, lands):
    x, y, c, _ = _place()
    return [(srcs[a].at[j, 1 - c], lands[a].at[j], (x, y, 1 - c))
            for a in range(len(srcs)) for j in range(N_SHARD)]


def _chip_plan(srcs, lands):
    x, y, c, chips = _place()
    return [(srcs[a].at[2 * px + py], lands[a].at[2 * x + y], (px, py, c))
            for a in range(len(srcs)) for (px, py) in chips]


def copies_start(arrs, land_shapes, plan, per_array, *, name):
    n = len(arrs)
    k = per_array * n

    def body(*refs):
        lands = refs[2 * n:3 * n]
        sems, token = refs[3 * n:3 * n + 2 * k], refs[3 * n + 2 * k]
        for i, (src, dst, dev) in enumerate(plan(refs[n:2 * n], lands)):
            pltpu.make_async_remote_copy(src_ref=src, dst_ref=dst, send_sem=sems[i], recv_sem=sems[k + i],
                                         device_id=dev, device_id_type=MESH).start()
        token[...] = jnp.zeros_like(token)

    res = pl.pallas_call(
        body, name=name,
        in_specs=[_ANY] * n,
        out_specs=[_ANY] * (2 * n) + [_SEM] * (2 * k) + [_VMEM],
        out_shape=[jax.ShapeDtypeStruct(t.shape, t.dtype) for t in arrs] + list(land_shapes)
        + [pltpu.SemaphoreType.DMA(())] * (2 * k) + [_TOKEN],
        input_output_aliases={a: a for a in range(n)},
        compiler_params=pltpu.CompilerParams(has_side_effects=_EFFECT),
    )(*arrs)
    return res[:n], res[n:2 * n], res[2 * n:2 * n + 2 * k], res[2 * n + 2 * k]


def copies_wait(arrs, lands, sems, after, plan, *, name):
    n = len(arrs)
    k = len(sems) // 2

    def body(*refs):
        s_in = refs[2 * n:2 * n + 2 * k]
        outs = refs[2 * n + 2 * k + 1:]
        for i, (src, dst, dev) in enumerate(plan(outs[:n], outs[n:])):
            cp = pltpu.make_async_remote_copy(src_ref=src, dst_ref=dst, send_sem=s_in[i], recv_sem=s_in[k + i],
                                              device_id=dev, device_id_type=MESH)
            cp.wait_send()
            cp.wait_recv()

    res = pl.pallas_call(
        body, name=name,
        in_specs=[_ANY] * (2 * n) + [_SEM] * (2 * k) + [_ANY],
        out_specs=[_ANY] * (2 * n),
        out_shape=[jax.ShapeDtypeStruct(t.shape, t.dtype) for t in list(arrs) + list(lands)],
        input_output_aliases={a: a for a in range(2 * n)},
        compiler_params=pltpu.CompilerParams(has_side_effects=_EFFECT),
    )(*arrs, *lands, *sems, after)
    return res[:n], res[n:]


def pair_send_halves(arrs, *, name):
    n = len(arrs)

    def body(*refs):
        ins, outs = refs[:n], refs[n:2 * n]
        send_sems, recv_sems = refs[2 * n:]
        x, y, c, _ = _place()
        copies = []
        for a in range(n):
            for j in range(N_SHARD):
                cp = pltpu.make_async_remote_copy(
                    src_ref=ins[a].at[j, 1 - c], dst_ref=outs[a].at[j], send_sem=send_sems.at[a, j],
                    recv_sem=recv_sems.at[a, j], device_id=(x, y, 1 - c), device_id_type=MESH)
                cp.start()
                copies.append(cp)
        for cp in copies:
            cp.wait()

    return pl.pallas_call(
        body, name=name,
        in_specs=[_ANY] * n, out_specs=[_ANY] * n,
        out_shape=[jax.ShapeDtypeStruct((N_SHARD,) + t.shape[2:], t.dtype) for t in arrs],
        scratch_shapes=[pltpu.SemaphoreType.DMA((n, N_SHARD)), pltpu.SemaphoreType.DMA((n, N_SHARD))],
    )(*arrs)


def chip_scatter(arrs, *, name):
    n = len(arrs)

    def body(*refs):
        ins, outs = refs[:n], refs[n:2 * n]
        send_sems, recv_sems = refs[2 * n:]
        x, y, c, chips = _place()
        me = 2 * x + y
        copies = []
        for a in range(n):
            for j, (px, py) in enumerate(chips):
                cp = pltpu.make_async_remote_copy(
                    src_ref=ins[a].at[2 * px + py], dst_ref=outs[a].at[me], send_sem=send_sems.at[a, j],
                    recv_sem=recv_sems.at[a, j], device_id=(px, py, c), device_id_type=MESH)
                cp.start()
                copies.append(cp)
        for cp in copies:
            cp.wait()

    return pl.pallas_call(
        body, name=name,
        in_specs=[_ANY] * n, out_specs=[_ANY] * n,
        out_shape=[jax.ShapeDtypeStruct(t.shape, t.dtype) for t in arrs],
        scratch_shapes=[pltpu.SemaphoreType.DMA((n, 3)), pltpu.SemaphoreType.DMA((n, 3))],
    )(*arrs)


def pair_gather(bufs, *, name):
    n = len(bufs)

    def body(*refs):
        outs = refs[n:2 * n]
        send_sems, recv_sems = refs[2 * n:]
        x, y, c, _ = _place()
        copies = []
        for a in range(n):
            cp = pltpu.make_async_remote_copy(
                src_ref=outs[a].at[c], dst_ref=outs[a].at[c], send_sem=send_sems.at[a],
                recv_sem=recv_sems.at[a], device_id=(x, y, 1 - c), device_id_type=MESH)
            cp.start()
            copies.append(cp)
        for cp in copies:
            cp.wait()

    return pl.pallas_call(
        body, name=name,
        in_specs=[_ANY] * n, out_specs=[_ANY] * n,
        out_shape=[jax.ShapeDtypeStruct(t.shape, t.dtype) for t in bufs],
        input_output_aliases={a: a for a in range(n)},
        scratch_shapes=[pltpu.SemaphoreType.DMA((n,)), pltpu.SemaphoreType.DMA((n,))],
    )(*bufs)


def all_sum_small(v, *, name):
    m = v.shape[0]

    def body(v_ref, o_ref, buf, send_sems, recv_sems, local_sem):
        x, y, c, chips = _place()
        me, sibling = (x, y, c), (x, y, 1 - c)

        def rows(px, py, pc):
            return buf.at[pl.ds((4 * px + 2 * py + pc) * m, m), :]

        def copy(k, block, to, src=None):
            return pltpu.make_async_remote_copy(
                src_ref=rows(*block) if src is None else src, dst_ref=rows(*block),
                send_sem=send_sems.at[k], recv_sem=recv_sems.at[k], device_id=to, device_id_type=MESH)

        mine = pltpu.make_async_copy(v_ref, rows(*me), local_sem)
        mine.start()
        first = [copy(0, me, sibling, src=v_ref)]
        first += [copy(1 + j, me, (*chip, c), src=v_ref) for j, chip in enumerate(chips)]
        for cp in first:
            cp.start()
        passed = [copy(4 + j, (*chip, c), sibling) for j, chip in enumerate(chips)]
        for j, chip in enumerate(chips):
            copy(1 + j, (*chip, c), me).wait_recv()
            passed[j].start()
        copy(0, sibling, me).wait_recv()
        for j, chip in enumerate(chips):
            copy(4 + j, (*chip, 1 - c), me).wait_recv()
        for cp in first + passed:
            cp.wait_send()
        mine.wait()
        acc = buf[pl.ds(0, m), :]
        for k in range(1, 8):
            acc = acc + buf[pl.ds(k * m, m), :]
        o_ref[...] = acc

    vm = pl.BlockSpec(memory_space=pltpu.VMEM)
    return pl.pallas_call(
        body, name=name, in_specs=[vm], out_specs=vm,
        out_shape=jax.ShapeDtypeStruct((m, 128), F32),
        scratch_shapes=[pltpu.VMEM((8 * m, 128), F32), pltpu.SemaphoreType.DMA((7,)),
                        pltpu.SemaphoreType.DMA((7,)), pltpu.SemaphoreType.DMA],
    )(v)


def pack_slab(srcs, blocks, maps, n_slots, slot_shape, *, place, dep=None, name):
    k = len(srcs)
    per = n_slots // k
    dep_specs, dep_args = _dep_operand(dep)

    def body(place_ref, *refs):
        o_ref = refs[k + len(dep_args)]
        s = pl.program_id(0)
        for i in range(k):
            @pl.when(jnp.logical_and(s >= i * per, s < (i + 1) * per))
            def _():
                o_ref[...] = refs[i][...].astype(BF16)

    return pl.pallas_call(
        body, name=name,
        grid_spec=pltpu.PrefetchScalarGridSpec(
            num_scalar_prefetch=1, grid=(n_slots,),
            in_specs=[pl.BlockSpec(blk, functools.partial(lambda s, pr, m: m(s), m=m)) for blk, m in zip(blocks, maps)]
            + dep_specs,
            out_specs=pl.BlockSpec((None, None) + slot_shape, lambda s, pr: (pr[1], s, 0, 0))),
        out_shape=jax.ShapeDtypeStruct((N_SHARD, n_slots) + slot_shape, BF16),
        compiler_params=_params("arbitrary"),
    )(place, *srcs, *dep_args)


def sum_halves(arr, recv, c_idx, *, name):
    _, _, r, c = arr.shape
    tr = r
    while tr * c * 4 > (1 << 20) and tr % 32 == 0:
        tr //= 2

    def body(c_ref, a_ref, b_ref, o_ref):
        o_ref[...] = (a_ref[...] + b_ref[...]).astype(BF16)

    return pl.pallas_call(
        body, name=name,
        grid_spec=pltpu.PrefetchScalarGridSpec(
            num_scalar_prefetch=1, grid=(N_SHARD, r // tr),
            in_specs=[pl.BlockSpec((None, None, tr, c), lambda j, i, cr: (j, cr[0], i, 0)),
                      pl.BlockSpec((None, tr, c), lambda j, i, cr: (j, i, 0))],
            out_specs=pl.BlockSpec((None, tr, c), lambda j, i, cr: (j, i, 0))),
        out_shape=jax.ShapeDtypeStruct((N_SHARD, r, c), BF16),
        compiler_params=_params("parallel", "parallel"),
    )(c_idx, arr, recv)


def sum_chips(recv, part, place, *, name):
    _, r, c = recv.shape
    tr = r
    while tr * c * 4 > (1 << 20) and tr % 32 == 0:
        tr //= 2

    def body(place_ref, a0, a1, a2, a3, p_ref, o_ref):
        me = place_ref[1]
        acc = None
        for k, a_ref in enumerate((a0, a1, a2, a3)):
            term = jnp.where(me == k, p_ref[...], a_ref[...]).astype(F32)
            acc = term if acc is None else acc + term
        o_ref[...] = acc

    def slab(k):
        return pl.BlockSpec((None, tr, c), lambda i, pr: (jnp.where(pr[1] == k, (k + 1) % N_SHARD, k), i, 0))

    return pl.pallas_call(
        body, name=name,
        grid_spec=pltpu.PrefetchScalarGridSpec(
            num_scalar_prefetch=1, grid=(r // tr,),
            in_specs=[slab(k) for k in range(N_SHARD)]
            + [pl.BlockSpec((None, tr, c), lambda i, pr: (pr[1], i, 0))],
            out_specs=pl.BlockSpec((None, tr, c), lambda i, pr: (pr[0], i, 0))),
        out_shape=jax.ShapeDtypeStruct((2, r, c), F32),
        compiler_params=_params("parallel"),
    )(place, recv, recv, recv, recv, part)


_WEIGHTS = ("ff1_norm", "ff1_wg", "ff1_wu", "ff1_wd", "mix_norm", "w_in", "sgu_ln_g", "sgu_ln_b", "sgu_w",
            "sgu_b", "lru_conv_w", "lru_conv_b", "lru_wa", "lru_ba", "lru_wx", "lru_bx", "lru_lambda",
            "gdn_conv_w", "gdn_a_log", "gdn_dt_bias", "gdn_norm_g", "pool_w", "pool_scale", "w_branch",
            "w_out", "ff2_norm", "ff2_wg", "ff2_wu", "ff2_wd", "final_norm")
_BIG = ("ff1_wg", "ff1_wu", "ff1_wd", "w_in", "w_branch", "w_out", "ff2_wg", "ff2_wu", "ff2_wd")
_SMALL = tuple(n for n in _WEIGHTS if n not in _BIG)
_CONV = ("lru_conv_w", "gdn_conv_w")


def _seg_rows(shape):
    return -(-math.prod(shape) // 1024) * 8


def _pack(arrs):
    segs = []
    for t in arrs:
        rows = _seg_rows(t.shape)
        flat = t.reshape(-1)
        segs.append(jnp.pad(flat, (0, rows * 128 - flat.shape[0])).reshape(rows, 128))
    return jnp.concatenate(segs, axis=0)


def _unpack(buf, shapes):
    out, o = [], 0
    for shp in shapes:
        rows = _seg_rows(shp)
        out.append(buf[o:o + rows].reshape(-1)[:math.prod(shp)].reshape(shp))
        o += rows
    return out


def _block_diag(w):
    h, n, _ = w.shape
    same = jnp.arange(h)[:, None, None, None] == jnp.arange(h)[None, None, :, None]
    return jnp.where(same, w[:, :, None, :], 0.0).reshape(h * n, h * n)


def _diag_blocks(m, h=8, n=64):
    return jnp.stack([m[i * n:(i + 1) * n, i * n:(i + 1) * n] for i in range(h)])


def _lane_row(v):
    return jnp.zeros((1, 128), F32).at[0, 4:8].set(v)


def _w_in_to_padded(raw):
    sem = jnp.concatenate([raw[k, :, :P_IN_SHARD] for k in range(N_SHARD)], axis=1)
    zero = jnp.zeros((sem.shape[0], C_GATE - C_BA - 8), sem.dtype)
    return jnp.concatenate([sem[:, 2048:4096], sem[:, 0:2048], sem[:, 4104:4616], sem[:, 4096:4104], zero,
                            sem[:, 4616:P_IN]], axis=1)


def _w_in_grad_to_shards(part):
    r = part.shape[1]
    dwp = part.transpose(1, 0, 2).reshape(r, P_PAD)
    sem = jnp.concatenate([dwp[:, C_AU:C_DX], dwp[:, 0:C_AU], dwp[:, C_BA:C_BA + 8], dwp[:, C_DX:C_BA],
                           dwp[:, C_GATE:]], axis=1)
    sh = sem.reshape(r, N_SHARD, P_IN_SHARD).transpose(1, 0, 2)
    return jnp.pad(sh, ((0, 0), (0, 0), (0, P_IN_SHARD_PAD - P_IN_SHARD)))


def kernel(x, ff1_norm, ff1_wg, ff1_wu, ff1_wd, mix_norm, w_in, sgu_ln_g, sgu_ln_b, sgu_w, sgu_b, lru_conv_w,
           lru_conv_b, lru_wa, lru_ba, lru_wx, lru_bx, lru_lambda, gdn_conv_w, gdn_a_log, gdn_dt_bias, gdn_norm_g,
           pool_w, pool_scale, w_branch, w_out, ff2_norm, ff2_wg, ff2_wu, ff2_wd, final_norm, loss_target, m_ff1_norm,
           m_ff1_wg, m_ff1_wu, m_ff1_wd, m_mix_norm, m_w_in, m_sgu_ln_g, m_sgu_ln_b, m_sgu_w, m_sgu_b, m_lru_conv_w,
           m_lru_conv_b, m_lru_wa, m_lru_ba, m_lru_wx, m_lru_bx, m_lru_lambda, m_gdn_conv_w, m_gdn_a_log, m_gdn_dt_bias,
           m_gdn_norm_g, m_pool_w, m_pool_scale, m_w_branch, m_w_out, m_ff2_norm, m_ff2_wg, m_ff2_wu, m_ff2_wd,
           m_final_norm, v_ff1_norm, v_ff1_wg, v_ff1_wu, v_ff1_wd, v_mix_norm, v_w_in, v_sgu_ln_g, v_sgu_ln_b, v_sgu_w,
           v_sgu_b, v_lru_conv_w, v_lru_conv_b, v_lru_wa, v_lru_ba, v_lru_wx, v_lru_bx, v_lru_lambda, v_gdn_conv_w,
           v_gdn_a_log, v_gdn_dt_bias, v_gdn_norm_g, v_pool_w, v_pool_scale, v_w_branch, v_w_out, v_ff2_norm, v_ff2_wg,
           v_ff2_wu, v_ff2_wd, v_final_norm):
    a = dict(locals())
    n_layer = ff1_norm.shape[0]
    d = D_MODEL
    x0 = x[0]
    chip = 2 * lax.axis_index("x") + lax.axis_index("y")
    c_idx = jnp.reshape(lax.axis_index("c"), (1,)).astype(jnp.int32)
    place = jnp.stack([lax.axis_index("c"), chip]).astype(jnp.int32)

    def own_slab(t, dtype=BF16):
        buf = lax.empty((N_SHARD,) + t.shape, dtype)
        return lax.dynamic_update_slice(buf, t.astype(dtype)[None], (chip,) + (0,) * t.ndim)

    assert n_layer == 2

    def layer_bufs(l, dep=None):
        fc = FF_SHARD
        whole = lambda s: (l, 0, 0)
        rows = lambda s: (l, s, 0)
        pack = functools.partial(pack_slab, place=place, dep=dep)
        bufs = []
        for tag, wg, wu, wd in (("ffn1", ff1_wg, ff1_wu, ff1_wd), ("ffn2", ff2_wg, ff2_wu, ff2_wd)):
            bufs.append([pack([wg, wu], [(None, d, fc)] * 2, [whole] * 2, 2, (d, fc), name=f"pack_{tag}_wgu_{l}"),
                         pack([wd], [(None, fc // 2, d)], [rows], 2, (fc // 2, d), name=f"pack_{tag}_wd_{l}")])
        win = pack([w_in], [(None, d // 2, P_IN_SHARD_PAD)], [rows], 2, (d // 2, P_IN_SHARD_PAD),
                   name=f"pack_w_in_{l}")
        wb = pack([w_branch], [(None, None, BRANCH_W, d // 4)], [lambda s: (l, s, 0, 0)], 4, (BRANCH_W, d // 4),
                  name=f"pack_wb_{l}")
        wo = pack([w_out], [(None, d // 8, d)], [rows], 2, (d // 8, d), name=f"pack_wo_{l}")
        return bufs[0] + [win, wb.reshape(N_SHARD, 2, 2, BRANCH_W, d // 4), wo] + bufs[1]

    n_buf = 7
    groups = {"ffn1": (0, 2), "mix": (2, 5), "ffn2": (5, 7)}

    def group_sems(sems, lo, hi):
        return list(sems[3 * lo:3 * hi]) + list(sems[3 * n_buf + 3 * lo:3 * n_buf + 3 * hi])

    def ffn_weights(wgu, wd):
        return (wgu, wgu, wd.reshape(N_SHARD, 1, FF_SHARD, d), (0, 1, 0))

    def mix_weights(g_w_in, g_wb, g_wo):
        return {"w_in": _w_in_to_padded(g_w_in.reshape(4, d, P_IN_SHARD_PAD)),
                "wb": g_wb.reshape(4, 4, BRANCH_W, d // 4).transpose(1, 2, 0, 3).reshape(4, BRANCH_W, d),
                "wo": g_wo.reshape(d, d)}

    g_lcw, g_gcw = gather_shards([own_slab(lru_conv_w, F32), own_slab(gdn_conv_w, F32)], name="gather_conv_weights")

    def small_params(l):
        return {"lcw": g_lcw[:, l].transpose(1, 0, 2).reshape(4, BRANCH_W),
                "gcw": g_gcw[:, l].transpose(1, 0, 2).reshape(4, 3 * BRANCH_W),
                "lcb": lru_conv_b[l][None],
                "bt": jnp.zeros((128, 128), F32).at[:, :4].set(sgu_b[l].T),
                "wa": _block_diag(lru_wa[l]), "wx": _block_diag(lru_wx[l]),
                "alog": _lane_row(gdn_a_log[l]), "dt": _lane_row(gdn_dt_bias[l])}

    fl0, sems0, tok = gather_start(layer_bufs(0), name="gather_weights_0_start")
    fl1 = sems1 = gathered1 = None
    saved = []
    xs = x0
    for l in range(n_layer):
        p = small_params(l)
        sv = {"p": p, "x0": xs}
        if l == 0:
            lo, hi = groups["ffn1"]
            got, sm, tk = gather_pass(fl0[lo:hi], group_sems(sems0, lo, hi), tok, name="gather_weights_0a_pass")
            p["ff1"] = ffn_weights(*gather_wait(got, sm, tk, name="gather_weights_0a_wait"))
        else:
            p["ff1"] = ffn_weights(*gathered1[0:2])
        x1 = ffn_fwd(xs, ff1_norm[l][None], *p["ff1"], tb=512, name=f"ffn1_fwd_{l}")
        if l == 0:
            lo, hi = groups["mix"]
            got, sm, tk = gather_pass(fl0[lo:hi], group_sems(sems0, lo, hi), x1, name="gather_weights_0b_pass")
            fl1, sems1, _ = gather_start(layer_bufs(1, dep=tk), name="gather_weights_1_start")
        h = norm_fwd(x1, mix_norm[l][None], tb=512, name=f"mix_norm_fwd_{l}")
        if l == 0:
            p.update(mix_weights(*gather_wait(got, sm, h, name="gather_weights_0b_wait")))
        else:
            p.update(mix_weights(*gathered1[2:5]))
        proj = matmul(h, p["w_in"], tm=1024, tn=1536, tk=1024, name=f"proj_{l}")
        ya = sgu_fwd(proj, sgu_ln_g[l][None], sgu_ln_b[l][None], sgu_w[l], p["bt"], tb=512, name=f"sgu_fwd_{l}")
        xc = conv_fwd(proj, C_BX, BRANCH_W, p["lcw"], p["lcb"], tb=512, name=f"lru_conv_fwd_{l}")
        yb, hl = lru_fwd(xc, proj, p["wa"], p["wx"], lru_ba[l][None], lru_bx[l][None], lru_lambda[l][None],
                         tb=256, name=f"lru_fwd_{l}")
        if l == 0:
            lo, hi = groups["ffn2"]
            got, sm, tk = gather_pass(fl0[lo:hi], group_sems(sems0, lo, hi), yb, name="gather_weights_0c_pass")
        qkv = conv_fwd(proj, C_QKV, 3 * BRANCH_W, p["gcw"], jnp.zeros((1, 3 * BRANCH_W), F32), tb=512,
                       name=f"gdn_conv_fwd_{l}")
        yc, sall = gdn_fwd(qkv, proj, p["alog"], p["dt"], gdn_norm_g[l][None], cps=GDN_CPS, name=f"gdn_fwd_{l}")
        dep2 = None
        if l == 0:
            p["ff2"] = ffn_weights(*gather_wait(got, sm, yc, name="gather_weights_0c_wait"))
            fl1, sems1, dep2 = gather_pass(fl1, sems1, yc, name="gather_weights_1_pass")
        else:
            p["ff2"] = ffn_weights(*gathered1[5:7])
        yd = pool_fwd(proj, pool_w[l], pool_scale[l][None], tb=512, name=f"pool_fwd_{l}")
        ys = (ya, yb, yc, yd)
        x2, merged = merge_fwd(x1, ys, proj, p["wb"], p["wo"], tb=256, dep=dep2, name=f"merge_fwd_{l}")
        x3 = ffn_fwd(x2, ff2_norm[l][None], *p["ff2"], tb=512, name=f"ffn2_fwd_{l}")
        if l == 0:
            gathered1 = gather_wait(fl1, sems1, x3, name="gather_weights_1_wait")
        sv.update(x1=x1, h=h, proj=proj, xc=xc, hl=hl, qkv=qkv, sall=sall, ys=ys, merged=merged, x2=x2)
        saved.append(sv)
        xs = x3

    dx, d_final, loss_blk = loss_head(xs, final_norm[None], loss_target[0], tb=512, name="loss_head")
    loss = lax.psum(loss_blk[0, 0], ("x", "y", "c"))

    small_g = {n: [None] * n_layer for n in _SMALL if n != "final_norm"}
    big_g = [None] * n_layer
    per_layer = 7
    flight = None

    def pair_sums(bigs, recv, tag, w_in_at):
        part = [sum_halves(t, r, c_idx, name=f"grad_pair_sum_{tag}_{i}") for i, (t, r) in enumerate(zip(bigs, recv))]
        if w_in_at is not None:
            part[w_in_at] = _w_in_grad_to_shards(part[w_in_at])
        return part

    def pair_start(arrs, tag):
        return copies_start(arrs, [jax.ShapeDtypeStruct((N_SHARD,) + t.shape[2:], t.dtype) for t in arrs],
                            _pair_plan, N_SHARD, name=f"grad_pair_exchange_{tag}_start")

    def pair_to_chip(fl, after, tag, w_in_at=None):
        sent, recv = copies_wait(*fl[:3], after, _pair_plan, name=f"grad_pair_exchange_{tag}_wait")
        part = pair_sums(sent, recv, tag, w_in_at)
        return copies_start(part, [jax.ShapeDtypeStruct(t.shape, t.dtype) for t in part], _chip_plan, 3,
                            name=f"grad_chip_exchange_{tag}_start")

    def chip_finish(fl, after, tag):
        part, recv = copies_wait(*fl[:3], after, _chip_plan, name=f"grad_chip_exchange_{tag}_wait")
        return [sum_chips(r, q, place, name=f"grad_chip_sum_{tag}_{i}") for i, (r, q) in enumerate(zip(recv, part))]

    early = mid = None

    for l in reversed(range(n_layer)):
        sv = saved[l]
        p = sv["p"]
        proj = sv["proj"]
        dx2, dg_ff2, dab2, hid2, h2, dy2 = ffn_bwd(sv["x2"], dx, ff2_norm[l][None], *p["ff2"], tb=512,
                                                  dep=None if flight is None else flight[3],
                                                  name=f"ffn2_bwd_{l}")
        dwgu2 = matmul(h2, dab2, ta=True, tm=1024, tn=FF_SHARD, tk=TK_DW, name=f"ffn2_dwgu_{l}")
        dwd2 = matmul(hid2, dy2, ta=True, tm=FF_SHARD, tn=1024, tk=TK_DW, name=f"ffn2_dwd_{l}")
        if flight is not None:
            flight = pair_to_chip(flight, dx2, "1", w_in_at=2)
        dgp, dbr, dya, dyb, dyc, dyd, dx2b = merge_bwd(dx2, sv["ys"], proj, p["wb"], p["wo"], tb=256,
                                                       dep=None if flight is None else flight[3],
                                                       name=f"merge_bwd_{l}")
        dwo = matmul(sv["merged"], dx2b, ta=True, tm=1024, tn=1024, tk=TK_DW, name=f"dwo_{l}")
        dwb = jnp.stack([matmul(sv["ys"][g], dbr[g], ta=True, tm=BRANCH_W, tn=1024, tk=TK_DW, name=f"dwb{g}_{l}")
                         for g in range(4)])
        dwb_sh = dwb.reshape(4, BRANCH_W, N_SHARD, d // N_SHARD).transpose(2, 0, 1, 3)
        g_late = [dwgu2, dwd2.reshape(N_SHARD, 2, FF_SHARD // 2, d),
                  dwb_sh.reshape(N_SHARD, 2, 2 * BRANCH_W, d // N_SHARD), dwo.reshape(N_SHARD, 2, d // 8, d)]
        tok = None
        if l == 0:
            early = pair_start(g_late, "0a")
            tok = early[3]
        du, dv, dlg, dlb, dws, dbt = sgu_bwd(proj, dya, sgu_ln_g[l][None], sgu_ln_b[l][None], sgu_w[l], p["bt"],
                                             tb=512, dep=tok, name=f"sgu_bwd_{l}")
        dxc, dgate, dwa, dwx, dba, dbx, dlam = lru_bwd(sv["xc"], proj, sv["hl"], dyb, p["wa"], p["wx"],
                                                       lru_ba[l][None], lru_bx[l][None], lru_lambda[l][None],
                                                       tb=256, name=f"lru_bwd_{l}")
        dbx_in, dcw_l = conv_bwd(proj, C_BX, BRANCH_W, dxc, p["lcw"], tb=512, name=f"lru_conv_bwd_{l}")
        dqkv, dz, dbeta, dal, ddt, dng = gdn_bwd(sv["qkv"], proj, sv["sall"], dyc, p["alog"], p["dt"],
                                                 gdn_norm_g[l][None], cps=GDN_CPS, name=f"gdn_bwd_{l}")
        dqkv_in, dcw_g = conv_bwd(proj, C_QKV, 3 * BRANCH_W, dqkv, p["gcw"], tb=512, name=f"gdn_conv_bwd_{l}")
        ddx, dwp, dsc = pool_bwd(proj, dyd, pool_w[l], pool_scale[l][None], tb=512, name=f"pool_bwd_{l}")
        if l == 0:
            early = pair_to_chip(early, ddx, "0a")
            tok = early[3]
        s = dx.shape[0]
        dproj = jnp.concatenate([dqkv_in, dz, du, dv, dbx_in, dgate, ddx, dbeta,
                                 jnp.zeros((s, C_GATE - C_BA - 128), BF16), dgp], axis=1)
        dh = matmul(dproj, p["w_in"], tb=True, tm=1024, tn=1024, tk=1536, dep=tok, name=f"dh_mix_{l}")
        dw_in = matmul(sv["h"], dproj, ta=True, tm=1024, tn=1152, tk=TK_DW, col_groups=N_SHARD, name=f"dw_in_{l}")
        dw_in = dw_in.reshape(N_SHARD, 2, d // 2, P_PAD // N_SHARD)
        if l == 0:
            mid = pair_start([dw_in], "0b")
            tok = mid[3]
        dx1, dg_mix = norm_bwd(sv["x1"], mix_norm[l][None], dh, dx2, tb=512, dep=tok, name=f"mix_norm_bwd_{l}")
        dx, dg_ff1, dab1, hid1, h1, dy1 = ffn_bwd(sv["x0"], dx1, ff1_norm[l][None], *p["ff1"],
                                                 tb=512, name=f"ffn1_bwd_{l}")
        if l == 0:
            mid = pair_to_chip(mid, dx, "0b", w_in_at=0)
            tok = mid[3]
        dwgu1 = matmul(h1, dab1, ta=True, tm=1024, tn=FF_SHARD, tk=TK_DW, dep=tok, name=f"ffn1_dwgu_{l}")
        dwd1 = matmul(hid1, dy1, ta=True, tm=FF_SHARD, tn=1024, tk=TK_DW, name=f"ffn1_dwd_{l}")

        for n, g in (("ff1_norm", dg_ff1[0]), ("mix_norm", dg_mix[0]), ("sgu_ln_g", dlg[0]), ("sgu_ln_b", dlb[0]),
                     ("sgu_w", dws), ("sgu_b", dbt[:, :4].T), ("lru_conv_w", dcw_l[:4]), ("lru_conv_b", dcw_l[4]),
                     ("lru_wa", _diag_blocks(dwa)), ("lru_ba", dba[0]), ("lru_wx", _diag_blocks(dwx)),
                     ("lru_bx", dbx[0]), ("lru_lambda", dlam[0]), ("gdn_conv_w", dcw_g[:4]),
                     ("gdn_a_log", dal[0, 4:8]), ("gdn_dt_bias", ddt[0, 4:8]), ("gdn_norm_g", dng[0]),
                     ("pool_w", dwp), ("pool_scale", dsc[0]), ("ff2_norm", dg_ff2[0])):
            small_g[n][l] = g
        big_g[l] = [dwgu1, dwd1.reshape(N_SHARD, 2, FF_SHARD // 2, d), dw_in, g_late[2], g_late[3], g_late[0], g_late[1]]
        if l == n_layer - 1:
            flight = pair_start(big_g[l], "1")

    last = big_g[0][:2]
    recv = pair_send_halves(last, name="grad_pair_exchange_0c")
    part = pair_sums(last, recv, "0c", None)
    recv = chip_scatter(part, name="grad_chip_exchange_0c")
    red_c = [sum_chips(r, q, place, name=f"grad_chip_sum_0c_{i}") for i, (r, q) in enumerate(zip(recv, part))]
    red1 = chip_finish(flight, red_c[0], "1")
    red_a = chip_finish(early, red_c[0], "0a")
    red_b = chip_finish(mid, red_c[0], "0b")
    full = pair_gather(red_c + red_b + [red_a[2], red_a[3], red_a[0], red_a[1]] + red1, name="grad_pair_gather")
    grads = {}

    def layers(k, f):
        return jnp.stack([f(full[l * per_layer + k]) for l in range(n_layer)])

    grads["ff1_wg"] = layers(0, lambda t: t[0])
    grads["ff1_wu"] = layers(0, lambda t: t[1])
    grads["ff1_wd"] = layers(1, lambda t: t.reshape(FF_SHARD, d))
    grads["w_in"] = layers(2, lambda t: t.reshape(d, P_IN_SHARD_PAD)[:, :P_IN_SHARD])
    grads["w_branch"] = layers(3, lambda t: t.reshape(4, BRANCH_W, d // N_SHARD))
    grads["w_out"] = layers(4, lambda t: t.reshape(d // N_SHARD, d))
    grads["ff2_wg"] = layers(5, lambda t: t[0])
    grads["ff2_wu"] = layers(5, lambda t: t[1])
    grads["ff2_wd"] = layers(6, lambda t: t.reshape(FF_SHARD, d))

    small_full = [jnp.stack(small_g[n]) for n in _SMALL if n != "final_norm"] + [d_final[0]]
    summed = _unpack(all_sum_small(_pack(small_full), name="grad_small_allreduce"), [t.shape for t in small_full])
    for n, g in zip(_SMALL, summed):
        if n in _CONV:
            width = a[n].shape[-1]
            g = lax.dynamic_slice_in_dim(g, chip * width, width, axis=2)
        grads[n] = g

    delta, new_m, new_v = {}, {}, {}
    for n in _BIG:
        delta[n], new_m[n], new_v[n] = adamw(a[n], grads[n], a["m_" + n], a["v_" + n], name=f"adamw_{n}")
    shapes = [a[n].shape for n in _SMALL]
    packed = [_pack([src[pre + n] for n in _SMALL]) for src, pre in ((a, ""), (grads, ""), (a, "m_"), (a, "v_"))]
    for store, buf in zip((delta, new_m, new_v), adamw(*packed, name="adamw_small")):
        store.update(zip(_SMALL, _unpack(buf, shapes)))

    return (loss, dx[None], *[grads[n] for n in _WEIGHTS], *[delta[n] for n in _WEIGHTS],
            *[new_m[n] for n in _WEIGHTS], *[new_v[n] for n in _WEIGHTS])
```

```python
import functools
import math

import jax
import jax.numpy as jnp
from jax import lax
from jax.experimental import pallas as pl
from jax.experimental.pallas import tpu as pltpu

F32 = jnp.float32
BF16 = jnp.bfloat16
MESH = pl.DeviceIdType.MESH

D_MODEL = 1024
N_SHARD = 4
D_FF = 2816
FF_SHARD = D_FF // N_SHARD
BRANCH_W = 512
CHUNK = 64
GDN_HEADS = 4
GDN_DK = 128
GDN_CPS = 4
TK_DW = 2048
LRU_C = 8.0
EPS = 1e-6
P_IN = 8712
P_IN_SHARD = P_IN // N_SHARD
P_IN_SHARD_PAD = 2304
P_PAD = 9216
C_QKV, C_Z, C_AU, C_AV, C_BX, C_BG, C_DX, C_BA, C_GATE = 0, 1536, 2048, 2560, 3072, 3584, 4096, 4608, 5120

ADAM_LR, ADAM_B1, ADAM_B2, ADAM_EPS, ADAM_WD, ADAM_STEP = 0.001, 0.9, 0.999, 1e-08, 0.01, 10


def _gelu(x):
    return 0.5 * x * (1.0 + jnp.tanh(0.7978845608028654 * (x + 0.044715 * (x * x * x))))


def _gelu_grad(x):
    u = 0.7978845608028654 * (x + 0.044715 * (x * x * x))
    t = jnp.tanh(u)
    return 0.5 * (1.0 + t) + 0.5 * x * (1.0 - t * t) * 0.7978845608028654 * (1.0 + 3.0 * 0.044715 * x * x)


def _silu(x):
    return x * jax.nn.sigmoid(x)


def _softplus(x):
    return jnp.maximum(x, 0.0) + jnp.log1p(jnp.exp(-jnp.abs(x)))


def _dg(a, b, ca, cb):
    return lax.dot_general(a.astype(BF16), b.astype(BF16), (((ca,), (cb,)), ((), ())),
                           preferred_element_type=F32)


def _dg3(a, b, ca, cb):
    a1 = a.astype(BF16)
    a2 = (a - a1.astype(F32)).astype(BF16)
    b1 = b.astype(BF16)
    b2 = (b - b1.astype(F32)).astype(BF16)
    dn = (((ca,), (cb,)), ((), ()))
    d = functools.partial(lax.dot_general, dimension_numbers=dn, preferred_element_type=F32)
    return d(a1, b1) + (d(a1, b2) + d(a2, b1))


def _make_mm(ca, cb, dot):
    @jax.custom_vjp
    def f(a, b):
        return dot(a, b, ca, cb)

    def fwd(a, b):
        return dot(a, b, ca, cb), (a, b)

    def bwd(res, g):
        a, b = res
        if (ca, cb) == (1, 0):
            return dot(g, b, 1, 1), dot(a, g, 0, 0)
        if (ca, cb) == (1, 1):
            return dot(g, b, 1, 0), dot(g, a, 0, 0)
        return dot(b, g, 1, 1), dot(a, g, 1, 0)

    f.defvjp(fwd, bwd)
    return f


mm_nn = _make_mm(1, 0, _dg)


def _bdg(a, b, ca, cb):
    return lax.dot_general(a.astype(BF16), b.astype(BF16), (((ca,), (cb,)), ((0,), (0,))),
                           preferred_element_type=F32)


def _split2(x):
    x1 = x.astype(BF16)
    return x1, (x - x1.astype(F32)).astype(BF16)


def _bdg3(a, b, ca, cb):
    a1, a2 = _split2(a)
    b1, b2 = _split2(b)
    d = functools.partial(lax.dot_general, dimension_numbers=(((ca,), (cb,)), ((0,), (0,))),
                          preferred_element_type=F32)
    return d(a1, b1) + (d(a1, b2) + d(a2, b1))


def _make_bmm(ca, cb):
    @jax.custom_vjp
    def f(a, b):
        return _bdg(a, b, ca, cb)

    def fwd(a, b):
        return _bdg(a, b, ca, cb), (a, b)

    def bwd(res, g):
        a, b = res
        if (ca, cb) == (2, 1):
            return _bdg(g, b, 2, 2), _bdg(a, g, 1, 1)
        if (ca, cb) == (2, 2):
            return _bdg(g, b, 2, 1), _bdg(g, a, 1, 1)
        return _bdg(b, g, 2, 2), _bdg(a, g, 2, 1)

    f.defvjp(fwd, bwd)
    return f


bmm_nn = _make_bmm(2, 1)
bmm_nt = _make_bmm(2, 2)
bmm_tn = _make_bmm(1, 1)


def _iota3(shape, dim):
    return lax.broadcasted_iota(jnp.int32, shape, dim)


@jax.custom_vjp
def _tri_inv(aw):
    b, n, _ = aw.shape
    shp = (b, n, 2 * n)
    left = _iota3(shp, 2) < n
    x = jnp.where(_iota3(shp, 1) == _iota3(shp, 2), 1.0, 0.0) - aw
    zero = jnp.zeros((b, n, 4 * n), BF16)
    k = 1
    while k < n:
        x1, x2 = _split2(x)
        lhs = jnp.concatenate([x1, x2], axis=1)
        rhs = jnp.concatenate([zero, jnp.concatenate([x1, x2], axis=2)], axis=1)
        r = lax.dot_general(lhs, rhs, (((2,), (1,)), ((0,), (0,))), preferred_element_type=F32)
        mx = (r[:, :n, :2 * n] + r[:, :n, 2 * n:]) + (r[:, n:, :2 * n] + r[:, n:, 2 * n:])
        x = jnp.where(left, x, 0.0) + mx
        k *= 2
    return jnp.where(left, x, 0.0)


def _tri_inv_fwd(aw):
    t = _tri_inv(aw)
    return t, t


def _tri_inv_bwd(t, dt):
    n = t.shape[1]
    x = _bdg3(t, dt, 1, 1)[:, :n]
    tp = jnp.concatenate([jnp.zeros_like(t), t], axis=1)
    return (-_bdg3(x, tp, 2, 2),)


_tri_inv.defvjp(_tri_inv_fwd, _tri_inv_bwd)


def _rows(shape):
    return lax.broadcasted_iota(jnp.int32, shape, 0)


def _lanes(shape):
    return lax.broadcasted_iota(jnp.int32, shape, 1)


def _col(x, j):
    return jnp.sum(jnp.where(_lanes(x.shape) == j, x, 0.0), axis=1, keepdims=True)


def _put_col(col, j, width):
    shape = (col.shape[0], width)
    return jnp.where(_lanes(shape) == j, jnp.broadcast_to(col, shape), 0.0)


VMEM_BYTES_V7X = 64 << 20


def _params(*sem, vmem_bytes=None):
    return pltpu.CompilerParams(dimension_semantics=sem, vmem_limit_bytes=vmem_bytes)


def matmul(a, b, *, ta=False, tb=False, tm, tn, tk, out_dtype=F32, col_groups=1, dep=None, name):
    ba, bb = a.shape[:-2], b.shape[:-2]
    batch = ba if len(ba) >= len(bb) else bb
    assert ba in ((), batch) and bb in ((), batch)
    nb = len(batch)
    (m, k) = (a.shape[-1], a.shape[-2]) if ta else (a.shape[-2], a.shape[-1])
    (k2, n) = (b.shape[-1], b.shape[-2]) if tb else (b.shape[-2], b.shape[-1])
    tm, tn, tk = min(tm, m), min(tn, n), min(tk, k)
    assert k == k2 and m % tm == 0 and n % tn == 0 and k % tk == 0, (a.shape, b.shape, tm, tn, tk)
    nk = k // tk
    grid = batch + (m // tm, n // tn, nk)

    def a_map(*g):
        i, kk = g[nb], g[nb + 2]
        return (g[:nb] if ba else ()) + ((kk, i) if ta else (i, kk))

    def b_map(*g):
        j, kk = g[nb + 1], g[nb + 2]
        return (g[:nb] if bb else ()) + ((j, kk) if tb else (kk, j))

    per_group = n // tn // col_groups
    assert col_groups == 1 or (nb == 0 and per_group * col_groups * tn == n)

    def o_map(*g):
        if col_groups > 1:
            return (g[1] // per_group, g[0], g[1] % per_group)
        return g[:nb] + (g[nb], g[nb + 1])

    a_blk = (None,) * len(ba) + ((tk, tm) if ta else (tm, tk))
    b_blk = (None,) * len(bb) + ((tn, tk) if tb else (tk, tn))
    o_blk = (None,) * (nb + (col_groups > 1)) + (tm, tn)
    o_shape = (col_groups, m, n // col_groups) if col_groups > 1 else batch + (m, n)

    dep_specs, dep_args = _dep_operand(dep)

    def body(a_ref, b_ref, *rest):
        o_ref, acc_ref = rest[len(dep_args):]
        kk = pl.program_id(nb + 2)

        @pl.when(kk == 0)
        def _():
            acc_ref[...] = jnp.zeros_like(acc_ref)

        acc_ref[...] += _dg(a_ref[...], b_ref[...], 0 if ta else 1, 1 if tb else 0)

        @pl.when(kk == nk - 1)
        def _():
            o_ref[...] = acc_ref[...].astype(o_ref.dtype)

    return pl.pallas_call(
        body, name=name, grid=grid,
        in_specs=[pl.BlockSpec(a_blk, a_map), pl.BlockSpec(b_blk, b_map)] + dep_specs,
        out_specs=pl.BlockSpec(o_blk, o_map),
        out_shape=jax.ShapeDtypeStruct(o_shape, out_dtype),
        scratch_shapes=[pltpu.VMEM((tm, tn), F32)],
        compiler_params=_params(*(("parallel",) * (nb + 2) + ("arbitrary",))),
    )(a, b, *dep_args)


def norm_fwd(x, g, *, tb, name):
    s, d = x.shape

    def body(x_ref, g_ref, h_ref):
        xv = x_ref[...]
        r = lax.rsqrt(jnp.mean(xv * xv, axis=-1, keepdims=True) + EPS)
        h_ref[...] = (xv * r * g_ref[...]).astype(BF16)

    return pl.pallas_call(
        body, name=name, grid=(s // tb,),
        in_specs=[pl.BlockSpec((tb, d), lambda i: (i, 0)), pl.BlockSpec((1, d), lambda i: (0, 0))],
        out_specs=pl.BlockSpec((tb, d), lambda i: (i, 0)),
        out_shape=jax.ShapeDtypeStruct((s, d), BF16),
        compiler_params=_params("parallel"),
    )(x, g)


def _rms_bwd(xv, gv, dh):
    r = lax.rsqrt(jnp.mean(xv * xv, axis=-1, keepdims=True) + EPS)
    xh = xv * r
    dxh = dh * gv
    dx = r * (dxh - xh * jnp.mean(dxh * xh, axis=-1, keepdims=True))
    return dx, jnp.sum(dh * xh, axis=0, keepdims=True)


def norm_bwd(x, g, dh, dres, *, tb, dep=None, name):
    s, d = x.shape
    dep_specs, dep_args = _dep_operand(dep)

    def body(x_ref, g_ref, dh_ref, dres_ref, *rest):
        dx_ref, dg_ref = rest[len(dep_args):]
        dx, dg = _rms_bwd(x_ref[...], g_ref[...], dh_ref[...])
        dx_ref[...] = dres_ref[...] + dx

        @pl.when(pl.program_id(0) == 0)
        def _():
            dg_ref[...] = jnp.zeros_like(dg_ref)

        dg_ref[...] += dg

    row = pl.BlockSpec((tb, d), lambda i: (i, 0))
    one = pl.BlockSpec((1, d), lambda i: (0, 0))
    return pl.pallas_call(
        body, name=name, grid=(s // tb,),
        in_specs=[row, one, row, row] + dep_specs, out_specs=[row, one],
        out_shape=[jax.ShapeDtypeStruct((s, d), F32), jax.ShapeDtypeStruct((1, d), F32)],
        compiler_params=_params("arbitrary"),
    )(x, g, dh, dres, *dep_args)


def _dep_operand(dep):
    return ([], []) if dep is None else ([_ANY], [dep])


def ffn_fwd(x, g, wg, wu, wd, layer, *, tb, dep=None, name):
    s, d = x.shape
    fc = wd.shape[-2]
    dep_specs, dep_args = _dep_operand(dep)

    def body(x_ref, g_ref, wg_ref, wu_ref, wd_ref, *rest):
        o_ref, h_scr, acc_scr = rest[len(dep_args):]
        j = pl.program_id(1)

        @pl.when(j == 0)
        def _():
            xv = x_ref[...]
            r = lax.rsqrt(jnp.mean(xv * xv, axis=-1, keepdims=True) + EPS)
            h_scr[...] = (xv * r * g_ref[...]).astype(BF16)
            acc_scr[...] = jnp.zeros_like(acc_scr)

        h = h_scr[...]
        a = _dg(h, wg_ref[...], 1, 0)
        b = _dg(h, wu_ref[...], 1, 0)
        acc_scr[...] += _dg(_silu(a) * b, wd_ref[...], 1, 0)

        @pl.when(j == N_SHARD - 1)
        def _():
            o_ref[...] = x_ref[...] + 0.5 * acc_scr[...]

    row = pl.BlockSpec((tb, d), lambda i, j: (i, 0))
    lg, lu, ld = layer
    return pl.pallas_call(
        body, name=name, grid=(s // tb, N_SHARD),
        in_specs=[row, pl.BlockSpec((1, d), lambda i, j: (0, 0)),
                  pl.BlockSpec((None, None, d, fc), lambda i, j: (j, lg, 0, 0)),
                  pl.BlockSpec((None, None, d, fc), lambda i, j: (j, lu, 0, 0)),
                  pl.BlockSpec((None, None, fc, d), lambda i, j: (j, ld, 0, 0))] + dep_specs,
        out_specs=row,
        out_shape=jax.ShapeDtypeStruct((s, d), F32),
        scratch_shapes=[pltpu.VMEM((tb, d), BF16), pltpu.VMEM((tb, d), F32)],
        compiler_params=_params("parallel", "arbitrary"),
    )(x, g, wg, wu, wd, *dep_args)


def ffn_bwd(x, dxo, g, wg, wu, wd, layer, *, tb, dep=None, name):
    s, d = x.shape
    fc = wd.shape[-2]
    dep_specs, dep_args = _dep_operand(dep)

    def body(x_ref, dxo_ref, g_ref, wg_ref, wu_ref, wd_ref, *rest):
        dx_ref, dg_ref, dab_ref, hid_ref, h_ref, dy_ref, dh_scr = rest[len(dep_args):]
        i, j = pl.program_id(0), pl.program_id(1)

        @pl.when(j == 0)
        def _():
            xv = x_ref[...]
            r = lax.rsqrt(jnp.mean(xv * xv, axis=-1, keepdims=True) + EPS)
            h_ref[...] = (xv * r * g_ref[...]).astype(BF16)
            dy_ref[...] = (0.5 * dxo_ref[...]).astype(BF16)
            dh_scr[...] = jnp.zeros_like(dh_scr)

        h = h_ref[...]
        a = _dg(h, wg_ref[...], 1, 0)
        b = _dg(h, wu_ref[...], 1, 0)
        sg = jax.nn.sigmoid(a)
        sl = a * sg
        dhid = _dg(dy_ref[...], wd_ref[...], 1, 1)
        da = (dhid * b * (sg * (1.0 + a * (1.0 - sg)))).astype(BF16)
        db = (dhid * sl).astype(BF16)
        dab_ref[0] = da
        dab_ref[1] = db
        hid_ref[...] = (sl * b).astype(BF16)
        dh_scr[...] += _dg(da, wg_ref[...], 1, 1) + _dg(db, wu_ref[...], 1, 1)

        @pl.when(jnp.logical_and(i == 0, j == 0))
        def _():
            dg_ref[...] = jnp.zeros_like(dg_ref)

        @pl.when(j == N_SHARD - 1)
        def _():
            dx, dg = _rms_bwd(x_ref[...], g_ref[...], dh_scr[...])
            dx_ref[...] = dxo_ref[...] + dx
            dg_ref[...] += dg

    row = pl.BlockSpec((tb, d), lambda i, j: (i, 0))
    one = pl.BlockSpec((1, d), lambda i, j: (0, 0))
    lg, lu, ld = layer
    return pl.pallas_call(
        body, name=name, grid=(s // tb, N_SHARD),
        in_specs=[row, row, one,
                  pl.BlockSpec((None, None, d, fc), lambda i, j: (j, lg, 0, 0)),
                  pl.BlockSpec((None, None, d, fc), lambda i, j: (j, lu, 0, 0)),
                  pl.BlockSpec((None, None, fc, d), lambda i, j: (j, ld, 0, 0))] + dep_specs,
        out_specs=[row, one,
                   pl.BlockSpec((None, 2, tb, fc), lambda i, j: (j, 0, i, 0)),
                   pl.BlockSpec((None, tb, fc), lambda i, j: (j, i, 0)),
                   row, row],
        out_shape=[jax.ShapeDtypeStruct((s, d), F32), jax.ShapeDtypeStruct((1, d), F32),
                   jax.ShapeDtypeStruct((N_SHARD, 2, s, fc), BF16),
                   jax.ShapeDtypeStruct((N_SHARD, s, fc), BF16),
                   jax.ShapeDtypeStruct((s, d), BF16), jax.ShapeDtypeStruct((s, d), BF16)],
        scratch_shapes=[pltpu.VMEM((tb, d), F32)],
        compiler_params=_params("arbitrary", "arbitrary",
                                vmem_bytes=min(VMEM_BYTES_V7X - (8 << 20),
                                               12 * d * fc + 44 * tb * d + 40 * tb * fc + (4 << 20))),
    )(x, dxo, g, wg, wu, wd, *dep_args)


def _pblk(tb, width, col0, rev_n=None):
    assert col0 % width == 0
    cb = col0 // width
    if rev_n is None:
        return pl.BlockSpec((tb, width), lambda i: (i, cb))
    return pl.BlockSpec((tb, width), lambda i: (rev_n - 1 - i, cb))


def _full(shape):
    return pl.BlockSpec(shape, lambda i: (0,) * len(shape))


def conv_fwd(src, col0, width, w, bias, *, tb, name):
    s = src.shape[0]
    cb = col0 // width

    def body(x_ref, p_ref, w_ref, b_ref, y_ref):
        i = pl.program_id(0)
        prev = jnp.where(i > 0, p_ref[...], 0.0)
        xe = jnp.concatenate([prev, x_ref[...]], axis=0)
        wv = w_ref[...]
        acc = b_ref[...] + wv[3:4] * xe[8:]
        for k in range(3):
            acc = acc + wv[k:k + 1] * pltpu.roll(xe, 3 - k, 0)[8:]
        y_ref[...] = acc

    return pl.pallas_call(
        body, name=name, grid=(s // tb,),
        in_specs=[_pblk(tb, width, col0),
                  pl.BlockSpec((8, width), lambda i: (jnp.maximum(i * (tb // 8) - 1, 0), cb)),
                  _full((4, width)), _full((1, width))],
        out_specs=pl.BlockSpec((tb, width), lambda i: (i, 0)),
        out_shape=jax.ShapeDtypeStruct((s, width), F32),
        compiler_params=_params("parallel"),
    )(src, src, w, bias)


def conv_bwd(src, col0, width, dpre, w, *, tb, name):
    s = src.shape[0]
    cb = col0 // width
    nt = s // tb

    def body(x_ref, p_ref, d_ref, n_ref, w_ref, dx_ref, dwb_ref):
        i = pl.program_id(0)
        d = d_ref[...]
        nxt = jnp.where(i < nt - 1, n_ref[...], 0.0)
        de = jnp.concatenate([d, nxt], axis=0)
        wv = w_ref[...]
        dx = wv[3:4] * d
        for k in range(3):
            dx = dx + wv[k:k + 1] * pltpu.roll(de, tb + 8 - (3 - k), 0)[:tb]
        dx_ref[...] = dx.astype(BF16)

        @pl.when(i == 0)
        def _():
            dwb_ref[...] = jnp.zeros_like(dwb_ref)

        prev = jnp.where(i > 0, p_ref[...], 0.0)
        xe = jnp.concatenate([prev, x_ref[...]], axis=0)
        dwb_ref[3:4, :] += jnp.sum(d * xe[8:], axis=0, keepdims=True)
        for k in range(3):
            dwb_ref[k:k + 1, :] += jnp.sum(d * pltpu.roll(xe, 3 - k, 0)[8:], axis=0, keepdims=True)
        dwb_ref[4:5, :] += jnp.sum(d, axis=0, keepdims=True)

    return pl.pallas_call(
        body, name=name, grid=(nt,),
        in_specs=[_pblk(tb, width, col0),
                  pl.BlockSpec((8, width), lambda i: (jnp.maximum(i * (tb // 8) - 1, 0), cb)),
                  pl.BlockSpec((tb, width), lambda i: (i, 0)),
                  pl.BlockSpec((8, width), lambda i: (jnp.minimum((i + 1) * (tb // 8), s // 8 - 1), 0)),
                  _full((4, width))],
        out_specs=[pl.BlockSpec((tb, width), lambda i: (i, 0)), _full((8, width))],
        out_shape=[jax.ShapeDtypeStruct((s, width), BF16), jax.ShapeDtypeStruct((8, width), F32)],
        compiler_params=_params("arbitrary"),
    )(src, src, dpre, dpre, w)


SGU_BLOCK = 128


def _sgu_pre(u, v, lg, lb):
    ug = _gelu(u)
    vg = _gelu(v)
    mu = jnp.mean(vg, axis=-1, keepdims=True)
    var = jnp.mean(jnp.square(vg - mu), axis=-1, keepdims=True)
    return ug, (vg - mu) * lax.rsqrt(var + EPS) * lg + lb


def _sgu_mask():
    shp = (SGU_BLOCK, SGU_BLOCK)
    return (_rows(shp) // CHUNK) >= (_lanes(shp) // CHUNK)


def sgu_fwd(proj, lg, lb, ws, bt, *, tb, name):
    s = proj.shape[0]
    w = BRANCH_W

    def body(u_ref, v_ref, lg_ref, lb_ref, w_ref, b_ref, y_ref):
        ug, vn = _sgu_pre(u_ref[...], v_ref[...], lg_ref[...], lb_ref[...])
        mask = _sgu_mask()
        for g in range(4):
            wm = jnp.where(mask, w_ref[g], 0.0)
            bcol = _col(b_ref[...], g)
            cs = slice(128 * g, 128 * g + 128)
            for n in range(tb // SGU_BLOCK):
                rs = slice(SGU_BLOCK * n, SGU_BLOCK * (n + 1))
                mixed = _dg(wm, vn[rs, cs], 1, 0) + bcol
                y_ref[rs, cs] = (ug[rs, cs] * mixed).astype(BF16)

    return pl.pallas_call(
        body, name=name, grid=(s // tb,),
        in_specs=[_pblk(tb, w, C_AU), _pblk(tb, w, C_AV), _full((1, w)), _full((1, w)),
                  _full((4, 128, 128)), _full((128, 128))],
        out_specs=pl.BlockSpec((tb, w), lambda i: (i, 0)),
        out_shape=jax.ShapeDtypeStruct((s, w), BF16),
        compiler_params=_params("parallel"),
    )(proj, proj, lg, lb, ws, bt)


def sgu_bwd(proj, dy, lg, lb, ws, bt, *, tb, dep=None, name):
    s = proj.shape[0]
    w = BRANCH_W
    dep_specs, dep_args = _dep_operand(dep)

    def body(u_ref, v_ref, dy_ref, lg_ref, lb_ref, w_ref, b_ref, *rest):
        du_ref, dv_ref, dlg_ref, dlb_ref, dw_ref, db_ref, dug_scr, dvn_scr = rest[len(dep_args):]

        @pl.when(pl.program_id(0) == 0)
        def _():
            dlg_ref[...] = jnp.zeros_like(dlg_ref)
            dlb_ref[...] = jnp.zeros_like(dlb_ref)
            dw_ref[...] = jnp.zeros_like(dw_ref)
            db_ref[...] = jnp.zeros_like(db_ref)

        (ug, vn), vf = jax.vjp(_sgu_pre, u_ref[...], v_ref[...], lg_ref[...], lb_ref[...])
        dyv = dy_ref[...]
        mask = _sgu_mask()
        for g in range(4):
            wm = jnp.where(mask, w_ref[g], 0.0)
            bcol = _col(b_ref[...], g)
            cs = slice(128 * g, 128 * g + 128)
            dwg = jnp.zeros((SGU_BLOCK, SGU_BLOCK), F32)
            dbc = jnp.zeros((SGU_BLOCK, 1), F32)
            for n in range(tb // SGU_BLOCK):
                rs = slice(SGU_BLOCK * n, SGU_BLOCK * (n + 1))
                blk = vn[rs, cs]
                mixed = _dg(wm, blk, 1, 0) + bcol
                dmix = dyv[rs, cs] * ug[rs, cs]
                dug_scr[rs, cs] = dyv[rs, cs] * mixed
                dvn_scr[rs, cs] = _dg(wm, dmix, 0, 0)
                dwg = dwg + _dg(dmix, blk, 1, 1)
                dbc = dbc + jnp.sum(dmix, axis=1, keepdims=True)
            dw_ref[g] += jnp.where(mask, dwg, 0.0)
            db_ref[...] += _put_col(dbc, g, 128)
        du, dv, dlg, dlb = vf((dug_scr[...], dvn_scr[...]))
        du_ref[...] = du.astype(BF16)
        dv_ref[...] = dv.astype(BF16)
        dlg_ref[...] += dlg
        dlb_ref[...] += dlb

    row = pl.BlockSpec((tb, w), lambda i: (i, 0))
    return pl.pallas_call(
        body, name=name, grid=(s // tb,),
        in_specs=[_pblk(tb, w, C_AU), _pblk(tb, w, C_AV), row, _full((1, w)), _full((1, w)),
                  _full((4, 128, 128)), _full((128, 128))] + dep_specs,
        out_specs=[row, row, _full((1, w)), _full((1, w)), _full((4, 128, 128)), _full((128, 128))],
        out_shape=[jax.ShapeDtypeStruct((s, w), BF16), jax.ShapeDtypeStruct((s, w), BF16),
                   jax.ShapeDtypeStruct((1, w), F32), jax.ShapeDtypeStruct((1, w), F32),
                   jax.ShapeDtypeStruct((4, 128, 128), F32), jax.ShapeDtypeStruct((128, 128), F32)],
        scratch_shapes=[pltpu.VMEM((tb, w), F32), pltpu.VMEM((tb, w), F32)],
        compiler_params=_params("arbitrary"),
    )(proj, proj, dy, lg, lb, ws, bt, *dep_args)


POOL_HALO = 16


def _pooled(xe, t0, tb):
    s2 = xe + pltpu.roll(xe, 1, 0)
    s4 = s2 + pltpu.roll(s2, 2, 0)
    s8 = s4 + pltpu.roll(s4, 4, 0)
    s16 = s8 + pltpu.roll(s8, 8, 0)
    t = (t0 + _rows((tb, 128))).astype(F32)
    outs = []
    for g, sw in enumerate((s2, s4, s8, s16)):
        cs = slice(128 * g, 128 * g + 128)
        cnt = jnp.minimum(t + 1.0, float(2 ** (g + 1)))
        outs.append(sw[POOL_HALO:, cs] / cnt - xe[POOL_HALO:, cs])
    return outs


def pool_fwd(proj, wp, scale, *, tb, name):
    s = proj.shape[0]
    w = BRANCH_W

    def body(x_ref, p_ref, w_ref, sc_ref, y_ref):
        i = pl.program_id(0)
        prev = jnp.where(i > 0, p_ref[...], 0.0)
        xe = jnp.concatenate([prev, x_ref[...]], axis=0)
        pooled = _pooled(xe, i * tb, tb)
        sc = sc_ref[...]
        for g in range(4):
            cs = slice(128 * g, 128 * g + 128)
            y_ref[:, cs] = (_dg(pooled[g], w_ref[g], 1, 0) * sc[:, cs]).astype(BF16)

    cb = C_DX // w
    return pl.pallas_call(
        body, name=name, grid=(s // tb,),
        in_specs=[_pblk(tb, w, C_DX),
                  pl.BlockSpec((POOL_HALO, w), lambda i: (jnp.maximum(i * (tb // POOL_HALO) - 1, 0), cb)),
                  _full((4, 128, 128)), _full((1, w))],
        out_specs=pl.BlockSpec((tb, w), lambda i: (i, 0)),
        out_shape=jax.ShapeDtypeStruct((s, w), BF16),
        compiler_params=_params("parallel"),
    )(proj, proj, wp, scale)


def pool_bwd(proj, dy, wp, scale, *, tb, name):
    s = proj.shape[0]
    w = BRANCH_W
    nt = s // tb
    te = tb + POOL_HALO

    def body(x_ref, p_ref, dy_ref, n_ref, w_ref, sc_ref, dx_ref, dw_ref, dsc_ref):
        i = pl.program_id(0)

        @pl.when(i == 0)
        def _():
            dw_ref[...] = jnp.zeros_like(dw_ref)
            dsc_ref[...] = jnp.zeros_like(dsc_ref)

        prev = jnp.where(i > 0, p_ref[...], 0.0)
        xe = jnp.concatenate([prev, x_ref[...]], axis=0)
        pooled = _pooled(xe, i * tb, tb)
        dyv = dy_ref[...]
        nxt = jnp.where(i < nt - 1, n_ref[...], 0.0)
        dye = jnp.concatenate([dyv, nxt], axis=0)
        sc = sc_ref[...]
        t = (i * tb + _rows((te, 128))).astype(F32)
        for g in range(4):
            cs = slice(128 * g, 128 * g + 128)
            win = 2 ** (g + 1)
            dpm = dye[:, cs] * sc[:, cs]
            dpool = _dg(dpm, w_ref[g], 1, 1)
            q = dpool / jnp.minimum(t + 1.0, float(win))
            r, sh = q, 1
            while sh < win:
                r = r + pltpu.roll(r, te - sh, 0)
                sh *= 2
            dx_ref[:, cs] = (r[:tb] - dpool[:tb]).astype(BF16)
            dw_ref[g] += _dg(pooled[g], dpm[:tb], 0, 0)
            dsc_ref[:, cs] += jnp.sum(dyv[:, cs] * _dg(pooled[g], w_ref[g], 1, 0), axis=0, keepdims=True)

    cb = C_DX // w
    row = pl.BlockSpec((tb, w), lambda i: (i, 0))
    return pl.pallas_call(
        body, name=name, grid=(nt,),
        in_specs=[_pblk(tb, w, C_DX),
                  pl.BlockSpec((POOL_HALO, w), lambda i: (jnp.maximum(i * (tb // POOL_HALO) - 1, 0), cb)),
                  row,
                  pl.BlockSpec((POOL_HALO, w),
                               lambda i: (jnp.minimum((i + 1) * (tb // POOL_HALO), s // POOL_HALO - 1), 0)),
                  _full((4, 128, 128)), _full((1, w))],
        out_specs=[row, _full((4, 128, 128)), _full((1, w))],
        out_shape=[jax.ShapeDtypeStruct((s, w), BF16), jax.ShapeDtypeStruct((4, 128, 128), F32),
                   jax.ShapeDtypeStruct((1, w), F32)],
        compiler_params=_params("arbitrary"),
    )(proj, proj, dy, dy, wp, scale)


def _neg_expm1(z):
    p = 1.0 + z * (1.0 / 9.0)
    for n in (8.0, 7.0, 6.0, 5.0, 4.0, 3.0, 2.0):
        p = 1.0 + z * (1.0 / n) * p
    return jnp.where(z > -0.5, -z * p, 1.0 - jnp.exp(z))


def _lru_gates(xc, wa, wx, ba, bx, lam):
    r = jax.nn.sigmoid(mm_nn(xc, wa) + ba)
    i = jax.nn.sigmoid(mm_nn(xc, wx) + bx)
    log_a = -LRU_C * r * _softplus(-lam)
    a = jnp.exp(log_a)
    return a, jnp.sqrt(_neg_expm1(2.0 * log_a)) * (i * xc)


def _scan_down(a, b):
    n = a.shape[0]
    row = _rows(a.shape)
    sh = 1
    while sh < n:
        keep = row >= sh
        a_sh = jnp.where(keep, pltpu.roll(a, sh, 0), 1.0)
        b_sh = jnp.where(keep, pltpu.roll(b, sh, 0), 0.0)
        b = a * b_sh + b
        a = a * a_sh
        sh *= 2
    return a, b


def _scan_up(a, b):
    n = a.shape[0]
    row = _rows(a.shape)
    sh = 1
    while sh < n:
        keep = row < n - sh
        a_sh = jnp.where(keep, pltpu.roll(a, n - sh, 0), 1.0)
        b_sh = jnp.where(keep, pltpu.roll(b, n - sh, 0), 0.0)
        b = a * b_sh + b
        a = a * a_sh
        sh *= 2
    return a, b


def lru_fwd(xc, proj, wa, wx, ba, bx, lam, *, tb, name):
    s = xc.shape[0]
    w = BRANCH_W

    def body(xc_ref, gate_ref, wa_ref, wx_ref, ba_ref, bx_ref, lam_ref, y_ref, h_ref, carry):
        @pl.when(pl.program_id(0) == 0)
        def _():
            carry[...] = jnp.zeros_like(carry)

        a, b = _lru_gates(xc_ref[...], wa_ref[...], wx_ref[...], ba_ref[...], bx_ref[...], lam_ref[...])
        pa, hb = _scan_down(a, b)
        h = pa * carry[...] + hb
        h_ref[...] = h
        carry[...] = h_ref[tb - 1:tb, :]
        y_ref[...] = (h * _gelu(gate_ref[...])).astype(BF16)

    row = pl.BlockSpec((tb, w), lambda i: (i, 0))
    return pl.pallas_call(
        body, name=name, grid=(s // tb,),
        in_specs=[row, _pblk(tb, w, C_BG), _full((w, w)), _full((w, w)),
                  _full((1, w)), _full((1, w)), _full((1, w))],
        out_specs=[row, row],
        out_shape=[jax.ShapeDtypeStruct((s, w), BF16), jax.ShapeDtypeStruct((s, w), F32)],
        scratch_shapes=[pltpu.VMEM((1, w), F32)],
        compiler_params=_params("arbitrary"),
    )(xc, proj, wa, wx, ba, bx, lam)


def lru_bwd(xc, proj, h, dy, wa, wx, ba, bx, lam, *, tb, name):
    s = xc.shape[0]
    w = BRANCH_W
    nt = s // tb

    def body(xc_ref, gate_ref, h_ref, hp_ref, dy_ref, wa_ref, wx_ref, ba_ref, bx_ref, lam_ref,
             dxc_ref, dgate_ref, dwa_ref, dwx_ref, dba_ref, dbx_ref, dlam_ref, carry):
        i = pl.program_id(0)

        @pl.when(i == 0)
        def _():
            carry[...] = jnp.zeros_like(carry)
            for r in (dwa_ref, dwx_ref, dba_ref, dbx_ref, dlam_ref):
                r[...] = jnp.zeros_like(r)

        (a, b), vf = jax.vjp(_lru_gates, xc_ref[...], wa_ref[...], wx_ref[...],
                             ba_ref[...], bx_ref[...], lam_ref[...])
        gate = gate_ref[...]
        dyv = dy_ref[...]
        hv = h_ref[...]
        row = _rows((tb, w))
        a_next = jnp.where(row < tb - 1, pltpu.roll(a, tb - 1, 0), 1.0)
        pa, xb = _scan_up(a_next, dyv * _gelu(gate))
        dh = xb + pa * carry[...]
        carry[...] = a[0:1, :] * dh[0:1, :]
        h_first = jnp.where(i < nt - 1, hp_ref[7:8, :], 0.0)
        h_prev = jnp.where(row >= 1, pltpu.roll(hv, 1, 0), h_first)
        dxc, dwa, dwx, dba, dbx, dlam = vf((dh * h_prev, dh))
        dxc_ref[...] = dxc
        dgate_ref[...] = (dyv * hv * _gelu_grad(gate)).astype(BF16)
        dwa_ref[...] += dwa
        dwx_ref[...] += dwx
        dba_ref[...] += dba
        dbx_ref[...] += dbx
        dlam_ref[...] += dlam

    row_spec = pl.BlockSpec((tb, w), lambda i: (nt - 1 - i, 0))
    hp_spec = pl.BlockSpec((8, w), lambda i: (jnp.maximum((nt - 1 - i) * (tb // 8) - 1, 0), 0))
    return pl.pallas_call(
        body, name=name, grid=(nt,),
        in_specs=[row_spec, _pblk(tb, w, C_BG, rev_n=nt), row_spec, hp_spec, row_spec,
                  _full((w, w)), _full((w, w)), _full((1, w)), _full((1, w)), _full((1, w))],
        out_specs=[row_spec, row_spec, _full((w, w)), _full((w, w)),
                   _full((1, w)), _full((1, w)), _full((1, w))],
        out_shape=[jax.ShapeDtypeStruct((s, w), F32), jax.ShapeDtypeStruct((s, w), BF16),
                   jax.ShapeDtypeStruct((w, w), F32), jax.ShapeDtypeStruct((w, w), F32),
                   jax.ShapeDtypeStruct((1, w), F32), jax.ShapeDtypeStruct((1, w), F32),
                   jax.ShapeDtypeStruct((1, w), F32)],
        scratch_shapes=[pltpu.VMEM((1, w), F32)],
        compiler_params=_params("arbitrary"),
    )(xc, proj, h, h, dy, wa, wx, ba, bx, lam)


def _gdn_g(ba, alog, dt):
    return -jnp.exp(alog) * _softplus(ba + dt)


def _gdn_local(qp, kp, vp, bpre, gc):
    b, c, dk = qp.shape
    q = _silu(qp)
    k = _silu(kp)
    v = _silu(vp)
    q = q * lax.rsqrt(jnp.sum(q * q, axis=-1, keepdims=True) + EPS) * (dk ** -0.5)
    k = k * lax.rsqrt(jnp.sum(k * k, axis=-1, keepdims=True) + EPS)
    beta = jax.nn.sigmoid(bpre)
    sq = (b, c, c)
    ii, jj = _iota3(sq, 1), _iota3(sq, 2)
    gr = jnp.sum(gc * (ii == jj).astype(F32), axis=1, keepdims=True)
    decay = jnp.exp(jnp.where(ii >= jj, gc - gr, -jnp.inf))
    wd = (b, c, 2 * c)
    wi, wj = _iota3(wd, 1), _iota3(wd, 2) - c
    gr_w = jnp.sum(gc * (wi == wj).astype(F32), axis=1, keepdims=True)
    decay_w = jnp.exp(jnp.where(jnp.logical_and(wj >= 0, wi > wj), gc - gr_w, -jnp.inf))
    kb = k * beta
    pad = jnp.zeros((b, c, dk), F32)
    t = _tri_inv(bmm_nt(kb, jnp.concatenate([pad, k], axis=1)) * decay_w)
    u = bmm_nn(t, jnp.concatenate([v * beta, pad], axis=1))
    wk = bmm_nn(t, jnp.concatenate([kb * jnp.exp(gc), pad], axis=1))
    attn = bmm_nt(q, k) * decay
    g_last = jnp.sum(jnp.where(_iota3((b, c, 1), 1) == c - 1, gc, 0.0), axis=1, keepdims=True)
    return u, wk, attn, q * jnp.exp(gc), k * jnp.exp(g_last - gc), jnp.exp(g_last)


def _gdn_recur(s, u, wk, attn, qg, kd, eg, zz, ng):
    v_new = u - bmm_nn(wk, s)
    o = bmm_nn(qg, s) + bmm_nn(attn, v_new)
    s_out = s * eg + bmm_tn(kd, v_new)
    o = o * lax.rsqrt(jnp.mean(o * o, axis=-1, keepdims=True) + EPS) * ng
    return o * _silu(zz), s_out


def _gdn_inputs(qkv_ref, ba, gcums, cps):
    c, hd = CHUNK, GDN_DK
    qs, ks, vs, bs, gs = [], [], [], [], []
    for ci in range(cps):
        rs = slice(c * ci, c * (ci + 1))
        for h in range(GDN_HEADS):
            qs.append(qkv_ref[rs, hd * h:hd * (h + 1)])
            ks.append(qkv_ref[rs, 512 + hd * h:512 + hd * (h + 1)])
            vs.append(qkv_ref[rs, 1024 + hd * h:1024 + hd * (h + 1)])
            bs.append(_col(ba[rs], h))
            gs.append(_col(gcums[ci], 4 + h))
    return tuple(jnp.stack(t) for t in (qs, ks, vs, bs, gs))


def _chunk_cumsum(g, transpose=False):
    c = CHUNK
    tri = (_rows((c, c)) >= _lanes((c, c))).astype(F32)
    return _dg3(tri, g, 0 if transpose else 1, 0)


def gdn_fwd(qkv, proj, alog, dt, ng, *, cps, name):
    s = qkv.shape[0]
    c = CHUNK
    nc = s // c
    hd = GDN_DK
    tb = c * cps

    def body(qkv_ref, z_ref, ba_ref, al_ref, dt_ref, ng_ref, y_ref, sall_ref, s_scr):
        @pl.when(pl.program_id(0) == 0)
        def _():
            s_scr[...] = jnp.zeros_like(s_scr)

        nh = GDN_HEADS
        ba = ba_ref[...]
        g = _gdn_g(ba, al_ref[...], dt_ref[...])
        gcums = [_chunk_cumsum(g[c * ci:c * (ci + 1)]) for ci in range(cps)]
        loc = _gdn_local(*_gdn_inputs(qkv_ref, ba, gcums, cps))
        for ci in range(cps):
            rs = slice(c * ci, c * (ci + 1))
            sl = slice(nh * ci, nh * (ci + 1))
            s_in = s_scr[...]
            sall_ref[ci] = s_in
            zz = jnp.stack([z_ref[rs, hd * h:hd * (h + 1)] for h in range(nh)])
            y, s_out = _gdn_recur(s_in, *(t[sl] for t in loc), zz, ng_ref[...])
            s_scr[...] = s_out
            for h in range(nh):
                y_ref[rs, hd * h:hd * (h + 1)] = y[h].astype(BF16)

    return pl.pallas_call(
        body, name=name, grid=(nc // cps,),
        in_specs=[pl.BlockSpec((tb, 1536), lambda i: (i, 0)), _pblk(tb, 512, C_Z), _pblk(tb, 128, C_BA),
                  _full((1, 128)), _full((1, 128)), _full((1, 128))],
        out_specs=[pl.BlockSpec((tb, 512), lambda i: (i, 0)),
                   pl.BlockSpec((cps, GDN_HEADS, hd, hd), lambda i: (i, 0, 0, 0))],
        out_shape=[jax.ShapeDtypeStruct((s, 512), BF16), jax.ShapeDtypeStruct((nc, GDN_HEADS, hd, hd), F32)],
        scratch_shapes=[pltpu.VMEM((GDN_HEADS, hd, hd), F32)],
        compiler_params=_params("arbitrary"),
    )(qkv, proj, proj, alog, dt, ng)


def gdn_bwd(qkv, proj, sall, dy, alog, dt, ng, *, cps, name):
    s = qkv.shape[0]
    c = CHUNK
    nc = s // c
    hd = GDN_DK
    tb = c * cps
    nt = nc // cps

    def body(qkv_ref, z_ref, ba_ref, sall_ref, dy_ref, al_ref, dt_ref, ng_ref,
             dqkv_ref, dz_ref, dba_ref, dal_ref, ddt_ref, dng_ref, ds_scr):
        @pl.when(pl.program_id(0) == 0)
        def _():
            ds_scr[...] = jnp.zeros_like(ds_scr)
            dal_ref[...] = jnp.zeros_like(dal_ref)
            ddt_ref[...] = jnp.zeros_like(ddt_ref)
            dng_ref[...] = jnp.zeros_like(dng_ref)

        nh = GDN_HEADS
        ba = ba_ref[...]
        g, gvf = jax.vjp(_gdn_g, ba, al_ref[...], dt_ref[...])
        gcums = [_chunk_cumsum(g[c * ci:c * (ci + 1)]) for ci in range(cps)]
        loc, loc_vf = jax.vjp(_gdn_local, *_gdn_inputs(qkv_ref, ba, gcums, cps))
        d_loc = [None] * cps
        dng = jnp.zeros((1, 128), F32)
        for ci in reversed(range(cps)):
            rs = slice(c * ci, c * (ci + 1))
            sl = slice(nh * ci, nh * (ci + 1))
            zz = jnp.stack([z_ref[rs, hd * h:hd * (h + 1)] for h in range(nh)])
            dyy = jnp.stack([dy_ref[rs, hd * h:hd * (h + 1)] for h in range(nh)])
            _, rvf = jax.vjp(_gdn_recur, sall_ref[ci], *(t[sl] for t in loc), zz, ng_ref[...])
            ds, du, dwk, dattn, dqg, dkd, deg, dzz, dn = rvf((dyy, ds_scr[...]))
            ds_scr[...] = ds
            d_loc[ci] = (du, dwk, dattn, dqg, dkd, deg)
            dng = dng + dn
            for h in range(nh):
                dz_ref[rs, hd * h:hd * (h + 1)] = dzz[h].astype(BF16)
        dq, dk, dv, dbp, dgc = loc_vf(tuple(jnp.concatenate([d_loc[ci][j] for ci in range(cps)], axis=0)
                                            for j in range(6)))
        dgs, dbas = [], []
        for ci in range(cps):
            rs = slice(c * ci, c * (ci + 1))
            dgcum = jnp.zeros((c, 128), F32)
            dba = jnp.zeros((c, 128), F32)
            for h in range(nh):
                b = nh * ci + h
                dqkv_ref[rs, hd * h:hd * (h + 1)] = dq[b]
                dqkv_ref[rs, 512 + hd * h:512 + hd * (h + 1)] = dk[b]
                dqkv_ref[rs, 1024 + hd * h:1024 + hd * (h + 1)] = dv[b]
                dgcum = dgcum + _put_col(dgc[b], 4 + h, 128)
                dba = dba + _put_col(dbp[b], h, 128)
            dgs.append(_chunk_cumsum(dgcum, transpose=True))
            dbas.append(dba)
        dba2, dal, ddt = gvf(jnp.concatenate(dgs, axis=0))
        dba_ref[...] = (jnp.concatenate(dbas, axis=0) + dba2).astype(BF16)
        dal_ref[...] += dal
        ddt_ref[...] += ddt
        dng_ref[...] += dng

    def rev(width):
        return pl.BlockSpec((tb, width), lambda i: (nt - 1 - i, 0))

    return pl.pallas_call(
        body, name=name, grid=(nt,),
        in_specs=[rev(1536), _pblk(tb, 512, C_Z, rev_n=nt), _pblk(tb, 128, C_BA, rev_n=nt),
                  pl.BlockSpec((cps, GDN_HEADS, hd, hd), lambda i: (nt - 1 - i, 0, 0, 0)), rev(512),
                  _full((1, 128)), _full((1, 128)), _full((1, 128))],
        out_specs=[rev(1536), rev(512), rev(128), _full((1, 128)), _full((1, 128)), _full((1, 128))],
        out_shape=[jax.ShapeDtypeStruct((s, 1536), F32), jax.ShapeDtypeStruct((s, 512), BF16),
                   jax.ShapeDtypeStruct((s, 128), BF16), jax.ShapeDtypeStruct((1, 128), F32),
                   jax.ShapeDtypeStruct((1, 128), F32), jax.ShapeDtypeStruct((1, 128), F32)],
        scratch_shapes=[pltpu.VMEM((GDN_HEADS, hd, hd), F32)],
        compiler_params=_params("arbitrary"),
    )(qkv, proj, proj, sall, dy, alog, dt, ng)


def merge_fwd(x, ys, proj, wb, wo, *, tb, dep=None, name):
    s, d = x.shape
    dep_specs, dep_args = _dep_operand(dep)

    def body(x_ref, ya, yb, yc, yd, g0, g1, g2, g3, wb_ref, wo_ref, *rest):
        o_ref, m_ref = rest[len(dep_args):]
        m = jnp.zeros((tb, d), F32)
        for g, (y_ref, gp_ref) in enumerate(zip((ya, yb, yc, yd), (g0, g1, g2, g3))):
            m = m + jax.nn.sigmoid(gp_ref[...]) * _dg(y_ref[...], wb_ref[g], 1, 0)
        mb = m.astype(BF16)
        m_ref[...] = mb
        o_ref[...] = x_ref[...] + _dg(mb, wo_ref[...], 1, 0)

    row = pl.BlockSpec((tb, d), lambda i: (i, 0))
    yrow = pl.BlockSpec((tb, BRANCH_W), lambda i: (i, 0))
    return pl.pallas_call(
        body, name=name, grid=(s // tb,),
        in_specs=[row] + [yrow] * 4 + [_pblk(tb, d, C_GATE + d * g) for g in range(4)]
        + [_full((4, BRANCH_W, d)), _full((d, d))] + dep_specs,
        out_specs=[row, row],
        out_shape=[jax.ShapeDtypeStruct((s, d), F32), jax.ShapeDtypeStruct((s, d), BF16)],
        compiler_params=_params("parallel"),
    )(x, *ys, proj, proj, proj, proj, wb, wo, *dep_args)


def merge_bwd(dx, ys, proj, wb, wo, *, tb, dep=None, name):
    s, d = dx.shape
    dep_specs, dep_args = _dep_operand(dep)

    def body(dx_ref, ya, yb, yc, yd, g0, g1, g2, g3, wb_ref, wo_ref, *rest):
        dgp_ref, dbr_ref, da, db, dc, dd, dxb_ref = rest[len(dep_args):]
        dxb = dx_ref[...].astype(BF16)
        dxb_ref[...] = dxb
        dm = _dg(dxb, wo_ref[...], 1, 1)
        for g, (y_ref, gp_ref, dy_ref) in enumerate(zip((ya, yb, yc, yd), (g0, g1, g2, g3), (da, db, dc, dd))):
            br = _dg(y_ref[...], wb_ref[g], 1, 0)
            sg = jax.nn.sigmoid(gp_ref[...])
            dgp_ref[:, d * g:d * (g + 1)] = (dm * br * sg * (1.0 - sg)).astype(BF16)
            dbr = (dm * sg).astype(BF16)
            dbr_ref[g] = dbr
            dy_ref[...] = _dg(dbr, wb_ref[g], 1, 1)

    row = pl.BlockSpec((tb, d), lambda i: (i, 0))
    yrow = pl.BlockSpec((tb, BRANCH_W), lambda i: (i, 0))
    return pl.pallas_call(
        body, name=name, grid=(s // tb,),
        in_specs=[row] + [yrow] * 4 + [_pblk(tb, d, C_GATE + d * g) for g in range(4)]
        + [_full((4, BRANCH_W, d)), _full((d, d))] + dep_specs,
        out_specs=[pl.BlockSpec((tb, 4 * d), lambda i: (i, 0)), pl.BlockSpec((4, tb, d), lambda i: (0, i, 0)),
                   yrow, yrow, yrow, yrow, row],
        out_shape=[jax.ShapeDtypeStruct((s, 4 * d), BF16), jax.ShapeDtypeStruct((4, s, d), BF16)]
        + [jax.ShapeDtypeStruct((s, BRANCH_W), F32)] * 4 + [jax.ShapeDtypeStruct((s, d), BF16)],
        compiler_params=_params("parallel"),
    )(dx, *ys, proj, proj, proj, proj, wb, wo, *dep_args)


def loss_head(x, g, target, *, tb, name):
    s, d = x.shape

    def body(x_ref, g_ref, t_ref, dx_ref, dg_ref, loss_ref):
        @pl.when(pl.program_id(0) == 0)
        def _():
            dg_ref[...] = jnp.zeros_like(dg_ref)
            loss_ref[...] = jnp.zeros_like(loss_ref)

        xv, gv = x_ref[...], g_ref[...]
        r = lax.rsqrt(jnp.mean(xv * xv, axis=-1, keepdims=True) + EPS)
        err = xv * r * gv - t_ref[...]
        loss_ref[...] += 0.5 * jnp.sum(jnp.mean(err * err, axis=-1, keepdims=True), axis=0, keepdims=True)
        dx, dg = _rms_bwd(xv, gv, err * (1.0 / d))
        dx_ref[...] = dx
        dg_ref[...] += dg

    row = pl.BlockSpec((tb, d), lambda i: (i, 0))
    return pl.pallas_call(
        body, name=name, grid=(s // tb,),
        in_specs=[row, _full((1, d)), row],
        out_specs=[row, _full((1, d)), _full((1, 128))],
        out_shape=[jax.ShapeDtypeStruct((s, d), F32), jax.ShapeDtypeStruct((1, d), F32),
                   jax.ShapeDtypeStruct((1, 128), F32)],
        compiler_params=_params("arbitrary"),
    )(x, g, target)


def adamw(w, g, m, v, *, name):
    shape = w.shape
    lead, (r, c) = shape[:-2], shape[-2:]
    tr = r
    while tr * c * 4 > (1 << 20) and tr % 16 == 0:
        tr //= 2
    c1 = 1.0 / (1.0 - ADAM_B1 ** ADAM_STEP)
    c2 = 1.0 / (1.0 - ADAM_B2 ** ADAM_STEP)

    def body(w_ref, g_ref, m_ref, v_ref, d_ref, nm_ref, nv_ref):
        gv = g_ref[...]
        nm = ADAM_B1 * m_ref[...] + (1.0 - ADAM_B1) * gv
        nv = ADAM_B2 * v_ref[...] + (1.0 - ADAM_B2) * (gv * gv)
        nm_ref[...] = nm
        nv_ref[...] = nv
        d_ref[...] = -ADAM_LR * ((nm * c1) / (jnp.sqrt(nv * c2) + ADAM_EPS) + ADAM_WD * w_ref[...])

    blk = pl.BlockSpec((None,) * len(lead) + (tr, c), lambda *i: i + (0,))
    return tuple(pl.pallas_call(
        body, name=name, grid=lead + (r // tr,),
        in_specs=[blk] * 4, out_specs=[blk] * 3,
        out_shape=[jax.ShapeDtypeStruct(shape, F32)] * 3,
        compiler_params=_params(*(("parallel",) * (len(lead) + 1))),
    )(w, g, m, v))


_ANY = pl.BlockSpec(memory_space=pl.ANY)


def _place():
    x, y, c = lax.axis_index("x"), lax.axis_index("y"), lax.axis_index("c")
    return x, y, c, [(1 - x, y), (x, 1 - y), (1 - x, 1 - y)]


def gather_shards(bufs, *, name):
    n = len(bufs)

    def body(*refs):
        outs = refs[n:2 * n]
        ici_send, ici_recv, d2d_send, d2d_recv = refs[2 * n:]
        x, y, c, chips = _place()
        me = 2 * x + y

        def ici(a, j, slab, to):
            return pltpu.make_async_remote_copy(
                src_ref=outs[a].at[slab, c], dst_ref=outs[a].at[slab, c], send_sem=ici_send.at[a, j],
                recv_sem=ici_recv.at[a, j], device_id=to, device_id_type=MESH)

        def d2d(a, j, slab, half):
            return pltpu.make_async_remote_copy(
                src_ref=outs[a].at[slab, half], dst_ref=outs[a].at[slab, half], send_sem=d2d_send.at[a, j],
                recv_sem=d2d_recv.at[a, j], device_id=(x, y, 1 - c), device_id_type=MESH)

        sends = []
        for a in range(n):
            for j, (px, py) in enumerate(chips):
                cp = ici(a, j, me, (px, py, c))
                cp.start()
                sends.append(cp)
        for a in range(n):
            for j, (px, py) in enumerate(chips):
                ici(a, j, 2 * px + py, (px, py, c)).wait_recv()
                cp = d2d(a, j, 2 * px + py, c)
                cp.start()
                sends.append(cp)
        for a in range(n):
            for j, (px, py) in enumerate(chips):
                d2d(a, j, 2 * px + py, 1 - c).wait_recv()
        for cp in sends:
            cp.wait_send()

    return pl.pallas_call(
        body, name=name,
        in_specs=[_ANY] * n, out_specs=[_ANY] * n,
        out_shape=[jax.ShapeDtypeStruct(t.shape, t.dtype) for t in bufs],
        input_output_aliases={a: a for a in range(n)},
        scratch_shapes=[pltpu.SemaphoreType.DMA((n, 3))] * 4,
    )(*bufs)


_SEM = pl.BlockSpec(memory_space=pltpu.SEMAPHORE)
_VMEM = pl.BlockSpec(memory_space=pltpu.VMEM)
_EFFECT = pltpu.SideEffectType.DATAFLOW_SIDE_EFFECTING
_TOKEN = jax.ShapeDtypeStruct((8, 128), F32)


def _gather_copies(outs, sems_ici, sems_d2d):
    x, y, c, chips = _place()

    def ici(a, j, slab, to):
        return pltpu.make_async_remote_copy(
            src_ref=outs[a].at[slab, c], dst_ref=outs[a].at[slab, c], send_sem=sems_ici[0][3 * a + j],
            recv_sem=sems_ici[1][3 * a + j], device_id=to, device_id_type=MESH)

    def d2d(a, j, slab, half):
        return pltpu.make_async_remote_copy(
            src_ref=outs[a].at[slab, half], dst_ref=outs[a].at[slab, half], send_sem=sems_d2d[0][3 * a + j],
            recv_sem=sems_d2d[1][3 * a + j], device_id=(x, y, 1 - c), device_id_type=MESH)

    return x, y, c, chips, ici, d2d


def _async_call(body, bufs, sems_in, after, sems_out, token, *, name):
    n = len(bufs)
    n_out = 6 * n if sems_out else 0
    extra = [] if after is None else [after]
    return pl.pallas_call(
        body, name=name,
        in_specs=[_ANY] * n + [_SEM] * len(sems_in) + [_ANY] * len(extra),
        out_specs=[_ANY] * n + [_SEM] * n_out + [_VMEM] * token,
        out_shape=[jax.ShapeDtypeStruct(t.shape, t.dtype) for t in bufs]
        + [pltpu.SemaphoreType.DMA(())] * n_out + [_TOKEN] * token,
        input_output_aliases={a: a for a in range(n)},
        compiler_params=pltpu.CompilerParams(has_side_effects=_EFFECT),
    )(*bufs, *sems_in, *extra)


def gather_start(bufs, *, name):
    n = len(bufs)

    def body(*refs):
        outs = refs[n:2 * n]
        sems, token = refs[2 * n:8 * n], refs[8 * n]
        x, y, c, chips, ici, _ = _gather_copies(outs, (sems[:3 * n], sems[3 * n:]), None)
        for a in range(n):
            for j, (px, py) in enumerate(chips):
                ici(a, j, 2 * x + y, (px, py, c)).start()
        token[...] = jnp.zeros_like(token)

    res = _async_call(body, bufs, [], None, True, True, name=name)
    return res[:n], res[n:7 * n], res[7 * n]


def gather_pass(bufs, sems, after, *, name):
    n = len(bufs)

    def body(*refs):
        s_in = refs[n:7 * n]
        outs = refs[7 * n + 1:8 * n + 1]
        s_out, token = refs[8 * n + 1:14 * n + 1], refs[14 * n + 1]
        x, y, c, chips, ici, d2d = _gather_copies(outs, (s_in[:3 * n], s_in[3 * n:]),
                                                  (s_out[:3 * n], s_out[3 * n:]))
        for a in range(n):
            for j, (px, py) in enumerate(chips):
                ici(a, j, 2 * px + py, (px, py, c)).wait_recv()
                d2d(a, j, 2 * px + py, c).start()
        for a in range(n):
            for j, (px, py) in enumerate(chips):
                ici(a, j, 2 * x + y, (px, py, c)).wait_send()
        token[...] = jnp.zeros_like(token)

    res = _async_call(body, bufs, list(sems), after, True, True, name=name)
    return res[:n], res[n:7 * n], res[7 * n]


def gather_wait(bufs, sems, after, *, name):
    n = len(bufs)

    def body(*refs):
        s_in = refs[n:7 * n]
        outs = refs[7 * n + 1:]
        x, y, c, chips, _, d2d = _gather_copies(outs, None, (s_in[:3 * n], s_in[3 * n:]))
        for a in range(n):
            for j, (px, py) in enumerate(chips):
                d2d(a, j, 2 * px + py, 1 - c).wait_recv()
                d2d(a, j, 2 * px + py, c).wait_send()

    return _async_call(body, bufs, list(sems), after, False, False, name=name)


def _pair_plan(srcs, lands):
    x, y, c, _ = _place()
    return [(srcs[a].at[j, 1 - c], lands[a].at[j], (x, y, 1 - c))
            for a in range(len(srcs)) for j in range(N_SHARD)]


def _chip_plan(srcs, lands):
    x, y, c, chips = _place()
    return [(srcs[a].at[2 * px + py], lands[a].at[2 * x + y], (px, py, c))
            for a in range(len(srcs)) for (px, py) in chips]


def copies_start(arrs, land_shapes, plan, per_array, *, name):
    n = len(arrs)
    k = per_array * n

    def body(*refs):
        lands = refs[2 * n:3 * n]
        sems, token = refs[3 * n:3 * n + 2 * k], refs[3 * n + 2 * k]
        for i, (src, dst, dev) in enumerate(plan(refs[n:2 * n], lands)):
            pltpu.make_async_remote_copy(src_ref=src, dst_ref=dst, send_sem=sems[i], recv_sem=sems[k + i],
                                         device_id=dev, device_id_type=MESH).start()
        token[...] = jnp.zeros_like(token)

    res = pl.pallas_call(
        body, name=name,
        in_specs=[_ANY] * n,
        out_specs=[_ANY] * (2 * n) + [_SEM] * (2 * k) + [_VMEM],
        out_shape=[jax.ShapeDtypeStruct(t.shape, t.dtype) for t in arrs] + list(land_shapes)
        + [pltpu.SemaphoreType.DMA(())] * (2 * k) + [_TOKEN],
        input_output_aliases={a: a for a in range(n)},
        compiler_params=pltpu.CompilerParams(has_side_effects=_EFFECT),
    )(*arrs)
    return res[:n], res[n:2 * n], res[2 * n:2 * n + 2 * k], res[2 * n + 2 * k]


def copies_wait(arrs, lands, sems, after, plan, *, name):
    n = len(arrs)
    k = len(sems) // 2

    def body(*refs):
        s_in = refs[2 * n:2 * n + 2 * k]
        outs = refs[2 * n + 2 * k + 1:]
        for i, (src, dst, dev) in enumerate(plan(outs[:n], outs[n:])):
            cp = pltpu.make_async_remote_copy(src_ref=src, dst_ref=dst, send_sem=s_in[i], recv_sem=s_in[k + i],
                                              device_id=dev, device_id_type=MESH)
            cp.wait_send()
            cp.wait_recv()

    res = pl.pallas_call(
        body, name=name,
        in_specs=[_ANY] * (2 * n) + [_SEM] * (2 * k) + [_ANY],
        out_specs=[_ANY] * (2 * n),
        out_shape=[jax.ShapeDtypeStruct(t.shape, t.dtype) for t in list(arrs) + list(lands)],
        input_output_aliases={a: a for a in range(2 * n)},
        compiler_params=pltpu.CompilerParams(has_side_effects=_EFFECT),
    )(*arrs, *lands, *sems, after)
    return res[:n], res[n:]


def pair_send_halves(arrs, *, name):
    n = len(arrs)

    def body(*refs):
        ins, outs = refs[:n], refs[n:2 * n]
        send_sems, recv_sems = refs[2 * n:]
        x, y, c, _ = _place()
        copies = []
        for a in range(n):
            for j in range(N_SHARD):
                cp = pltpu.make_async_remote_copy(
                    src_ref=ins[a].at[j, 1 - c], dst_ref=outs[a].at[j], send_sem=send_sems.at[a, j],
                    recv_sem=recv_sems.at[a, j], device_id=(x, y, 1 - c), device_id_type=MESH)
                cp.start()
                copies.append(cp)
        for cp in copies:
            cp.wait()

    return pl.pallas_call(
        body, name=name,
        in_specs=[_ANY] * n, out_specs=[_ANY] * n,
        out_shape=[jax.ShapeDtypeStruct((N_SHARD,) + t.shape[2:], t.dtype) for t in arrs],
        scratch_shapes=[pltpu.SemaphoreType.DMA((n, N_SHARD)), pltpu.SemaphoreType.DMA((n, N_SHARD))],
    )(*arrs)


def chip_scatter(arrs, *, name):
    n = len(arrs)

    def body(*refs):
        ins, outs = refs[:n], refs[n:2 * n]
        send_sems, recv_sems = refs[2 * n:]
        x, y, c, chips = _place()
        me = 2 * x + y
        copies = []
        for a in range(n):
            for j, (px, py) in enumerate(chips):
                cp = pltpu.make_async_remote_copy(
                    src_ref=ins[a].at[2 * px + py], dst_ref=outs[a].at[me], send_sem=send_sems.at[a, j],
                    recv_sem=recv_sems.at[a, j], device_id=(px, py, c), device_id_type=MESH)
                cp.start()
                copies.append(cp)
        for cp in copies:
            cp.wait()

    return pl.pallas_call(
        body, name=name,
        in_specs=[_ANY] * n, out_specs=[_ANY] * n,
        out_shape=[jax.ShapeDtypeStruct(t.shape, t.dtype) for t in arrs],
        scratch_shapes=[pltpu.SemaphoreType.DMA((n, 3)), pltpu.SemaphoreType.DMA((n, 3))],
    )(*arrs)


def pair_gather(bufs, *, name):
    n = len(bufs)

    def body(*refs):
        outs = refs[n:2 * n]
        send_sems, recv_sems = refs[2 * n:]
        x, y, c, _ = _place()
        copies = []
        for a in range(n):
            cp = pltpu.make_async_remote_copy(
                src_ref=outs[a].at[c], dst_ref=outs[a].at[c], send_sem=send_sems.at[a],
                recv_sem=recv_sems.at[a], device_id=(x, y, 1 - c), device_id_type=MESH)
            cp.start()
            copies.append(cp)
        for cp in copies:
            cp.wait()

    return pl.pallas_call(
        body, name=name,
        in_specs=[_ANY] * n, out_specs=[_ANY] * n,
        out_shape=[jax.ShapeDtypeStruct(t.shape, t.dtype) for t in bufs],
        input_output_aliases={a: a for a in range(n)},
        scratch_shapes=[pltpu.SemaphoreType.DMA((n,)), pltpu.SemaphoreType.DMA((n,))],
    )(*bufs)


def all_sum_small(v, *, name):
    m = v.shape[0]

    def body(v_ref, o_ref, buf, send_sems, recv_sems, local_sem):
        x, y, c, chips = _place()
        me, sibling = (x, y, c), (x, y, 1 - c)

        def rows(px, py, pc):
            return buf.at[pl.ds((4 * px + 2 * py + pc) * m, m), :]

        def copy(k, block, to, src=None):
            return pltpu.make_async_remote_copy(
                src_ref=rows(*block) if src is None else src, dst_ref=rows(*block),
                send_sem=send_sems.at[k], recv_sem=recv_sems.at[k], device_id=to, device_id_type=MESH)

        mine = pltpu.make_async_copy(v_ref, rows(*me), local_sem)
        mine.start()
        first = [copy(0, me, sibling, src=v_ref)]
        first += [copy(1 + j, me, (*chip, c), src=v_ref) for j, chip in enumerate(chips)]
        for cp in first:
            cp.start()
        passed = [copy(4 + j, (*chip, c), sibling) for j, chip in enumerate(chips)]
        for j, chip in enumerate(chips):
            copy(1 + j, (*chip, c), me).wait_recv()
            passed[j].start()
        copy(0, sibling, me).wait_recv()
        for j, chip in enumerate(chips):
            copy(4 + j, (*chip, 1 - c), me).wait_recv()
        for cp in first + passed:
            cp.wait_send()
        mine.wait()
        acc = buf[pl.ds(0, m), :]
        for k in range(1, 8):
            acc = acc + buf[pl.ds(k * m, m), :]
        o_ref[...] = acc

    vm = pl.BlockSpec(memory_space=pltpu.VMEM)
    return pl.pallas_call(
        body, name=name, in_specs=[vm], out_specs=vm,
        out_shape=jax.ShapeDtypeStruct((m, 128), F32),
        scratch_shapes=[pltpu.VMEM((8 * m, 128), F32), pltpu.SemaphoreType.DMA((7,)),
                        pltpu.SemaphoreType.DMA((7,)), pltpu.SemaphoreType.DMA],
    )(v)


def pack_slab(srcs, blocks, maps, n_slots, slot_shape, *, place, dep=None, name):
    k = len(srcs)
    per = n_slots // k
    dep_specs, dep_args = _dep_operand(dep)

    def body(place_ref, *refs):
        o_ref = refs[k + len(dep_args)]
        s = pl.program_id(0)
        for i in range(k):
            @pl.when(jnp.logical_and(s >= i * per, s < (i + 1) * per))
            def _():
                o_ref[...] = refs[i][...].astype(BF16)

    return pl.pallas_call(
        body, name=name,
        grid_spec=pltpu.PrefetchScalarGridSpec(
            num_scalar_prefetch=1, grid=(n_slots,),
            in_specs=[pl.BlockSpec(blk, functools.partial(lambda s, pr, m: m(s), m=m)) for blk, m in zip(blocks, maps)]
            + dep_specs,
            out_specs=pl.BlockSpec((None, None) + slot_shape, lambda s, pr: (pr[1], s, 0, 0))),
        out_shape=jax.ShapeDtypeStruct((N_SHARD, n_slots) + slot_shape, BF16),
        compiler_params=_params("arbitrary"),
    )(place, *srcs, *dep_args)


def sum_halves(arr, recv, c_idx, *, name):
    _, _, r, c = arr.shape
    tr = r
    while tr * c * 4 > (1 << 20) and tr % 32 == 0:
        tr //= 2

    def body(c_ref, a_ref, b_ref, o_ref):
        o_ref[...] = (a_ref[...] + b_ref[...]).astype(BF16)

    return pl.pallas_call(
        body, name=name,
        grid_spec=pltpu.PrefetchScalarGridSpec(
            num_scalar_prefetch=1, grid=(N_SHARD, r // tr),
            in_specs=[pl.BlockSpec((None, None, tr, c), lambda j, i, cr: (j, cr[0], i, 0)),
                      pl.BlockSpec((None, tr, c), lambda j, i, cr: (j, i, 0))],
            out_specs=pl.BlockSpec((None, tr, c), lambda j, i, cr: (j, i, 0))),
        out_shape=jax.ShapeDtypeStruct((N_SHARD, r, c), BF16),
        compiler_params=_params("parallel", "parallel"),
    )(c_idx, arr, recv)


def sum_chips(recv, part, place, *, name):
    _, r, c = recv.shape
    tr = r
    while tr * c * 4 > (1 << 20) and tr % 32 == 0:
        tr //= 2

    def body(place_ref, a0, a1, a2, a3, p_ref, o_ref):
        me = place_ref[1]
        acc = None
        for k, a_ref in enumerate((a0, a1, a2, a3)):
            term = jnp.where(me == k, p_ref[...], a_ref[...]).astype(F32)
            acc = term if acc is None else acc + term
        o_ref[...] = acc

    def slab(k):
        return pl.BlockSpec((None, tr, c), lambda i, pr: (jnp.where(pr[1] == k, (k + 1) % N_SHARD, k), i, 0))

    return pl.pallas_call(
        body, name=name,
        grid_spec=pltpu.PrefetchScalarGridSpec(
            num_scalar_prefetch=1, grid=(r // tr,),
            in_specs=[slab(k) for k in range(N_SHARD)]
            + [pl.BlockSpec((None, tr, c), lambda i, pr: (pr[1], i, 0))],
            out_specs=pl.BlockSpec((None, tr, c), lambda i, pr: (pr[0], i, 0))),
        out_shape=jax.ShapeDtypeStruct((2, r, c), F32),
        compiler_params=_params("parallel"),
    )(place, recv, recv, recv, recv, part)


_WEIGHTS = ("ff1_norm", "ff1_wg", "ff1_wu", "ff1_wd", "mix_norm", "w_in", "sgu_ln_g", "sgu_ln_b", "sgu_w",
            "sgu_b", "lru_conv_w", "lru_conv_b", "lru_wa", "lru_ba", "lru_wx", "lru_bx", "lru_lambda",
            "gdn_conv_w", "gdn_a_log", "gdn_dt_bias", "gdn_norm_g", "pool_w", "pool_scale", "w_branch",
            "w_out", "ff2_norm", "ff2_wg", "ff2_wu", "ff2_wd", "final_norm")
_BIG = ("ff1_wg", "ff1_wu", "ff1_wd", "w_in", "w_branch", "w_out", "ff2_wg", "ff2_wu", "ff2_wd")
_SMALL = tuple(n for n in _WEIGHTS if n not in _BIG)
_CONV = ("lru_conv_w", "gdn_conv_w")


def _seg_rows(shape):
    return -(-math.prod(shape) // 1024) * 8


def _pack(arrs):
    segs = []
    for t in arrs:
        rows = _seg_rows(t.shape)
        flat = t.reshape(-1)
        segs.append(jnp.pad(flat, (0, rows * 128 - flat.shape[0])).reshape(rows, 128))
    return jnp.concatenate(segs, axis=0)


def _unpack(buf, shapes):
    out, o = [], 0
    for shp in shapes:
        rows = _seg_rows(shp)
        out.append(buf[o:o + rows].reshape(-1)[:math.prod(shp)].reshape(shp))
        o += rows
    return out


def _block_diag(w):
    h, n, _ = w.shape
    same = jnp.arange(h)[:, None, None, None] == jnp.arange(h)[None, None, :, None]
    return jnp.where(same, w[:, :, None, :], 0.0).reshape(h * n, h * n)


def _diag_blocks(m, h=8, n=64):
    return jnp.stack([m[i * n:(i + 1) * n, i * n:(i + 1) * n] for i in range(h)])


def _lane_row(v):
    return jnp.zeros((1, 128), F32).at[0, 4:8].set(v)


def _w_in_to_padded(raw):
    sem = jnp.concatenate([raw[k, :, :P_IN_SHARD] for k in range(N_SHARD)], axis=1)
    zero = jnp.zeros((sem.shape[0], C_GATE - C_BA - 8), sem.dtype)
    return jnp.concatenate([sem[:, 2048:4096], sem[:, 0:2048], sem[:, 4104:4616], sem[:, 4096:4104], zero,
                            sem[:, 4616:P_IN]], axis=1)


def _w_in_grad_to_shards(part):
    r = part.shape[1]
    dwp = part.transpose(1, 0, 2).reshape(r, P_PAD)
    sem = jnp.concatenate([dwp[:, C_AU:C_DX], dwp[:, 0:C_AU], dwp[:, C_BA:C_BA + 8], dwp[:, C_DX:C_BA],
                           dwp[:, C_GATE:]], axis=1)
    sh = sem.reshape(r, N_SHARD, P_IN_SHARD).transpose(1, 0, 2)
    return jnp.pad(sh, ((0, 0), (0, 0), (0, P_IN_SHARD_PAD - P_IN_SHARD)))


def kernel(x, ff1_norm, ff1_wg, ff1_wu, ff1_wd, mix_norm, w_in, sgu_ln_g, sgu_ln_b, sgu_w, sgu_b, lru_conv_w,
           lru_conv_b, lru_wa, lru_ba, lru_wx, lru_bx, lru_lambda, gdn_conv_w, gdn_a_log, gdn_dt_bias, gdn_norm_g,
           pool_w, pool_scale, w_branch, w_out, ff2_norm, ff2_wg, ff2_wu, ff2_wd, final_norm, loss_target, m_ff1_norm,
           m_ff1_wg, m_ff1_wu, m_ff1_wd, m_mix_norm, m_w_in, m_sgu_ln_g, m_sgu_ln_b, m_sgu_w, m_sgu_b, m_lru_conv_w,
           m_lru_conv_b, m_lru_wa, m_lru_ba, m_lru_wx, m_lru_bx, m_lru_lambda, m_gdn_conv_w, m_gdn_a_log, m_gdn_dt_bias,
           m_gdn_norm_g, m_pool_w, m_pool_scale, m_w_branch, m_w_out, m_ff2_norm, m_ff2_wg, m_ff2_wu, m_ff2_wd,
           m_final_norm, v_ff1_norm, v_ff1_wg, v_ff1_wu, v_ff1_wd, v_mix_norm, v_w_in, v_sgu_ln_g, v_sgu_ln_b, v_sgu_w,
           v_sgu_b, v_lru_conv_w, v_lru_conv_b, v_lru_wa, v_lru_ba, v_lru_wx, v_lru_bx, v_lru_lambda, v_gdn_conv_w,
           v_gdn_a_log, v_gdn_dt_bias, v_gdn_norm_g, v_pool_w, v_pool_scale, v_w_branch, v_w_out, v_ff2_norm, v_ff2_wg,
           v_ff2_wu, v_ff2_wd, v_final_norm):
    a = dict(locals())
    n_layer = ff1_norm.shape[0]
    d = D_MODEL
    x0 = x[0]
    chip = 2 * lax.axis_index("x") + lax.axis_index("y")
    c_idx = jnp.reshape(lax.axis_index("c"), (1,)).astype(jnp.int32)
    place = jnp.stack([lax.axis_index("c"), chip]).astype(jnp.int32)

    def own_slab(t, dtype=BF16):
        buf = lax.empty((N_SHARD,) + t.shape, dtype)
        return lax.dynamic_update_slice(buf, t.astype(dtype)[None], (chip,) + (0,) * t.ndim)

    assert n_layer == 2

    def layer_bufs(l, dep=None):
        fc = FF_SHARD
        whole = lambda s: (l, 0, 0)
        rows = lambda s: (l, s, 0)
        pack = functools.partial(pack_slab, place=place, dep=dep)
        bufs = []
        for tag, wg, wu, wd in (("ffn1", ff1_wg, ff1_wu, ff1_wd), ("ffn2", ff2_wg, ff2_wu, ff2_wd)):
            bufs.append([pack([wg, wu], [(None, d, fc)] * 2, [whole] * 2, 2, (d, fc), name=f"pack_{tag}_wgu_{l}"),
                         pack([wd], [(None, fc // 2, d)], [rows], 2, (fc // 2, d), name=f"pack_{tag}_wd_{l}")])
        win = pack([w_in], [(None, d // 2, P_IN_SHARD_PAD)], [rows], 2, (d // 2, P_IN_SHARD_PAD),
                   name=f"pack_w_in_{l}")
        wb = pack([w_branch], [(None, None, BRANCH_W, d // 4)], [lambda s: (l, s, 0, 0)], 4, (BRANCH_W, d // 4),
                  name=f"pack_wb_{l}")
        wo = pack([w_out], [(None, d // 8, d)], [rows], 2, (d // 8, d), name=f"pack_wo_{l}")
        return bufs[0] + [win, wb.reshape(N_SHARD, 2, 2, BRANCH_W, d // 4), wo] + bufs[1]

    n_buf = 7
    groups = {"ffn1": (0, 2), "mix": (2, 5), "ffn2": (5, 7)}

    def group_sems(sems, lo, hi):
        return list(sems[3 * lo:3 * hi]) + list(sems[3 * n_buf + 3 * lo:3 * n_buf + 3 * hi])

    def ffn_weights(wgu, wd):
        return (wgu, wgu, wd.reshape(N_SHARD, 1, FF_SHARD, d), (0, 1, 0))

    def mix_weights(g_w_in, g_wb, g_wo):
        return {"w_in": _w_in_to_padded(g_w_in.reshape(4, d, P_IN_SHARD_PAD)),
                "wb": g_wb.reshape(4, 4, BRANCH_W, d // 4).transpose(1, 2, 0, 3).reshape(4, BRANCH_W, d),
                "wo": g_wo.reshape(d, d)}

    g_lcw, g_gcw = gather_shards([own_slab(lru_conv_w, F32), own_slab(gdn_conv_w, F32)], name="gather_conv_weights")

    def small_params(l):
        return {"lcw": g_lcw[:, l].transpose(1, 0, 2).reshape(4, BRANCH_W),
                "gcw": g_gcw[:, l].transpose(1, 0, 2).reshape(4, 3 * BRANCH_W),
                "lcb": lru_conv_b[l][None],
                "bt": jnp.zeros((128, 128), F32).at[:, :4].set(sgu_b[l].T),
                "wa": _block_diag(lru_wa[l]), "wx": _block_diag(lru_wx[l]),
                "alog": _lane_row(gdn_a_log[l]), "dt": _lane_row(gdn_dt_bias[l])}

    flying = {0: gather_start(layer_bufs(0), name="gather_weights_0_start")[:2]}
    handed = {}

    def hand_on(l, grp, after):
        lo, hi = groups[grp]
        bufs_l, sems_l = flying[l]
        got, sm, tk = gather_pass(bufs_l[lo:hi], group_sems(sems_l, lo, hi), after,
                                  name=f"gather_weights_{l}_{grp}_pass")
        handed[l, grp] = (got, sm)
        return tk

    def arrived(l, grp, after):
        return gather_wait(*handed.pop((l, grp)), after, name=f"gather_weights_{l}_{grp}_wait")

    saved = []
    xs = x0
    tok = hand_on(0, "ffn1", x0)
    for l in range(n_layer):
        p = small_params(l)
        sv = {"p": p, "x0": xs}
        p["ff1"] = ffn_weights(*arrived(l, "ffn1", tok))
        x1 = ffn_fwd(xs, ff1_norm[l][None], *p["ff1"], tb=512, name=f"ffn1_fwd_{l}")
        tok = hand_on(l, "mix", x1)
        if l == 0:
            flying[1] = gather_start(layer_bufs(1, dep=tok), name="gather_weights_1_start")[:2]
        h = norm_fwd(x1, mix_norm[l][None], tb=512, name=f"mix_norm_fwd_{l}")
        p.update(mix_weights(*arrived(l, "mix", h)))
        proj = matmul(h, p["w_in"], tm=1024, tn=1536, tk=1024, name=f"proj_{l}")
        ya = sgu_fwd(proj, sgu_ln_g[l][None], sgu_ln_b[l][None], sgu_w[l], p["bt"], tb=512, name=f"sgu_fwd_{l}")
        xc = conv_fwd(proj, C_BX, BRANCH_W, p["lcw"], p["lcb"], tb=512, name=f"lru_conv_fwd_{l}")
        yb, hl = lru_fwd(xc, proj, p["wa"], p["wx"], lru_ba[l][None], lru_bx[l][None], lru_lambda[l][None],
                         tb=256, name=f"lru_fwd_{l}")
        hand_on(l, "ffn2", yb)
        qkv = conv_fwd(proj, C_QKV, 3 * BRANCH_W, p["gcw"], jnp.zeros((1, 3 * BRANCH_W), F32), tb=512,
                       name=f"gdn_conv_fwd_{l}")
        yc, sall = gdn_fwd(qkv, proj, p["alog"], p["dt"], gdn_norm_g[l][None], cps=GDN_CPS, name=f"gdn_fwd_{l}")
        p["ff2"] = ffn_weights(*arrived(l, "ffn2", yc))
        dep2 = None
        if l + 1 < n_layer:
            dep2 = tok = hand_on(l + 1, "ffn1", yc)
        yd = pool_fwd(proj, pool_w[l], pool_scale[l][None], tb=512, name=f"pool_fwd_{l}")
        ys = (ya, yb, yc, yd)
        x2, merged = merge_fwd(x1, ys, proj, p["wb"], p["wo"], tb=256, dep=dep2, name=f"merge_fwd_{l}")
        x3 = ffn_fwd(x2, ff2_norm[l][None], *p["ff2"], tb=512, name=f"ffn2_fwd_{l}")
        tok = x3
        sv.update(x1=x1, h=h, proj=proj, xc=xc, hl=hl, qkv=qkv, sall=sall, ys=ys, merged=merged, x2=x2)
        saved.append(sv)
        xs = x3

    dx, d_final, loss_blk = loss_head(xs, final_norm[None], loss_target[0], tb=512, name="loss_head")
    loss = lax.psum(loss_blk[0, 0], ("x", "y", "c"))

    small_g = {n: [None] * n_layer for n in _SMALL if n != "final_norm"}
    big_g = [None] * n_layer
    per_layer = 7
    flight = None

    def pair_sums(bigs, recv, tag, w_in_at):
        part = [sum_halves(t, r, c_idx, name=f"grad_pair_sum_{tag}_{i}") for i, (t, r) in enumerate(zip(bigs, recv))]
        if w_in_at is not None:
            part[w_in_at] = _w_in_grad_to_shards(part[w_in_at])
        return part

    def pair_start(arrs, tag):
        return copies_start(arrs, [jax.ShapeDtypeStruct((N_SHARD,) + t.shape[2:], t.dtype) for t in arrs],
                            _pair_plan, N_SHARD, name=f"grad_pair_exchange_{tag}_start")

    def pair_to_chip(fl, after, tag, w_in_at=None):
        sent, recv = copies_wait(*fl[:3], after, _pair_plan, name=f"grad_pair_exchange_{tag}_wait")
        part = pair_sums(sent, recv, tag, w_in_at)
        return copies_start(part, [jax.ShapeDtypeStruct(t.shape, t.dtype) for t in part], _chip_plan, 3,
                            name=f"grad_chip_exchange_{tag}_start")

    def chip_finish(fl, after, tag):
        part, recv = copies_wait(*fl[:3], after, _chip_plan, name=f"grad_chip_exchange_{tag}_wait")
        return [sum_chips(r, q, place, name=f"grad_chip_sum_{tag}_{i}") for i, (r, q) in enumerate(zip(recv, part))]

    early = mid = None

    for l in reversed(range(n_layer)):
        sv = saved[l]
        p = sv["p"]
        proj = sv["proj"]
        dx2, dg_ff2, dab2, hid2, h2, dy2 = ffn_bwd(sv["x2"], dx, ff2_norm[l][None], *p["ff2"], tb=512,
                                                  dep=None if flight is None else flight[3],
                                                  name=f"ffn2_bwd_{l}")
        dwgu2 = matmul(h2, dab2, ta=True, tm=1024, tn=FF_SHARD, tk=TK_DW, name=f"ffn2_dwgu_{l}")
        dwd2 = matmul(hid2, dy2, ta=True, tm=FF_SHARD, tn=1024, tk=TK_DW, name=f"ffn2_dwd_{l}")
        if flight is not None:
            flight = pair_to_chip(flight, dx2, "1", w_in_at=2)
        dgp, dbr, dya, dyb, dyc, dyd, dx2b = merge_bwd(dx2, sv["ys"], proj, p["wb"], p["wo"], tb=256,
                                                       dep=None if flight is None else flight[3],
                                                       name=f"merge_bwd_{l}")
        dwo = matmul(sv["merged"], dx2b, ta=True, tm=1024, tn=1024, tk=TK_DW, name=f"dwo_{l}")
        dwb = jnp.stack([matmul(sv["ys"][g], dbr[g], ta=True, tm=BRANCH_W, tn=1024, tk=TK_DW, name=f"dwb{g}_{l}")
                         for g in range(4)])
        dwb_sh = dwb.reshape(4, BRANCH_W, N_SHARD, d // N_SHARD).transpose(2, 0, 1, 3)
        g_late = [dwgu2, dwd2.reshape(N_SHARD, 2, FF_SHARD // 2, d),
                  dwb_sh.reshape(N_SHARD, 2, 2 * BRANCH_W, d // N_SHARD), dwo.reshape(N_SHARD, 2, d // 8, d)]
        tok = None
        if l == 0:
            early = pair_start(g_late, "0a")
            tok = early[3]
        du, dv, dlg, dlb, dws, dbt = sgu_bwd(proj, dya, sgu_ln_g[l][None], sgu_ln_b[l][None], sgu_w[l], p["bt"],
                                             tb=512, dep=tok, name=f"sgu_bwd_{l}")
        dxc, dgate, dwa, dwx, dba, dbx, dlam = lru_bwd(sv["xc"], proj, sv["hl"], dyb, p["wa"], p["wx"],
                                                       lru_ba[l][None], lru_bx[l][None], lru_lambda[l][None],
                                                       tb=256, name=f"lru_bwd_{l}")
        dbx_in, dcw_l = conv_bwd(proj, C_BX, BRANCH_W, dxc, p["lcw"], tb=512, name=f"lru_conv_bwd_{l}")
        dqkv, dz, dbeta, dal, ddt, dng = gdn_bwd(sv["qkv"], proj, sv["sall"], dyc, p["alog"], p["dt"],
                                                 gdn_norm_g[l][None], cps=GDN_CPS, name=f"gdn_bwd_{l}")
        dqkv_in, dcw_g = conv_bwd(proj, C_QKV, 3 * BRANCH_W, dqkv, p["gcw"], tb=512, name=f"gdn_conv_bwd_{l}")
        ddx, dwp, dsc = pool_bwd(proj, dyd, pool_w[l], pool_scale[l][None], tb=512, name=f"pool_bwd_{l}")
        if l == 0:
            early = pair_to_chip(early, ddx, "0a")
            tok = early[3]
        s = dx.shape[0]
        dproj = jnp.concatenate([dqkv_in, dz, du, dv, dbx_in, dgate, ddx, dbeta,
                                 jnp.zeros((s, C_GATE - C_BA - 128), BF16), dgp], axis=1)
        dh = matmul(dproj, p["w_in"], tb=True, tm=1024, tn=1024, tk=1536, dep=tok, name=f"dh_mix_{l}")
        dw_in = matmul(sv["h"], dproj, ta=True, tm=1024, tn=1152, tk=TK_DW, col_groups=N_SHARD, name=f"dw_in_{l}")
        dw_in = dw_in.reshape(N_SHARD, 2, d // 2, P_PAD // N_SHARD)
        if l == 0:
            mid = pair_start([dw_in], "0b")
            tok = mid[3]
        dx1, dg_mix = norm_bwd(sv["x1"], mix_norm[l][None], dh, dx2, tb=512, dep=tok, name=f"mix_norm_bwd_{l}")
        dx, dg_ff1, dab1, hid1, h1, dy1 = ffn_bwd(sv["x0"], dx1, ff1_norm[l][None], *p["ff1"],
                                                 tb=512, name=f"ffn1_bwd_{l}")
        if l == 0:
            mid = pair_to_chip(mid, dx, "0b", w_in_at=0)
            tok = mid[3]
        dwgu1 = matmul(h1, dab1, ta=True, tm=1024, tn=FF_SHARD, tk=TK_DW, dep=tok, name=f"ffn1_dwgu_{l}")
        dwd1 = matmul(hid1, dy1, ta=True, tm=FF_SHARD, tn=1024, tk=TK_DW, name=f"ffn1_dwd_{l}")

        for n, g in (("ff1_norm", dg_ff1[0]), ("mix_norm", dg_mix[0]), ("sgu_ln_g", dlg[0]), ("sgu_ln_b", dlb[0]),
                     ("sgu_w", dws), ("sgu_b", dbt[:, :4].T), ("lru_conv_w", dcw_l[:4]), ("lru_conv_b", dcw_l[4]),
                     ("lru_wa", _diag_blocks(dwa)), ("lru_ba", dba[0]), ("lru_wx", _diag_blocks(dwx)),
                     ("lru_bx", dbx[0]), ("lru_lambda", dlam[0]), ("gdn_conv_w", dcw_g[:4]),
                     ("gdn_a_log", dal[0, 4:8]), ("gdn_dt_bias", ddt[0, 4:8]), ("gdn_norm_g", dng[0]),
                     ("pool_w", dwp), ("pool_scale", dsc[0]), ("ff2_norm", dg_ff2[0])):
            small_g[n][l] = g
        big_g[l] = [dwgu1, dwd1.reshape(N_SHARD, 2, FF_SHARD // 2, d), dw_in, g_late[2], g_late[3], g_late[0], g_late[1]]
        if l == n_layer - 1:
            flight = pair_start(big_g[l], "1")

    last = big_g[0][:2]
    recv = pair_send_halves(last, name="grad_pair_exchange_0c")
    part = pair_sums(last, recv, "0c", None)
    recv = chip_scatter(part, name="grad_chip_exchange_0c")
    red_c = [sum_chips(r, q, place, name=f"grad_chip_sum_0c_{i}") for i, (r, q) in enumerate(zip(recv, part))]
    red1 = chip_finish(flight, red_c[0], "1")
    red_a = chip_finish(early, red_c[0], "0a")
    red_b = chip_finish(mid, red_c[0], "0b")
    full = pair_gather(red_c + red_b + [red_a[2], red_a[3], red_a[0], red_a[1]] + red1, name="grad_pair_gather")
    grads = {}

    def layers(k, f):
        return jnp.stack([f(full[l * per_layer + k]) for l in range(n_layer)])

    grads["ff1_wg"] = layers(0, lambda t: t[0])
    grads["ff1_wu"] = layers(0, lambda t: t[1])
    grads["ff1_wd"] = layers(1, lambda t: t.reshape(FF_SHARD, d))
    grads["w_in"] = layers(2, lambda t: t.reshape(d, P_IN_SHARD_PAD)[:, :P_IN_SHARD])
    grads["w_branch"] = layers(3, lambda t: t.reshape(4, BRANCH_W, d // N_SHARD))
    grads["w_out"] = layers(4, lambda t: t.reshape(d // N_SHARD, d))
    grads["ff2_wg"] = layers(5, lambda t: t[0])
    grads["ff2_wu"] = layers(5, lambda t: t[1])
    grads["ff2_wd"] = layers(6, lambda t: t.reshape(FF_SHARD, d))

    small_full = [jnp.stack(small_g[n]) for n in _SMALL if n != "final_norm"] + [d_final[0]]
    summed = _unpack(all_sum_small(_pack(small_full), name="grad_small_allreduce"), [t.shape for t in small_full])
    for n, g in zip(_SMALL, summed):
        if n in _CONV:
            width = a[n].shape[-1]
            g = lax.dynamic_slice_in_dim(g, chip * width, width, axis=2)
        grads[n] = g

    delta, new_m, new_v = {}, {}, {}
    for n in _BIG:
        delta[n], new_m[n], new_v[n] = adamw(a[n], grads[n], a["m_" + n], a["v_" + n], name=f"adamw_{n}")
    shapes = [a[n].shape for n in _SMALL]
    packed = [_pack([src[pre + n] for n in _SMALL]) for src, pre in ((a, ""), (grads, ""), (a, "m_"), (a, "v_"))]
    for store, buf in zip((delta, new_m, new_v), adamw(*packed, name="adamw_small")):
        store.update(zip(_SMALL, _unpack(buf, shapes)))

    return (loss, dx[None], *[grads[n] for n in _WEIGHTS], *[delta[n] for n in _WEIGHTS],
            *[new_m[n] for n in _WEIGHTS], *[new_v[n] for n in _WEIGHTS])
```

```python
import functools
import math

import jax
import jax.numpy as jnp
from jax import lax
from jax.experimental import pallas as pl
from jax.experimental.pallas import tpu as pltpu

F32 = jnp.float32
BF16 = jnp.bfloat16
MESH = pl.DeviceIdType.MESH

D_MODEL = 1024
N_SHARD = 4
D_FF = 2816
FF_SHARD = D_FF // N_SHARD
BRANCH_W = 512
CHUNK = 64
GDN_HEADS = 4
GDN_DK = 128
GDN_CPS = 4
TK_DW = 2048
LRU_C = 8.0
EPS = 1e-6
P_IN = 8712
P_IN_SHARD = P_IN // N_SHARD
P_IN_SHARD_PAD = 2304
P_PAD = 9216
C_QKV, C_Z, C_AU, C_AV, C_BX, C_BG, C_DX, C_BA, C_GATE = 0, 1536, 2048, 2560, 3072, 3584, 4096, 4608, 5120

ADAM_LR, ADAM_B1, ADAM_B2, ADAM_EPS, ADAM_WD, ADAM_STEP = 0.001, 0.9, 0.999, 1e-08, 0.01, 10


def _gelu(x):
    return 0.5 * x * (1.0 + jnp.tanh(0.7978845608028654 * (x + 0.044715 * (x * x * x))))


def _gelu_grad(x):
    u = 0.7978845608028654 * (x + 0.044715 * (x * x * x))
    t = jnp.tanh(u)
    return 0.5 * (1.0 + t) + 0.5 * x * (1.0 - t * t) * 0.7978845608028654 * (1.0 + 3.0 * 0.044715 * x * x)


def _silu(x):
    return x * jax.nn.sigmoid(x)


def _softplus(x):
    return jnp.maximum(x, 0.0) + jnp.log1p(jnp.exp(-jnp.abs(x)))


def _dg(a, b, ca, cb):
    return lax.dot_general(a.astype(BF16), b.astype(BF16), (((ca,), (cb,)), ((), ())),
                           preferred_element_type=F32)


def _dg3(a, b, ca, cb):
    a1 = a.astype(BF16)
    a2 = (a - a1.astype(F32)).astype(BF16)
    b1 = b.astype(BF16)
    b2 = (b - b1.astype(F32)).astype(BF16)
    dn = (((ca,), (cb,)), ((), ()))
    d = functools.partial(lax.dot_general, dimension_numbers=dn, preferred_element_type=F32)
    return d(a1, b1) + (d(a1, b2) + d(a2, b1))


def _make_mm(ca, cb, dot):
    @jax.custom_vjp
    def f(a, b):
        return dot(a, b, ca, cb)

    def fwd(a, b):
        return dot(a, b, ca, cb), (a, b)

    def bwd(res, g):
        a, b = res
        if (ca, cb) == (1, 0):
            return dot(g, b, 1, 1), dot(a, g, 0, 0)
        if (ca, cb) == (1, 1):
            return dot(g, b, 1, 0), dot(g, a, 0, 0)
        return dot(b, g, 1, 1), dot(a, g, 1, 0)

    f.defvjp(fwd, bwd)
    return f


mm_nn = _make_mm(1, 0, _dg)


def _bdg(a, b, ca, cb):
    return lax.dot_general(a.astype(BF16), b.astype(BF16), (((ca,), (cb,)), ((0,), (0,))),
                           preferred_element_type=F32)


def _split2(x):
    x1 = x.astype(BF16)
    return x1, (x - x1.astype(F32)).astype(BF16)


def _bdg3(a, b, ca, cb):
    a1, a2 = _split2(a)
    b1, b2 = _split2(b)
    d = functools.partial(lax.dot_general, dimension_numbers=(((ca,), (cb,)), ((0,), (0,))),
                          preferred_element_type=F32)
    return d(a1, b1) + (d(a1, b2) + d(a2, b1))


def _make_bmm(ca, cb):
    @jax.custom_vjp
    def f(a, b):
        return _bdg(a, b, ca, cb)

    def fwd(a, b):
        return _bdg(a, b, ca, cb), (a, b)

    def bwd(res, g):
        a, b = res
        if (ca, cb) == (2, 1):
            return _bdg(g, b, 2, 2), _bdg(a, g, 1, 1)
        if (ca, cb) == (2, 2):
            return _bdg(g, b, 2, 1), _bdg(g, a, 1, 1)
        return _bdg(b, g, 2, 2), _bdg(a, g, 2, 1)

    f.defvjp(fwd, bwd)
    return f


bmm_nn = _make_bmm(2, 1)
bmm_nt = _make_bmm(2, 2)
bmm_tn = _make_bmm(1, 1)


def _iota3(shape, dim):
    return lax.broadcasted_iota(jnp.int32, shape, dim)


@jax.custom_vjp
def _tri_inv(aw):
    b, n, _ = aw.shape
    shp = (b, n, 2 * n)
    left = _iota3(shp, 2) < n
    x = jnp.where(_iota3(shp, 1) == _iota3(shp, 2), 1.0, 0.0) - aw
    zero = jnp.zeros((b, n, 4 * n), BF16)
    k = 1
    while k < n:
        x1, x2 = _split2(x)
        lhs = jnp.concatenate([x1, x2], axis=1)
        rhs = jnp.concatenate([zero, jnp.concatenate([x1, x2], axis=2)], axis=1)
        r = lax.dot_general(lhs, rhs, (((2,), (1,)), ((0,), (0,))), preferred_element_type=F32)
        mx = (r[:, :n, :2 * n] + r[:, :n, 2 * n:]) + (r[:, n:, :2 * n] + r[:, n:, 2 * n:])
        x = jnp.where(left, x, 0.0) + mx
        k *= 2
    return jnp.where(left, x, 0.0)


def _tri_inv_fwd(aw):
    t = _tri_inv(aw)
    return t, t


def _tri_inv_bwd(t, dt):
    n = t.shape[1]
    x = _bdg3(t, dt, 1, 1)[:, :n]
    tp = jnp.concatenate([jnp.zeros_like(t), t], axis=1)
    return (-_bdg3(x, tp, 2, 2),)


_tri_inv.defvjp(_tri_inv_fwd, _tri_inv_bwd)


def _rows(shape):
    return lax.broadcasted_iota(jnp.int32, shape, 0)


def _lanes(shape):
    return lax.broadcasted_iota(jnp.int32, shape, 1)


def _col(x, j):
    return jnp.sum(jnp.where(_lanes(x.shape) == j, x, 0.0), axis=1, keepdims=True)


def _put_col(col, j, width):
    shape = (col.shape[0], width)
    return jnp.where(_lanes(shape) == j, jnp.broadcast_to(col, shape), 0.0)


VMEM_BYTES_V7X = 64 << 20


def _params(*sem, vmem_bytes=None):
    return pltpu.CompilerParams(dimension_semantics=sem, vmem_limit_bytes=vmem_bytes)


def matmul(a, b, *, ta=False, tb=False, tm, tn, tk, out_dtype=F32, col_groups=1, dep=None, name):
    ba, bb = a.shape[:-2], b.shape[:-2]
    batch = ba if len(ba) >= len(bb) else bb
    assert ba in ((), batch) and bb in ((), batch)
    nb = len(batch)
    (m, k) = (a.shape[-1], a.shape[-2]) if ta else (a.shape[-2], a.shape[-1])
    (k2, n) = (b.shape[-1], b.shape[-2]) if tb else (b.shape[-2], b.shape[-1])
    tm, tn, tk = min(tm, m), min(tn, n), min(tk, k)
    assert k == k2 and m % tm == 0 and n % tn == 0 and k % tk == 0, (a.shape, b.shape, tm, tn, tk)
    nk = k // tk
    grid = batch + (m // tm, n // tn, nk)

    def a_map(*g):
        i, kk = g[nb], g[nb + 2]
        return (g[:nb] if ba else ()) + ((kk, i) if ta else (i, kk))

    def b_map(*g):
        j, kk = g[nb + 1], g[nb + 2]
        return (g[:nb] if bb else ()) + ((j, kk) if tb else (kk, j))

    per_group = n // tn // col_groups
    assert col_groups == 1 or (nb == 0 and per_group * col_groups * tn == n)

    def o_map(*g):
        if col_groups > 1:
            return (g[1] // per_group, g[0], g[1] % per_group)
        return g[:nb] + (g[nb], g[nb + 1])

    a_blk = (None,) * len(ba) + ((tk, tm) if ta else (tm, tk))
    b_blk = (None,) * len(bb) + ((tn, tk) if tb else (tk, tn))
    o_blk = (None,) * (nb + (col_groups > 1)) + (tm, tn)
    o_shape = (col_groups, m, n // col_groups) if col_groups > 1 else batch + (m, n)

    dep_specs, dep_args = _dep_operand(dep)

    def body(a_ref, b_ref, *rest):
        o_ref, acc_ref = rest[len(dep_args):]
        kk = pl.program_id(nb + 2)

        @pl.when(kk == 0)
        def _():
            acc_ref[...] = jnp.zeros_like(acc_ref)

        acc_ref[...] += _dg(a_ref[...], b_ref[...], 0 if ta else 1, 1 if tb else 0)

        @pl.when(kk == nk - 1)
        def _():
            o_ref[...] = acc_ref[...].astype(o_ref.dtype)

    return pl.pallas_call(
        body, name=name, grid=grid,
        in_specs=[pl.BlockSpec(a_blk, a_map), pl.BlockSpec(b_blk, b_map)] + dep_specs,
        out_specs=pl.BlockSpec(o_blk, o_map),
        out_shape=jax.ShapeDtypeStruct(o_shape, out_dtype),
        scratch_shapes=[pltpu.VMEM((tm, tn), F32)],
        compiler_params=_params(*(("parallel",) * (nb + 2) + ("arbitrary",))),
    )(a, b, *dep_args)


def norm_fwd(x, g, *, tb, name):
    s, d = x.shape

    def body(x_ref, g_ref, h_ref):
        xv = x_ref[...]
        r = lax.rsqrt(jnp.mean(xv * xv, axis=-1, keepdims=True) + EPS)
        h_ref[...] = (xv * r * g_ref[...]).astype(BF16)

    return pl.pallas_call(
        body, name=name, grid=(s // tb,),
        in_specs=[pl.BlockSpec((tb, d), lambda i: (i, 0)), pl.BlockSpec((1, d), lambda i: (0, 0))],
        out_specs=pl.BlockSpec((tb, d), lambda i: (i, 0)),
        out_shape=jax.ShapeDtypeStruct((s, d), BF16),
        compiler_params=_params("parallel"),
    )(x, g)


def _rms_bwd(xv, gv, dh):
    r = lax.rsqrt(jnp.mean(xv * xv, axis=-1, keepdims=True) + EPS)
    xh = xv * r
    dxh = dh * gv
    dx = r * (dxh - xh * jnp.mean(dxh * xh, axis=-1, keepdims=True))
    return dx, jnp.sum(dh * xh, axis=0, keepdims=True)


def norm_bwd(x, g, dh, dres, *, tb, dep=None, name):
    s, d = x.shape
    dep_specs, dep_args = _dep_operand(dep)

    def body(x_ref, g_ref, dh_ref, dres_ref, *rest):
        dx_ref, dg_ref = rest[len(dep_args):]
        dx, dg = _rms_bwd(x_ref[...], g_ref[...], dh_ref[...])
        dx_ref[...] = dres_ref[...] + dx

        @pl.when(pl.program_id(0) == 0)
        def _():
            dg_ref[...] = jnp.zeros_like(dg_ref)

        dg_ref[...] += dg

    row = pl.BlockSpec((tb, d), lambda i: (i, 0))
    one = pl.BlockSpec((1, d), lambda i: (0, 0))
    return pl.pallas_call(
        body, name=name, grid=(s // tb,),
        in_specs=[row, one, row, row] + dep_specs, out_specs=[row, one],
        out_shape=[jax.ShapeDtypeStruct((s, d), F32), jax.ShapeDtypeStruct((1, d), F32)],
        compiler_params=_params("arbitrary"),
    )(x, g, dh, dres, *dep_args)


def _dep_operand(dep):
    return ([], []) if dep is None else ([_ANY], [dep])


def ffn_fwd(x, g, wg, wu, wd, layer, *, tb, dep=None, name):
    s, d = x.shape
    fc = wd.shape[-2]
    dep_specs, dep_args = _dep_operand(dep)

    def body(x_ref, g_ref, wg_ref, wu_ref, wd_ref, *rest):
        o_ref, h_scr, acc_scr = rest[len(dep_args):]
        j = pl.program_id(1)

        @pl.when(j == 0)
        def _():
            xv = x_ref[...]
            r = lax.rsqrt(jnp.mean(xv * xv, axis=-1, keepdims=True) + EPS)
            h_scr[...] = (xv * r * g_ref[...]).astype(BF16)
            acc_scr[...] = jnp.zeros_like(acc_scr)

        h = h_scr[...]
        a = _dg(h, wg_ref[...], 1, 0)
        b = _dg(h, wu_ref[...], 1, 0)
        acc_scr[...] += _dg(_silu(a) * b, wd_ref[...], 1, 0)

        @pl.when(j == N_SHARD - 1)
        def _():
            o_ref[...] = x_ref[...] + 0.5 * acc_scr[...]

    row = pl.BlockSpec((tb, d), lambda i, j: (i, 0))
    lg, lu, ld = layer
    return pl.pallas_call(
        body, name=name, grid=(s // tb, N_SHARD),
        in_specs=[row, pl.BlockSpec((1, d), lambda i, j: (0, 0)),
                  pl.BlockSpec((None, None, d, fc), lambda i, j: (j, lg, 0, 0)),
                  pl.BlockSpec((None, None, d, fc), lambda i, j: (j, lu, 0, 0)),
                  pl.BlockSpec((None, None, fc, d), lambda i, j: (j, ld, 0, 0))] + dep_specs,
        out_specs=row,
        out_shape=jax.ShapeDtypeStruct((s, d), F32),
        scratch_shapes=[pltpu.VMEM((tb, d), BF16), pltpu.VMEM((tb, d), F32)],
        compiler_params=_params("parallel", "arbitrary"),
    )(x, g, wg, wu, wd, *dep_args)


def ffn_bwd(x, dxo, g, wg, wu, wd, layer, *, tb, dep=None, name):
    s, d = x.shape
    fc = wd.shape[-2]
    dep_specs, dep_args = _dep_operand(dep)

    def body(x_ref, dxo_ref, g_ref, wg_ref, wu_ref, wd_ref, *rest):
        dx_ref, dg_ref, dab_ref, hid_ref, h_ref, dy_ref, dh_scr = rest[len(dep_args):]
        i, j = pl.program_id(0), pl.program_id(1)

        @pl.when(j == 0)
        def _():
            xv = x_ref[...]
            r = lax.rsqrt(jnp.mean(xv * xv, axis=-1, keepdims=True) + EPS)
            h_ref[...] = (xv * r * g_ref[...]).astype(BF16)
            dy_ref[...] = (0.5 * dxo_ref[...]).astype(BF16)
            dh_scr[...] = jnp.zeros_like(dh_scr)

        h = h_ref[...]
        a = _dg(h, wg_ref[...], 1, 0)
        b = _dg(h, wu_ref[...], 1, 0)
        sg = jax.nn.sigmoid(a)
        sl = a * sg
        dhid = _dg(dy_ref[...], wd_ref[...], 1, 1)
        da = (dhid * b * (sg * (1.0 + a * (1.0 - sg)))).astype(BF16)
        db = (dhid * sl).astype(BF16)
        dab_ref[0] = da
        dab_ref[1] = db
        hid_ref[...] = (sl * b).astype(BF16)
        dh_scr[...] += _dg(da, wg_ref[...], 1, 1) + _dg(db, wu_ref[...], 1, 1)

        @pl.when(jnp.logical_and(i == 0, j == 0))
        def _():
            dg_ref[...] = jnp.zeros_like(dg_ref)

        @pl.when(j == N_SHARD - 1)
        def _():
            dx, dg = _rms_bwd(x_ref[...], g_ref[...], dh_scr[...])
            dx_ref[...] = dxo_ref[...] + dx
            dg_ref[...] += dg

    row = pl.BlockSpec((tb, d), lambda i, j: (i, 0))
    one = pl.BlockSpec((1, d), lambda i, j: (0, 0))
    lg, lu, ld = layer
    return pl.pallas_call(
        body, name=name, grid=(s // tb, N_SHARD),
        in_specs=[row, row, one,
                  pl.BlockSpec((None, None, d, fc), lambda i, j: (j, lg, 0, 0)),
                  pl.BlockSpec((None, None, d, fc), lambda i, j: (j, lu, 0, 0)),
                  pl.BlockSpec((None, None, fc, d), lambda i, j: (j, ld, 0, 0))] + dep_specs,
        out_specs=[row, one,
                   pl.BlockSpec((None, 2, tb, fc), lambda i, j: (j, 0, i, 0)),
                   pl.BlockSpec((None, tb, fc), lambda i, j: (j, i, 0)),
                   row, row],
        out_shape=[jax.ShapeDtypeStruct((s, d), F32), jax.ShapeDtypeStruct((1, d), F32),
                   jax.ShapeDtypeStruct((N_SHARD, 2, s, fc), BF16),
                   jax.ShapeDtypeStruct((N_SHARD, s, fc), BF16),
                   jax.ShapeDtypeStruct((s, d), BF16), jax.ShapeDtypeStruct((s, d), BF16)],
        scratch_shapes=[pltpu.VMEM((tb, d), F32)],
        compiler_params=_params("arbitrary", "arbitrary",
                                vmem_bytes=min(VMEM_BYTES_V7X - (8 << 20),
                                               12 * d * fc + 44 * tb * d + 40 * tb * fc + (4 << 20))),
    )(x, dxo, g, wg, wu, wd, *dep_args)


def _pblk(tb, width, col0, rev_n=None):
    assert col0 % width == 0
    cb = col0 // width
    if rev_n is None:
        return pl.BlockSpec((tb, width), lambda i: (i, cb))
    return pl.BlockSpec((tb, width), lambda i: (rev_n - 1 - i, cb))


def _full(shape):
    return pl.BlockSpec(shape, lambda i: (0,) * len(shape))


def conv_fwd(src, col0, width, w, bias, *, tb, name):
    s = src.shape[0]
    cb = col0 // width

    def body(x_ref, p_ref, w_ref, b_ref, y_ref):
        i = pl.program_id(0)
        prev = jnp.where(i > 0, p_ref[...], 0.0)
        xe = jnp.concatenate([prev, x_ref[...]], axis=0)
        wv = w_ref[...]
        acc = b_ref[...] + wv[3:4] * xe[8:]
        for k in range(3):
            acc = acc + wv[k:k + 1] * pltpu.roll(xe, 3 - k, 0)[8:]
        y_ref[...] = acc

    return pl.pallas_call(
        body, name=name, grid=(s // tb,),
        in_specs=[_pblk(tb, width, col0),
                  pl.BlockSpec((8, width), lambda i: (jnp.maximum(i * (tb // 8) - 1, 0), cb)),
                  _full((4, width)), _full((1, width))],
        out_specs=pl.BlockSpec((tb, width), lambda i: (i, 0)),
        out_shape=jax.ShapeDtypeStruct((s, width), F32),
        compiler_params=_params("parallel"),
    )(src, src, w, bias)


def conv_bwd(src, col0, width, dpre, w, *, tb, name):
    s = src.shape[0]
    cb = col0 // width
    nt = s // tb

    def body(x_ref, p_ref, d_ref, n_ref, w_ref, dx_ref, dwb_ref):
        i = pl.program_id(0)
        d = d_ref[...]
        nxt = jnp.where(i < nt - 1, n_ref[...], 0.0)
        de = jnp.concatenate([d, nxt], axis=0)
        wv = w_ref[...]
        dx = wv[3:4] * d
        for k in range(3):
            dx = dx + wv[k:k + 1] * pltpu.roll(de, tb + 8 - (3 - k), 0)[:tb]
        dx_ref[...] = dx.astype(BF16)

        @pl.when(i == 0)
        def _():
            dwb_ref[...] = jnp.zeros_like(dwb_ref)

        prev = jnp.where(i > 0, p_ref[...], 0.0)
        xe = jnp.concatenate([prev, x_ref[...]], axis=0)
        dwb_ref[3:4, :] += jnp.sum(d * xe[8:], axis=0, keepdims=True)
        for k in range(3):
            dwb_ref[k:k + 1, :] += jnp.sum(d * pltpu.roll(xe, 3 - k, 0)[8:], axis=0, keepdims=True)
        dwb_ref[4:5, :] += jnp.sum(d, axis=0, keepdims=True)

    return pl.pallas_call(
        body, name=name, grid=(nt,),
        in_specs=[_pblk(tb, width, col0),
                  pl.BlockSpec((8, width), lambda i: (jnp.maximum(i * (tb // 8) - 1, 0), cb)),
                  pl.BlockSpec((tb, width), lambda i: (i, 0)),
                  pl.BlockSpec((8, width), lambda i: (jnp.minimum((i + 1) * (tb // 8), s // 8 - 1), 0)),
                  _full((4, width))],
        out_specs=[pl.BlockSpec((tb, width), lambda i: (i, 0)), _full((8, width))],
        out_shape=[jax.ShapeDtypeStruct((s, width), BF16), jax.ShapeDtypeStruct((8, width), F32)],
        compiler_params=_params("arbitrary"),
    )(src, src, dpre, dpre, w)


SGU_BLOCK = 128


def _sgu_pre(u, v, lg, lb):
    ug = _gelu(u)
    vg = _gelu(v)
    mu = jnp.mean(vg, axis=-1, keepdims=True)
    var = jnp.mean(jnp.square(vg - mu), axis=-1, keepdims=True)
    return ug, (vg - mu) * lax.rsqrt(var + EPS) * lg + lb


def _sgu_mask():
    shp = (SGU_BLOCK, SGU_BLOCK)
    return (_rows(shp) // CHUNK) >= (_lanes(shp) // CHUNK)


def sgu_fwd(proj, lg, lb, ws, bt, *, tb, name):
    s = proj.shape[0]
    w = BRANCH_W

    def body(u_ref, v_ref, lg_ref, lb_ref, w_ref, b_ref, y_ref):
        ug, vn = _sgu_pre(u_ref[...], v_ref[...], lg_ref[...], lb_ref[...])
        mask = _sgu_mask()
        for g in range(4):
            wm = jnp.where(mask, w_ref[g], 0.0)
            bcol = _col(b_ref[...], g)
            cs = slice(128 * g, 128 * g + 128)
            for n in range(tb // SGU_BLOCK):
                rs = slice(SGU_BLOCK * n, SGU_BLOCK * (n + 1))
                mixed = _dg(wm, vn[rs, cs], 1, 0) + bcol
                y_ref[rs, cs] = (ug[rs, cs] * mixed).astype(BF16)

    return pl.pallas_call(
        body, name=name, grid=(s // tb,),
        in_specs=[_pblk(tb, w, C_AU), _pblk(tb, w, C_AV), _full((1, w)), _full((1, w)),
                  _full((4, 128, 128)), _full((128, 128))],
        out_specs=pl.BlockSpec((tb, w), lambda i: (i, 0)),
        out_shape=jax.ShapeDtypeStruct((s, w), BF16),
        compiler_params=_params("parallel"),
    )(proj, proj, lg, lb, ws, bt)


def sgu_bwd(proj, dy, lg, lb, ws, bt, *, tb, dep=None, name):
    s = proj.shape[0]
    w = BRANCH_W
    dep_specs, dep_args = _dep_operand(dep)

    def body(u_ref, v_ref, dy_ref, lg_ref, lb_ref, w_ref, b_ref, *rest):
        du_ref, dv_ref, dlg_ref, dlb_ref, dw_ref, db_ref, dug_scr, dvn_scr = rest[len(dep_args):]

        @pl.when(pl.program_id(0) == 0)
        def _():
            dlg_ref[...] = jnp.zeros_like(dlg_ref)
            dlb_ref[...] = jnp.zeros_like(dlb_ref)
            dw_ref[...] = jnp.zeros_like(dw_ref)
            db_ref[...] = jnp.zeros_like(db_ref)

        (ug, vn), vf = jax.vjp(_sgu_pre, u_ref[...], v_ref[...], lg_ref[...], lb_ref[...])
        dyv = dy_ref[...]
        mask = _sgu_mask()
        for g in range(4):
            wm = jnp.where(mask, w_ref[g], 0.0)
            bcol = _col(b_ref[...], g)
            cs = slice(128 * g, 128 * g + 128)
            dwg = jnp.zeros((SGU_BLOCK, SGU_BLOCK), F32)
            dbc = jnp.zeros((SGU_BLOCK, 1), F32)
            for n in range(tb // SGU_BLOCK):
                rs = slice(SGU_BLOCK * n, SGU_BLOCK * (n + 1))
                blk = vn[rs, cs]
                mixed = _dg(wm, blk, 1, 0) + bcol
                dmix = dyv[rs, cs] * ug[rs, cs]
                dug_scr[rs, cs] = dyv[rs, cs] * mixed
                dvn_scr[rs, cs] = _dg(wm, dmix, 0, 0)
                dwg = dwg + _dg(dmix, blk, 1, 1)
                dbc = dbc + jnp.sum(dmix, axis=1, keepdims=True)
            dw_ref[g] += jnp.where(mask, dwg, 0.0)
            db_ref[...] += _put_col(dbc, g, 128)
        du, dv, dlg, dlb = vf((dug_scr[...], dvn_scr[...]))
        du_ref[...] = du.astype(BF16)
        dv_ref[...] = dv.astype(BF16)
        dlg_ref[...] += dlg
        dlb_ref[...] += dlb

    row = pl.BlockSpec((tb, w), lambda i: (i, 0))
    return pl.pallas_call(
        body, name=name, grid=(s // tb,),
        in_specs=[_pblk(tb, w, C_AU), _pblk(tb, w, C_AV), row, _full((1, w)), _full((1, w)),
                  _full((4, 128, 128)), _full((128, 128))] + dep_specs,
        out_specs=[row, row, _full((1, w)), _full((1, w)), _full((4, 128, 128)), _full((128, 128))],
        out_shape=[jax.ShapeDtypeStruct((s, w), BF16), jax.ShapeDtypeStruct((s, w), BF16),
                   jax.ShapeDtypeStruct((1, w), F32), jax.ShapeDtypeStruct((1, w), F32),
                   jax.ShapeDtypeStruct((4, 128, 128), F32), jax.ShapeDtypeStruct((128, 128), F32)],
        scratch_shapes=[pltpu.VMEM((tb, w), F32), pltpu.VMEM((tb, w), F32)],
        compiler_params=_params("arbitrary"),
    )(proj, proj, dy, lg, lb, ws, bt, *dep_args)


POOL_HALO = 16


def _pooled(xe, t0, tb):
    s2 = xe + pltpu.roll(xe, 1, 0)
    s4 = s2 + pltpu.roll(s2, 2, 0)
    s8 = s4 + pltpu.roll(s4, 4, 0)
    s16 = s8 + pltpu.roll(s8, 8, 0)
    t = (t0 + _rows((tb, 128))).astype(F32)
    outs = []
    for g, sw in enumerate((s2, s4, s8, s16)):
        cs = slice(128 * g, 128 * g + 128)
        cnt = jnp.minimum(t + 1.0, float(2 ** (g + 1)))
        outs.append(sw[POOL_HALO:, cs] / cnt - xe[POOL_HALO:, cs])
    return outs


def pool_fwd(proj, wp, scale, *, tb, name):
    s = proj.shape[0]
    w = BRANCH_W

    def body(x_ref, p_ref, w_ref, sc_ref, y_ref):
        i = pl.program_id(0)
        prev = jnp.where(i > 0, p_ref[...], 0.0)
        xe = jnp.concatenate([prev, x_ref[...]], axis=0)
        pooled = _pooled(xe, i * tb, tb)
        sc = sc_ref[...]
        for g in range(4):
            cs = slice(128 * g, 128 * g + 128)
            y_ref[:, cs] = (_dg(pooled[g], w_ref[g], 1, 0) * sc[:, cs]).astype(BF16)

    cb = C_DX // w
    return pl.pallas_call(
        body, name=name, grid=(s // tb,),
        in_specs=[_pblk(tb, w, C_DX),
                  pl.BlockSpec((POOL_HALO, w), lambda i: (jnp.maximum(i * (tb // POOL_HALO) - 1, 0), cb)),
                  _full((4, 128, 128)), _full((1, w))],
        out_specs=pl.BlockSpec((tb, w), lambda i: (i, 0)),
        out_shape=jax.ShapeDtypeStruct((s, w), BF16),
        compiler_params=_params("parallel"),
    )(proj, proj, wp, scale)


def pool_bwd(proj, dy, wp, scale, *, tb, name):
    s = proj.shape[0]
    w = BRANCH_W
    nt = s // tb
    te = tb + POOL_HALO

    def body(x_ref, p_ref, dy_ref, n_ref, w_ref, sc_ref, dx_ref, dw_ref, dsc_ref):
        i = pl.program_id(0)

        @pl.when(i == 0)
        def _():
            dw_ref[...] = jnp.zeros_like(dw_ref)
            dsc_ref[...] = jnp.zeros_like(dsc_ref)

        prev = jnp.where(i > 0, p_ref[...], 0.0)
        xe = jnp.concatenate([prev, x_ref[...]], axis=0)
        pooled = _pooled(xe, i * tb, tb)
        dyv = dy_ref[...]
        nxt = jnp.where(i < nt - 1, n_ref[...], 0.0)
        dye = jnp.concatenate([dyv, nxt], axis=0)
        sc = sc_ref[...]
        t = (i * tb + _rows((te, 128))).astype(F32)
        for g in range(4):
            cs = slice(128 * g, 128 * g + 128)
            win = 2 ** (g + 1)
            dpm = dye[:, cs] * sc[:, cs]
            dpool = _dg(dpm, w_ref[g], 1, 1)
            q = dpool / jnp.minimum(t + 1.0, float(win))
            r, sh = q, 1
            while sh < win:
                r = r + pltpu.roll(r, te - sh, 0)
                sh *= 2
            dx_ref[:, cs] = (r[:tb] - dpool[:tb]).astype(BF16)
            dw_ref[g] += _dg(pooled[g], dpm[:tb], 0, 0)
            dsc_ref[:, cs] += jnp.sum(dyv[:, cs] * _dg(pooled[g], w_ref[g], 1, 0), axis=0, keepdims=True)

    cb = C_DX // w
    row = pl.BlockSpec((tb, w), lambda i: (i, 0))
    return pl.pallas_call(
        body, name=name, grid=(nt,),
        in_specs=[_pblk(tb, w, C_DX),
                  pl.BlockSpec((POOL_HALO, w), lambda i: (jnp.maximum(i * (tb // POOL_HALO) - 1, 0), cb)),
                  row,
                  pl.BlockSpec((POOL_HALO, w),
                               lambda i: (jnp.minimum((i + 1) * (tb // POOL_HALO), s // POOL_HALO - 1), 0)),
                  _full((4, 128, 128)), _full((1, w))],
        out_specs=[row, _full((4, 128, 128)), _full((1, w))],
        out_shape=[jax.ShapeDtypeStruct((s, w), BF16), jax.ShapeDtypeStruct((4, 128, 128), F32),
                   jax.ShapeDtypeStruct((1, w), F32)],
        compiler_params=_params("arbitrary"),
    )(proj, proj, dy, dy, wp, scale)


def _neg_expm1(z):
    p = 1.0 + z * (1.0 / 9.0)
    for n in (8.0, 7.0, 6.0, 5.0, 4.0, 3.0, 2.0):
        p = 1.0 + z * (1.0 / n) * p
    return jnp.where(z > -0.5, -z * p, 1.0 - jnp.exp(z))


def _lru_gates(xc, wa, wx, ba, bx, lam):
    r = jax.nn.sigmoid(mm_nn(xc, wa) + ba)
    i = jax.nn.sigmoid(mm_nn(xc, wx) + bx)
    log_a = -LRU_C * r * _softplus(-lam)
    a = jnp.exp(log_a)
    return a, jnp.sqrt(_neg_expm1(2.0 * log_a)) * (i * xc)


def _scan_down(a, b):
    n = a.shape[0]
    row = _rows(a.shape)
    sh = 1
    while sh < n:
        keep = row >= sh
        a_sh = jnp.where(keep, pltpu.roll(a, sh, 0), 1.0)
        b_sh = jnp.where(keep, pltpu.roll(b, sh, 0), 0.0)
        b = a * b_sh + b
        a = a * a_sh
        sh *= 2
    return a, b


def _scan_up(a, b):
    n = a.shape[0]
    row = _rows(a.shape)
    sh = 1
    while sh < n:
        keep = row < n - sh
        a_sh = jnp.where(keep, pltpu.roll(a, n - sh, 0), 1.0)
        b_sh = jnp.where(keep, pltpu.roll(b, n - sh, 0), 0.0)
        b = a * b_sh + b
        a = a * a_sh
        sh *= 2
    return a, b


def lru_fwd(xc, proj, wa, wx, ba, bx, lam, *, tb, name):
    s = xc.shape[0]
    w = BRANCH_W

    def body(xc_ref, gate_ref, wa_ref, wx_ref, ba_ref, bx_ref, lam_ref, y_ref, h_ref, carry):
        @pl.when(pl.program_id(0) == 0)
        def _():
            carry[...] = jnp.zeros_like(carry)

        a, b = _lru_gates(xc_ref[...], wa_ref[...], wx_ref[...], ba_ref[...], bx_ref[...], lam_ref[...])
        pa, hb = _scan_down(a, b)
        h = pa * carry[...] + hb
        h_ref[...] = h
        carry[...] = h_ref[tb - 1:tb, :]
        y_ref[...] = (h * _gelu(gate_ref[...])).astype(BF16)

    row = pl.BlockSpec((tb, w), lambda i: (i, 0))
    return pl.pallas_call(
        body, name=name, grid=(s // tb,),
        in_specs=[row, _pblk(tb, w, C_BG), _full((w, w)), _full((w, w)),
                  _full((1, w)), _full((1, w)), _full((1, w))],
        out_specs=[row, row],
        out_shape=[jax.ShapeDtypeStruct((s, w), BF16), jax.ShapeDtypeStruct((s, w), F32)],
        scratch_shapes=[pltpu.VMEM((1, w), F32)],
        compiler_params=_params("arbitrary"),
    )(xc, proj, wa, wx, ba, bx, lam)


def lru_bwd(xc, proj, h, dy, wa, wx, ba, bx, lam, *, tb, name):
    s = xc.shape[0]
    w = BRANCH_W
    nt = s // tb

    def body(xc_ref, gate_ref, h_ref, hp_ref, dy_ref, wa_ref, wx_ref, ba_ref, bx_ref, lam_ref,
             dxc_ref, dgate_ref, dwa_ref, dwx_ref, dba_ref, dbx_ref, dlam_ref, carry):
        i = pl.program_id(0)

        @pl.when(i == 0)
        def _():
            carry[...] = jnp.zeros_like(carry)
            for r in (dwa_ref, dwx_ref, dba_ref, dbx_ref, dlam_ref):
                r[...] = jnp.zeros_like(r)

        (a, b), vf = jax.vjp(_lru_gates, xc_ref[...], wa_ref[...], wx_ref[...],
                             ba_ref[...], bx_ref[...], lam_ref[...])
        gate = gate_ref[...]
        dyv = dy_ref[...]
        hv = h_ref[...]
        row = _rows((tb, w))
        a_next = jnp.where(row < tb - 1, pltpu.roll(a, tb - 1, 0), 1.0)
        pa, xb = _scan_up(a_next, dyv * _gelu(gate))
        dh = xb + pa * carry[...]
        carry[...] = a[0:1, :] * dh[0:1, :]
        h_first = jnp.where(i < nt - 1, hp_ref[7:8, :], 0.0)
        h_prev = jnp.where(row >= 1, pltpu.roll(hv, 1, 0), h_first)
        dxc, dwa, dwx, dba, dbx, dlam = vf((dh * h_prev, dh))
        dxc_ref[...] = dxc
        dgate_ref[...] = (dyv * hv * _gelu_grad(gate)).astype(BF16)
        dwa_ref[...] += dwa
        dwx_ref[...] += dwx
        dba_ref[...] += dba
        dbx_ref[...] += dbx
        dlam_ref[...] += dlam

    row_spec = pl.BlockSpec((tb, w), lambda i: (nt - 1 - i, 0))
    hp_spec = pl.BlockSpec((8, w), lambda i: (jnp.maximum((nt - 1 - i) * (tb // 8) - 1, 0), 0))
    return pl.pallas_call(
        body, name=name, grid=(nt,),
        in_specs=[row_spec, _pblk(tb, w, C_BG, rev_n=nt), row_spec, hp_spec, row_spec,
                  _full((w, w)), _full((w, w)), _full((1, w)), _full((1, w)), _full((1, w))],
        out_specs=[row_spec, row_spec, _full((w, w)), _full((w, w)),
                   _full((1, w)), _full((1, w)), _full((1, w))],
        out_shape=[jax.ShapeDtypeStruct((s, w), F32), jax.ShapeDtypeStruct((s, w), BF16),
                   jax.ShapeDtypeStruct((w, w), F32), jax.ShapeDtypeStruct((w, w), F32),
                   jax.ShapeDtypeStruct((1, w), F32), jax.ShapeDtypeStruct((1, w), F32),
                   jax.ShapeDtypeStruct((1, w), F32)],
        scratch_shapes=[pltpu.VMEM((1, w), F32)],
        compiler_params=_params("arbitrary"),
    )(xc, proj, h, h, dy, wa, wx, ba, bx, lam)


def _gdn_g(ba, alog, dt):
    return -jnp.exp(alog) * _softplus(ba + dt)


def _gdn_local(qp, kp, vp, bpre, gc):
    b, c, dk = qp.shape
    q = _silu(qp)
    k = _silu(kp)
    v = _silu(vp)
    q = q * lax.rsqrt(jnp.sum(q * q, axis=-1, keepdims=True) + EPS) * (dk ** -0.5)
    k = k * lax.rsqrt(jnp.sum(k * k, axis=-1, keepdims=True) + EPS)
    beta = jax.nn.sigmoid(bpre)
    sq = (b, c, c)
    ii, jj = _iota3(sq, 1), _iota3(sq, 2)
    gr = jnp.sum(gc * (ii == jj).astype(F32), axis=1, keepdims=True)
    decay = jnp.exp(jnp.where(ii >= jj, gc - gr, -jnp.inf))
    wd = (b, c, 2 * c)
    wi, wj = _iota3(wd, 1), _iota3(wd, 2) - c
    gr_w = jnp.sum(gc * (wi == wj).astype(F32), axis=1, keepdims=True)
    decay_w = jnp.exp(jnp.where(jnp.logical_and(wj >= 0, wi > wj), gc - gr_w, -jnp.inf))
    kb = k * beta
    pad = jnp.zeros((b, c, dk), F32)
    t = _tri_inv(bmm_nt(kb, jnp.concatenate([pad, k], axis=1)) * decay_w)
    u = bmm_nn(t, jnp.concatenate([v * beta, pad], axis=1))
    wk = bmm_nn(t, jnp.concatenate([kb * jnp.exp(gc), pad], axis=1))
    attn = bmm_nt(q, k) * decay
    g_last = jnp.sum(jnp.where(_iota3((b, c, 1), 1) == c - 1, gc, 0.0), axis=1, keepdims=True)
    return u, wk, attn, q * jnp.exp(gc), k * jnp.exp(g_last - gc), jnp.exp(g_last)


def _gdn_recur(s, u, wk, attn, qg, kd, eg, zz, ng):
    v_new = u - bmm_nn(wk, s)
    o = bmm_nn(qg, s) + bmm_nn(attn, v_new)
    s_out = s * eg + bmm_tn(kd, v_new)
    o = o * lax.rsqrt(jnp.mean(o * o, axis=-1, keepdims=True) + EPS) * ng
    return o * _silu(zz), s_out


def _gdn_inputs(qkv_ref, ba, gcums, cps):
    c, hd = CHUNK, GDN_DK
    qs, ks, vs, bs, gs = [], [], [], [], []
    for ci in range(cps):
        rs = slice(c * ci, c * (ci + 1))
        for h in range(GDN_HEADS):
            qs.append(qkv_ref[rs, hd * h:hd * (h + 1)])
            ks.append(qkv_ref[rs, 512 + hd * h:512 + hd * (h + 1)])
            vs.append(qkv_ref[rs, 1024 + hd * h:1024 + hd * (h + 1)])
            bs.append(_col(ba[rs], h))
            gs.append(_col(gcums[ci], 4 + h))
    return tuple(jnp.stack(t) for t in (qs, ks, vs, bs, gs))


def _chunk_cumsum(g, transpose=False):
    c = CHUNK
    tri = (_rows((c, c)) >= _lanes((c, c))).astype(F32)
    return _dg3(tri, g, 0 if transpose else 1, 0)


def gdn_fwd(qkv, proj, alog, dt, ng, *, cps, name):
    s = qkv.shape[0]
    c = CHUNK
    nc = s // c
    hd = GDN_DK
    tb = c * cps

    def body(qkv_ref, z_ref, ba_ref, al_ref, dt_ref, ng_ref, y_ref, sall_ref, s_scr):
        @pl.when(pl.program_id(0) == 0)
        def _():
            s_scr[...] = jnp.zeros_like(s_scr)

        nh = GDN_HEADS
        ba = ba_ref[...]
        g = _gdn_g(ba, al_ref[...], dt_ref[...])
        gcums = [_chunk_cumsum(g[c * ci:c * (ci + 1)]) for ci in range(cps)]
        loc = _gdn_local(*_gdn_inputs(qkv_ref, ba, gcums, cps))
        for ci in range(cps):
            rs = slice(c * ci, c * (ci + 1))
            sl = slice(nh * ci, nh * (ci + 1))
            s_in = s_scr[...]
            sall_ref[ci] = s_in
            zz = jnp.stack([z_ref[rs, hd * h:hd * (h + 1)] for h in range(nh)])
            y, s_out = _gdn_recur(s_in, *(t[sl] for t in loc), zz, ng_ref[...])
            s_scr[...] = s_out
            for h in range(nh):
                y_ref[rs, hd * h:hd * (h + 1)] = y[h].astype(BF16)

    return pl.pallas_call(
        body, name=name, grid=(nc // cps,),
        in_specs=[pl.BlockSpec((tb, 1536), lambda i: (i, 0)), _pblk(tb, 512, C_Z), _pblk(tb, 128, C_BA),
                  _full((1, 128)), _full((1, 128)), _full((1, 128))],
        out_specs=[pl.BlockSpec((tb, 512), lambda i: (i, 0)),
                   pl.BlockSpec((cps, GDN_HEADS, hd, hd), lambda i: (i, 0, 0, 0))],
        out_shape=[jax.ShapeDtypeStruct((s, 512), BF16), jax.ShapeDtypeStruct((nc, GDN_HEADS, hd, hd), F32)],
        scratch_shapes=[pltpu.VMEM((GDN_HEADS, hd, hd), F32)],
        compiler_params=_params("arbitrary"),
    )(qkv, proj, proj, alog, dt, ng)


def gdn_bwd(qkv, proj, sall, dy, alog, dt, ng, *, cps, name):
    s = qkv.shape[0]
    c = CHUNK
    nc = s // c
    hd = GDN_DK
    tb = c * cps
    nt = nc // cps

    def body(qkv_ref, z_ref, ba_ref, sall_ref, dy_ref, al_ref, dt_ref, ng_ref,
             dqkv_ref, dz_ref, dba_ref, dal_ref, ddt_ref, dng_ref, ds_scr):
        @pl.when(pl.program_id(0) == 0)
        def _():
            ds_scr[...] = jnp.zeros_like(ds_scr)
            dal_ref[...] = jnp.zeros_like(dal_ref)
            ddt_ref[...] = jnp.zeros_like(ddt_ref)
            dng_ref[...] = jnp.zeros_like(dng_ref)

        nh = GDN_HEADS
        ba = ba_ref[...]
        g, gvf = jax.vjp(_gdn_g, ba, al_ref[...], dt_ref[...])
        gcums = [_chunk_cumsum(g[c * ci:c * (ci + 1)]) for ci in range(cps)]
        loc, loc_vf = jax.vjp(_gdn_local, *_gdn_inputs(qkv_ref, ba, gcums, cps))
        d_loc = [None] * cps
        dng = jnp.zeros((1, 128), F32)
        for ci in reversed(range(cps)):
            rs = slice(c * ci, c * (ci + 1))
            sl = slice(nh * ci, nh * (ci + 1))
            zz = jnp.stack([z_ref[rs, hd * h:hd * (h + 1)] for h in range(nh)])
            dyy = jnp.stack([dy_ref[rs, hd * h:hd * (h + 1)] for h in range(nh)])
            _, rvf = jax.vjp(_gdn_recur, sall_ref[ci], *(t[sl] for t in loc), zz, ng_ref[...])
            ds, du, dwk, dattn, dqg, dkd, deg, dzz, dn = rvf((dyy, ds_scr[...]))
            ds_scr[...] = ds
            d_loc[ci] = (du, dwk, dattn, dqg, dkd, deg)
            dng = dng + dn
            for h in range(nh):
                dz_ref[rs, hd * h:hd * (h + 1)] = dzz[h].astype(BF16)
        dq, dk, dv, dbp, dgc = loc_vf(tuple(jnp.concatenate([d_loc[ci][j] for ci in range(cps)], axis=0)
                                            for j in range(6)))
        dgs, dbas = [], []
        for ci in range(cps):
            rs = slice(c * ci, c * (ci + 1))
            dgcum = jnp.zeros((c, 128), F32)
            dba = jnp.zeros((c, 128), F32)
            for h in range(nh):
                b = nh * ci + h
                dqkv_ref[rs, hd * h:hd * (h + 1)] = dq[b]
                dqkv_ref[rs, 512 + hd * h:512 + hd * (h + 1)] = dk[b]
                dqkv_ref[rs, 1024 + hd * h:1024 + hd * (h + 1)] = dv[b]
                dgcum = dgcum + _put_col(dgc[b], 4 + h, 128)
                dba = dba + _put_col(dbp[b], h, 128)
            dgs.append(_chunk_cumsum(dgcum, transpose=True))
            dbas.append(dba)
        dba2, dal, ddt = gvf(jnp.concatenate(dgs, axis=0))
        dba_ref[...] = (jnp.concatenate(dbas, axis=0) + dba2).astype(BF16)
        dal_ref[...] += dal
        ddt_ref[...] += ddt
        dng_ref[...] += dng

    def rev(width):
        return pl.BlockSpec((tb, width), lambda i: (nt - 1 - i, 0))

    return pl.pallas_call(
        body, name=name, grid=(nt,),
        in_specs=[rev(1536), _pblk(tb, 512, C_Z, rev_n=nt), _pblk(tb, 128, C_BA, rev_n=nt),
                  pl.BlockSpec((cps, GDN_HEADS, hd, hd), lambda i: (nt - 1 - i, 0, 0, 0)), rev(512),
                  _full((1, 128)), _full((1, 128)), _full((1, 128))],
        out_specs=[rev(1536), rev(512), rev(128), _full((1, 128)), _full((1, 128)), _full((1, 128))],
        out_shape=[jax.ShapeDtypeStruct((s, 1536), F32), jax.ShapeDtypeStruct((s, 512), BF16),
                   jax.ShapeDtypeStruct((s, 128), BF16), jax.ShapeDtypeStruct((1, 128), F32),
                   jax.ShapeDtypeStruct((1, 128), F32), jax.ShapeDtypeStruct((1, 128), F32)],
        scratch_shapes=[pltpu.VMEM((GDN_HEADS, hd, hd), F32)],
        compiler_params=_params("arbitrary"),
    )(qkv, proj, proj, sall, dy, alog, dt, ng)


def merge_fwd(x, ys, proj, wb, wo, *, tb, dep=None, name):
    s, d = x.shape
    dep_specs, dep_args = _dep_operand(dep)

    def body(x_ref, ya, yb, yc, yd, g0, g1, g2, g3, wb_ref, wo_ref, *rest):
        o_ref, m_ref = rest[len(dep_args):]
        m = jnp.zeros((tb, d), F32)
        for g, (y_ref, gp_ref) in enumerate(zip((ya, yb, yc, yd), (g0, g1, g2, g3))):
            m = m + jax.nn.sigmoid(gp_ref[...]) * _dg(y_ref[...], wb_ref[g], 1, 0)
        mb = m.astype(BF16)
        m_ref[...] = mb
        o_ref[...] = x_ref[...] + _dg(mb, wo_ref[...], 1, 0)

    row = pl.BlockSpec((tb, d), lambda i: (i, 0))
    yrow = pl.BlockSpec((tb, BRANCH_W), lambda i: (i, 0))
    return pl.pallas_call(
        body, name=name, grid=(s // tb,),
        in_specs=[row] + [yrow] * 4 + [_pblk(tb, d, C_GATE + d * g) for g in range(4)]
        + [_full((4, BRANCH_W, d)), _full((d, d))] + dep_specs,
        out_specs=[row, row],
        out_shape=[jax.ShapeDtypeStruct((s, d), F32), jax.ShapeDtypeStruct((s, d), BF16)],
        compiler_params=_params("parallel"),
    )(x, *ys, proj, proj, proj, proj, wb, wo, *dep_args)


def merge_bwd(dx, ys, proj, wb, wo, *, tb, dep=None, name):
    s, d = dx.shape
    dep_specs, dep_args = _dep_operand(dep)

    def body(dx_ref, ya, yb, yc, yd, g0, g1, g2, g3, wb_ref, wo_ref, *rest):
        dgp_ref, dbr_ref, da, db, dc, dd, dxb_ref = rest[len(dep_args):]
        dxb = dx_ref[...].astype(BF16)
        dxb_ref[...] = dxb
        dm = _dg(dxb, wo_ref[...], 1, 1)
        for g, (y_ref, gp_ref, dy_ref) in enumerate(zip((ya, yb, yc, yd), (g0, g1, g2, g3), (da, db, dc, dd))):
            br = _dg(y_ref[...], wb_ref[g], 1, 0)
            sg = jax.nn.sigmoid(gp_ref[...])
            dgp_ref[:, d * g:d * (g + 1)] = (dm * br * sg * (1.0 - sg)).astype(BF16)
            dbr = (dm * sg).astype(BF16)
            dbr_ref[g] = dbr
            dy_ref[...] = _dg(dbr, wb_ref[g], 1, 1)

    row = pl.BlockSpec((tb, d), lambda i: (i, 0))
    yrow = pl.BlockSpec((tb, BRANCH_W), lambda i: (i, 0))
    return pl.pallas_call(
        body, name=name, grid=(s // tb,),
        in_specs=[row] + [yrow] * 4 + [_pblk(tb, d, C_GATE + d * g) for g in range(4)]
        + [_full((4, BRANCH_W, d)), _full((d, d))] + dep_specs,
        out_specs=[pl.BlockSpec((tb, 4 * d), lambda i: (i, 0)), pl.BlockSpec((4, tb, d), lambda i: (0, i, 0)),
                   yrow, yrow, yrow, yrow, row],
        out_shape=[jax.ShapeDtypeStruct((s, 4 * d), BF16), jax.ShapeDtypeStruct((4, s, d), BF16)]
        + [jax.ShapeDtypeStruct((s, BRANCH_W), F32)] * 4 + [jax.ShapeDtypeStruct((s, d), BF16)],
        compiler_params=_params("parallel"),
    )(dx, *ys, proj, proj, proj, proj, wb, wo, *dep_args)


def loss_head(x, g, target, *, tb, name):
    s, d = x.shape

    def body(x_ref, g_ref, t_ref, dx_ref, dg_ref, loss_ref):
        @pl.when(pl.program_id(0) == 0)
        def _():
            dg_ref[...] = jnp.zeros_like(dg_ref)
            loss_ref[...] = jnp.zeros_like(loss_ref)

        xv, gv = x_ref[...], g_ref[...]
        r = lax.rsqrt(jnp.mean(xv * xv, axis=-1, keepdims=True) + EPS)
        err = xv * r * gv - t_ref[...]
        loss_ref[...] += 0.5 * jnp.sum(jnp.mean(err * err, axis=-1, keepdims=True), axis=0, keepdims=True)
        dx, dg = _rms_bwd(xv, gv, err * (1.0 / d))
        dx_ref[...] = dx
        dg_ref[...] += dg

    row = pl.BlockSpec((tb, d), lambda i: (i, 0))
    return pl.pallas_call(
        body, name=name, grid=(s // tb,),
        in_specs=[row, _full((1, d)), row],
        out_specs=[row, _full((1, d)), _full((1, 128))],
        out_shape=[jax.ShapeDtypeStruct((s, d), F32), jax.ShapeDtypeStruct((1, d), F32),
                   jax.ShapeDtypeStruct((1, 128), F32)],
        compiler_params=_params("arbitrary"),
    )(x, g, target)


def adamw(w, g, m, v, *, name):
    shape = w.shape
    lead, (r, c) = shape[:-2], shape[-2:]
    tr = r
    while tr * c * 4 > (1 << 20) and tr % 16 == 0:
        tr //= 2
    c1 = 1.0 / (1.0 - ADAM_B1 ** ADAM_STEP)
    c2 = 1.0 / (1.0 - ADAM_B2 ** ADAM_STEP)

    def body(w_ref, g_ref, m_ref, v_ref, d_ref, nm_ref, nv_ref):
        gv = g_ref[...]
        nm = ADAM_B1 * m_ref[...] + (1.0 - ADAM_B1) * gv
        nv = ADAM_B2 * v_ref[...] + (1.0 - ADAM_B2) * (gv * gv)
        nm_ref[...] = nm
        nv_ref[...] = nv
        d_ref[...] = -ADAM_LR * ((nm * c1) / (jnp.sqrt(nv * c2) + ADAM_EPS) + ADAM_WD * w_ref[...])

    blk = pl.BlockSpec((None,) * len(lead) + (tr, c), lambda *i: i + (0,))
    return tuple(pl.pallas_call(
        body, name=name, grid=lead + (r // tr,),
        in_specs=[blk] * 4, out_specs=[blk] * 3,
        out_shape=[jax.ShapeDtypeStruct(shape, F32)] * 3,
        compiler_params=_params(*(("parallel",) * (len(lead) + 1))),
    )(w, g, m, v))


_ANY = pl.BlockSpec(memory_space=pl.ANY)


def _place():
    x, y, c = lax.axis_index("x"), lax.axis_index("y"), lax.axis_index("c")
    return x, y, c, [(1 - x, y), (x, 1 - y), (1 - x, 1 - y)]


def gather_shards(bufs, *, name):
    n = len(bufs)

    def body(*refs):
        outs = refs[n:2 * n]
        ici_send, ici_recv, d2d_send, d2d_recv = refs[2 * n:]
        x, y, c, chips = _place()
        me = 2 * x + y

        def ici(a, j, slab, to):
            return pltpu.make_async_remote_copy(
                src_ref=outs[a].at[slab, c], dst_ref=outs[a].at[slab, c], send_sem=ici_send.at[a, j],
                recv_sem=ici_recv.at[a, j], device_id=to, device_id_type=MESH)

        def d2d(a, j, slab, half):
            return pltpu.make_async_remote_copy(
                src_ref=outs[a].at[slab, half], dst_ref=outs[a].at[slab, half], send_sem=d2d_send.at[a, j],
                recv_sem=d2d_recv.at[a, j], device_id=(x, y, 1 - c), device_id_type=MESH)

        sends = []
        for a in range(n):
            for j, (px, py) in enumerate(chips):
                cp = ici(a, j, me, (px, py, c))
                cp.start()
                sends.append(cp)
        for a in range(n):
            for j, (px, py) in enumerate(chips):
                ici(a, j, 2 * px + py, (px, py, c)).wait_recv()
                cp = d2d(a, j, 2 * px + py, c)
                cp.start()
                sends.append(cp)
        for a in range(n):
            for j, (px, py) in enumerate(chips):
                d2d(a, j, 2 * px + py, 1 - c).wait_recv()
        for cp in sends:
            cp.wait_send()

    return pl.pallas_call(
        body, name=name,
        in_specs=[_ANY] * n, out_specs=[_ANY] * n,
        out_shape=[jax.ShapeDtypeStruct(t.shape, t.dtype) for t in bufs],
        input_output_aliases={a: a for a in range(n)},
        scratch_shapes=[pltpu.SemaphoreType.DMA((n, 3))] * 4,
    )(*bufs)


_SEM = pl.BlockSpec(memory_space=pltpu.SEMAPHORE)
_VMEM = pl.BlockSpec(memory_space=pltpu.VMEM)
_EFFECT = pltpu.SideEffectType.DATAFLOW_SIDE_EFFECTING
_TOKEN = jax.ShapeDtypeStruct((8, 128), F32)


def _gather_copies(outs, sems_ici, sems_d2d):
    x, y, c, chips = _place()

    def ici(a, j, slab, to):
        return pltpu.make_async_remote_copy(
            src_ref=outs[a].at[slab, c], dst_ref=outs[a].at[slab, c], send_sem=sems_ici[0][3 * a + j],
            recv_sem=sems_ici[1][3 * a + j], device_id=to, device_id_type=MESH)

    def d2d(a, j, slab, half):
        return pltpu.make_async_remote_copy(
            src_ref=outs[a].at[slab, half], dst_ref=outs[a].at[slab, half], send_sem=sems_d2d[0][3 * a + j],
            recv_sem=sems_d2d[1][3 * a + j], device_id=(x, y, 1 - c), device_id_type=MESH)

    return x, y, c, chips, ici, d2d


def _async_call(body, bufs, sems_in, after, sems_out, token, *, name):
    n = len(bufs)
    n_out = 6 * n if sems_out else 0
    extra = [] if after is None else [after]
    return pl.pallas_call(
        body, name=name,
        in_specs=[_ANY] * n + [_SEM] * len(sems_in) + [_ANY] * len(extra),
        out_specs=[_ANY] * n + [_SEM] * n_out + [_VMEM] * token,
        out_shape=[jax.ShapeDtypeStruct(t.shape, t.dtype) for t in bufs]
        + [pltpu.SemaphoreType.DMA(())] * n_out + [_TOKEN] * token,
        input_output_aliases={a: a for a in range(n)},
        compiler_params=pltpu.CompilerParams(has_side_effects=_EFFECT),
    )(*bufs, *sems_in, *extra)


def gather_start(bufs, after=None, *, name):
    n = len(bufs)
    k = n + (after is not None)

    def body(*refs):
        outs = refs[k:k + n]
        sems, token = refs[k + n:k + 7 * n], refs[k + 7 * n]
        x, y, c, chips, ici, _ = _gather_copies(outs, (sems[:3 * n], sems[3 * n:]), None)
        for a in range(n):
            for j, (px, py) in enumerate(chips):
                ici(a, j, 2 * x + y, (px, py, c)).start()
        token[...] = jnp.zeros_like(token)

    res = _async_call(body, bufs, [], after, True, True, name=name)
    return res[:n], res[n:7 * n], res[7 * n]


def gather_pass(bufs, sems, after, *, name):
    n = len(bufs)

    def body(*refs):
        s_in = refs[n:7 * n]
        outs = refs[7 * n + 1:8 * n + 1]
        s_out, token = refs[8 * n + 1:14 * n + 1], refs[14 * n + 1]
        x, y, c, chips, ici, d2d = _gather_copies(outs, (s_in[:3 * n], s_in[3 * n:]),
                                                  (s_out[:3 * n], s_out[3 * n:]))
        for a in range(n):
            for j, (px, py) in enumerate(chips):
                ici(a, j, 2 * px + py, (px, py, c)).wait_recv()
                d2d(a, j, 2 * px + py, c).start()
        for a in range(n):
            for j, (px, py) in enumerate(chips):
                ici(a, j, 2 * x + y, (px, py, c)).wait_send()
        token[...] = jnp.zeros_like(token)

    res = _async_call(body, bufs, list(sems), after, True, True, name=name)
    return res[:n], res[n:7 * n], res[7 * n]


def gather_wait(bufs, sems, after, *, name):
    n = len(bufs)

    def body(*refs):
        s_in = refs[n:7 * n]
        outs = refs[7 * n + 1:]
        x, y, c, chips, _, d2d = _gather_copies(outs, None, (s_in[:3 * n], s_in[3 * n:]))
        for a in range(n):
            for j, (px, py) in enumerate(chips):
                d2d(a, j, 2 * px + py, 1 - c).wait_recv()
                d2d(a, j, 2 * px + py, c).wait_send()

    return _async_call(body, bufs, list(sems), after, False, False, name=name)


def _pair_plan(srcs, lands):
    x, y, c, _ = _place()
    return [(srcs[a].at[j, 1 - c], lands[a].at[j], (x, y, 1 - c))
            for a in range(len(srcs)) for j in range(N_SHARD)]


def _chip_plan(srcs, lands):
    x, y, c, chips = _place()
    return [(srcs[a].at[2 * px + py], lands[a].at[2 * x + y], (px, py, c))
            for a in range(len(srcs)) for (px, py) in chips]


def copies_start(arrs, land_shapes, plan, per_array, *, name):
    n = len(arrs)
    k = per_array * n

    def body(*refs):
        lands = refs[2 * n:3 * n]
        sems, token = refs[3 * n:3 * n + 2 * k], refs[3 * n + 2 * k]
        for i, (src, dst, dev) in enumerate(plan(refs[n:2 * n], lands)):
            pltpu.make_async_remote_copy(src_ref=src, dst_ref=dst, send_sem=sems[i], recv_sem=sems[k + i],
                                         device_id=dev, device_id_type=MESH).start()
        token[...] = jnp.zeros_like(token)

    res = pl.pallas_call(
        body, name=name,
        in_specs=[_ANY] * n,
        out_specs=[_ANY] * (2 * n) + [_SEM] * (2 * k) + [_VMEM],
        out_shape=[jax.ShapeDtypeStruct(t.shape, t.dtype) for t in arrs] + list(land_shapes)
        + [pltpu.SemaphoreType.DMA(())] * (2 * k) + [_TOKEN],
        input_output_aliases={a: a for a in range(n)},
        compiler_params=pltpu.CompilerParams(has_side_effects=_EFFECT),
    )(*arrs)
    return res[:n], res[n:2 * n], res[2 * n:2 * n + 2 * k], res[2 * n + 2 * k]


def copies_wait(arrs, lands, sems, after, plan, *, name):
    n = len(arrs)
    k = len(sems) // 2

    def body(*refs):
        s_in = refs[2 * n:2 * n + 2 * k]
        outs = refs[2 * n + 2 * k + 1:]
        for i, (src, dst, dev) in enumerate(plan(outs[:n], outs[n:])):
            cp = pltpu.make_async_remote_copy(src_ref=src, dst_ref=dst, send_sem=s_in[i], recv_sem=s_in[k + i],
                                              device_id=dev, device_id_type=MESH)
            cp.wait_send()
            cp.wait_recv()

    res = pl.pallas_call(
        body, name=name,
        in_specs=[_ANY] * (2 * n) + [_SEM] * (2 * k) + [_ANY],
        out_specs=[_ANY] * (2 * n),
        out_shape=[jax.ShapeDtypeStruct(t.shape, t.dtype) for t in list(arrs) + list(lands)],
        input_output_aliases={a: a for a in range(2 * n)},
        compiler_params=pltpu.CompilerParams(has_side_effects=_EFFECT),
    )(*arrs, *lands, *sems, after)
    return res[:n], res[n:]


def pair_send_halves(arrs, *, name):
    n = len(arrs)

    def body(*refs):
        ins, outs = refs[:n], refs[n:2 * n]
        send_sems, recv_sems = refs[2 * n:]
        x, y, c, _ = _place()
        copies = []
        for a in range(n):
            for j in range(N_SHARD):
                cp = pltpu.make_async_remote_copy(
                    src_ref=ins[a].at[j, 1 - c], dst_ref=outs[a].at[j], send_sem=send_sems.at[a, j],
                    recv_sem=recv_sems.at[a, j], device_id=(x, y, 1 - c), device_id_type=MESH)
                cp.start()
                copies.append(cp)
        for cp in copies:
            cp.wait()

    return pl.pallas_call(
        body, name=name,
        in_specs=[_ANY] * n, out_specs=[_ANY] * n,
        out_shape=[jax.ShapeDtypeStruct((N_SHARD,) + t.shape[2:], t.dtype) for t in arrs],
        scratch_shapes=[pltpu.SemaphoreType.DMA((n, N_SHARD)), pltpu.SemaphoreType.DMA((n, N_SHARD))],
    )(*arrs)


def chip_scatter(arrs, *, name):
    n = len(arrs)

    def body(*refs):
        ins, outs = refs[:n], refs[n:2 * n]
        send_sems, recv_sems = refs[2 * n:]
        x, y, c, chips = _place()
        me = 2 * x + y
        copies = []
        for a in range(n):
            for j, (px, py) in enumerate(chips):
                cp = pltpu.make_async_remote_copy(
                    src_ref=ins[a].at[2 * px + py], dst_ref=outs[a].at[me], send_sem=send_sems.at[a, j],
                    recv_sem=recv_sems.at[a, j], device_id=(px, py, c), device_id_type=MESH)
                cp.start()
                copies.append(cp)
        for cp in copies:
            cp.wait()

    return pl.pallas_call(
        body, name=name,
        in_specs=[_ANY] * n, out_specs=[_ANY] * n,
        out_shape=[jax.ShapeDtypeStruct(t.shape, t.dtype) for t in arrs],
        scratch_shapes=[pltpu.SemaphoreType.DMA((n, 3)), pltpu.SemaphoreType.DMA((n, 3))],
    )(*arrs)


def pair_gather(bufs, *, name):
    n = len(bufs)

    def body(*refs):
        outs = refs[n:2 * n]
        send_sems, recv_sems = refs[2 * n:]
        x, y, c, _ = _place()
        copies = []
        for a in range(n):
            cp = pltpu.make_async_remote_copy(
                src_ref=outs[a].at[c], dst_ref=outs[a].at[c], send_sem=send_sems.at[a],
                recv_sem=recv_sems.at[a], device_id=(x, y, 1 - c), device_id_type=MESH)
            cp.start()
            copies.append(cp)
        for cp in copies:
            cp.wait()

    return pl.pallas_call(
        body, name=name,
        in_specs=[_ANY] * n, out_specs=[_ANY] * n,
        out_shape=[jax.ShapeDtypeStruct(t.shape, t.dtype) for t in bufs],
        input_output_aliases={a: a for a in range(n)},
        scratch_shapes=[pltpu.SemaphoreType.DMA((n,)), pltpu.SemaphoreType.DMA((n,))],
    )(*bufs)


def all_sum_small(v, *, name):
    m = v.shape[0]

    def body(v_ref, o_ref, buf, send_sems, recv_sems, local_sem):
        x, y, c, chips = _place()
        me, sibling = (x, y, c), (x, y, 1 - c)

        def rows(px, py, pc):
            return buf.at[pl.ds((4 * px + 2 * py + pc) * m, m), :]

        def copy(k, block, to, src=None):
            return pltpu.make_async_remote_copy(
                src_ref=rows(*block) if src is None else src, dst_ref=rows(*block),
                send_sem=send_sems.at[k], recv_sem=recv_sems.at[k], device_id=to, device_id_type=MESH)

        mine = pltpu.make_async_copy(v_ref, rows(*me), local_sem)
        mine.start()
        first = [copy(0, me, sibling, src=v_ref)]
        first += [copy(1 + j, me, (*chip, c), src=v_ref) for j, chip in enumerate(chips)]
        for cp in first:
            cp.start()
        passed = [copy(4 + j, (*chip, c), sibling) for j, chip in enumerate(chips)]
        for j, chip in enumerate(chips):
            copy(1 + j, (*chip, c), me).wait_recv()
            passed[j].start()
        copy(0, sibling, me).wait_recv()
        for j, chip in enumerate(chips):
            copy(4 + j, (*chip, 1 - c), me).wait_recv()
        for cp in first + passed:
            cp.wait_send()
        mine.wait()
        acc = buf[pl.ds(0, m), :]
        for k in range(1, 8):
            acc = acc + buf[pl.ds(k * m, m), :]
        o_ref[...] = acc

    vm = pl.BlockSpec(memory_space=pltpu.VMEM)
    return pl.pallas_call(
        body, name=name, in_specs=[vm], out_specs=vm,
        out_shape=jax.ShapeDtypeStruct((m, 128), F32),
        scratch_shapes=[pltpu.VMEM((8 * m, 128), F32), pltpu.SemaphoreType.DMA((7,)),
                        pltpu.SemaphoreType.DMA((7,)), pltpu.SemaphoreType.DMA],
    )(v)


def pack_slab(srcs, blocks, maps, n_slots, slot_shape, *, place, dep=None, name):
    k = len(srcs)
    per = n_slots // k
    dep_specs, dep_args = _dep_operand(dep)

    def body(place_ref, *refs):
        o_ref = refs[k + len(dep_args)]
        s = pl.program_id(0)
        for i in range(k):
            @pl.when(jnp.logical_and(s >= i * per, s < (i + 1) * per))
            def _():
                o_ref[...] = refs[i][...].astype(BF16)

    return pl.pallas_call(
        body, name=name,
        grid_spec=pltpu.PrefetchScalarGridSpec(
            num_scalar_prefetch=1, grid=(n_slots,),
            in_specs=[pl.BlockSpec(blk, functools.partial(lambda s, pr, m: m(s), m=m)) for blk, m in zip(blocks, maps)]
            + dep_specs,
            out_specs=pl.BlockSpec((None, None) + slot_shape, lambda s, pr: (pr[1], s, 0, 0))),
        out_shape=jax.ShapeDtypeStruct((N_SHARD, n_slots) + slot_shape, BF16),
        compiler_params=_params("arbitrary"),
    )(place, *srcs, *dep_args)


def sum_halves(arr, recv, c_idx, *, name):
    _, _, r, c = arr.shape
    tr = r
    while tr * c * 4 > (1 << 20) and tr % 32 == 0:
        tr //= 2

    def body(c_ref, a_ref, b_ref, o_ref):
        o_ref[...] = (a_ref[...] + b_ref[...]).astype(BF16)

    return pl.pallas_call(
        body, name=name,
        grid_spec=pltpu.PrefetchScalarGridSpec(
            num_scalar_prefetch=1, grid=(N_SHARD, r // tr),
            in_specs=[pl.BlockSpec((None, None, tr, c), lambda j, i, cr: (j, cr[0], i, 0)),
                      pl.BlockSpec((None, tr, c), lambda j, i, cr: (j, i, 0))],
            out_specs=pl.BlockSpec((None, tr, c), lambda j, i, cr: (j, i, 0))),
        out_shape=jax.ShapeDtypeStruct((N_SHARD, r, c), BF16),
        compiler_params=_params("parallel", "parallel"),
    )(c_idx, arr, recv)


def sum_chips(recv, part, place, *, name):
    _, r, c = recv.shape
    tr = r
    while tr * c * 4 > (1 << 20) and tr % 32 == 0:
        tr //= 2

    def body(place_ref, a0, a1, a2, a3, p_ref, o_ref):
        me = place_ref[1]
        acc = None
        for k, a_ref in enumerate((a0, a1, a2, a3)):
            term = jnp.where(me == k, p_ref[...], a_ref[...]).astype(F32)
            acc = term if acc is None else acc + term
        o_ref[...] = acc

    def slab(k):
        return pl.BlockSpec((None, tr, c), lambda i, pr: (jnp.where(pr[1] == k, (k + 1) % N_SHARD, k), i, 0))

    return pl.pallas_call(
        body, name=name,
        grid_spec=pltpu.PrefetchScalarGridSpec(
            num_scalar_prefetch=1, grid=(r // tr,),
            in_specs=[slab(k) for k in range(N_SHARD)]
            + [pl.BlockSpec((None, tr, c), lambda i, pr: (pr[1], i, 0))],
            out_specs=pl.BlockSpec((None, tr, c), lambda i, pr: (pr[0], i, 0))),
        out_shape=jax.ShapeDtypeStruct((2, r, c), F32),
        compiler_params=_params("parallel"),
    )(place, recv, recv, recv, recv, part)


_WEIGHTS = ("ff1_norm", "ff1_wg", "ff1_wu", "ff1_wd", "mix_norm", "w_in", "sgu_ln_g", "sgu_ln_b", "sgu_w",
            "sgu_b", "lru_conv_w", "lru_conv_b", "lru_wa", "lru_ba", "lru_wx", "lru_bx", "lru_lambda",
            "gdn_conv_w", "gdn_a_log", "gdn_dt_bias", "gdn_norm_g", "pool_w", "pool_scale", "w_branch",
            "w_out", "ff2_norm", "ff2_wg", "ff2_wu", "ff2_wd", "final_norm")
_BIG = ("ff1_wg", "ff1_wu", "ff1_wd", "w_in", "w_branch", "w_out", "ff2_wg", "ff2_wu", "ff2_wd")
_SMALL = tuple(n for n in _WEIGHTS if n not in _BIG)
_CONV = ("lru_conv_w", "gdn_conv_w")


def _seg_rows(shape):
    return -(-math.prod(shape) // 1024) * 8


def _pack(arrs):
    segs = []
    for t in arrs:
        rows = _seg_rows(t.shape)
        flat = t.reshape(-1)
        segs.append(jnp.pad(flat, (0, rows * 128 - flat.shape[0])).reshape(rows, 128))
    return jnp.concatenate(segs, axis=0)


def _unpack(buf, shapes):
    out, o = [], 0
    for shp in shapes:
        rows = _seg_rows(shp)
        out.append(buf[o:o + rows].reshape(-1)[:math.prod(shp)].reshape(shp))
        o += rows
    return out


def _block_diag(w):
    h, n, _ = w.shape
    same = jnp.arange(h)[:, None, None, None] == jnp.arange(h)[None, None, :, None]
    return jnp.where(same, w[:, :, None, :], 0.0).reshape(h * n, h * n)


def _diag_blocks(m, h=8, n=64):
    return jnp.stack([m[i * n:(i + 1) * n, i * n:(i + 1) * n] for i in range(h)])


def _lane_row(v):
    return jnp.zeros((1, 128), F32).at[0, 4:8].set(v)


def _w_in_to_padded(raw):
    sem = jnp.concatenate([raw[k, :, :P_IN_SHARD] for k in range(N_SHARD)], axis=1)
    zero = jnp.zeros((sem.shape[0], C_GATE - C_BA - 8), sem.dtype)
    return jnp.concatenate([sem[:, 2048:4096], sem[:, 0:2048], sem[:, 4104:4616], sem[:, 4096:4104], zero,
                            sem[:, 4616:P_IN]], axis=1)


def _w_in_grad_to_shards(part):
    r = part.shape[1]
    dwp = part.transpose(1, 0, 2).reshape(r, P_PAD)
    sem = jnp.concatenate([dwp[:, C_AU:C_DX], dwp[:, 0:C_AU], dwp[:, C_BA:C_BA + 8], dwp[:, C_DX:C_BA],
                           dwp[:, C_GATE:]], axis=1)
    sh = sem.reshape(r, N_SHARD, P_IN_SHARD).transpose(1, 0, 2)
    return jnp.pad(sh, ((0, 0), (0, 0), (0, P_IN_SHARD_PAD - P_IN_SHARD)))


def kernel(x, ff1_norm, ff1_wg, ff1_wu, ff1_wd, mix_norm, w_in, sgu_ln_g, sgu_ln_b, sgu_w, sgu_b, lru_conv_w,
           lru_conv_b, lru_wa, lru_ba, lru_wx, lru_bx, lru_lambda, gdn_conv_w, gdn_a_log, gdn_dt_bias, gdn_norm_g,
           pool_w, pool_scale, w_branch, w_out, ff2_norm, ff2_wg, ff2_wu, ff2_wd, final_norm, loss_target, m_ff1_norm,
           m_ff1_wg, m_ff1_wu, m_ff1_wd, m_mix_norm, m_w_in, m_sgu_ln_g, m_sgu_ln_b, m_sgu_w, m_sgu_b, m_lru_conv_w,
           m_lru_conv_b, m_lru_wa, m_lru_ba, m_lru_wx, m_lru_bx, m_lru_lambda, m_gdn_conv_w, m_gdn_a_log, m_gdn_dt_bias,
           m_gdn_norm_g, m_pool_w, m_pool_scale, m_w_branch, m_w_out, m_ff2_norm, m_ff2_wg, m_ff2_wu, m_ff2_wd,
           m_final_norm, v_ff1_norm, v_ff1_wg, v_ff1_wu, v_ff1_wd, v_mix_norm, v_w_in, v_sgu_ln_g, v_sgu_ln_b, v_sgu_w,
           v_sgu_b, v_lru_conv_w, v_lru_conv_b, v_lru_wa, v_lru_ba, v_lru_wx, v_lru_bx, v_lru_lambda, v_gdn_conv_w,
           v_gdn_a_log, v_gdn_dt_bias, v_gdn_norm_g, v_pool_w, v_pool_scale, v_w_branch, v_w_out, v_ff2_norm, v_ff2_wg,
           v_ff2_wu, v_ff2_wd, v_final_norm):
    a = dict(locals())
    n_layer = ff1_norm.shape[0]
    d = D_MODEL
    x0 = x[0]
    chip = 2 * lax.axis_index("x") + lax.axis_index("y")
    c_idx = jnp.reshape(lax.axis_index("c"), (1,)).astype(jnp.int32)
    place = jnp.stack([lax.axis_index("c"), chip]).astype(jnp.int32)

    def own_slab(t, dtype=BF16):
        buf = lax.empty((N_SHARD,) + t.shape, dtype)
        return lax.dynamic_update_slice(buf, t.astype(dtype)[None], (chip,) + (0,) * t.ndim)

    assert n_layer == 2

    def layer_bufs(l, dep=None):
        fc = FF_SHARD
        whole = lambda s: (l, 0, 0)
        rows = lambda s: (l, s, 0)
        pack = functools.partial(pack_slab, place=place, dep=dep)
        bufs = []
        for tag, wg, wu, wd in (("ffn1", ff1_wg, ff1_wu, ff1_wd), ("ffn2", ff2_wg, ff2_wu, ff2_wd)):
            bufs.append([pack([wg, wu], [(None, d, fc)] * 2, [whole] * 2, 2, (d, fc), name=f"pack_{tag}_wgu_{l}"),
                         pack([wd], [(None, fc // 2, d)], [rows], 2, (fc // 2, d), name=f"pack_{tag}_wd_{l}")])
        win = pack([w_in], [(None, d // 2, P_IN_SHARD_PAD)], [rows], 2, (d // 2, P_IN_SHARD_PAD),
                   name=f"pack_w_in_{l}")
        wb = pack([w_branch], [(None, None, BRANCH_W, d // 4)], [lambda s: (l, s, 0, 0)], 4, (BRANCH_W, d // 4),
                  name=f"pack_wb_{l}")
        wo = pack([w_out], [(None, d // 8, d)], [rows], 2, (d // 8, d), name=f"pack_wo_{l}")
        return bufs[0] + [win, wb.reshape(N_SHARD, 2, 2, BRANCH_W, d // 4), wo] + bufs[1]

    n_buf = 7
    groups = {"ffn1": (0, 2), "mix": (2, 5), "ffn2": (5, 7)}

    def group_sems(sems, lo, hi):
        return list(sems[3 * lo:3 * hi]) + list(sems[3 * n_buf + 3 * lo:3 * n_buf + 3 * hi])

    def ffn_weights(wgu, wd):
        return (wgu, wgu, wd.reshape(N_SHARD, 1, FF_SHARD, d), (0, 1, 0))

    def mix_weights(g_w_in, g_wb, g_wo):
        return {"w_in": _w_in_to_padded(g_w_in.reshape(4, d, P_IN_SHARD_PAD)),
                "wb": g_wb.reshape(4, 4, BRANCH_W, d // 4).transpose(1, 2, 0, 3).reshape(4, BRANCH_W, d),
                "wo": g_wo.reshape(d, d)}

    g_lcw, g_gcw = gather_shards([own_slab(lru_conv_w, F32), own_slab(gdn_conv_w, F32)], name="gather_conv_weights")

    def small_params(l):
        return {"lcw": g_lcw[:, l].transpose(1, 0, 2).reshape(4, BRANCH_W),
                "gcw": g_gcw[:, l].transpose(1, 0, 2).reshape(4, 3 * BRANCH_W),
                "lcb": lru_conv_b[l][None],
                "bt": jnp.zeros((128, 128), F32).at[:, :4].set(sgu_b[l].T),
                "wa": _block_diag(lru_wa[l]), "wx": _block_diag(lru_wx[l]),
                "alog": _lane_row(gdn_a_log[l]), "dt": _lane_row(gdn_dt_bias[l])}

    flying = {0: gather_start(layer_bufs(0), name="gather_weights_0_start")[:2]}
    handed = {}

    def hand_on(l, grp, after):
        lo, hi = groups[grp]
        bufs_l, sems_l = flying[l]
        got, sm, tk = gather_pass(bufs_l[lo:hi], group_sems(sems_l, lo, hi), after,
                                  name=f"gather_weights_{l}_{grp}_pass")
        handed[l, grp] = (got, sm)
        return tk

    def arrived(l, grp, after):
        return gather_wait(*handed.pop((l, grp)), after, name=f"gather_weights_{l}_{grp}_wait")

    saved = []
    xs = x0
    tok = hand_on(0, "ffn1", x0)
    for l in range(n_layer):
        p = small_params(l)
        sv = {"p": p, "x0": xs}
        p["ff1"] = ffn_weights(*arrived(l, "ffn1", tok))
        x1 = ffn_fwd(xs, ff1_norm[l][None], *p["ff1"], tb=512, name=f"ffn1_fwd_{l}")
        tok = hand_on(l, "mix", x1)
        if l == 0:
            bufs1 = layer_bufs(1, dep=tok)
            split = groups["ffn1"][1]
            first1 = gather_start(bufs1[:split], name="gather_weights_1_start_ffn1")
        h = norm_fwd(x1, mix_norm[l][None], tb=512, name=f"mix_norm_fwd_{l}")
        p.update(mix_weights(*arrived(l, "mix", h)))
        proj = matmul(h, p["w_in"], tm=1024, tn=1536, tk=1024, name=f"proj_{l}")
        ya = sgu_fwd(proj, sgu_ln_g[l][None], sgu_ln_b[l][None], sgu_w[l], p["bt"], tb=512, name=f"sgu_fwd_{l}")
        xc = conv_fwd(proj, C_BX, BRANCH_W, p["lcw"], p["lcb"], tb=512, name=f"lru_conv_fwd_{l}")
        yb, hl = lru_fwd(xc, proj, p["wa"], p["wx"], lru_ba[l][None], lru_bx[l][None], lru_lambda[l][None],
                         tb=256, name=f"lru_fwd_{l}")
        hand_on(l, "ffn2", yb)
        if l == 0:
            rest1 = gather_start(bufs1[split:], yb, name="gather_weights_1_start_rest")
            k1, k2 = 3 * split, 3 * (n_buf - split)
            flying[1] = (list(first1[0]) + list(rest1[0]),
                         list(first1[1][:k1]) + list(rest1[1][:k2]) + list(first1[1][k1:]) + list(rest1[1][k2:]))
        qkv = conv_fwd(proj, C_QKV, 3 * BRANCH_W, p["gcw"], jnp.zeros((1, 3 * BRANCH_W), F32), tb=512,
                       name=f"gdn_conv_fwd_{l}")
        yc, sall = gdn_fwd(qkv, proj, p["alog"], p["dt"], gdn_norm_g[l][None], cps=GDN_CPS, name=f"gdn_fwd_{l}")
        p["ff2"] = ffn_weights(*arrived(l, "ffn2", yc))
        dep2 = None
        if l + 1 < n_layer:
            dep2 = tok = hand_on(l + 1, "ffn1", yc)
        yd = pool_fwd(proj, pool_w[l], pool_scale[l][None], tb=512, name=f"pool_fwd_{l}")
        ys = (ya, yb, yc, yd)
        x2, merged = merge_fwd(x1, ys, proj, p["wb"], p["wo"], tb=256, dep=dep2, name=f"merge_fwd_{l}")
        x3 = ffn_fwd(x2, ff2_norm[l][None], *p["ff2"], tb=512, name=f"ffn2_fwd_{l}")
        tok = x3
        sv.update(x1=x1, h=h, proj=proj, xc=xc, hl=hl, qkv=qkv, sall=sall, ys=ys, merged=merged, x2=x2)
        saved.append(sv)
        xs = x3

    dx, d_final, loss_blk = loss_head(xs, final_norm[None], loss_target[0], tb=512, name="loss_head")
    loss = lax.psum(loss_blk[0, 0], ("x", "y", "c"))

    small_g = {n: [None] * n_layer for n in _SMALL if n != "final_norm"}
    big_g = [None] * n_layer
    per_layer = 7
    flight = None

    def pair_sums(bigs, recv, tag, w_in_at):
        part = [sum_halves(t, r, c_idx, name=f"grad_pair_sum_{tag}_{i}") for i, (t, r) in enumerate(zip(bigs, recv))]
        if w_in_at is not None:
            part[w_in_at] = _w_in_grad_to_shards(part[w_in_at])
        return part

    def pair_start(arrs, tag):
        return copies_start(arrs, [jax.ShapeDtypeStruct((N_SHARD,) + t.shape[2:], t.dtype) for t in arrs],
                            _pair_plan, N_SHARD, name=f"grad_pair_exchange_{tag}_start")

    def pair_to_chip(fl, after, tag, w_in_at=None):
        sent, recv = copies_wait(*fl[:3], after, _pair_plan, name=f"grad_pair_exchange_{tag}_wait")
        part = pair_sums(sent, recv, tag, w_in_at)
        return copies_start(part, [jax.ShapeDtypeStruct(t.shape, t.dtype) for t in part], _chip_plan, 3,
                            name=f"grad_chip_exchange_{tag}_start")

    def chip_finish(fl, after, tag):
        part, recv = copies_wait(*fl[:3], after, _chip_plan, name=f"grad_chip_exchange_{tag}_wait")
        return [sum_chips(r, q, place, name=f"grad_chip_sum_{tag}_{i}") for i, (r, q) in enumerate(zip(recv, part))]

    early = mid = None

    for l in reversed(range(n_layer)):
        sv = saved[l]
        p = sv["p"]
        proj = sv["proj"]
        dx2, dg_ff2, dab2, hid2, h2, dy2 = ffn_bwd(sv["x2"], dx, ff2_norm[l][None], *p["ff2"], tb=512,
                                                  dep=None if flight is None else flight[3],
                                                  name=f"ffn2_bwd_{l}")
        dwgu2 = matmul(h2, dab2, ta=True, tm=1024, tn=FF_SHARD, tk=TK_DW, name=f"ffn2_dwgu_{l}")
        dwd2 = matmul(hid2, dy2, ta=True, tm=FF_SHARD, tn=1024, tk=TK_DW, name=f"ffn2_dwd_{l}")
        if flight is not None:
            flight = pair_to_chip(flight, dx2, "1", w_in_at=2)
        dgp, dbr, dya, dyb, dyc, dyd, dx2b = merge_bwd(dx2, sv["ys"], proj, p["wb"], p["wo"], tb=256,
                                                       dep=None if flight is None else flight[3],
                                                       name=f"merge_bwd_{l}")
        dwo = matmul(sv["merged"], dx2b, ta=True, tm=1024, tn=1024, tk=TK_DW, name=f"dwo_{l}")
        dwb = jnp.stack([matmul(sv["ys"][g], dbr[g], ta=True, tm=BRANCH_W, tn=1024, tk=TK_DW, name=f"dwb{g}_{l}")
                         for g in range(4)])
        dwb_sh = dwb.reshape(4, BRANCH_W, N_SHARD, d // N_SHARD).transpose(2, 0, 1, 3)
        g_late = [dwgu2, dwd2.reshape(N_SHARD, 2, FF_SHARD // 2, d),
                  dwb_sh.reshape(N_SHARD, 2, 2 * BRANCH_W, d // N_SHARD), dwo.reshape(N_SHARD, 2, d // 8, d)]
        tok = None
        if l == 0:
            early = pair_start(g_late, "0a")
            tok = early[3]
        du, dv, dlg, dlb, dws, dbt = sgu_bwd(proj, dya, sgu_ln_g[l][None], sgu_ln_b[l][None], sgu_w[l], p["bt"],
                                             tb=512, dep=tok, name=f"sgu_bwd_{l}")
        dxc, dgate, dwa, dwx, dba, dbx, dlam = lru_bwd(sv["xc"], proj, sv["hl"], dyb, p["wa"], p["wx"],
                                                       lru_ba[l][None], lru_bx[l][None], lru_lambda[l][None],
                                                       tb=256, name=f"lru_bwd_{l}")
        dbx_in, dcw_l = conv_bwd(proj, C_BX, BRANCH_W, dxc, p["lcw"], tb=512, name=f"lru_conv_bwd_{l}")
        dqkv, dz, dbeta, dal, ddt, dng = gdn_bwd(sv["qkv"], proj, sv["sall"], dyc, p["alog"], p["dt"],
                                                 gdn_norm_g[l][None], cps=GDN_CPS, name=f"gdn_bwd_{l}")
        dqkv_in, dcw_g = conv_bwd(proj, C_QKV, 3 * BRANCH_W, dqkv, p["gcw"], tb=512, name=f"gdn_conv_bwd_{l}")
        ddx, dwp, dsc = pool_bwd(proj, dyd, pool_w[l], pool_scale[l][None], tb=512, name=f"pool_bwd_{l}")
        if l == 0:
            early = pair_to_chip(early, ddx, "0a")
            tok = early[3]
        s = dx.shape[0]
        dproj = jnp.concatenate([dqkv_in, dz, du, dv, dbx_in, dgate, ddx, dbeta,
                                 jnp.zeros((s, C_GATE - C_BA - 128), BF16), dgp], axis=1)
        dh = matmul(dproj, p["w_in"], tb=True, tm=1024, tn=1024, tk=1536, dep=tok, name=f"dh_mix_{l}")
        dw_in = matmul(sv["h"], dproj, ta=True, tm=1024, tn=1152, tk=TK_DW, col_groups=N_SHARD, name=f"dw_in_{l}")
        dw_in = dw_in.reshape(N_SHARD, 2, d // 2, P_PAD // N_SHARD)
        if l == 0:
            mid = pair_start([dw_in], "0b")
            tok = mid[3]
        dx1, dg_mix = norm_bwd(sv["x1"], mix_norm[l][None], dh, dx2, tb=512, dep=tok, name=f"mix_norm_bwd_{l}")
        dx, dg_ff1, dab1, hid1, h1, dy1 = ffn_bwd(sv["x0"], dx1, ff1_norm[l][None], *p["ff1"],
                                                 tb=512, name=f"ffn1_bwd_{l}")
        if l == 0:
            mid = pair_to_chip(mid, dx, "0b", w_in_at=0)
            tok = mid[3]
        dwgu1 = matmul(h1, dab1, ta=True, tm=1024, tn=FF_SHARD, tk=TK_DW, dep=tok, name=f"ffn1_dwgu_{l}")
        dwd1 = matmul(hid1, dy1, ta=True, tm=FF_SHARD, tn=1024, tk=TK_DW, name=f"ffn1_dwd_{l}")

        for n, g in (("ff1_norm", dg_ff1[0]), ("mix_norm", dg_mix[0]), ("sgu_ln_g", dlg[0]), ("sgu_ln_b", dlb[0]),
                     ("sgu_w", dws), ("sgu_b", dbt[:, :4].T), ("lru_conv_w", dcw_l[:4]), ("lru_conv_b", dcw_l[4]),
                     ("lru_wa", _diag_blocks(dwa)), ("lru_ba", dba[0]), ("lru_wx", _diag_blocks(dwx)),
                     ("lru_bx", dbx[0]), ("lru_lambda", dlam[0]), ("gdn_conv_w", dcw_g[:4]),
                     ("gdn_a_log", dal[0, 4:8]), ("gdn_dt_bias", ddt[0, 4:8]), ("gdn_norm_g", dng[0]),
                     ("pool_w", dwp), ("pool_scale", dsc[0]), ("ff2_norm", dg_ff2[0])):
            small_g[n][l] = g
        big_g[l] = [dwgu1, dwd1.reshape(N_SHARD, 2, FF_SHARD // 2, d), dw_in, g_late[2], g_late[3], g_late[0], g_late[1]]
        if l == n_layer - 1:
            flight = pair_start(big_g[l], "1")

    last = big_g[0][:2]
    recv = pair_send_halves(last, name="grad_pair_exchange_0c")
    part = pair_sums(last, recv, "0c", None)
    recv = chip_scatter(part, name="grad_chip_exchange_0c")
    red_c = [sum_chips(r, q, place, name=f"grad_chip_sum_0c_{i}") for i, (r, q) in enumerate(zip(recv, part))]
    red1 = chip_finish(flight, red_c[0], "1")
    red_a = chip_finish(early, red_c[0], "0a")
    red_b = chip_finish(mid, red_c[0], "0b")
    full = pair_gather(red_c + red_b + [red_a[2], red_a[3], red_a[0], red_a[1]] + red1, name="grad_pair_gather")
    grads = {}

    def layers(k, f):
        return jnp.stack([f(full[l * per_layer + k]) for l in range(n_layer)])

    grads["ff1_wg"] = layers(0, lambda t: t[0])
    grads["ff1_wu"] = layers(0, lambda t: t[1])
    grads["ff1_wd"] = layers(1, lambda t: t.reshape(FF_SHARD, d))
    grads["w_in"] = layers(2, lambda t: t.reshape(d, P_IN_SHARD_PAD)[:, :P_IN_SHARD])
    grads["w_branch"] = layers(3, lambda t: t.reshape(4, BRANCH_W, d // N_SHARD))
    grads["w_out"] = layers(4, lambda t: t.reshape(d // N_SHARD, d))
    grads["ff2_wg"] = layers(5, lambda t: t[0])
    grads["ff2_wu"] = layers(5, lambda t: t[1])
    grads["ff2_wd"] = layers(6, lambda t: t.reshape(FF_SHARD, d))

    small_full = [jnp.stack(small_g[n]) for n in _SMALL if n != "final_norm"] + [d_final[0]]
    summed = _unpack(all_sum_small(_pack(small_full), name="grad_small_allreduce"), [t.shape for t in small_full])
    for n, g in zip(_SMALL, summed):
        if n in _CONV:
            width = a[n].shape[-1]
            g = lax.dynamic_slice_in_dim(g, chip * width, width, axis=2)
        grads[n] = g

    delta, new_m, new_v = {}, {}, {}
    for n in _BIG:
        args = (a[n], grads[n], a["m_" + n], a["v_" + n])
        if a[n].shape[-1] == FF_SHARD:
            outs = adamw(*(t.swapaxes(1, 2) for t in args), name=f"adamw_{n}")
            delta[n], new_m[n], new_v[n] = (t.swapaxes(1, 2) for t in outs)
        else:
            delta[n], new_m[n], new_v[n] = adamw(*args, name=f"adamw_{n}")
    shapes = [a[n].shape for n in _SMALL]
    packed = [_pack([src[pre + n] for n in _SMALL]) for src, pre in ((a, ""), (grads, ""), (a, "m_"), (a, "v_"))]
    for store, buf in zip((delta, new_m, new_v), adamw(*packed, name="adamw_small")):
        store.update(zip(_SMALL, _unpack(buf, shapes)))

    return (loss, dx[None], *[grads[n] for n in _WEIGHTS], *[delta[n] for n in _WEIGHTS],
            *[new_m[n] for n in _WEIGHTS], *[new_v[n] for n in _WEIGHTS])
```

```python
import functools
import math

import jax
import jax.numpy as jnp
from jax import lax
from jax.experimental import pallas as pl
from jax.experimental.pallas import tpu as pltpu

F32 = jnp.float32
BF16 = jnp.bfloat16
MESH = pl.DeviceIdType.MESH

D_MODEL = 1024
N_SHARD = 4
D_FF = 2816
FF_SHARD = D_FF // N_SHARD
BRANCH_W = 512
CHUNK = 64
GDN_HEADS = 4
GDN_DK = 128
GDN_CPS = 4
TK_DW = 2048
LRU_C = 8.0
EPS = 1e-6
P_IN = 8712
P_IN_SHARD = P_IN // N_SHARD
P_IN_SHARD_PAD = 2304
P_PAD = 9216
C_QKV, C_Z, C_AU, C_AV, C_BX, C_BG, C_DX, C_BA, C_GATE = 0, 1536, 2048, 2560, 3072, 3584, 4096, 4608, 5120

ADAM_LR, ADAM_B1, ADAM_B2, ADAM_EPS, ADAM_WD, ADAM_STEP = 0.001, 0.9, 0.999, 1e-08, 0.01, 10


def _gelu(x):
    return 0.5 * x * (1.0 + jnp.tanh(0.7978845608028654 * (x + 0.044715 * (x * x * x))))


def _gelu_grad(x):
    u = 0.7978845608028654 * (x + 0.044715 * (x * x * x))
    t = jnp.tanh(u)
    return 0.5 * (1.0 + t) + 0.5 * x * (1.0 - t * t) * 0.7978845608028654 * (1.0 + 3.0 * 0.044715 * x * x)


def _silu(x):
    return x * jax.nn.sigmoid(x)


def _softplus(x):
    return jnp.maximum(x, 0.0) + jnp.log1p(jnp.exp(-jnp.abs(x)))


def _dg(a, b, ca, cb):
    return lax.dot_general(a.astype(BF16), b.astype(BF16), (((ca,), (cb,)), ((), ())),
                           preferred_element_type=F32)


def _dg3(a, b, ca, cb):
    a1 = a.astype(BF16)
    a2 = (a - a1.astype(F32)).astype(BF16)
    b1 = b.astype(BF16)
    b2 = (b - b1.astype(F32)).astype(BF16)
    dn = (((ca,), (cb,)), ((), ()))
    d = functools.partial(lax.dot_general, dimension_numbers=dn, preferred_element_type=F32)
    return d(a1, b1) + (d(a1, b2) + d(a2, b1))


def _make_mm(ca, cb, dot):
    @jax.custom_vjp
    def f(a, b):
        return dot(a, b, ca, cb)

    def fwd(a, b):
        return dot(a, b, ca, cb), (a, b)

    def bwd(res, g):
        a, b = res
        if (ca, cb) == (1, 0):
            return dot(g, b, 1, 1), dot(a, g, 0, 0)
        if (ca, cb) == (1, 1):
            return dot(g, b, 1, 0), dot(g, a, 0, 0)
        return dot(b, g, 1, 1), dot(a, g, 1, 0)

    f.defvjp(fwd, bwd)
    return f


mm_nn = _make_mm(1, 0, _dg)


def _bdg(a, b, ca, cb):
    return lax.dot_general(a.astype(BF16), b.astype(BF16), (((ca,), (cb,)), ((0,), (0,))),
                           preferred_element_type=F32)


def _split2(x):
    x1 = x.astype(BF16)
    return x1, (x - x1.astype(F32)).astype(BF16)


def _bdg3(a, b, ca, cb):
    a1, a2 = _split2(a)
    b1, b2 = _split2(b)
    d = functools.partial(lax.dot_general, dimension_numbers=(((ca,), (cb,)), ((0,), (0,))),
                          preferred_element_type=F32)
    return d(a1, b1) + (d(a1, b2) + d(a2, b1))


def _make_bmm(ca, cb):
    @jax.custom_vjp
    def f(a, b):
        return _bdg(a, b, ca, cb)

    def fwd(a, b):
        return _bdg(a, b, ca, cb), (a, b)

    def bwd(res, g):
        a, b = res
        if (ca, cb) == (2, 1):
            return _bdg(g, b, 2, 2), _bdg(a, g, 1, 1)
        if (ca, cb) == (2, 2):
            return _bdg(g, b, 2, 1), _bdg(g, a, 1, 1)
        return _bdg(b, g, 2, 2), _bdg(a, g, 2, 1)

    f.defvjp(fwd, bwd)
    return f


bmm_nn = _make_bmm(2, 1)
bmm_nt = _make_bmm(2, 2)
bmm_tn = _make_bmm(1, 1)


def _iota3(shape, dim):
    return lax.broadcasted_iota(jnp.int32, shape, dim)


@jax.custom_vjp
def _tri_inv(aw):
    b, n, _ = aw.shape
    shp = (b, n, 2 * n)
    left = _iota3(shp, 2) < n
    x = jnp.where(_iota3(shp, 1) == _iota3(shp, 2), 1.0, 0.0) - aw
    zero = jnp.zeros((b, n, 4 * n), BF16)
    k = 1
    while k < n:
        x1, x2 = _split2(x)
        lhs = jnp.concatenate([x1, x2], axis=1)
        rhs = jnp.concatenate([zero, jnp.concatenate([x1, x2], axis=2)], axis=1)
        r = lax.dot_general(lhs, rhs, (((2,), (1,)), ((0,), (0,))), preferred_element_type=F32)
        mx = (r[:, :n, :2 * n] + r[:, :n, 2 * n:]) + (r[:, n:, :2 * n] + r[:, n:, 2 * n:])
        x = jnp.where(left, x, 0.0) + mx
        k *= 2
    return jnp.where(left, x, 0.0)


def _tri_inv_fwd(aw):
    t = _tri_inv(aw)
    return t, t


def _tri_inv_bwd(t, dt):
    n = t.shape[1]
    x = _bdg3(t, dt, 1, 1)[:, :n]
    tp = jnp.concatenate([jnp.zeros_like(t), t], axis=1)
    return (-_bdg3(x, tp, 2, 2),)


_tri_inv.defvjp(_tri_inv_fwd, _tri_inv_bwd)


def _rows(shape):
    return lax.broadcasted_iota(jnp.int32, shape, 0)


def _lanes(shape):
    return lax.broadcasted_iota(jnp.int32, shape, 1)


def _col(x, j):
    return jnp.sum(jnp.where(_lanes(x.shape) == j, x, 0.0), axis=1, keepdims=True)


def _put_col(col, j, width):
    shape = (col.shape[0], width)
    return jnp.where(_lanes(shape) == j, jnp.broadcast_to(col, shape), 0.0)


VMEM_BYTES_V7X = 64 << 20


def _params(*sem, vmem_bytes=None):
    return pltpu.CompilerParams(dimension_semantics=sem, vmem_limit_bytes=vmem_bytes)


def matmul(a, b, *, ta=False, tb=False, tm, tn, tk, out_dtype=F32, col_groups=1, dep=None, name):
    ba, bb = a.shape[:-2], b.shape[:-2]
    batch = ba if len(ba) >= len(bb) else bb
    assert ba in ((), batch) and bb in ((), batch)
    nb = len(batch)
    (m, k) = (a.shape[-1], a.shape[-2]) if ta else (a.shape[-2], a.shape[-1])
    (k2, n) = (b.shape[-1], b.shape[-2]) if tb else (b.shape[-2], b.shape[-1])
    tm, tn, tk = min(tm, m), min(tn, n), min(tk, k)
    assert k == k2 and m % tm == 0 and n % tn == 0 and k % tk == 0, (a.shape, b.shape, tm, tn, tk)
    nk = k // tk
    grid = batch + (m // tm, n // tn, nk)

    def a_map(*g):
        i, kk = g[nb], g[nb + 2]
        return (g[:nb] if ba else ()) + ((kk, i) if ta else (i, kk))

    def b_map(*g):
        j, kk = g[nb + 1], g[nb + 2]
        return (g[:nb] if bb else ()) + ((j, kk) if tb else (kk, j))

    per_group = n // tn // col_groups
    assert col_groups == 1 or (nb == 0 and per_group * col_groups * tn == n)

    def o_map(*g):
        if col_groups > 1:
            return (g[1] // per_group, g[0], g[1] % per_group)
        return g[:nb] + (g[nb], g[nb + 1])

    a_blk = (None,) * len(ba) + ((tk, tm) if ta else (tm, tk))
    b_blk = (None,) * len(bb) + ((tn, tk) if tb else (tk, tn))
    o_blk = (None,) * (nb + (col_groups > 1)) + (tm, tn)
    o_shape = (col_groups, m, n // col_groups) if col_groups > 1 else batch + (m, n)

    dep_specs, dep_args = _dep_operand(dep)

    def body(a_ref, b_ref, *rest):
        o_ref, acc_ref = rest[len(dep_args):]
        kk = pl.program_id(nb + 2)

        @pl.when(kk == 0)
        def _():
            acc_ref[...] = jnp.zeros_like(acc_ref)

        acc_ref[...] += _dg(a_ref[...], b_ref[...], 0 if ta else 1, 1 if tb else 0)

        @pl.when(kk == nk - 1)
        def _():
            o_ref[...] = acc_ref[...].astype(o_ref.dtype)

    return pl.pallas_call(
        body, name=name, grid=grid,
        in_specs=[pl.BlockSpec(a_blk, a_map), pl.BlockSpec(b_blk, b_map)] + dep_specs,
        out_specs=pl.BlockSpec(o_blk, o_map),
        out_shape=jax.ShapeDtypeStruct(o_shape, out_dtype),
        scratch_shapes=[pltpu.VMEM((tm, tn), F32)],
        compiler_params=_params(*(("parallel",) * (nb + 2) + ("arbitrary",))),
    )(a, b, *dep_args)


def norm_fwd(x, g, *, tb, name):
    s, d = x.shape

    def body(x_ref, g_ref, h_ref):
        xv = x_ref[...]
        r = lax.rsqrt(jnp.mean(xv * xv, axis=-1, keepdims=True) + EPS)
        h_ref[...] = (xv * r * g_ref[...]).astype(BF16)

    return pl.pallas_call(
        body, name=name, grid=(s // tb,),
        in_specs=[pl.BlockSpec((tb, d), lambda i: (i, 0)), pl.BlockSpec((1, d), lambda i: (0, 0))],
        out_specs=pl.BlockSpec((tb, d), lambda i: (i, 0)),
        out_shape=jax.ShapeDtypeStruct((s, d), BF16),
        compiler_params=_params("parallel"),
    )(x, g)


def _rms_bwd(xv, gv, dh):
    r = lax.rsqrt(jnp.mean(xv * xv, axis=-1, keepdims=True) + EPS)
    xh = xv * r
    dxh = dh * gv
    dx = r * (dxh - xh * jnp.mean(dxh * xh, axis=-1, keepdims=True))
    return dx, jnp.sum(dh * xh, axis=0, keepdims=True)


def norm_bwd(x, g, dh, dres, *, tb, dep=None, name):
    s, d = x.shape
    dep_specs, dep_args = _dep_operand(dep)

    def body(x_ref, g_ref, dh_ref, dres_ref, *rest):
        dx_ref, dg_ref = rest[len(dep_args):]
        dx, dg = _rms_bwd(x_ref[...], g_ref[...], dh_ref[...])
        dx_ref[...] = dres_ref[...] + dx

        @pl.when(pl.program_id(0) == 0)
        def _():
            dg_ref[...] = jnp.zeros_like(dg_ref)

        dg_ref[...] += dg

    row = pl.BlockSpec((tb, d), lambda i: (i, 0))
    one = pl.BlockSpec((1, d), lambda i: (0, 0))
    return pl.pallas_call(
        body, name=name, grid=(s // tb,),
        in_specs=[row, one, row, row] + dep_specs, out_specs=[row, one],
        out_shape=[jax.ShapeDtypeStruct((s, d), F32), jax.ShapeDtypeStruct((1, d), F32)],
        compiler_params=_params("arbitrary"),
    )(x, g, dh, dres, *dep_args)


def _dep_operand(dep):
    return ([], []) if dep is None else ([_ANY], [dep])


def ffn_fwd(x, g, wg, wu, wd, layer, *, tb, dep=None, name):
    s, d = x.shape
    fc = wd.shape[-2]
    dep_specs, dep_args = _dep_operand(dep)

    def body(x_ref, g_ref, wg_ref, wu_ref, wd_ref, *rest):
        o_ref, h_scr, acc_scr = rest[len(dep_args):]
        j = pl.program_id(1)

        @pl.when(j == 0)
        def _():
            xv = x_ref[...]
            r = lax.rsqrt(jnp.mean(xv * xv, axis=-1, keepdims=True) + EPS)
            h_scr[...] = (xv * r * g_ref[...]).astype(BF16)
            acc_scr[...] = jnp.zeros_like(acc_scr)

        h = h_scr[...]
        a = _dg(h, wg_ref[...], 1, 0)
        b = _dg(h, wu_ref[...], 1, 0)
        acc_scr[...] += _dg(_silu(a) * b, wd_ref[...], 1, 0)

        @pl.when(j == N_SHARD - 1)
        def _():
            o_ref[...] = x_ref[...] + 0.5 * acc_scr[...]

    row = pl.BlockSpec((tb, d), lambda i, j: (i, 0))
    lg, lu, ld = layer
    return pl.pallas_call(
        body, name=name, grid=(s // tb, N_SHARD),
        in_specs=[row, pl.BlockSpec((1, d), lambda i, j: (0, 0)),
                  pl.BlockSpec((None, None, d, fc), lambda i, j: (j, lg, 0, 0)),
                  pl.BlockSpec((None, None, d, fc), lambda i, j: (j, lu, 0, 0)),
                  pl.BlockSpec((None, None, fc, d), lambda i, j: (j, ld, 0, 0))] + dep_specs,
        out_specs=row,
        out_shape=jax.ShapeDtypeStruct((s, d), F32),
        scratch_shapes=[pltpu.VMEM((tb, d), BF16), pltpu.VMEM((tb, d), F32)],
        compiler_params=_params("parallel", "arbitrary"),
    )(x, g, wg, wu, wd, *dep_args)


def ffn_bwd(x, dxo, g, wg, wu, wd, layer, *, tb, dep=None, name):
    s, d = x.shape
    fc = wd.shape[-2]
    dep_specs, dep_args = _dep_operand(dep)

    def body(x_ref, dxo_ref, g_ref, wg_ref, wu_ref, wd_ref, *rest):
        dx_ref, dg_ref, dab_ref, hid_ref, h_ref, dy_ref, dh_scr = rest[len(dep_args):]
        i, j = pl.program_id(0), pl.program_id(1)

        @pl.when(j == 0)
        def _():
            xv = x_ref[...]
            r = lax.rsqrt(jnp.mean(xv * xv, axis=-1, keepdims=True) + EPS)
            h_ref[...] = (xv * r * g_ref[...]).astype(BF16)
            dy_ref[...] = (0.5 * dxo_ref[...]).astype(BF16)
            dh_scr[...] = jnp.zeros_like(dh_scr)

        h = h_ref[...]
        a = _dg(h, wg_ref[...], 1, 0)
        b = _dg(h, wu_ref[...], 1, 0)
        sg = jax.nn.sigmoid(a)
        sl = a * sg
        dhid = _dg(dy_ref[...], wd_ref[...], 1, 1)
        da = (dhid * b * (sg * (1.0 + a * (1.0 - sg)))).astype(BF16)
        db = (dhid * sl).astype(BF16)
        dab_ref[0] = da
        dab_ref[1] = db
        hid_ref[...] = (sl * b).astype(BF16)
        dh_scr[...] += _dg(da, wg_ref[...], 1, 1) + _dg(db, wu_ref[...], 1, 1)

        @pl.when(jnp.logical_and(i == 0, j == 0))
        def _():
            dg_ref[...] = jnp.zeros_like(dg_ref)

        @pl.when(j == N_SHARD - 1)
        def _():
            dx, dg = _rms_bwd(x_ref[...], g_ref[...], dh_scr[...])
            dx_ref[...] = dxo_ref[...] + dx
            dg_ref[...] += dg

    row = pl.BlockSpec((tb, d), lambda i, j: (i, 0))
    one = pl.BlockSpec((1, d), lambda i, j: (0, 0))
    lg, lu, ld = layer
    return pl.pallas_call(
        body, name=name, grid=(s // tb, N_SHARD),
        in_specs=[row, row, one,
                  pl.BlockSpec((None, None, d, fc), lambda i, j: (j, lg, 0, 0)),
                  pl.BlockSpec((None, None, d, fc), lambda i, j: (j, lu, 0, 0)),
                  pl.BlockSpec((None, None, fc, d), lambda i, j: (j, ld, 0, 0))] + dep_specs,
        out_specs=[row, one,
                   pl.BlockSpec((None, 2, tb, fc), lambda i, j: (j, 0, i, 0)),
                   pl.BlockSpec((None, tb, fc), lambda i, j: (j, i, 0)),
                   row, row],
        out_shape=[jax.ShapeDtypeStruct((s, d), F32), jax.ShapeDtypeStruct((1, d), F32),
                   jax.ShapeDtypeStruct((N_SHARD, 2, s, fc), BF16),
                   jax.ShapeDtypeStruct((N_SHARD, s, fc), BF16),
                   jax.ShapeDtypeStruct((s, d), BF16), jax.ShapeDtypeStruct((s, d), BF16)],
        scratch_shapes=[pltpu.VMEM((tb, d), F32)],
        compiler_params=_params("arbitrary", "arbitrary",
                                vmem_bytes=min(VMEM_BYTES_V7X - (8 << 20),
                                               12 * d * fc + 44 * tb * d + 40 * tb * fc + (4 << 20))),
    )(x, dxo, g, wg, wu, wd, *dep_args)


def _pblk(tb, width, col0, rev_n=None):
    assert col0 % width == 0
    cb = col0 // width
    if rev_n is None:
        return pl.BlockSpec((tb, width), lambda i: (i, cb))
    return pl.BlockSpec((tb, width), lambda i: (rev_n - 1 - i, cb))


def _full(shape):
    return pl.BlockSpec(shape, lambda i: (0,) * len(shape))


def conv_fwd(src, col0, width, w, bias, *, tb, name):
    s = src.shape[0]
    cb = col0 // width

    def body(x_ref, p_ref, w_ref, b_ref, y_ref):
        i = pl.program_id(0)
        prev = jnp.where(i > 0, p_ref[...], 0.0)
        xe = jnp.concatenate([prev, x_ref[...]], axis=0)
        wv = w_ref[...]
        acc = b_ref[...] + wv[3:4] * xe[8:]
        for k in range(3):
            acc = acc + wv[k:k + 1] * pltpu.roll(xe, 3 - k, 0)[8:]
        y_ref[...] = acc

    return pl.pallas_call(
        body, name=name, grid=(s // tb,),
        in_specs=[_pblk(tb, width, col0),
                  pl.BlockSpec((8, width), lambda i: (jnp.maximum(i * (tb // 8) - 1, 0), cb)),
                  _full((4, width)), _full((1, width))],
        out_specs=pl.BlockSpec((tb, width), lambda i: (i, 0)),
        out_shape=jax.ShapeDtypeStruct((s, width), F32),
        compiler_params=_params("parallel"),
    )(src, src, w, bias)


def conv_bwd(src, col0, width, dpre, w, *, tb, name):
    s = src.shape[0]
    cb = col0 // width
    nt = s // tb

    def body(x_ref, p_ref, d_ref, n_ref, w_ref, dx_ref, dwb_ref):
        i = pl.program_id(0)
        d = d_ref[...]
        nxt = jnp.where(i < nt - 1, n_ref[...], 0.0)
        de = jnp.concatenate([d, nxt], axis=0)
        wv = w_ref[...]
        dx = wv[3:4] * d
        for k in range(3):
            dx = dx + wv[k:k + 1] * pltpu.roll(de, tb + 8 - (3 - k), 0)[:tb]
        dx_ref[...] = dx.astype(BF16)

        @pl.when(i == 0)
        def _():
            dwb_ref[...] = jnp.zeros_like(dwb_ref)

        prev = jnp.where(i > 0, p_ref[...], 0.0)
        xe = jnp.concatenate([prev, x_ref[...]], axis=0)
        dwb_ref[3:4, :] += jnp.sum(d * xe[8:], axis=0, keepdims=True)
        for k in range(3):
            dwb_ref[k:k + 1, :] += jnp.sum(d * pltpu.roll(xe, 3 - k, 0)[8:], axis=0, keepdims=True)
        dwb_ref[4:5, :] += jnp.sum(d, axis=0, keepdims=True)

    return pl.pallas_call(
        body, name=name, grid=(nt,),
        in_specs=[_pblk(tb, width, col0),
                  pl.BlockSpec((8, width), lambda i: (jnp.maximum(i * (tb // 8) - 1, 0), cb)),
                  pl.BlockSpec((tb, width), lambda i: (i, 0)),
                  pl.BlockSpec((8, width), lambda i: (jnp.minimum((i + 1) * (tb // 8), s // 8 - 1), 0)),
                  _full((4, width))],
        out_specs=[pl.BlockSpec((tb, width), lambda i: (i, 0)), _full((8, width))],
        out_shape=[jax.ShapeDtypeStruct((s, width), BF16), jax.ShapeDtypeStruct((8, width), F32)],
        compiler_params=_params("arbitrary"),
    )(src, src, dpre, dpre, w)


SGU_BLOCK = 128


def _sgu_pre(u, v, lg, lb):
    ug = _gelu(u)
    vg = _gelu(v)
    mu = jnp.mean(vg, axis=-1, keepdims=True)
    var = jnp.mean(jnp.square(vg - mu), axis=-1, keepdims=True)
    return ug, (vg - mu) * lax.rsqrt(var + EPS) * lg + lb


def _sgu_mask():
    shp = (SGU_BLOCK, SGU_BLOCK)
    return (_rows(shp) // CHUNK) >= (_lanes(shp) // CHUNK)


def sgu_fwd(proj, lg, lb, ws, bt, *, tb, name):
    s = proj.shape[0]
    w = BRANCH_W

    def body(u_ref, v_ref, lg_ref, lb_ref, w_ref, b_ref, y_ref):
        ug, vn = _sgu_pre(u_ref[...], v_ref[...], lg_ref[...], lb_ref[...])
        mask = _sgu_mask()
        for g in range(4):
            wm = jnp.where(mask, w_ref[g], 0.0)
            bcol = _col(b_ref[...], g)
            cs = slice(128 * g, 128 * g + 128)
            for n in range(tb // SGU_BLOCK):
                rs = slice(SGU_BLOCK * n, SGU_BLOCK * (n + 1))
                mixed = _dg(wm, vn[rs, cs], 1, 0) + bcol
                y_ref[rs, cs] = (ug[rs, cs] * mixed).astype(BF16)

    return pl.pallas_call(
        body, name=name, grid=(s // tb,),
        in_specs=[_pblk(tb, w, C_AU), _pblk(tb, w, C_AV), _full((1, w)), _full((1, w)),
                  _full((4, 128, 128)), _full((128, 128))],
        out_specs=pl.BlockSpec((tb, w), lambda i: (i, 0)),
        out_shape=jax.ShapeDtypeStruct((s, w), BF16),
        compiler_params=_params("parallel"),
    )(proj, proj, lg, lb, ws, bt)


def sgu_bwd(proj, dy, lg, lb, ws, bt, *, tb, dep=None, name):
    s = proj.shape[0]
    w = BRANCH_W
    dep_specs, dep_args = _dep_operand(dep)

    def body(u_ref, v_ref, dy_ref, lg_ref, lb_ref, w_ref, b_ref, *rest):
        du_ref, dv_ref, dlg_ref, dlb_ref, dw_ref, db_ref, dug_scr, dvn_scr = rest[len(dep_args):]

        @pl.when(pl.program_id(0) == 0)
        def _():
            dlg_ref[...] = jnp.zeros_like(dlg_ref)
            dlb_ref[...] = jnp.zeros_like(dlb_ref)
            dw_ref[...] = jnp.zeros_like(dw_ref)
            db_ref[...] = jnp.zeros_like(db_ref)

        (ug, vn), vf = jax.vjp(_sgu_pre, u_ref[...], v_ref[...], lg_ref[...], lb_ref[...])
        dyv = dy_ref[...]
        mask = _sgu_mask()
        for g in range(4):
            wm = jnp.where(mask, w_ref[g], 0.0)
            bcol = _col(b_ref[...], g)
            cs = slice(128 * g, 128 * g + 128)
            dwg = jnp.zeros((SGU_BLOCK, SGU_BLOCK), F32)
            dbc = jnp.zeros((SGU_BLOCK, 1), F32)
            for n in range(tb // SGU_BLOCK):
                rs = slice(SGU_BLOCK * n, SGU_BLOCK * (n + 1))
                blk = vn[rs, cs]
                mixed = _dg(wm, blk, 1, 0) + bcol
                dmix = dyv[rs, cs] * ug[rs, cs]
                dug_scr[rs, cs] = dyv[rs, cs] * mixed
                dvn_scr[rs, cs] = _dg(wm, dmix, 0, 0)
                dwg = dwg + _dg(dmix, blk, 1, 1)
                dbc = dbc + jnp.sum(dmix, axis=1, keepdims=True)
            dw_ref[g] += jnp.where(mask, dwg, 0.0)
            db_ref[...] += _put_col(dbc, g, 128)
        du, dv, dlg, dlb = vf((dug_scr[...], dvn_scr[...]))
        du_ref[...] = du.astype(BF16)
        dv_ref[...] = dv.astype(BF16)
        dlg_ref[...] += dlg
        dlb_ref[...] += dlb

    row = pl.BlockSpec((tb, w), lambda i: (i, 0))
    return pl.pallas_call(
        body, name=name, grid=(s // tb,),
        in_specs=[_pblk(tb, w, C_AU), _pblk(tb, w, C_AV), row, _full((1, w)), _full((1, w)),
                  _full((4, 128, 128)), _full((128, 128))] + dep_specs,
        out_specs=[row, row, _full((1, w)), _full((1, w)), _full((4, 128, 128)), _full((128, 128))],
        out_shape=[jax.ShapeDtypeStruct((s, w), BF16), jax.ShapeDtypeStruct((s, w), BF16),
                   jax.ShapeDtypeStruct((1, w), F32), jax.ShapeDtypeStruct((1, w), F32),
                   jax.ShapeDtypeStruct((4, 128, 128), F32), jax.ShapeDtypeStruct((128, 128), F32)],
        scratch_shapes=[pltpu.VMEM((tb, w), F32), pltpu.VMEM((tb, w), F32)],
        compiler_params=_params("arbitrary"),
    )(proj, proj, dy, lg, lb, ws, bt, *dep_args)


POOL_HALO = 16


def _pooled(xe, t0, tb):
    s2 = xe + pltpu.roll(xe, 1, 0)
    s4 = s2 + pltpu.roll(s2, 2, 0)
    s8 = s4 + pltpu.roll(s4, 4, 0)
    s16 = s8 + pltpu.roll(s8, 8, 0)
    t = (t0 + _rows((tb, 128))).astype(F32)
    outs = []
    for g, sw in enumerate((s2, s4, s8, s16)):
        cs = slice(128 * g, 128 * g + 128)
        cnt = jnp.minimum(t + 1.0, float(2 ** (g + 1)))
        outs.append(sw[POOL_HALO:, cs] / cnt - xe[POOL_HALO:, cs])
    return outs


def pool_fwd(proj, wp, scale, *, tb, name):
    s = proj.shape[0]
    w = BRANCH_W

    def body(x_ref, p_ref, w_ref, sc_ref, y_ref):
        i = pl.program_id(0)
        prev = jnp.where(i > 0, p_ref[...], 0.0)
        xe = jnp.concatenate([prev, x_ref[...]], axis=0)
        pooled = _pooled(xe, i * tb, tb)
        sc = sc_ref[...]
        for g in range(4):
            cs = slice(128 * g, 128 * g + 128)
            y_ref[:, cs] = (_dg(pooled[g], w_ref[g], 1, 0) * sc[:, cs]).astype(BF16)

    cb = C_DX // w
    return pl.pallas_call(
        body, name=name, grid=(s // tb,),
        in_specs=[_pblk(tb, w, C_DX),
                  pl.BlockSpec((POOL_HALO, w), lambda i: (jnp.maximum(i * (tb // POOL_HALO) - 1, 0), cb)),
                  _full((4, 128, 128)), _full((1, w))],
        out_specs=pl.BlockSpec((tb, w), lambda i: (i, 0)),
        out_shape=jax.ShapeDtypeStruct((s, w), BF16),
        compiler_params=_params("parallel"),
    )(proj, proj, wp, scale)


def pool_bwd(proj, dy, wp, scale, *, tb, name):
    s = proj.shape[0]
    w = BRANCH_W
    nt = s // tb
    te = tb + POOL_HALO

    def body(x_ref, p_ref, dy_ref, n_ref, w_ref, sc_ref, dx_ref, dw_ref, dsc_ref):
        i = pl.program_id(0)

        @pl.when(i == 0)
        def _():
            dw_ref[...] = jnp.zeros_like(dw_ref)
            dsc_ref[...] = jnp.zeros_like(dsc_ref)

        prev = jnp.where(i > 0, p_ref[...], 0.0)
        xe = jnp.concatenate([prev, x_ref[...]], axis=0)
        pooled = _pooled(xe, i * tb, tb)
        dyv = dy_ref[...]
        nxt = jnp.where(i < nt - 1, n_ref[...], 0.0)
        dye = jnp.concatenate([dyv, nxt], axis=0)
        sc = sc_ref[...]
        t = (i * tb + _rows((te, 128))).astype(F32)
        for g in range(4):
            cs = slice(128 * g, 128 * g + 128)
            win = 2 ** (g + 1)
            dpm = dye[:, cs] * sc[:, cs]
            dpool = _dg(dpm, w_ref[g], 1, 1)
            q = dpool / jnp.minimum(t + 1.0, float(win))
            r, sh = q, 1
            while sh < win:
                r = r + pltpu.roll(r, te - sh, 0)
                sh *= 2
            dx_ref[:, cs] = (r[:tb] - dpool[:tb]).astype(BF16)
            dw_ref[g] += _dg(pooled[g], dpm[:tb], 0, 0)
            dsc_ref[:, cs] += jnp.sum(dyv[:, cs] * _dg(pooled[g], w_ref[g], 1, 0), axis=0, keepdims=True)

    cb = C_DX // w
    row = pl.BlockSpec((tb, w), lambda i: (i, 0))
    return pl.pallas_call(
        body, name=name, grid=(nt,),
        in_specs=[_pblk(tb, w, C_DX),
                  pl.BlockSpec((POOL_HALO, w), lambda i: (jnp.maximum(i * (tb // POOL_HALO) - 1, 0), cb)),
                  row,
                  pl.BlockSpec((POOL_HALO, w),
                               lambda i: (jnp.minimum((i + 1) * (tb // POOL_HALO), s // POOL_HALO - 1), 0)),
                  _full((4, 128, 128)), _full((1, w))],
        out_specs=[row, _full((4, 128, 128)), _full((1, w))],
        out_shape=[jax.ShapeDtypeStruct((s, w), BF16), jax.ShapeDtypeStruct((4, 128, 128), F32),
                   jax.ShapeDtypeStruct((1, w), F32)],
        compiler_params=_params("arbitrary"),
    )(proj, proj, dy, dy, wp, scale)


def _neg_expm1(z):
    p = 1.0 + z * (1.0 / 9.0)
    for n in (8.0, 7.0, 6.0, 5.0, 4.0, 3.0, 2.0):
        p = 1.0 + z * (1.0 / n) * p
    return jnp.where(z > -0.5, -z * p, 1.0 - jnp.exp(z))


def _lru_gates(xc, wa, wx, ba, bx, lam):
    r = jax.nn.sigmoid(mm_nn(xc, wa) + ba)
    i = jax.nn.sigmoid(mm_nn(xc, wx) + bx)
    log_a = -LRU_C * r * _softplus(-lam)
    a = jnp.exp(log_a)
    return a, jnp.sqrt(_neg_expm1(2.0 * log_a)) * (i * xc)


def _scan_down(a, b):
    n = a.shape[0]
    row = _rows(a.shape)
    sh = 1
    while sh < n:
        keep = row >= sh
        a_sh = jnp.where(keep, pltpu.roll(a, sh, 0), 1.0)
        b_sh = jnp.where(keep, pltpu.roll(b, sh, 0), 0.0)
        b = a * b_sh + b
        a = a * a_sh
        sh *= 2
    return a, b


def _scan_up(a, b):
    n = a.shape[0]
    row = _rows(a.shape)
    sh = 1
    while sh < n:
        keep = row < n - sh
        a_sh = jnp.where(keep, pltpu.roll(a, n - sh, 0), 1.0)
        b_sh = jnp.where(keep, pltpu.roll(b, n - sh, 0), 0.0)
        b = a * b_sh + b
        a = a * a_sh
        sh *= 2
    return a, b


def lru_fwd(xc, proj, wa, wx, ba, bx, lam, *, tb, name):
    s = xc.shape[0]
    w = BRANCH_W

    def body(xc_ref, gate_ref, wa_ref, wx_ref, ba_ref, bx_ref, lam_ref, y_ref, h_ref, carry):
        @pl.when(pl.program_id(0) == 0)
        def _():
            carry[...] = jnp.zeros_like(carry)

        a, b = _lru_gates(xc_ref[...], wa_ref[...], wx_ref[...], ba_ref[...], bx_ref[...], lam_ref[...])
        pa, hb = _scan_down(a, b)
        h = pa * carry[...] + hb
        h_ref[...] = h
        carry[...] = h_ref[tb - 1:tb, :]
        y_ref[...] = (h * _gelu(gate_ref[...])).astype(BF16)

    row = pl.BlockSpec((tb, w), lambda i: (i, 0))
    return pl.pallas_call(
        body, name=name, grid=(s // tb,),
        in_specs=[row, _pblk(tb, w, C_BG), _full((w, w)), _full((w, w)),
                  _full((1, w)), _full((1, w)), _full((1, w))],
        out_specs=[row, row],
        out_shape=[jax.ShapeDtypeStruct((s, w), BF16), jax.ShapeDtypeStruct((s, w), F32)],
        scratch_shapes=[pltpu.VMEM((1, w), F32)],
        compiler_params=_params("arbitrary"),
    )(xc, proj, wa, wx, ba, bx, lam)


def lru_bwd(xc, proj, h, dy, wa, wx, ba, bx, lam, *, tb, name):
    s = xc.shape[0]
    w = BRANCH_W
    nt = s // tb

    def body(xc_ref, gate_ref, h_ref, hp_ref, dy_ref, wa_ref, wx_ref, ba_ref, bx_ref, lam_ref,
             dxc_ref, dgate_ref, dwa_ref, dwx_ref, dba_ref, dbx_ref, dlam_ref, carry):
        i = pl.program_id(0)

        @pl.when(i == 0)
        def _():
            carry[...] = jnp.zeros_like(carry)
            for r in (dwa_ref, dwx_ref, dba_ref, dbx_ref, dlam_ref):
                r[...] = jnp.zeros_like(r)

        (a, b), vf = jax.vjp(_lru_gates, xc_ref[...], wa_ref[...], wx_ref[...],
                             ba_ref[...], bx_ref[...], lam_ref[...])
        gate = gate_ref[...]
        dyv = dy_ref[...]
        hv = h_ref[...]
        row = _rows((tb, w))
        a_next = jnp.where(row < tb - 1, pltpu.roll(a, tb - 1, 0), 1.0)
        pa, xb = _scan_up(a_next, dyv * _gelu(gate))
        dh = xb + pa * carry[...]
        carry[...] = a[0:1, :] * dh[0:1, :]
        h_first = jnp.where(i < nt - 1, hp_ref[7:8, :], 0.0)
        h_prev = jnp.where(row >= 1, pltpu.roll(hv, 1, 0), h_first)
        dxc, dwa, dwx, dba, dbx, dlam = vf((dh * h_prev, dh))
        dxc_ref[...] = dxc
        dgate_ref[...] = (dyv * hv * _gelu_grad(gate)).astype(BF16)
        dwa_ref[...] += dwa
        dwx_ref[...] += dwx
        dba_ref[...] += dba
        dbx_ref[...] += dbx
        dlam_ref[...] += dlam

    row_spec = pl.BlockSpec((tb, w), lambda i: (nt - 1 - i, 0))
    hp_spec = pl.BlockSpec((8, w), lambda i: (jnp.maximum((nt - 1 - i) * (tb // 8) - 1, 0), 0))
    return pl.pallas_call(
        body, name=name, grid=(nt,),
        in_specs=[row_spec, _pblk(tb, w, C_BG, rev_n=nt), row_spec, hp_spec, row_spec,
                  _full((w, w)), _full((w, w)), _full((1, w)), _full((1, w)), _full((1, w))],
        out_specs=[row_spec, row_spec, _full((w, w)), _full((w, w)),
                   _full((1, w)), _full((1, w)), _full((1, w))],
        out_shape=[jax.ShapeDtypeStruct((s, w), F32), jax.ShapeDtypeStruct((s, w), BF16),
                   jax.ShapeDtypeStruct((w, w), F32), jax.ShapeDtypeStruct((w, w), F32),
                   jax.ShapeDtypeStruct((1, w), F32), jax.ShapeDtypeStruct((1, w), F32),
                   jax.ShapeDtypeStruct((1, w), F32)],
        scratch_shapes=[pltpu.VMEM((1, w), F32)],
        compiler_params=_params("arbitrary"),
    )(xc, proj, h, h, dy, wa, wx, ba, bx, lam)


def _gdn_g(ba, alog, dt):
    return -jnp.exp(alog) * _softplus(ba + dt)


def _gdn_local(qp, kp, vp, bpre, gc):
    b, c, dk = qp.shape
    q = _silu(qp)
    k = _silu(kp)
    v = _silu(vp)
    q = q * lax.rsqrt(jnp.sum(q * q, axis=-1, keepdims=True) + EPS) * (dk ** -0.5)
    k = k * lax.rsqrt(jnp.sum(k * k, axis=-1, keepdims=True) + EPS)
    beta = jax.nn.sigmoid(bpre)
    sq = (b, c, c)
    ii, jj = _iota3(sq, 1), _iota3(sq, 2)
    gr = jnp.sum(gc * (ii == jj).astype(F32), axis=1, keepdims=True)
    decay = jnp.exp(jnp.where(ii >= jj, gc - gr, -jnp.inf))
    wd = (b, c, 2 * c)
    wi, wj = _iota3(wd, 1), _iota3(wd, 2) - c
    gr_w = jnp.sum(gc * (wi == wj).astype(F32), axis=1, keepdims=True)
    decay_w = jnp.exp(jnp.where(jnp.logical_and(wj >= 0, wi > wj), gc - gr_w, -jnp.inf))
    kb = k * beta
    pad = jnp.zeros((b, c, dk), F32)
    t = _tri_inv(bmm_nt(kb, jnp.concatenate([pad, k], axis=1)) * decay_w)
    u = bmm_nn(t, jnp.concatenate([v * beta, pad], axis=1))
    wk = bmm_nn(t, jnp.concatenate([kb * jnp.exp(gc), pad], axis=1))
    attn = bmm_nt(q, k) * decay
    g_last = jnp.sum(jnp.where(_iota3((b, c, 1), 1) == c - 1, gc, 0.0), axis=1, keepdims=True)
    return u, wk, attn, q * jnp.exp(gc), k * jnp.exp(g_last - gc), jnp.exp(g_last)


def _gdn_recur(s, u, wk, attn, qg, kd, eg, zz, ng):
    v_new = u - bmm_nn(wk, s)
    o = bmm_nn(qg, s) + bmm_nn(attn, v_new)
    s_out = s * eg + bmm_tn(kd, v_new)
    o = o * lax.rsqrt(jnp.mean(o * o, axis=-1, keepdims=True) + EPS) * ng
    return o * _silu(zz), s_out


def _gdn_inputs(qkv_ref, ba, gcums, cps):
    c, hd = CHUNK, GDN_DK
    qs, ks, vs, bs, gs = [], [], [], [], []
    for ci in range(cps):
        rs = slice(c * ci, c * (ci + 1))
        for h in range(GDN_HEADS):
            qs.append(qkv_ref[rs, hd * h:hd * (h + 1)])
            ks.append(qkv_ref[rs, 512 + hd * h:512 + hd * (h + 1)])
            vs.append(qkv_ref[rs, 1024 + hd * h:1024 + hd * (h + 1)])
            bs.append(_col(ba[rs], h))
            gs.append(_col(gcums[ci], 4 + h))
    return tuple(jnp.stack(t) for t in (qs, ks, vs, bs, gs))


def _chunk_cumsum(g, transpose=False):
    c = CHUNK
    tri = (_rows((c, c)) >= _lanes((c, c))).astype(F32)
    return _dg3(tri, g, 0 if transpose else 1, 0)


def gdn_fwd(qkv, proj, alog, dt, ng, *, cps, name):
    s = qkv.shape[0]
    c = CHUNK
    nc = s // c
    hd = GDN_DK
    tb = c * cps

    def body(qkv_ref, z_ref, ba_ref, al_ref, dt_ref, ng_ref, y_ref, sall_ref, s_scr):
        @pl.when(pl.program_id(0) == 0)
        def _():
            s_scr[...] = jnp.zeros_like(s_scr)

        nh = GDN_HEADS
        ba = ba_ref[...]
        g = _gdn_g(ba, al_ref[...], dt_ref[...])
        gcums = [_chunk_cumsum(g[c * ci:c * (ci + 1)]) for ci in range(cps)]
        loc = _gdn_local(*_gdn_inputs(qkv_ref, ba, gcums, cps))
        for ci in range(cps):
            rs = slice(c * ci, c * (ci + 1))
            sl = slice(nh * ci, nh * (ci + 1))
            s_in = s_scr[...]
            sall_ref[ci] = s_in
            zz = jnp.stack([z_ref[rs, hd * h:hd * (h + 1)] for h in range(nh)])
            y, s_out = _gdn_recur(s_in, *(t[sl] for t in loc), zz, ng_ref[...])
            s_scr[...] = s_out
            for h in range(nh):
                y_ref[rs, hd * h:hd * (h + 1)] = y[h].astype(BF16)

    return pl.pallas_call(
        body, name=name, grid=(nc // cps,),
        in_specs=[pl.BlockSpec((tb, 1536), lambda i: (i, 0)), _pblk(tb, 512, C_Z), _pblk(tb, 128, C_BA),
                  _full((1, 128)), _full((1, 128)), _full((1, 128))],
        out_specs=[pl.BlockSpec((tb, 512), lambda i: (i, 0)),
                   pl.BlockSpec((cps, GDN_HEADS, hd, hd), lambda i: (i, 0, 0, 0))],
        out_shape=[jax.ShapeDtypeStruct((s, 512), BF16), jax.ShapeDtypeStruct((nc, GDN_HEADS, hd, hd), F32)],
        scratch_shapes=[pltpu.VMEM((GDN_HEADS, hd, hd), F32)],
        compiler_params=_params("arbitrary"),
    )(qkv, proj, proj, alog, dt, ng)


def gdn_bwd(qkv, proj, sall, dy, alog, dt, ng, *, cps, name):
    s = qkv.shape[0]
    c = CHUNK
    nc = s // c
    hd = GDN_DK
    tb = c * cps
    nt = nc // cps

    def body(qkv_ref, z_ref, ba_ref, sall_ref, dy_ref, al_ref, dt_ref, ng_ref,
             dqkv_ref, dz_ref, dba_ref, dal_ref, ddt_ref, dng_ref, ds_scr):
        @pl.when(pl.program_id(0) == 0)
        def _():
            ds_scr[...] = jnp.zeros_like(ds_scr)
            dal_ref[...] = jnp.zeros_like(dal_ref)
            ddt_ref[...] = jnp.zeros_like(ddt_ref)
            dng_ref[...] = jnp.zeros_like(dng_ref)

        nh = GDN_HEADS
        ba = ba_ref[...]
        g, gvf = jax.vjp(_gdn_g, ba, al_ref[...], dt_ref[...])
        gcums = [_chunk_cumsum(g[c * ci:c * (ci + 1)]) for ci in range(cps)]
        loc, loc_vf = jax.vjp(_gdn_local, *_gdn_inputs(qkv_ref, ba, gcums, cps))
        d_loc = [None] * cps
        dng = jnp.zeros((1, 128), F32)
        for ci in reversed(range(cps)):
            rs = slice(c * ci, c * (ci + 1))
            sl = slice(nh * ci, nh * (ci + 1))
            zz = jnp.stack([z_ref[rs, hd * h:hd * (h + 1)] for h in range(nh)])
            dyy = jnp.stack([dy_ref[rs, hd * h:hd * (h + 1)] for h in range(nh)])
            _, rvf = jax.vjp(_gdn_recur, sall_ref[ci], *(t[sl] for t in loc), zz, ng_ref[...])
            ds, du, dwk, dattn, dqg, dkd, deg, dzz, dn = rvf((dyy, ds_scr[...]))
            ds_scr[...] = ds
            d_loc[ci] = (du, dwk, dattn, dqg, dkd, deg)
            dng = dng + dn
            for h in range(nh):
                dz_ref[rs, hd * h:hd * (h + 1)] = dzz[h].astype(BF16)
        dq, dk, dv, dbp, dgc = loc_vf(tuple(jnp.concatenate([d_loc[ci][j] for ci in range(cps)], axis=0)
                                            for j in range(6)))
        dgs, dbas = [], []
        for ci in range(cps):
            rs = slice(c * ci, c * (ci + 1))
            dgcum = jnp.zeros((c, 128), F32)
            dba = jnp.zeros((c, 128), F32)
            for h in range(nh):
                b = nh * ci + h
                dqkv_ref[rs, hd * h:hd * (h + 1)] = dq[b]
                dqkv_ref[rs, 512 + hd * h:512 + hd * (h + 1)] = dk[b]
                dqkv_ref[rs, 1024 + hd * h:1024 + hd * (h + 1)] = dv[b]
                dgcum = dgcum + _put_col(dgc[b], 4 + h, 128)
                dba = dba + _put_col(dbp[b], h, 128)
            dgs.append(_chunk_cumsum(dgcum, transpose=True))
            dbas.append(dba)
        dba2, dal, ddt = gvf(jnp.concatenate(dgs, axis=0))
        dba_ref[...] = (jnp.concatenate(dbas, axis=0) + dba2).astype(BF16)
        dal_ref[...] += dal
        ddt_ref[...] += ddt
        dng_ref[...] += dng

    def rev(width):
        return pl.BlockSpec((tb, width), lambda i: (nt - 1 - i, 0))

    return pl.pallas_call(
        body, name=name, grid=(nt,),
        in_specs=[rev(1536), _pblk(tb, 512, C_Z, rev_n=nt), _pblk(tb, 128, C_BA, rev_n=nt),
                  pl.BlockSpec((cps, GDN_HEADS, hd, hd), lambda i: (nt - 1 - i, 0, 0, 0)), rev(512),
                  _full((1, 128)), _full((1, 128)), _full((1, 128))],
        out_specs=[rev(1536), rev(512), rev(128), _full((1, 128)), _full((1, 128)), _full((1, 128))],
        out_shape=[jax.ShapeDtypeStruct((s, 1536), F32), jax.ShapeDtypeStruct((s, 512), BF16),
                   jax.ShapeDtypeStruct((s, 128), BF16), jax.ShapeDtypeStruct((1, 128), F32),
                   jax.ShapeDtypeStruct((1, 128), F32), jax.ShapeDtypeStruct((1, 128), F32)],
        scratch_shapes=[pltpu.VMEM((GDN_HEADS, hd, hd), F32)],
        compiler_params=_params("arbitrary"),
    )(qkv, proj, proj, sall, dy, alog, dt, ng)


def merge_fwd(x, ys, proj, wb, wo, *, tb, dep=None, name):
    s, d = x.shape
    dep_specs, dep_args = _dep_operand(dep)

    def body(x_ref, ya, yb, yc, yd, g0, g1, g2, g3, wb_ref, wo_ref, *rest):
        o_ref, m_ref = rest[len(dep_args):]
        m = jnp.zeros((tb, d), F32)
        for g, (y_ref, gp_ref) in enumerate(zip((ya, yb, yc, yd), (g0, g1, g2, g3))):
            m = m + jax.nn.sigmoid(gp_ref[...]) * _dg(y_ref[...], wb_ref[g], 1, 0)
        mb = m.astype(BF16)
        m_ref[...] = mb
        o_ref[...] = x_ref[...] + _dg(mb, wo_ref[...], 1, 0)

    row = pl.BlockSpec((tb, d), lambda i: (i, 0))
    yrow = pl.BlockSpec((tb, BRANCH_W), lambda i: (i, 0))
    return pl.pallas_call(
        body, name=name, grid=(s // tb,),
        in_specs=[row] + [yrow] * 4 + [_pblk(tb, d, C_GATE + d * g) for g in range(4)]
        + [_full((4, BRANCH_W, d)), _full((d, d))] + dep_specs,
        out_specs=[row, row],
        out_shape=[jax.ShapeDtypeStruct((s, d), F32), jax.ShapeDtypeStruct((s, d), BF16)],
        compiler_params=_params("parallel"),
    )(x, *ys, proj, proj, proj, proj, wb, wo, *dep_args)


def merge_bwd(dx, ys, proj, wb, wo, *, tb, dep=None, name):
    s, d = dx.shape
    dep_specs, dep_args = _dep_operand(dep)

    def body(dx_ref, ya, yb, yc, yd, g0, g1, g2, g3, wb_ref, wo_ref, *rest):
        dgp_ref, dbr_ref, da, db, dc, dd, dxb_ref = rest[len(dep_args):]
        dxb = dx_ref[...].astype(BF16)
        dxb_ref[...] = dxb
        dm = _dg(dxb, wo_ref[...], 1, 1)
        for g, (y_ref, gp_ref, dy_ref) in enumerate(zip((ya, yb, yc, yd), (g0, g1, g2, g3), (da, db, dc, dd))):
            br = _dg(y_ref[...], wb_ref[g], 1, 0)
            sg = jax.nn.sigmoid(gp_ref[...])
            dgp_ref[:, d * g:d * (g + 1)] = (dm * br * sg * (1.0 - sg)).astype(BF16)
            dbr = (dm * sg).astype(BF16)
            dbr_ref[g] = dbr
            dy_ref[...] = _dg(dbr, wb_ref[g], 1, 1)

    row = pl.BlockSpec((tb, d), lambda i: (i, 0))
    yrow = pl.BlockSpec((tb, BRANCH_W), lambda i: (i, 0))
    return pl.pallas_call(
        body, name=name, grid=(s // tb,),
        in_specs=[row] + [yrow] * 4 + [_pblk(tb, d, C_GATE + d * g) for g in range(4)]
        + [_full((4, BRANCH_W, d)), _full((d, d))] + dep_specs,
        out_specs=[pl.BlockSpec((tb, 4 * d), lambda i: (i, 0)), pl.BlockSpec((4, tb, d), lambda i: (0, i, 0)),
                   yrow, yrow, yrow, yrow, row],
        out_shape=[jax.ShapeDtypeStruct((s, 4 * d), BF16), jax.ShapeDtypeStruct((4, s, d), BF16)]
        + [jax.ShapeDtypeStruct((s, BRANCH_W), F32)] * 4 + [jax.ShapeDtypeStruct((s, d), BF16)],
        compiler_params=_params("parallel"),
    )(dx, *ys, proj, proj, proj, proj, wb, wo, *dep_args)


def loss_head(x, g, target, *, tb, name):
    s, d = x.shape

    def body(x_ref, g_ref, t_ref, dx_ref, dg_ref, loss_ref):
        @pl.when(pl.program_id(0) == 0)
        def _():
            dg_ref[...] = jnp.zeros_like(dg_ref)
            loss_ref[...] = jnp.zeros_like(loss_ref)

        xv, gv = x_ref[...], g_ref[...]
        r = lax.rsqrt(jnp.mean(xv * xv, axis=-1, keepdims=True) + EPS)
        err = xv * r * gv - t_ref[...]
        loss_ref[...] += 0.5 * jnp.sum(jnp.mean(err * err, axis=-1, keepdims=True), axis=0, keepdims=True)
        dx, dg = _rms_bwd(xv, gv, err * (1.0 / d))
        dx_ref[...] = dx
        dg_ref[...] += dg

    row = pl.BlockSpec((tb, d), lambda i: (i, 0))
    return pl.pallas_call(
        body, name=name, grid=(s // tb,),
        in_specs=[row, _full((1, d)), row],
        out_specs=[row, _full((1, d)), _full((1, 128))],
        out_shape=[jax.ShapeDtypeStruct((s, d), F32), jax.ShapeDtypeStruct((1, d), F32),
                   jax.ShapeDtypeStruct((1, 128), F32)],
        compiler_params=_params("arbitrary"),
    )(x, g, target)


def adamw(w, g, m, v, *, name):
    shape = w.shape
    lead, (r, c) = shape[:-2], shape[-2:]
    tr = r
    while tr * c * 4 > (1 << 20) and tr % 16 == 0:
        tr //= 2
    c1 = 1.0 / (1.0 - ADAM_B1 ** ADAM_STEP)
    c2 = 1.0 / (1.0 - ADAM_B2 ** ADAM_STEP)

    def body(w_ref, g_ref, m_ref, v_ref, d_ref, nm_ref, nv_ref):
        gv = g_ref[...]
        nm = ADAM_B1 * m_ref[...] + (1.0 - ADAM_B1) * gv
        nv = ADAM_B2 * v_ref[...] + (1.0 - ADAM_B2) * (gv * gv)
        nm_ref[...] = nm
        nv_ref[...] = nv
        d_ref[...] = -ADAM_LR * ((nm * c1) / (jnp.sqrt(nv * c2) + ADAM_EPS) + ADAM_WD * w_ref[...])

    blk = pl.BlockSpec((None,) * len(lead) + (tr, c), lambda *i: i + (0,))
    grid = lead + (r // tr,)
    if len(lead) == 1 and r < 8:
        tl = max(t for t in range(1, lead[0] + 1) if lead[0] % t == 0 and t * r * c * 4 <= (1 << 20))
        blk = pl.BlockSpec((tl, r, c), lambda i: (i, 0, 0))
        grid = (lead[0] // tl,)
    return tuple(pl.pallas_call(
        body, name=name, grid=grid,
        in_specs=[blk] * 4, out_specs=[blk] * 3,
        out_shape=[jax.ShapeDtypeStruct(shape, F32)] * 3,
        compiler_params=_params(*(("parallel",) * len(grid))),
    )(w, g, m, v))


_ANY = pl.BlockSpec(memory_space=pl.ANY)


def _place():
    x, y, c = lax.axis_index("x"), lax.axis_index("y"), lax.axis_index("c")
    return x, y, c, [(1 - x, y), (x, 1 - y), (1 - x, 1 - y)]


def gather_shards(bufs, *, name):
    n = len(bufs)

    def body(*refs):
        outs = refs[n:2 * n]
        ici_send, ici_recv, d2d_send, d2d_recv = refs[2 * n:]
        x, y, c, chips = _place()
        me = 2 * x + y

        def ici(a, j, slab, to):
            return pltpu.make_async_remote_copy(
                src_ref=outs[a].at[slab, c], dst_ref=outs[a].at[slab, c], send_sem=ici_send.at[a, j],
                recv_sem=ici_recv.at[a, j], device_id=to, device_id_type=MESH)

        def d2d(a, j, slab, half):
            return pltpu.make_async_remote_copy(
                src_ref=outs[a].at[slab, half], dst_ref=outs[a].at[slab, half], send_sem=d2d_send.at[a, j],
                recv_sem=d2d_recv.at[a, j], device_id=(x, y, 1 - c), device_id_type=MESH)

        sends = []
        for a in range(n):
            for j, (px, py) in enumerate(chips):
                cp = ici(a, j, me, (px, py, c))
                cp.start()
                sends.append(cp)
        for a in range(n):
            for j, (px, py) in enumerate(chips):
                ici(a, j, 2 * px + py, (px, py, c)).wait_recv()
                cp = d2d(a, j, 2 * px + py, c)
                cp.start()
                sends.append(cp)
        for a in range(n):
            for j, (px, py) in enumerate(chips):
                d2d(a, j, 2 * px + py, 1 - c).wait_recv()
        for cp in sends:
            cp.wait_send()

    return pl.pallas_call(
        body, name=name,
        in_specs=[_ANY] * n, out_specs=[_ANY] * n,
        out_shape=[jax.ShapeDtypeStruct(t.shape, t.dtype) for t in bufs],
        input_output_aliases={a: a for a in range(n)},
        scratch_shapes=[pltpu.SemaphoreType.DMA((n, 3))] * 4,
    )(*bufs)


_SEM = pl.BlockSpec(memory_space=pltpu.SEMAPHORE)
_VMEM = pl.BlockSpec(memory_space=pltpu.VMEM)
_EFFECT = pltpu.SideEffectType.DATAFLOW_SIDE_EFFECTING
_TOKEN = jax.ShapeDtypeStruct((8, 128), F32)


def _gather_copies(outs, sems_ici, sems_d2d):
    x, y, c, chips = _place()

    def ici(a, j, slab, to):
        return pltpu.make_async_remote_copy(
            src_ref=outs[a].at[slab, c], dst_ref=outs[a].at[slab, c], send_sem=sems_ici[0][3 * a + j],
            recv_sem=sems_ici[1][3 * a + j], device_id=to, device_id_type=MESH)

    def d2d(a, j, slab, half):
        return pltpu.make_async_remote_copy(
            src_ref=outs[a].at[slab, half], dst_ref=outs[a].at[slab, half], send_sem=sems_d2d[0][3 * a + j],
            recv_sem=sems_d2d[1][3 * a + j], device_id=(x, y, 1 - c), device_id_type=MESH)

    return x, y, c, chips, ici, d2d


def _async_call(body, bufs, sems_in, after, sems_out, token, *, name):
    n = len(bufs)
    n_out = 6 * n if sems_out else 0
    extra = [] if after is None else [after]
    return pl.pallas_call(
        body, name=name,
        in_specs=[_ANY] * n + [_SEM] * len(sems_in) + [_ANY] * len(extra),
        out_specs=[_ANY] * n + [_SEM] * n_out + [_VMEM] * token,
        out_shape=[jax.ShapeDtypeStruct(t.shape, t.dtype) for t in bufs]
        + [pltpu.SemaphoreType.DMA(())] * n_out + [_TOKEN] * token,
        input_output_aliases={a: a for a in range(n)},
        compiler_params=pltpu.CompilerParams(has_side_effects=_EFFECT),
    )(*bufs, *sems_in, *extra)


def gather_start(bufs, after=None, *, name):
    n = len(bufs)
    k = n + (after is not None)

    def body(*refs):
        outs = refs[k:k + n]
        sems, token = refs[k + n:k + 7 * n], refs[k + 7 * n]
        x, y, c, chips, ici, _ = _gather_copies(outs, (sems[:3 * n], sems[3 * n:]), None)
        for a in range(n):
            for j, (px, py) in enumerate(chips):
                ici(a, j, 2 * x + y, (px, py, c)).start()
        token[...] = jnp.zeros_like(token)

    res = _async_call(body, bufs, [], after, True, True, name=name)
    return res[:n], res[n:7 * n], res[7 * n]


def gather_pass(bufs, sems, after, *, name):
    n = len(bufs)

    def body(*refs):
        s_in = refs[n:7 * n]
        outs = refs[7 * n + 1:8 * n + 1]
        s_out, token = refs[8 * n + 1:14 * n + 1], refs[14 * n + 1]
        x, y, c, chips, ici, d2d = _gather_copies(outs, (s_in[:3 * n], s_in[3 * n:]),
                                                  (s_out[:3 * n], s_out[3 * n:]))
        for a in range(n):
            for j, (px, py) in enumerate(chips):
                ici(a, j, 2 * px + py, (px, py, c)).wait_recv()
                d2d(a, j, 2 * px + py, c).start()
        for a in range(n):
            for j, (px, py) in enumerate(chips):
                ici(a, j, 2 * x + y, (px, py, c)).wait_send()
        token[...] = jnp.zeros_like(token)

    res = _async_call(body, bufs, list(sems), after, True, True, name=name)
    return res[:n], res[n:7 * n], res[7 * n]


def gather_wait(bufs, sems, after, *, name):
    n = len(bufs)

    def body(*refs):
        s_in = refs[n:7 * n]
        outs = refs[7 * n + 1:]
        x, y, c, chips, _, d2d = _gather_copies(outs, None, (s_in[:3 * n], s_in[3 * n:]))
        for a in range(n):
            for j, (px, py) in enumerate(chips):
                d2d(a, j, 2 * px + py, 1 - c).wait_recv()
                d2d(a, j, 2 * px + py, c).wait_send()

    return _async_call(body, bufs, list(sems), after, False, False, name=name)


def _pair_plan(srcs, lands):
    x, y, c, _ = _place()
    return [(srcs[a].at[j, 1 - c], lands[a].at[j], (x, y, 1 - c))
            for a in range(len(srcs)) for j in range(N_SHARD)]


def _chip_plan(srcs, lands):
    x, y, c, chips = _place()
    return [(srcs[a].at[2 * px + py], lands[a].at[2 * x + y], (px, py, c))
            for a in range(len(srcs)) for (px, py) in chips]


def copies_start(arrs, land_shapes, plan, per_array, *, name):
    n = len(arrs)
    k = per_array * n

    def body(*refs):
        lands = refs[2 * n:3 * n]
        sems, token = refs[3 * n:3 * n + 2 * k], refs[3 * n + 2 * k]
        for i, (src, dst, dev) in enumerate(plan(refs[n:2 * n], lands)):
            pltpu.make_async_remote_copy(src_ref=src, dst_ref=dst, send_sem=sems[i], recv_sem=sems[k + i],
                                         device_id=dev, device_id_type=MESH).start()
        token[...] = jnp.zeros_like(token)

    res = pl.pallas_call(
        body, name=name,
        in_specs=[_ANY] * n,
        out_specs=[_ANY] * (2 * n) + [_SEM] * (2 * k) + [_VMEM],
        out_shape=[jax.ShapeDtypeStruct(t.shape, t.dtype) for t in arrs] + list(land_shapes)
        + [pltpu.SemaphoreType.DMA(())] * (2 * k) + [_TOKEN],
        input_output_aliases={a: a for a in range(n)},
        compiler_params=pltpu.CompilerParams(has_side_effects=_EFFECT),
    )(*arrs)
    return res[:n], res[n:2 * n], res[2 * n:2 * n + 2 * k], res[2 * n + 2 * k]


def copies_wait(arrs, lands, sems, after, plan, *, name):
    n = len(arrs)
    k = len(sems) // 2

    def body(*refs):
        s_in = refs[2 * n:2 * n + 2 * k]
        outs = refs[2 * n + 2 * k + 1:]
        for i, (src, dst, dev) in enumerate(plan(outs[:n], outs[n:])):
            cp = pltpu.make_async_remote_copy(src_ref=src, dst_ref=dst, send_sem=s_in[i], recv_sem=s_in[k + i],
                                              device_id=dev, device_id_type=MESH)
            cp.wait_send()
            cp.wait_recv()

    res = pl.pallas_call(
        body, name=name,
        in_specs=[_ANY] * (2 * n) + [_SEM] * (2 * k) + [_ANY],
        out_specs=[_ANY] * (2 * n),
        out_shape=[jax.ShapeDtypeStruct(t.shape, t.dtype) for t in list(arrs) + list(lands)],
        input_output_aliases={a: a for a in range(2 * n)},
        compiler_params=pltpu.CompilerParams(has_side_effects=_EFFECT),
    )(*arrs, *lands, *sems, after)
    return res[:n], res[n:]


def pair_send_halves(arrs, *, name):
    n = len(arrs)

    def body(*refs):
        ins, outs = refs[:n], refs[n:2 * n]
        send_sems, recv_sems = refs[2 * n:]
        x, y, c, _ = _place()
        copies = []
        for a in range(n):
            for j in range(N_SHARD):
                cp = pltpu.make_async_remote_copy(
                    src_ref=ins[a].at[j, 1 - c], dst_ref=outs[a].at[j], send_sem=send_sems.at[a, j],
                    recv_sem=recv_sems.at[a, j], device_id=(x, y, 1 - c), device_id_type=MESH)
                cp.start()
                copies.append(cp)
        for cp in copies:
            cp.wait()

    return pl.pallas_call(
        body, name=name,
        in_specs=[_ANY] * n, out_specs=[_ANY] * n,
        out_shape=[jax.ShapeDtypeStruct((N_SHARD,) + t.shape[2:], t.dtype) for t in arrs],
        scratch_shapes=[pltpu.SemaphoreType.DMA((n, N_SHARD)), pltpu.SemaphoreType.DMA((n, N_SHARD))],
    )(*arrs)


def chip_scatter(arrs, *, name):
    n = len(arrs)

    def body(*refs):
        ins, outs = refs[:n], refs[n:2 * n]
        send_sems, recv_sems = refs[2 * n:]
        x, y, c, chips = _place()
        me = 2 * x + y
        copies = []
        for a in range(n):
            for j, (px, py) in enumerate(chips):
                cp = pltpu.make_async_remote_copy(
                    src_ref=ins[a].at[2 * px + py], dst_ref=outs[a].at[me], send_sem=send_sems.at[a, j],
                    recv_sem=recv_sems.at[a, j], device_id=(px, py, c), device_id_type=MESH)
                cp.start()
                copies.append(cp)
        for cp in copies:
            cp.wait()

    return pl.pallas_call(
        body, name=name,
        in_specs=[_ANY] * n, out_specs=[_ANY] * n,
        out_shape=[jax.ShapeDtypeStruct(t.shape, t.dtype) for t in arrs],
        scratch_shapes=[pltpu.SemaphoreType.DMA((n, 3)), pltpu.SemaphoreType.DMA((n, 3))],
    )(*arrs)


def pair_gather(bufs, *, name):
    n = len(bufs)

    def body(*refs):
        outs = refs[n:2 * n]
        send_sems, recv_sems = refs[2 * n:]
        x, y, c, _ = _place()
        copies = []
        for a in range(n):
            cp = pltpu.make_async_remote_copy(
                src_ref=outs[a].at[c], dst_ref=outs[a].at[c], send_sem=send_sems.at[a],
                recv_sem=recv_sems.at[a], device_id=(x, y, 1 - c), device_id_type=MESH)
            cp.start()
            copies.append(cp)
        for cp in copies:
            cp.wait()

    return pl.pallas_call(
        body, name=name,
        in_specs=[_ANY] * n, out_specs=[_ANY] * n,
        out_shape=[jax.ShapeDtypeStruct(t.shape, t.dtype) for t in bufs],
        input_output_aliases={a: a for a in range(n)},
        scratch_shapes=[pltpu.SemaphoreType.DMA((n,)), pltpu.SemaphoreType.DMA((n,))],
    )(*bufs)


def all_sum_small(v, *, name):
    m = v.shape[0]

    def body(v_ref, o_ref, buf, send_sems, recv_sems, local_sem):
        x, y, c, chips = _place()
        me, sibling = (x, y, c), (x, y, 1 - c)

        def rows(px, py, pc):
            return buf.at[pl.ds((4 * px + 2 * py + pc) * m, m), :]

        def copy(k, block, to, src=None):
            return pltpu.make_async_remote_copy(
                src_ref=rows(*block) if src is None else src, dst_ref=rows(*block),
                send_sem=send_sems.at[k], recv_sem=recv_sems.at[k], device_id=to, device_id_type=MESH)

        mine = pltpu.make_async_copy(v_ref, rows(*me), local_sem)
        mine.start()
        first = [copy(0, me, sibling, src=v_ref)]
        first += [copy(1 + j, me, (*chip, c), src=v_ref) for j, chip in enumerate(chips)]
        for cp in first:
            cp.start()
        passed = [copy(4 + j, (*chip, c), sibling) for j, chip in enumerate(chips)]
        for j, chip in enumerate(chips):
            copy(1 + j, (*chip, c), me).wait_recv()
            passed[j].start()
        copy(0, sibling, me).wait_recv()
        for j, chip in enumerate(chips):
            copy(4 + j, (*chip, 1 - c), me).wait_recv()
        for cp in first + passed:
            cp.wait_send()
        mine.wait()
        acc = buf[pl.ds(0, m), :]
        for k in range(1, 8):
            acc = acc + buf[pl.ds(k * m, m), :]
        o_ref[...] = acc

    vm = pl.BlockSpec(memory_space=pltpu.VMEM)
    return pl.pallas_call(
        body, name=name, in_specs=[vm], out_specs=vm,
        out_shape=jax.ShapeDtypeStruct((m, 128), F32),
        scratch_shapes=[pltpu.VMEM((8 * m, 128), F32), pltpu.SemaphoreType.DMA((7,)),
                        pltpu.SemaphoreType.DMA((7,)), pltpu.SemaphoreType.DMA],
    )(v)


def pack_slab(srcs, blocks, maps, n_slots, slot_shape, *, place, dep=None, name):
    k = len(srcs)
    per = n_slots // k
    dep_specs, dep_args = _dep_operand(dep)

    def body(place_ref, *refs):
        o_ref = refs[k + len(dep_args)]
        s = pl.program_id(0)
        for i in range(k):
            @pl.when(jnp.logical_and(s >= i * per, s < (i + 1) * per))
            def _():
                o_ref[...] = refs[i][...].astype(BF16)

    return pl.pallas_call(
        body, name=name,
        grid_spec=pltpu.PrefetchScalarGridSpec(
            num_scalar_prefetch=1, grid=(n_slots,),
            in_specs=[pl.BlockSpec(blk, functools.partial(lambda s, pr, m: m(s), m=m)) for blk, m in zip(blocks, maps)]
            + dep_specs,
            out_specs=pl.BlockSpec((None, None) + slot_shape, lambda s, pr: (pr[1], s, 0, 0))),
        out_shape=jax.ShapeDtypeStruct((N_SHARD, n_slots) + slot_shape, BF16),
        compiler_params=_params("arbitrary"),
    )(place, *srcs, *dep_args)


def sum_halves(arr, recv, c_idx, *, name):
    _, _, r, c = arr.shape
    tr = r
    while tr * c * 4 > (1 << 20) and tr % 32 == 0:
        tr //= 2

    def body(c_ref, a_ref, b_ref, o_ref):
        o_ref[...] = (a_ref[...] + b_ref[...]).astype(BF16)

    return pl.pallas_call(
        body, name=name,
        grid_spec=pltpu.PrefetchScalarGridSpec(
            num_scalar_prefetch=1, grid=(N_SHARD, r // tr),
            in_specs=[pl.BlockSpec((None, None, tr, c), lambda j, i, cr: (j, cr[0], i, 0)),
                      pl.BlockSpec((None, tr, c), lambda j, i, cr: (j, i, 0))],
            out_specs=pl.BlockSpec((None, tr, c), lambda j, i, cr: (j, i, 0))),
        out_shape=jax.ShapeDtypeStruct((N_SHARD, r, c), BF16),
        compiler_params=_params("parallel", "parallel"),
    )(c_idx, arr, recv)


def sum_chips(recv, part, place, *, name):
    _, r, c = recv.shape
    tr = r
    while tr * c * 4 > (1 << 20) and tr % 32 == 0:
        tr //= 2

    def body(place_ref, a0, a1, a2, a3, p_ref, o_ref):
        me = place_ref[1]
        acc = None
        for k, a_ref in enumerate((a0, a1, a2, a3)):
            term = jnp.where(me == k, p_ref[...], a_ref[...]).astype(F32)
            acc = term if acc is None else acc + term
        o_ref[...] = acc

    def slab(k):
        return pl.BlockSpec((None, tr, c), lambda i, pr: (jnp.where(pr[1] == k, (k + 1) % N_SHARD, k), i, 0))

    return pl.pallas_call(
        body, name=name,
        grid_spec=pltpu.PrefetchScalarGridSpec(
            num_scalar_prefetch=1, grid=(r // tr,),
            in_specs=[slab(k) for k in range(N_SHARD)]
            + [pl.BlockSpec((None, tr, c), lambda i, pr: (pr[1], i, 0))],
            out_specs=pl.BlockSpec((None, tr, c), lambda i, pr: (pr[0], i, 0))),
        out_shape=jax.ShapeDtypeStruct((2, r, c), F32),
        compiler_params=_params("parallel"),
    )(place, recv, recv, recv, recv, part)


_WEIGHTS = ("ff1_norm", "ff1_wg", "ff1_wu", "ff1_wd", "mix_norm", "w_in", "sgu_ln_g", "sgu_ln_b", "sgu_w",
            "sgu_b", "lru_conv_w", "lru_conv_b", "lru_wa", "lru_ba", "lru_wx", "lru_bx", "lru_lambda",
            "gdn_conv_w", "gdn_a_log", "gdn_dt_bias", "gdn_norm_g", "pool_w", "pool_scale", "w_branch",
            "w_out", "ff2_norm", "ff2_wg", "ff2_wu", "ff2_wd", "final_norm")
_BIG = ("ff1_wg", "ff1_wu", "ff1_wd", "w_in", "w_branch", "w_out", "ff2_wg", "ff2_wu", "ff2_wd")
_SMALL = tuple(n for n in _WEIGHTS if n not in _BIG)
_CONV = ("lru_conv_w", "gdn_conv_w")


def _seg_rows(shape):
    return -(-math.prod(shape) // 1024) * 8


def _pack(arrs):
    segs = []
    for t in arrs:
        rows = _seg_rows(t.shape)
        flat = t.reshape(-1)
        segs.append(jnp.pad(flat, (0, rows * 128 - flat.shape[0])).reshape(rows, 128))
    return jnp.concatenate(segs, axis=0)


def _unpack(buf, shapes):
    out, o = [], 0
    for shp in shapes:
        rows = _seg_rows(shp)
        out.append(buf[o:o + rows].reshape(-1)[:math.prod(shp)].reshape(shp))
        o += rows
    return out


def _block_diag(w):
    h, n, _ = w.shape
    same = jnp.arange(h)[:, None, None, None] == jnp.arange(h)[None, None, :, None]
    return jnp.where(same, w[:, :, None, :], 0.0).reshape(h * n, h * n)


def _diag_blocks(m, h=8, n=64):
    return jnp.stack([m[i * n:(i + 1) * n, i * n:(i + 1) * n] for i in range(h)])


def _lane_row(v):
    return jnp.zeros((1, 128), F32).at[0, 4:8].set(v)


def _w_in_to_padded(raw):
    sem = jnp.concatenate([raw[k, :, :P_IN_SHARD] for k in range(N_SHARD)], axis=1)
    zero = jnp.zeros((sem.shape[0], C_GATE - C_BA - 8), sem.dtype)
    return jnp.concatenate([sem[:, 2048:4096], sem[:, 0:2048], sem[:, 4104:4616], sem[:, 4096:4104], zero,
                            sem[:, 4616:P_IN]], axis=1)


def _w_in_grad_to_shards(part):
    r = part.shape[1]
    dwp = part.transpose(1, 0, 2).reshape(r, P_PAD)
    sem = jnp.concatenate([dwp[:, C_AU:C_DX], dwp[:, 0:C_AU], dwp[:, C_BA:C_BA + 8], dwp[:, C_DX:C_BA],
                           dwp[:, C_GATE:]], axis=1)
    sh = sem.reshape(r, N_SHARD, P_IN_SHARD).transpose(1, 0, 2)
    return jnp.pad(sh, ((0, 0), (0, 0), (0, P_IN_SHARD_PAD - P_IN_SHARD)))


def kernel(x, ff1_norm, ff1_wg, ff1_wu, ff1_wd, mix_norm, w_in, sgu_ln_g, sgu_ln_b, sgu_w, sgu_b, lru_conv_w,
           lru_conv_b, lru_wa, lru_ba, lru_wx, lru_bx, lru_lambda, gdn_conv_w, gdn_a_log, gdn_dt_bias, gdn_norm_g,
           pool_w, pool_scale, w_branch, w_out, ff2_norm, ff2_wg, ff2_wu, ff2_wd, final_norm, loss_target, m_ff1_norm,
           m_ff1_wg, m_ff1_wu, m_ff1_wd, m_mix_norm, m_w_in, m_sgu_ln_g, m_sgu_ln_b, m_sgu_w, m_sgu_b, m_lru_conv_w,
           m_lru_conv_b, m_lru_wa, m_lru_ba, m_lru_wx, m_lru_bx, m_lru_lambda, m_gdn_conv_w, m_gdn_a_log, m_gdn_dt_bias,
           m_gdn_norm_g, m_pool_w, m_pool_scale, m_w_branch, m_w_out, m_ff2_norm, m_ff2_wg, m_ff2_wu, m_ff2_wd,
           m_final_norm, v_ff1_norm, v_ff1_wg, v_ff1_wu, v_ff1_wd, v_mix_norm, v_w_in, v_sgu_ln_g, v_sgu_ln_b, v_sgu_w,
           v_sgu_b, v_lru_conv_w, v_lru_conv_b, v_lru_wa, v_lru_ba, v_lru_wx, v_lru_bx, v_lru_lambda, v_gdn_conv_w,
           v_gdn_a_log, v_gdn_dt_bias, v_gdn_norm_g, v_pool_w, v_pool_scale, v_w_branch, v_w_out, v_ff2_norm, v_ff2_wg,
           v_ff2_wu, v_ff2_wd, v_final_norm):
    a = dict(locals())
    n_layer = ff1_norm.shape[0]
    d = D_MODEL
    x0 = x[0]
    chip = 2 * lax.axis_index("x") + lax.axis_index("y")
    c_idx = jnp.reshape(lax.axis_index("c"), (1,)).astype(jnp.int32)
    place = jnp.stack([lax.axis_index("c"), chip]).astype(jnp.int32)

    def own_slab(t, dtype=BF16):
        buf = lax.empty((N_SHARD,) + t.shape, dtype)
        return lax.dynamic_update_slice(buf, t.astype(dtype)[None], (chip,) + (0,) * t.ndim)

    assert n_layer == 2

    def layer_bufs(l, dep=None):
        fc = FF_SHARD
        whole = lambda s: (l, 0, 0)
        rows = lambda s: (l, s, 0)
        pack = functools.partial(pack_slab, place=place, dep=dep)
        bufs = []
        for tag, wg, wu, wd in (("ffn1", ff1_wg, ff1_wu, ff1_wd), ("ffn2", ff2_wg, ff2_wu, ff2_wd)):
            bufs.append([pack([wg, wu], [(None, d, fc)] * 2, [whole] * 2, 2, (d, fc), name=f"pack_{tag}_wgu_{l}"),
                         pack([wd], [(None, fc // 2, d)], [rows], 2, (fc // 2, d), name=f"pack_{tag}_wd_{l}")])
        win = pack([w_in], [(None, d // 2, P_IN_SHARD_PAD)], [rows], 2, (d // 2, P_IN_SHARD_PAD),
                   name=f"pack_w_in_{l}")
        wb = pack([w_branch], [(None, None, BRANCH_W, d // 4)], [lambda s: (l, s, 0, 0)], 4, (BRANCH_W, d // 4),
                  name=f"pack_wb_{l}")
        wo = pack([w_out], [(None, d // 8, d)], [rows], 2, (d // 8, d), name=f"pack_wo_{l}")
        return bufs[0] + [win, wb.reshape(N_SHARD, 2, 2, BRANCH_W, d // 4), wo] + bufs[1]

    n_buf = 7
    groups = {"ffn1": (0, 2), "mix": (2, 5), "ffn2": (5, 7)}

    def group_sems(sems, lo, hi):
        return list(sems[3 * lo:3 * hi]) + list(sems[3 * n_buf + 3 * lo:3 * n_buf + 3 * hi])

    def ffn_weights(wgu, wd):
        return (wgu, wgu, wd.reshape(N_SHARD, 1, FF_SHARD, d), (0, 1, 0))

    def mix_weights(g_w_in, g_wb, g_wo):
        return {"w_in": _w_in_to_padded(g_w_in.reshape(4, d, P_IN_SHARD_PAD)),
                "wb": g_wb.reshape(4, 4, BRANCH_W, d // 4).transpose(1, 2, 0, 3).reshape(4, BRANCH_W, d),
                "wo": g_wo.reshape(d, d)}

    g_lcw, g_gcw = gather_shards([own_slab(lru_conv_w, F32), own_slab(gdn_conv_w, F32)], name="gather_conv_weights")

    def small_params(l):
        return {"lcw": g_lcw[:, l].transpose(1, 0, 2).reshape(4, BRANCH_W),
                "gcw": g_gcw[:, l].transpose(1, 0, 2).reshape(4, 3 * BRANCH_W),
                "lcb": lru_conv_b[l][None],
                "bt": jnp.zeros((128, 128), F32).at[:, :4].set(sgu_b[l].T),
                "wa": _block_diag(lru_wa[l]), "wx": _block_diag(lru_wx[l]),
                "alog": _lane_row(gdn_a_log[l]), "dt": _lane_row(gdn_dt_bias[l])}

    flying = {0: gather_start(layer_bufs(0), name="gather_weights_0_start")[:2]}
    handed = {}

    def hand_on(l, grp, after):
        lo, hi = groups[grp]
        bufs_l, sems_l = flying[l]
        got, sm, tk = gather_pass(bufs_l[lo:hi], group_sems(sems_l, lo, hi), after,
                                  name=f"gather_weights_{l}_{grp}_pass")
        handed[l, grp] = (got, sm)
        return tk

    def arrived(l, grp, after):
        return gather_wait(*handed.pop((l, grp)), after, name=f"gather_weights_{l}_{grp}_wait")

    saved = []
    xs = x0
    tok = hand_on(0, "ffn1", x0)
    for l in range(n_layer):
        p = small_params(l)
        sv = {"p": p, "x0": xs}
        p["ff1"] = ffn_weights(*arrived(l, "ffn1", tok))
        x1 = ffn_fwd(xs, ff1_norm[l][None], *p["ff1"], tb=512, name=f"ffn1_fwd_{l}")
        tok = hand_on(l, "mix", x1)
        if l == 0:
            bufs1 = layer_bufs(1, dep=tok)
            split = groups["ffn1"][1]
            first1 = gather_start(bufs1[:split], name="gather_weights_1_start_ffn1")
        h = norm_fwd(x1, mix_norm[l][None], tb=512, name=f"mix_norm_fwd_{l}")
        p.update(mix_weights(*arrived(l, "mix", h)))
        proj = matmul(h, p["w_in"], tm=1024, tn=1536, tk=1024, name=f"proj_{l}")
        ya = sgu_fwd(proj, sgu_ln_g[l][None], sgu_ln_b[l][None], sgu_w[l], p["bt"], tb=512, name=f"sgu_fwd_{l}")
        xc = conv_fwd(proj, C_BX, BRANCH_W, p["lcw"], p["lcb"], tb=512, name=f"lru_conv_fwd_{l}")
        yb, hl = lru_fwd(xc, proj, p["wa"], p["wx"], lru_ba[l][None], lru_bx[l][None], lru_lambda[l][None],
                         tb=256, name=f"lru_fwd_{l}")
        hand_on(l, "ffn2", yb)
        if l == 0:
            rest1 = gather_start(bufs1[split:], yb, name="gather_weights_1_start_rest")
            k1, k2 = 3 * split, 3 * (n_buf - split)
            flying[1] = (list(first1[0]) + list(rest1[0]),
                         list(first1[1][:k1]) + list(rest1[1][:k2]) + list(first1[1][k1:]) + list(rest1[1][k2:]))
        qkv = conv_fwd(proj, C_QKV, 3 * BRANCH_W, p["gcw"], jnp.zeros((1, 3 * BRANCH_W), F32), tb=512,
                       name=f"gdn_conv_fwd_{l}")
        yc, sall = gdn_fwd(qkv, proj, p["alog"], p["dt"], gdn_norm_g[l][None], cps=GDN_CPS, name=f"gdn_fwd_{l}")
        p["ff2"] = ffn_weights(*arrived(l, "ffn2", yc))
        dep2 = None
        if l + 1 < n_layer:
            dep2 = tok = hand_on(l + 1, "ffn1", rest1[2])
        yd = pool_fwd(proj, pool_w[l], pool_scale[l][None], tb=512, name=f"pool_fwd_{l}")
        ys = (ya, yb, yc, yd)
        x2, merged = merge_fwd(x1, ys, proj, p["wb"], p["wo"], tb=256, dep=dep2, name=f"merge_fwd_{l}")
        x3 = ffn_fwd(x2, ff2_norm[l][None], *p["ff2"], tb=512, name=f"ffn2_fwd_{l}")
        tok = x3
        sv.update(x1=x1, h=h, proj=proj, xc=xc, hl=hl, qkv=qkv, sall=sall, ys=ys, merged=merged, x2=x2)
        saved.append(sv)
        xs = x3

    dx, d_final, loss_blk = loss_head(xs, final_norm[None], loss_target[0], tb=512, name="loss_head")
    loss = lax.psum(loss_blk[0, 0], ("x", "y", "c"))

    small_g = {n: [None] * n_layer for n in _SMALL if n != "final_norm"}
    big_g = [None] * n_layer
    per_layer = 7
    flight = None

    def pair_sums(bigs, recv, tag, w_in_at):
        part = [sum_halves(t, r, c_idx, name=f"grad_pair_sum_{tag}_{i}") for i, (t, r) in enumerate(zip(bigs, recv))]
        if w_in_at is not None:
            part[w_in_at] = _w_in_grad_to_shards(part[w_in_at])
        return part

    def pair_start(arrs, tag):
        return copies_start(arrs, [jax.ShapeDtypeStruct((N_SHARD,) + t.shape[2:], t.dtype) for t in arrs],
                            _pair_plan, N_SHARD, name=f"grad_pair_exchange_{tag}_start")

    def pair_to_chip(fl, after, tag, w_in_at=None):
        sent, recv = copies_wait(*fl[:3], after, _pair_plan, name=f"grad_pair_exchange_{tag}_wait")
        part = pair_sums(sent, recv, tag, w_in_at)
        return copies_start(part, [jax.ShapeDtypeStruct(t.shape, t.dtype) for t in part], _chip_plan, 3,
                            name=f"grad_chip_exchange_{tag}_start")

    def chip_finish(fl, after, tag):
        part, recv = copies_wait(*fl[:3], after, _chip_plan, name=f"grad_chip_exchange_{tag}_wait")
        return [sum_chips(r, q, place, name=f"grad_chip_sum_{tag}_{i}") for i, (r, q) in enumerate(zip(recv, part))]

    early = mid = None

    for l in reversed(range(n_layer)):
        sv = saved[l]
        p = sv["p"]
        proj = sv["proj"]
        dx2, dg_ff2, dab2, hid2, h2, dy2 = ffn_bwd(sv["x2"], dx, ff2_norm[l][None], *p["ff2"], tb=512,
                                                  dep=None if flight is None else flight[3],
                                                  name=f"ffn2_bwd_{l}")
        dwgu2 = matmul(h2, dab2, ta=True, tm=1024, tn=FF_SHARD, tk=TK_DW, name=f"ffn2_dwgu_{l}")
        dwd2 = matmul(hid2, dy2, ta=True, tm=FF_SHARD, tn=1024, tk=TK_DW, name=f"ffn2_dwd_{l}")
        if flight is not None:
            flight = pair_to_chip(flight, dx2, "1", w_in_at=2)
        dgp, dbr, dya, dyb, dyc, dyd, dx2b = merge_bwd(dx2, sv["ys"], proj, p["wb"], p["wo"], tb=256,
                                                       dep=None if flight is None else flight[3],
                                                       name=f"merge_bwd_{l}")
        dwo = matmul(sv["merged"], dx2b, ta=True, tm=1024, tn=1024, tk=TK_DW, name=f"dwo_{l}")
        dwb = jnp.stack([matmul(sv["ys"][g], dbr[g], ta=True, tm=BRANCH_W, tn=1024, tk=TK_DW, name=f"dwb{g}_{l}")
                         for g in range(4)])
        dwb_sh = dwb.reshape(4, BRANCH_W, N_SHARD, d // N_SHARD).transpose(2, 0, 1, 3)
        g_late = [dwgu2, dwd2.reshape(N_SHARD, 2, FF_SHARD // 2, d),
                  dwb_sh.reshape(N_SHARD, 2, 2 * BRANCH_W, d // N_SHARD), dwo.reshape(N_SHARD, 2, d // 8, d)]
        tok = None
        if l == 0:
            early = pair_start(g_late, "0a")
            tok = early[3]
        du, dv, dlg, dlb, dws, dbt = sgu_bwd(proj, dya, sgu_ln_g[l][None], sgu_ln_b[l][None], sgu_w[l], p["bt"],
                                             tb=512, dep=tok, name=f"sgu_bwd_{l}")
        dxc, dgate, dwa, dwx, dba, dbx, dlam = lru_bwd(sv["xc"], proj, sv["hl"], dyb, p["wa"], p["wx"],
                                                       lru_ba[l][None], lru_bx[l][None], lru_lambda[l][None],
                                                       tb=256, name=f"lru_bwd_{l}")
        dbx_in, dcw_l = conv_bwd(proj, C_BX, BRANCH_W, dxc, p["lcw"], tb=512, name=f"lru_conv_bwd_{l}")
        dqkv, dz, dbeta, dal, ddt, dng = gdn_bwd(sv["qkv"], proj, sv["sall"], dyc, p["alog"], p["dt"],
                                                 gdn_norm_g[l][None], cps=GDN_CPS, name=f"gdn_bwd_{l}")
        dqkv_in, dcw_g = conv_bwd(proj, C_QKV, 3 * BRANCH_W, dqkv, p["gcw"], tb=512, name=f"gdn_conv_bwd_{l}")
        ddx, dwp, dsc = pool_bwd(proj, dyd, pool_w[l], pool_scale[l][None], tb=512, name=f"pool_bwd_{l}")
        if l == 0:
            early = pair_to_chip(early, ddx, "0a")
            tok = early[3]
        s = dx.shape[0]
        dproj = jnp.concatenate([dqkv_in, dz, du, dv, dbx_in, dgate, ddx, dbeta,
                                 jnp.zeros((s, C_GATE - C_BA - 128), BF16), dgp], axis=1)
        dh = matmul(dproj, p["w_in"], tb=True, tm=1024, tn=1024, tk=1536, dep=tok, name=f"dh_mix_{l}")
        dw_in = matmul(sv["h"], dproj, ta=True, tm=1024, tn=1152, tk=TK_DW, col_groups=N_SHARD, name=f"dw_in_{l}")
        dw_in = dw_in.reshape(N_SHARD, 2, d // 2, P_PAD // N_SHARD)
        if l == 0:
            mid = pair_start([dw_in], "0b")
            tok = mid[3]
        dx1, dg_mix = norm_bwd(sv["x1"], mix_norm[l][None], dh, dx2, tb=512, dep=tok, name=f"mix_norm_bwd_{l}")
        dx, dg_ff1, dab1, hid1, h1, dy1 = ffn_bwd(sv["x0"], dx1, ff1_norm[l][None], *p["ff1"],
                                                 tb=512, name=f"ffn1_bwd_{l}")
        if l == 0:
            mid = pair_to_chip(mid, dx, "0b", w_in_at=0)
            tok = mid[3]
        dwgu1 = matmul(h1, dab1, ta=True, tm=1024, tn=FF_SHARD, tk=TK_DW, dep=tok, name=f"ffn1_dwgu_{l}")
        dwd1 = matmul(hid1, dy1, ta=True, tm=FF_SHARD, tn=1024, tk=TK_DW, name=f"ffn1_dwd_{l}")

        for n, g in (("ff1_norm", dg_ff1[0]), ("mix_norm", dg_mix[0]), ("sgu_ln_g", dlg[0]), ("sgu_ln_b", dlb[0]),
                     ("sgu_w", dws), ("sgu_b", dbt[:, :4].T), ("lru_conv_w", dcw_l[:4]), ("lru_conv_b", dcw_l[4]),
                     ("lru_wa", _diag_blocks(dwa)), ("lru_ba", dba[0]), ("lru_wx", _diag_blocks(dwx)),
                     ("lru_bx", dbx[0]), ("lru_lambda", dlam[0]), ("gdn_conv_w", dcw_g[:4]),
                     ("gdn_a_log", dal[0, 4:8]), ("gdn_dt_bias", ddt[0, 4:8]), ("gdn_norm_g", dng[0]),
                     ("pool_w", dwp), ("pool_scale", dsc[0]), ("ff2_norm", dg_ff2[0])):
            small_g[n][l] = g
        big_g[l] = [dwgu1, dwd1.reshape(N_SHARD, 2, FF_SHARD // 2, d), dw_in, g_late[2], g_late[3], g_late[0], g_late[1]]
        if l == n_layer - 1:
            flight = pair_start(big_g[l], "1")

    last = big_g[0][:2]
    recv = pair_send_halves(last, name="grad_pair_exchange_0c")
    part = pair_sums(last, recv, "0c", None)
    recv = chip_scatter(part, name="grad_chip_exchange_0c")
    red_c = [sum_chips(r, q, place, name=f"grad_chip_sum_0c_{i}") for i, (r, q) in enumerate(zip(recv, part))]
    red1 = chip_finish(flight, red_c[0], "1")
    red_a = chip_finish(early, red_c[0], "0a")
    red_b = chip_finish(mid, red_c[0], "0b")
    full = pair_gather(red_c + red_b + [red_a[2], red_a[3], red_a[0], red_a[1]] + red1, name="grad_pair_gather")
    grads = {}

    def layers(k, f):
        return jnp.stack([f(full[l * per_layer + k]) for l in range(n_layer)])

    grads["ff1_wg"] = layers(0, lambda t: t[0])
    grads["ff1_wu"] = layers(0, lambda t: t[1])
    grads["ff1_wd"] = layers(1, lambda t: t.reshape(FF_SHARD, d))
    grads["w_in"] = layers(2, lambda t: t.reshape(d, P_IN_SHARD_PAD)[:, :P_IN_SHARD])
    grads["w_branch"] = layers(3, lambda t: t.reshape(4, BRANCH_W, d // N_SHARD))
    grads["w_out"] = layers(4, lambda t: t.reshape(d // N_SHARD, d))
    grads["ff2_wg"] = layers(5, lambda t: t[0])
    grads["ff2_wu"] = layers(5, lambda t: t[1])
    grads["ff2_wd"] = layers(6, lambda t: t.reshape(FF_SHARD, d))

    small_full = [jnp.stack(small_g[n]) for n in _SMALL if n != "final_norm"] + [d_final[0]]
    summed = _unpack(all_sum_small(_pack(small_full), name="grad_small_allreduce"), [t.shape for t in small_full])
    for n, g in zip(_SMALL, summed):
        if n in _CONV:
            width = a[n].shape[-1]
            g = lax.dynamic_slice_in_dim(g, chip * width, width, axis=2)
        grads[n] = g

    delta, new_m, new_v = {}, {}, {}
    for n in _BIG:
        args = (a[n], grads[n], a["m_" + n], a["v_" + n])
        if a[n].shape[-1] == FF_SHARD:
            outs = adamw(*(t.swapaxes(1, 2) for t in args), name=f"adamw_{n}")
            delta[n], new_m[n], new_v[n] = (t.swapaxes(1, 2) for t in outs)
        elif n == "w_in":
            outs = adamw(*(t.transpose(2, 0, 1) for t in args), name=f"adamw_{n}")
            delta[n], new_m[n], new_v[n] = (t.transpose(1, 2, 0) for t in outs)
        else:
            delta[n], new_m[n], new_v[n] = adamw(*args, name=f"adamw_{n}")
    shapes = [a[n].shape for n in _SMALL]
    packed = [_pack([src[pre + n] for n in _SMALL]) for src, pre in ((a, ""), (grads, ""), (a, "m_"), (a, "v_"))]
    for store, buf in zip((delta, new_m, new_v), adamw(*packed, name="adamw_small")):
        store.update(zip(_SMALL, _unpack(buf, shapes)))

    return (loss, dx[None], *[grads[n] for n in _WEIGHTS], *[delta[n] for n in _WEIGHTS],
            *[new_m[n] for n in _WEIGHTS], *[new_v[n] for n in _WEIGHTS])
```

```python
import functools
import math

import jax
import jax.numpy as jnp
from jax import lax
from jax.experimental import pallas as pl
from jax.experimental.pallas import tpu as pltpu

F32 = jnp.float32
BF16 = jnp.bfloat16
MESH = pl.DeviceIdType.MESH

D_MODEL = 1024
N_SHARD = 4
D_FF = 2816
FF_SHARD = D_FF // N_SHARD
BRANCH_W = 512
CHUNK = 64
GDN_HEADS = 4
GDN_DK = 128
GDN_CPS = 4
TK_DW = 2048
LRU_C = 8.0
EPS = 1e-6
P_IN = 8712
P_IN_SHARD = P_IN // N_SHARD
P_IN_SHARD_PAD = 2304
P_PAD = 9216
C_QKV, C_Z, C_AU, C_AV, C_BX, C_BG, C_DX, C_BA, C_GATE = 0, 1536, 2048, 2560, 3072, 3584, 4096, 4608, 5120

ADAM_LR, ADAM_B1, ADAM_B2, ADAM_EPS, ADAM_WD, ADAM_STEP = 0.001, 0.9, 0.999, 1e-08, 0.01, 10


def _gelu(x):
    return 0.5 * x * (1.0 + jnp.tanh(0.7978845608028654 * (x + 0.044715 * (x * x * x))))


def _gelu_grad(x):
    u = 0.7978845608028654 * (x + 0.044715 * (x * x * x))
    t = jnp.tanh(u)
    return 0.5 * (1.0 + t) + 0.5 * x * (1.0 - t * t) * 0.7978845608028654 * (1.0 + 3.0 * 0.044715 * x * x)


def _silu(x):
    return x * jax.nn.sigmoid(x)


def _softplus(x):
    return jnp.maximum(x, 0.0) + jnp.log1p(jnp.exp(-jnp.abs(x)))


def _dg(a, b, ca, cb):
    return lax.dot_general(a.astype(BF16), b.astype(BF16), (((ca,), (cb,)), ((), ())),
                           preferred_element_type=F32)


def _dg3(a, b, ca, cb):
    a1 = a.astype(BF16)
    a2 = (a - a1.astype(F32)).astype(BF16)
    b1 = b.astype(BF16)
    b2 = (b - b1.astype(F32)).astype(BF16)
    dn = (((ca,), (cb,)), ((), ()))
    d = functools.partial(lax.dot_general, dimension_numbers=dn, preferred_element_type=F32)
    return d(a1, b1) + (d(a1, b2) + d(a2, b1))


def _make_mm(ca, cb, dot):
    @jax.custom_vjp
    def f(a, b):
        return dot(a, b, ca, cb)

    def fwd(a, b):
        return dot(a, b, ca, cb), (a, b)

    def bwd(res, g):
        a, b = res
        if (ca, cb) == (1, 0):
            return dot(g, b, 1, 1), dot(a, g, 0, 0)
        if (ca, cb) == (1, 1):
            return dot(g, b, 1, 0), dot(g, a, 0, 0)
        return dot(b, g, 1, 1), dot(a, g, 1, 0)

    f.defvjp(fwd, bwd)
    return f


mm_nn = _make_mm(1, 0, _dg)


def _bdg(a, b, ca, cb):
    return lax.dot_general(a.astype(BF16), b.astype(BF16), (((ca,), (cb,)), ((0,), (0,))),
                           preferred_element_type=F32)


def _split2(x):
    x1 = x.astype(BF16)
    return x1, (x - x1.astype(F32)).astype(BF16)


def _bdg3(a, b, ca, cb):
    a1, a2 = _split2(a)
    b1, b2 = _split2(b)
    d = functools.partial(lax.dot_general, dimension_numbers=(((ca,), (cb,)), ((0,), (0,))),
                          preferred_element_type=F32)
    return d(a1, b1) + (d(a1, b2) + d(a2, b1))


def _make_bmm(ca, cb):
    @jax.custom_vjp
    def f(a, b):
        return _bdg(a, b, ca, cb)

    def fwd(a, b):
        return _bdg(a, b, ca, cb), (a, b)

    def bwd(res, g):
        a, b = res
        if (ca, cb) == (2, 1):
            return _bdg(g, b, 2, 2), _bdg(a, g, 1, 1)
        if (ca, cb) == (2, 2):
            return _bdg(g, b, 2, 1), _bdg(g, a, 1, 1)
        return _bdg(b, g, 2, 2), _bdg(a, g, 2, 1)

    f.defvjp(fwd, bwd)
    return f


bmm_nn = _make_bmm(2, 1)
bmm_nt = _make_bmm(2, 2)
bmm_tn = _make_bmm(1, 1)


def _iota3(shape, dim):
    return lax.broadcasted_iota(jnp.int32, shape, dim)


@jax.custom_vjp
def _tri_inv(aw):
    b, n, _ = aw.shape
    shp = (b, n, 2 * n)
    left = _iota3(shp, 2) < n
    x = jnp.where(_iota3(shp, 1) == _iota3(shp, 2), 1.0, 0.0) - aw
    zero = jnp.zeros((b, n, 4 * n), BF16)
    k = 1
    while k < n:
        x1, x2 = _split2(x)
        lhs = jnp.concatenate([x1, x2], axis=1)
        rhs = jnp.concatenate([zero, jnp.concatenate([x1, x2], axis=2)], axis=1)
        r = lax.dot_general(lhs, rhs, (((2,), (1,)), ((0,), (0,))), preferred_element_type=F32)
        mx = (r[:, :n, :2 * n] + r[:, :n, 2 * n:]) + (r[:, n:, :2 * n] + r[:, n:, 2 * n:])
        x = jnp.where(left, x, 0.0) + mx
        k *= 2
    return jnp.where(left, x, 0.0)


def _tri_inv_fwd(aw):
    t = _tri_inv(aw)
    return t, t


def _tri_inv_bwd(t, dt):
    n = t.shape[1]
    x = _bdg3(t, dt, 1, 1)[:, :n]
    tp = jnp.concatenate([jnp.zeros_like(t), t], axis=1)
    return (-_bdg3(x, tp, 2, 2),)


_tri_inv.defvjp(_tri_inv_fwd, _tri_inv_bwd)


def _rows(shape):
    return lax.broadcasted_iota(jnp.int32, shape, 0)


def _lanes(shape):
    return lax.broadcasted_iota(jnp.int32, shape, 1)


def _col(x, j):
    return jnp.sum(jnp.where(_lanes(x.shape) == j, x, 0.0), axis=1, keepdims=True)


def _put_col(col, j, width):
    shape = (col.shape[0], width)
    return jnp.where(_lanes(shape) == j, jnp.broadcast_to(col, shape), 0.0)


VMEM_BYTES_V7X = 64 << 20


def _params(*sem, vmem_bytes=None):
    return pltpu.CompilerParams(dimension_semantics=sem, vmem_limit_bytes=vmem_bytes)


def matmul(a, b, *, ta=False, tb=False, tm, tn, tk, out_dtype=F32, col_groups=1, dep=None, name):
    ba, bb = a.shape[:-2], b.shape[:-2]
    batch = ba if len(ba) >= len(bb) else bb
    assert ba in ((), batch) and bb in ((), batch)
    nb = len(batch)
    (m, k) = (a.shape[-1], a.shape[-2]) if ta else (a.shape[-2], a.shape[-1])
    (k2, n) = (b.shape[-1], b.shape[-2]) if tb else (b.shape[-2], b.shape[-1])
    tm, tn, tk = min(tm, m), min(tn, n), min(tk, k)
    assert k == k2 and m % tm == 0 and n % tn == 0 and k % tk == 0, (a.shape, b.shape, tm, tn, tk)
    nk = k // tk
    grid = batch + (m // tm, n // tn, nk)

    def a_map(*g):
        i, kk = g[nb], g[nb + 2]
        return (g[:nb] if ba else ()) + ((kk, i) if ta else (i, kk))

    def b_map(*g):
        j, kk = g[nb + 1], g[nb + 2]
        return (g[:nb] if bb else ()) + ((j, kk) if tb else (kk, j))

    per_group = n // tn // col_groups
    assert col_groups == 1 or (nb == 0 and per_group * col_groups * tn == n)

    def o_map(*g):
        if col_groups > 1:
            return (g[1] // per_group, g[0], g[1] % per_group)
        return g[:nb] + (g[nb], g[nb + 1])

    a_blk = (None,) * len(ba) + ((tk, tm) if ta else (tm, tk))
    b_blk = (None,) * len(bb) + ((tn, tk) if tb else (tk, tn))
    o_blk = (None,) * (nb + (col_groups > 1)) + (tm, tn)
    o_shape = (col_groups, m, n // col_groups) if col_groups > 1 else batch + (m, n)

    dep_specs, dep_args = _dep_operand(dep)

    def body(a_ref, b_ref, *rest):
        o_ref, acc_ref = rest[len(dep_args):]
        kk = pl.program_id(nb + 2)

        @pl.when(kk == 0)
        def _():
            acc_ref[...] = jnp.zeros_like(acc_ref)

        acc_ref[...] += _dg(a_ref[...], b_ref[...], 0 if ta else 1, 1 if tb else 0)

        @pl.when(kk == nk - 1)
        def _():
            o_ref[...] = acc_ref[...].astype(o_ref.dtype)

    return pl.pallas_call(
        body, name=name, grid=grid,
        in_specs=[pl.BlockSpec(a_blk, a_map), pl.BlockSpec(b_blk, b_map)] + dep_specs,
        out_specs=pl.BlockSpec(o_blk, o_map),
        out_shape=jax.ShapeDtypeStruct(o_shape, out_dtype),
        scratch_shapes=[pltpu.VMEM((tm, tn), F32)],
        compiler_params=_params(*(("parallel",) * (nb + 2) + ("arbitrary",))),
    )(a, b, *dep_args)


def norm_fwd(x, g, *, tb, name):
    s, d = x.shape

    def body(x_ref, g_ref, h_ref):
        xv = x_ref[...]
        r = lax.rsqrt(jnp.mean(xv * xv, axis=-1, keepdims=True) + EPS)
        h_ref[...] = (xv * r * g_ref[...]).astype(BF16)

    return pl.pallas_call(
        body, name=name, grid=(s // tb,),
        in_specs=[pl.BlockSpec((tb, d), lambda i: (i, 0)), pl.BlockSpec((1, d), lambda i: (0, 0))],
        out_specs=pl.BlockSpec((tb, d), lambda i: (i, 0)),
        out_shape=jax.ShapeDtypeStruct((s, d), BF16),
        compiler_params=_params("parallel"),
    )(x, g)


def _rms_bwd(xv, gv, dh):
    r = lax.rsqrt(jnp.mean(xv * xv, axis=-1, keepdims=True) + EPS)
    xh = xv * r
    dxh = dh * gv
    dx = r * (dxh - xh * jnp.mean(dxh * xh, axis=-1, keepdims=True))
    return dx, jnp.sum(dh * xh, axis=0, keepdims=True)


def norm_bwd(x, g, dh, dres, *, tb, dep=None, name):
    s, d = x.shape
    dep_specs, dep_args = _dep_operand(dep)

    def body(x_ref, g_ref, dh_ref, dres_ref, *rest):
        dx_ref, dg_ref = rest[len(dep_args):]
        dx, dg = _rms_bwd(x_ref[...], g_ref[...], dh_ref[...])
        dx_ref[...] = dres_ref[...] + dx

        @pl.when(pl.program_id(0) == 0)
        def _():
            dg_ref[...] = jnp.zeros_like(dg_ref)

        dg_ref[...] += dg

    row = pl.BlockSpec((tb, d), lambda i: (i, 0))
    one = pl.BlockSpec((1, d), lambda i: (0, 0))
    return pl.pallas_call(
        body, name=name, grid=(s // tb,),
        in_specs=[row, one, row, row] + dep_specs, out_specs=[row, one],
        out_shape=[jax.ShapeDtypeStruct((s, d), F32), jax.ShapeDtypeStruct((1, d), F32)],
        compiler_params=_params("arbitrary"),
    )(x, g, dh, dres, *dep_args)


def _dep_operand(dep):
    return ([], []) if dep is None else ([_ANY], [dep])


def ffn_fwd(x, g, wg, wu, wd, layer, *, tb, dep=None, name):
    s, d = x.shape
    fc = wd.shape[-2]
    dep_specs, dep_args = _dep_operand(dep)

    def body(x_ref, g_ref, wg_ref, wu_ref, wd_ref, *rest):
        o_ref, h_scr, acc_scr = rest[len(dep_args):]
        j = pl.program_id(1)

        @pl.when(j == 0)
        def _():
            xv = x_ref[...]
            r = lax.rsqrt(jnp.mean(xv * xv, axis=-1, keepdims=True) + EPS)
            h_scr[...] = (xv * r * g_ref[...]).astype(BF16)
            acc_scr[...] = jnp.zeros_like(acc_scr)

        h = h_scr[...]
        a = _dg(h, wg_ref[...], 1, 0)
        b = _dg(h, wu_ref[...], 1, 0)
        acc_scr[...] += _dg(_silu(a) * b, wd_ref[...], 1, 0)

        @pl.when(j == N_SHARD - 1)
        def _():
            o_ref[...] = x_ref[...] + 0.5 * acc_scr[...]

    row = pl.BlockSpec((tb, d), lambda i, j: (i, 0))
    lg, lu, ld = layer
    return pl.pallas_call(
        body, name=name, grid=(s // tb, N_SHARD),
        in_specs=[row, pl.BlockSpec((1, d), lambda i, j: (0, 0)),
                  pl.BlockSpec((None, None, d, fc), lambda i, j: (j, lg, 0, 0)),
                  pl.BlockSpec((None, None, d, fc), lambda i, j: (j, lu, 0, 0)),
                  pl.BlockSpec((None, None, fc, d), lambda i, j: (j, ld, 0, 0))] + dep_specs,
        out_specs=row,
        out_shape=jax.ShapeDtypeStruct((s, d), F32),
        scratch_shapes=[pltpu.VMEM((tb, d), BF16), pltpu.VMEM((tb, d), F32)],
        compiler_params=_params("parallel", "arbitrary"),
    )(x, g, wg, wu, wd, *dep_args)


def ffn_bwd(x, dxo, g, wg, wu, wd, layer, *, tb, dep=None, name):
    s, d = x.shape
    fc = wd.shape[-2]
    dep_specs, dep_args = _dep_operand(dep)

    def body(x_ref, dxo_ref, g_ref, wg_ref, wu_ref, wd_ref, *rest):
        dx_ref, dg_ref, dab_ref, hid_ref, h_ref, dy_ref, dh_scr = rest[len(dep_args):]
        i, j = pl.program_id(0), pl.program_id(1)

        @pl.when(j == 0)
        def _():
            xv = x_ref[...]
            r = lax.rsqrt(jnp.mean(xv * xv, axis=-1, keepdims=True) + EPS)
            h_ref[...] = (xv * r * g_ref[...]).astype(BF16)
            dy_ref[...] = (0.5 * dxo_ref[...]).astype(BF16)
            dh_scr[...] = jnp.zeros_like(dh_scr)

        h = h_ref[...]
        a = _dg(h, wg_ref[...], 1, 0)
        b = _dg(h, wu_ref[...], 1, 0)
        sg = jax.nn.sigmoid(a)
        sl = a * sg
        dhid = _dg(dy_ref[...], wd_ref[...], 1, 1)
        da = (dhid * b * (sg * (1.0 + a * (1.0 - sg)))).astype(BF16)
        db = (dhid * sl).astype(BF16)
        dab_ref[0] = da
        dab_ref[1] = db
        hid_ref[...] = (sl * b).astype(BF16)
        dh_scr[...] += _dg(da, wg_ref[...], 1, 1) + _dg(db, wu_ref[...], 1, 1)

        @pl.when(jnp.logical_and(i == 0, j == 0))
        def _():
            dg_ref[...] = jnp.zeros_like(dg_ref)

        @pl.when(j == N_SHARD - 1)
        def _():
            dx, dg = _rms_bwd(x_ref[...], g_ref[...], dh_scr[...])
            dx_ref[...] = dxo_ref[...] + dx
            dg_ref[...] += dg

    row = pl.BlockSpec((tb, d), lambda i, j: (i, 0))
    one = pl.BlockSpec((1, d), lambda i, j: (0, 0))
    lg, lu, ld = layer
    return pl.pallas_call(
        body, name=name, grid=(s // tb, N_SHARD),
        in_specs=[row, row, one,
                  pl.BlockSpec((None, None, d, fc), lambda i, j: (j, lg, 0, 0)),
                  pl.BlockSpec((None, None, d, fc), lambda i, j: (j, lu, 0, 0)),
                  pl.BlockSpec((None, None, fc, d), lambda i, j: (j, ld, 0, 0))] + dep_specs,
        out_specs=[row, one,
                   pl.BlockSpec((None, 2, tb, fc), lambda i, j: (j, 0, i, 0)),
                   pl.BlockSpec((None, tb, fc), lambda i, j: (j, i, 0)),
                   row, row],
        out_shape=[jax.ShapeDtypeStruct((s, d), F32), jax.ShapeDtypeStruct((1, d), F32),
                   jax.ShapeDtypeStruct((N_SHARD, 2, s, fc), BF16),
                   jax.ShapeDtypeStruct((N_SHARD, s, fc), BF16),
                   jax.ShapeDtypeStruct((s, d), BF16), jax.ShapeDtypeStruct((s, d), BF16)],
        scratch_shapes=[pltpu.VMEM((tb, d), F32)],
        compiler_params=_params("arbitrary", "arbitrary",
                                vmem_bytes=min(VMEM_BYTES_V7X - (8 << 20),
                                               12 * d * fc + 44 * tb * d + 40 * tb * fc + (4 << 20))),
    )(x, dxo, g, wg, wu, wd, *dep_args)


def _pblk(tb, width, col0, rev_n=None):
    assert col0 % width == 0
    cb = col0 // width
    if rev_n is None:
        return pl.BlockSpec((tb, width), lambda i: (i, cb))
    return pl.BlockSpec((tb, width), lambda i: (rev_n - 1 - i, cb))


def _full(shape):
    return pl.BlockSpec(shape, lambda i: (0,) * len(shape))


def conv_fwd(src, col0, width, w, bias, *, tb, name):
    s = src.shape[0]
    cb = col0 // width

    def body(x_ref, p_ref, w_ref, b_ref, y_ref):
        i = pl.program_id(0)
        prev = jnp.where(i > 0, p_ref[...], 0.0)
        xe = jnp.concatenate([prev, x_ref[...]], axis=0)
        wv = w_ref[...]
        acc = b_ref[...] + wv[3:4] * xe[8:]
        for k in range(3):
            acc = acc + wv[k:k + 1] * pltpu.roll(xe, 3 - k, 0)[8:]
        y_ref[...] = acc

    return pl.pallas_call(
        body, name=name, grid=(s // tb,),
        in_specs=[_pblk(tb, width, col0),
                  pl.BlockSpec((8, width), lambda i: (jnp.maximum(i * (tb // 8) - 1, 0), cb)),
                  _full((4, width)), _full((1, width))],
        out_specs=pl.BlockSpec((tb, width), lambda i: (i, 0)),
        out_shape=jax.ShapeDtypeStruct((s, width), F32),
        compiler_params=_params("parallel"),
    )(src, src, w, bias)


def conv_bwd(src, col0, width, dpre, w, *, tb, name):
    s = src.shape[0]
    cb = col0 // width
    nt = s // tb

    def body(x_ref, p_ref, d_ref, n_ref, w_ref, dx_ref, dwb_ref):
        i = pl.program_id(0)
        d = d_ref[...]
        nxt = jnp.where(i < nt - 1, n_ref[...], 0.0)
        de = jnp.concatenate([d, nxt], axis=0)
        wv = w_ref[...]
        dx = wv[3:4] * d
        for k in range(3):
            dx = dx + wv[k:k + 1] * pltpu.roll(de, tb + 8 - (3 - k), 0)[:tb]
        dx_ref[...] = dx.astype(BF16)

        @pl.when(i == 0)
        def _():
            dwb_ref[...] = jnp.zeros_like(dwb_ref)

        prev = jnp.where(i > 0, p_ref[...], 0.0)
        xe = jnp.concatenate([prev, x_ref[...]], axis=0)
        dwb_ref[3:4, :] += jnp.sum(d * xe[8:], axis=0, keepdims=True)
        for k in range(3):
            dwb_ref[k:k + 1, :] += jnp.sum(d * pltpu.roll(xe, 3 - k, 0)[8:], axis=0, keepdims=True)
        dwb_ref[4:5, :] += jnp.sum(d, axis=0, keepdims=True)

    return pl.pallas_call(
        body, name=name, grid=(nt,),
        in_specs=[_pblk(tb, width, col0),
                  pl.BlockSpec((8, width), lambda i: (jnp.maximum(i * (tb // 8) - 1, 0), cb)),
                  pl.BlockSpec((tb, width), lambda i: (i, 0)),
                  pl.BlockSpec((8, width), lambda i: (jnp.minimum((i + 1) * (tb // 8), s // 8 - 1), 0)),
                  _full((4, width))],
        out_specs=[pl.BlockSpec((tb, width), lambda i: (i, 0)), _full((8, width))],
        out_shape=[jax.ShapeDtypeStruct((s, width), BF16), jax.ShapeDtypeStruct((8, width), F32)],
        compiler_params=_params("arbitrary"),
    )(src, src, dpre, dpre, w)


SGU_BLOCK = 128


def _sgu_pre(u, v, lg, lb):
    ug = _gelu(u)
    vg = _gelu(v)
    mu = jnp.mean(vg, axis=-1, keepdims=True)
    var = jnp.mean(jnp.square(vg - mu), axis=-1, keepdims=True)
    return ug, (vg - mu) * lax.rsqrt(var + EPS) * lg + lb


def _sgu_mask():
    shp = (SGU_BLOCK, SGU_BLOCK)
    return (_rows(shp) // CHUNK) >= (_lanes(shp) // CHUNK)


def sgu_fwd(proj, lg, lb, ws, bt, *, tb, name):
    s = proj.shape[0]
    w = BRANCH_W

    def body(u_ref, v_ref, lg_ref, lb_ref, w_ref, b_ref, y_ref):
        ug, vn = _sgu_pre(u_ref[...], v_ref[...], lg_ref[...], lb_ref[...])
        mask = _sgu_mask()
        for g in range(4):
            wm = jnp.where(mask, w_ref[g], 0.0)
            bcol = _col(b_ref[...], g)
            cs = slice(128 * g, 128 * g + 128)
            for n in range(tb // SGU_BLOCK):
                rs = slice(SGU_BLOCK * n, SGU_BLOCK * (n + 1))
                mixed = _dg(wm, vn[rs, cs], 1, 0) + bcol
                y_ref[rs, cs] = (ug[rs, cs] * mixed).astype(BF16)

    return pl.pallas_call(
        body, name=name, grid=(s // tb,),
        in_specs=[_pblk(tb, w, C_AU), _pblk(tb, w, C_AV), _full((1, w)), _full((1, w)),
                  _full((4, 128, 128)), _full((128, 128))],
        out_specs=pl.BlockSpec((tb, w), lambda i: (i, 0)),
        out_shape=jax.ShapeDtypeStruct((s, w), BF16),
        compiler_params=_params("parallel"),
    )(proj, proj, lg, lb, ws, bt)


def sgu_bwd(proj, dy, lg, lb, ws, bt, *, tb, dep=None, name):
    s = proj.shape[0]
    w = BRANCH_W
    dep_specs, dep_args = _dep_operand(dep)

    def body(u_ref, v_ref, dy_ref, lg_ref, lb_ref, w_ref, b_ref, *rest):
        du_ref, dv_ref, dlg_ref, dlb_ref, dw_ref, db_ref, dug_scr, dvn_scr = rest[len(dep_args):]

        @pl.when(pl.program_id(0) == 0)
        def _():
            dlg_ref[...] = jnp.zeros_like(dlg_ref)
            dlb_ref[...] = jnp.zeros_like(dlb_ref)
            dw_ref[...] = jnp.zeros_like(dw_ref)
            db_ref[...] = jnp.zeros_like(db_ref)

        (ug, vn), vf = jax.vjp(_sgu_pre, u_ref[...], v_ref[...], lg_ref[...], lb_ref[...])
        dyv = dy_ref[...]
        mask = _sgu_mask()
        for g in range(4):
            wm = jnp.where(mask, w_ref[g], 0.0)
            bcol = _col(b_ref[...], g)
            cs = slice(128 * g, 128 * g + 128)
            dwg = jnp.zeros((SGU_BLOCK, SGU_BLOCK), F32)
            dbc = jnp.zeros((SGU_BLOCK, 1), F32)
            for n in range(tb // SGU_BLOCK):
                rs = slice(SGU_BLOCK * n, SGU_BLOCK * (n + 1))
                blk = vn[rs, cs]
                mixed = _dg(wm, blk, 1, 0) + bcol
                dmix = dyv[rs, cs] * ug[rs, cs]
                dug_scr[rs, cs] = dyv[rs, cs] * mixed
                dvn_scr[rs, cs] = _dg(wm, dmix, 0, 0)
                dwg = dwg + _dg(dmix, blk, 1, 1)
                dbc = dbc + jnp.sum(dmix, axis=1, keepdims=True)
            dw_ref[g] += jnp.where(mask, dwg, 0.0)
            db_ref[...] += _put_col(dbc, g, 128)
        du, dv, dlg, dlb = vf((dug_scr[...], dvn_scr[...]))
        du_ref[...] = du.astype(BF16)
        dv_ref[...] = dv.astype(BF16)
        dlg_ref[...] += dlg
        dlb_ref[...] += dlb

    row = pl.BlockSpec((tb, w), lambda i: (i, 0))
    return pl.pallas_call(
        body, name=name, grid=(s // tb,),
        in_specs=[_pblk(tb, w, C_AU), _pblk(tb, w, C_AV), row, _full((1, w)), _full((1, w)),
                  _full((4, 128, 128)), _full((128, 128))] + dep_specs,
        out_specs=[row, row, _full((1, w)), _full((1, w)), _full((4, 128, 128)), _full((128, 128))],
        out_shape=[jax.ShapeDtypeStruct((s, w), BF16), jax.ShapeDtypeStruct((s, w), BF16),
                   jax.ShapeDtypeStruct((1, w), F32), jax.ShapeDtypeStruct((1, w), F32),
                   jax.ShapeDtypeStruct((4, 128, 128), F32), jax.ShapeDtypeStruct((128, 128), F32)],
        scratch_shapes=[pltpu.VMEM((tb, w), F32), pltpu.VMEM((tb, w), F32)],
        compiler_params=_params("arbitrary"),
    )(proj, proj, dy, lg, lb, ws, bt, *dep_args)


POOL_HALO = 16


def _pooled(xe, t0, tb):
    s2 = xe + pltpu.roll(xe, 1, 0)
    s4 = s2 + pltpu.roll(s2, 2, 0)
    s8 = s4 + pltpu.roll(s4, 4, 0)
    s16 = s8 + pltpu.roll(s8, 8, 0)
    t = (t0 + _rows((tb, 128))).astype(F32)
    outs = []
    for g, sw in enumerate((s2, s4, s8, s16)):
        cs = slice(128 * g, 128 * g + 128)
        cnt = jnp.minimum(t + 1.0, float(2 ** (g + 1)))
        outs.append(sw[POOL_HALO:, cs] / cnt - xe[POOL_HALO:, cs])
    return outs


def pool_fwd(proj, wp, scale, *, tb, name):
    s = proj.shape[0]
    w = BRANCH_W

    def body(x_ref, p_ref, w_ref, sc_ref, y_ref):
        i = pl.program_id(0)
        prev = jnp.where(i > 0, p_ref[...], 0.0)
        xe = jnp.concatenate([prev, x_ref[...]], axis=0)
        pooled = _pooled(xe, i * tb, tb)
        sc = sc_ref[...]
        for g in range(4):
            cs = slice(128 * g, 128 * g + 128)
            y_ref[:, cs] = (_dg(pooled[g], w_ref[g], 1, 0) * sc[:, cs]).astype(BF16)

    cb = C_DX // w
    return pl.pallas_call(
        body, name=name, grid=(s // tb,),
        in_specs=[_pblk(tb, w, C_DX),
                  pl.BlockSpec((POOL_HALO, w), lambda i: (jnp.maximum(i * (tb // POOL_HALO) - 1, 0), cb)),
                  _full((4, 128, 128)), _full((1, w))],
        out_specs=pl.BlockSpec((tb, w), lambda i: (i, 0)),
        out_shape=jax.ShapeDtypeStruct((s, w), BF16),
        compiler_params=_params("parallel"),
    )(proj, proj, wp, scale)


def pool_bwd(proj, dy, wp, scale, *, tb, name):
    s = proj.shape[0]
    w = BRANCH_W
    nt = s // tb
    te = tb + POOL_HALO

    def body(x_ref, p_ref, dy_ref, n_ref, w_ref, sc_ref, dx_ref, dw_ref, dsc_ref):
        i = pl.program_id(0)

        @pl.when(i == 0)
        def _():
            dw_ref[...] = jnp.zeros_like(dw_ref)
            dsc_ref[...] = jnp.zeros_like(dsc_ref)

        prev = jnp.where(i > 0, p_ref[...], 0.0)
        xe = jnp.concatenate([prev, x_ref[...]], axis=0)
        pooled = _pooled(xe, i * tb, tb)
        dyv = dy_ref[...]
        nxt = jnp.where(i < nt - 1, n_ref[...], 0.0)
        dye = jnp.concatenate([dyv, nxt], axis=0)
        sc = sc_ref[...]
        t = (i * tb + _rows((te, 128))).astype(F32)
        for g in range(4):
            cs = slice(128 * g, 128 * g + 128)
            win = 2 ** (g + 1)
            dpm = dye[:, cs] * sc[:, cs]
            dpool = _dg(dpm, w_ref[g], 1, 1)
            q = dpool / jnp.minimum(t + 1.0, float(win))
            r, sh = q, 1
            while sh < win:
                r = r + pltpu.roll(r, te - sh, 0)
                sh *= 2
            dx_ref[:, cs] = (r[:tb] - dpool[:tb]).astype(BF16)
            dw_ref[g] += _dg(pooled[g], dpm[:tb], 0, 0)
            dsc_ref[:, cs] += jnp.sum(dyv[:, cs] * _dg(pooled[g], w_ref[g], 1, 0), axis=0, keepdims=True)

    cb = C_DX // w
    row = pl.BlockSpec((tb, w), lambda i: (i, 0))
    return pl.pallas_call(
        body, name=name, grid=(nt,),
        in_specs=[_pblk(tb, w, C_DX),
                  pl.BlockSpec((POOL_HALO, w), lambda i: (jnp.maximum(i * (tb // POOL_HALO) - 1, 0), cb)),
                  row,
                  pl.BlockSpec((POOL_HALO, w),
                               lambda i: (jnp.minimum((i + 1) * (tb // POOL_HALO), s // POOL_HALO - 1), 0)),
                  _full((4, 128, 128)), _full((1, w))],
        out_specs=[row, _full((4, 128, 128)), _full((1, w))],
        out_shape=[jax.ShapeDtypeStruct((s, w), BF16), jax.ShapeDtypeStruct((4, 128, 128), F32),
                   jax.ShapeDtypeStruct((1, w), F32)],
        compiler_params=_params("arbitrary"),
    )(proj, proj, dy, dy, wp, scale)


def _neg_expm1(z):
    p = 1.0 + z * (1.0 / 9.0)
    for n in (8.0, 7.0, 6.0, 5.0, 4.0, 3.0, 2.0):
        p = 1.0 + z * (1.0 / n) * p
    return jnp.where(z > -0.5, -z * p, 1.0 - jnp.exp(z))


def _lru_gates(xc, wa, wx, ba, bx, lam):
    r = jax.nn.sigmoid(mm_nn(xc, wa) + ba)
    i = jax.nn.sigmoid(mm_nn(xc, wx) + bx)
    log_a = -LRU_C * r * _softplus(-lam)
    a = jnp.exp(log_a)
    return a, jnp.sqrt(_neg_expm1(2.0 * log_a)) * (i * xc)


def _scan_down(a, b):
    n = a.shape[0]
    row = _rows(a.shape)
    sh = 1
    while sh < n:
        keep = row >= sh
        a_sh = jnp.where(keep, pltpu.roll(a, sh, 0), 1.0)
        b_sh = jnp.where(keep, pltpu.roll(b, sh, 0), 0.0)
        b = a * b_sh + b
        a = a * a_sh
        sh *= 2
    return a, b


def _scan_up(a, b):
    n = a.shape[0]
    row = _rows(a.shape)
    sh = 1
    while sh < n:
        keep = row < n - sh
        a_sh = jnp.where(keep, pltpu.roll(a, n - sh, 0), 1.0)
        b_sh = jnp.where(keep, pltpu.roll(b, n - sh, 0), 0.0)
        b = a * b_sh + b
        a = a * a_sh
        sh *= 2
    return a, b


def lru_fwd(xc, proj, wa, wx, ba, bx, lam, *, tb, name):
    s = xc.shape[0]
    w = BRANCH_W

    def body(xc_ref, gate_ref, wa_ref, wx_ref, ba_ref, bx_ref, lam_ref, y_ref, h_ref, carry):
        @pl.when(pl.program_id(0) == 0)
        def _():
            carry[...] = jnp.zeros_like(carry)

        a, b = _lru_gates(xc_ref[...], wa_ref[...], wx_ref[...], ba_ref[...], bx_ref[...], lam_ref[...])
        pa, hb = _scan_down(a, b)
        h = pa * carry[...] + hb
        h_ref[...] = h
        carry[...] = h_ref[tb - 1:tb, :]
        y_ref[...] = (h * _gelu(gate_ref[...])).astype(BF16)

    row = pl.BlockSpec((tb, w), lambda i: (i, 0))
    return pl.pallas_call(
        body, name=name, grid=(s // tb,),
        in_specs=[row, _pblk(tb, w, C_BG), _full((w, w)), _full((w, w)),
                  _full((1, w)), _full((1, w)), _full((1, w))],
        out_specs=[row, row],
        out_shape=[jax.ShapeDtypeStruct((s, w), BF16), jax.ShapeDtypeStruct((s, w), F32)],
        scratch_shapes=[pltpu.VMEM((1, w), F32)],
        compiler_params=_params("arbitrary"),
    )(xc, proj, wa, wx, ba, bx, lam)


def lru_bwd(xc, proj, h, dy, wa, wx, ba, bx, lam, *, tb, name):
    s = xc.shape[0]
    w = BRANCH_W
    nt = s // tb

    def body(xc_ref, gate_ref, h_ref, hp_ref, dy_ref, wa_ref, wx_ref, ba_ref, bx_ref, lam_ref,
             dxc_ref, dgate_ref, dwa_ref, dwx_ref, dba_ref, dbx_ref, dlam_ref, carry):
        i = pl.program_id(0)

        @pl.when(i == 0)
        def _():
            carry[...] = jnp.zeros_like(carry)
            for r in (dwa_ref, dwx_ref, dba_ref, dbx_ref, dlam_ref):
                r[...] = jnp.zeros_like(r)

        (a, b), vf = jax.vjp(_lru_gates, xc_ref[...], wa_ref[...], wx_ref[...],
                             ba_ref[...], bx_ref[...], lam_ref[...])
        gate = gate_ref[...]
        dyv = dy_ref[...]
        hv = h_ref[...]
        row = _rows((tb, w))
        a_next = jnp.where(row < tb - 1, pltpu.roll(a, tb - 1, 0), 1.0)
        pa, xb = _scan_up(a_next, dyv * _gelu(gate))
        dh = xb + pa * carry[...]
        carry[...] = a[0:1, :] * dh[0:1, :]
        h_first = jnp.where(i < nt - 1, hp_ref[7:8, :], 0.0)
        h_prev = jnp.where(row >= 1, pltpu.roll(hv, 1, 0), h_first)
        dxc, dwa, dwx, dba, dbx, dlam = vf((dh * h_prev, dh))
        dxc_ref[...] = dxc
        dgate_ref[...] = (dyv * hv * _gelu_grad(gate)).astype(BF16)
        dwa_ref[...] += dwa
        dwx_ref[...] += dwx
        dba_ref[...] += dba
        dbx_ref[...] += dbx
        dlam_ref[...] += dlam

    row_spec = pl.BlockSpec((tb, w), lambda i: (nt - 1 - i, 0))
    hp_spec = pl.BlockSpec((8, w), lambda i: (jnp.maximum((nt - 1 - i) * (tb // 8) - 1, 0), 0))
    return pl.pallas_call(
        body, name=name, grid=(nt,),
        in_specs=[row_spec, _pblk(tb, w, C_BG, rev_n=nt), row_spec, hp_spec, row_spec,
                  _full((w, w)), _full((w, w)), _full((1, w)), _full((1, w)), _full((1, w))],
        out_specs=[row_spec, row_spec, _full((w, w)), _full((w, w)),
                   _full((1, w)), _full((1, w)), _full((1, w))],
        out_shape=[jax.ShapeDtypeStruct((s, w), F32), jax.ShapeDtypeStruct((s, w), BF16),
                   jax.ShapeDtypeStruct((w, w), F32), jax.ShapeDtypeStruct((w, w), F32),
                   jax.ShapeDtypeStruct((1, w), F32), jax.ShapeDtypeStruct((1, w), F32),
                   jax.ShapeDtypeStruct((1, w), F32)],
        scratch_shapes=[pltpu.VMEM((1, w), F32)],
        compiler_params=_params("arbitrary"),
    )(xc, proj, h, h, dy, wa, wx, ba, bx, lam)


def _gdn_g(ba, alog, dt):
    return -jnp.exp(alog) * _softplus(ba + dt)


def _gdn_local(qp, kp, vp, bpre, gc):
    b, c, dk = qp.shape
    q = _silu(qp)
    k = _silu(kp)
    v = _silu(vp)
    q = q * lax.rsqrt(jnp.sum(q * q, axis=-1, keepdims=True) + EPS) * (dk ** -0.5)
    k = k * lax.rsqrt(jnp.sum(k * k, axis=-1, keepdims=True) + EPS)
    beta = jax.nn.sigmoid(bpre)
    sq = (b, c, c)
    ii, jj = _iota3(sq, 1), _iota3(sq, 2)
    gr = jnp.sum(gc * (ii == jj).astype(F32), axis=1, keepdims=True)
    decay = jnp.exp(jnp.where(ii >= jj, gc - gr, -jnp.inf))
    wd = (b, c, 2 * c)
    wi, wj = _iota3(wd, 1), _iota3(wd, 2) - c
    gr_w = jnp.sum(gc * (wi == wj).astype(F32), axis=1, keepdims=True)
    decay_w = jnp.exp(jnp.where(jnp.logical_and(wj >= 0, wi > wj), gc - gr_w, -jnp.inf))
    kb = k * beta
    pad = jnp.zeros((b, c, dk), F32)
    t = _tri_inv(bmm_nt(kb, jnp.concatenate([pad, k], axis=1)) * decay_w)
    u = bmm_nn(t, jnp.concatenate([v * beta, pad], axis=1))
    wk = bmm_nn(t, jnp.concatenate([kb * jnp.exp(gc), pad], axis=1))
    attn = bmm_nt(q, k) * decay
    g_last = jnp.sum(jnp.where(_iota3((b, c, 1), 1) == c - 1, gc, 0.0), axis=1, keepdims=True)
    return u, wk, attn, q * jnp.exp(gc), k * jnp.exp(g_last - gc), jnp.exp(g_last)


def _gdn_recur(s, u, wk, attn, qg, kd, eg, zz, ng):
    v_new = u - bmm_nn(wk, s)
    o = bmm_nn(qg, s) + bmm_nn(attn, v_new)
    s_out = s * eg + bmm_tn(kd, v_new)
    o = o * lax.rsqrt(jnp.mean(o * o, axis=-1, keepdims=True) + EPS) * ng
    return o * _silu(zz), s_out


def _gdn_inputs(qkv_ref, ba, gcums, cps):
    c, hd = CHUNK, GDN_DK
    qs, ks, vs, bs, gs = [], [], [], [], []
    for ci in range(cps):
        rs = slice(c * ci, c * (ci + 1))
        for h in range(GDN_HEADS):
            qs.append(qkv_ref[rs, hd * h:hd * (h + 1)])
            ks.append(qkv_ref[rs, 512 + hd * h:512 + hd * (h + 1)])
            vs.append(qkv_ref[rs, 1024 + hd * h:1024 + hd * (h + 1)])
            bs.append(_col(ba[rs], h))
            gs.append(_col(gcums[ci], 4 + h))
    return tuple(jnp.stack(t) for t in (qs, ks, vs, bs, gs))


def _chunk_cumsum(g, transpose=False):
    c = CHUNK
    tri = (_rows((c, c)) >= _lanes((c, c))).astype(F32)
    return _dg3(tri, g, 0 if transpose else 1, 0)


def gdn_fwd(qkv, proj, alog, dt, ng, *, cps, name):
    s = qkv.shape[0]
    c = CHUNK
    nc = s // c
    hd = GDN_DK
    tb = c * cps

    def body(qkv_ref, z_ref, ba_ref, al_ref, dt_ref, ng_ref, y_ref, sall_ref, s_scr):
        @pl.when(pl.program_id(0) == 0)
        def _():
            s_scr[...] = jnp.zeros_like(s_scr)

        nh = GDN_HEADS
        ba = ba_ref[...]
        g = _gdn_g(ba, al_ref[...], dt_ref[...])
        gcums = [_chunk_cumsum(g[c * ci:c * (ci + 1)]) for ci in range(cps)]
        loc = _gdn_local(*_gdn_inputs(qkv_ref, ba, gcums, cps))
        for ci in range(cps):
            rs = slice(c * ci, c * (ci + 1))
            sl = slice(nh * ci, nh * (ci + 1))
            s_in = s_scr[...]
            sall_ref[ci] = s_in
            zz = jnp.stack([z_ref[rs, hd * h:hd * (h + 1)] for h in range(nh)])
            y, s_out = _gdn_recur(s_in, *(t[sl] for t in loc), zz, ng_ref[...])
            s_scr[...] = s_out
            for h in range(nh):
                y_ref[rs, hd * h:hd * (h + 1)] = y[h].astype(BF16)

    return pl.pallas_call(
        body, name=name, grid=(nc // cps,),
        in_specs=[pl.BlockSpec((tb, 1536), lambda i: (i, 0)), _pblk(tb, 512, C_Z), _pblk(tb, 128, C_BA),
                  _full((1, 128)), _full((1, 128)), _full((1, 128))],
        out_specs=[pl.BlockSpec((tb, 512), lambda i: (i, 0)),
                   pl.BlockSpec((cps, GDN_HEADS, hd, hd), lambda i: (i, 0, 0, 0))],
        out_shape=[jax.ShapeDtypeStruct((s, 512), BF16), jax.ShapeDtypeStruct((nc, GDN_HEADS, hd, hd), F32)],
        scratch_shapes=[pltpu.VMEM((GDN_HEADS, hd, hd), F32)],
        compiler_params=_params("arbitrary"),
    )(qkv, proj, proj, alog, dt, ng)


def gdn_bwd(qkv, proj, sall, dy, alog, dt, ng, *, cps, name):
    s = qkv.shape[0]
    c = CHUNK
    nc = s // c
    hd = GDN_DK
    tb = c * cps
    nt = nc // cps

    def body(qkv_ref, z_ref, ba_ref, sall_ref, dy_ref, al_ref, dt_ref, ng_ref,
             dqkv_ref, dz_ref, dba_ref, dal_ref, ddt_ref, dng_ref, ds_scr):
        @pl.when(pl.program_id(0) == 0)
        def _():
            ds_scr[...] = jnp.zeros_like(ds_scr)
            dal_ref[...] = jnp.zeros_like(dal_ref)
            ddt_ref[...] = jnp.zeros_like(ddt_ref)
            dng_ref[...] = jnp.zeros_like(dng_ref)

        nh = GDN_HEADS
        ba = ba_ref[...]
        g, gvf = jax.vjp(_gdn_g, ba, al_ref[...], dt_ref[...])
        gcums = [_chunk_cumsum(g[c * ci:c * (ci + 1)]) for ci in range(cps)]
        loc, loc_vf = jax.vjp(_gdn_local, *_gdn_inputs(qkv_ref, ba, gcums, cps))
        d_loc = [None] * cps
        dng = jnp.zeros((1, 128), F32)
        for ci in reversed(range(cps)):
            rs = slice(c * ci, c * (ci + 1))
            sl = slice(nh * ci, nh * (ci + 1))
            zz = jnp.stack([z_ref[rs, hd * h:hd * (h + 1)] for h in range(nh)])
            dyy = jnp.stack([dy_ref[rs, hd * h:hd * (h + 1)] for h in range(nh)])
            _, rvf = jax.vjp(_gdn_recur, sall_ref[ci], *(t[sl] for t in loc), zz, ng_ref[...])
            ds, du, dwk, dattn, dqg, dkd, deg, dzz, dn = rvf((dyy, ds_scr[...]))
            ds_scr[...] = ds
            d_loc[ci] = (du, dwk, dattn, dqg, dkd, deg)
            dng = dng + dn
            for h in range(nh):
                dz_ref[rs, hd * h:hd * (h + 1)] = dzz[h].astype(BF16)
        dq, dk, dv, dbp, dgc = loc_vf(tuple(jnp.concatenate([d_loc[ci][j] for ci in range(cps)], axis=0)
                                            for j in range(6)))
        dgs, dbas = [], []
        for ci in range(cps):
            rs = slice(c * ci, c * (ci + 1))
            dgcum = jnp.zeros((c, 128), F32)
            dba = jnp.zeros((c, 128), F32)
            for h in range(nh):
                b = nh * ci + h
                dqkv_ref[rs, hd * h:hd * (h + 1)] = dq[b]
                dqkv_ref[rs, 512 + hd * h:512 + hd * (h + 1)] = dk[b]
                dqkv_ref[rs, 1024 + hd * h:1024 + hd * (h + 1)] = dv[b]
                dgcum = dgcum + _put_col(dgc[b], 4 + h, 128)
                dba = dba + _put_col(dbp[b], h, 128)
            dgs.append(_chunk_cumsum(dgcum, transpose=True))
            dbas.append(dba)
        dba2, dal, ddt = gvf(jnp.concatenate(dgs, axis=0))
        dba_ref[...] = (jnp.concatenate(dbas, axis=0) + dba2).astype(BF16)
        dal_ref[...] += dal
        ddt_ref[...] += ddt
        dng_ref[...] += dng

    def rev(width):
        return pl.BlockSpec((tb, width), lambda i: (nt - 1 - i, 0))

    return pl.pallas_call(
        body, name=name, grid=(nt,),
        in_specs=[rev(1536), _pblk(tb, 512, C_Z, rev_n=nt), _pblk(tb, 128, C_BA, rev_n=nt),
                  pl.BlockSpec((cps, GDN_HEADS, hd, hd), lambda i: (nt - 1 - i, 0, 0, 0)), rev(512),
                  _full((1, 128)), _full((1, 128)), _full((1, 128))],
        out_specs=[rev(1536), rev(512), rev(128), _full((1, 128)), _full((1, 128)), _full((1, 128))],
        out_shape=[jax.ShapeDtypeStruct((s, 1536), F32), jax.ShapeDtypeStruct((s, 512), BF16),
                   jax.ShapeDtypeStruct((s, 128), BF16), jax.ShapeDtypeStruct((1, 128), F32),
                   jax.ShapeDtypeStruct((1, 128), F32), jax.ShapeDtypeStruct((1, 128), F32)],
        scratch_shapes=[pltpu.VMEM((GDN_HEADS, hd, hd), F32)],
        compiler_params=_params("arbitrary"),
    )(qkv, proj, proj, sall, dy, alog, dt, ng)


def merge_fwd(x, ys, proj, wb, wo, *, tb, dep=None, name):
    s, d = x.shape
    dep_specs, dep_args = _dep_operand(dep)

    def body(x_ref, ya, yb, yc, yd, g0, g1, g2, g3, wb_ref, wo_ref, *rest):
        o_ref, m_ref = rest[len(dep_args):]
        m = jnp.zeros((tb, d), F32)
        for g, (y_ref, gp_ref) in enumerate(zip((ya, yb, yc, yd), (g0, g1, g2, g3))):
            m = m + jax.nn.sigmoid(gp_ref[...]) * _dg(y_ref[...], wb_ref[g], 1, 0)
        mb = m.astype(BF16)
        m_ref[...] = mb
        o_ref[...] = x_ref[...] + _dg(mb, wo_ref[...], 1, 0)

    row = pl.BlockSpec((tb, d), lambda i: (i, 0))
    yrow = pl.BlockSpec((tb, BRANCH_W), lambda i: (i, 0))
    return pl.pallas_call(
        body, name=name, grid=(s // tb,),
        in_specs=[row] + [yrow] * 4 + [_pblk(tb, d, C_GATE + d * g) for g in range(4)]
        + [_full((4, BRANCH_W, d)), _full((d, d))] + dep_specs,
        out_specs=[row, row],
        out_shape=[jax.ShapeDtypeStruct((s, d), F32), jax.ShapeDtypeStruct((s, d), BF16)],
        compiler_params=_params("parallel"),
    )(x, *ys, proj, proj, proj, proj, wb, wo, *dep_args)


def merge_bwd(dx, ys, proj, wb, wo, *, tb, dep=None, name):
    s, d = dx.shape
    dep_specs, dep_args = _dep_operand(dep)

    def body(dx_ref, ya, yb, yc, yd, g0, g1, g2, g3, wb_ref, wo_ref, *rest):
        dgp_ref, dbr_ref, da, db, dc, dd, dxb_ref = rest[len(dep_args):]
        dxb = dx_ref[...].astype(BF16)
        dxb_ref[...] = dxb
        dm = _dg(dxb, wo_ref[...], 1, 1)
        for g, (y_ref, gp_ref, dy_ref) in enumerate(zip((ya, yb, yc, yd), (g0, g1, g2, g3), (da, db, dc, dd))):
            br = _dg(y_ref[...], wb_ref[g], 1, 0)
            sg = jax.nn.sigmoid(gp_ref[...])
            dgp_ref[:, d * g:d * (g + 1)] = (dm * br * sg * (1.0 - sg)).astype(BF16)
            dbr = (dm * sg).astype(BF16)
            dbr_ref[g] = dbr
            dy_ref[...] = _dg(dbr, wb_ref[g], 1, 1)

    row = pl.BlockSpec((tb, d), lambda i: (i, 0))
    yrow = pl.BlockSpec((tb, BRANCH_W), lambda i: (i, 0))
    return pl.pallas_call(
        body, name=name, grid=(s // tb,),
        in_specs=[row] + [yrow] * 4 + [_pblk(tb, d, C_GATE + d * g) for g in range(4)]
        + [_full((4, BRANCH_W, d)), _full((d, d))] + dep_specs,
        out_specs=[pl.BlockSpec((tb, 4 * d), lambda i: (i, 0)), pl.BlockSpec((4, tb, d), lambda i: (0, i, 0)),
                   yrow, yrow, yrow, yrow, row],
        out_shape=[jax.ShapeDtypeStruct((s, 4 * d), BF16), jax.ShapeDtypeStruct((4, s, d), BF16)]
        + [jax.ShapeDtypeStruct((s, BRANCH_W), F32)] * 4 + [jax.ShapeDtypeStruct((s, d), BF16)],
        compiler_params=_params("parallel"),
    )(dx, *ys, proj, proj, proj, proj, wb, wo, *dep_args)


def loss_head(x, g, target, *, tb, name):
    s, d = x.shape

    def body(x_ref, g_ref, t_ref, dx_ref, dg_ref, loss_ref):
        @pl.when(pl.program_id(0) == 0)
        def _():
            dg_ref[...] = jnp.zeros_like(dg_ref)
            loss_ref[...] = jnp.zeros_like(loss_ref)

        xv, gv = x_ref[...], g_ref[...]
        r = lax.rsqrt(jnp.mean(xv * xv, axis=-1, keepdims=True) + EPS)
        err = xv * r * gv - t_ref[...]
        loss_ref[...] += 0.5 * jnp.sum(jnp.mean(err * err, axis=-1, keepdims=True), axis=0, keepdims=True)
        dx, dg = _rms_bwd(xv, gv, err * (1.0 / d))
        dx_ref[...] = dx
        dg_ref[...] += dg

    row = pl.BlockSpec((tb, d), lambda i: (i, 0))
    return pl.pallas_call(
        body, name=name, grid=(s // tb,),
        in_specs=[row, _full((1, d)), row],
        out_specs=[row, _full((1, d)), _full((1, 128))],
        out_shape=[jax.ShapeDtypeStruct((s, d), F32), jax.ShapeDtypeStruct((1, d), F32),
                   jax.ShapeDtypeStruct((1, 128), F32)],
        compiler_params=_params("arbitrary"),
    )(x, g, target)


def adamw(w, g, m, v, *, name):
    shape = w.shape
    lead, (r, c) = shape[:-2], shape[-2:]
    tr = r
    while tr * c * 4 > (1 << 20) and tr % 16 == 0:
        tr //= 2
    c1 = 1.0 / (1.0 - ADAM_B1 ** ADAM_STEP)
    c2 = 1.0 / (1.0 - ADAM_B2 ** ADAM_STEP)

    def body(w_ref, g_ref, m_ref, v_ref, d_ref, nm_ref, nv_ref):
        gv = g_ref[...]
        nm = ADAM_B1 * m_ref[...] + (1.0 - ADAM_B1) * gv
        nv = ADAM_B2 * v_ref[...] + (1.0 - ADAM_B2) * (gv * gv)
        nm_ref[...] = nm
        nv_ref[...] = nv
        d_ref[...] = -ADAM_LR * ((nm * c1) / (jnp.sqrt(nv * c2) + ADAM_EPS) + ADAM_WD * w_ref[...])

    blk = pl.BlockSpec((None,) * len(lead) + (tr, c), lambda *i: i + (0,))
    grid = lead + (r // tr,)
    if len(lead) == 1 and r < 8:
        tl = max(t for t in range(1, lead[0] + 1) if lead[0] % t == 0 and t * r * c * 4 <= (1 << 20))
        blk = pl.BlockSpec((tl, r, c), lambda i: (i, 0, 0))
        grid = (lead[0] // tl,)
    return tuple(pl.pallas_call(
        body, name=name, grid=grid,
        in_specs=[blk] * 4, out_specs=[blk] * 3,
        out_shape=[jax.ShapeDtypeStruct(shape, F32)] * 3,
        compiler_params=_params(*(("parallel",) * len(grid))),
    )(w, g, m, v))


_ANY = pl.BlockSpec(memory_space=pl.ANY)


def _place():
    x, y, c = lax.axis_index("x"), lax.axis_index("y"), lax.axis_index("c")
    return x, y, c, [(1 - x, y), (x, 1 - y), (1 - x, 1 - y)]


def gather_shards(bufs, *, name):
    n = len(bufs)

    def body(*refs):
        outs = refs[n:2 * n]
        ici_send, ici_recv, d2d_send, d2d_recv = refs[2 * n:]
        x, y, c, chips = _place()
        me = 2 * x + y

        def ici(a, j, slab, to):
            return pltpu.make_async_remote_copy(
                src_ref=outs[a].at[slab, c], dst_ref=outs[a].at[slab, c], send_sem=ici_send.at[a, j],
                recv_sem=ici_recv.at[a, j], device_id=to, device_id_type=MESH)

        def d2d(a, j, slab, half):
            return pltpu.make_async_remote_copy(
                src_ref=outs[a].at[slab, half], dst_ref=outs[a].at[slab, half], send_sem=d2d_send.at[a, j],
                recv_sem=d2d_recv.at[a, j], device_id=(x, y, 1 - c), device_id_type=MESH)

        sends = []
        for a in range(n):
            for j, (px, py) in enumerate(chips):
                cp = ici(a, j, me, (px, py, c))
                cp.start()
                sends.append(cp)
        for a in range(n):
            for j, (px, py) in enumerate(chips):
                ici(a, j, 2 * px + py, (px, py, c)).wait_recv()
                cp = d2d(a, j, 2 * px + py, c)
                cp.start()
                sends.append(cp)
        for a in range(n):
            for j, (px, py) in enumerate(chips):
                d2d(a, j, 2 * px + py, 1 - c).wait_recv()
        for cp in sends:
            cp.wait_send()

    return pl.pallas_call(
        body, name=name,
        in_specs=[_ANY] * n, out_specs=[_ANY] * n,
        out_shape=[jax.ShapeDtypeStruct(t.shape, t.dtype) for t in bufs],
        input_output_aliases={a: a for a in range(n)},
        scratch_shapes=[pltpu.SemaphoreType.DMA((n, 3))] * 4,
    )(*bufs)


_SEM = pl.BlockSpec(memory_space=pltpu.SEMAPHORE)
_VMEM = pl.BlockSpec(memory_space=pltpu.VMEM)
_EFFECT = pltpu.SideEffectType.DATAFLOW_SIDE_EFFECTING
_TOKEN = jax.ShapeDtypeStruct((8, 128), F32)


def _gather_copies(outs, sems_ici, sems_d2d):
    x, y, c, chips = _place()

    def ici(a, j, slab, to):
        return pltpu.make_async_remote_copy(
            src_ref=outs[a].at[slab, c], dst_ref=outs[a].at[slab, c], send_sem=sems_ici[0][3 * a + j],
            recv_sem=sems_ici[1][3 * a + j], device_id=to, device_id_type=MESH)

    def d2d(a, j, slab, half):
        return pltpu.make_async_remote_copy(
            src_ref=outs[a].at[slab, half], dst_ref=outs[a].at[slab, half], send_sem=sems_d2d[0][3 * a + j],
            recv_sem=sems_d2d[1][3 * a + j], device_id=(x, y, 1 - c), device_id_type=MESH)

    return x, y, c, chips, ici, d2d


def _async_call(body, bufs, sems_in, after, sems_out, token, *, name):
    n = len(bufs)
    n_out = 6 * n if sems_out else 0
    extra = [] if after is None else [after]
    return pl.pallas_call(
        body, name=name,
        in_specs=[_ANY] * n + [_SEM] * len(sems_in) + [_ANY] * len(extra),
        out_specs=[_ANY] * n + [_SEM] * n_out + [_VMEM] * token,
        out_shape=[jax.ShapeDtypeStruct(t.shape, t.dtype) for t in bufs]
        + [pltpu.SemaphoreType.DMA(())] * n_out + [_TOKEN] * token,
        input_output_aliases={a: a for a in range(n)},
        compiler_params=pltpu.CompilerParams(has_side_effects=_EFFECT),
    )(*bufs, *sems_in, *extra)


def gather_start(bufs, after=None, *, name):
    n = len(bufs)
    k = n + (after is not None)

    def body(*refs):
        outs = refs[k:k + n]
        sems, token = refs[k + n:k + 7 * n], refs[k + 7 * n]
        x, y, c, chips, ici, _ = _gather_copies(outs, (sems[:3 * n], sems[3 * n:]), None)
        for a in range(n):
            for j, (px, py) in enumerate(chips):
                ici(a, j, 2 * x + y, (px, py, c)).start()
        token[...] = jnp.zeros_like(token)

    res = _async_call(body, bufs, [], after, True, True, name=name)
    return res[:n], res[n:7 * n], res[7 * n]


def gather_pass(bufs, sems, after, *, name):
    n = len(bufs)

    def body(*refs):
        s_in = refs[n:7 * n]
        outs = refs[7 * n + 1:8 * n + 1]
        s_out, token = refs[8 * n + 1:14 * n + 1], refs[14 * n + 1]
        x, y, c, chips, ici, d2d = _gather_copies(outs, (s_in[:3 * n], s_in[3 * n:]),
                                                  (s_out[:3 * n], s_out[3 * n:]))
        for a in range(n):
            for j, (px, py) in enumerate(chips):
                ici(a, j, 2 * px + py, (px, py, c)).wait_recv()
                d2d(a, j, 2 * px + py, c).start()
        for a in range(n):
            for j, (px, py) in enumerate(chips):
                ici(a, j, 2 * x + y, (px, py, c)).wait_send()
        token[...] = jnp.zeros_like(token)

    res = _async_call(body, bufs, list(sems), after, True, True, name=name)
    return res[:n], res[n:7 * n], res[7 * n]


def gather_wait(bufs, sems, after, *, name):
    n = len(bufs)

    def body(*refs):
        s_in = refs[n:7 * n]
        outs = refs[7 * n + 1:]
        x, y, c, chips, _, d2d = _gather_copies(outs, None, (s_in[:3 * n], s_in[3 * n:]))
        for a in range(n):
            for j, (px, py) in enumerate(chips):
                d2d(a, j, 2 * px + py, 1 - c).wait_recv()
                d2d(a, j, 2 * px + py, c).wait_send()

    return _async_call(body, bufs, list(sems), after, False, False, name=name)


def _pair_plan(srcs, lands):
    x, y, c, _ = _place()
    return [(srcs[a].at[j, 1 - c], lands[a].at[j], (x, y, 1 - c))
            for a in range(len(srcs)) for j in range(N_SHARD)]


def _chip_plan(srcs, lands):
    x, y, c, chips = _place()
    return [(srcs[a].at[2 * px + py], lands[a].at[2 * x + y], (px, py, c))
            for a in range(len(srcs)) for (px, py) in chips]


def copies_start(arrs, land_shapes, plan, per_array, *, name):
    n = len(arrs)
    k = per_array * n

    def body(*refs):
        lands = refs[2 * n:3 * n]
        sems, token = refs[3 * n:3 * n + 2 * k], refs[3 * n + 2 * k]
        for i, (src, dst, dev) in enumerate(plan(refs[n:2 * n], lands)):
            pltpu.make_async_remote_copy(src_ref=src, dst_ref=dst, send_sem=sems[i], recv_sem=sems[k + i],
                                         device_id=dev, device_id_type=MESH).start()
        token[...] = jnp.zeros_like(token)

    res = pl.pallas_call(
        body, name=name,
        in_specs=[_ANY] * n,
        out_specs=[_ANY] * (2 * n) + [_SEM] * (2 * k) + [_VMEM],
        out_shape=[jax.ShapeDtypeStruct(t.shape, t.dtype) for t in arrs] + list(land_shapes)
        + [pltpu.SemaphoreType.DMA(())] * (2 * k) + [_TOKEN],
        input_output_aliases={a: a for a in range(n)},
        compiler_params=pltpu.CompilerParams(has_side_effects=_EFFECT),
    )(*arrs)
    return res[:n], res[n:2 * n], res[2 * n:2 * n + 2 * k], res[2 * n + 2 * k]


def copies_wait(arrs, lands, sems, after, plan, *, name):
    n = len(arrs)
    k = len(sems) // 2

    def body(*refs):
        s_in = refs[2 * n:2 * n + 2 * k]
        outs = refs[2 * n + 2 * k + 1:]
        for i, (src, dst, dev) in enumerate(plan(outs[:n], outs[n:])):
            cp = pltpu.make_async_remote_copy(src_ref=src, dst_ref=dst, send_sem=s_in[i], recv_sem=s_in[k + i],
                                              device_id=dev, device_id_type=MESH)
            cp.wait_send()
            cp.wait_recv()

    res = pl.pallas_call(
        body, name=name,
        in_specs=[_ANY] * (2 * n) + [_SEM] * (2 * k) + [_ANY],
        out_specs=[_ANY] * (2 * n),
        out_shape=[jax.ShapeDtypeStruct(t.shape, t.dtype) for t in list(arrs) + list(lands)],
        input_output_aliases={a: a for a in range(2 * n)},
        compiler_params=pltpu.CompilerParams(has_side_effects=_EFFECT),
    )(*arrs, *lands, *sems, after)
    return res[:n], res[n:]


def pair_send_halves(arrs, *, name):
    n = len(arrs)

    def body(*refs):
        ins, outs = refs[:n], refs[n:2 * n]
        send_sems, recv_sems = refs[2 * n:]
        x, y, c, _ = _place()
        copies = []
        for a in range(n):
            for j in range(N_SHARD):
                cp = pltpu.make_async_remote_copy(
                    src_ref=ins[a].at[j, 1 - c], dst_ref=outs[a].at[j], send_sem=send_sems.at[a, j],
                    recv_sem=recv_sems.at[a, j], device_id=(x, y, 1 - c), device_id_type=MESH)
                cp.start()
                copies.append(cp)
        for cp in copies:
            cp.wait()

    return pl.pallas_call(
        body, name=name,
        in_specs=[_ANY] * n, out_specs=[_ANY] * n,
        out_shape=[jax.ShapeDtypeStruct((N_SHARD,) + t.shape[2:], t.dtype) for t in arrs],
        scratch_shapes=[pltpu.SemaphoreType.DMA((n, N_SHARD)), pltpu.SemaphoreType.DMA((n, N_SHARD))],
    )(*arrs)


def chip_scatter(arrs, *, name):
    n = len(arrs)

    def body(*refs):
        ins, outs = refs[:n], refs[n:2 * n]
        send_sems, recv_sems = refs[2 * n:]
        x, y, c, chips = _place()
        me = 2 * x + y
        copies = []
        for a in range(n):
            for j, (px, py) in enumerate(chips):
                cp = pltpu.make_async_remote_copy(
                    src_ref=ins[a].at[2 * px + py], dst_ref=outs[a].at[me], send_sem=send_sems.at[a, j],
                    recv_sem=recv_sems.at[a, j], device_id=(px, py, c), device_id_type=MESH)
                cp.start()
                copies.append(cp)
        for cp in copies:
            cp.wait()

    return pl.pallas_call(
        body, name=name,
        in_specs=[_ANY] * n, out_specs=[_ANY] * n,
        out_shape=[jax.ShapeDtypeStruct(t.shape, t.dtype) for t in arrs],
        scratch_shapes=[pltpu.SemaphoreType.DMA((n, 3)), pltpu.SemaphoreType.DMA((n, 3))],
    )(*arrs)


def pair_gather(bufs, *, dep=None, name):
    n = len(bufs)
    dep_specs, dep_args = _dep_operand(dep)
    k = n + len(dep_args)

    def body(*refs):
        outs = refs[k:k + n]
        send_sems, recv_sems = refs[k + n:]
        x, y, c, _ = _place()
        copies = []
        for a in range(n):
            cp = pltpu.make_async_remote_copy(
                src_ref=outs[a].at[c], dst_ref=outs[a].at[c], send_sem=send_sems.at[a],
                recv_sem=recv_sems.at[a], device_id=(x, y, 1 - c), device_id_type=MESH)
            cp.start()
            copies.append(cp)
        for cp in copies:
            cp.wait()

    return pl.pallas_call(
        body, name=name,
        in_specs=[_ANY] * n + dep_specs, out_specs=[_ANY] * n,
        out_shape=[jax.ShapeDtypeStruct(t.shape, t.dtype) for t in bufs],
        input_output_aliases={a: a for a in range(n)},
        scratch_shapes=[pltpu.SemaphoreType.DMA((n,)), pltpu.SemaphoreType.DMA((n,))],
    )(*bufs, *dep_args)


def all_sum_small(v, *, name):
    m = v.shape[0]

    def body(v_ref, o_ref, buf, send_sems, recv_sems, local_sem):
        x, y, c, chips = _place()
        me, sibling = (x, y, c), (x, y, 1 - c)

        def rows(px, py, pc):
            return buf.at[pl.ds((4 * px + 2 * py + pc) * m, m), :]

        def copy(k, block, to, src=None):
            return pltpu.make_async_remote_copy(
                src_ref=rows(*block) if src is None else src, dst_ref=rows(*block),
                send_sem=send_sems.at[k], recv_sem=recv_sems.at[k], device_id=to, device_id_type=MESH)

        mine = pltpu.make_async_copy(v_ref, rows(*me), local_sem)
        mine.start()
        first = [copy(0, me, sibling, src=v_ref)]
        first += [copy(1 + j, me, (*chip, c), src=v_ref) for j, chip in enumerate(chips)]
        for cp in first:
            cp.start()
        passed = [copy(4 + j, (*chip, c), sibling) for j, chip in enumerate(chips)]
        for j, chip in enumerate(chips):
            copy(1 + j, (*chip, c), me).wait_recv()
            passed[j].start()
        copy(0, sibling, me).wait_recv()
        for j, chip in enumerate(chips):
            copy(4 + j, (*chip, 1 - c), me).wait_recv()
        for cp in first + passed:
            cp.wait_send()
        mine.wait()
        acc = buf[pl.ds(0, m), :]
        for k in range(1, 8):
            acc = acc + buf[pl.ds(k * m, m), :]
        o_ref[...] = acc

    vm = pl.BlockSpec(memory_space=pltpu.VMEM)
    return pl.pallas_call(
        body, name=name, in_specs=[vm], out_specs=vm,
        out_shape=jax.ShapeDtypeStruct((m, 128), F32),
        scratch_shapes=[pltpu.VMEM((8 * m, 128), F32), pltpu.SemaphoreType.DMA((7,)),
                        pltpu.SemaphoreType.DMA((7,)), pltpu.SemaphoreType.DMA],
    )(v)


def pack_slab(srcs, blocks, maps, n_slots, slot_shape, *, place, dep=None, name):
    k = len(srcs)
    per = n_slots // k
    dep_specs, dep_args = _dep_operand(dep)

    def body(place_ref, *refs):
        o_ref = refs[k + len(dep_args)]
        s = pl.program_id(0)
        for i in range(k):
            @pl.when(jnp.logical_and(s >= i * per, s < (i + 1) * per))
            def _():
                o_ref[...] = refs[i][...].astype(BF16)

    return pl.pallas_call(
        body, name=name,
        grid_spec=pltpu.PrefetchScalarGridSpec(
            num_scalar_prefetch=1, grid=(n_slots,),
            in_specs=[pl.BlockSpec(blk, functools.partial(lambda s, pr, m: m(s), m=m)) for blk, m in zip(blocks, maps)]
            + dep_specs,
            out_specs=pl.BlockSpec((None, None) + slot_shape, lambda s, pr: (pr[1], s, 0, 0))),
        out_shape=jax.ShapeDtypeStruct((N_SHARD, n_slots) + slot_shape, BF16),
        compiler_params=_params("arbitrary"),
    )(place, *srcs, *dep_args)


def sum_halves(arr, recv, c_idx, *, name):
    _, _, r, c = arr.shape
    tr = r
    while tr * c * 4 > (1 << 20) and tr % 32 == 0:
        tr //= 2

    def body(c_ref, a_ref, b_ref, o_ref):
        o_ref[...] = (a_ref[...] + b_ref[...]).astype(BF16)

    return pl.pallas_call(
        body, name=name,
        grid_spec=pltpu.PrefetchScalarGridSpec(
            num_scalar_prefetch=1, grid=(N_SHARD, r // tr),
            in_specs=[pl.BlockSpec((None, None, tr, c), lambda j, i, cr: (j, cr[0], i, 0)),
                      pl.BlockSpec((None, tr, c), lambda j, i, cr: (j, i, 0))],
            out_specs=pl.BlockSpec((None, tr, c), lambda j, i, cr: (j, i, 0))),
        out_shape=jax.ShapeDtypeStruct((N_SHARD, r, c), BF16),
        compiler_params=_params("parallel", "parallel"),
    )(c_idx, arr, recv)


def sum_chips(recv, part, place, *, name):
    _, r, c = recv.shape
    tr = r
    while tr * c * 4 > (1 << 20) and tr % 32 == 0:
        tr //= 2

    def body(place_ref, a0, a1, a2, a3, p_ref, o_ref):
        me = place_ref[1]
        acc = None
        for k, a_ref in enumerate((a0, a1, a2, a3)):
            term = jnp.where(me == k, p_ref[...], a_ref[...]).astype(F32)
            acc = term if acc is None else acc + term
        o_ref[...] = acc

    def slab(k):
        return pl.BlockSpec((None, tr, c), lambda i, pr: (jnp.where(pr[1] == k, (k + 1) % N_SHARD, k), i, 0))

    return pl.pallas_call(
        body, name=name,
        grid_spec=pltpu.PrefetchScalarGridSpec(
            num_scalar_prefetch=1, grid=(r // tr,),
            in_specs=[slab(k) for k in range(N_SHARD)]
            + [pl.BlockSpec((None, tr, c), lambda i, pr: (pr[1], i, 0))],
            out_specs=pl.BlockSpec((None, tr, c), lambda i, pr: (pr[0], i, 0))),
        out_shape=jax.ShapeDtypeStruct((2, r, c), F32),
        compiler_params=_params("parallel"),
    )(place, recv, recv, recv, recv, part)


_WEIGHTS = ("ff1_norm", "ff1_wg", "ff1_wu", "ff1_wd", "mix_norm", "w_in", "sgu_ln_g", "sgu_ln_b", "sgu_w",
            "sgu_b", "lru_conv_w", "lru_conv_b", "lru_wa", "lru_ba", "lru_wx", "lru_bx", "lru_lambda",
            "gdn_conv_w", "gdn_a_log", "gdn_dt_bias", "gdn_norm_g", "pool_w", "pool_scale", "w_branch",
            "w_out", "ff2_norm", "ff2_wg", "ff2_wu", "ff2_wd", "final_norm")
_BIG = ("ff1_wg", "ff1_wu", "ff1_wd", "w_in", "w_branch", "w_out", "ff2_wg", "ff2_wu", "ff2_wd")
_SMALL = tuple(n for n in _WEIGHTS if n not in _BIG)
_CONV = ("lru_conv_w", "gdn_conv_w")


def _seg_rows(shape):
    return -(-math.prod(shape) // 1024) * 8


def _pack(arrs):
    segs = []
    for t in arrs:
        rows = _seg_rows(t.shape)
        flat = t.reshape(-1)
        segs.append(jnp.pad(flat, (0, rows * 128 - flat.shape[0])).reshape(rows, 128))
    return jnp.concatenate(segs, axis=0)


def _unpack(buf, shapes):
    out, o = [], 0
    for shp in shapes:
        rows = _seg_rows(shp)
        out.append(buf[o:o + rows].reshape(-1)[:math.prod(shp)].reshape(shp))
        o += rows
    return out


def _block_diag(w):
    h, n, _ = w.shape
    same = jnp.arange(h)[:, None, None, None] == jnp.arange(h)[None, None, :, None]
    return jnp.where(same, w[:, :, None, :], 0.0).reshape(h * n, h * n)


def _diag_blocks(m, h=8, n=64):
    return jnp.stack([m[i * n:(i + 1) * n, i * n:(i + 1) * n] for i in range(h)])


def _lane_row(v):
    return jnp.zeros((1, 128), F32).at[0, 4:8].set(v)


def _w_in_to_padded(raw):
    sem = jnp.concatenate([raw[k, :, :P_IN_SHARD] for k in range(N_SHARD)], axis=1)
    zero = jnp.zeros((sem.shape[0], C_GATE - C_BA - 8), sem.dtype)
    return jnp.concatenate([sem[:, 2048:4096], sem[:, 0:2048], sem[:, 4104:4616], sem[:, 4096:4104], zero,
                            sem[:, 4616:P_IN]], axis=1)


def _w_in_grad_to_shards(part):
    r = part.shape[1]
    dwp = part.transpose(1, 0, 2).reshape(r, P_PAD)
    sem = jnp.concatenate([dwp[:, C_AU:C_DX], dwp[:, 0:C_AU], dwp[:, C_BA:C_BA + 8], dwp[:, C_DX:C_BA],
                           dwp[:, C_GATE:]], axis=1)
    sh = sem.reshape(r, N_SHARD, P_IN_SHARD).transpose(1, 0, 2)
    return jnp.pad(sh, ((0, 0), (0, 0), (0, P_IN_SHARD_PAD - P_IN_SHARD)))


def kernel(x, ff1_norm, ff1_wg, ff1_wu, ff1_wd, mix_norm, w_in, sgu_ln_g, sgu_ln_b, sgu_w, sgu_b, lru_conv_w,
           lru_conv_b, lru_wa, lru_ba, lru_wx, lru_bx, lru_lambda, gdn_conv_w, gdn_a_log, gdn_dt_bias, gdn_norm_g,
           pool_w, pool_scale, w_branch, w_out, ff2_norm, ff2_wg, ff2_wu, ff2_wd, final_norm, loss_target, m_ff1_norm,
           m_ff1_wg, m_ff1_wu, m_ff1_wd, m_mix_norm, m_w_in, m_sgu_ln_g, m_sgu_ln_b, m_sgu_w, m_sgu_b, m_lru_conv_w,
           m_lru_conv_b, m_lru_wa, m_lru_ba, m_lru_wx, m_lru_bx, m_lru_lambda, m_gdn_conv_w, m_gdn_a_log, m_gdn_dt_bias,
           m_gdn_norm_g, m_pool_w, m_pool_scale, m_w_branch, m_w_out, m_ff2_norm, m_ff2_wg, m_ff2_wu, m_ff2_wd,
           m_final_norm, v_ff1_norm, v_ff1_wg, v_ff1_wu, v_ff1_wd, v_mix_norm, v_w_in, v_sgu_ln_g, v_sgu_ln_b, v_sgu_w,
           v_sgu_b, v_lru_conv_w, v_lru_conv_b, v_lru_wa, v_lru_ba, v_lru_wx, v_lru_bx, v_lru_lambda, v_gdn_conv_w,
           v_gdn_a_log, v_gdn_dt_bias, v_gdn_norm_g, v_pool_w, v_pool_scale, v_w_branch, v_w_out, v_ff2_norm, v_ff2_wg,
           v_ff2_wu, v_ff2_wd, v_final_norm):
    a = dict(locals())
    n_layer = ff1_norm.shape[0]
    d = D_MODEL
    x0 = x[0]
    chip = 2 * lax.axis_index("x") + lax.axis_index("y")
    c_idx = jnp.reshape(lax.axis_index("c"), (1,)).astype(jnp.int32)
    place = jnp.stack([lax.axis_index("c"), chip]).astype(jnp.int32)

    def own_slab(t, dtype=BF16):
        buf = lax.empty((N_SHARD,) + t.shape, dtype)
        return lax.dynamic_update_slice(buf, t.astype(dtype)[None], (chip,) + (0,) * t.ndim)

    assert n_layer == 2

    def layer_bufs(l, dep=None):
        fc = FF_SHARD
        whole = lambda s: (l, 0, 0)
        rows = lambda s: (l, s, 0)
        pack = functools.partial(pack_slab, place=place, dep=dep)
        bufs = []
        for tag, wg, wu, wd in (("ffn1", ff1_wg, ff1_wu, ff1_wd), ("ffn2", ff2_wg, ff2_wu, ff2_wd)):
            bufs.append([pack([wg, wu], [(None, d, fc)] * 2, [whole] * 2, 2, (d, fc), name=f"pack_{tag}_wgu_{l}"),
                         pack([wd], [(None, fc // 2, d)], [rows], 2, (fc // 2, d), name=f"pack_{tag}_wd_{l}")])
        win = pack([w_in], [(None, d // 2, P_IN_SHARD_PAD)], [rows], 2, (d // 2, P_IN_SHARD_PAD),
                   name=f"pack_w_in_{l}")
        wb = pack([w_branch], [(None, None, BRANCH_W, d // 4)], [lambda s: (l, s, 0, 0)], 4, (BRANCH_W, d // 4),
                  name=f"pack_wb_{l}")
        wo = pack([w_out], [(None, d // 8, d)], [rows], 2, (d // 8, d), name=f"pack_wo_{l}")
        return bufs[0] + [win, wb.reshape(N_SHARD, 2, 2, BRANCH_W, d // 4), wo] + bufs[1]

    n_buf = 7
    groups = {"ffn1": (0, 2), "mix": (2, 5), "ffn2": (5, 7)}

    def group_sems(sems, lo, hi):
        return list(sems[3 * lo:3 * hi]) + list(sems[3 * n_buf + 3 * lo:3 * n_buf + 3 * hi])

    def ffn_weights(wgu, wd):
        return (wgu, wgu, wd.reshape(N_SHARD, 1, FF_SHARD, d), (0, 1, 0))

    def mix_weights(g_w_in, g_wb, g_wo):
        return {"w_in": _w_in_to_padded(g_w_in.reshape(4, d, P_IN_SHARD_PAD)),
                "wb": g_wb.reshape(4, 4, BRANCH_W, d // 4).transpose(1, 2, 0, 3).reshape(4, BRANCH_W, d),
                "wo": g_wo.reshape(d, d)}

    g_lcw, g_gcw = gather_shards([own_slab(lru_conv_w, F32), own_slab(gdn_conv_w, F32)], name="gather_conv_weights")

    def small_params(l):
        return {"lcw": g_lcw[:, l].transpose(1, 0, 2).reshape(4, BRANCH_W),
                "gcw": g_gcw[:, l].transpose(1, 0, 2).reshape(4, 3 * BRANCH_W),
                "lcb": lru_conv_b[l][None],
                "bt": jnp.zeros((128, 128), F32).at[:, :4].set(sgu_b[l].T),
                "wa": _block_diag(lru_wa[l]), "wx": _block_diag(lru_wx[l]),
                "alog": _lane_row(gdn_a_log[l]), "dt": _lane_row(gdn_dt_bias[l])}

    flying = {0: gather_start(layer_bufs(0), name="gather_weights_0_start")[:2]}
    handed = {}

    def hand_on(l, grp, after):
        lo, hi = groups[grp]
        bufs_l, sems_l = flying[l]
        got, sm, tk = gather_pass(bufs_l[lo:hi], group_sems(sems_l, lo, hi), after,
                                  name=f"gather_weights_{l}_{grp}_pass")
        handed[l, grp] = (got, sm)
        return tk

    def arrived(l, grp, after):
        return gather_wait(*handed.pop((l, grp)), after, name=f"gather_weights_{l}_{grp}_wait")

    saved = []
    xs = x0
    tok = hand_on(0, "ffn1", x0)
    for l in range(n_layer):
        p = small_params(l)
        sv = {"p": p, "x0": xs}
        p["ff1"] = ffn_weights(*arrived(l, "ffn1", tok))
        x1 = ffn_fwd(xs, ff1_norm[l][None], *p["ff1"], tb=512, name=f"ffn1_fwd_{l}")
        tok = hand_on(l, "mix", x1)
        if l == 0:
            bufs1 = layer_bufs(1, dep=tok)
            split = groups["ffn1"][1]
            first1 = gather_start(bufs1[:split], name="gather_weights_1_start_ffn1")
        h = norm_fwd(x1, mix_norm[l][None], tb=512, name=f"mix_norm_fwd_{l}")
        p.update(mix_weights(*arrived(l, "mix", h)))
        proj = matmul(h, p["w_in"], tm=1024, tn=1536, tk=1024, name=f"proj_{l}")
        ya = sgu_fwd(proj, sgu_ln_g[l][None], sgu_ln_b[l][None], sgu_w[l], p["bt"], tb=512, name=f"sgu_fwd_{l}")
        xc = conv_fwd(proj, C_BX, BRANCH_W, p["lcw"], p["lcb"], tb=512, name=f"lru_conv_fwd_{l}")
        yb, hl = lru_fwd(xc, proj, p["wa"], p["wx"], lru_ba[l][None], lru_bx[l][None], lru_lambda[l][None],
                         tb=256, name=f"lru_fwd_{l}")
        hand_on(l, "ffn2", yb)
        if l == 0:
            rest1 = gather_start(bufs1[split:], yb, name="gather_weights_1_start_rest")
            k1, k2 = 3 * split, 3 * (n_buf - split)
            flying[1] = (list(first1[0]) + list(rest1[0]),
                         list(first1[1][:k1]) + list(rest1[1][:k2]) + list(first1[1][k1:]) + list(rest1[1][k2:]))
        qkv = conv_fwd(proj, C_QKV, 3 * BRANCH_W, p["gcw"], jnp.zeros((1, 3 * BRANCH_W), F32), tb=512,
                       name=f"gdn_conv_fwd_{l}")
        yc, sall = gdn_fwd(qkv, proj, p["alog"], p["dt"], gdn_norm_g[l][None], cps=GDN_CPS, name=f"gdn_fwd_{l}")
        p["ff2"] = ffn_weights(*arrived(l, "ffn2", yc))
        dep2 = None
        if l + 1 < n_layer:
            dep2 = tok = hand_on(l + 1, "ffn1", rest1[2])
        yd = pool_fwd(proj, pool_w[l], pool_scale[l][None], tb=512, name=f"pool_fwd_{l}")
        ys = (ya, yb, yc, yd)
        x2, merged = merge_fwd(x1, ys, proj, p["wb"], p["wo"], tb=256, dep=dep2, name=f"merge_fwd_{l}")
        x3 = ffn_fwd(x2, ff2_norm[l][None], *p["ff2"], tb=512, name=f"ffn2_fwd_{l}")
        tok = x3
        sv.update(x1=x1, h=h, proj=proj, xc=xc, hl=hl, qkv=qkv, sall=sall, ys=ys, merged=merged, x2=x2)
        saved.append(sv)
        xs = x3

    dx, d_final, loss_blk = loss_head(xs, final_norm[None], loss_target[0], tb=512, name="loss_head")
    loss = lax.psum(loss_blk[0, 0], ("x", "y", "c"))

    small_g = {n: [None] * n_layer for n in _SMALL if n != "final_norm"}
    big_g = [None] * n_layer
    per_layer = 7
    flight = None

    def pair_sums(bigs, recv, tag, w_in_at):
        part = [sum_halves(t, r, c_idx, name=f"grad_pair_sum_{tag}_{i}") for i, (t, r) in enumerate(zip(bigs, recv))]
        if w_in_at is not None:
            part[w_in_at] = _w_in_grad_to_shards(part[w_in_at])
        return part

    def pair_start(arrs, tag):
        return copies_start(arrs, [jax.ShapeDtypeStruct((N_SHARD,) + t.shape[2:], t.dtype) for t in arrs],
                            _pair_plan, N_SHARD, name=f"grad_pair_exchange_{tag}_start")

    def pair_to_chip(fl, after, tag, w_in_at=None):
        sent, recv = copies_wait(*fl[:3], after, _pair_plan, name=f"grad_pair_exchange_{tag}_wait")
        part = pair_sums(sent, recv, tag, w_in_at)
        return copies_start(part, [jax.ShapeDtypeStruct(t.shape, t.dtype) for t in part], _chip_plan, 3,
                            name=f"grad_chip_exchange_{tag}_start")

    def chip_finish(fl, after, tag):
        part, recv = copies_wait(*fl[:3], after, _chip_plan, name=f"grad_chip_exchange_{tag}_wait")
        return [sum_chips(r, q, place, name=f"grad_chip_sum_{tag}_{i}") for i, (r, q) in enumerate(zip(recv, part))]

    early = mid = None

    for l in reversed(range(n_layer)):
        sv = saved[l]
        p = sv["p"]
        proj = sv["proj"]
        dx2, dg_ff2, dab2, hid2, h2, dy2 = ffn_bwd(sv["x2"], dx, ff2_norm[l][None], *p["ff2"], tb=512,
                                                  dep=None if flight is None else flight[3],
                                                  name=f"ffn2_bwd_{l}")
        dwgu2 = matmul(h2, dab2, ta=True, tm=1024, tn=FF_SHARD, tk=TK_DW, name=f"ffn2_dwgu_{l}")
        dwd2 = matmul(hid2, dy2, ta=True, tm=FF_SHARD, tn=1024, tk=TK_DW, name=f"ffn2_dwd_{l}")
        if flight is not None:
            flight = pair_to_chip(flight, dx2, "1", w_in_at=2)
        dgp, dbr, dya, dyb, dyc, dyd, dx2b = merge_bwd(dx2, sv["ys"], proj, p["wb"], p["wo"], tb=256,
                                                       dep=None if flight is None else flight[3],
                                                       name=f"merge_bwd_{l}")
        dwo = matmul(sv["merged"], dx2b, ta=True, tm=1024, tn=1024, tk=TK_DW, name=f"dwo_{l}")
        dwb = jnp.stack([matmul(sv["ys"][g], dbr[g], ta=True, tm=BRANCH_W, tn=1024, tk=TK_DW, name=f"dwb{g}_{l}")
                         for g in range(4)])
        dwb_sh = dwb.reshape(4, BRANCH_W, N_SHARD, d // N_SHARD).transpose(2, 0, 1, 3)
        g_late = [dwgu2, dwd2.reshape(N_SHARD, 2, FF_SHARD // 2, d),
                  dwb_sh.reshape(N_SHARD, 2, 2 * BRANCH_W, d // N_SHARD), dwo.reshape(N_SHARD, 2, d // 8, d)]
        tok = None
        if l == 0:
            early = pair_start(g_late, "0a")
            tok = early[3]
        du, dv, dlg, dlb, dws, dbt = sgu_bwd(proj, dya, sgu_ln_g[l][None], sgu_ln_b[l][None], sgu_w[l], p["bt"],
                                             tb=512, dep=tok, name=f"sgu_bwd_{l}")
        dxc, dgate, dwa, dwx, dba, dbx, dlam = lru_bwd(sv["xc"], proj, sv["hl"], dyb, p["wa"], p["wx"],
                                                       lru_ba[l][None], lru_bx[l][None], lru_lambda[l][None],
                                                       tb=256, name=f"lru_bwd_{l}")
        dbx_in, dcw_l = conv_bwd(proj, C_BX, BRANCH_W, dxc, p["lcw"], tb=512, name=f"lru_conv_bwd_{l}")
        dqkv, dz, dbeta, dal, ddt, dng = gdn_bwd(sv["qkv"], proj, sv["sall"], dyc, p["alog"], p["dt"],
                                                 gdn_norm_g[l][None], cps=GDN_CPS, name=f"gdn_bwd_{l}")
        dqkv_in, dcw_g = conv_bwd(proj, C_QKV, 3 * BRANCH_W, dqkv, p["gcw"], tb=512, name=f"gdn_conv_bwd_{l}")
        ddx, dwp, dsc = pool_bwd(proj, dyd, pool_w[l], pool_scale[l][None], tb=512, name=f"pool_bwd_{l}")
        if l == 0:
            early = pair_to_chip(early, ddx, "0a")
            tok = early[3]
        s = dx.shape[0]
        dproj = jnp.concatenate([dqkv_in, dz, du, dv, dbx_in, dgate, ddx, dbeta,
                                 jnp.zeros((s, C_GATE - C_BA - 128), BF16), dgp], axis=1)
        dh = matmul(dproj, p["w_in"], tb=True, tm=1024, tn=1024, tk=1536, dep=tok, name=f"dh_mix_{l}")
        dw_in = matmul(sv["h"], dproj, ta=True, tm=1024, tn=1152, tk=TK_DW, col_groups=N_SHARD, name=f"dw_in_{l}")
        dw_in = dw_in.reshape(N_SHARD, 2, d // 2, P_PAD // N_SHARD)
        if l == 0:
            mid = pair_start([dw_in], "0b")
            tok = mid[3]
        dx1, dg_mix = norm_bwd(sv["x1"], mix_norm[l][None], dh, dx2, tb=512, dep=tok, name=f"mix_norm_bwd_{l}")
        dx, dg_ff1, dab1, hid1, h1, dy1 = ffn_bwd(sv["x0"], dx1, ff1_norm[l][None], *p["ff1"],
                                                 tb=512, name=f"ffn1_bwd_{l}")
        if l == 0:
            mid = pair_to_chip(mid, dx, "0b", w_in_at=0)
            tok = mid[3]
        dwgu1 = matmul(h1, dab1, ta=True, tm=1024, tn=FF_SHARD, tk=TK_DW, dep=tok, name=f"ffn1_dwgu_{l}")
        dwd1 = matmul(hid1, dy1, ta=True, tm=FF_SHARD, tn=1024, tk=TK_DW, name=f"ffn1_dwd_{l}")

        for n, g in (("ff1_norm", dg_ff1[0]), ("mix_norm", dg_mix[0]), ("sgu_ln_g", dlg[0]), ("sgu_ln_b", dlb[0]),
                     ("sgu_w", dws), ("sgu_b", dbt[:, :4].T), ("lru_conv_w", dcw_l[:4]), ("lru_conv_b", dcw_l[4]),
                     ("lru_wa", _diag_blocks(dwa)), ("lru_ba", dba[0]), ("lru_wx", _diag_blocks(dwx)),
                     ("lru_bx", dbx[0]), ("lru_lambda", dlam[0]), ("gdn_conv_w", dcw_g[:4]),
                     ("gdn_a_log", dal[0, 4:8]), ("gdn_dt_bias", ddt[0, 4:8]), ("gdn_norm_g", dng[0]),
                     ("pool_w", dwp), ("pool_scale", dsc[0]), ("ff2_norm", dg_ff2[0])):
            small_g[n][l] = g
        big_g[l] = [dwgu1, dwd1.reshape(N_SHARD, 2, FF_SHARD // 2, d), dw_in, g_late[2], g_late[3], g_late[0], g_late[1]]
        if l == n_layer - 1:
            flight = pair_start(big_g[l], "1")

    late = pair_start(big_g[0][:2], "0c")
    red = {}
    red.update(((1, k), t) for k, t in enumerate(chip_finish(flight, late[3], "1")))
    red.update(zip(((0, 5), (0, 6), (0, 3), (0, 4)), chip_finish(early, late[3], "0a")))
    red[0, 2], = chip_finish(mid, late[3], "0b")
    late = pair_to_chip(late, red[0, 2], "0c")
    full = {}

    def pair_gather_into(keys, dep, tag):
        full.update(zip(keys, pair_gather([red[k] for k in keys], dep=dep, name=f"grad_pair_gather_{tag}")))

    pair_gather_into(sorted(red), late[3], "rest")

    def layers(k, f):
        return jnp.stack([f(full[l, k]) for l in range(n_layer)])

    grads = {}
    grads["w_in"] = layers(2, lambda t: t.reshape(d, P_IN_SHARD_PAD)[:, :P_IN_SHARD])
    grads["w_branch"] = layers(3, lambda t: t.reshape(4, BRANCH_W, d // N_SHARD))
    grads["w_out"] = layers(4, lambda t: t.reshape(d // N_SHARD, d))
    grads["ff2_wg"] = layers(5, lambda t: t[0])
    grads["ff2_wu"] = layers(5, lambda t: t[1])
    grads["ff2_wd"] = layers(6, lambda t: t.reshape(FF_SHARD, d))

    small_full = [jnp.stack(small_g[n]) for n in _SMALL if n != "final_norm"] + [d_final[0]]
    summed = _unpack(all_sum_small(_pack(small_full), name="grad_small_allreduce"), [t.shape for t in small_full])
    for n, g in zip(_SMALL, summed):
        if n in _CONV:
            width = a[n].shape[-1]
            g = lax.dynamic_slice_in_dim(g, chip * width, width, axis=2)
        grads[n] = g

    delta, new_m, new_v = {}, {}, {}

    def finish_ffn1():
        red.update(zip(((0, 0), (0, 1)), chip_finish(late, new_v["ff2_wd"], "0c")))
        pair_gather_into([(0, 0), (0, 1)], None, "0c")
        grads["ff1_wg"] = layers(0, lambda t: t[0])
        grads["ff1_wu"] = layers(0, lambda t: t[1])
        grads["ff1_wd"] = layers(1, lambda t: t.reshape(FF_SHARD, d))

    for n in sorted(_BIG, key=lambda n: n.startswith("ff1_")):
        if n.startswith("ff1_") and n not in grads:
            finish_ffn1()
        args = (a[n], grads[n], a["m_" + n], a["v_" + n])
        if a[n].shape[-1] == FF_SHARD:
            outs = adamw(*(t.swapaxes(1, 2) for t in args), name=f"adamw_{n}")
            delta[n], new_m[n], new_v[n] = (t.swapaxes(1, 2) for t in outs)
        elif n == "w_in":
            outs = adamw(*(t.transpose(2, 0, 1) for t in args), name=f"adamw_{n}")
            delta[n], new_m[n], new_v[n] = (t.transpose(1, 2, 0) for t in outs)
        else:
            delta[n], new_m[n], new_v[n] = adamw(*args, name=f"adamw_{n}")
    shapes = [a[n].shape for n in _SMALL]
    packed = [_pack([src[pre + n] for n in _SMALL]) for src, pre in ((a, ""), (grads, ""), (a, "m_"), (a, "v_"))]
    for store, buf in zip((delta, new_m, new_v), adamw(*packed, name="adamw_small")):
        store.update(zip(_SMALL, _unpack(buf, shapes)))

    return (loss, dx[None], *[grads[n] for n in _WEIGHTS], *[delta[n] for n in _WEIGHTS],
            *[new_m[n] for n in _WEIGHTS], *[new_v[n] for n in _WEIGHTS])
```

```python
import functools
import math

import jax
import jax.numpy as jnp
from jax import lax
from jax.experimental import pallas as pl
from jax.experimental.pallas import tpu as pltpu

F32 = jnp.float32
BF16 = jnp.bfloat16
MESH = pl.DeviceIdType.MESH

D_MODEL = 1024
N_SHARD = 4
D_FF = 2816
FF_SHARD = D_FF // N_SHARD
BRANCH_W = 512
CHUNK = 64
GDN_HEADS = 4
GDN_DK = 128
GDN_CPS = 4
TK_DW = 2048
LRU_C = 8.0
EPS = 1e-6
P_IN = 8712
P_IN_SHARD = P_IN // N_SHARD
P_IN_SHARD_PAD = 2304
P_PAD = 9216
C_QKV, C_Z, C_AU, C_AV, C_BX, C_BG, C_DX, C_BA, C_GATE = 0, 1536, 2048, 2560, 3072, 3584, 4096, 4608, 5120

ADAM_LR, ADAM_B1, ADAM_B2, ADAM_EPS, ADAM_WD, ADAM_STEP = 0.001, 0.9, 0.999, 1e-08, 0.01, 10


def _gelu(x):
    return 0.5 * x * (1.0 + jnp.tanh(0.7978845608028654 * (x + 0.044715 * (x * x * x))))


def _gelu_grad(x):
    u = 0.7978845608028654 * (x + 0.044715 * (x * x * x))
    t = jnp.tanh(u)
    return 0.5 * (1.0 + t) + 0.5 * x * (1.0 - t * t) * 0.7978845608028654 * (1.0 + 3.0 * 0.044715 * x * x)


def _silu(x):
    return x * jax.nn.sigmoid(x)


def _softplus(x):
    return jnp.maximum(x, 0.0) + jnp.log1p(jnp.exp(-jnp.abs(x)))


def _dg(a, b, ca, cb):
    return lax.dot_general(a.astype(BF16), b.astype(BF16), (((ca,), (cb,)), ((), ())),
                           preferred_element_type=F32)


def _dg3(a, b, ca, cb):
    a1 = a.astype(BF16)
    a2 = (a - a1.astype(F32)).astype(BF16)
    b1 = b.astype(BF16)
    b2 = (b - b1.astype(F32)).astype(BF16)
    dn = (((ca,), (cb,)), ((), ()))
    d = functools.partial(lax.dot_general, dimension_numbers=dn, preferred_element_type=F32)
    return d(a1, b1) + (d(a1, b2) + d(a2, b1))


def _make_mm(ca, cb, dot):
    @jax.custom_vjp
    def f(a, b):
        return dot(a, b, ca, cb)

    def fwd(a, b):
        return dot(a, b, ca, cb), (a, b)

    def bwd(res, g):
        a, b = res
        if (ca, cb) == (1, 0):
            return dot(g, b, 1, 1), dot(a, g, 0, 0)
        if (ca, cb) == (1, 1):
            return dot(g, b, 1, 0), dot(g, a, 0, 0)
        return dot(b, g, 1, 1), dot(a, g, 1, 0)

    f.defvjp(fwd, bwd)
    return f


mm_nn = _make_mm(1, 0, _dg)


def _bdg(a, b, ca, cb):
    return lax.dot_general(a.astype(BF16), b.astype(BF16), (((ca,), (cb,)), ((0,), (0,))),
                           preferred_element_type=F32)


def _split2(x):
    x1 = x.astype(BF16)
    return x1, (x - x1.astype(F32)).astype(BF16)


def _bdg3(a, b, ca, cb):
    a1, a2 = _split2(a)
    b1, b2 = _split2(b)
    d = functools.partial(lax.dot_general, dimension_numbers=(((ca,), (cb,)), ((0,), (0,))),
                          preferred_element_type=F32)
    return d(a1, b1) + (d(a1, b2) + d(a2, b1))


def _make_bmm(ca, cb):
    @jax.custom_vjp
    def f(a, b):
        return _bdg(a, b, ca, cb)

    def fwd(a, b):
        return _bdg(a, b, ca, cb), (a, b)

    def bwd(res, g):
        a, b = res
        if (ca, cb) == (2, 1):
            return _bdg(g, b, 2, 2), _bdg(a, g, 1, 1)
        if (ca, cb) == (2, 2):
            return _bdg(g, b, 2, 1), _bdg(g, a, 1, 1)
        return _bdg(b, g, 2, 2), _bdg(a, g, 2, 1)

    f.defvjp(fwd, bwd)
    return f


bmm_nn = _make_bmm(2, 1)
bmm_nt = _make_bmm(2, 2)
bmm_tn = _make_bmm(1, 1)


def _iota3(shape, dim):
    return lax.broadcasted_iota(jnp.int32, shape, dim)


@jax.custom_vjp
def _tri_inv(aw):
    b, n, _ = aw.shape
    shp = (b, n, 2 * n)
    left = _iota3(shp, 2) < n
    x = jnp.where(_iota3(shp, 1) == _iota3(shp, 2), 1.0, 0.0) - aw
    zero = jnp.zeros((b, n, 4 * n), BF16)
    k = 1
    while k < n:
        x1, x2 = _split2(x)
        lhs = jnp.concatenate([x1, x2], axis=1)
        rhs = jnp.concatenate([zero, jnp.concatenate([x1, x2], axis=2)], axis=1)
        r = lax.dot_general(lhs, rhs, (((2,), (1,)), ((0,), (0,))), preferred_element_type=F32)
        mx = (r[:, :n, :2 * n] + r[:, :n, 2 * n:]) + (r[:, n:, :2 * n] + r[:, n:, 2 * n:])
        x = jnp.where(left, x, 0.0) + mx
        k *= 2
    return jnp.where(left, x, 0.0)


def _tri_inv_fwd(aw):
    t = _tri_inv(aw)
    return t, t


def _tri_inv_bwd(t, dt):
    n = t.shape[1]
    x = _bdg3(t, dt, 1, 1)[:, :n]
    tp = jnp.concatenate([jnp.zeros_like(t), t], axis=1)
    return (-_bdg3(x, tp, 2, 2),)


_tri_inv.defvjp(_tri_inv_fwd, _tri_inv_bwd)


def _rows(shape):
    return lax.broadcasted_iota(jnp.int32, shape, 0)


def _lanes(shape):
    return lax.broadcasted_iota(jnp.int32, shape, 1)


def _col(x, j):
    return jnp.sum(jnp.where(_lanes(x.shape) == j, x, 0.0), axis=1, keepdims=True)


def _put_col(col, j, width):
    shape = (col.shape[0], width)
    return jnp.where(_lanes(shape) == j, jnp.broadcast_to(col, shape), 0.0)


VMEM_BYTES_V7X = 64 << 20


def _params(*sem, vmem_bytes=None):
    return pltpu.CompilerParams(dimension_semantics=sem, vmem_limit_bytes=vmem_bytes)


def matmul(a, b, *, ta=False, tb=False, tm, tn, tk, out_dtype=F32, col_groups=1, dep=None, name):
    ba, bb = a.shape[:-2], b.shape[:-2]
    batch = ba if len(ba) >= len(bb) else bb
    assert ba in ((), batch) and bb in ((), batch)
    nb = len(batch)
    (m, k) = (a.shape[-1], a.shape[-2]) if ta else (a.shape[-2], a.shape[-1])
    (k2, n) = (b.shape[-1], b.shape[-2]) if tb else (b.shape[-2], b.shape[-1])
    tm, tn, tk = min(tm, m), min(tn, n), min(tk, k)
    assert k == k2 and m % tm == 0 and n % tn == 0 and k % tk == 0, (a.shape, b.shape, tm, tn, tk)
    nk = k // tk
    grid = batch + (m // tm, n // tn, nk)

    def a_map(*g):
        i, kk = g[nb], g[nb + 2]
        return (g[:nb] if ba else ()) + ((kk, i) if ta else (i, kk))

    def b_map(*g):
        j, kk = g[nb + 1], g[nb + 2]
        return (g[:nb] if bb else ()) + ((j, kk) if tb else (kk, j))

    per_group = n // tn // col_groups
    assert col_groups == 1 or (nb == 0 and per_group * col_groups * tn == n)

    def o_map(*g):
        if col_groups > 1:
            return (g[1] // per_group, g[0], g[1] % per_group)
        return g[:nb] + (g[nb], g[nb + 1])

    a_blk = (None,) * len(ba) + ((tk, tm) if ta else (tm, tk))
    b_blk = (None,) * len(bb) + ((tn, tk) if tb else (tk, tn))
    o_blk = (None,) * (nb + (col_groups > 1)) + (tm, tn)
    o_shape = (col_groups, m, n // col_groups) if col_groups > 1 else batch + (m, n)

    dep_specs, dep_args = _dep_operand(dep)

    def body(a_ref, b_ref, *rest):
        o_ref, acc_ref = rest[len(dep_args):]
        kk = pl.program_id(nb + 2)
        part = _dg(a_ref[...], b_ref[...], 0 if ta else 1, 1 if tb else 0)
        if nk == 1:
            o_ref[...] = part.astype(o_ref.dtype)
            return

        @pl.when(kk == 0)
        def _():
            acc_ref[...] = part

        @pl.when(jnp.logical_and(kk > 0, kk < nk - 1))
        def _():
            acc_ref[...] += part

        @pl.when(kk == nk - 1)
        def _():
            o_ref[...] = (acc_ref[...] + part).astype(o_ref.dtype)

    return pl.pallas_call(
        body, name=name, grid=grid,
        in_specs=[pl.BlockSpec(a_blk, a_map), pl.BlockSpec(b_blk, b_map)] + dep_specs,
        out_specs=pl.BlockSpec(o_blk, o_map),
        out_shape=jax.ShapeDtypeStruct(o_shape, out_dtype),
        scratch_shapes=[pltpu.VMEM((tm, tn), F32)],
        compiler_params=_params(*(("parallel",) * (nb + 2) + ("arbitrary",))),
    )(a, b, *dep_args)


def norm_fwd(x, g, *, tb, name):
    s, d = x.shape

    def body(x_ref, g_ref, h_ref):
        xv = x_ref[...]
        r = lax.rsqrt(jnp.mean(xv * xv, axis=-1, keepdims=True) + EPS)
        h_ref[...] = (xv * r * g_ref[...]).astype(BF16)

    return pl.pallas_call(
        body, name=name, grid=(s // tb,),
        in_specs=[pl.BlockSpec((tb, d), lambda i: (i, 0)), pl.BlockSpec((1, d), lambda i: (0, 0))],
        out_specs=pl.BlockSpec((tb, d), lambda i: (i, 0)),
        out_shape=jax.ShapeDtypeStruct((s, d), BF16),
        compiler_params=_params("parallel"),
    )(x, g)


def _rms_bwd(xv, gv, dh):
    r = lax.rsqrt(jnp.mean(xv * xv, axis=-1, keepdims=True) + EPS)
    xh = xv * r
    dxh = dh * gv
    dx = r * (dxh - xh * jnp.mean(dxh * xh, axis=-1, keepdims=True))
    return dx, jnp.sum(dh * xh, axis=0, keepdims=True)


def norm_bwd(x, g, dh, dres, *, tb, dep=None, name):
    s, d = x.shape
    dep_specs, dep_args = _dep_operand(dep)

    def body(x_ref, g_ref, dh_ref, dres_ref, *rest):
        dx_ref, dg_ref = rest[len(dep_args):]
        dx, dg = _rms_bwd(x_ref[...], g_ref[...], dh_ref[...])
        dx_ref[...] = dres_ref[...] + dx

        @pl.when(pl.program_id(0) == 0)
        def _():
            dg_ref[...] = jnp.zeros_like(dg_ref)

        dg_ref[...] += dg

    row = pl.BlockSpec((tb, d), lambda i: (i, 0))
    one = pl.BlockSpec((1, d), lambda i: (0, 0))
    return pl.pallas_call(
        body, name=name, grid=(s // tb,),
        in_specs=[row, one, row, row] + dep_specs, out_specs=[row, one],
        out_shape=[jax.ShapeDtypeStruct((s, d), F32), jax.ShapeDtypeStruct((1, d), F32)],
        compiler_params=_params("arbitrary"),
    )(x, g, dh, dres, *dep_args)


def _dep_operand(dep):
    return ([], []) if dep is None else ([_ANY], [dep])


def ffn_fwd(x, g, wg, wu, wd, layer, *, tb, dep=None, name):
    s, d = x.shape
    fc = wd.shape[-2]
    dep_specs, dep_args = _dep_operand(dep)

    def body(x_ref, g_ref, wg_ref, wu_ref, wd_ref, *rest):
        o_ref, h_scr, acc_scr = rest[len(dep_args):]
        j = pl.program_id(1)

        @pl.when(j == 0)
        def _():
            xv = x_ref[...]
            r = lax.rsqrt(jnp.mean(xv * xv, axis=-1, keepdims=True) + EPS)
            h_scr[...] = (xv * r * g_ref[...]).astype(BF16)
            acc_scr[...] = jnp.zeros_like(acc_scr)

        h = h_scr[...]
        a = _dg(h, wg_ref[...], 1, 0)
        b = _dg(h, wu_ref[...], 1, 0)
        acc_scr[...] += _dg(_silu(a) * b, wd_ref[...], 1, 0)

        @pl.when(j == N_SHARD - 1)
        def _():
            o_ref[...] = x_ref[...] + 0.5 * acc_scr[...]

    row = pl.BlockSpec((tb, d), lambda i, j: (i, 0))
    lg, lu, ld = layer
    return pl.pallas_call(
        body, name=name, grid=(s // tb, N_SHARD),
        in_specs=[row, pl.BlockSpec((1, d), lambda i, j: (0, 0)),
                  pl.BlockSpec((None, None, d, fc), lambda i, j: (j, lg, 0, 0)),
                  pl.BlockSpec((None, None, d, fc), lambda i, j: (j, lu, 0, 0)),
                  pl.BlockSpec((None, None, fc, d), lambda i, j: (j, ld, 0, 0))] + dep_specs,
        out_specs=row,
        out_shape=jax.ShapeDtypeStruct((s, d), F32),
        scratch_shapes=[pltpu.VMEM((tb, d), BF16), pltpu.VMEM((tb, d), F32)],
        compiler_params=_params("parallel", "arbitrary"),
    )(x, g, wg, wu, wd, *dep_args)


def ffn_bwd(x, dxo, g, wg, wu, wd, layer, *, tb, dep=None, name):
    s, d = x.shape
    fc = wd.shape[-2]
    dep_specs, dep_args = _dep_operand(dep)

    def body(x_ref, dxo_ref, g_ref, wg_ref, wu_ref, wd_ref, *rest):
        dx_ref, dg_ref, dab_ref, hid_ref, h_ref, dy_ref, dh_scr = rest[len(dep_args):]
        i, j = pl.program_id(0), pl.program_id(1)

        @pl.when(j == 0)
        def _():
            xv = x_ref[...]
            r = lax.rsqrt(jnp.mean(xv * xv, axis=-1, keepdims=True) + EPS)
            h_ref[...] = (xv * r * g_ref[...]).astype(BF16)
            dy_ref[...] = (0.5 * dxo_ref[...]).astype(BF16)
            dh_scr[...] = jnp.zeros_like(dh_scr)

        h = h_ref[...]
        a = _dg(h, wg_ref[...], 1, 0)
        b = _dg(h, wu_ref[...], 1, 0)
        sg = jax.nn.sigmoid(a)
        sl = a * sg
        dhid = _dg(dy_ref[...], wd_ref[...], 1, 1)
        da = (dhid * b * (sg * (1.0 + a * (1.0 - sg)))).astype(BF16)
        db = (dhid * sl).astype(BF16)
        dab_ref[0] = da
        dab_ref[1] = db
        hid_ref[...] = (sl * b).astype(BF16)
        dh_scr[...] += _dg(da, wg_ref[...], 1, 1) + _dg(db, wu_ref[...], 1, 1)

        @pl.when(jnp.logical_and(i == 0, j == 0))
        def _():
            dg_ref[...] = jnp.zeros_like(dg_ref)

        @pl.when(j == N_SHARD - 1)
        def _():
            dx, dg = _rms_bwd(x_ref[...], g_ref[...], dh_scr[...])
            dx_ref[...] = dxo_ref[...] + dx
            dg_ref[...] += dg

    row = pl.BlockSpec((tb, d), lambda i, j: (i, 0))
    one = pl.BlockSpec((1, d), lambda i, j: (0, 0))
    lg, lu, ld = layer
    return pl.pallas_call(
        body, name=name, grid=(s // tb, N_SHARD),
        in_specs=[row, row, one,
                  pl.BlockSpec((None, None, d, fc), lambda i, j: (j, lg, 0, 0)),
                  pl.BlockSpec((None, None, d, fc), lambda i, j: (j, lu, 0, 0)),
                  pl.BlockSpec((None, None, fc, d), lambda i, j: (j, ld, 0, 0))] + dep_specs,
        out_specs=[row, one,
                   pl.BlockSpec((None, 2, tb, fc), lambda i, j: (j, 0, i, 0)),
                   pl.BlockSpec((None, tb, fc), lambda i, j: (j, i, 0)),
                   row, row],
        out_shape=[jax.ShapeDtypeStruct((s, d), F32), jax.ShapeDtypeStruct((1, d), F32),
                   jax.ShapeDtypeStruct((N_SHARD, 2, s, fc), BF16),
                   jax.ShapeDtypeStruct((N_SHARD, s, fc), BF16),
                   jax.ShapeDtypeStruct((s, d), BF16), jax.ShapeDtypeStruct((s, d), BF16)],
        scratch_shapes=[pltpu.VMEM((tb, d), F32)],
        compiler_params=_params("arbitrary", "arbitrary",
                                vmem_bytes=min(VMEM_BYTES_V7X - (8 << 20),
                                               12 * d * fc + 44 * tb * d + 40 * tb * fc + (4 << 20))),
    )(x, dxo, g, wg, wu, wd, *dep_args)


def _pblk(tb, width, col0, rev_n=None):
    assert col0 % width == 0
    cb = col0 // width
    if rev_n is None:
        return pl.BlockSpec((tb, width), lambda i: (i, cb))
    return pl.BlockSpec((tb, width), lambda i: (rev_n - 1 - i, cb))


def _full(shape):
    return pl.BlockSpec(shape, lambda i: (0,) * len(shape))


def conv_fwd(src, col0, width, w, bias, *, tb, name):
    s = src.shape[0]
    cb = col0 // width

    def body(x_ref, p_ref, w_ref, b_ref, y_ref):
        i = pl.program_id(0)
        prev = jnp.where(i > 0, p_ref[...], 0.0)
        xe = jnp.concatenate([prev, x_ref[...]], axis=0)
        wv = w_ref[...]
        acc = b_ref[...] + wv[3:4] * xe[8:]
        for k in range(3):
            acc = acc + wv[k:k + 1] * pltpu.roll(xe, 3 - k, 0)[8:]
        y_ref[...] = acc

    return pl.pallas_call(
        body, name=name, grid=(s // tb,),
        in_specs=[_pblk(tb, width, col0),
                  pl.BlockSpec((8, width), lambda i: (jnp.maximum(i * (tb // 8) - 1, 0), cb)),
                  _full((4, width)), _full((1, width))],
        out_specs=pl.BlockSpec((tb, width), lambda i: (i, 0)),
        out_shape=jax.ShapeDtypeStruct((s, width), F32),
        compiler_params=_params("parallel"),
    )(src, src, w, bias)


def conv_bwd(src, col0, width, dpre, w, *, tb, name):
    s = src.shape[0]
    cb = col0 // width
    nt = s // tb

    def body(x_ref, p_ref, d_ref, n_ref, w_ref, dx_ref, dwb_ref):
        i = pl.program_id(0)
        d = d_ref[...]
        nxt = jnp.where(i < nt - 1, n_ref[...], 0.0)
        de = jnp.concatenate([d, nxt], axis=0)
        wv = w_ref[...]
        dx = wv[3:4] * d
        for k in range(3):
            dx = dx + wv[k:k + 1] * pltpu.roll(de, tb + 8 - (3 - k), 0)[:tb]
        dx_ref[...] = dx.astype(BF16)

        @pl.when(i == 0)
        def _():
            dwb_ref[...] = jnp.zeros_like(dwb_ref)

        prev = jnp.where(i > 0, p_ref[...], 0.0)
        xe = jnp.concatenate([prev, x_ref[...]], axis=0)
        dwb_ref[3:4, :] += jnp.sum(d * xe[8:], axis=0, keepdims=True)
        for k in range(3):
            dwb_ref[k:k + 1, :] += jnp.sum(d * pltpu.roll(xe, 3 - k, 0)[8:], axis=0, keepdims=True)
        dwb_ref[4:5, :] += jnp.sum(d, axis=0, keepdims=True)

    return pl.pallas_call(
        body, name=name, grid=(nt,),
        in_specs=[_pblk(tb, width, col0),
                  pl.BlockSpec((8, width), lambda i: (jnp.maximum(i * (tb // 8) - 1, 0), cb)),
                  pl.BlockSpec((tb, width), lambda i: (i, 0)),
                  pl.BlockSpec((8, width), lambda i: (jnp.minimum((i + 1) * (tb // 8), s // 8 - 1), 0)),
                  _full((4, width))],
        out_specs=[pl.BlockSpec((tb, width), lambda i: (i, 0)), _full((8, width))],
        out_shape=[jax.ShapeDtypeStruct((s, width), BF16), jax.ShapeDtypeStruct((8, width), F32)],
        compiler_params=_params("arbitrary"),
    )(src, src, dpre, dpre, w)


SGU_BLOCK = 128


def _sgu_pre(u, v, lg, lb):
    ug = _gelu(u)
    vg = _gelu(v)
    mu = jnp.mean(vg, axis=-1, keepdims=True)
    var = jnp.mean(jnp.square(vg - mu), axis=-1, keepdims=True)
    return ug, (vg - mu) * lax.rsqrt(var + EPS) * lg + lb


def _sgu_mask():
    shp = (SGU_BLOCK, SGU_BLOCK)
    return (_rows(shp) // CHUNK) >= (_lanes(shp) // CHUNK)


def sgu_fwd(proj, lg, lb, ws, bt, *, tb, name):
    s = proj.shape[0]
    w = BRANCH_W

    def body(u_ref, v_ref, lg_ref, lb_ref, w_ref, b_ref, y_ref):
        ug, vn = _sgu_pre(u_ref[...], v_ref[...], lg_ref[...], lb_ref[...])
        mask = _sgu_mask()
        for g in range(4):
            wm = jnp.where(mask, w_ref[g], 0.0)
            bcol = _col(b_ref[...], g)
            cs = slice(128 * g, 128 * g + 128)
            for n in range(tb // SGU_BLOCK):
                rs = slice(SGU_BLOCK * n, SGU_BLOCK * (n + 1))
                mixed = _dg(wm, vn[rs, cs], 1, 0) + bcol
                y_ref[rs, cs] = (ug[rs, cs] * mixed).astype(BF16)

    return pl.pallas_call(
        body, name=name, grid=(s // tb,),
        in_specs=[_pblk(tb, w, C_AU), _pblk(tb, w, C_AV), _full((1, w)), _full((1, w)),
                  _full((4, 128, 128)), _full((128, 128))],
        out_specs=pl.BlockSpec((tb, w), lambda i: (i, 0)),
        out_shape=jax.ShapeDtypeStruct((s, w), BF16),
        compiler_params=_params("parallel"),
    )(proj, proj, lg, lb, ws, bt)


def sgu_bwd(proj, dy, lg, lb, ws, bt, *, tb, dep=None, name):
    s = proj.shape[0]
    w = BRANCH_W
    dep_specs, dep_args = _dep_operand(dep)

    def body(u_ref, v_ref, dy_ref, lg_ref, lb_ref, w_ref, b_ref, *rest):
        du_ref, dv_ref, dlg_ref, dlb_ref, dw_ref, db_ref, dug_scr, dvn_scr = rest[len(dep_args):]

        @pl.when(pl.program_id(0) == 0)
        def _():
            dlg_ref[...] = jnp.zeros_like(dlg_ref)
            dlb_ref[...] = jnp.zeros_like(dlb_ref)
            dw_ref[...] = jnp.zeros_like(dw_ref)
            db_ref[...] = jnp.zeros_like(db_ref)

        (ug, vn), vf = jax.vjp(_sgu_pre, u_ref[...], v_ref[...], lg_ref[...], lb_ref[...])
        dyv = dy_ref[...]
        mask = _sgu_mask()
        for g in range(4):
            wm = jnp.where(mask, w_ref[g], 0.0)
            bcol = _col(b_ref[...], g)
            cs = slice(128 * g, 128 * g + 128)
            dwg = jnp.zeros((SGU_BLOCK, SGU_BLOCK), F32)
            dbc = jnp.zeros((SGU_BLOCK, 1), F32)
            for n in range(tb // SGU_BLOCK):
                rs = slice(SGU_BLOCK * n, SGU_BLOCK * (n + 1))
                blk = vn[rs, cs]
                mixed = _dg(wm, blk, 1, 0) + bcol
                dmix = dyv[rs, cs] * ug[rs, cs]
                dug_scr[rs, cs] = dyv[rs, cs] * mixed
                dvn_scr[rs, cs] = _dg(wm, dmix, 0, 0)
                dwg = dwg + _dg(dmix, blk, 1, 1)
                dbc = dbc + jnp.sum(dmix, axis=1, keepdims=True)
            dw_ref[g] += jnp.where(mask, dwg, 0.0)
            db_ref[...] += _put_col(dbc, g, 128)
        du, dv, dlg, dlb = vf((dug_scr[...], dvn_scr[...]))
        du_ref[...] = du.astype(BF16)
        dv_ref[...] = dv.astype(BF16)
        dlg_ref[...] += dlg
        dlb_ref[...] += dlb

    row = pl.BlockSpec((tb, w), lambda i: (i, 0))
    return pl.pallas_call(
        body, name=name, grid=(s // tb,),
        in_specs=[_pblk(tb, w, C_AU), _pblk(tb, w, C_AV), row, _full((1, w)), _full((1, w)),
                  _full((4, 128, 128)), _full((128, 128))] + dep_specs,
        out_specs=[row, row, _full((1, w)), _full((1, w)), _full((4, 128, 128)), _full((128, 128))],
        out_shape=[jax.ShapeDtypeStruct((s, w), BF16), jax.ShapeDtypeStruct((s, w), BF16),
                   jax.ShapeDtypeStruct((1, w), F32), jax.ShapeDtypeStruct((1, w), F32),
                   jax.ShapeDtypeStruct((4, 128, 128), F32), jax.ShapeDtypeStruct((128, 128), F32)],
        scratch_shapes=[pltpu.VMEM((tb, w), F32), pltpu.VMEM((tb, w), F32)],
        compiler_params=_params("arbitrary"),
    )(proj, proj, dy, lg, lb, ws, bt, *dep_args)


POOL_HALO = 16


def _pooled(xe, t0, tb):
    s2 = xe + pltpu.roll(xe, 1, 0)
    s4 = s2 + pltpu.roll(s2, 2, 0)
    s8 = s4 + pltpu.roll(s4, 4, 0)
    s16 = s8 + pltpu.roll(s8, 8, 0)
    t = (t0 + _rows((tb, 128))).astype(F32)
    outs = []
    for g, sw in enumerate((s2, s4, s8, s16)):
        cs = slice(128 * g, 128 * g + 128)
        cnt = jnp.minimum(t + 1.0, float(2 ** (g + 1)))
        outs.append(sw[POOL_HALO:, cs] / cnt - xe[POOL_HALO:, cs])
    return outs


def pool_fwd(proj, wp, scale, *, tb, name):
    s = proj.shape[0]
    w = BRANCH_W

    def body(x_ref, p_ref, w_ref, sc_ref, y_ref):
        i = pl.program_id(0)
        prev = jnp.where(i > 0, p_ref[...], 0.0)
        xe = jnp.concatenate([prev, x_ref[...]], axis=0)
        pooled = _pooled(xe, i * tb, tb)
        sc = sc_ref[...]
        for g in range(4):
            cs = slice(128 * g, 128 * g + 128)
            y_ref[:, cs] = (_dg(pooled[g], w_ref[g], 1, 0) * sc[:, cs]).astype(BF16)

    cb = C_DX // w
    return pl.pallas_call(
        body, name=name, grid=(s // tb,),
        in_specs=[_pblk(tb, w, C_DX),
                  pl.BlockSpec((POOL_HALO, w), lambda i: (jnp.maximum(i * (tb // POOL_HALO) - 1, 0), cb)),
                  _full((4, 128, 128)), _full((1, w))],
        out_specs=pl.BlockSpec((tb, w), lambda i: (i, 0)),
        out_shape=jax.ShapeDtypeStruct((s, w), BF16),
        compiler_params=_params("parallel"),
    )(proj, proj, wp, scale)


def pool_bwd(proj, dy, wp, scale, *, tb, name):
    s = proj.shape[0]
    w = BRANCH_W
    nt = s // tb
    te = tb + POOL_HALO

    def body(x_ref, p_ref, dy_ref, n_ref, w_ref, sc_ref, dx_ref, dw_ref, dsc_ref):
        i = pl.program_id(0)

        @pl.when(i == 0)
        def _():
            dw_ref[...] = jnp.zeros_like(dw_ref)
            dsc_ref[...] = jnp.zeros_like(dsc_ref)

        prev = jnp.where(i > 0, p_ref[...], 0.0)
        xe = jnp.concatenate([prev, x_ref[...]], axis=0)
        pooled = _pooled(xe, i * tb, tb)
        dyv = dy_ref[...]
        nxt = jnp.where(i < nt - 1, n_ref[...], 0.0)
        dye = jnp.concatenate([dyv, nxt], axis=0)
        sc = sc_ref[...]
        t = (i * tb + _rows((te, 128))).astype(F32)
        for g in range(4):
            cs = slice(128 * g, 128 * g + 128)
            win = 2 ** (g + 1)
            dpm = dye[:, cs] * sc[:, cs]
            dpool = _dg(dpm, w_ref[g], 1, 1)
            q = dpool / jnp.minimum(t + 1.0, float(win))
            r, sh = q, 1
            while sh < win:
                r = r + pltpu.roll(r, te - sh, 0)
                sh *= 2
            dx_ref[:, cs] = (r[:tb] - dpool[:tb]).astype(BF16)
            dw_ref[g] += _dg(pooled[g], dpm[:tb], 0, 0)
            dsc_ref[:, cs] += jnp.sum(dyv[:, cs] * _dg(pooled[g], w_ref[g], 1, 0), axis=0, keepdims=True)

    cb = C_DX // w
    row = pl.BlockSpec((tb, w), lambda i: (i, 0))
    return pl.pallas_call(
        body, name=name, grid=(nt,),
        in_specs=[_pblk(tb, w, C_DX),
                  pl.BlockSpec((POOL_HALO, w), lambda i: (jnp.maximum(i * (tb // POOL_HALO) - 1, 0), cb)),
                  row,
                  pl.BlockSpec((POOL_HALO, w),
                               lambda i: (jnp.minimum((i + 1) * (tb // POOL_HALO), s // POOL_HALO - 1), 0)),
                  _full((4, 128, 128)), _full((1, w))],
        out_specs=[row, _full((4, 128, 128)), _full((1, w))],
        out_shape=[jax.ShapeDtypeStruct((s, w), BF16), jax.ShapeDtypeStruct((4, 128, 128), F32),
                   jax.ShapeDtypeStruct((1, w), F32)],
        compiler_params=_params("arbitrary"),
    )(proj, proj, dy, dy, wp, scale)


def _neg_expm1(z):
    p = 1.0 + z * (1.0 / 9.0)
    for n in (8.0, 7.0, 6.0, 5.0, 4.0, 3.0, 2.0):
        p = 1.0 + z * (1.0 / n) * p
    return jnp.where(z > -0.5, -z * p, 1.0 - jnp.exp(z))


def _lru_gates(xc, wa, wx, ba, bx, lam):
    r = jax.nn.sigmoid(mm_nn(xc, wa) + ba)
    i = jax.nn.sigmoid(mm_nn(xc, wx) + bx)
    log_a = -LRU_C * r * _softplus(-lam)
    a = jnp.exp(log_a)
    return a, jnp.sqrt(_neg_expm1(2.0 * log_a)) * (i * xc)


def _scan_down(a, b):
    n = a.shape[0]
    row = _rows(a.shape)
    sh = 1
    while sh < n:
        keep = row >= sh
        a_sh = jnp.where(keep, pltpu.roll(a, sh, 0), 1.0)
        b_sh = jnp.where(keep, pltpu.roll(b, sh, 0), 0.0)
        b = a * b_sh + b
        a = a * a_sh
        sh *= 2
    return a, b


def _scan_up(a, b):
    n = a.shape[0]
    row = _rows(a.shape)
    sh = 1
    while sh < n:
        keep = row < n - sh
        a_sh = jnp.where(keep, pltpu.roll(a, n - sh, 0), 1.0)
        b_sh = jnp.where(keep, pltpu.roll(b, n - sh, 0), 0.0)
        b = a * b_sh + b
        a = a * a_sh
        sh *= 2
    return a, b


def lru_fwd(xc, proj, wa, wx, ba, bx, lam, *, tb, name):
    s = xc.shape[0]
    w = BRANCH_W

    def body(xc_ref, gate_ref, wa_ref, wx_ref, ba_ref, bx_ref, lam_ref, y_ref, h_ref, carry):
        @pl.when(pl.program_id(0) == 0)
        def _():
            carry[...] = jnp.zeros_like(carry)

        a, b = _lru_gates(xc_ref[...], wa_ref[...], wx_ref[...], ba_ref[...], bx_ref[...], lam_ref[...])
        pa, hb = _scan_down(a, b)
        h = pa * carry[...] + hb
        h_ref[...] = h
        carry[...] = h_ref[tb - 1:tb, :]
        y_ref[...] = (h * _gelu(gate_ref[...])).astype(BF16)

    row = pl.BlockSpec((tb, w), lambda i: (i, 0))
    return pl.pallas_call(
        body, name=name, grid=(s // tb,),
        in_specs=[row, _pblk(tb, w, C_BG), _full((w, w)), _full((w, w)),
                  _full((1, w)), _full((1, w)), _full((1, w))],
        out_specs=[row, row],
        out_shape=[jax.ShapeDtypeStruct((s, w), BF16), jax.ShapeDtypeStruct((s, w), F32)],
        scratch_shapes=[pltpu.VMEM((1, w), F32)],
        compiler_params=_params("arbitrary"),
    )(xc, proj, wa, wx, ba, bx, lam)


def lru_bwd(xc, proj, h, dy, wa, wx, ba, bx, lam, *, tb, name):
    s = xc.shape[0]
    w = BRANCH_W
    nt = s // tb

    def body(xc_ref, gate_ref, h_ref, hp_ref, dy_ref, wa_ref, wx_ref, ba_ref, bx_ref, lam_ref,
             dxc_ref, dgate_ref, dwa_ref, dwx_ref, dba_ref, dbx_ref, dlam_ref, carry):
        i = pl.program_id(0)

        @pl.when(i == 0)
        def _():
            carry[...] = jnp.zeros_like(carry)
            for r in (dwa_ref, dwx_ref, dba_ref, dbx_ref, dlam_ref):
                r[...] = jnp.zeros_like(r)

        (a, b), vf = jax.vjp(_lru_gates, xc_ref[...], wa_ref[...], wx_ref[...],
                             ba_ref[...], bx_ref[...], lam_ref[...])
        gate = gate_ref[...]
        dyv = dy_ref[...]
        hv = h_ref[...]
        row = _rows((tb, w))
        a_next = jnp.where(row < tb - 1, pltpu.roll(a, tb - 1, 0), 1.0)
        pa, xb = _scan_up(a_next, dyv * _gelu(gate))
        dh = xb + pa * carry[...]
        carry[...] = a[0:1, :] * dh[0:1, :]
        h_first = jnp.where(i < nt - 1, hp_ref[7:8, :], 0.0)
        h_prev = jnp.where(row >= 1, pltpu.roll(hv, 1, 0), h_first)
        dxc, dwa, dwx, dba, dbx, dlam = vf((dh * h_prev, dh))
        dxc_ref[...] = dxc
        dgate_ref[...] = (dyv * hv * _gelu_grad(gate)).astype(BF16)
        dwa_ref[...] += dwa
        dwx_ref[...] += dwx
        dba_ref[...] += dba
        dbx_ref[...] += dbx
        dlam_ref[...] += dlam

    row_spec = pl.BlockSpec((tb, w), lambda i: (nt - 1 - i, 0))
    hp_spec = pl.BlockSpec((8, w), lambda i: (jnp.maximum((nt - 1 - i) * (tb // 8) - 1, 0), 0))
    return pl.pallas_call(
        body, name=name, grid=(nt,),
        in_specs=[row_spec, _pblk(tb, w, C_BG, rev_n=nt), row_spec, hp_spec, row_spec,
                  _full((w, w)), _full((w, w)), _full((1, w)), _full((1, w)), _full((1, w))],
        out_specs=[row_spec, row_spec, _full((w, w)), _full((w, w)),
                   _full((1, w)), _full((1, w)), _full((1, w))],
        out_shape=[jax.ShapeDtypeStruct((s, w), F32), jax.ShapeDtypeStruct((s, w), BF16),
                   jax.ShapeDtypeStruct((w, w), F32), jax.ShapeDtypeStruct((w, w), F32),
                   jax.ShapeDtypeStruct((1, w), F32), jax.ShapeDtypeStruct((1, w), F32),
                   jax.ShapeDtypeStruct((1, w), F32)],
        scratch_shapes=[pltpu.VMEM((1, w), F32)],
        compiler_params=_params("arbitrary"),
    )(xc, proj, h, h, dy, wa, wx, ba, bx, lam)


def _gdn_g(ba, alog, dt):
    return -jnp.exp(alog) * _softplus(ba + dt)


def _gdn_local(qp, kp, vp, bpre, gc):
    b, c, dk = qp.shape
    q = _silu(qp)
    k = _silu(kp)
    v = _silu(vp)
    q = q * lax.rsqrt(jnp.sum(q * q, axis=-1, keepdims=True) + EPS) * (dk ** -0.5)
    k = k * lax.rsqrt(jnp.sum(k * k, axis=-1, keepdims=True) + EPS)
    beta = jax.nn.sigmoid(bpre)
    sq = (b, c, c)
    ii, jj = _iota3(sq, 1), _iota3(sq, 2)
    gr = jnp.sum(gc * (ii == jj).astype(F32), axis=1, keepdims=True)
    decay = jnp.exp(jnp.where(ii >= jj, gc - gr, -jnp.inf))
    wd = (b, c, 2 * c)
    wi, wj = _iota3(wd, 1), _iota3(wd, 2) - c
    gr_w = jnp.sum(gc * (wi == wj).astype(F32), axis=1, keepdims=True)
    decay_w = jnp.exp(jnp.where(jnp.logical_and(wj >= 0, wi > wj), gc - gr_w, -jnp.inf))
    kb = k * beta
    pad = jnp.zeros((b, c, dk), F32)
    t = _tri_inv(bmm_nt(kb, jnp.concatenate([pad, k], axis=1)) * decay_w)
    u = bmm_nn(t, jnp.concatenate([v * beta, pad], axis=1))
    wk = bmm_nn(t, jnp.concatenate([kb * jnp.exp(gc), pad], axis=1))
    attn = bmm_nt(q, k) * decay
    g_last = jnp.sum(jnp.where(_iota3((b, c, 1), 1) == c - 1, gc, 0.0), axis=1, keepdims=True)
    return u, wk, attn, q * jnp.exp(gc), k * jnp.exp(g_last - gc), jnp.exp(g_last)


def _gdn_recur(s, u, wk, attn, qg, kd, eg, zz, ng):
    v_new = u - bmm_nn(wk, s)
    o = bmm_nn(qg, s) + bmm_nn(attn, v_new)
    s_out = s * eg + bmm_tn(kd, v_new)
    o = o * lax.rsqrt(jnp.mean(o * o, axis=-1, keepdims=True) + EPS) * ng
    return o * _silu(zz), s_out


def _gdn_inputs(qkv_ref, ba, gcums, cps):
    c, hd = CHUNK, GDN_DK
    qs, ks, vs, bs, gs = [], [], [], [], []
    for ci in range(cps):
        rs = slice(c * ci, c * (ci + 1))
        for h in range(GDN_HEADS):
            qs.append(qkv_ref[rs, hd * h:hd * (h + 1)])
            ks.append(qkv_ref[rs, 512 + hd * h:512 + hd * (h + 1)])
            vs.append(qkv_ref[rs, 1024 + hd * h:1024 + hd * (h + 1)])
            bs.append(_col(ba[rs], h))
            gs.append(_col(gcums[ci], 4 + h))
    return tuple(jnp.stack(t) for t in (qs, ks, vs, bs, gs))


def _chunk_cumsum(g, transpose=False):
    c = CHUNK
    tri = (_rows((c, c)) >= _lanes((c, c))).astype(F32)
    return _dg3(tri, g, 0 if transpose else 1, 0)


def gdn_fwd(qkv, proj, alog, dt, ng, *, cps, name):
    s = qkv.shape[0]
    c = CHUNK
    nc = s // c
    hd = GDN_DK
    tb = c * cps

    def body(qkv_ref, z_ref, ba_ref, al_ref, dt_ref, ng_ref, y_ref, sall_ref, s_scr):
        @pl.when(pl.program_id(0) == 0)
        def _():
            s_scr[...] = jnp.zeros_like(s_scr)

        nh = GDN_HEADS
        ba = ba_ref[...]
        g = _gdn_g(ba, al_ref[...], dt_ref[...])
        gcums = [_chunk_cumsum(g[c * ci:c * (ci + 1)]) for ci in range(cps)]
        loc = _gdn_local(*_gdn_inputs(qkv_ref, ba, gcums, cps))
        for ci in range(cps):
            rs = slice(c * ci, c * (ci + 1))
            sl = slice(nh * ci, nh * (ci + 1))
            s_in = s_scr[...]
            sall_ref[ci] = s_in
            zz = jnp.stack([z_ref[rs, hd * h:hd * (h + 1)] for h in range(nh)])
            y, s_out = _gdn_recur(s_in, *(t[sl] for t in loc), zz, ng_ref[...])
            s_scr[...] = s_out
            for h in range(nh):
                y_ref[rs, hd * h:hd * (h + 1)] = y[h].astype(BF16)

    return pl.pallas_call(
        body, name=name, grid=(nc // cps,),
        in_specs=[pl.BlockSpec((tb, 1536), lambda i: (i, 0)), _pblk(tb, 512, C_Z), _pblk(tb, 128, C_BA),
                  _full((1, 128)), _full((1, 128)), _full((1, 128))],
        out_specs=[pl.BlockSpec((tb, 512), lambda i: (i, 0)),
                   pl.BlockSpec((cps, GDN_HEADS, hd, hd), lambda i: (i, 0, 0, 0))],
        out_shape=[jax.ShapeDtypeStruct((s, 512), BF16), jax.ShapeDtypeStruct((nc, GDN_HEADS, hd, hd), F32)],
        scratch_shapes=[pltpu.VMEM((GDN_HEADS, hd, hd), F32)],
        compiler_params=_params("arbitrary"),
    )(qkv, proj, proj, alog, dt, ng)


def gdn_bwd(qkv, proj, sall, dy, alog, dt, ng, *, cps, name):
    s = qkv.shape[0]
    c = CHUNK
    nc = s // c
    hd = GDN_DK
    tb = c * cps
    nt = nc // cps

    def body(qkv_ref, z_ref, ba_ref, sall_ref, dy_ref, al_ref, dt_ref, ng_ref,
             dqkv_ref, dz_ref, dba_ref, dal_ref, ddt_ref, dng_ref, ds_scr):
        @pl.when(pl.program_id(0) == 0)
        def _():
            ds_scr[...] = jnp.zeros_like(ds_scr)
            dal_ref[...] = jnp.zeros_like(dal_ref)
            ddt_ref[...] = jnp.zeros_like(ddt_ref)
            dng_ref[...] = jnp.zeros_like(dng_ref)

        nh = GDN_HEADS
        ba = ba_ref[...]
        g, gvf = jax.vjp(_gdn_g, ba, al_ref[...], dt_ref[...])
        gcums = [_chunk_cumsum(g[c * ci:c * (ci + 1)]) for ci in range(cps)]
        loc, loc_vf = jax.vjp(_gdn_local, *_gdn_inputs(qkv_ref, ba, gcums, cps))
        d_loc = [None] * cps
        dng = jnp.zeros((1, 128), F32)
        for ci in reversed(range(cps)):
            rs = slice(c * ci, c * (ci + 1))
            sl = slice(nh * ci, nh * (ci + 1))
            zz = jnp.stack([z_ref[rs, hd * h:hd * (h + 1)] for h in range(nh)])
            dyy = jnp.stack([dy_ref[rs, hd * h:hd * (h + 1)] for h in range(nh)])
            _, rvf = jax.vjp(_gdn_recur, sall_ref[ci], *(t[sl] for t in loc), zz, ng_ref[...])
            ds, du, dwk, dattn, dqg, dkd, deg, dzz, dn = rvf((dyy, ds_scr[...]))
            ds_scr[...] = ds
            d_loc[ci] = (du, dwk, dattn, dqg, dkd, deg)
            dng = dng + dn
            for h in range(nh):
                dz_ref[rs, hd * h:hd * (h + 1)] = dzz[h].astype(BF16)
        dq, dk, dv, dbp, dgc = loc_vf(tuple(jnp.concatenate([d_loc[ci][j] for ci in range(cps)], axis=0)
                                            for j in range(6)))
        dgs, dbas = [], []
        for ci in range(cps):
            rs = slice(c * ci, c * (ci + 1))
            dgcum = jnp.zeros((c, 128), F32)
            dba = jnp.zeros((c, 128), F32)
            for h in range(nh):
                b = nh * ci + h
                dqkv_ref[rs, hd * h:hd * (h + 1)] = dq[b]
                dqkv_ref[rs, 512 + hd * h:512 + hd * (h + 1)] = dk[b]
                dqkv_ref[rs, 1024 + hd * h:1024 + hd * (h + 1)] = dv[b]
                dgcum = dgcum + _put_col(dgc[b], 4 + h, 128)
                dba = dba + _put_col(dbp[b], h, 128)
            dgs.append(_chunk_cumsum(dgcum, transpose=True))
            dbas.append(dba)
        dba2, dal, ddt = gvf(jnp.concatenate(dgs, axis=0))
        dba_ref[...] = (jnp.concatenate(dbas, axis=0) + dba2).astype(BF16)
        dal_ref[...] += dal
        ddt_ref[...] += ddt
        dng_ref[...] += dng

    def rev(width):
        return pl.BlockSpec((tb, width), lambda i: (nt - 1 - i, 0))

    return pl.pallas_call(
        body, name=name, grid=(nt,),
        in_specs=[rev(1536), _pblk(tb, 512, C_Z, rev_n=nt), _pblk(tb, 128, C_BA, rev_n=nt),
                  pl.BlockSpec((cps, GDN_HEADS, hd, hd), lambda i: (nt - 1 - i, 0, 0, 0)), rev(512),
                  _full((1, 128)), _full((1, 128)), _full((1, 128))],
        out_specs=[rev(1536), rev(512), rev(128), _full((1, 128)), _full((1, 128)), _full((1, 128))],
        out_shape=[jax.ShapeDtypeStruct((s, 1536), F32), jax.ShapeDtypeStruct((s, 512), BF16),
                   jax.ShapeDtypeStruct((s, 128), BF16), jax.ShapeDtypeStruct((1, 128), F32),
                   jax.ShapeDtypeStruct((1, 128), F32), jax.ShapeDtypeStruct((1, 128), F32)],
        scratch_shapes=[pltpu.VMEM((GDN_HEADS, hd, hd), F32)],
        compiler_params=_params("arbitrary"),
    )(qkv, proj, proj, sall, dy, alog, dt, ng)


def merge_fwd(x, ys, proj, wb, wo, *, tb, dep=None, name):
    s, d = x.shape
    dep_specs, dep_args = _dep_operand(dep)

    def body(x_ref, ya, yb, yc, yd, g0, g1, g2, g3, wb_ref, wo_ref, *rest):
        o_ref, m_ref = rest[len(dep_args):]
        m = jnp.zeros((tb, d), F32)
        for g, (y_ref, gp_ref) in enumerate(zip((ya, yb, yc, yd), (g0, g1, g2, g3))):
            m = m + jax.nn.sigmoid(gp_ref[...]) * _dg(y_ref[...], wb_ref[g], 1, 0)
        mb = m.astype(BF16)
        m_ref[...] = mb
        o_ref[...] = x_ref[...] + _dg(mb, wo_ref[...], 1, 0)

    row = pl.BlockSpec((tb, d), lambda i: (i, 0))
    yrow = pl.BlockSpec((tb, BRANCH_W), lambda i: (i, 0))
    return pl.pallas_call(
        body, name=name, grid=(s // tb,),
        in_specs=[row] + [yrow] * 4 + [_pblk(tb, d, C_GATE + d * g) for g in range(4)]
        + [_full((4, BRANCH_W, d)), _full((d, d))] + dep_specs,
        out_specs=[row, row],
        out_shape=[jax.ShapeDtypeStruct((s, d), F32), jax.ShapeDtypeStruct((s, d), BF16)],
        compiler_params=_params("parallel"),
    )(x, *ys, proj, proj, proj, proj, wb, wo, *dep_args)


def merge_bwd(dx, ys, proj, wb, wo, *, tb, dep=None, name):
    s, d = dx.shape
    dep_specs, dep_args = _dep_operand(dep)

    def body(dx_ref, ya, yb, yc, yd, g0, g1, g2, g3, wb_ref, wo_ref, *rest):
        dgp_ref, dbr_ref, da, db, dc, dd, dxb_ref = rest[len(dep_args):]
        dxb = dx_ref[...].astype(BF16)
        dxb_ref[...] = dxb
        dm = _dg(dxb, wo_ref[...], 1, 1)
        for g, (y_ref, gp_ref, dy_ref) in enumerate(zip((ya, yb, yc, yd), (g0, g1, g2, g3), (da, db, dc, dd))):
            br = _dg(y_ref[...], wb_ref[g], 1, 0)
            sg = jax.nn.sigmoid(gp_ref[...])
            dgp_ref[:, d * g:d * (g + 1)] = (dm * br * sg * (1.0 - sg)).astype(BF16)
            dbr = (dm * sg).astype(BF16)
            dbr_ref[g] = dbr
            dy_ref[...] = _dg(dbr, wb_ref[g], 1, 1)

    row = pl.BlockSpec((tb, d), lambda i: (i, 0))
    yrow = pl.BlockSpec((tb, BRANCH_W), lambda i: (i, 0))
    return pl.pallas_call(
        body, name=name, grid=(s // tb,),
        in_specs=[row] + [yrow] * 4 + [_pblk(tb, d, C_GATE + d * g) for g in range(4)]
        + [_full((4, BRANCH_W, d)), _full((d, d))] + dep_specs,
        out_specs=[pl.BlockSpec((tb, 4 * d), lambda i: (i, 0)), pl.BlockSpec((4, tb, d), lambda i: (0, i, 0)),
                   yrow, yrow, yrow, yrow, row],
        out_shape=[jax.ShapeDtypeStruct((s, 4 * d), BF16), jax.ShapeDtypeStruct((4, s, d), BF16)]
        + [jax.ShapeDtypeStruct((s, BRANCH_W), F32)] * 4 + [jax.ShapeDtypeStruct((s, d), BF16)],
        compiler_params=_params("parallel"),
    )(dx, *ys, proj, proj, proj, proj, wb, wo, *dep_args)


def loss_head(x, g, target, *, tb, name):
    s, d = x.shape

    def body(x_ref, g_ref, t_ref, dx_ref, dg_ref, loss_ref):
        @pl.when(pl.program_id(0) == 0)
        def _():
            dg_ref[...] = jnp.zeros_like(dg_ref)
            loss_ref[...] = jnp.zeros_like(loss_ref)

        xv, gv = x_ref[...], g_ref[...]
        r = lax.rsqrt(jnp.mean(xv * xv, axis=-1, keepdims=True) + EPS)
        err = xv * r * gv - t_ref[...]
        loss_ref[...] += 0.5 * jnp.sum(jnp.mean(err * err, axis=-1, keepdims=True), axis=0, keepdims=True)
        dx, dg = _rms_bwd(xv, gv, err * (1.0 / d))
        dx_ref[...] = dx
        dg_ref[...] += dg

    row = pl.BlockSpec((tb, d), lambda i: (i, 0))
    return pl.pallas_call(
        body, name=name, grid=(s // tb,),
        in_specs=[row, _full((1, d)), row],
        out_specs=[row, _full((1, d)), _full((1, 128))],
        out_shape=[jax.ShapeDtypeStruct((s, d), F32), jax.ShapeDtypeStruct((1, d), F32),
                   jax.ShapeDtypeStruct((1, 128), F32)],
        compiler_params=_params("arbitrary"),
    )(x, g, target)


def adamw(w, g, m, v, *, name):
    shape = w.shape
    lead, (r, c) = shape[:-2], shape[-2:]
    tr = r
    while tr * c * 4 > (1 << 20) and tr % 16 == 0:
        tr //= 2
    c1 = 1.0 / (1.0 - ADAM_B1 ** ADAM_STEP)
    c2 = 1.0 / (1.0 - ADAM_B2 ** ADAM_STEP)

    def body(w_ref, g_ref, m_ref, v_ref, d_ref, nm_ref, nv_ref):
        gv = g_ref[...]
        nm = ADAM_B1 * m_ref[...] + (1.0 - ADAM_B1) * gv
        nv = ADAM_B2 * v_ref[...] + (1.0 - ADAM_B2) * (gv * gv)
        nm_ref[...] = nm
        nv_ref[...] = nv
        d_ref[...] = -ADAM_LR * ((nm * c1) / (jnp.sqrt(nv * c2) + ADAM_EPS) + ADAM_WD * w_ref[...])

    blk = pl.BlockSpec((None,) * len(lead) + (tr, c), lambda *i: i + (0,))
    grid = lead + (r // tr,)
    if len(lead) == 1 and r < 8:
        tl = max(t for t in range(1, lead[0] + 1) if lead[0] % t == 0 and t * r * c * 4 <= (1 << 20))
        blk = pl.BlockSpec((tl, r, c), lambda i: (i, 0, 0))
        grid = (lead[0] // tl,)
    return tuple(pl.pallas_call(
        body, name=name, grid=grid,
        in_specs=[blk] * 4, out_specs=[blk] * 3,
        out_shape=[jax.ShapeDtypeStruct(shape, F32)] * 3,
        compiler_params=_params(*(("parallel",) * len(grid))),
    )(w, g, m, v))


_ANY = pl.BlockSpec(memory_space=pl.ANY)


def _place():
    x, y, c = lax.axis_index("x"), lax.axis_index("y"), lax.axis_index("c")
    return x, y, c, [(1 - x, y), (x, 1 - y), (1 - x, 1 - y)]


def gather_shards(bufs, *, name):
    n = len(bufs)

    def body(*refs):
        outs = refs[n:2 * n]
        ici_send, ici_recv, d2d_send, d2d_recv = refs[2 * n:]
        x, y, c, chips = _place()
        me = 2 * x + y

        def ici(a, j, slab, to):
            return pltpu.make_async_remote_copy(
                src_ref=outs[a].at[slab, c], dst_ref=outs[a].at[slab, c], send_sem=ici_send.at[a, j],
                recv_sem=ici_recv.at[a, j], device_id=to, device_id_type=MESH)

        def d2d(a, j, slab, half):
            return pltpu.make_async_remote_copy(
                src_ref=outs[a].at[slab, half], dst_ref=outs[a].at[slab, half], send_sem=d2d_send.at[a, j],
                recv_sem=d2d_recv.at[a, j], device_id=(x, y, 1 - c), device_id_type=MESH)

        sends = []
        for a in range(n):
            for j, (px, py) in enumerate(chips):
                cp = ici(a, j, me, (px, py, c))
                cp.start()
                sends.append(cp)
        for a in range(n):
            for j, (px, py) in enumerate(chips):
                ici(a, j, 2 * px + py, (px, py, c)).wait_recv()
                cp = d2d(a, j, 2 * px + py, c)
                cp.start()
                sends.append(cp)
        for a in range(n):
            for j, (px, py) in enumerate(chips):
                d2d(a, j, 2 * px + py, 1 - c).wait_recv()
        for cp in sends:
            cp.wait_send()

    return pl.pallas_call(
        body, name=name,
        in_specs=[_ANY] * n, out_specs=[_ANY] * n,
        out_shape=[jax.ShapeDtypeStruct(t.shape, t.dtype) for t in bufs],
        input_output_aliases={a: a for a in range(n)},
        scratch_shapes=[pltpu.SemaphoreType.DMA((n, 3))] * 4,
    )(*bufs)


_SEM = pl.BlockSpec(memory_space=pltpu.SEMAPHORE)
_VMEM = pl.BlockSpec(memory_space=pltpu.VMEM)
_EFFECT = pltpu.SideEffectType.DATAFLOW_SIDE_EFFECTING
_TOKEN = jax.ShapeDtypeStruct((8, 128), F32)


def _gather_copies(outs, sems_ici, sems_d2d):
    x, y, c, chips = _place()

    def ici(a, j, slab, to):
        return pltpu.make_async_remote_copy(
            src_ref=outs[a].at[slab, c], dst_ref=outs[a].at[slab, c], send_sem=sems_ici[0][3 * a + j],
            recv_sem=sems_ici[1][3 * a + j], device_id=to, device_id_type=MESH)

    def d2d(a, j, slab, half):
        return pltpu.make_async_remote_copy(
            src_ref=outs[a].at[slab, half], dst_ref=outs[a].at[slab, half], send_sem=sems_d2d[0][3 * a + j],
            recv_sem=sems_d2d[1][3 * a + j], device_id=(x, y, 1 - c), device_id_type=MESH)

    return x, y, c, chips, ici, d2d


def _async_call(body, bufs, sems_in, after, sems_out, token, *, name):
    n = len(bufs)
    n_out = 6 * n if sems_out else 0
    extra = [] if after is None else [after]
    return pl.pallas_call(
        body, name=name,
        in_specs=[_ANY] * n + [_SEM] * len(sems_in) + [_ANY] * len(extra),
        out_specs=[_ANY] * n + [_SEM] * n_out + [_VMEM] * token,
        out_shape=[jax.ShapeDtypeStruct(t.shape, t.dtype) for t in bufs]
        + [pltpu.SemaphoreType.DMA(())] * n_out + [_TOKEN] * token,
        input_output_aliases={a: a for a in range(n)},
        compiler_params=pltpu.CompilerParams(has_side_effects=_EFFECT),
    )(*bufs, *sems_in, *extra)


def gather_start(bufs, after=None, *, name):
    n = len(bufs)
    k = n + (after is not None)

    def body(*refs):
        outs = refs[k:k + n]
        sems, token = refs[k + n:k + 7 * n], refs[k + 7 * n]
        x, y, c, chips, ici, _ = _gather_copies(outs, (sems[:3 * n], sems[3 * n:]), None)
        for a in range(n):
            for j, (px, py) in enumerate(chips):
                ici(a, j, 2 * x + y, (px, py, c)).start()
        token[...] = jnp.zeros_like(token)

    res = _async_call(body, bufs, [], after, True, True, name=name)
    return res[:n], res[n:7 * n], res[7 * n]


def gather_pass(bufs, sems, after, *, name):
    n = len(bufs)

    def body(*refs):
        s_in = refs[n:7 * n]
        outs = refs[7 * n + 1:8 * n + 1]
        s_out, token = refs[8 * n + 1:14 * n + 1], refs[14 * n + 1]
        x, y, c, chips, ici, d2d = _gather_copies(outs, (s_in[:3 * n], s_in[3 * n:]),
                                                  (s_out[:3 * n], s_out[3 * n:]))
        for a in range(n):
            for j, (px, py) in enumerate(chips):
                ici(a, j, 2 * px + py, (px, py, c)).wait_recv()
                d2d(a, j, 2 * px + py, c).start()
        for a in range(n):
            for j, (px, py) in enumerate(chips):
                ici(a, j, 2 * x + y, (px, py, c)).wait_send()
        token[...] = jnp.zeros_like(token)

    res = _async_call(body, bufs, list(sems), after, True, True, name=name)
    return res[:n], res[n:7 * n], res[7 * n]


def gather_wait(bufs, sems, after, *, name):
    n = len(bufs)

    def body(*refs):
        s_in = refs[n:7 * n]
        outs = refs[7 * n + 1:]
        x, y, c, chips, _, d2d = _gather_copies(outs, None, (s_in[:3 * n], s_in[3 * n:]))
        for a in range(n):
            for j, (px, py) in enumerate(chips):
                d2d(a, j, 2 * px + py, 1 - c).wait_recv()
                d2d(a, j, 2 * px + py, c).wait_send()

    return _async_call(body, bufs, list(sems), after, False, False, name=name)


def _pair_plan(srcs, lands):
    x, y, c, _ = _place()
    return [(srcs[a].at[j, 1 - c], lands[a].at[j], (x, y, 1 - c))
            for a in range(len(srcs)) for j in range(N_SHARD)]


def _chip_plan(srcs, lands):
    x, y, c, chips = _place()
    return [(srcs[a].at[2 * px + py], lands[a].at[2 * x + y], (px, py, c))
            for a in range(len(srcs)) for (px, py) in chips]


def copies_start(arrs, land_shapes, plan, per_array, *, name):
    n = len(arrs)
    k = per_array * n

    def body(*refs):
        lands = refs[2 * n:3 * n]
        sems, token = refs[3 * n:3 * n + 2 * k], refs[3 * n + 2 * k]
        for i, (src, dst, dev) in enumerate(plan(refs[n:2 * n], lands)):
            pltpu.make_async_remote_copy(src_ref=src, dst_ref=dst, send_sem=sems[i], recv_sem=sems[k + i],
                                         device_id=dev, device_id_type=MESH).start()
        token[...] = jnp.zeros_like(token)

    res = pl.pallas_call(
        body, name=name,
        in_specs=[_ANY] * n,
        out_specs=[_ANY] * (2 * n) + [_SEM] * (2 * k) + [_VMEM],
        out_shape=[jax.ShapeDtypeStruct(t.shape, t.dtype) for t in arrs] + list(land_shapes)
        + [pltpu.SemaphoreType.DMA(())] * (2 * k) + [_TOKEN],
        input_output_aliases={a: a for a in range(n)},
        compiler_params=pltpu.CompilerParams(has_side_effects=_EFFECT),
    )(*arrs)
    return res[:n], res[n:2 * n], res[2 * n:2 * n + 2 * k], res[2 * n + 2 * k]


def copies_wait(arrs, lands, sems, after, plan, *, name):
    n = len(arrs)
    k = len(sems) // 2

    def body(*refs):
        s_in = refs[2 * n:2 * n + 2 * k]
        outs = refs[2 * n + 2 * k + 1:]
        for i, (src, dst, dev) in enumerate(plan(outs[:n], outs[n:])):
            cp = pltpu.make_async_remote_copy(src_ref=src, dst_ref=dst, send_sem=s_in[i], recv_sem=s_in[k + i],
                                              device_id=dev, device_id_type=MESH)
            cp.wait_send()
            cp.wait_recv()

    res = pl.pallas_call(
        body, name=name,
        in_specs=[_ANY] * (2 * n) + [_SEM] * (2 * k) + [_ANY],
        out_specs=[_ANY] * (2 * n),
        out_shape=[jax.ShapeDtypeStruct(t.shape, t.dtype) for t in list(arrs) + list(lands)],
        input_output_aliases={a: a for a in range(2 * n)},
        compiler_params=pltpu.CompilerParams(has_side_effects=_EFFECT),
    )(*arrs, *lands, *sems, after)
    return res[:n], res[n:]


def pair_gather(bufs, *, dep=None, name):
    n = len(bufs)
    dep_specs, dep_args = _dep_operand(dep)
    k = n + len(dep_args)

    def body(*refs):
        outs = refs[k:k + n]
        send_sems, recv_sems = refs[k + n:]
        x, y, c, _ = _place()
        copies = []
        for a in range(n):
            cp = pltpu.make_async_remote_copy(
                src_ref=outs[a].at[c], dst_ref=outs[a].at[c], send_sem=send_sems.at[a],
                recv_sem=recv_sems.at[a], device_id=(x, y, 1 - c), device_id_type=MESH)
            cp.start()
            copies.append(cp)
        for cp in copies:
            cp.wait()

    return pl.pallas_call(
        body, name=name,
        in_specs=[_ANY] * n + dep_specs, out_specs=[_ANY] * n,
        out_shape=[jax.ShapeDtypeStruct(t.shape, t.dtype) for t in bufs],
        input_output_aliases={a: a for a in range(n)},
        scratch_shapes=[pltpu.SemaphoreType.DMA((n,)), pltpu.SemaphoreType.DMA((n,))],
    )(*bufs, *dep_args)


def all_sum_small(v, *, name):
    m = v.shape[0]

    def body(v_ref, o_ref, buf, send_sems, recv_sems, local_sem):
        x, y, c, chips = _place()
        me, sibling = (x, y, c), (x, y, 1 - c)

        def rows(px, py, pc):
            return buf.at[pl.ds((4 * px + 2 * py + pc) * m, m), :]

        def copy(k, block, to, src=None):
            return pltpu.make_async_remote_copy(
                src_ref=rows(*block) if src is None else src, dst_ref=rows(*block),
                send_sem=send_sems.at[k], recv_sem=recv_sems.at[k], device_id=to, device_id_type=MESH)

        mine = pltpu.make_async_copy(v_ref, rows(*me), local_sem)
        mine.start()
        first = [copy(0, me, sibling, src=v_ref)]
        first += [copy(1 + j, me, (*chip, c), src=v_ref) for j, chip in enumerate(chips)]
        for cp in first:
            cp.start()
        passed = [copy(4 + j, (*chip, c), sibling) for j, chip in enumerate(chips)]
        for j, chip in enumerate(chips):
            copy(1 + j, (*chip, c), me).wait_recv()
            passed[j].start()
        copy(0, sibling, me).wait_recv()
        for j, chip in enumerate(chips):
            copy(4 + j, (*chip, 1 - c), me).wait_recv()
        for cp in first + passed:
            cp.wait_send()
        mine.wait()
        acc = buf[pl.ds(0, m), :]
        for k in range(1, 8):
            acc = acc + buf[pl.ds(k * m, m), :]
        o_ref[...] = acc

    vm = pl.BlockSpec(memory_space=pltpu.VMEM)
    return pl.pallas_call(
        body, name=name, in_specs=[vm], out_specs=vm,
        out_shape=jax.ShapeDtypeStruct((m, 128), F32),
        scratch_shapes=[pltpu.VMEM((8 * m, 128), F32), pltpu.SemaphoreType.DMA((7,)),
                        pltpu.SemaphoreType.DMA((7,)), pltpu.SemaphoreType.DMA],
    )(v)


def pack_slab(srcs, blocks, maps, n_slots, slot_shape, *, place, dep=None, name):
    k = len(srcs)
    per = n_slots // k
    dep_specs, dep_args = _dep_operand(dep)

    def body(place_ref, *refs):
        o_ref = refs[k + len(dep_args)]
        s = pl.program_id(0)
        for i in range(k):
            @pl.when(jnp.logical_and(s >= i * per, s < (i + 1) * per))
            def _():
                o_ref[...] = refs[i][...].astype(BF16)

    return pl.pallas_call(
        body, name=name,
        grid_spec=pltpu.PrefetchScalarGridSpec(
            num_scalar_prefetch=1, grid=(n_slots,),
            in_specs=[pl.BlockSpec(blk, functools.partial(lambda s, pr, m: m(s), m=m)) for blk, m in zip(blocks, maps)]
            + dep_specs,
            out_specs=pl.BlockSpec((None, None) + slot_shape, lambda s, pr: (pr[1], s, 0, 0))),
        out_shape=jax.ShapeDtypeStruct((N_SHARD, n_slots) + slot_shape, BF16),
        compiler_params=_params("arbitrary"),
    )(place, *srcs, *dep_args)


def sum_halves(arr, recv, c_idx, *, name):
    _, _, r, c = arr.shape
    tr = r
    while tr * c * 4 > (1 << 20) and tr % 32 == 0:
        tr //= 2

    def body(c_ref, a_ref, b_ref, o_ref):
        o_ref[...] = (a_ref[...] + b_ref[...]).astype(BF16)

    return pl.pallas_call(
        body, name=name,
        grid_spec=pltpu.PrefetchScalarGridSpec(
            num_scalar_prefetch=1, grid=(N_SHARD, r // tr),
            in_specs=[pl.BlockSpec((None, None, tr, c), lambda j, i, cr: (j, cr[0], i, 0)),
                      pl.BlockSpec((None, tr, c), lambda j, i, cr: (j, i, 0))],
            out_specs=pl.BlockSpec((None, tr, c), lambda j, i, cr: (j, i, 0))),
        out_shape=jax.ShapeDtypeStruct((N_SHARD, r, c), BF16),
        compiler_params=_params("parallel", "parallel"),
    )(c_idx, arr, recv)


def sum_chips(recv, part, place, *, name):
    _, r, c = recv.shape
    tr = r
    while tr * c * 4 > (1 << 20) and tr % 32 == 0:
        tr //= 2

    def body(place_ref, a0, a1, a2, a3, p_ref, o_ref):
        me = place_ref[1]
        acc = None
        for k, a_ref in enumerate((a0, a1, a2, a3)):
            term = jnp.where(me == k, p_ref[...], a_ref[...]).astype(F32)
            acc = term if acc is None else acc + term
        o_ref[...] = acc

    def slab(k):
        return pl.BlockSpec((None, tr, c), lambda i, pr: (jnp.where(pr[1] == k, (k + 1) % N_SHARD, k), i, 0))

    return pl.pallas_call(
        body, name=name,
        grid_spec=pltpu.PrefetchScalarGridSpec(
            num_scalar_prefetch=1, grid=(r // tr,),
            in_specs=[slab(k) for k in range(N_SHARD)]
            + [pl.BlockSpec((None, tr, c), lambda i, pr: (pr[1], i, 0))],
            out_specs=pl.BlockSpec((None, tr, c), lambda i, pr: (pr[0], i, 0))),
        out_shape=jax.ShapeDtypeStruct((2, r, c), F32),
        compiler_params=_params("parallel"),
    )(place, recv, recv, recv, recv, part)


_WEIGHTS = ("ff1_norm", "ff1_wg", "ff1_wu", "ff1_wd", "mix_norm", "w_in", "sgu_ln_g", "sgu_ln_b", "sgu_w",
            "sgu_b", "lru_conv_w", "lru_conv_b", "lru_wa", "lru_ba", "lru_wx", "lru_bx", "lru_lambda",
            "gdn_conv_w", "gdn_a_log", "gdn_dt_bias", "gdn_norm_g", "pool_w", "pool_scale", "w_branch",
            "w_out", "ff2_norm", "ff2_wg", "ff2_wu", "ff2_wd", "final_norm")
_BIG = ("ff1_wg", "ff1_wu", "ff1_wd", "w_in", "w_branch", "w_out", "ff2_wg", "ff2_wu", "ff2_wd")
_SMALL = tuple(n for n in _WEIGHTS if n not in _BIG)
_CONV = ("lru_conv_w", "gdn_conv_w")


def _seg_rows(shape):
    return -(-math.prod(shape) // 1024) * 8


def _pack(arrs):
    segs = []
    for t in arrs:
        rows = _seg_rows(t.shape)
        flat = t.reshape(-1)
        segs.append(jnp.pad(flat, (0, rows * 128 - flat.shape[0])).reshape(rows, 128))
    return jnp.concatenate(segs, axis=0)


def _unpack(buf, shapes):
    out, o = [], 0
    for shp in shapes:
        rows = _seg_rows(shp)
        out.append(buf[o:o + rows].reshape(-1)[:math.prod(shp)].reshape(shp))
        o += rows
    return out


def _block_diag(w):
    h, n, _ = w.shape
    same = jnp.arange(h)[:, None, None, None] == jnp.arange(h)[None, None, :, None]
    return jnp.where(same, w[:, :, None, :], 0.0).reshape(h * n, h * n)


def _diag_blocks(m, h=8, n=64):
    return jnp.stack([m[i * n:(i + 1) * n, i * n:(i + 1) * n] for i in range(h)])


def _lane_row(v):
    return jnp.zeros((1, 128), F32).at[0, 4:8].set(v)


def _w_in_to_padded(raw):
    sem = jnp.concatenate([raw[k, :, :P_IN_SHARD] for k in range(N_SHARD)], axis=1)
    zero = jnp.zeros((sem.shape[0], C_GATE - C_BA - 8), sem.dtype)
    return jnp.concatenate([sem[:, 2048:4096], sem[:, 0:2048], sem[:, 4104:4616], sem[:, 4096:4104], zero,
                            sem[:, 4616:P_IN]], axis=1)


def _w_in_grad_to_shards(part):
    r = part.shape[1]
    dwp = part.transpose(1, 0, 2).reshape(r, P_PAD)
    sem = jnp.concatenate([dwp[:, C_AU:C_DX], dwp[:, 0:C_AU], dwp[:, C_BA:C_BA + 8], dwp[:, C_DX:C_BA],
                           dwp[:, C_GATE:]], axis=1)
    sh = sem.reshape(r, N_SHARD, P_IN_SHARD).transpose(1, 0, 2)
    return jnp.pad(sh, ((0, 0), (0, 0), (0, P_IN_SHARD_PAD - P_IN_SHARD)))


def kernel(x, ff1_norm, ff1_wg, ff1_wu, ff1_wd, mix_norm, w_in, sgu_ln_g, sgu_ln_b, sgu_w, sgu_b, lru_conv_w,
           lru_conv_b, lru_wa, lru_ba, lru_wx, lru_bx, lru_lambda, gdn_conv_w, gdn_a_log, gdn_dt_bias, gdn_norm_g,
           pool_w, pool_scale, w_branch, w_out, ff2_norm, ff2_wg, ff2_wu, ff2_wd, final_norm, loss_target, m_ff1_norm,
           m_ff1_wg, m_ff1_wu, m_ff1_wd, m_mix_norm, m_w_in, m_sgu_ln_g, m_sgu_ln_b, m_sgu_w, m_sgu_b, m_lru_conv_w,
           m_lru_conv_b, m_lru_wa, m_lru_ba, m_lru_wx, m_lru_bx, m_lru_lambda, m_gdn_conv_w, m_gdn_a_log, m_gdn_dt_bias,
           m_gdn_norm_g, m_pool_w, m_pool_scale, m_w_branch, m_w_out, m_ff2_norm, m_ff2_wg, m_ff2_wu, m_ff2_wd,
           m_final_norm, v_ff1_norm, v_ff1_wg, v_ff1_wu, v_ff1_wd, v_mix_norm, v_w_in, v_sgu_ln_g, v_sgu_ln_b, v_sgu_w,
           v_sgu_b, v_lru_conv_w, v_lru_conv_b, v_lru_wa, v_lru_ba, v_lru_wx, v_lru_bx, v_lru_lambda, v_gdn_conv_w,
           v_gdn_a_log, v_gdn_dt_bias, v_gdn_norm_g, v_pool_w, v_pool_scale, v_w_branch, v_w_out, v_ff2_norm, v_ff2_wg,
           v_ff2_wu, v_ff2_wd, v_final_norm):
    a = dict(locals())
    n_layer = ff1_norm.shape[0]
    d = D_MODEL
    x0 = x[0]
    chip = 2 * lax.axis_index("x") + lax.axis_index("y")
    c_idx = jnp.reshape(lax.axis_index("c"), (1,)).astype(jnp.int32)
    place = jnp.stack([lax.axis_index("c"), chip]).astype(jnp.int32)

    def own_slab(t, dtype=BF16):
        buf = lax.empty((N_SHARD,) + t.shape, dtype)
        return lax.dynamic_update_slice(buf, t.astype(dtype)[None], (chip,) + (0,) * t.ndim)

    assert n_layer == 2

    def layer_bufs(l, dep=None):
        fc = FF_SHARD
        whole = lambda s: (l, 0, 0)
        rows = lambda s: (l, s, 0)
        pack = functools.partial(pack_slab, place=place, dep=dep)
        bufs = []
        for tag, wg, wu, wd in (("ffn1", ff1_wg, ff1_wu, ff1_wd), ("ffn2", ff2_wg, ff2_wu, ff2_wd)):
            bufs.append([pack([wg, wu], [(None, d, fc)] * 2, [whole] * 2, 2, (d, fc), name=f"pack_{tag}_wgu_{l}"),
                         pack([wd], [(None, fc // 2, d)], [rows], 2, (fc // 2, d), name=f"pack_{tag}_wd_{l}")])
        win = pack([w_in], [(None, d // 2, P_IN_SHARD_PAD)], [rows], 2, (d // 2, P_IN_SHARD_PAD),
                   name=f"pack_w_in_{l}")
        wb = pack([w_branch], [(None, None, BRANCH_W, d // 4)], [lambda s: (l, s, 0, 0)], 4, (BRANCH_W, d // 4),
                  name=f"pack_wb_{l}")
        wo = pack([w_out], [(None, d // 8, d)], [rows], 2, (d // 8, d), name=f"pack_wo_{l}")
        return bufs[0] + [win, wb.reshape(N_SHARD, 2, 2, BRANCH_W, d // 4), wo] + bufs[1]

    n_buf = 7
    groups = {"ffn1": (0, 2), "mix": (2, 5), "ffn2": (5, 7)}

    def group_sems(sems, lo, hi):
        return list(sems[3 * lo:3 * hi]) + list(sems[3 * n_buf + 3 * lo:3 * n_buf + 3 * hi])

    def ffn_weights(wgu, wd):
        return (wgu, wgu, wd.reshape(N_SHARD, 1, FF_SHARD, d), (0, 1, 0))

    def mix_weights(g_w_in, g_wb, g_wo):
        return {"w_in": _w_in_to_padded(g_w_in.reshape(4, d, P_IN_SHARD_PAD)),
                "wb": g_wb.reshape(4, 4, BRANCH_W, d // 4).transpose(1, 2, 0, 3).reshape(4, BRANCH_W, d),
                "wo": g_wo.reshape(d, d)}

    g_lcw, g_gcw = gather_shards([own_slab(lru_conv_w, F32), own_slab(gdn_conv_w, F32)], name="gather_conv_weights")

    def small_params(l):
        return {"lcw": g_lcw[:, l].transpose(1, 0, 2).reshape(4, BRANCH_W),
                "gcw": g_gcw[:, l].transpose(1, 0, 2).reshape(4, 3 * BRANCH_W),
                "lcb": lru_conv_b[l][None],
                "bt": jnp.zeros((128, 128), F32).at[:, :4].set(sgu_b[l].T),
                "wa": _block_diag(lru_wa[l]), "wx": _block_diag(lru_wx[l]),
                "alog": _lane_row(gdn_a_log[l]), "dt": _lane_row(gdn_dt_bias[l])}

    flying = {0: gather_start(layer_bufs(0), name="gather_weights_0_start")[:2]}
    handed = {}

    def hand_on(l, grp, after):
        lo, hi = groups[grp]
        bufs_l, sems_l = flying[l]
        got, sm, tk = gather_pass(bufs_l[lo:hi], group_sems(sems_l, lo, hi), after,
                                  name=f"gather_weights_{l}_{grp}_pass")
        handed[l, grp] = (got, sm)
        return tk

    def arrived(l, grp, after):
        return gather_wait(*handed.pop((l, grp)), after, name=f"gather_weights_{l}_{grp}_wait")

    saved = []
    xs = x0
    tok = hand_on(0, "ffn1", x0)
    for l in range(n_layer):
        p = small_params(l)
        sv = {"p": p, "x0": xs}
        p["ff1"] = ffn_weights(*arrived(l, "ffn1", tok))
        x1 = ffn_fwd(xs, ff1_norm[l][None], *p["ff1"], tb=512, name=f"ffn1_fwd_{l}")
        tok = hand_on(l, "mix", x1)
        if l == 0:
            bufs1 = layer_bufs(1, dep=tok)
            split = groups["ffn1"][1]
            first1 = gather_start(bufs1[:split], name="gather_weights_1_start_ffn1")
        h = norm_fwd(x1, mix_norm[l][None], tb=512, name=f"mix_norm_fwd_{l}")
        p.update(mix_weights(*arrived(l, "mix", h)))
        proj = matmul(h, p["w_in"], tm=1024, tn=1536, tk=1024, name=f"proj_{l}")
        ya = sgu_fwd(proj, sgu_ln_g[l][None], sgu_ln_b[l][None], sgu_w[l], p["bt"], tb=512, name=f"sgu_fwd_{l}")
        xc = conv_fwd(proj, C_BX, BRANCH_W, p["lcw"], p["lcb"], tb=512, name=f"lru_conv_fwd_{l}")
        yb, hl = lru_fwd(xc, proj, p["wa"], p["wx"], lru_ba[l][None], lru_bx[l][None], lru_lambda[l][None],
                         tb=256, name=f"lru_fwd_{l}")
        hand_on(l, "ffn2", yb)
        if l == 0:
            rest1 = gather_start(bufs1[split:], yb, name="gather_weights_1_start_rest")
            k1, k2 = 3 * split, 3 * (n_buf - split)
            flying[1] = (list(first1[0]) + list(rest1[0]),
                         list(first1[1][:k1]) + list(rest1[1][:k2]) + list(first1[1][k1:]) + list(rest1[1][k2:]))
        qkv = conv_fwd(proj, C_QKV, 3 * BRANCH_W, p["gcw"], jnp.zeros((1, 3 * BRANCH_W), F32), tb=512,
                       name=f"gdn_conv_fwd_{l}")
        yc, sall = gdn_fwd(qkv, proj, p["alog"], p["dt"], gdn_norm_g[l][None], cps=GDN_CPS, name=f"gdn_fwd_{l}")
        p["ff2"] = ffn_weights(*arrived(l, "ffn2", yc))
        dep2 = None
        if l + 1 < n_layer:
            dep2 = tok = hand_on(l + 1, "ffn1", rest1[2])
        yd = pool_fwd(proj, pool_w[l], pool_scale[l][None], tb=512, name=f"pool_fwd_{l}")
        ys = (ya, yb, yc, yd)
        x2, merged = merge_fwd(x1, ys, proj, p["wb"], p["wo"], tb=256, dep=dep2, name=f"merge_fwd_{l}")
        x3 = ffn_fwd(x2, ff2_norm[l][None], *p["ff2"], tb=512, name=f"ffn2_fwd_{l}")
        tok = x3
        sv.update(x1=x1, h=h, proj=proj, xc=xc, hl=hl, qkv=qkv, sall=sall, ys=ys, merged=merged, x2=x2)
        saved.append(sv)
        xs = x3

    dx, d_final, loss_blk = loss_head(xs, final_norm[None], loss_target[0], tb=512, name="loss_head")
    loss = lax.psum(loss_blk[0, 0], ("x", "y", "c"))

    small_g = {n: [None] * n_layer for n in _SMALL if n != "final_norm"}
    big_g = [None] * n_layer
    per_layer = 7
    flight = None

    def pair_sums(bigs, recv, tag, w_in_at):
        part = [sum_halves(t, r, c_idx, name=f"grad_pair_sum_{tag}_{i}") for i, (t, r) in enumerate(zip(bigs, recv))]
        if w_in_at is not None:
            part[w_in_at] = _w_in_grad_to_shards(part[w_in_at])
        return part

    def pair_start(arrs, tag):
        return copies_start(arrs, [jax.ShapeDtypeStruct((N_SHARD,) + t.shape[2:], t.dtype) for t in arrs],
                            _pair_plan, N_SHARD, name=f"grad_pair_exchange_{tag}_start")

    def pair_to_chip(fl, after, tag, w_in_at=None):
        sent, recv = copies_wait(*fl[:3], after, _pair_plan, name=f"grad_pair_exchange_{tag}_wait")
        part = pair_sums(sent, recv, tag, w_in_at)
        return copies_start(part, [jax.ShapeDtypeStruct(t.shape, t.dtype) for t in part], _chip_plan, 3,
                            name=f"grad_chip_exchange_{tag}_start")

    def chip_finish(fl, after, tag):
        part, recv = copies_wait(*fl[:3], after, _chip_plan, name=f"grad_chip_exchange_{tag}_wait")
        return [sum_chips(r, q, place, name=f"grad_chip_sum_{tag}_{i}") for i, (r, q) in enumerate(zip(recv, part))]

    early = mid = None

    for l in reversed(range(n_layer)):
        sv = saved[l]
        p = sv["p"]
        proj = sv["proj"]
        dx2, dg_ff2, dab2, hid2, h2, dy2 = ffn_bwd(sv["x2"], dx, ff2_norm[l][None], *p["ff2"], tb=512,
                                                  dep=None if flight is None else flight[3],
                                                  name=f"ffn2_bwd_{l}")
        dwgu2 = matmul(h2, dab2, ta=True, tm=1024, tn=FF_SHARD, tk=TK_DW, name=f"ffn2_dwgu_{l}")
        dwd2 = matmul(hid2, dy2, ta=True, tm=FF_SHARD, tn=1024, tk=TK_DW, name=f"ffn2_dwd_{l}")
        if flight is not None:
            flight = pair_to_chip(flight, dx2, "1", w_in_at=2)
        dgp, dbr, dya, dyb, dyc, dyd, dx2b = merge_bwd(dx2, sv["ys"], proj, p["wb"], p["wo"], tb=256,
                                                       dep=None if flight is None else flight[3],
                                                       name=f"merge_bwd_{l}")
        dwo = matmul(sv["merged"], dx2b, ta=True, tm=1024, tn=1024, tk=TK_DW, name=f"dwo_{l}")
        dwb = jnp.stack([matmul(sv["ys"][g], dbr[g], ta=True, tm=BRANCH_W, tn=1024, tk=TK_DW, name=f"dwb{g}_{l}")
                         for g in range(4)])
        dwb_sh = dwb.reshape(4, BRANCH_W, N_SHARD, d // N_SHARD).transpose(2, 0, 1, 3)
        g_late = [dwgu2, dwd2.reshape(N_SHARD, 2, FF_SHARD // 2, d),
                  dwb_sh.reshape(N_SHARD, 2, 2 * BRANCH_W, d // N_SHARD), dwo.reshape(N_SHARD, 2, d // 8, d)]
        tok = None
        if l == 0:
            early = pair_start(g_late, "0a")
            tok = early[3]
        du, dv, dlg, dlb, dws, dbt = sgu_bwd(proj, dya, sgu_ln_g[l][None], sgu_ln_b[l][None], sgu_w[l], p["bt"],
                                             tb=512, dep=tok, name=f"sgu_bwd_{l}")
        dxc, dgate, dwa, dwx, dba, dbx, dlam = lru_bwd(sv["xc"], proj, sv["hl"], dyb, p["wa"], p["wx"],
                                                       lru_ba[l][None], lru_bx[l][None], lru_lambda[l][None],
                                                       tb=256, name=f"lru_bwd_{l}")
        dbx_in, dcw_l = conv_bwd(proj, C_BX, BRANCH_W, dxc, p["lcw"], tb=512, name=f"lru_conv_bwd_{l}")
        dqkv, dz, dbeta, dal, ddt, dng = gdn_bwd(sv["qkv"], proj, sv["sall"], dyc, p["alog"], p["dt"],
                                                 gdn_norm_g[l][None], cps=GDN_CPS, name=f"gdn_bwd_{l}")
        dqkv_in, dcw_g = conv_bwd(proj, C_QKV, 3 * BRANCH_W, dqkv, p["gcw"], tb=512, name=f"gdn_conv_bwd_{l}")
        ddx, dwp, dsc = pool_bwd(proj, dyd, pool_w[l], pool_scale[l][None], tb=512, name=f"pool_bwd_{l}")
        if l == 0:
            early = pair_to_chip(early, ddx, "0a")
            tok = early[3]
        s = dx.shape[0]
        dproj = jnp.concatenate([dqkv_in, dz, du, dv, dbx_in, dgate, ddx, dbeta,
                                 jnp.zeros((s, C_GATE - C_BA - 128), BF16), dgp], axis=1)
        dh = matmul(dproj, p["w_in"], tb=True, tm=1024, tn=1024, tk=1536, dep=tok, name=f"dh_mix_{l}")
        dw_in = matmul(sv["h"], dproj, ta=True, tm=1024, tn=1152, tk=TK_DW, col_groups=N_SHARD, name=f"dw_in_{l}")
        dw_in = dw_in.reshape(N_SHARD, 2, d // 2, P_PAD // N_SHARD)
        if l == 0:
            mid = pair_start([dw_in], "0b")
            tok = mid[3]
        dx1, dg_mix = norm_bwd(sv["x1"], mix_norm[l][None], dh, dx2, tb=512, dep=tok, name=f"mix_norm_bwd_{l}")
        dx, dg_ff1, dab1, hid1, h1, dy1 = ffn_bwd(sv["x0"], dx1, ff1_norm[l][None], *p["ff1"],
                                                 tb=512, name=f"ffn1_bwd_{l}")
        if l == 0:
            mid = pair_to_chip(mid, dx, "0b", w_in_at=0)
            tok = mid[3]
        dwgu1 = matmul(h1, dab1, ta=True, tm=1024, tn=FF_SHARD, tk=TK_DW, dep=tok, name=f"ffn1_dwgu_{l}")
        dwd1 = matmul(hid1, dy1, ta=True, tm=FF_SHARD, tn=1024, tk=TK_DW, name=f"ffn1_dwd_{l}")

        for n, g in (("ff1_norm", dg_ff1[0]), ("mix_norm", dg_mix[0]), ("sgu_ln_g", dlg[0]), ("sgu_ln_b", dlb[0]),
                     ("sgu_w", dws), ("sgu_b", dbt[:, :4].T), ("lru_conv_w", dcw_l[:4]), ("lru_conv_b", dcw_l[4]),
                     ("lru_wa", _diag_blocks(dwa)), ("lru_ba", dba[0]), ("lru_wx", _diag_blocks(dwx)),
                     ("lru_bx", dbx[0]), ("lru_lambda", dlam[0]), ("gdn_conv_w", dcw_g[:4]),
                     ("gdn_a_log", dal[0, 4:8]), ("gdn_dt_bias", ddt[0, 4:8]), ("gdn_norm_g", dng[0]),
                     ("pool_w", dwp), ("pool_scale", dsc[0]), ("ff2_norm", dg_ff2[0])):
            small_g[n][l] = g
        big_g[l] = [dwgu1, dwd1.reshape(N_SHARD, 2, FF_SHARD // 2, d), dw_in, g_late[2], g_late[3], g_late[0], g_late[1]]
        if l == n_layer - 1:
            flight = pair_start(big_g[l], "1")

    late = pair_start(big_g[0][:2], "0c")
    red = {}
    red.update(((1, k), t) for k, t in enumerate(chip_finish(flight, late[3], "1")))
    red.update(zip(((0, 5), (0, 6), (0, 3), (0, 4)), chip_finish(early, late[3], "0a")))
    red[0, 2], = chip_finish(mid, late[3], "0b")
    late = pair_to_chip(late, red[0, 2], "0c")
    full = {}

    def pair_gather_into(keys, dep, tag):
        full.update(zip(keys, pair_gather([red[k] for k in keys], dep=dep, name=f"grad_pair_gather_{tag}")))

    pair_gather_into(sorted(red), late[3], "rest")

    def layers(k, f):
        return jnp.stack([f(full[l, k]) for l in range(n_layer)])

    grads = {}
    grads["w_in"] = layers(2, lambda t: t.reshape(d, P_IN_SHARD_PAD)[:, :P_IN_SHARD])
    grads["w_branch"] = layers(3, lambda t: t.reshape(4, BRANCH_W, d // N_SHARD))
    grads["w_out"] = layers(4, lambda t: t.reshape(d // N_SHARD, d))
    grads["ff2_wg"] = layers(5, lambda t: t[0])
    grads["ff2_wu"] = layers(5, lambda t: t[1])
    grads["ff2_wd"] = layers(6, lambda t: t.reshape(FF_SHARD, d))

    small_full = [jnp.stack(small_g[n]) for n in _SMALL if n != "final_norm"] + [d_final[0]]
    summed = _unpack(all_sum_small(_pack(small_full), name="grad_small_allreduce"), [t.shape for t in small_full])
    for n, g in zip(_SMALL, summed):
        if n in _CONV:
            width = a[n].shape[-1]
            g = lax.dynamic_slice_in_dim(g, chip * width, width, axis=2)
        grads[n] = g

    delta, new_m, new_v = {}, {}, {}

    def finish_ffn1():
        red.update(zip(((0, 0), (0, 1)), chip_finish(late, new_v["ff2_wd"], "0c")))
        pair_gather_into([(0, 0), (0, 1)], None, "0c")
        grads["ff1_wg"] = layers(0, lambda t: t[0])
        grads["ff1_wu"] = layers(0, lambda t: t[1])
        grads["ff1_wd"] = layers(1, lambda t: t.reshape(FF_SHARD, d))

    for n in sorted(_BIG, key=lambda n: n.startswith("ff1_")):
        if n.startswith("ff1_") and n not in grads:
            finish_ffn1()
        args = (a[n], grads[n], a["m_" + n], a["v_" + n])
        if a[n].shape[-1] == FF_SHARD:
            outs = adamw(*(t.swapaxes(1, 2) for t in args), name=f"adamw_{n}")
            delta[n], new_m[n], new_v[n] = (t.swapaxes(1, 2) for t in outs)
        elif n == "w_in":
            outs = adamw(*(t.transpose(2, 0, 1) for t in args), name=f"adamw_{n}")
            delta[n], new_m[n], new_v[n] = (t.transpose(1, 2, 0) for t in outs)
        else:
            delta[n], new_m[n], new_v[n] = adamw(*args, name=f"adamw_{n}")
    shapes = [a[n].shape for n in _SMALL]
    packed = [_pack([src[pre + n] for n in _SMALL]) for src, pre in ((a, ""), (grads, ""), (a, "m_"), (a, "v_"))]
    for store, buf in zip((delta, new_m, new_v), adamw(*packed, name="adamw_small")):
        store.update(zip(_SMALL, _unpack(buf, shapes)))

    return (loss, dx[None], *[grads[n] for n in _WEIGHTS], *[delta[n] for n in _WEIGHTS],
            *[new_m[n] for n in _WEIGHTS], *[new_v[n] for n in _WEIGHTS])
```

```python
import functools
import math

import jax
import jax.numpy as jnp
from jax import lax
from jax.experimental import pallas as pl
from jax.experimental.pallas import tpu as pltpu

F32 = jnp.float32
BF16 = jnp.bfloat16
MESH = pl.DeviceIdType.MESH

D_MODEL = 1024
N_SHARD = 4
D_FF = 2816
FF_SHARD = D_FF // N_SHARD
BRANCH_W = 512
CHUNK = 64
GDN_HEADS = 4
GDN_DK = 128
GDN_CPS = 4
TK_DW = 2048
LRU_C = 8.0
EPS = 1e-6
P_IN = 8712
P_IN_SHARD = P_IN // N_SHARD
P_IN_SHARD_PAD = 2304
P_PAD = 9216
C_QKV, C_Z, C_AU, C_AV, C_BX, C_BG, C_DX, C_BA, C_GATE = 0, 1536, 2048, 2560, 3072, 3584, 4096, 4608, 5120

ADAM_LR, ADAM_B1, ADAM_B2, ADAM_EPS, ADAM_WD, ADAM_STEP = 0.001, 0.9, 0.999, 1e-08, 0.01, 10


def _gelu(x):
    return 0.5 * x * (1.0 + jnp.tanh(0.7978845608028654 * (x + 0.044715 * (x * x * x))))


def _gelu_grad(x):
    u = 0.7978845608028654 * (x + 0.044715 * (x * x * x))
    t = jnp.tanh(u)
    return 0.5 * (1.0 + t) + 0.5 * x * (1.0 - t * t) * 0.7978845608028654 * (1.0 + 3.0 * 0.044715 * x * x)


def _silu(x):
    return x * jax.nn.sigmoid(x)


def _softplus(x):
    return jnp.maximum(x, 0.0) + jnp.log1p(jnp.exp(-jnp.abs(x)))


def _dg(a, b, ca, cb):
    return lax.dot_general(a.astype(BF16), b.astype(BF16), (((ca,), (cb,)), ((), ())),
                           preferred_element_type=F32)


def _dg3(a, b, ca, cb):
    a1 = a.astype(BF16)
    a2 = (a - a1.astype(F32)).astype(BF16)
    b1 = b.astype(BF16)
    b2 = (b - b1.astype(F32)).astype(BF16)
    dn = (((ca,), (cb,)), ((), ()))
    d = functools.partial(lax.dot_general, dimension_numbers=dn, preferred_element_type=F32)
    return d(a1, b1) + (d(a1, b2) + d(a2, b1))


def _make_mm(ca, cb, dot):
    @jax.custom_vjp
    def f(a, b):
        return dot(a, b, ca, cb)

    def fwd(a, b):
        return dot(a, b, ca, cb), (a, b)

    def bwd(res, g):
        a, b = res
        if (ca, cb) == (1, 0):
            return dot(g, b, 1, 1), dot(a, g, 0, 0)
        if (ca, cb) == (1, 1):
            return dot(g, b, 1, 0), dot(g, a, 0, 0)
        return dot(b, g, 1, 1), dot(a, g, 1, 0)

    f.defvjp(fwd, bwd)
    return f


mm_nn = _make_mm(1, 0, _dg)


def _bdg(a, b, ca, cb):
    return lax.dot_general(a.astype(BF16), b.astype(BF16), (((ca,), (cb,)), ((0,), (0,))),
                           preferred_element_type=F32)


def _split2(x):
    x1 = x.astype(BF16)
    return x1, (x - x1.astype(F32)).astype(BF16)


def _bdg3(a, b, ca, cb):
    a1, a2 = _split2(a)
    b1, b2 = _split2(b)
    d = functools.partial(lax.dot_general, dimension_numbers=(((ca,), (cb,)), ((0,), (0,))),
                          preferred_element_type=F32)
    return d(a1, b1) + (d(a1, b2) + d(a2, b1))


def _make_bmm(ca, cb):
    @jax.custom_vjp
    def f(a, b):
        return _bdg(a, b, ca, cb)

    def fwd(a, b):
        return _bdg(a, b, ca, cb), (a, b)

    def bwd(res, g):
        a, b = res
        if (ca, cb) == (2, 1):
            return _bdg(g, b, 2, 2), _bdg(a, g, 1, 1)
        if (ca, cb) == (2, 2):
            return _bdg(g, b, 2, 1), _bdg(g, a, 1, 1)
        return _bdg(b, g, 2, 2), _bdg(a, g, 2, 1)

    f.defvjp(fwd, bwd)
    return f


bmm_nn = _make_bmm(2, 1)
bmm_nt = _make_bmm(2, 2)
bmm_tn = _make_bmm(1, 1)


def _iota3(shape, dim):
    return lax.broadcasted_iota(jnp.int32, shape, dim)


@jax.custom_vjp
def _tri_inv(aw):
    b, n, _ = aw.shape
    shp = (b, n, 2 * n)
    left = _iota3(shp, 2) < n
    x = jnp.where(_iota3(shp, 1) == _iota3(shp, 2), 1.0, 0.0) - aw
    zero = jnp.zeros((b, n, 4 * n), BF16)
    k = 1
    while k < n:
        x1, x2 = _split2(x)
        lhs = jnp.concatenate([x1, x2], axis=1)
        rhs = jnp.concatenate([zero, jnp.concatenate([x1, x2], axis=2)], axis=1)
        r = lax.dot_general(lhs, rhs, (((2,), (1,)), ((0,), (0,))), preferred_element_type=F32)
        mx = (r[:, :n, :2 * n] + r[:, :n, 2 * n:]) + (r[:, n:, :2 * n] + r[:, n:, 2 * n:])
        x = jnp.where(left, x, 0.0) + mx
        k *= 2
    return jnp.where(left, x, 0.0)


def _tri_inv_fwd(aw):
    t = _tri_inv(aw)
    return t, t


def _tri_inv_bwd(t, dt):
    n = t.shape[1]
    x = _bdg3(t, dt, 1, 1)[:, :n]
    tp = jnp.concatenate([jnp.zeros_like(t), t], axis=1)
    return (-_bdg3(x, tp, 2, 2),)


_tri_inv.defvjp(_tri_inv_fwd, _tri_inv_bwd)


def _rows(shape):
    return lax.broadcasted_iota(jnp.int32, shape, 0)


def _lanes(shape):
    return lax.broadcasted_iota(jnp.int32, shape, 1)


def _col(x, j):
    return jnp.sum(jnp.where(_lanes(x.shape) == j, x, 0.0), axis=1, keepdims=True)


def _put_col(col, j, width):
    shape = (col.shape[0], width)
    return jnp.where(_lanes(shape) == j, jnp.broadcast_to(col, shape), 0.0)


VMEM_BYTES_V7X = 64 << 20


def _params(*sem, vmem_bytes=None):
    return pltpu.CompilerParams(dimension_semantics=sem, vmem_limit_bytes=vmem_bytes)


def matmul(a, b, *, ta=False, tb=False, tm, tn, tk, out_dtype=F32, col_groups=1, dep=None, name):
    ba, bb = a.shape[:-2], b.shape[:-2]
    batch = ba if len(ba) >= len(bb) else bb
    assert ba in ((), batch) and bb in ((), batch)
    nb = len(batch)
    (m, k) = (a.shape[-1], a.shape[-2]) if ta else (a.shape[-2], a.shape[-1])
    (k2, n) = (b.shape[-1], b.shape[-2]) if tb else (b.shape[-2], b.shape[-1])
    tm, tn, tk = min(tm, m), min(tn, n), min(tk, k)
    assert k == k2 and m % tm == 0 and n % tn == 0 and k % tk == 0, (a.shape, b.shape, tm, tn, tk)
    nk = k // tk
    grid = batch + (m // tm, n // tn, nk)

    def a_map(*g):
        i, kk = g[nb], g[nb + 2]
        return (g[:nb] if ba else ()) + ((kk, i) if ta else (i, kk))

    def b_map(*g):
        j, kk = g[nb + 1], g[nb + 2]
        return (g[:nb] if bb else ()) + ((j, kk) if tb else (kk, j))

    per_group = n // tn // col_groups
    assert col_groups == 1 or (nb == 0 and per_group * col_groups * tn == n)

    def o_map(*g):
        if col_groups > 1:
            return (g[1] // per_group, g[0], g[1] % per_group)
        return g[:nb] + (g[nb], g[nb + 1])

    a_blk = (None,) * len(ba) + ((tk, tm) if ta else (tm, tk))
    b_blk = (None,) * len(bb) + ((tn, tk) if tb else (tk, tn))
    o_blk = (None,) * (nb + (col_groups > 1)) + (tm, tn)
    o_shape = (col_groups, m, n // col_groups) if col_groups > 1 else batch + (m, n)

    dep_specs, dep_args = _dep_operand(dep)

    def body(a_ref, b_ref, *rest):
        o_ref, acc_ref = rest[len(dep_args):]
        kk = pl.program_id(nb + 2)

        @pl.when(kk == 0)
        def _():
            acc_ref[...] = jnp.zeros_like(acc_ref)

        acc_ref[...] += _dg(a_ref[...], b_ref[...], 0 if ta else 1, 1 if tb else 0)

        @pl.when(kk == nk - 1)
        def _():
            o_ref[...] = acc_ref[...].astype(o_ref.dtype)

    return pl.pallas_call(
        body, name=name, grid=grid,
        in_specs=[pl.BlockSpec(a_blk, a_map), pl.BlockSpec(b_blk, b_map)] + dep_specs,
        out_specs=pl.BlockSpec(o_blk, o_map),
        out_shape=jax.ShapeDtypeStruct(o_shape, out_dtype),
        scratch_shapes=[pltpu.VMEM((tm, tn), F32)],
        compiler_params=_params(*(("parallel",) * (nb + 2) + ("arbitrary",))),
    )(a, b, *dep_args)


def norm_fwd(x, g, *, tb, name):
    s, d = x.shape

    def body(x_ref, g_ref, h_ref):
        xv = x_ref[...]
        r = lax.rsqrt(jnp.mean(xv * xv, axis=-1, keepdims=True) + EPS)
        h_ref[...] = (xv * r * g_ref[...]).astype(BF16)

    return pl.pallas_call(
        body, name=name, grid=(s // tb,),
        in_specs=[pl.BlockSpec((tb, d), lambda i: (i, 0)), pl.BlockSpec((1, d), lambda i: (0, 0))],
        out_specs=pl.BlockSpec((tb, d), lambda i: (i, 0)),
        out_shape=jax.ShapeDtypeStruct((s, d), BF16),
        compiler_params=_params("parallel"),
    )(x, g)


def _rms_bwd(xv, gv, dh):
    r = lax.rsqrt(jnp.mean(xv * xv, axis=-1, keepdims=True) + EPS)
    xh = xv * r
    dxh = dh * gv
    dx = r * (dxh - xh * jnp.mean(dxh * xh, axis=-1, keepdims=True))
    return dx, jnp.sum(dh * xh, axis=0, keepdims=True)


def norm_bwd(x, g, dh, dres, *, tb, dep=None, name):
    s, d = x.shape
    dep_specs, dep_args = _dep_operand(dep)

    def body(x_ref, g_ref, dh_ref, dres_ref, *rest):
        dx_ref, dg_ref = rest[len(dep_args):]
        dx, dg = _rms_bwd(x_ref[...], g_ref[...], dh_ref[...])
        dx_ref[...] = dres_ref[...] + dx

        @pl.when(pl.program_id(0) == 0)
        def _():
            dg_ref[...] = jnp.zeros_like(dg_ref)

        dg_ref[...] += dg

    row = pl.BlockSpec((tb, d), lambda i: (i, 0))
    one = pl.BlockSpec((1, d), lambda i: (0, 0))
    return pl.pallas_call(
        body, name=name, grid=(s // tb,),
        in_specs=[row, one, row, row] + dep_specs, out_specs=[row, one],
        out_shape=[jax.ShapeDtypeStruct((s, d), F32), jax.ShapeDtypeStruct((1, d), F32)],
        compiler_params=_params("arbitrary"),
    )(x, g, dh, dres, *dep_args)


def _dep_operand(dep):
    return ([], []) if dep is None else ([_ANY], [dep])


def ffn_fwd(x, g, wg, wu, wd, layer, *, tb, dep=None, name):
    s, d = x.shape
    fc = wd.shape[-2]
    dep_specs, dep_args = _dep_operand(dep)

    def body(x_ref, g_ref, wg_ref, wu_ref, wd_ref, *rest):
        o_ref, h_scr, acc_scr = rest[len(dep_args):]
        j = pl.program_id(1)

        @pl.when(j == 0)
        def _():
            xv = x_ref[...]
            r = lax.rsqrt(jnp.mean(xv * xv, axis=-1, keepdims=True) + EPS)
            h_scr[...] = (xv * r * g_ref[...]).astype(BF16)
            acc_scr[...] = jnp.zeros_like(acc_scr)

        h = h_scr[...]
        a = _dg(h, wg_ref[...], 1, 0)
        b = _dg(h, wu_ref[...], 1, 0)
        acc_scr[...] += _dg(_silu(a) * b, wd_ref[...], 1, 0)

        @pl.when(j == N_SHARD - 1)
        def _():
            o_ref[...] = x_ref[...] + 0.5 * acc_scr[...]

    row = pl.BlockSpec((tb, d), lambda i, j: (i, 0))
    lg, lu, ld = layer
    return pl.pallas_call(
        body, name=name, grid=(s // tb, N_SHARD),
        in_specs=[row, pl.BlockSpec((1, d), lambda i, j: (0, 0)),
                  pl.BlockSpec((None, None, d, fc), lambda i, j: (j, lg, 0, 0)),
                  pl.BlockSpec((None, None, d, fc), lambda i, j: (j, lu, 0, 0)),
                  pl.BlockSpec((None, None, fc, d), lambda i, j: (j, ld, 0, 0))] + dep_specs,
        out_specs=row,
        out_shape=jax.ShapeDtypeStruct((s, d), F32),
        scratch_shapes=[pltpu.VMEM((tb, d), BF16), pltpu.VMEM((tb, d), F32)],
        compiler_params=_params("parallel", "arbitrary"),
    )(x, g, wg, wu, wd, *dep_args)


def ffn_bwd(x, dxo, g, wg, wu, wd, layer, *, tb, dep=None, name):
    s, d = x.shape
    fc = wd.shape[-2]
    dep_specs, dep_args = _dep_operand(dep)

    def body(x_ref, dxo_ref, g_ref, wg_ref, wu_ref, wd_ref, *rest):
        dx_ref, dg_ref, dab_ref, hid_ref, h_ref, dy_ref, dh_scr = rest[len(dep_args):]
        i, j = pl.program_id(0), pl.program_id(1)

        @pl.when(j == 0)
        def _():
            xv = x_ref[...]
            r = lax.rsqrt(jnp.mean(xv * xv, axis=-1, keepdims=True) + EPS)
            h_ref[...] = (xv * r * g_ref[...]).astype(BF16)
            dy_ref[...] = (0.5 * dxo_ref[...]).astype(BF16)
            dh_scr[...] = jnp.zeros_like(dh_scr)

        h = h_ref[...]
        a = _dg(h, wg_ref[...], 1, 0)
        b = _dg(h, wu_ref[...], 1, 0)
        sg = jax.nn.sigmoid(a)
        sl = a * sg
        dhid = _dg(dy_ref[...], wd_ref[...], 1, 1)
        da = (dhid * b * (sg * (1.0 + a * (1.0 - sg)))).astype(BF16)
        db = (dhid * sl).astype(BF16)
        dab_ref[0] = da
        dab_ref[1] = db
        hid_ref[...] = (sl * b).astype(BF16)
        dh_scr[...] += _dg(da, wg_ref[...], 1, 1) + _dg(db, wu_ref[...], 1, 1)

        @pl.when(jnp.logical_and(i == 0, j == 0))
        def _():
            dg_ref[...] = jnp.zeros_like(dg_ref)

        @pl.when(j == N_SHARD - 1)
        def _():
            dx, dg = _rms_bwd(x_ref[...], g_ref[...], dh_scr[...])
            dx_ref[...] = dxo_ref[...] + dx
            dg_ref[...] += dg

    row = pl.BlockSpec((tb, d), lambda i, j: (i, 0))
    one = pl.BlockSpec((1, d), lambda i, j: (0, 0))
    lg, lu, ld = layer
    return pl.pallas_call(
        body, name=name, grid=(s // tb, N_SHARD),
        in_specs=[row, row, one,
                  pl.BlockSpec((None, None, d, fc), lambda i, j: (j, lg, 0, 0)),
                  pl.BlockSpec((None, None, d, fc), lambda i, j: (j, lu, 0, 0)),
                  pl.BlockSpec((None, None, fc, d), lambda i, j: (j, ld, 0, 0))] + dep_specs,
        out_specs=[row, one,
                   pl.BlockSpec((None, 2, tb, fc), lambda i, j: (j, 0, i, 0)),
                   pl.BlockSpec((None, tb, fc), lambda i, j: (j, i, 0)),
                   row, row],
        out_shape=[jax.ShapeDtypeStruct((s, d), F32), jax.ShapeDtypeStruct((1, d), F32),
                   jax.ShapeDtypeStruct((N_SHARD, 2, s, fc), BF16),
                   jax.ShapeDtypeStruct((N_SHARD, s, fc), BF16),
                   jax.ShapeDtypeStruct((s, d), BF16), jax.ShapeDtypeStruct((s, d), BF16)],
        scratch_shapes=[pltpu.VMEM((tb, d), F32)],
        compiler_params=_params("arbitrary", "arbitrary",
                                vmem_bytes=min(VMEM_BYTES_V7X - (8 << 20),
                                               12 * d * fc + 44 * tb * d + 40 * tb * fc + (4 << 20))),
    )(x, dxo, g, wg, wu, wd, *dep_args)


def _pblk(tb, width, col0, rev_n=None):
    assert col0 % width == 0
    cb = col0 // width
    if rev_n is None:
        return pl.BlockSpec((tb, width), lambda i: (i, cb))
    return pl.BlockSpec((tb, width), lambda i: (rev_n - 1 - i, cb))


def _full(shape):
    return pl.BlockSpec(shape, lambda i: (0,) * len(shape))


def conv_fwd(src, col0, width, w, bias, *, tb, name):
    s = src.shape[0]
    cb = col0 // width

    def body(x_ref, p_ref, w_ref, b_ref, y_ref):
        i = pl.program_id(0)
        prev = jnp.where(i > 0, p_ref[...], 0.0)
        xe = jnp.concatenate([prev, x_ref[...]], axis=0)
        wv = w_ref[...]
        acc = b_ref[...] + wv[3:4] * xe[8:]
        for k in range(3):
            acc = acc + wv[k:k + 1] * pltpu.roll(xe, 3 - k, 0)[8:]
        y_ref[...] = acc

    return pl.pallas_call(
        body, name=name, grid=(s // tb,),
        in_specs=[_pblk(tb, width, col0),
                  pl.BlockSpec((8, width), lambda i: (jnp.maximum(i * (tb // 8) - 1, 0), cb)),
                  _full((4, width)), _full((1, width))],
        out_specs=pl.BlockSpec((tb, width), lambda i: (i, 0)),
        out_shape=jax.ShapeDtypeStruct((s, width), F32),
        compiler_params=_params("parallel"),
    )(src, src, w, bias)


def conv_bwd(src, col0, width, dpre, w, *, tb, name):
    s = src.shape[0]
    cb = col0 // width
    nt = s // tb

    def body(x_ref, p_ref, d_ref, n_ref, w_ref, dx_ref, dwb_ref):
        i = pl.program_id(0)
        d = d_ref[...]
        nxt = jnp.where(i < nt - 1, n_ref[...], 0.0)
        de = jnp.concatenate([d, nxt], axis=0)
        wv = w_ref[...]
        dx = wv[3:4] * d
        for k in range(3):
            dx = dx + wv[k:k + 1] * pltpu.roll(de, tb + 8 - (3 - k), 0)[:tb]
        dx_ref[...] = dx.astype(BF16)

        @pl.when(i == 0)
        def _():
            dwb_ref[...] = jnp.zeros_like(dwb_ref)

        prev = jnp.where(i > 0, p_ref[...], 0.0)
        xe = jnp.concatenate([prev, x_ref[...]], axis=0)
        dwb_ref[3:4, :] += jnp.sum(d * xe[8:], axis=0, keepdims=True)
        for k in range(3):
            dwb_ref[k:k + 1, :] += jnp.sum(d * pltpu.roll(xe, 3 - k, 0)[8:], axis=0, keepdims=True)
        dwb_ref[4:5, :] += jnp.sum(d, axis=0, keepdims=True)

    return pl.pallas_call(
        body, name=name, grid=(nt,),
        in_specs=[_pblk(tb, width, col0),
                  pl.BlockSpec((8, width), lambda i: (jnp.maximum(i * (tb // 8) - 1, 0), cb)),
                  pl.BlockSpec((tb, width), lambda i: (i, 0)),
                  pl.BlockSpec((8, width), lambda i: (jnp.minimum((i + 1) * (tb // 8), s // 8 - 1), 0)),
                  _full((4, width))],
        out_specs=[pl.BlockSpec((tb, width), lambda i: (i, 0)), _full((8, width))],
        out_shape=[jax.ShapeDtypeStruct((s, width), BF16), jax.ShapeDtypeStruct((8, width), F32)],
        compiler_params=_params("arbitrary"),
    )(src, src, dpre, dpre, w)


SGU_BLOCK = 128


def _sgu_pre(u, v, lg, lb):
    ug = _gelu(u)
    vg = _gelu(v)
    mu = jnp.mean(vg, axis=-1, keepdims=True)
    var = jnp.mean(jnp.square(vg - mu), axis=-1, keepdims=True)
    return ug, (vg - mu) * lax.rsqrt(var + EPS) * lg + lb


def _sgu_mask():
    shp = (SGU_BLOCK, SGU_BLOCK)
    return (_rows(shp) // CHUNK) >= (_lanes(shp) // CHUNK)


def sgu_fwd(proj, lg, lb, ws, bt, *, tb, name):
    s = proj.shape[0]
    w = BRANCH_W

    def body(u_ref, v_ref, lg_ref, lb_ref, w_ref, b_ref, y_ref):
        ug, vn = _sgu_pre(u_ref[...], v_ref[...], lg_ref[...], lb_ref[...])
        mask = _sgu_mask()
        for g in range(4):
            wm = jnp.where(mask, w_ref[g], 0.0)
            bcol = _col(b_ref[...], g)
            cs = slice(128 * g, 128 * g + 128)
            for n in range(tb // SGU_BLOCK):
                rs = slice(SGU_BLOCK * n, SGU_BLOCK * (n + 1))
                mixed = _dg(wm, vn[rs, cs], 1, 0) + bcol
                y_ref[rs, cs] = (ug[rs, cs] * mixed).astype(BF16)

    return pl.pallas_call(
        body, name=name, grid=(s // tb,),
        in_specs=[_pblk(tb, w, C_AU), _pblk(tb, w, C_AV), _full((1, w)), _full((1, w)),
                  _full((4, 128, 128)), _full((128, 128))],
        out_specs=pl.BlockSpec((tb, w), lambda i: (i, 0)),
        out_shape=jax.ShapeDtypeStruct((s, w), BF16),
        compiler_params=_params("parallel"),
    )(proj, proj, lg, lb, ws, bt)


def sgu_bwd(proj, dy, lg, lb, ws, bt, *, tb, dep=None, name):
    s = proj.shape[0]
    w = BRANCH_W
    dep_specs, dep_args = _dep_operand(dep)

    def body(u_ref, v_ref, dy_ref, lg_ref, lb_ref, w_ref, b_ref, *rest):
        du_ref, dv_ref, dlg_ref, dlb_ref, dw_ref, db_ref, dug_scr, dvn_scr = rest[len(dep_args):]

        @pl.when(pl.program_id(0) == 0)
        def _():
            dlg_ref[...] = jnp.zeros_like(dlg_ref)
            dlb_ref[...] = jnp.zeros_like(dlb_ref)
            dw_ref[...] = jnp.zeros_like(dw_ref)
            db_ref[...] = jnp.zeros_like(db_ref)

        (ug, vn), vf = jax.vjp(_sgu_pre, u_ref[...], v_ref[...], lg_ref[...], lb_ref[...])
        dyv = dy_ref[...]
        mask = _sgu_mask()
        for g in range(4):
            wm = jnp.where(mask, w_ref[g], 0.0)
            bcol = _col(b_ref[...], g)
            cs = slice(128 * g, 128 * g + 128)
            dwg = jnp.zeros((SGU_BLOCK, SGU_BLOCK), F32)
            dbc = jnp.zeros((SGU_BLOCK, 1), F32)
            for n in range(tb // SGU_BLOCK):
                rs = slice(SGU_BLOCK * n, SGU_BLOCK * (n + 1))
                blk = vn[rs, cs]
                mixed = _dg(wm, blk, 1, 0) + bcol
                dmix = dyv[rs, cs] * ug[rs, cs]
                dug_scr[rs, cs] = dyv[rs, cs] * mixed
                dvn_scr[rs, cs] = _dg(wm, dmix, 0, 0)
                dwg = dwg + _dg(dmix, blk, 1, 1)
                dbc = dbc + jnp.sum(dmix, axis=1, keepdims=True)
            dw_ref[g] += jnp.where(mask, dwg, 0.0)
            db_ref[...] += _put_col(dbc, g, 128)
        du, dv, dlg, dlb = vf((dug_scr[...], dvn_scr[...]))
        du_ref[...] = du.astype(BF16)
        dv_ref[...] = dv.astype(BF16)
        dlg_ref[...] += dlg
        dlb_ref[...] += dlb

    row = pl.BlockSpec((tb, w), lambda i: (i, 0))
    return pl.pallas_call(
        body, name=name, grid=(s // tb,),
        in_specs=[_pblk(tb, w, C_AU), _pblk(tb, w, C_AV), row, _full((1, w)), _full((1, w)),
                  _full((4, 128, 128)), _full((128, 128))] + dep_specs,
        out_specs=[row, row, _full((1, w)), _full((1, w)), _full((4, 128, 128)), _full((128, 128))],
        out_shape=[jax.ShapeDtypeStruct((s, w), BF16), jax.ShapeDtypeStruct((s, w), BF16),
                   jax.ShapeDtypeStruct((1, w), F32), jax.ShapeDtypeStruct((1, w), F32),
                   jax.ShapeDtypeStruct((4, 128, 128), F32), jax.ShapeDtypeStruct((128, 128), F32)],
        scratch_shapes=[pltpu.VMEM((tb, w), F32), pltpu.VMEM((tb, w), F32)],
        compiler_params=_params("arbitrary"),
    )(proj, proj, dy, lg, lb, ws, bt, *dep_args)


POOL_HALO = 16


def _pooled(xe, t0, tb):
    s2 = xe + pltpu.roll(xe, 1, 0)
    s4 = s2 + pltpu.roll(s2, 2, 0)
    s8 = s4 + pltpu.roll(s4, 4, 0)
    s16 = s8 + pltpu.roll(s8, 8, 0)
    t = (t0 + _rows((tb, 128))).astype(F32)
    outs = []
    for g, sw in enumerate((s2, s4, s8, s16)):
        cs = slice(128 * g, 128 * g + 128)
        cnt = jnp.minimum(t + 1.0, float(2 ** (g + 1)))
        outs.append(sw[POOL_HALO:, cs] / cnt - xe[POOL_HALO:, cs])
    return outs


def pool_fwd(proj, wp, scale, *, tb, name):
    s = proj.shape[0]
    w = BRANCH_W

    def body(x_ref, p_ref, w_ref, sc_ref, y_ref):
        i = pl.program_id(0)
        prev = jnp.where(i > 0, p_ref[...], 0.0)
        xe = jnp.concatenate([prev, x_ref[...]], axis=0)
        pooled = _pooled(xe, i * tb, tb)
        sc = sc_ref[...]
        for g in range(4):
            cs = slice(128 * g, 128 * g + 128)
            y_ref[:, cs] = (_dg(pooled[g], w_ref[g], 1, 0) * sc[:, cs]).astype(BF16)

    cb = C_DX // w
    return pl.pallas_call(
        body, name=name, grid=(s // tb,),
        in_specs=[_pblk(tb, w, C_DX),
                  pl.BlockSpec((POOL_HALO, w), lambda i: (jnp.maximum(i * (tb // POOL_HALO) - 1, 0), cb)),
                  _full((4, 128, 128)), _full((1, w))],
        out_specs=pl.BlockSpec((tb, w), lambda i: (i, 0)),
        out_shape=jax.ShapeDtypeStruct((s, w), BF16),
        compiler_params=_params("parallel"),
    )(proj, proj, wp, scale)


def pool_bwd(proj, dy, wp, scale, *, tb, name):
    s = proj.shape[0]
    w = BRANCH_W
    nt = s // tb
    te = tb + POOL_HALO

    def body(x_ref, p_ref, dy_ref, n_ref, w_ref, sc_ref, dx_ref, dw_ref, dsc_ref):
        i = pl.program_id(0)

        @pl.when(i == 0)
        def _():
            dw_ref[...] = jnp.zeros_like(dw_ref)
            dsc_ref[...] = jnp.zeros_like(dsc_ref)

        prev = jnp.where(i > 0, p_ref[...], 0.0)
        xe = jnp.concatenate([prev, x_ref[...]], axis=0)
        pooled = _pooled(xe, i * tb, tb)
        dyv = dy_ref[...]
        nxt = jnp.where(i < nt - 1, n_ref[...], 0.0)
        dye = jnp.concatenate([dyv, nxt], axis=0)
        sc = sc_ref[...]
        t = (i * tb + _rows((te, 128))).astype(F32)
        for g in range(4):
            cs = slice(128 * g, 128 * g + 128)
            win = 2 ** (g + 1)
            dpm = dye[:, cs] * sc[:, cs]
            dpool = _dg(dpm, w_ref[g], 1, 1)
            q = dpool / jnp.minimum(t + 1.0, float(win))
            r, sh = q, 1
            while sh < win:
                r = r + pltpu.roll(r, te - sh, 0)
                sh *= 2
            dx_ref[:, cs] = (r[:tb] - dpool[:tb]).astype(BF16)
            dw_ref[g] += _dg(pooled[g], dpm[:tb], 0, 0)
            dsc_ref[:, cs] += jnp.sum(dyv[:, cs] * _dg(pooled[g], w_ref[g], 1, 0), axis=0, keepdims=True)

    cb = C_DX // w
    row = pl.BlockSpec((tb, w), lambda i: (i, 0))
    return pl.pallas_call(
        body, name=name, grid=(nt,),
        in_specs=[_pblk(tb, w, C_DX),
                  pl.BlockSpec((POOL_HALO, w), lambda i: (jnp.maximum(i * (tb // POOL_HALO) - 1, 0), cb)),
                  row,
                  pl.BlockSpec((POOL_HALO, w),
                               lambda i: (jnp.minimum((i + 1) * (tb // POOL_HALO), s // POOL_HALO - 1), 0)),
                  _full((4, 128, 128)), _full((1, w))],
        out_specs=[row, _full((4, 128, 128)), _full((1, w))],
        out_shape=[jax.ShapeDtypeStruct((s, w), BF16), jax.ShapeDtypeStruct((4, 128, 128), F32),
                   jax.ShapeDtypeStruct((1, w), F32)],
        compiler_params=_params("arbitrary"),
    )(proj, proj, dy, dy, wp, scale)


def _neg_expm1(z):
    p = 1.0 + z * (1.0 / 9.0)
    for n in (8.0, 7.0, 6.0, 5.0, 4.0, 3.0, 2.0):
        p = 1.0 + z * (1.0 / n) * p
    return jnp.where(z > -0.5, -z * p, 1.0 - jnp.exp(z))


def _lru_gates(xc, wa, wx, ba, bx, lam):
    r = jax.nn.sigmoid(mm_nn(xc, wa) + ba)
    i = jax.nn.sigmoid(mm_nn(xc, wx) + bx)
    log_a = -LRU_C * r * _softplus(-lam)
    a = jnp.exp(log_a)
    return a, jnp.sqrt(_neg_expm1(2.0 * log_a)) * (i * xc)


def _scan_down(a, b):
    n = a.shape[0]
    row = _rows(a.shape)
    sh = 1
    while sh < n:
        keep = row >= sh
        a_sh = jnp.where(keep, pltpu.roll(a, sh, 0), 1.0)
        b_sh = jnp.where(keep, pltpu.roll(b, sh, 0), 0.0)
        b = a * b_sh + b
        a = a * a_sh
        sh *= 2
    return a, b


def _scan_up(a, b):
    n = a.shape[0]
    row = _rows(a.shape)
    sh = 1
    while sh < n:
        keep = row < n - sh
        a_sh = jnp.where(keep, pltpu.roll(a, n - sh, 0), 1.0)
        b_sh = jnp.where(keep, pltpu.roll(b, n - sh, 0), 0.0)
        b = a * b_sh + b
        a = a * a_sh
        sh *= 2
    return a, b


def lru_fwd(xc, proj, wa, wx, ba, bx, lam, *, tb, name):
    s = xc.shape[0]
    w = BRANCH_W

    def body(xc_ref, gate_ref, wa_ref, wx_ref, ba_ref, bx_ref, lam_ref, y_ref, h_ref, carry):
        @pl.when(pl.program_id(0) == 0)
        def _():
            carry[...] = jnp.zeros_like(carry)

        a, b = _lru_gates(xc_ref[...], wa_ref[...], wx_ref[...], ba_ref[...], bx_ref[...], lam_ref[...])
        pa, hb = _scan_down(a, b)
        h = pa * carry[...] + hb
        h_ref[...] = h
        carry[...] = h_ref[tb - 1:tb, :]
        y_ref[...] = (h * _gelu(gate_ref[...])).astype(BF16)

    row = pl.BlockSpec((tb, w), lambda i: (i, 0))
    return pl.pallas_call(
        body, name=name, grid=(s // tb,),
        in_specs=[row, _pblk(tb, w, C_BG), _full((w, w)), _full((w, w)),
                  _full((1, w)), _full((1, w)), _full((1, w))],
        out_specs=[row, row],
        out_shape=[jax.ShapeDtypeStruct((s, w), BF16), jax.ShapeDtypeStruct((s, w), F32)],
        scratch_shapes=[pltpu.VMEM((1, w), F32)],
        compiler_params=_params("arbitrary"),
    )(xc, proj, wa, wx, ba, bx, lam)


def lru_bwd(xc, proj, h, dy, wa, wx, ba, bx, lam, *, tb, name):
    s = xc.shape[0]
    w = BRANCH_W
    nt = s // tb

    def body(xc_ref, gate_ref, h_ref, hp_ref, dy_ref, wa_ref, wx_ref, ba_ref, bx_ref, lam_ref,
             dxc_ref, dgate_ref, dwa_ref, dwx_ref, dba_ref, dbx_ref, dlam_ref, carry):
        i = pl.program_id(0)

        @pl.when(i == 0)
        def _():
            carry[...] = jnp.zeros_like(carry)
            for r in (dwa_ref, dwx_ref, dba_ref, dbx_ref, dlam_ref):
                r[...] = jnp.zeros_like(r)

        (a, b), vf = jax.vjp(_lru_gates, xc_ref[...], wa_ref[...], wx_ref[...],
                             ba_ref[...], bx_ref[...], lam_ref[...])
        gate = gate_ref[...]
        dyv = dy_ref[...]
        hv = h_ref[...]
        row = _rows((tb, w))
        a_next = jnp.where(row < tb - 1, pltpu.roll(a, tb - 1, 0), 1.0)
        pa, xb = _scan_up(a_next, dyv * _gelu(gate))
        dh = xb + pa * carry[...]
        carry[...] = a[0:1, :] * dh[0:1, :]
        h_first = jnp.where(i < nt - 1, hp_ref[7:8, :], 0.0)
        h_prev = jnp.where(row >= 1, pltpu.roll(hv, 1, 0), h_first)
        dxc, dwa, dwx, dba, dbx, dlam = vf((dh * h_prev, dh))
        dxc_ref[...] = dxc
        dgate_ref[...] = (dyv * hv * _gelu_grad(gate)).astype(BF16)
        dwa_ref[...] += dwa
        dwx_ref[...] += dwx
        dba_ref[...] += dba
        dbx_ref[...] += dbx
        dlam_ref[...] += dlam

    row_spec = pl.BlockSpec((tb, w), lambda i: (nt - 1 - i, 0))
    hp_spec = pl.BlockSpec((8, w), lambda i: (jnp.maximum((nt - 1 - i) * (tb // 8) - 1, 0), 0))
    return pl.pallas_call(
        body, name=name, grid=(nt,),
        in_specs=[row_spec, _pblk(tb, w, C_BG, rev_n=nt), row_spec, hp_spec, row_spec,
                  _full((w, w)), _full((w, w)), _full((1, w)), _full((1, w)), _full((1, w))],
        out_specs=[row_spec, row_spec, _full((w, w)), _full((w, w)),
                   _full((1, w)), _full((1, w)), _full((1, w))],
        out_shape=[jax.ShapeDtypeStruct((s, w), F32), jax.ShapeDtypeStruct((s, w), BF16),
                   jax.ShapeDtypeStruct((w, w), F32), jax.ShapeDtypeStruct((w, w), F32),
                   jax.ShapeDtypeStruct((1, w), F32), jax.ShapeDtypeStruct((1, w), F32),
                   jax.ShapeDtypeStruct((1, w), F32)],
        scratch_shapes=[pltpu.VMEM((1, w), F32)],
        compiler_params=_params("arbitrary"),
    )(xc, proj, h, h, dy, wa, wx, ba, bx, lam)


def _gdn_g(ba, alog, dt):
    return -jnp.exp(alog) * _softplus(ba + dt)


def _gdn_local(qp, kp, vp, bpre, gc):
    b, c, dk = qp.shape
    q = _silu(qp)
    k = _silu(kp)
    v = _silu(vp)
    q = q * lax.rsqrt(jnp.sum(q * q, axis=-1, keepdims=True) + EPS) * (dk ** -0.5)
    k = k * lax.rsqrt(jnp.sum(k * k, axis=-1, keepdims=True) + EPS)
    beta = jax.nn.sigmoid(bpre)
    sq = (b, c, c)
    ii, jj = _iota3(sq, 1), _iota3(sq, 2)
    gr = jnp.sum(gc * (ii == jj).astype(F32), axis=1, keepdims=True)
    decay = jnp.exp(jnp.where(ii >= jj, gc - gr, -jnp.inf))
    wd = (b, c, 2 * c)
    wi, wj = _iota3(wd, 1), _iota3(wd, 2) - c
    gr_w = jnp.sum(gc * (wi == wj).astype(F32), axis=1, keepdims=True)
    decay_w = jnp.exp(jnp.where(jnp.logical_and(wj >= 0, wi > wj), gc - gr_w, -jnp.inf))
    kb = k * beta
    pad = jnp.zeros((b, c, dk), F32)
    t = _tri_inv(bmm_nt(kb, jnp.concatenate([pad, k], axis=1)) * decay_w)
    u = bmm_nn(t, jnp.concatenate([v * beta, pad], axis=1))
    wk = bmm_nn(t, jnp.concatenate([kb * jnp.exp(gc), pad], axis=1))
    attn = bmm_nt(q, k) * decay
    g_last = jnp.sum(jnp.where(_iota3((b, c, 1), 1) == c - 1, gc, 0.0), axis=1, keepdims=True)
    return u, wk, attn, q * jnp.exp(gc), k * jnp.exp(g_last - gc), jnp.exp(g_last)


def _gdn_recur(s, u, wk, attn, qg, kd, eg, zz, ng):
    v_new = u - bmm_nn(wk, s)
    o = bmm_nn(qg, s) + bmm_nn(attn, v_new)
    s_out = s * eg + bmm_tn(kd, v_new)
    o = o * lax.rsqrt(jnp.mean(o * o, axis=-1, keepdims=True) + EPS) * ng
    return o * _silu(zz), s_out


def _gdn_inputs(qkv_ref, ba, gcums, cps):
    c, hd = CHUNK, GDN_DK
    qs, ks, vs, bs, gs = [], [], [], [], []
    for ci in range(cps):
        rs = slice(c * ci, c * (ci + 1))
        for h in range(GDN_HEADS):
            qs.append(qkv_ref[rs, hd * h:hd * (h + 1)])
            ks.append(qkv_ref[rs, 512 + hd * h:512 + hd * (h + 1)])
            vs.append(qkv_ref[rs, 1024 + hd * h:1024 + hd * (h + 1)])
            bs.append(_col(ba[rs], h))
            gs.append(_col(gcums[ci], 4 + h))
    return tuple(jnp.stack(t) for t in (qs, ks, vs, bs, gs))


def _chunk_cumsum(g, transpose=False):
    c = CHUNK
    tri = (_rows((c, c)) >= _lanes((c, c))).astype(F32)
    return _dg3(tri, g, 0 if transpose else 1, 0)


def gdn_fwd(qkv, proj, alog, dt, ng, *, cps, name):
    s = qkv.shape[0]
    c = CHUNK
    nc = s // c
    hd = GDN_DK
    tb = c * cps

    def body(qkv_ref, z_ref, ba_ref, al_ref, dt_ref, ng_ref, y_ref, sall_ref, s_scr):
        @pl.when(pl.program_id(0) == 0)
        def _():
            s_scr[...] = jnp.zeros_like(s_scr)

        nh = GDN_HEADS
        ba = ba_ref[...]
        g = _gdn_g(ba, al_ref[...], dt_ref[...])
        gcums = [_chunk_cumsum(g[c * ci:c * (ci + 1)]) for ci in range(cps)]
        loc = _gdn_local(*_gdn_inputs(qkv_ref, ba, gcums, cps))
        for ci in range(cps):
            rs = slice(c * ci, c * (ci + 1))
            sl = slice(nh * ci, nh * (ci + 1))
            s_in = s_scr[...]
            sall_ref[ci] = s_in
            zz = jnp.stack([z_ref[rs, hd * h:hd * (h + 1)] for h in range(nh)])
            y, s_out = _gdn_recur(s_in, *(t[sl] for t in loc), zz, ng_ref[...])
            s_scr[...] = s_out
            for h in range(nh):
                y_ref[rs, hd * h:hd * (h + 1)] = y[h].astype(BF16)

    return pl.pallas_call(
        body, name=name, grid=(nc // cps,),
        in_specs=[pl.BlockSpec((tb, 1536), lambda i: (i, 0)), _pblk(tb, 512, C_Z), _pblk(tb, 128, C_BA),
                  _full((1, 128)), _full((1, 128)), _full((1, 128))],
        out_specs=[pl.BlockSpec((tb, 512), lambda i: (i, 0)),
                   pl.BlockSpec((cps, GDN_HEADS, hd, hd), lambda i: (i, 0, 0, 0))],
        out_shape=[jax.ShapeDtypeStruct((s, 512), BF16), jax.ShapeDtypeStruct((nc, GDN_HEADS, hd, hd), F32)],
        scratch_shapes=[pltpu.VMEM((GDN_HEADS, hd, hd), F32)],
        compiler_params=_params("arbitrary"),
    )(qkv, proj, proj, alog, dt, ng)


def gdn_bwd(qkv, proj, sall, dy, alog, dt, ng, *, cps, name):
    s = qkv.shape[0]
    c = CHUNK
    nc = s // c
    hd = GDN_DK
    tb = c * cps
    nt = nc // cps

    def body(qkv_ref, z_ref, ba_ref, sall_ref, dy_ref, al_ref, dt_ref, ng_ref,
             dqkv_ref, dz_ref, dba_ref, dal_ref, ddt_ref, dng_ref, ds_scr):
        @pl.when(pl.program_id(0) == 0)
        def _():
            ds_scr[...] = jnp.zeros_like(ds_scr)
            dal_ref[...] = jnp.zeros_like(dal_ref)
            ddt_ref[...] = jnp.zeros_like(ddt_ref)
            dng_ref[...] = jnp.zeros_like(dng_ref)

        nh = GDN_HEADS
        ba = ba_ref[...]
        g, gvf = jax.vjp(_gdn_g, ba, al_ref[...], dt_ref[...])
        gcums = [_chunk_cumsum(g[c * ci:c * (ci + 1)]) for ci in range(cps)]
        loc, loc_vf = jax.vjp(_gdn_local, *_gdn_inputs(qkv_ref, ba, gcums, cps))
        d_loc = [None] * cps
        dng = jnp.zeros((1, 128), F32)
        for ci in reversed(range(cps)):
            rs = slice(c * ci, c * (ci + 1))
            sl = slice(nh * ci, nh * (ci + 1))
            zz = jnp.stack([z_ref[rs, hd * h:hd * (h + 1)] for h in range(nh)])
            dyy = jnp.stack([dy_ref[rs, hd * h:hd * (h + 1)] for h in range(nh)])
            _, rvf = jax.vjp(_gdn_recur, sall_ref[ci], *(t[sl] for t in loc), zz, ng_ref[...])
            ds, du, dwk, dattn, dqg, dkd, deg, dzz, dn = rvf((dyy, ds_scr[...]))
            ds_scr[...] = ds
            d_loc[ci] = (du, dwk, dattn, dqg, dkd, deg)
            dng = dng + dn
            for h in range(nh):
                dz_ref[rs, hd * h:hd * (h + 1)] = dzz[h].astype(BF16)
        dq, dk, dv, dbp, dgc = loc_vf(tuple(jnp.concatenate([d_loc[ci][j] for ci in range(cps)], axis=0)
                                            for j in range(6)))
        dgs, dbas = [], []
        for ci in range(cps):
            rs = slice(c * ci, c * (ci + 1))
            dgcum = jnp.zeros((c, 128), F32)
            dba = jnp.zeros((c, 128), F32)
            for h in range(nh):
                b = nh * ci + h
                dqkv_ref[rs, hd * h:hd * (h + 1)] = dq[b]
                dqkv_ref[rs, 512 + hd * h:512 + hd * (h + 1)] = dk[b]
                dqkv_ref[rs, 1024 + hd * h:1024 + hd * (h + 1)] = dv[b]
                dgcum = dgcum + _put_col(dgc[b], 4 + h, 128)
                dba = dba + _put_col(dbp[b], h, 128)
            dgs.append(_chunk_cumsum(dgcum, transpose=True))
            dbas.append(dba)
        dba2, dal, ddt = gvf(jnp.concatenate(dgs, axis=0))
        dba_ref[...] = (jnp.concatenate(dbas, axis=0) + dba2).astype(BF16)
        dal_ref[...] += dal
        ddt_ref[...] += ddt
        dng_ref[...] += dng

    def rev(width):
        return pl.BlockSpec((tb, width), lambda i: (nt - 1 - i, 0))

    return pl.pallas_call(
        body, name=name, grid=(nt,),
        in_specs=[rev(1536), _pblk(tb, 512, C_Z, rev_n=nt), _pblk(tb, 128, C_BA, rev_n=nt),
                  pl.BlockSpec((cps, GDN_HEADS, hd, hd), lambda i: (nt - 1 - i, 0, 0, 0)), rev(512),
                  _full((1, 128)), _full((1, 128)), _full((1, 128))],
        out_specs=[rev(1536), rev(512), rev(128), _full((1, 128)), _full((1, 128)), _full((1, 128))],
        out_shape=[jax.ShapeDtypeStruct((s, 1536), F32), jax.ShapeDtypeStruct((s, 512), BF16),
                   jax.ShapeDtypeStruct((s, 128), BF16), jax.ShapeDtypeStruct((1, 128), F32),
                   jax.ShapeDtypeStruct((1, 128), F32), jax.ShapeDtypeStruct((1, 128), F32)],
        scratch_shapes=[pltpu.VMEM((GDN_HEADS, hd, hd), F32)],
        compiler_params=_params("arbitrary"),
    )(qkv, proj, proj, sall, dy, alog, dt, ng)


def merge_fwd(x, ys, proj, wb, wo, *, tb, dep=None, name):
    s, d = x.shape
    dep_specs, dep_args = _dep_operand(dep)

    def body(x_ref, ya, yb, yc, yd, g0, g1, g2, g3, wb_ref, wo_ref, *rest):
        o_ref, m_ref = rest[len(dep_args):]
        m = jnp.zeros((tb, d), F32)
        for g, (y_ref, gp_ref) in enumerate(zip((ya, yb, yc, yd), (g0, g1, g2, g3))):
            m = m + jax.nn.sigmoid(gp_ref[...]) * _dg(y_ref[...], wb_ref[g], 1, 0)
        mb = m.astype(BF16)
        m_ref[...] = mb
        o_ref[...] = x_ref[...] + _dg(mb, wo_ref[...], 1, 0)

    row = pl.BlockSpec((tb, d), lambda i: (i, 0))
    yrow = pl.BlockSpec((tb, BRANCH_W), lambda i: (i, 0))
    return pl.pallas_call(
        body, name=name, grid=(s // tb,),
        in_specs=[row] + [yrow] * 4 + [_pblk(tb, d, C_GATE + d * g) for g in range(4)]
        + [_full((4, BRANCH_W, d)), _full((d, d))] + dep_specs,
        out_specs=[row, row],
        out_shape=[jax.ShapeDtypeStruct((s, d), F32), jax.ShapeDtypeStruct((s, d), BF16)],
        compiler_params=_params("parallel"),
    )(x, *ys, proj, proj, proj, proj, wb, wo, *dep_args)


def merge_bwd(dx, ys, proj, wb, wo, *, tb, dep=None, name):
    s, d = dx.shape
    dep_specs, dep_args = _dep_operand(dep)

    def body(dx_ref, ya, yb, yc, yd, g0, g1, g2, g3, wb_ref, wo_ref, *rest):
        dgp_ref, dbr_ref, da, db, dc, dd, dxb_ref = rest[len(dep_args):]
        dxb = dx_ref[...].astype(BF16)
        dxb_ref[...] = dxb
        dm = _dg(dxb, wo_ref[...], 1, 1)
        for g, (y_ref, gp_ref, dy_ref) in enumerate(zip((ya, yb, yc, yd), (g0, g1, g2, g3), (da, db, dc, dd))):
            br = _dg(y_ref[...], wb_ref[g], 1, 0)
            sg = jax.nn.sigmoid(gp_ref[...])
            dgp_ref[:, d * g:d * (g + 1)] = (dm * br * sg * (1.0 - sg)).astype(BF16)
            dbr = (dm * sg).astype(BF16)
            dbr_ref[g] = dbr
            dy_ref[...] = _dg(dbr, wb_ref[g], 1, 1)

    row = pl.BlockSpec((tb, d), lambda i: (i, 0))
    yrow = pl.BlockSpec((tb, BRANCH_W), lambda i: (i, 0))
    return pl.pallas_call(
        body, name=name, grid=(s // tb,),
        in_specs=[row] + [yrow] * 4 + [_pblk(tb, d, C_GATE + d * g) for g in range(4)]
        + [_full((4, BRANCH_W, d)), _full((d, d))] + dep_specs,
        out_specs=[pl.BlockSpec((tb, 4 * d), lambda i: (i, 0)), pl.BlockSpec((4, tb, d), lambda i: (0, i, 0)),
                   yrow, yrow, yrow, yrow, row],
        out_shape=[jax.ShapeDtypeStruct((s, 4 * d), BF16), jax.ShapeDtypeStruct((4, s, d), BF16)]
        + [jax.ShapeDtypeStruct((s, BRANCH_W), F32)] * 4 + [jax.ShapeDtypeStruct((s, d), BF16)],
        compiler_params=_params("parallel"),
    )(dx, *ys, proj, proj, proj, proj, wb, wo, *dep_args)


def loss_head(x, g, target, *, tb, name):
    s, d = x.shape

    def body(x_ref, g_ref, t_ref, dx_ref, dg_ref, loss_ref):
        @pl.when(pl.program_id(0) == 0)
        def _():
            dg_ref[...] = jnp.zeros_like(dg_ref)
            loss_ref[...] = jnp.zeros_like(loss_ref)

        xv, gv = x_ref[...], g_ref[...]
        r = lax.rsqrt(jnp.mean(xv * xv, axis=-1, keepdims=True) + EPS)
        err = xv * r * gv - t_ref[...]
        loss_ref[...] += 0.5 * jnp.sum(jnp.mean(err * err, axis=-1, keepdims=True), axis=0, keepdims=True)
        dx, dg = _rms_bwd(xv, gv, err * (1.0 / d))
        dx_ref[...] = dx
        dg_ref[...] += dg

    row = pl.BlockSpec((tb, d), lambda i: (i, 0))
    return pl.pallas_call(
        body, name=name, grid=(s // tb,),
        in_specs=[row, _full((1, d)), row],
        out_specs=[row, _full((1, d)), _full((1, 128))],
        out_shape=[jax.ShapeDtypeStruct((s, d), F32), jax.ShapeDtypeStruct((1, d), F32),
                   jax.ShapeDtypeStruct((1, 128), F32)],
        compiler_params=_params("arbitrary"),
    )(x, g, target)


def adamw(w, g, m, v, *, name):
    shape = w.shape
    lead, (r, c) = shape[:-2], shape[-2:]
    tr = r
    while tr * c * 4 > (1 << 20) and tr % 16 == 0:
        tr //= 2
    c1 = 1.0 / (1.0 - ADAM_B1 ** ADAM_STEP)
    c2 = 1.0 / (1.0 - ADAM_B2 ** ADAM_STEP)

    def body(w_ref, g_ref, m_ref, v_ref, d_ref, nm_ref, nv_ref):
        gv = g_ref[...]
        nm = ADAM_B1 * m_ref[...] + (1.0 - ADAM_B1) * gv
        nv = ADAM_B2 * v_ref[...] + (1.0 - ADAM_B2) * (gv * gv)
        nm_ref[...] = nm
        nv_ref[...] = nv
        d_ref[...] = -ADAM_LR * ((nm * c1) / (jnp.sqrt(nv * c2) + ADAM_EPS) + ADAM_WD * w_ref[...])

    blk = pl.BlockSpec((None,) * len(lead) + (tr, c), lambda *i: i + (0,))
    grid = lead + (r // tr,)
    if len(lead) == 1 and r < 8:
        tl = max(t for t in range(1, lead[0] + 1) if lead[0] % t == 0 and t * r * c * 4 <= (1 << 20))
        blk = pl.BlockSpec((tl, r, c), lambda i: (i, 0, 0))
        grid = (lead[0] // tl,)
    return tuple(pl.pallas_call(
        body, name=name, grid=grid,
        in_specs=[blk] * 4, out_specs=[blk] * 3,
        out_shape=[jax.ShapeDtypeStruct(shape, F32)] * 3,
        compiler_params=_params(*(("parallel",) * len(grid))),
    )(w, g, m, v))


_ANY = pl.BlockSpec(memory_space=pl.ANY)


def _place():
    x, y, c = lax.axis_index("x"), lax.axis_index("y"), lax.axis_index("c")
    return x, y, c, [(1 - x, y), (x, 1 - y), (1 - x, 1 - y)]


def gather_shards(bufs, *, name):
    n = len(bufs)

    def body(*refs):
        outs = refs[n:2 * n]
        ici_send, ici_recv, d2d_send, d2d_recv = refs[2 * n:]
        x, y, c, chips = _place()
        me = 2 * x + y

        def ici(a, j, slab, to):
            return pltpu.make_async_remote_copy(
                src_ref=outs[a].at[slab, c], dst_ref=outs[a].at[slab, c], send_sem=ici_send.at[a, j],
                recv_sem=ici_recv.at[a, j], device_id=to, device_id_type=MESH)

        def d2d(a, j, slab, half):
            return pltpu.make_async_remote_copy(
                src_ref=outs[a].at[slab, half], dst_ref=outs[a].at[slab, half], send_sem=d2d_send.at[a, j],
                recv_sem=d2d_recv.at[a, j], device_id=(x, y, 1 - c), device_id_type=MESH)

        sends = []
        for a in range(n):
            for j, (px, py) in enumerate(chips):
                cp = ici(a, j, me, (px, py, c))
                cp.start()
                sends.append(cp)
        for a in range(n):
            for j, (px, py) in enumerate(chips):
                ici(a, j, 2 * px + py, (px, py, c)).wait_recv()
                cp = d2d(a, j, 2 * px + py, c)
                cp.start()
                sends.append(cp)
        for a in range(n):
            for j, (px, py) in enumerate(chips):
                d2d(a, j, 2 * px + py, 1 - c).wait_recv()
        for cp in sends:
            cp.wait_send()

    return pl.pallas_call(
        body, name=name,
        in_specs=[_ANY] * n, out_specs=[_ANY] * n,
        out_shape=[jax.ShapeDtypeStruct(t.shape, t.dtype) for t in bufs],
        input_output_aliases={a: a for a in range(n)},
        scratch_shapes=[pltpu.SemaphoreType.DMA((n, 3))] * 4,
    )(*bufs)


_SEM = pl.BlockSpec(memory_space=pltpu.SEMAPHORE)
_VMEM = pl.BlockSpec(memory_space=pltpu.VMEM)
_EFFECT = pltpu.SideEffectType.DATAFLOW_SIDE_EFFECTING
_TOKEN = jax.ShapeDtypeStruct((8, 128), F32)


def _gather_copies(outs, sems_ici, sems_d2d):
    x, y, c, chips = _place()

    def ici(a, j, slab, to):
        return pltpu.make_async_remote_copy(
            src_ref=outs[a].at[slab, c], dst_ref=outs[a].at[slab, c], send_sem=sems_ici[0][3 * a + j],
            recv_sem=sems_ici[1][3 * a + j], device_id=to, device_id_type=MESH)

    def d2d(a, j, slab, half):
        return pltpu.make_async_remote_copy(
            src_ref=outs[a].at[slab, half], dst_ref=outs[a].at[slab, half], send_sem=sems_d2d[0][3 * a + j],
            recv_sem=sems_d2d[1][3 * a + j], device_id=(x, y, 1 - c), device_id_type=MESH)

    return x, y, c, chips, ici, d2d


def _async_call(body, bufs, sems_in, after, sems_out, token, *, name):
    n = len(bufs)
    n_out = 6 * n if sems_out else 0
    extra = [] if after is None else [after]
    return pl.pallas_call(
        body, name=name,
        in_specs=[_ANY] * n + [_SEM] * len(sems_in) + [_ANY] * len(extra),
        out_specs=[_ANY] * n + [_SEM] * n_out + [_VMEM] * token,
        out_shape=[jax.ShapeDtypeStruct(t.shape, t.dtype) for t in bufs]
        + [pltpu.SemaphoreType.DMA(())] * n_out + [_TOKEN] * token,
        input_output_aliases={a: a for a in range(n)},
        compiler_params=pltpu.CompilerParams(has_side_effects=_EFFECT),
    )(*bufs, *sems_in, *extra)


def gather_start(bufs, after=None, *, name):
    n = len(bufs)
    k = n + (after is not None)

    def body(*refs):
        outs = refs[k:k + n]
        sems, token = refs[k + n:k + 7 * n], refs[k + 7 * n]
        x, y, c, chips, ici, _ = _gather_copies(outs, (sems[:3 * n], sems[3 * n:]), None)
        for a in range(n):
            for j, (px, py) in enumerate(chips):
                ici(a, j, 2 * x + y, (px, py, c)).start()
        token[...] = jnp.zeros_like(token)

    res = _async_call(body, bufs, [], after, True, True, name=name)
    return res[:n], res[n:7 * n], res[7 * n]


def gather_pass(bufs, sems, after, *, name):
    n = len(bufs)

    def body(*refs):
        s_in = refs[n:7 * n]
        outs = refs[7 * n + 1:8 * n + 1]
        s_out, token = refs[8 * n + 1:14 * n + 1], refs[14 * n + 1]
        x, y, c, chips, ici, d2d = _gather_copies(outs, (s_in[:3 * n], s_in[3 * n:]),
                                                  (s_out[:3 * n], s_out[3 * n:]))
        for a in range(n):
            for j, (px, py) in enumerate(chips):
                ici(a, j, 2 * px + py, (px, py, c)).wait_recv()
                d2d(a, j, 2 * px + py, c).start()
        for a in range(n):
            for j, (px, py) in enumerate(chips):
                ici(a, j, 2 * x + y, (px, py, c)).wait_send()
        token[...] = jnp.zeros_like(token)

    res = _async_call(body, bufs, list(sems), after, True, True, name=name)
    return res[:n], res[n:7 * n], res[7 * n]


def gather_wait(bufs, sems, after, *, name):
    n = len(bufs)

    def body(*refs):
        s_in = refs[n:7 * n]
        outs = refs[7 * n + 1:]
        x, y, c, chips, _, d2d = _gather_copies(outs, None, (s_in[:3 * n], s_in[3 * n:]))
        for a in range(n):
            for j, (px, py) in enumerate(chips):
                d2d(a, j, 2 * px + py, 1 - c).wait_recv()
                d2d(a, j, 2 * px + py, c).wait_send()

    return _async_call(body, bufs, list(sems), after, False, False, name=name)


def _pair_plan(srcs, lands):
    x, y, c, _ = _place()
    return [(srcs[a].at[j, 1 - c], lands[a].at[j], (x, y, 1 - c))
            for a in range(len(srcs)) for j in range(N_SHARD)]


def _chip_plan(srcs, lands):
    x, y, c, chips = _place()
    return [(srcs[a].at[2 * px + py], lands[a].at[2 * x + y], (px, py, c))
            for a in range(len(srcs)) for (px, py) in chips]


def copies_start(arrs, land_shapes, plan, per_array, *, name):
    n = len(arrs)
    k = per_array * n

    def body(*refs):
        lands = refs[2 * n:3 * n]
        sems, token = refs[3 * n:3 * n + 2 * k], refs[3 * n + 2 * k]
        for i, (src, dst, dev) in enumerate(plan(refs[n:2 * n], lands)):
            pltpu.make_async_remote_copy(src_ref=src, dst_ref=dst, send_sem=sems[i], recv_sem=sems[k + i],
                                         device_id=dev, device_id_type=MESH).start()
        token[...] = jnp.zeros_like(token)

    res = pl.pallas_call(
        body, name=name,
        in_specs=[_ANY] * n,
        out_specs=[_ANY] * (2 * n) + [_SEM] * (2 * k) + [_VMEM],
        out_shape=[jax.ShapeDtypeStruct(t.shape, t.dtype) for t in arrs] + list(land_shapes)
        + [pltpu.SemaphoreType.DMA(())] * (2 * k) + [_TOKEN],
        input_output_aliases={a: a for a in range(n)},
        compiler_params=pltpu.CompilerParams(has_side_effects=_EFFECT),
    )(*arrs)
    return res[:n], res[n:2 * n], res[2 * n:2 * n + 2 * k], res[2 * n + 2 * k]


def copies_wait(arrs, lands, sems, after, plan, *, name):
    n = len(arrs)
    k = len(sems) // 2

    def body(*refs):
        s_in = refs[2 * n:2 * n + 2 * k]
        outs = refs[2 * n + 2 * k + 1:]
        for i, (src, dst, dev) in enumerate(plan(outs[:n], outs[n:])):
            cp = pltpu.make_async_remote_copy(src_ref=src, dst_ref=dst, send_sem=s_in[i], recv_sem=s_in[k + i],
                                              device_id=dev, device_id_type=MESH)
            cp.wait_send()
            cp.wait_recv()

    res = pl.pallas_call(
        body, name=name,
        in_specs=[_ANY] * (2 * n) + [_SEM] * (2 * k) + [_ANY],
        out_specs=[_ANY] * (2 * n),
        out_shape=[jax.ShapeDtypeStruct(t.shape, t.dtype) for t in list(arrs) + list(lands)],
        input_output_aliases={a: a for a in range(2 * n)},
        compiler_params=pltpu.CompilerParams(has_side_effects=_EFFECT),
    )(*arrs, *lands, *sems, after)
    return res[:n], res[n:]


def pair_send_halves(arrs, *, name):
    n = len(arrs)

    def body(*refs):
        ins, outs = refs[:n], refs[n:2 * n]
        send_sems, recv_sems = refs[2 * n:]
        x, y, c, _ = _place()
        copies = []
        for a in range(n):
            for j in range(N_SHARD):
                cp = pltpu.make_async_remote_copy(
                    src_ref=ins[a].at[j, 1 - c], dst_ref=outs[a].at[j], send_sem=send_sems.at[a, j],
                    recv_sem=recv_sems.at[a, j], device_id=(x, y, 1 - c), device_id_type=MESH)
                cp.start()
                copies.append(cp)
        for cp in copies:
            cp.wait()

    return pl.pallas_call(
        body, name=name,
        in_specs=[_ANY] * n, out_specs=[_ANY] * n,
        out_shape=[jax.ShapeDtypeStruct((N_SHARD,) + t.shape[2:], t.dtype) for t in arrs],
        scratch_shapes=[pltpu.SemaphoreType.DMA((n, N_SHARD)), pltpu.SemaphoreType.DMA((n, N_SHARD))],
    )(*arrs)


def chip_scatter(arrs, *, name):
    n = len(arrs)

    def body(*refs):
        ins, outs = refs[:n], refs[n:2 * n]
        send_sems, recv_sems = refs[2 * n:]
        x, y, c, chips = _place()
        me = 2 * x + y
        copies = []
        for a in range(n):
            for j, (px, py) in enumerate(chips):
                cp = pltpu.make_async_remote_copy(
                    src_ref=ins[a].at[2 * px + py], dst_ref=outs[a].at[me], send_sem=send_sems.at[a, j],
                    recv_sem=recv_sems.at[a, j], device_id=(px, py, c), device_id_type=MESH)
                cp.start()
                copies.append(cp)
        for cp in copies:
            cp.wait()

    return pl.pallas_call(
        body, name=name,
        in_specs=[_ANY] * n, out_specs=[_ANY] * n,
        out_shape=[jax.ShapeDtypeStruct(t.shape, t.dtype) for t in arrs],
        scratch_shapes=[pltpu.SemaphoreType.DMA((n, 3)), pltpu.SemaphoreType.DMA((n, 3))],
    )(*arrs)


def pair_gather(bufs, *, dep=None, name):
    n = len(bufs)
    dep_specs, dep_args = _dep_operand(dep)
    k = n + len(dep_args)

    def body(*refs):
        outs = refs[k:k + n]
        send_sems, recv_sems = refs[k + n:]
        x, y, c, _ = _place()
        copies = []
        for a in range(n):
            cp = pltpu.make_async_remote_copy(
                src_ref=outs[a].at[c], dst_ref=outs[a].at[c], send_sem=send_sems.at[a],
                recv_sem=recv_sems.at[a], device_id=(x, y, 1 - c), device_id_type=MESH)
            cp.start()
            copies.append(cp)
        for cp in copies:
            cp.wait()

    return pl.pallas_call(
        body, name=name,
        in_specs=[_ANY] * n + dep_specs, out_specs=[_ANY] * n,
        out_shape=[jax.ShapeDtypeStruct(t.shape, t.dtype) for t in bufs],
        input_output_aliases={a: a for a in range(n)},
        scratch_shapes=[pltpu.SemaphoreType.DMA((n,)), pltpu.SemaphoreType.DMA((n,))],
    )(*bufs, *dep_args)


def all_sum_small(v, *, name):
    m = v.shape[0]

    def body(v_ref, o_ref, buf, send_sems, recv_sems, local_sem):
        x, y, c, chips = _place()
        me, sibling = (x, y, c), (x, y, 1 - c)

        def rows(px, py, pc):
            return buf.at[pl.ds((4 * px + 2 * py + pc) * m, m), :]

        def copy(k, block, to, src=None):
            return pltpu.make_async_remote_copy(
                src_ref=rows(*block) if src is None else src, dst_ref=rows(*block),
                send_sem=send_sems.at[k], recv_sem=recv_sems.at[k], device_id=to, device_id_type=MESH)

        mine = pltpu.make_async_copy(v_ref, rows(*me), local_sem)
        mine.start()
        first = [copy(0, me, sibling, src=v_ref)]
        first += [copy(1 + j, me, (*chip, c), src=v_ref) for j, chip in enumerate(chips)]
        for cp in first:
            cp.start()
        passed = [copy(4 + j, (*chip, c), sibling) for j, chip in enumerate(chips)]
        for j, chip in enumerate(chips):
            copy(1 + j, (*chip, c), me).wait_recv()
            passed[j].start()
        copy(0, sibling, me).wait_recv()
        for j, chip in enumerate(chips):
            copy(4 + j, (*chip, 1 - c), me).wait_recv()
        for cp in first + passed:
            cp.wait_send()
        mine.wait()
        acc = buf[pl.ds(0, m), :]
        for k in range(1, 8):
            acc = acc + buf[pl.ds(k * m, m), :]
        o_ref[...] = acc

    vm = pl.BlockSpec(memory_space=pltpu.VMEM)
    return pl.pallas_call(
        body, name=name, in_specs=[vm], out_specs=vm,
        out_shape=jax.ShapeDtypeStruct((m, 128), F32),
        scratch_shapes=[pltpu.VMEM((8 * m, 128), F32), pltpu.SemaphoreType.DMA((7,)),
                        pltpu.SemaphoreType.DMA((7,)), pltpu.SemaphoreType.DMA],
    )(v)


def pack_slab(srcs, blocks, maps, n_slots, slot_shape, *, place, dep=None, name):
    k = len(srcs)
    per = n_slots // k
    dep_specs, dep_args = _dep_operand(dep)

    def body(place_ref, *refs):
        o_ref = refs[k + len(dep_args)]
        s = pl.program_id(0)
        for i in range(k):
            @pl.when(jnp.logical_and(s >= i * per, s < (i + 1) * per))
            def _():
                o_ref[...] = refs[i][...].astype(BF16)

    return pl.pallas_call(
        body, name=name,
        grid_spec=pltpu.PrefetchScalarGridSpec(
            num_scalar_prefetch=1, grid=(n_slots,),
            in_specs=[pl.BlockSpec(blk, functools.partial(lambda s, pr, m: m(s), m=m)) for blk, m in zip(blocks, maps)]
            + dep_specs,
            out_specs=pl.BlockSpec((None, None) + slot_shape, lambda s, pr: (pr[1], s, 0, 0))),
        out_shape=jax.ShapeDtypeStruct((N_SHARD, n_slots) + slot_shape, BF16),
        compiler_params=_params("arbitrary"),
    )(place, *srcs, *dep_args)


def sum_halves(arr, recv, c_idx, *, name):
    _, _, r, c = arr.shape
    tr = r
    while tr * c * 4 > (1 << 20) and tr % 32 == 0:
        tr //= 2

    def body(c_ref, a_ref, b_ref, o_ref):
        o_ref[...] = (a_ref[...] + b_ref[...]).astype(BF16)

    return pl.pallas_call(
        body, name=name,
        grid_spec=pltpu.PrefetchScalarGridSpec(
            num_scalar_prefetch=1, grid=(N_SHARD, r // tr),
            in_specs=[pl.BlockSpec((None, None, tr, c), lambda j, i, cr: (j, cr[0], i, 0)),
                      pl.BlockSpec((None, tr, c), lambda j, i, cr: (j, i, 0))],
            out_specs=pl.BlockSpec((None, tr, c), lambda j, i, cr: (j, i, 0))),
        out_shape=jax.ShapeDtypeStruct((N_SHARD, r, c), BF16),
        compiler_params=_params("parallel", "parallel"),
    )(c_idx, arr, recv)


def sum_chips(recv, part, place, *, name):
    _, r, c = recv.shape
    tr = r
    while tr * c * 4 > (1 << 20) and tr % 32 == 0:
        tr //= 2

    def body(place_ref, a0, a1, a2, a3, p_ref, o_ref):
        me = place_ref[1]
        acc = None
        for k, a_ref in enumerate((a0, a1, a2, a3)):
            term = jnp.where(me == k, p_ref[...], a_ref[...]).astype(F32)
            acc = term if acc is None else acc + term
        o_ref[...] = acc

    def slab(k):
        return pl.BlockSpec((None, tr, c), lambda i, pr: (jnp.where(pr[1] == k, (k + 1) % N_SHARD, k), i, 0))

    return pl.pallas_call(
        body, name=name,
        grid_spec=pltpu.PrefetchScalarGridSpec(
            num_scalar_prefetch=1, grid=(r // tr,),
            in_specs=[slab(k) for k in range(N_SHARD)]
            + [pl.BlockSpec((None, tr, c), lambda i, pr: (pr[1], i, 0))],
            out_specs=pl.BlockSpec((None, tr, c), lambda i, pr: (pr[0], i, 0))),
        out_shape=jax.ShapeDtypeStruct((2, r, c), F32),
        compiler_params=_params("parallel"),
    )(place, recv, recv, recv, recv, part)


_WEIGHTS = ("ff1_norm", "ff1_wg", "ff1_wu", "ff1_wd", "mix_norm", "w_in", "sgu_ln_g", "sgu_ln_b", "sgu_w",
            "sgu_b", "lru_conv_w", "lru_conv_b", "lru_wa", "lru_ba", "lru_wx", "lru_bx", "lru_lambda",
            "gdn_conv_w", "gdn_a_log", "gdn_dt_bias", "gdn_norm_g", "pool_w", "pool_scale", "w_branch",
            "w_out", "ff2_norm", "ff2_wg", "ff2_wu", "ff2_wd", "final_norm")
_BIG = ("ff1_wg", "ff1_wu", "ff1_wd", "w_in", "w_branch", "w_out", "ff2_wg", "ff2_wu", "ff2_wd")
_SMALL = tuple(n for n in _WEIGHTS if n not in _BIG)
_CONV = ("lru_conv_w", "gdn_conv_w")


def _seg_rows(shape):
    return -(-math.prod(shape) // 1024) * 8


def _pack(arrs):
    segs = []
    for t in arrs:
        rows = _seg_rows(t.shape)
        flat = t.reshape(-1)
        segs.append(jnp.pad(flat, (0, rows * 128 - flat.shape[0])).reshape(rows, 128))
    return jnp.concatenate(segs, axis=0)


def _unpack(buf, shapes):
    out, o = [], 0
    for shp in shapes:
        rows = _seg_rows(shp)
        out.append(buf[o:o + rows].reshape(-1)[:math.prod(shp)].reshape(shp))
        o += rows
    return out


def _block_diag(w):
    h, n, _ = w.shape
    same = jnp.arange(h)[:, None, None, None] == jnp.arange(h)[None, None, :, None]
    return jnp.where(same, w[:, :, None, :], 0.0).reshape(h * n, h * n)


def _diag_blocks(m, h=8, n=64):
    return jnp.stack([m[i * n:(i + 1) * n, i * n:(i + 1) * n] for i in range(h)])


def _lane_row(v):
    return jnp.zeros((1, 128), F32).at[0, 4:8].set(v)


def _w_in_to_padded(raw):
    sem = jnp.concatenate([raw[k, :, :P_IN_SHARD] for k in range(N_SHARD)], axis=1)
    zero = jnp.zeros((sem.shape[0], C_GATE - C_BA - 8), sem.dtype)
    return jnp.concatenate([sem[:, 2048:4096], sem[:, 0:2048], sem[:, 4104:4616], sem[:, 4096:4104], zero,
                            sem[:, 4616:P_IN]], axis=1)


def _w_in_grad_to_shards(part):
    r = part.shape[1]
    dwp = part.transpose(1, 0, 2).reshape(r, P_PAD)
    sem = jnp.concatenate([dwp[:, C_AU:C_DX], dwp[:, 0:C_AU], dwp[:, C_BA:C_BA + 8], dwp[:, C_DX:C_BA],
                           dwp[:, C_GATE:]], axis=1)
    sh = sem.reshape(r, N_SHARD, P_IN_SHARD).transpose(1, 0, 2)
    return jnp.pad(sh, ((0, 0), (0, 0), (0, P_IN_SHARD_PAD - P_IN_SHARD)))


def kernel(x, ff1_norm, ff1_wg, ff1_wu, ff1_wd, mix_norm, w_in, sgu_ln_g, sgu_ln_b, sgu_w, sgu_b, lru_conv_w,
           lru_conv_b, lru_wa, lru_ba, lru_wx, lru_bx, lru_lambda, gdn_conv_w, gdn_a_log, gdn_dt_bias, gdn_norm_g,
           pool_w, pool_scale, w_branch, w_out, ff2_norm, ff2_wg, ff2_wu, ff2_wd, final_norm, loss_target, m_ff1_norm,
           m_ff1_wg, m_ff1_wu, m_ff1_wd, m_mix_norm, m_w_in, m_sgu_ln_g, m_sgu_ln_b, m_sgu_w, m_sgu_b, m_lru_conv_w,
           m_lru_conv_b, m_lru_wa, m_lru_ba, m_lru_wx, m_lru_bx, m_lru_lambda, m_gdn_conv_w, m_gdn_a_log, m_gdn_dt_bias,
           m_gdn_norm_g, m_pool_w, m_pool_scale, m_w_branch, m_w_out, m_ff2_norm, m_ff2_wg, m_ff2_wu, m_ff2_wd,
           m_final_norm, v_ff1_norm, v_ff1_wg, v_ff1_wu, v_ff1_wd, v_mix_norm, v_w_in, v_sgu_ln_g, v_sgu_ln_b, v_sgu_w,
           v_sgu_b, v_lru_conv_w, v_lru_conv_b, v_lru_wa, v_lru_ba, v_lru_wx, v_lru_bx, v_lru_lambda, v_gdn_conv_w,
           v_gdn_a_log, v_gdn_dt_bias, v_gdn_norm_g, v_pool_w, v_pool_scale, v_w_branch, v_w_out, v_ff2_norm, v_ff2_wg,
           v_ff2_wu, v_ff2_wd, v_final_norm):
    a = dict(locals())
    n_layer = ff1_norm.shape[0]
    d = D_MODEL
    x0 = x[0]
    chip = 2 * lax.axis_index("x") + lax.axis_index("y")
    c_idx = jnp.reshape(lax.axis_index("c"), (1,)).astype(jnp.int32)
    place = jnp.stack([lax.axis_index("c"), chip]).astype(jnp.int32)

    def own_slab(t, dtype=BF16):
        buf = lax.empty((N_SHARD,) + t.shape, dtype)
        return lax.dynamic_update_slice(buf, t.astype(dtype)[None], (chip,) + (0,) * t.ndim)

    assert n_layer == 2

    def layer_bufs(l, dep=None):
        fc = FF_SHARD
        whole = lambda s: (l, 0, 0)
        rows = lambda s: (l, s, 0)
        made = [dep]

        def pack(*args, name):
            made.append(pack_slab(*args, place=place, dep=made[-1], name=name))
            return made[-1]

        bufs = []
        for tag, wg, wu, wd in (("ffn1", ff1_wg, ff1_wu, ff1_wd), ("ffn2", ff2_wg, ff2_wu, ff2_wd)):
            bufs.append([pack([wg, wu], [(None, d, fc)] * 2, [whole] * 2, 2, (d, fc), name=f"pack_{tag}_wgu_{l}"),
                         pack([wd], [(None, fc // 2, d)], [rows], 2, (fc // 2, d), name=f"pack_{tag}_wd_{l}")])
        win = pack([w_in], [(None, d // 2, P_IN_SHARD_PAD)], [rows], 2, (d // 2, P_IN_SHARD_PAD),
                   name=f"pack_w_in_{l}")
        wb = pack([w_branch], [(None, None, BRANCH_W, d // 4)], [lambda s: (l, s, 0, 0)], 4, (BRANCH_W, d // 4),
                  name=f"pack_wb_{l}")
        wo = pack([w_out], [(None, d // 8, d)], [rows], 2, (d // 8, d), name=f"pack_wo_{l}")
        return bufs[0] + [win, wb.reshape(N_SHARD, 2, 2, BRANCH_W, d // 4), wo] + bufs[1]

    n_buf = 7
    groups = {"ffn1": (0, 2), "mix": (2, 5), "ffn2": (5, 7)}

    def group_sems(sems, lo, hi):
        return list(sems[3 * lo:3 * hi]) + list(sems[3 * n_buf + 3 * lo:3 * n_buf + 3 * hi])

    def ffn_weights(wgu, wd):
        return (wgu, wgu, wd.reshape(N_SHARD, 1, FF_SHARD, d), (0, 1, 0))

    def mix_weights(g_w_in, g_wb, g_wo):
        return {"w_in": _w_in_to_padded(g_w_in.reshape(4, d, P_IN_SHARD_PAD)),
                "wb": g_wb.reshape(4, 4, BRANCH_W, d // 4).transpose(1, 2, 0, 3).reshape(4, BRANCH_W, d),
                "wo": g_wo.reshape(d, d)}

    g_lcw, g_gcw = gather_shards([own_slab(lru_conv_w, F32), own_slab(gdn_conv_w, F32)], name="gather_conv_weights")

    def small_params(l):
        return {"lcw": g_lcw[:, l].transpose(1, 0, 2).reshape(4, BRANCH_W),
                "gcw": g_gcw[:, l].transpose(1, 0, 2).reshape(4, 3 * BRANCH_W),
                "lcb": lru_conv_b[l][None],
                "bt": jnp.zeros((128, 128), F32).at[:, :4].set(sgu_b[l].T),
                "wa": _block_diag(lru_wa[l]), "wx": _block_diag(lru_wx[l]),
                "alog": _lane_row(gdn_a_log[l]), "dt": _lane_row(gdn_dt_bias[l])}

    flying = {0: gather_start(layer_bufs(0), name="gather_weights_0_start")[:2]}
    handed = {}

    def hand_on(l, grp, after):
        lo, hi = groups[grp]
        bufs_l, sems_l = flying[l]
        got, sm, tk = gather_pass(bufs_l[lo:hi], group_sems(sems_l, lo, hi), after,
                                  name=f"gather_weights_{l}_{grp}_pass")
        handed[l, grp] = (got, sm)
        return tk

    def arrived(l, grp, after):
        return gather_wait(*handed.pop((l, grp)), after, name=f"gather_weights_{l}_{grp}_wait")

    saved = []
    xs = x0
    bufs1 = layer_bufs(1)
    tok = hand_on(0, "ffn1", bufs1[4])
    for l in range(n_layer):
        p = small_params(l)
        sv = {"p": p, "x0": xs}
        p["ff1"] = ffn_weights(*arrived(l, "ffn1", tok))
        x1 = ffn_fwd(xs, ff1_norm[l][None], *p["ff1"], tb=512, name=f"ffn1_fwd_{l}")
        tok = hand_on(l, "mix", x1)
        if l == 0:
            split = groups["ffn1"][1]
            first1 = gather_start(bufs1[:split], tok, name="gather_weights_1_start_ffn1")
        h = norm_fwd(x1, mix_norm[l][None], tb=512, name=f"mix_norm_fwd_{l}")
        p.update(mix_weights(*arrived(l, "mix", h)))
        proj = matmul(h, p["w_in"], tm=1024, tn=1536, tk=1024, name=f"proj_{l}")
        ya = sgu_fwd(proj, sgu_ln_g[l][None], sgu_ln_b[l][None], sgu_w[l], p["bt"], tb=512, name=f"sgu_fwd_{l}")
        xc = conv_fwd(proj, C_BX, BRANCH_W, p["lcw"], p["lcb"], tb=512, name=f"lru_conv_fwd_{l}")
        yb, hl = lru_fwd(xc, proj, p["wa"], p["wx"], lru_ba[l][None], lru_bx[l][None], lru_lambda[l][None],
                         tb=256, name=f"lru_fwd_{l}")
        hand_on(l, "ffn2", yb)
        if l == 0:
            rest1 = gather_start(bufs1[split:], yb, name="gather_weights_1_start_rest")
            k1, k2 = 3 * split, 3 * (n_buf - split)
            flying[1] = (list(first1[0]) + list(rest1[0]),
                         list(first1[1][:k1]) + list(rest1[1][:k2]) + list(first1[1][k1:]) + list(rest1[1][k2:]))
        qkv = conv_fwd(proj, C_QKV, 3 * BRANCH_W, p["gcw"], jnp.zeros((1, 3 * BRANCH_W), F32), tb=512,
                       name=f"gdn_conv_fwd_{l}")
        yc, sall = gdn_fwd(qkv, proj, p["alog"], p["dt"], gdn_norm_g[l][None], cps=GDN_CPS, name=f"gdn_fwd_{l}")
        p["ff2"] = ffn_weights(*arrived(l, "ffn2", yc))
        dep2 = None
        if l + 1 < n_layer:
            dep2 = tok = hand_on(l + 1, "ffn1", rest1[2])
        yd = pool_fwd(proj, pool_w[l], pool_scale[l][None], tb=512, name=f"pool_fwd_{l}")
        ys = (ya, yb, yc, yd)
        x2, merged = merge_fwd(x1, ys, proj, p["wb"], p["wo"], tb=256, dep=dep2, name=f"merge_fwd_{l}")
        x3 = ffn_fwd(x2, ff2_norm[l][None], *p["ff2"], tb=512, name=f"ffn2_fwd_{l}")
        tok = x3
        sv.update(x1=x1, h=h, proj=proj, xc=xc, hl=hl, qkv=qkv, sall=sall, ys=ys, merged=merged, x2=x2)
        saved.append(sv)
        xs = x3

    dx, d_final, loss_blk = loss_head(xs, final_norm[None], loss_target[0], tb=512, name="loss_head")
    loss = lax.psum(loss_blk[0, 0], ("x", "y", "c"))

    small_g = {n: [None] * n_layer for n in _SMALL if n != "final_norm"}
    big_g = [None] * n_layer
    per_layer = 7
    flight = None

    def pair_sums(bigs, recv, tag, w_in_at):
        part = [sum_halves(t, r, c_idx, name=f"grad_pair_sum_{tag}_{i}") for i, (t, r) in enumerate(zip(bigs, recv))]
        if w_in_at is not None:
            part[w_in_at] = _w_in_grad_to_shards(part[w_in_at])
        return part

    def pair_start(arrs, tag):
        return copies_start(arrs, [jax.ShapeDtypeStruct((N_SHARD,) + t.shape[2:], t.dtype) for t in arrs],
                            _pair_plan, N_SHARD, name=f"grad_pair_exchange_{tag}_start")

    def pair_to_chip(fl, after, tag, w_in_at=None):
        sent, recv = copies_wait(*fl[:3], after, _pair_plan, name=f"grad_pair_exchange_{tag}_wait")
        part = pair_sums(sent, recv, tag, w_in_at)
        return copies_start(part, [jax.ShapeDtypeStruct(t.shape, t.dtype) for t in part], _chip_plan, 3,
                            name=f"grad_chip_exchange_{tag}_start")

    def chip_finish(fl, after, tag):
        part, recv = copies_wait(*fl[:3], after, _chip_plan, name=f"grad_chip_exchange_{tag}_wait")
        return [sum_chips(r, q, place, name=f"grad_chip_sum_{tag}_{i}") for i, (r, q) in enumerate(zip(recv, part))]

    early = mid = None

    for l in reversed(range(n_layer)):
        sv = saved[l]
        p = sv["p"]
        proj = sv["proj"]
        dx2, dg_ff2, dab2, hid2, h2, dy2 = ffn_bwd(sv["x2"], dx, ff2_norm[l][None], *p["ff2"], tb=512,
                                                  dep=None if flight is None else flight[3],
                                                  name=f"ffn2_bwd_{l}")
        dwgu2 = matmul(h2, dab2, ta=True, tm=1024, tn=FF_SHARD, tk=TK_DW, name=f"ffn2_dwgu_{l}")
        dwd2 = matmul(hid2, dy2, ta=True, tm=FF_SHARD, tn=1024, tk=TK_DW, name=f"ffn2_dwd_{l}")
        if flight is not None:
            flight = pair_to_chip(flight, dx2, "1", w_in_at=2)
        dgp, dbr, dya, dyb, dyc, dyd, dx2b = merge_bwd(dx2, sv["ys"], proj, p["wb"], p["wo"], tb=256,
                                                       dep=None if flight is None else flight[3],
                                                       name=f"merge_bwd_{l}")
        dwo = matmul(sv["merged"], dx2b, ta=True, tm=1024, tn=1024, tk=TK_DW, name=f"dwo_{l}")
        dwb = jnp.stack([matmul(sv["ys"][g], dbr[g], ta=True, tm=BRANCH_W, tn=1024, tk=TK_DW, name=f"dwb{g}_{l}")
                         for g in range(4)])
        dwb_sh = dwb.reshape(4, BRANCH_W, N_SHARD, d // N_SHARD).transpose(2, 0, 1, 3)
        g_late = [dwgu2, dwd2.reshape(N_SHARD, 2, FF_SHARD // 2, d),
                  dwb_sh.reshape(N_SHARD, 2, 2 * BRANCH_W, d // N_SHARD), dwo.reshape(N_SHARD, 2, d // 8, d)]
        tok = None
        if l == 0:
            early = pair_start(g_late, "0a")
            tok = early[3]
        du, dv, dlg, dlb, dws, dbt = sgu_bwd(proj, dya, sgu_ln_g[l][None], sgu_ln_b[l][None], sgu_w[l], p["bt"],
                                             tb=512, dep=tok, name=f"sgu_bwd_{l}")
        dxc, dgate, dwa, dwx, dba, dbx, dlam = lru_bwd(sv["xc"], proj, sv["hl"], dyb, p["wa"], p["wx"],
                                                       lru_ba[l][None], lru_bx[l][None], lru_lambda[l][None],
                                                       tb=256, name=f"lru_bwd_{l}")
        dbx_in, dcw_l = conv_bwd(proj, C_BX, BRANCH_W, dxc, p["lcw"], tb=512, name=f"lru_conv_bwd_{l}")
        dqkv, dz, dbeta, dal, ddt, dng = gdn_bwd(sv["qkv"], proj, sv["sall"], dyc, p["alog"], p["dt"],
                                                 gdn_norm_g[l][None], cps=GDN_CPS, name=f"gdn_bwd_{l}")
        dqkv_in, dcw_g = conv_bwd(proj, C_QKV, 3 * BRANCH_W, dqkv, p["gcw"], tb=512, name=f"gdn_conv_bwd_{l}")
        ddx, dwp, dsc = pool_bwd(proj, dyd, pool_w[l], pool_scale[l][None], tb=512, name=f"pool_bwd_{l}")
        if l == 0:
            early = pair_to_chip(early, ddx, "0a")
            tok = early[3]
        s = dx.shape[0]
        dproj = jnp.concatenate([dqkv_in, dz, du, dv, dbx_in, dgate, ddx, dbeta,
                                 jnp.zeros((s, C_GATE - C_BA - 128), BF16), dgp], axis=1)
        dh = matmul(dproj, p["w_in"], tb=True, tm=1024, tn=1024, tk=1536, dep=tok, name=f"dh_mix_{l}")
        dw_in = matmul(sv["h"], dproj, ta=True, tm=1024, tn=1152, tk=TK_DW, col_groups=N_SHARD, name=f"dw_in_{l}")
        dw_in = dw_in.reshape(N_SHARD, 2, d // 2, P_PAD // N_SHARD)
        if l == 0:
            mid = pair_start([dw_in], "0b")
            tok = mid[3]
        dx1, dg_mix = norm_bwd(sv["x1"], mix_norm[l][None], dh, dx2, tb=512, dep=tok, name=f"mix_norm_bwd_{l}")
        dx, dg_ff1, dab1, hid1, h1, dy1 = ffn_bwd(sv["x0"], dx1, ff1_norm[l][None], *p["ff1"],
                                                 tb=512, name=f"ffn1_bwd_{l}")
        if l == 0:
            mid = pair_to_chip(mid, dx, "0b", w_in_at=0)
            tok = mid[3]
        dwgu1 = matmul(h1, dab1, ta=True, tm=1024, tn=FF_SHARD, tk=TK_DW, dep=tok, name=f"ffn1_dwgu_{l}")
        dwd1 = matmul(hid1, dy1, ta=True, tm=FF_SHARD, tn=1024, tk=TK_DW, name=f"ffn1_dwd_{l}")

        for n, g in (("ff1_norm", dg_ff1[0]), ("mix_norm", dg_mix[0]), ("sgu_ln_g", dlg[0]), ("sgu_ln_b", dlb[0]),
                     ("sgu_w", dws), ("sgu_b", dbt[:, :4].T), ("lru_conv_w", dcw_l[:4]), ("lru_conv_b", dcw_l[4]),
                     ("lru_wa", _diag_blocks(dwa)), ("lru_ba", dba[0]), ("lru_wx", _diag_blocks(dwx)),
                     ("lru_bx", dbx[0]), ("lru_lambda", dlam[0]), ("gdn_conv_w", dcw_g[:4]),
                     ("gdn_a_log", dal[0, 4:8]), ("gdn_dt_bias", ddt[0, 4:8]), ("gdn_norm_g", dng[0]),
                     ("pool_w", dwp), ("pool_scale", dsc[0]), ("ff2_norm", dg_ff2[0])):
            small_g[n][l] = g
        big_g[l] = [dwgu1, dwd1.reshape(N_SHARD, 2, FF_SHARD // 2, d), dw_in, g_late[2], g_late[3], g_late[0], g_late[1]]
        if l == n_layer - 1:
            flight = pair_start(big_g[l], "1")

    late = pair_start(big_g[0][:2], "0c")
    red = {}
    red.update(((1, k), t) for k, t in enumerate(chip_finish(flight, late[3], "1")))
    red.update(zip(((0, 5), (0, 6), (0, 3), (0, 4)), chip_finish(early, late[3], "0a")))
    red[0, 2], = chip_finish(mid, late[3], "0b")
    late = pair_to_chip(late, red[0, 2], "0c")
    full = {}

    def pair_gather_into(keys, dep, tag):
        full.update(zip(keys, pair_gather([red[k] for k in keys], dep=dep, name=f"grad_pair_gather_{tag}")))

    pair_gather_into(sorted(red), late[3], "rest")

    def layers(k, f):
        return jnp.stack([f(full[l, k]) for l in range(n_layer)])

    grads = {}
    grads["w_in"] = layers(2, lambda t: t.reshape(d, P_IN_SHARD_PAD)[:, :P_IN_SHARD])
    grads["w_branch"] = layers(3, lambda t: t.reshape(4, BRANCH_W, d // N_SHARD))
    grads["w_out"] = layers(4, lambda t: t.reshape(d // N_SHARD, d))
    grads["ff2_wg"] = layers(5, lambda t: t[0])
    grads["ff2_wu"] = layers(5, lambda t: t[1])
    grads["ff2_wd"] = layers(6, lambda t: t.reshape(FF_SHARD, d))

    small_full = [jnp.stack(small_g[n]) for n in _SMALL if n != "final_norm"] + [d_final[0]]
    summed = _unpack(all_sum_small(_pack(small_full), name="grad_small_allreduce"), [t.shape for t in small_full])
    for n, g in zip(_SMALL, summed):
        if n in _CONV:
            width = a[n].shape[-1]
            g = lax.dynamic_slice_in_dim(g, chip * width, width, axis=2)
        grads[n] = g

    delta, new_m, new_v = {}, {}, {}

    def finish_ffn1():
        red.update(zip(((0, 0), (0, 1)), chip_finish(late, new_v["ff2_wd"], "0c")))
        pair_gather_into([(0, 0), (0, 1)], None, "0c")
        grads["ff1_wg"] = layers(0, lambda t: t[0])
        grads["ff1_wu"] = layers(0, lambda t: t[1])
        grads["ff1_wd"] = layers(1, lambda t: t.reshape(FF_SHARD, d))

    for n in sorted(_BIG, key=lambda n: n.startswith("ff1_")):
        if n.startswith("ff1_") and n not in grads:
            finish_ffn1()
        args = (a[n], grads[n], a["m_" + n], a["v_" + n])
        if a[n].shape[-1] == FF_SHARD:
            outs = adamw(*(t.swapaxes(1, 2) for t in args), name=f"adamw_{n}")
            delta[n], new_m[n], new_v[n] = (t.swapaxes(1, 2) for t in outs)
        elif n == "w_in":
            outs = adamw(*(t.transpose(2, 0, 1) for t in args), name=f"adamw_{n}")
            delta[n], new_m[n], new_v[n] = (t.transpose(1, 2, 0) for t in outs)
        else:
            delta[n], new_m[n], new_v[n] = adamw(*args, name=f"adamw_{n}")
    shapes = [a[n].shape for n in _SMALL]
    packed = [_pack([src[pre + n] for n in _SMALL]) for src, pre in ((a, ""), (grads, ""), (a, "m_"), (a, "v_"))]
    for store, buf in zip((delta, new_m, new_v), adamw(*packed, name="adamw_small")):
        store.update(zip(_SMALL, _unpack(buf, shapes)))

    return (loss, dx[None], *[grads[n] for n in _WEIGHTS], *[delta[n] for n in _WEIGHTS],
            *[new_m[n] for n in _WEIGHTS], *[new_v[n] for n in _WEIGHTS])
```

```python
import functools
import math

import jax
import jax.numpy as jnp
from jax import lax
from jax.experimental import pallas as pl
from jax.experimental.pallas import tpu as pltpu

F32 = jnp.float32
BF16 = jnp.bfloat16
MESH = pl.DeviceIdType.MESH

D_MODEL = 1024
N_SHARD = 4
D_FF = 2816
FF_SHARD = D_FF // N_SHARD
BRANCH_W = 512
CHUNK = 64
GDN_HEADS = 4
GDN_DK = 128
GDN_CPS = 4
FFN_PARTS = 2
TK_DW = 2048
LRU_C = 8.0
EPS = 1e-6
P_IN = 8712
P_IN_SHARD = P_IN // N_SHARD
P_IN_SHARD_PAD = 2304
P_PAD = 9216
C_QKV, C_Z, C_AU, C_AV, C_BX, C_BG, C_DX, C_BA, C_GATE = 0, 1536, 2048, 2560, 3072, 3584, 4096, 4608, 5120

ADAM_LR, ADAM_B1, ADAM_B2, ADAM_EPS, ADAM_WD, ADAM_STEP = 0.001, 0.9, 0.999, 1e-08, 0.01, 10


def _gelu(x):
    return 0.5 * x * (1.0 + jnp.tanh(0.7978845608028654 * (x + 0.044715 * (x * x * x))))


def _gelu_grad(x):
    u = 0.7978845608028654 * (x + 0.044715 * (x * x * x))
    t = jnp.tanh(u)
    return 0.5 * (1.0 + t) + 0.5 * x * (1.0 - t * t) * 0.7978845608028654 * (1.0 + 3.0 * 0.044715 * x * x)


def _silu(x):
    return x * jax.nn.sigmoid(x)


def _softplus(x):
    return jnp.maximum(x, 0.0) + jnp.log1p(jnp.exp(-jnp.abs(x)))


def _dg(a, b, ca, cb):
    return lax.dot_general(a.astype(BF16), b.astype(BF16), (((ca,), (cb,)), ((), ())),
                           preferred_element_type=F32)


def _dg3(a, b, ca, cb):
    a1 = a.astype(BF16)
    a2 = (a - a1.astype(F32)).astype(BF16)
    b1 = b.astype(BF16)
    b2 = (b - b1.astype(F32)).astype(BF16)
    dn = (((ca,), (cb,)), ((), ()))
    d = functools.partial(lax.dot_general, dimension_numbers=dn, preferred_element_type=F32)
    return d(a1, b1) + (d(a1, b2) + d(a2, b1))


def _make_mm(ca, cb, dot):
    @jax.custom_vjp
    def f(a, b):
        return dot(a, b, ca, cb)

    def fwd(a, b):
        return dot(a, b, ca, cb), (a, b)

    def bwd(res, g):
        a, b = res
        if (ca, cb) == (1, 0):
            return dot(g, b, 1, 1), dot(a, g, 0, 0)
        if (ca, cb) == (1, 1):
            return dot(g, b, 1, 0), dot(g, a, 0, 0)
        return dot(b, g, 1, 1), dot(a, g, 1, 0)

    f.defvjp(fwd, bwd)
    return f


mm_nn = _make_mm(1, 0, _dg)


def _bdg(a, b, ca, cb):
    return lax.dot_general(a.astype(BF16), b.astype(BF16), (((ca,), (cb,)), ((0,), (0,))),
                           preferred_element_type=F32)


def _split2(x):
    x1 = x.astype(BF16)
    return x1, (x - x1.astype(F32)).astype(BF16)


def _bdg3(a, b, ca, cb):
    a1, a2 = _split2(a)
    b1, b2 = _split2(b)
    d = functools.partial(lax.dot_general, dimension_numbers=(((ca,), (cb,)), ((0,), (0,))),
                          preferred_element_type=F32)
    return d(a1, b1) + (d(a1, b2) + d(a2, b1))


def _make_bmm(ca, cb):
    @jax.custom_vjp
    def f(a, b):
        return _bdg(a, b, ca, cb)

    def fwd(a, b):
        return _bdg(a, b, ca, cb), (a, b)

    def bwd(res, g):
        a, b = res
        if (ca, cb) == (2, 1):
            return _bdg(g, b, 2, 2), _bdg(a, g, 1, 1)
        if (ca, cb) == (2, 2):
            return _bdg(g, b, 2, 1), _bdg(g, a, 1, 1)
        return _bdg(b, g, 2, 2), _bdg(a, g, 2, 1)

    f.defvjp(fwd, bwd)
    return f


bmm_nn = _make_bmm(2, 1)
bmm_nt = _make_bmm(2, 2)
bmm_tn = _make_bmm(1, 1)


def _iota3(shape, dim):
    return lax.broadcasted_iota(jnp.int32, shape, dim)


@jax.custom_vjp
def _tri_inv(aw):
    b, n, _ = aw.shape
    shp = (b, n, 2 * n)
    left = _iota3(shp, 2) < n
    x = jnp.where(_iota3(shp, 1) == _iota3(shp, 2), 1.0, 0.0) - aw
    zero = jnp.zeros((b, n, 4 * n), BF16)
    k = 1
    while k < n:
        x1, x2 = _split2(x)
        lhs = jnp.concatenate([x1, x2], axis=1)
        rhs = jnp.concatenate([zero, jnp.concatenate([x1, x2], axis=2)], axis=1)
        r = lax.dot_general(lhs, rhs, (((2,), (1,)), ((0,), (0,))), preferred_element_type=F32)
        mx = (r[:, :n, :2 * n] + r[:, :n, 2 * n:]) + (r[:, n:, :2 * n] + r[:, n:, 2 * n:])
        x = jnp.where(left, x, 0.0) + mx
        k *= 2
    return jnp.where(left, x, 0.0)


def _tri_inv_fwd(aw):
    t = _tri_inv(aw)
    return t, t


def _tri_inv_bwd(t, dt):
    n = t.shape[1]
    x = _bdg3(t, dt, 1, 1)[:, :n]
    tp = jnp.concatenate([jnp.zeros_like(t), t], axis=1)
    return (-_bdg3(x, tp, 2, 2),)


_tri_inv.defvjp(_tri_inv_fwd, _tri_inv_bwd)


def _rows(shape):
    return lax.broadcasted_iota(jnp.int32, shape, 0)


def _lanes(shape):
    return lax.broadcasted_iota(jnp.int32, shape, 1)


def _col(x, j):
    return jnp.sum(jnp.where(_lanes(x.shape) == j, x, 0.0), axis=1, keepdims=True)


def _put_col(col, j, width):
    shape = (col.shape[0], width)
    return jnp.where(_lanes(shape) == j, jnp.broadcast_to(col, shape), 0.0)


VMEM_BYTES_V7X = 64 << 20


def _params(*sem, vmem_bytes=None):
    return pltpu.CompilerParams(dimension_semantics=sem, vmem_limit_bytes=vmem_bytes)


def matmul(a, b, *, ta=False, tb=False, tm, tn, tk, out_dtype=F32, col_groups=1, dep=None, name):
    ba, bb = a.shape[:-2], b.shape[:-2]
    batch = ba if len(ba) >= len(bb) else bb
    assert ba in ((), batch) and bb in ((), batch)
    nb = len(batch)
    (m, k) = (a.shape[-1], a.shape[-2]) if ta else (a.shape[-2], a.shape[-1])
    (k2, n) = (b.shape[-1], b.shape[-2]) if tb else (b.shape[-2], b.shape[-1])
    tm, tn, tk = min(tm, m), min(tn, n), min(tk, k)
    assert k == k2 and m % tm == 0 and n % tn == 0 and k % tk == 0, (a.shape, b.shape, tm, tn, tk)
    nk = k // tk
    grid = batch + (m // tm, n // tn, nk)

    def a_map(*g):
        i, kk = g[nb], g[nb + 2]
        return (g[:nb] if ba else ()) + ((kk, i) if ta else (i, kk))

    def b_map(*g):
        j, kk = g[nb + 1], g[nb + 2]
        return (g[:nb] if bb else ()) + ((j, kk) if tb else (kk, j))

    per_group = n // tn // col_groups
    assert col_groups == 1 or (nb == 0 and per_group * col_groups * tn == n)

    def o_map(*g):
        if col_groups > 1:
            return (g[1] // per_group, g[0], g[1] % per_group)
        return g[:nb] + (g[nb], g[nb + 1])

    a_blk = (None,) * len(ba) + ((tk, tm) if ta else (tm, tk))
    b_blk = (None,) * len(bb) + ((tn, tk) if tb else (tk, tn))
    o_blk = (None,) * (nb + (col_groups > 1)) + (tm, tn)
    o_shape = (col_groups, m, n // col_groups) if col_groups > 1 else batch + (m, n)

    dep_specs, dep_args = _dep_operand(dep)

    def body(a_ref, b_ref, *rest):
        o_ref, acc_ref = rest[len(dep_args):]
        kk = pl.program_id(nb + 2)

        @pl.when(kk == 0)
        def _():
            acc_ref[...] = jnp.zeros_like(acc_ref)

        acc_ref[...] += _dg(a_ref[...], b_ref[...], 0 if ta else 1, 1 if tb else 0)

        @pl.when(kk == nk - 1)
        def _():
            o_ref[...] = acc_ref[...].astype(o_ref.dtype)

    return pl.pallas_call(
        body, name=name, grid=grid,
        in_specs=[pl.BlockSpec(a_blk, a_map), pl.BlockSpec(b_blk, b_map)] + dep_specs,
        out_specs=pl.BlockSpec(o_blk, o_map),
        out_shape=jax.ShapeDtypeStruct(o_shape, out_dtype),
        scratch_shapes=[pltpu.VMEM((tm, tn), F32)],
        compiler_params=_params(*(("parallel",) * (nb + 2) + ("arbitrary",))),
    )(a, b, *dep_args)


def norm_fwd(x, g, *, tb, name):
    s, d = x.shape

    def body(x_ref, g_ref, h_ref):
        xv = x_ref[...]
        r = lax.rsqrt(jnp.mean(xv * xv, axis=-1, keepdims=True) + EPS)
        h_ref[...] = (xv * r * g_ref[...]).astype(BF16)

    return pl.pallas_call(
        body, name=name, grid=(s // tb,),
        in_specs=[pl.BlockSpec((tb, d), lambda i: (i, 0)), pl.BlockSpec((1, d), lambda i: (0, 0))],
        out_specs=pl.BlockSpec((tb, d), lambda i: (i, 0)),
        out_shape=jax.ShapeDtypeStruct((s, d), BF16),
        compiler_params=_params("parallel"),
    )(x, g)


def _rms_bwd(xv, gv, dh):
    r = lax.rsqrt(jnp.mean(xv * xv, axis=-1, keepdims=True) + EPS)
    xh = xv * r
    dxh = dh * gv
    dx = r * (dxh - xh * jnp.mean(dxh * xh, axis=-1, keepdims=True))
    return dx, jnp.sum(dh * xh, axis=0, keepdims=True)


def norm_bwd(x, g, dh, dres, *, tb, dep=None, name):
    s, d = x.shape
    dep_specs, dep_args = _dep_operand(dep)

    def body(x_ref, g_ref, dh_ref, dres_ref, *rest):
        dx_ref, dg_ref = rest[len(dep_args):]
        dx, dg = _rms_bwd(x_ref[...], g_ref[...], dh_ref[...])
        dx_ref[...] = dres_ref[...] + dx

        @pl.when(pl.program_id(0) == 0)
        def _():
            dg_ref[...] = jnp.zeros_like(dg_ref)

        dg_ref[...] += dg

    row = pl.BlockSpec((tb, d), lambda i: (i, 0))
    one = pl.BlockSpec((1, d), lambda i: (0, 0))
    return pl.pallas_call(
        body, name=name, grid=(s // tb,),
        in_specs=[row, one, row, row] + dep_specs, out_specs=[row, one],
        out_shape=[jax.ShapeDtypeStruct((s, d), F32), jax.ShapeDtypeStruct((1, d), F32)],
        compiler_params=_params("arbitrary"),
    )(x, g, dh, dres, *dep_args)


def _dep_operand(dep):
    return ([], []) if dep is None else ([_ANY], [dep])


def ffn_fwd(x, g, wg, wu, wd, layer, *, tb, dep=None, name):
    s, d = x.shape
    fc = wd.shape[-2]
    dep_specs, dep_args = _dep_operand(dep)

    def body(x_ref, g_ref, wg_ref, wu_ref, wd_ref, *rest):
        o_ref, h_scr, acc_scr = rest[len(dep_args):]
        j = pl.program_id(1)

        @pl.when(j == 0)
        def _():
            xv = x_ref[...]
            r = lax.rsqrt(jnp.mean(xv * xv, axis=-1, keepdims=True) + EPS)
            h_scr[...] = (xv * r * g_ref[...]).astype(BF16)
            acc_scr[...] = jnp.zeros_like(acc_scr)

        h = h_scr[...]
        a = _dg(h, wg_ref[...], 1, 0)
        b = _dg(h, wu_ref[...], 1, 0)
        acc_scr[...] += _dg(_silu(a) * b, wd_ref[...], 1, 0)

        @pl.when(j == N_SHARD - 1)
        def _():
            o_ref[...] = x_ref[...] + 0.5 * acc_scr[...]

    row = pl.BlockSpec((tb, d), lambda i, j: (i, 0))
    lg, lu, ld = layer
    return pl.pallas_call(
        body, name=name, grid=(s // tb, N_SHARD),
        in_specs=[row, pl.BlockSpec((1, d), lambda i, j: (0, 0)),
                  pl.BlockSpec((None, None, d, fc), lambda i, j: (j, lg, 0, 0)),
                  pl.BlockSpec((None, None, d, fc), lambda i, j: (j, lu, 0, 0)),
                  pl.BlockSpec((None, None, fc, d), lambda i, j: (j, ld, 0, 0))] + dep_specs,
        out_specs=row,
        out_shape=jax.ShapeDtypeStruct((s, d), F32),
        scratch_shapes=[pltpu.VMEM((tb, d), BF16), pltpu.VMEM((tb, d), F32)],
        compiler_params=_params("parallel", "arbitrary"),
    )(x, g, wg, wu, wd, *dep_args)


def ffn_bwd(x, dxo, g, wg, wu, wd, layer, *, tb, dep=None, name):
    s, d = x.shape
    fc = wd.shape[-2]
    dep_specs, dep_args = _dep_operand(dep)

    def body(x_ref, dxo_ref, g_ref, wg_ref, wu_ref, wd_ref, *rest):
        dx_ref, dg_ref, dab_ref, hid_ref, h_ref, dy_ref, dh_scr = rest[len(dep_args):]
        i, j = pl.program_id(0), pl.program_id(1)

        @pl.when(j == 0)
        def _():
            xv = x_ref[...]
            r = lax.rsqrt(jnp.mean(xv * xv, axis=-1, keepdims=True) + EPS)
            h_ref[...] = (xv * r * g_ref[...]).astype(BF16)
            dy_ref[...] = (0.5 * dxo_ref[...]).astype(BF16)
            dh_scr[...] = jnp.zeros_like(dh_scr)

        parts = [slice(k * (tb // FFN_PARTS), (k + 1) * (tb // FFN_PARTS)) for k in range(FFN_PARTS)]
        a = [_dg(h_ref[rs, :], wg_ref[...], 1, 0) for rs in parts]
        b = [_dg(h_ref[rs, :], wu_ref[...], 1, 0) for rs in parts]
        dhid = [_dg(dy_ref[rs, :], wd_ref[...], 1, 1) for rs in parts]
        da, db = [], []
        for k, rs in enumerate(parts):
            sg = jax.nn.sigmoid(a[k])
            sl = a[k] * sg
            da.append((dhid[k] * b[k] * (sg * (1.0 + a[k] * (1.0 - sg)))).astype(BF16))
            db.append((dhid[k] * sl).astype(BF16))
            dab_ref[0, rs, :] = da[k]
            dab_ref[1, rs, :] = db[k]
            hid_ref[rs, :] = (sl * b[k]).astype(BF16)
        for k, rs in enumerate(parts):
            dh_scr[rs, :] += _dg(da[k], wg_ref[...], 1, 1) + _dg(db[k], wu_ref[...], 1, 1)

        @pl.when(jnp.logical_and(i == 0, j == 0))
        def _():
            dg_ref[...] = jnp.zeros_like(dg_ref)

        @pl.when(j == N_SHARD - 1)
        def _():
            dx, dg = _rms_bwd(x_ref[...], g_ref[...], dh_scr[...])
            dx_ref[...] = dxo_ref[...] + dx
            dg_ref[...] += dg

    row = pl.BlockSpec((tb, d), lambda i, j: (i, 0))
    one = pl.BlockSpec((1, d), lambda i, j: (0, 0))
    lg, lu, ld = layer
    return pl.pallas_call(
        body, name=name, grid=(s // tb, N_SHARD),
        in_specs=[row, row, one,
                  pl.BlockSpec((None, None, d, fc), lambda i, j: (j, lg, 0, 0)),
                  pl.BlockSpec((None, None, d, fc), lambda i, j: (j, lu, 0, 0)),
                  pl.BlockSpec((None, None, fc, d), lambda i, j: (j, ld, 0, 0))] + dep_specs,
        out_specs=[row, one,
                   pl.BlockSpec((None, 2, tb, fc), lambda i, j: (j, 0, i, 0)),
                   pl.BlockSpec((None, tb, fc), lambda i, j: (j, i, 0)),
                   row, row],
        out_shape=[jax.ShapeDtypeStruct((s, d), F32), jax.ShapeDtypeStruct((1, d), F32),
                   jax.ShapeDtypeStruct((N_SHARD, 2, s, fc), BF16),
                   jax.ShapeDtypeStruct((N_SHARD, s, fc), BF16),
                   jax.ShapeDtypeStruct((s, d), BF16), jax.ShapeDtypeStruct((s, d), BF16)],
        scratch_shapes=[pltpu.VMEM((tb, d), F32)],
        compiler_params=_params("arbitrary", "arbitrary",
                                vmem_bytes=min(VMEM_BYTES_V7X - (8 << 20),
                                               12 * d * fc + 44 * tb * d + 40 * tb * fc + (4 << 20))),
    )(x, dxo, g, wg, wu, wd, *dep_args)


def _pblk(tb, width, col0, rev_n=None):
    assert col0 % width == 0
    cb = col0 // width
    if rev_n is None:
        return pl.BlockSpec((tb, width), lambda i: (i, cb))
    return pl.BlockSpec((tb, width), lambda i: (rev_n - 1 - i, cb))


def _full(shape):
    return pl.BlockSpec(shape, lambda i: (0,) * len(shape))


def conv_fwd(src, col0, width, w, bias, *, tb, name):
    s = src.shape[0]
    cb = col0 // width

    def body(x_ref, p_ref, w_ref, b_ref, y_ref):
        i = pl.program_id(0)
        prev = jnp.where(i > 0, p_ref[...], 0.0)
        xe = jnp.concatenate([prev, x_ref[...]], axis=0)
        wv = w_ref[...]
        acc = b_ref[...] + wv[3:4] * xe[8:]
        for k in range(3):
            acc = acc + wv[k:k + 1] * pltpu.roll(xe, 3 - k, 0)[8:]
        y_ref[...] = acc

    return pl.pallas_call(
        body, name=name, grid=(s // tb,),
        in_specs=[_pblk(tb, width, col0),
                  pl.BlockSpec((8, width), lambda i: (jnp.maximum(i * (tb // 8) - 1, 0), cb)),
                  _full((4, width)), _full((1, width))],
        out_specs=pl.BlockSpec((tb, width), lambda i: (i, 0)),
        out_shape=jax.ShapeDtypeStruct((s, width), F32),
        compiler_params=_params("parallel"),
    )(src, src, w, bias)


def conv_bwd(src, col0, width, dpre, w, *, tb, name):
    s = src.shape[0]
    cb = col0 // width
    nt = s // tb

    def body(x_ref, p_ref, d_ref, n_ref, w_ref, dx_ref, dwb_ref):
        i = pl.program_id(0)
        d = d_ref[...]
        nxt = jnp.where(i < nt - 1, n_ref[...], 0.0)
        de = jnp.concatenate([d, nxt], axis=0)
        wv = w_ref[...]
        dx = wv[3:4] * d
        for k in range(3):
            dx = dx + wv[k:k + 1] * pltpu.roll(de, tb + 8 - (3 - k), 0)[:tb]
        dx_ref[...] = dx.astype(BF16)

        @pl.when(i == 0)
        def _():
            dwb_ref[...] = jnp.zeros_like(dwb_ref)

        prev = jnp.where(i > 0, p_ref[...], 0.0)
        xe = jnp.concatenate([prev, x_ref[...]], axis=0)
        dwb_ref[3:4, :] += jnp.sum(d * xe[8:], axis=0, keepdims=True)
        for k in range(3):
            dwb_ref[k:k + 1, :] += jnp.sum(d * pltpu.roll(xe, 3 - k, 0)[8:], axis=0, keepdims=True)
        dwb_ref[4:5, :] += jnp.sum(d, axis=0, keepdims=True)

    return pl.pallas_call(
        body, name=name, grid=(nt,),
        in_specs=[_pblk(tb, width, col0),
                  pl.BlockSpec((8, width), lambda i: (jnp.maximum(i * (tb // 8) - 1, 0), cb)),
                  pl.BlockSpec((tb, width), lambda i: (i, 0)),
                  pl.BlockSpec((8, width), lambda i: (jnp.minimum((i + 1) * (tb // 8), s // 8 - 1), 0)),
                  _full((4, width))],
        out_specs=[pl.BlockSpec((tb, width), lambda i: (i, 0)), _full((8, width))],
        out_shape=[jax.ShapeDtypeStruct((s, width), BF16), jax.ShapeDtypeStruct((8, width), F32)],
        compiler_params=_params("arbitrary"),
    )(src, src, dpre, dpre, w)


SGU_BLOCK = 128


def _sgu_pre(u, v, lg, lb):
    ug = _gelu(u)
    vg = _gelu(v)
    mu = jnp.mean(vg, axis=-1, keepdims=True)
    var = jnp.mean(jnp.square(vg - mu), axis=-1, keepdims=True)
    return ug, (vg - mu) * lax.rsqrt(var + EPS) * lg + lb


def _sgu_mask():
    shp = (SGU_BLOCK, SGU_BLOCK)
    return (_rows(shp) // CHUNK) >= (_lanes(shp) // CHUNK)


def sgu_fwd(proj, lg, lb, ws, bt, *, tb, name):
    s = proj.shape[0]
    w = BRANCH_W

    def body(u_ref, v_ref, lg_ref, lb_ref, w_ref, b_ref, y_ref):
        ug, vn = _sgu_pre(u_ref[...], v_ref[...], lg_ref[...], lb_ref[...])
        mask = _sgu_mask()
        for g in range(4):
            wm = jnp.where(mask, w_ref[g], 0.0)
            bcol = _col(b_ref[...], g)
            cs = slice(128 * g, 128 * g + 128)
            for n in range(tb // SGU_BLOCK):
                rs = slice(SGU_BLOCK * n, SGU_BLOCK * (n + 1))
                mixed = _dg(wm, vn[rs, cs], 1, 0) + bcol
                y_ref[rs, cs] = (ug[rs, cs] * mixed).astype(BF16)

    return pl.pallas_call(
        body, name=name, grid=(s // tb,),
        in_specs=[_pblk(tb, w, C_AU), _pblk(tb, w, C_AV), _full((1, w)), _full((1, w)),
                  _full((4, 128, 128)), _full((128, 128))],
        out_specs=pl.BlockSpec((tb, w), lambda i: (i, 0)),
        out_shape=jax.ShapeDtypeStruct((s, w), BF16),
        compiler_params=_params("parallel"),
    )(proj, proj, lg, lb, ws, bt)


def sgu_bwd(proj, dy, lg, lb, ws, bt, *, tb, dep=None, name):
    s = proj.shape[0]
    w = BRANCH_W
    dep_specs, dep_args = _dep_operand(dep)

    def body(u_ref, v_ref, dy_ref, lg_ref, lb_ref, w_ref, b_ref, *rest):
        du_ref, dv_ref, dlg_ref, dlb_ref, dw_ref, db_ref, dug_scr, dvn_scr = rest[len(dep_args):]

        @pl.when(pl.program_id(0) == 0)
        def _():
            dlg_ref[...] = jnp.zeros_like(dlg_ref)
            dlb_ref[...] = jnp.zeros_like(dlb_ref)
            dw_ref[...] = jnp.zeros_like(dw_ref)
            db_ref[...] = jnp.zeros_like(db_ref)

        (ug, vn), vf = jax.vjp(_sgu_pre, u_ref[...], v_ref[...], lg_ref[...], lb_ref[...])
        dyv = dy_ref[...]
        mask = _sgu_mask()
        for g in range(4):
            wm = jnp.where(mask, w_ref[g], 0.0)
            bcol = _col(b_ref[...], g)
            cs = slice(128 * g, 128 * g + 128)
            dwg = jnp.zeros((SGU_BLOCK, SGU_BLOCK), F32)
            dbc = jnp.zeros((SGU_BLOCK, 1), F32)
            for n in range(tb // SGU_BLOCK):
                rs = slice(SGU_BLOCK * n, SGU_BLOCK * (n + 1))
                blk = vn[rs, cs]
                mixed = _dg(wm, blk, 1, 0) + bcol
                dmix = dyv[rs, cs] * ug[rs, cs]
                dug_scr[rs, cs] = dyv[rs, cs] * mixed
                dvn_scr[rs, cs] = _dg(wm, dmix, 0, 0)
                dwg = dwg + _dg(dmix, blk, 1, 1)
                dbc = dbc + jnp.sum(dmix, axis=1, keepdims=True)
            dw_ref[g] += jnp.where(mask, dwg, 0.0)
            db_ref[...] += _put_col(dbc, g, 128)
        du, dv, dlg, dlb = vf((dug_scr[...], dvn_scr[...]))
        du_ref[...] = du.astype(BF16)
        dv_ref[...] = dv.astype(BF16)
        dlg_ref[...] += dlg
        dlb_ref[...] += dlb

    row = pl.BlockSpec((tb, w), lambda i: (i, 0))
    return pl.pallas_call(
        body, name=name, grid=(s // tb,),
        in_specs=[_pblk(tb, w, C_AU), _pblk(tb, w, C_AV), row, _full((1, w)), _full((1, w)),
                  _full((4, 128, 128)), _full((128, 128))] + dep_specs,
        out_specs=[row, row, _full((1, w)), _full((1, w)), _full((4, 128, 128)), _full((128, 128))],
        out_shape=[jax.ShapeDtypeStruct((s, w), BF16), jax.ShapeDtypeStruct((s, w), BF16),
                   jax.ShapeDtypeStruct((1, w), F32), jax.ShapeDtypeStruct((1, w), F32),
                   jax.ShapeDtypeStruct((4, 128, 128), F32), jax.ShapeDtypeStruct((128, 128), F32)],
        scratch_shapes=[pltpu.VMEM((tb, w), F32), pltpu.VMEM((tb, w), F32)],
        compiler_params=_params("arbitrary"),
    )(proj, proj, dy, lg, lb, ws, bt, *dep_args)


POOL_HALO = 16


def _pooled(xe, t0, tb):
    s2 = xe + pltpu.roll(xe, 1, 0)
    s4 = s2 + pltpu.roll(s2, 2, 0)
    s8 = s4 + pltpu.roll(s4, 4, 0)
    s16 = s8 + pltpu.roll(s8, 8, 0)
    t = (t0 + _rows((tb, 128))).astype(F32)
    outs = []
    for g, sw in enumerate((s2, s4, s8, s16)):
        cs = slice(128 * g, 128 * g + 128)
        cnt = jnp.minimum(t + 1.0, float(2 ** (g + 1)))
        outs.append(sw[POOL_HALO:, cs] / cnt - xe[POOL_HALO:, cs])
    return outs


def pool_fwd(proj, wp, scale, *, tb, name):
    s = proj.shape[0]
    w = BRANCH_W

    def body(x_ref, p_ref, w_ref, sc_ref, y_ref):
        i = pl.program_id(0)
        prev = jnp.where(i > 0, p_ref[...], 0.0)
        xe = jnp.concatenate([prev, x_ref[...]], axis=0)
        pooled = _pooled(xe, i * tb, tb)
        sc = sc_ref[...]
        for g in range(4):
            cs = slice(128 * g, 128 * g + 128)
            y_ref[:, cs] = (_dg(pooled[g], w_ref[g], 1, 0) * sc[:, cs]).astype(BF16)

    cb = C_DX // w
    return pl.pallas_call(
        body, name=name, grid=(s // tb,),
        in_specs=[_pblk(tb, w, C_DX),
                  pl.BlockSpec((POOL_HALO, w), lambda i: (jnp.maximum(i * (tb // POOL_HALO) - 1, 0), cb)),
                  _full((4, 128, 128)), _full((1, w))],
        out_specs=pl.BlockSpec((tb, w), lambda i: (i, 0)),
        out_shape=jax.ShapeDtypeStruct((s, w), BF16),
        compiler_params=_params("parallel"),
    )(proj, proj, wp, scale)


def pool_bwd(proj, dy, wp, scale, *, tb, name):
    s = proj.shape[0]
    w = BRANCH_W
    nt = s // tb
    te = tb + POOL_HALO

    def body(x_ref, p_ref, dy_ref, n_ref, w_ref, sc_ref, dx_ref, dw_ref, dsc_ref):
        i = pl.program_id(0)

        @pl.when(i == 0)
        def _():
            dw_ref[...] = jnp.zeros_like(dw_ref)
            dsc_ref[...] = jnp.zeros_like(dsc_ref)

        prev = jnp.where(i > 0, p_ref[...], 0.0)
        xe = jnp.concatenate([prev, x_ref[...]], axis=0)
        pooled = _pooled(xe, i * tb, tb)
        dyv = dy_ref[...]
        nxt = jnp.where(i < nt - 1, n_ref[...], 0.0)
        dye = jnp.concatenate([dyv, nxt], axis=0)
        sc = sc_ref[...]
        t = (i * tb + _rows((te, 128))).astype(F32)
        for g in range(4):
            cs = slice(128 * g, 128 * g + 128)
            win = 2 ** (g + 1)
            dpm = dye[:, cs] * sc[:, cs]
            dpool = _dg(dpm, w_ref[g], 1, 1)
            q = dpool / jnp.minimum(t + 1.0, float(win))
            r, sh = q, 1
            while sh < win:
                r = r + pltpu.roll(r, te - sh, 0)
                sh *= 2
            dx_ref[:, cs] = (r[:tb] - dpool[:tb]).astype(BF16)
            dw_ref[g] += _dg(pooled[g], dpm[:tb], 0, 0)
            dsc_ref[:, cs] += jnp.sum(dyv[:, cs] * _dg(pooled[g], w_ref[g], 1, 0), axis=0, keepdims=True)

    cb = C_DX // w
    row = pl.BlockSpec((tb, w), lambda i: (i, 0))
    return pl.pallas_call(
        body, name=name, grid=(nt,),
        in_specs=[_pblk(tb, w, C_DX),
                  pl.BlockSpec((POOL_HALO, w), lambda i: (jnp.maximum(i * (tb // POOL_HALO) - 1, 0), cb)),
                  row,
                  pl.BlockSpec((POOL_HALO, w),
                               lambda i: (jnp.minimum((i + 1) * (tb // POOL_HALO), s // POOL_HALO - 1), 0)),
                  _full((4, 128, 128)), _full((1, w))],
        out_specs=[row, _full((4, 128, 128)), _full((1, w))],
        out_shape=[jax.ShapeDtypeStruct((s, w), BF16), jax.ShapeDtypeStruct((4, 128, 128), F32),
                   jax.ShapeDtypeStruct((1, w), F32)],
        compiler_params=_params("arbitrary"),
    )(proj, proj, dy, dy, wp, scale)


def _neg_expm1(z):
    p = 1.0 + z * (1.0 / 9.0)
    for n in (8.0, 7.0, 6.0, 5.0, 4.0, 3.0, 2.0):
        p = 1.0 + z * (1.0 / n) * p
    return jnp.where(z > -0.5, -z * p, 1.0 - jnp.exp(z))


def _lru_gates(xc, wa, wx, ba, bx, lam):
    r = jax.nn.sigmoid(mm_nn(xc, wa) + ba)
    i = jax.nn.sigmoid(mm_nn(xc, wx) + bx)
    log_a = -LRU_C * r * _softplus(-lam)
    a = jnp.exp(log_a)
    return a, jnp.sqrt(_neg_expm1(2.0 * log_a)) * (i * xc)


def _scan_down(a, b):
    n = a.shape[0]
    row = _rows(a.shape)
    sh = 1
    while sh < n:
        keep = row >= sh
        a_sh = jnp.where(keep, pltpu.roll(a, sh, 0), 1.0)
        b_sh = jnp.where(keep, pltpu.roll(b, sh, 0), 0.0)
        b = a * b_sh + b
        a = a * a_sh
        sh *= 2
    return a, b


def _scan_up(a, b):
    n = a.shape[0]
    row = _rows(a.shape)
    sh = 1
    while sh < n:
        keep = row < n - sh
        a_sh = jnp.where(keep, pltpu.roll(a, n - sh, 0), 1.0)
        b_sh = jnp.where(keep, pltpu.roll(b, n - sh, 0), 0.0)
        b = a * b_sh + b
        a = a * a_sh
        sh *= 2
    return a, b


def lru_fwd(xc, proj, wa, wx, ba, bx, lam, *, tb, name):
    s = xc.shape[0]
    w = BRANCH_W

    def body(xc_ref, gate_ref, wa_ref, wx_ref, ba_ref, bx_ref, lam_ref, y_ref, h_ref, carry):
        @pl.when(pl.program_id(0) == 0)
        def _():
            carry[...] = jnp.zeros_like(carry)

        a, b = _lru_gates(xc_ref[...], wa_ref[...], wx_ref[...], ba_ref[...], bx_ref[...], lam_ref[...])
        pa, hb = _scan_down(a, b)
        h = pa * carry[...] + hb
        h_ref[...] = h
        carry[...] = h_ref[tb - 1:tb, :]
        y_ref[...] = (h * _gelu(gate_ref[...])).astype(BF16)

    row = pl.BlockSpec((tb, w), lambda i: (i, 0))
    return pl.pallas_call(
        body, name=name, grid=(s // tb,),
        in_specs=[row, _pblk(tb, w, C_BG), _full((w, w)), _full((w, w)),
                  _full((1, w)), _full((1, w)), _full((1, w))],
        out_specs=[row, row],
        out_shape=[jax.ShapeDtypeStruct((s, w), BF16), jax.ShapeDtypeStruct((s, w), F32)],
        scratch_shapes=[pltpu.VMEM((1, w), F32)],
        compiler_params=_params("arbitrary"),
    )(xc, proj, wa, wx, ba, bx, lam)


def lru_bwd(xc, proj, h, dy, wa, wx, ba, bx, lam, *, tb, name):
    s = xc.shape[0]
    w = BRANCH_W
    nt = s // tb

    def body(xc_ref, gate_ref, h_ref, hp_ref, dy_ref, wa_ref, wx_ref, ba_ref, bx_ref, lam_ref,
             dxc_ref, dgate_ref, dwa_ref, dwx_ref, dba_ref, dbx_ref, dlam_ref, carry):
        i = pl.program_id(0)

        @pl.when(i == 0)
        def _():
            carry[...] = jnp.zeros_like(carry)
            for r in (dwa_ref, dwx_ref, dba_ref, dbx_ref, dlam_ref):
                r[...] = jnp.zeros_like(r)

        (a, b), vf = jax.vjp(_lru_gates, xc_ref[...], wa_ref[...], wx_ref[...],
                             ba_ref[...], bx_ref[...], lam_ref[...])
        gate = gate_ref[...]
        dyv = dy_ref[...]
        hv = h_ref[...]
        row = _rows((tb, w))
        a_next = jnp.where(row < tb - 1, pltpu.roll(a, tb - 1, 0), 1.0)
        pa, xb = _scan_up(a_next, dyv * _gelu(gate))
        dh = xb + pa * carry[...]
        carry[...] = a[0:1, :] * dh[0:1, :]
        h_first = jnp.where(i < nt - 1, hp_ref[7:8, :], 0.0)
        h_prev = jnp.where(row >= 1, pltpu.roll(hv, 1, 0), h_first)
        dxc, dwa, dwx, dba, dbx, dlam = vf((dh * h_prev, dh))
        dxc_ref[...] = dxc
        dgate_ref[...] = (dyv * hv * _gelu_grad(gate)).astype(BF16)
        dwa_ref[...] += dwa
        dwx_ref[...] += dwx
        dba_ref[...] += dba
        dbx_ref[...] += dbx
        dlam_ref[...] += dlam

    row_spec = pl.BlockSpec((tb, w), lambda i: (nt - 1 - i, 0))
    hp_spec = pl.BlockSpec((8, w), lambda i: (jnp.maximum((nt - 1 - i) * (tb // 8) - 1, 0), 0))
    return pl.pallas_call(
        body, name=name, grid=(nt,),
        in_specs=[row_spec, _pblk(tb, w, C_BG, rev_n=nt), row_spec, hp_spec, row_spec,
                  _full((w, w)), _full((w, w)), _full((1, w)), _full((1, w)), _full((1, w))],
        out_specs=[row_spec, row_spec, _full((w, w)), _full((w, w)),
                   _full((1, w)), _full((1, w)), _full((1, w))],
        out_shape=[jax.ShapeDtypeStruct((s, w), F32), jax.ShapeDtypeStruct((s, w), BF16),
                   jax.ShapeDtypeStruct((w, w), F32), jax.ShapeDtypeStruct((w, w), F32),
                   jax.ShapeDtypeStruct((1, w), F32), jax.ShapeDtypeStruct((1, w), F32),
                   jax.ShapeDtypeStruct((1, w), F32)],
        scratch_shapes=[pltpu.VMEM((1, w), F32)],
        compiler_params=_params("arbitrary"),
    )(xc, proj, h, h, dy, wa, wx, ba, bx, lam)


def _gdn_g(ba, alog, dt):
    return -jnp.exp(alog) * _softplus(ba + dt)


def _gdn_local(qp, kp, vp, bpre, gc):
    b, c, dk = qp.shape
    q = _silu(qp)
    k = _silu(kp)
    v = _silu(vp)
    q = q * lax.rsqrt(jnp.sum(q * q, axis=-1, keepdims=True) + EPS) * (dk ** -0.5)
    k = k * lax.rsqrt(jnp.sum(k * k, axis=-1, keepdims=True) + EPS)
    beta = jax.nn.sigmoid(bpre)
    sq = (b, c, c)
    ii, jj = _iota3(sq, 1), _iota3(sq, 2)
    gr = jnp.sum(gc * (ii == jj).astype(F32), axis=1, keepdims=True)
    decay = jnp.exp(jnp.where(ii >= jj, gc - gr, -jnp.inf))
    wd = (b, c, 2 * c)
    wi, wj = _iota3(wd, 1), _iota3(wd, 2) - c
    gr_w = jnp.sum(gc * (wi == wj).astype(F32), axis=1, keepdims=True)
    decay_w = jnp.exp(jnp.where(jnp.logical_and(wj >= 0, wi > wj), gc - gr_w, -jnp.inf))
    kb = k * beta
    pad = jnp.zeros((b, c, dk), F32)
    t = _tri_inv(bmm_nt(kb, jnp.concatenate([pad, k], axis=1)) * decay_w)
    u = bmm_nn(t, jnp.concatenate([v * beta, pad], axis=1))
    wk = bmm_nn(t, jnp.concatenate([kb * jnp.exp(gc), pad], axis=1))
    attn = bmm_nt(q, k) * decay
    g_last = jnp.sum(jnp.where(_iota3((b, c, 1), 1) == c - 1, gc, 0.0), axis=1, keepdims=True)
    return u, wk, attn, q * jnp.exp(gc), k * jnp.exp(g_last - gc), jnp.exp(g_last)


def _gdn_recur(s, u, wk, attn, qg, kd, eg, zz, ng):
    v_new = u - bmm_nn(wk, s)
    o = bmm_nn(qg, s) + bmm_nn(attn, v_new)
    s_out = s * eg + bmm_tn(kd, v_new)
    o = o * lax.rsqrt(jnp.mean(o * o, axis=-1, keepdims=True) + EPS) * ng
    return o * _silu(zz), s_out


def _gdn_inputs(qkv_ref, ba, gcums, cps):
    c, hd = CHUNK, GDN_DK
    qs, ks, vs, bs, gs = [], [], [], [], []
    for ci in range(cps):
        rs = slice(c * ci, c * (ci + 1))
        for h in range(GDN_HEADS):
            qs.append(qkv_ref[rs, hd * h:hd * (h + 1)])
            ks.append(qkv_ref[rs, 512 + hd * h:512 + hd * (h + 1)])
            vs.append(qkv_ref[rs, 1024 + hd * h:1024 + hd * (h + 1)])
            bs.append(_col(ba[rs], h))
            gs.append(_col(gcums[ci], 4 + h))
    return tuple(jnp.stack(t) for t in (qs, ks, vs, bs, gs))


def _chunk_cumsum(g, transpose=False):
    c = CHUNK
    tri = (_rows((c, c)) >= _lanes((c, c))).astype(F32)
    return _dg3(tri, g, 0 if transpose else 1, 0)


def gdn_fwd(qkv, proj, alog, dt, ng, *, cps, name):
    s = qkv.shape[0]
    c = CHUNK
    nc = s // c
    hd = GDN_DK
    tb = c * cps

    def body(qkv_ref, z_ref, ba_ref, al_ref, dt_ref, ng_ref, y_ref, sall_ref, s_scr):
        @pl.when(pl.program_id(0) == 0)
        def _():
            s_scr[...] = jnp.zeros_like(s_scr)

        nh = GDN_HEADS
        ba = ba_ref[...]
        g = _gdn_g(ba, al_ref[...], dt_ref[...])
        gcums = [_chunk_cumsum(g[c * ci:c * (ci + 1)]) for ci in range(cps)]
        loc = _gdn_local(*_gdn_inputs(qkv_ref, ba, gcums, cps))
        for ci in range(cps):
            rs = slice(c * ci, c * (ci + 1))
            sl = slice(nh * ci, nh * (ci + 1))
            s_in = s_scr[...]
            sall_ref[ci] = s_in
            zz = jnp.stack([z_ref[rs, hd * h:hd * (h + 1)] for h in range(nh)])
            y, s_out = _gdn_recur(s_in, *(t[sl] for t in loc), zz, ng_ref[...])
            s_scr[...] = s_out
            for h in range(nh):
                y_ref[rs, hd * h:hd * (h + 1)] = y[h].astype(BF16)

    return pl.pallas_call(
        body, name=name, grid=(nc // cps,),
        in_specs=[pl.BlockSpec((tb, 1536), lambda i: (i, 0)), _pblk(tb, 512, C_Z), _pblk(tb, 128, C_BA),
                  _full((1, 128)), _full((1, 128)), _full((1, 128))],
        out_specs=[pl.BlockSpec((tb, 512), lambda i: (i, 0)),
                   pl.BlockSpec((cps, GDN_HEADS, hd, hd), lambda i: (i, 0, 0, 0))],
        out_shape=[jax.ShapeDtypeStruct((s, 512), BF16), jax.ShapeDtypeStruct((nc, GDN_HEADS, hd, hd), F32)],
        scratch_shapes=[pltpu.VMEM((GDN_HEADS, hd, hd), F32)],
        compiler_params=_params("arbitrary"),
    )(qkv, proj, proj, alog, dt, ng)


def gdn_bwd(qkv, proj, sall, dy, alog, dt, ng, *, cps, name):
    s = qkv.shape[0]
    c = CHUNK
    nc = s // c
    hd = GDN_DK
    tb = c * cps
    nt = nc // cps

    def body(qkv_ref, z_ref, ba_ref, sall_ref, dy_ref, al_ref, dt_ref, ng_ref,
             dqkv_ref, dz_ref, dba_ref, dal_ref, ddt_ref, dng_ref, ds_scr):
        @pl.when(pl.program_id(0) == 0)
        def _():
            ds_scr[...] = jnp.zeros_like(ds_scr)
            dal_ref[...] = jnp.zeros_like(dal_ref)
            ddt_ref[...] = jnp.zeros_like(ddt_ref)
            dng_ref[...] = jnp.zeros_like(dng_ref)

        nh = GDN_HEADS
        ba = ba_ref[...]
        g, gvf = jax.vjp(_gdn_g, ba, al_ref[...], dt_ref[...])
        gcums = [_chunk_cumsum(g[c * ci:c * (ci + 1)]) for ci in range(cps)]
        loc, loc_vf = jax.vjp(_gdn_local, *_gdn_inputs(qkv_ref, ba, gcums, cps))
        d_loc = [None] * cps
        dng = jnp.zeros((1, 128), F32)
        for ci in reversed(range(cps)):
            rs = slice(c * ci, c * (ci + 1))
            sl = slice(nh * ci, nh * (ci + 1))
            zz = jnp.stack([z_ref[rs, hd * h:hd * (h + 1)] for h in range(nh)])
            dyy = jnp.stack([dy_ref[rs, hd * h:hd * (h + 1)] for h in range(nh)])
            _, rvf = jax.vjp(_gdn_recur, sall_ref[ci], *(t[sl] for t in loc), zz, ng_ref[...])
            ds, du, dwk, dattn, dqg, dkd, deg, dzz, dn = rvf((dyy, ds_scr[...]))
            ds_scr[...] = ds
            d_loc[ci] = (du, dwk, dattn, dqg, dkd, deg)
            dng = dng + dn
            for h in range(nh):
                dz_ref[rs, hd * h:hd * (h + 1)] = dzz[h].astype(BF16)
        dq, dk, dv, dbp, dgc = loc_vf(tuple(jnp.concatenate([d_loc[ci][j] for ci in range(cps)], axis=0)
                                            for j in range(6)))
        dgs, dbas = [], []
        for ci in range(cps):
            rs = slice(c * ci, c * (ci + 1))
            dgcum = jnp.zeros((c, 128), F32)
            dba = jnp.zeros((c, 128), F32)
            for h in range(nh):
                b = nh * ci + h
                dqkv_ref[rs, hd * h:hd * (h + 1)] = dq[b]
                dqkv_ref[rs, 512 + hd * h:512 + hd * (h + 1)] = dk[b]
                dqkv_ref[rs, 1024 + hd * h:1024 + hd * (h + 1)] = dv[b]
                dgcum = dgcum + _put_col(dgc[b], 4 + h, 128)
                dba = dba + _put_col(dbp[b], h, 128)
            dgs.append(_chunk_cumsum(dgcum, transpose=True))
            dbas.append(dba)
        dba2, dal, ddt = gvf(jnp.concatenate(dgs, axis=0))
        dba_ref[...] = (jnp.concatenate(dbas, axis=0) + dba2).astype(BF16)
        dal_ref[...] += dal
        ddt_ref[...] += ddt
        dng_ref[...] += dng

    def rev(width):
        return pl.BlockSpec((tb, width), lambda i: (nt - 1 - i, 0))

    return pl.pallas_call(
        body, name=name, grid=(nt,),
        in_specs=[rev(1536), _pblk(tb, 512, C_Z, rev_n=nt), _pblk(tb, 128, C_BA, rev_n=nt),
                  pl.BlockSpec((cps, GDN_HEADS, hd, hd), lambda i: (nt - 1 - i, 0, 0, 0)), rev(512),
                  _full((1, 128)), _full((1, 128)), _full((1, 128))],
        out_specs=[rev(1536), rev(512), rev(128), _full((1, 128)), _full((1, 128)), _full((1, 128))],
        out_shape=[jax.ShapeDtypeStruct((s, 1536), F32), jax.ShapeDtypeStruct((s, 512), BF16),
                   jax.ShapeDtypeStruct((s, 128), BF16), jax.ShapeDtypeStruct((1, 128), F32),
                   jax.ShapeDtypeStruct((1, 128), F32), jax.ShapeDtypeStruct((1, 128), F32)],
        scratch_shapes=[pltpu.VMEM((GDN_HEADS, hd, hd), F32)],
        compiler_params=_params("arbitrary"),
    )(qkv, proj, proj, sall, dy, alog, dt, ng)


def merge_fwd(x, ys, proj, wb, wo, *, tb, dep=None, name):
    s, d = x.shape
    dep_specs, dep_args = _dep_operand(dep)

    def body(x_ref, ya, yb, yc, yd, g0, g1, g2, g3, wb_ref, wo_ref, *rest):
        o_ref, m_ref = rest[len(dep_args):]
        m = jnp.zeros((tb, d), F32)
        for g, (y_ref, gp_ref) in enumerate(zip((ya, yb, yc, yd), (g0, g1, g2, g3))):
            m = m + jax.nn.sigmoid(gp_ref[...]) * _dg(y_ref[...], wb_ref[g], 1, 0)
        mb = m.astype(BF16)
        m_ref[...] = mb
        o_ref[...] = x_ref[...] + _dg(mb, wo_ref[...], 1, 0)

    row = pl.BlockSpec((tb, d), lambda i: (i, 0))
    yrow = pl.BlockSpec((tb, BRANCH_W), lambda i: (i, 0))
    return pl.pallas_call(
        body, name=name, grid=(s // tb,),
        in_specs=[row] + [yrow] * 4 + [_pblk(tb, d, C_GATE + d * g) for g in range(4)]
        + [_full((4, BRANCH_W, d)), _full((d, d))] + dep_specs,
        out_specs=[row, row],
        out_shape=[jax.ShapeDtypeStruct((s, d), F32), jax.ShapeDtypeStruct((s, d), BF16)],
        compiler_params=_params("parallel"),
    )(x, *ys, proj, proj, proj, proj, wb, wo, *dep_args)


def merge_bwd(dx, ys, proj, wb, wo, *, tb, dep=None, name):
    s, d = dx.shape
    dep_specs, dep_args = _dep_operand(dep)

    def body(dx_ref, ya, yb, yc, yd, g0, g1, g2, g3, wb_ref, wo_ref, *rest):
        dgp_ref, dbr_ref, da, db, dc, dd, dxb_ref = rest[len(dep_args):]
        dxb = dx_ref[...].astype(BF16)
        dxb_ref[...] = dxb
        dm = _dg(dxb, wo_ref[...], 1, 1)
        for g, (y_ref, gp_ref, dy_ref) in enumerate(zip((ya, yb, yc, yd), (g0, g1, g2, g3), (da, db, dc, dd))):
            br = _dg(y_ref[...], wb_ref[g], 1, 0)
            sg = jax.nn.sigmoid(gp_ref[...])
            dgp_ref[:, d * g:d * (g + 1)] = (dm * br * sg * (1.0 - sg)).astype(BF16)
            dbr = (dm * sg).astype(BF16)
            dbr_ref[g] = dbr
            dy_ref[...] = _dg(dbr, wb_ref[g], 1, 1)

    row = pl.BlockSpec((tb, d), lambda i: (i, 0))
    yrow = pl.BlockSpec((tb, BRANCH_W), lambda i: (i, 0))
    return pl.pallas_call(
        body, name=name, grid=(s // tb,),
        in_specs=[row] + [yrow] * 4 + [_pblk(tb, d, C_GATE + d * g) for g in range(4)]
        + [_full((4, BRANCH_W, d)), _full((d, d))] + dep_specs,
        out_specs=[pl.BlockSpec((tb, 4 * d), lambda i: (i, 0)), pl.BlockSpec((4, tb, d), lambda i: (0, i, 0)),
                   yrow, yrow, yrow, yrow, row],
        out_shape=[jax.ShapeDtypeStruct((s, 4 * d), BF16), jax.ShapeDtypeStruct((4, s, d), BF16)]
        + [jax.ShapeDtypeStruct((s, BRANCH_W), F32)] * 4 + [jax.ShapeDtypeStruct((s, d), BF16)],
        compiler_params=_params("parallel"),
    )(dx, *ys, proj, proj, proj, proj, wb, wo, *dep_args)


def loss_head(x, g, target, *, tb, name):
    s, d = x.shape

    def body(x_ref, g_ref, t_ref, dx_ref, dg_ref, loss_ref):
        @pl.when(pl.program_id(0) == 0)
        def _():
            dg_ref[...] = jnp.zeros_like(dg_ref)
            loss_ref[...] = jnp.zeros_like(loss_ref)

        xv, gv = x_ref[...], g_ref[...]
        r = lax.rsqrt(jnp.mean(xv * xv, axis=-1, keepdims=True) + EPS)
        err = xv * r * gv - t_ref[...]
        loss_ref[...] += 0.5 * jnp.sum(jnp.mean(err * err, axis=-1, keepdims=True), axis=0, keepdims=True)
        dx, dg = _rms_bwd(xv, gv, err * (1.0 / d))
        dx_ref[...] = dx
        dg_ref[...] += dg

    row = pl.BlockSpec((tb, d), lambda i: (i, 0))
    return pl.pallas_call(
        body, name=name, grid=(s // tb,),
        in_specs=[row, _full((1, d)), row],
        out_specs=[row, _full((1, d)), _full((1, 128))],
        out_shape=[jax.ShapeDtypeStruct((s, d), F32), jax.ShapeDtypeStruct((1, d), F32),
                   jax.ShapeDtypeStruct((1, 128), F32)],
        compiler_params=_params("arbitrary"),
    )(x, g, target)


def adamw(w, g, m, v, *, name):
    shape = w.shape
    lead, (r, c) = shape[:-2], shape[-2:]
    tr = r
    while tr * c * 4 > (1 << 20) and tr % 16 == 0:
        tr //= 2
    c1 = 1.0 / (1.0 - ADAM_B1 ** ADAM_STEP)
    c2 = 1.0 / (1.0 - ADAM_B2 ** ADAM_STEP)

    def body(w_ref, g_ref, m_ref, v_ref, d_ref, nm_ref, nv_ref):
        gv = g_ref[...]
        nm = ADAM_B1 * m_ref[...] + (1.0 - ADAM_B1) * gv
        nv = ADAM_B2 * v_ref[...] + (1.0 - ADAM_B2) * (gv * gv)
        nm_ref[...] = nm
        nv_ref[...] = nv
        d_ref[...] = -ADAM_LR * ((nm * c1) / (jnp.sqrt(nv * c2) + ADAM_EPS) + ADAM_WD * w_ref[...])

    blk = pl.BlockSpec((None,) * len(lead) + (tr, c), lambda *i: i + (0,))
    grid = lead + (r // tr,)
    if len(lead) == 1 and r < 8:
        tl = max(t for t in range(1, lead[0] + 1) if lead[0] % t == 0 and t * r * c * 4 <= (1 << 20))
        blk = pl.BlockSpec((tl, r, c), lambda i: (i, 0, 0))
        grid = (lead[0] // tl,)
    return tuple(pl.pallas_call(
        body, name=name, grid=grid,
        in_specs=[blk] * 4, out_specs=[blk] * 3,
        out_shape=[jax.ShapeDtypeStruct(shape, F32)] * 3,
        compiler_params=_params(*(("parallel",) * len(grid))),
    )(w, g, m, v))


_ANY = pl.BlockSpec(memory_space=pl.ANY)


def _place():
    x, y, c = lax.axis_index("x"), lax.axis_index("y"), lax.axis_index("c")
    return x, y, c, [(1 - x, y), (x, 1 - y), (1 - x, 1 - y)]


def gather_shards(bufs, *, name):
    n = len(bufs)

    def body(*refs):
        outs = refs[n:2 * n]
        ici_send, ici_recv, d2d_send, d2d_recv = refs[2 * n:]
        x, y, c, chips = _place()
        me = 2 * x + y

        def ici(a, j, slab, to):
            return pltpu.make_async_remote_copy(
                src_ref=outs[a].at[slab, c], dst_ref=outs[a].at[slab, c], send_sem=ici_send.at[a, j],
                recv_sem=ici_recv.at[a, j], device_id=to, device_id_type=MESH)

        def d2d(a, j, slab, half):
            return pltpu.make_async_remote_copy(
                src_ref=outs[a].at[slab, half], dst_ref=outs[a].at[slab, half], send_sem=d2d_send.at[a, j],
                recv_sem=d2d_recv.at[a, j], device_id=(x, y, 1 - c), device_id_type=MESH)

        sends = []
        for a in range(n):
            for j, (px, py) in enumerate(chips):
                cp = ici(a, j, me, (px, py, c))
                cp.start()
                sends.append(cp)
        for a in range(n):
            for j, (px, py) in enumerate(chips):
                ici(a, j, 2 * px + py, (px, py, c)).wait_recv()
                cp = d2d(a, j, 2 * px + py, c)
                cp.start()
                sends.append(cp)
        for a in range(n):
            for j, (px, py) in enumerate(chips):
                d2d(a, j, 2 * px + py, 1 - c).wait_recv()
        for cp in sends:
            cp.wait_send()

    return pl.pallas_call(
        body, name=name,
        in_specs=[_ANY] * n, out_specs=[_ANY] * n,
        out_shape=[jax.ShapeDtypeStruct(t.shape, t.dtype) for t in bufs],
        input_output_aliases={a: a for a in range(n)},
        scratch_shapes=[pltpu.SemaphoreType.DMA((n, 3))] * 4,
    )(*bufs)


_SEM = pl.BlockSpec(memory_space=pltpu.SEMAPHORE)
_VMEM = pl.BlockSpec(memory_space=pltpu.VMEM)
_EFFECT = pltpu.SideEffectType.DATAFLOW_SIDE_EFFECTING
_TOKEN = jax.ShapeDtypeStruct((8, 128), F32)


def _gather_copies(outs, sems_ici, sems_d2d):
    x, y, c, chips = _place()

    def ici(a, j, slab, to):
        return pltpu.make_async_remote_copy(
            src_ref=outs[a].at[slab, c], dst_ref=outs[a].at[slab, c], send_sem=sems_ici[0][3 * a + j],
            recv_sem=sems_ici[1][3 * a + j], device_id=to, device_id_type=MESH)

    def d2d(a, j, slab, half):
        return pltpu.make_async_remote_copy(
            src_ref=outs[a].at[slab, half], dst_ref=outs[a].at[slab, half], send_sem=sems_d2d[0][3 * a + j],
            recv_sem=sems_d2d[1][3 * a + j], device_id=(x, y, 1 - c), device_id_type=MESH)

    return x, y, c, chips, ici, d2d


def _async_call(body, bufs, sems_in, after, sems_out, token, *, name):
    n = len(bufs)
    n_out = 6 * n if sems_out else 0
    extra = [] if after is None else [after]
    return pl.pallas_call(
        body, name=name,
        in_specs=[_ANY] * n + [_SEM] * len(sems_in) + [_ANY] * len(extra),
        out_specs=[_ANY] * n + [_SEM] * n_out + [_VMEM] * token,
        out_shape=[jax.ShapeDtypeStruct(t.shape, t.dtype) for t in bufs]
        + [pltpu.SemaphoreType.DMA(())] * n_out + [_TOKEN] * token,
        input_output_aliases={a: a for a in range(n)},
        compiler_params=pltpu.CompilerParams(has_side_effects=_EFFECT),
    )(*bufs, *sems_in, *extra)


def gather_start(bufs, after=None, *, name):
    n = len(bufs)
    k = n + (after is not None)

    def body(*refs):
        outs = refs[k:k + n]
        sems, token = refs[k + n:k + 7 * n], refs[k + 7 * n]
        x, y, c, chips, ici, _ = _gather_copies(outs, (sems[:3 * n], sems[3 * n:]), None)
        for a in range(n):
            for j, (px, py) in enumerate(chips):
                ici(a, j, 2 * x + y, (px, py, c)).start()
        token[...] = jnp.zeros_like(token)

    res = _async_call(body, bufs, [], after, True, True, name=name)
    return res[:n], res[n:7 * n], res[7 * n]


def gather_pass(bufs, sems, after, *, name):
    n = len(bufs)

    def body(*refs):
        s_in = refs[n:7 * n]
        outs = refs[7 * n + 1:8 * n + 1]
        s_out, token = refs[8 * n + 1:14 * n + 1], refs[14 * n + 1]
        x, y, c, chips, ici, d2d = _gather_copies(outs, (s_in[:3 * n], s_in[3 * n:]),
                                                  (s_out[:3 * n], s_out[3 * n:]))
        for a in range(n):
            for j, (px, py) in enumerate(chips):
                ici(a, j, 2 * px + py, (px, py, c)).wait_recv()
                d2d(a, j, 2 * px + py, c).start()
        for a in range(n):
            for j, (px, py) in enumerate(chips):
                ici(a, j, 2 * x + y, (px, py, c)).wait_send()
        token[...] = jnp.zeros_like(token)

    res = _async_call(body, bufs, list(sems), after, True, True, name=name)
    return res[:n], res[n:7 * n], res[7 * n]


def gather_wait(bufs, sems, after, *, name):
    n = len(bufs)

    def body(*refs):
        s_in = refs[n:7 * n]
        outs = refs[7 * n + 1:]
        x, y, c, chips, _, d2d = _gather_copies(outs, None, (s_in[:3 * n], s_in[3 * n:]))
        for a in range(n):
            for j, (px, py) in enumerate(chips):
                d2d(a, j, 2 * px + py, 1 - c).wait_recv()
                d2d(a, j, 2 * px + py, c).wait_send()

    return _async_call(body, bufs, list(sems), after, False, False, name=name)


def _pair_plan(srcs, lands):
    x, y, c, _ = _place()
    return [(srcs[a].at[j, 1 - c], lands[a].at[j], (x, y, 1 - c))
            for a in range(len(srcs)) for j in range(N_SHARD)]


def _chip_plan(srcs, lands):
    x, y, c, chips = _place()
    return [(srcs[a].at[2 * px + py], lands[a].at[2 * x + y], (px, py, c))
            for a in range(len(srcs)) for (px, py) in chips]


def copies_start(arrs, land_shapes, plan, per_array, *, name):
    n = len(arrs)
    k = per_array * n

    def body(*refs):
        lands = refs[2 * n:3 * n]
        sems, token = refs[3 * n:3 * n + 2 * k], refs[3 * n + 2 * k]
        for i, (src, dst, dev) in enumerate(plan(refs[n:2 * n], lands)):
            pltpu.make_async_remote_copy(src_ref=src, dst_ref=dst, send_sem=sems[i], recv_sem=sems[k + i],
                                         device_id=dev, device_id_type=MESH).start()
        token[...] = jnp.zeros_like(token)

    res = pl.pallas_call(
        body, name=name,
        in_specs=[_ANY] * n,
        out_specs=[_ANY] * (2 * n) + [_SEM] * (2 * k) + [_VMEM],
        out_shape=[jax.ShapeDtypeStruct(t.shape, t.dtype) for t in arrs] + list(land_shapes)
        + [pltpu.SemaphoreType.DMA(())] * (2 * k) + [_TOKEN],
        input_output_aliases={a: a for a in range(n)},
        compiler_params=pltpu.CompilerParams(has_side_effects=_EFFECT),
    )(*arrs)
    return res[:n], res[n:2 * n], res[2 * n:2 * n + 2 * k], res[2 * n + 2 * k]


def copies_wait(arrs, lands, sems, after, plan, *, name):
    n = len(arrs)
    k = len(sems) // 2

    def body(*refs):
        s_in = refs[2 * n:2 * n + 2 * k]
        outs = refs[2 * n + 2 * k + 1:]
        for i, (src, dst, dev) in enumerate(plan(outs[:n], outs[n:])):
            cp = pltpu.make_async_remote_copy(src_ref=src, dst_ref=dst, send_sem=s_in[i], recv_sem=s_in[k + i],
                                              device_id=dev, device_id_type=MESH)
            cp.wait_send()
            cp.wait_recv()

    res = pl.pallas_call(
        body, name=name,
        in_specs=[_ANY] * (2 * n) + [_SEM] * (2 * k) + [_ANY],
        out_specs=[_ANY] * (2 * n),
        out_shape=[jax.ShapeDtypeStruct(t.shape, t.dtype) for t in list(arrs) + list(lands)],
        input_output_aliases={a: a for a in range(2 * n)},
        compiler_params=pltpu.CompilerParams(has_side_effects=_EFFECT),
    )(*arrs, *lands, *sems, after)
    return res[:n], res[n:]


def pair_send_halves(arrs, *, name):
    n = len(arrs)

    def body(*refs):
        ins, outs = refs[:n], refs[n:2 * n]
        send_sems, recv_sems = refs[2 * n:]
        x, y, c, _ = _place()
        copies = []
        for a in range(n):
            for j in range(N_SHARD):
                cp = pltpu.make_async_remote_copy(
                    src_ref=ins[a].at[j, 1 - c], dst_ref=outs[a].at[j], send_sem=send_sems.at[a, j],
                    recv_sem=recv_sems.at[a, j], device_id=(x, y, 1 - c), device_id_type=MESH)
                cp.start()
                copies.append(cp)
        for cp in copies:
            cp.wait()

    return pl.pallas_call(
        body, name=name,
        in_specs=[_ANY] * n, out_specs=[_ANY] * n,
        out_shape=[jax.ShapeDtypeStruct((N_SHARD,) + t.shape[2:], t.dtype) for t in arrs],
        scratch_shapes=[pltpu.SemaphoreType.DMA((n, N_SHARD)), pltpu.SemaphoreType.DMA((n, N_SHARD))],
    )(*arrs)


def chip_scatter(arrs, *, name):
    n = len(arrs)

    def body(*refs):
        ins, outs = refs[:n], refs[n:2 * n]
        send_sems, recv_sems = refs[2 * n:]
        x, y, c, chips = _place()
        me = 2 * x + y
        copies = []
        for a in range(n):
            for j, (px, py) in enumerate(chips):
                cp = pltpu.make_async_remote_copy(
                    src_ref=ins[a].at[2 * px + py], dst_ref=outs[a].at[me], send_sem=send_sems.at[a, j],
                    recv_sem=recv_sems.at[a, j], device_id=(px, py, c), device_id_type=MESH)
                cp.start()
                copies.append(cp)
        for cp in copies:
            cp.wait()

    return pl.pallas_call(
        body, name=name,
        in_specs=[_ANY] * n, out_specs=[_ANY] * n,
        out_shape=[jax.ShapeDtypeStruct(t.shape, t.dtype) for t in arrs],
        scratch_shapes=[pltpu.SemaphoreType.DMA((n, 3)), pltpu.SemaphoreType.DMA((n, 3))],
    )(*arrs)


def pair_gather(bufs, *, dep=None, name):
    n = len(bufs)
    dep_specs, dep_args = _dep_operand(dep)
    k = n + len(dep_args)

    def body(*refs):
        outs = refs[k:k + n]
        send_sems, recv_sems = refs[k + n:]
        x, y, c, _ = _place()
        copies = []
        for a in range(n):
            cp = pltpu.make_async_remote_copy(
                src_ref=outs[a].at[c], dst_ref=outs[a].at[c], send_sem=send_sems.at[a],
                recv_sem=recv_sems.at[a], device_id=(x, y, 1 - c), device_id_type=MESH)
            cp.start()
            copies.append(cp)
        for cp in copies:
            cp.wait()

    return pl.pallas_call(
        body, name=name,
        in_specs=[_ANY] * n + dep_specs, out_specs=[_ANY] * n,
        out_shape=[jax.ShapeDtypeStruct(t.shape, t.dtype) for t in bufs],
        input_output_aliases={a: a for a in range(n)},
        scratch_shapes=[pltpu.SemaphoreType.DMA((n,)), pltpu.SemaphoreType.DMA((n,))],
    )(*bufs, *dep_args)


def all_sum_small(v, *, name):
    m = v.shape[0]

    def body(v_ref, o_ref, buf, send_sems, recv_sems, local_sem):
        x, y, c, chips = _place()
        me, sibling = (x, y, c), (x, y, 1 - c)

        def rows(px, py, pc):
            return buf.at[pl.ds((4 * px + 2 * py + pc) * m, m), :]

        def copy(k, block, to, src=None):
            return pltpu.make_async_remote_copy(
                src_ref=rows(*block) if src is None else src, dst_ref=rows(*block),
                send_sem=send_sems.at[k], recv_sem=recv_sems.at[k], device_id=to, device_id_type=MESH)

        mine = pltpu.make_async_copy(v_ref, rows(*me), local_sem)
        mine.start()
        first = [copy(0, me, sibling, src=v_ref)]
        first += [copy(1 + j, me, (*chip, c), src=v_ref) for j, chip in enumerate(chips)]
        for cp in first:
            cp.start()
        passed = [copy(4 + j, (*chip, c), sibling) for j, chip in enumerate(chips)]
        for j, chip in enumerate(chips):
            copy(1 + j, (*chip, c), me).wait_recv()
            passed[j].start()
        copy(0, sibling, me).wait_recv()
        for j, chip in enumerate(chips):
            copy(4 + j, (*chip, 1 - c), me).wait_recv()
        for cp in first + passed:
            cp.wait_send()
        mine.wait()
        acc = buf[pl.ds(0, m), :]
        for k in range(1, 8):
            acc = acc + buf[pl.ds(k * m, m), :]
        o_ref[...] = acc

    vm = pl.BlockSpec(memory_space=pltpu.VMEM)
    return pl.pallas_call(
        body, name=name, in_specs=[vm], out_specs=vm,
        out_shape=jax.ShapeDtypeStruct((m, 128), F32),
        scratch_shapes=[pltpu.VMEM((8 * m, 128), F32), pltpu.SemaphoreType.DMA((7,)),
                        pltpu.SemaphoreType.DMA((7,)), pltpu.SemaphoreType.DMA],
    )(v)


def pack_slab(srcs, blocks, maps, n_slots, slot_shape, *, place, dep=None, name):
    k = len(srcs)
    per = n_slots // k
    dep_specs, dep_args = _dep_operand(dep)

    def body(place_ref, *refs):
        o_ref = refs[k + len(dep_args)]
        s = pl.program_id(0)
        for i in range(k):
            @pl.when(jnp.logical_and(s >= i * per, s < (i + 1) * per))
            def _():
                o_ref[...] = refs[i][...].astype(BF16)

    return pl.pallas_call(
        body, name=name,
        grid_spec=pltpu.PrefetchScalarGridSpec(
            num_scalar_prefetch=1, grid=(n_slots,),
            in_specs=[pl.BlockSpec(blk, functools.partial(lambda s, pr, m: m(s), m=m)) for blk, m in zip(blocks, maps)]
            + dep_specs,
            out_specs=pl.BlockSpec((None, None) + slot_shape, lambda s, pr: (pr[1], s, 0, 0))),
        out_shape=jax.ShapeDtypeStruct((N_SHARD, n_slots) + slot_shape, BF16),
        compiler_params=_params("arbitrary"),
    )(place, *srcs, *dep_args)


def sum_halves(arr, recv, c_idx, *, name):
    _, _, r, c = arr.shape
    tr = r
    while tr * c * 4 > (1 << 20) and tr % 32 == 0:
        tr //= 2

    def body(c_ref, a_ref, b_ref, o_ref):
        o_ref[...] = (a_ref[...] + b_ref[...]).astype(BF16)

    return pl.pallas_call(
        body, name=name,
        grid_spec=pltpu.PrefetchScalarGridSpec(
            num_scalar_prefetch=1, grid=(N_SHARD, r // tr),
            in_specs=[pl.BlockSpec((None, None, tr, c), lambda j, i, cr: (j, cr[0], i, 0)),
                      pl.BlockSpec((None, tr, c), lambda j, i, cr: (j, i, 0))],
            out_specs=pl.BlockSpec((None, tr, c), lambda j, i, cr: (j, i, 0))),
        out_shape=jax.ShapeDtypeStruct((N_SHARD, r, c), BF16),
        compiler_params=_params("parallel", "parallel"),
    )(c_idx, arr, recv)


def sum_chips(recv, part, place, *, name):
    _, r, c = recv.shape
    tr = r
    while tr * c * 4 > (1 << 20) and tr % 32 == 0:
        tr //= 2

    def body(place_ref, a0, a1, a2, a3, p_ref, o_ref):
        me = place_ref[1]
        acc = None
        for k, a_ref in enumerate((a0, a1, a2, a3)):
            term = jnp.where(me == k, p_ref[...], a_ref[...]).astype(F32)
            acc = term if acc is None else acc + term
        o_ref[...] = acc

    def slab(k):
        return pl.BlockSpec((None, tr, c), lambda i, pr: (jnp.where(pr[1] == k, (k + 1) % N_SHARD, k), i, 0))

    return pl.pallas_call(
        body, name=name,
        grid_spec=pltpu.PrefetchScalarGridSpec(
            num_scalar_prefetch=1, grid=(r // tr,),
            in_specs=[slab(k) for k in range(N_SHARD)]
            + [pl.BlockSpec((None, tr, c), lambda i, pr: (pr[1], i, 0))],
            out_specs=pl.BlockSpec((None, tr, c), lambda i, pr: (pr[0], i, 0))),
        out_shape=jax.ShapeDtypeStruct((2, r, c), F32),
        compiler_params=_params("parallel"),
    )(place, recv, recv, recv, recv, part)


_WEIGHTS = ("ff1_norm", "ff1_wg", "ff1_wu", "ff1_wd", "mix_norm", "w_in", "sgu_ln_g", "sgu_ln_b", "sgu_w",
            "sgu_b", "lru_conv_w", "lru_conv_b", "lru_wa", "lru_ba", "lru_wx", "lru_bx", "lru_lambda",
            "gdn_conv_w", "gdn_a_log", "gdn_dt_bias", "gdn_norm_g", "pool_w", "pool_scale", "w_branch",
            "w_out", "ff2_norm", "ff2_wg", "ff2_wu", "ff2_wd", "final_norm")
_BIG = ("ff1_wg", "ff1_wu", "ff1_wd", "w_in", "w_branch", "w_out", "ff2_wg", "ff2_wu", "ff2_wd")
_SMALL = tuple(n for n in _WEIGHTS if n not in _BIG)
_CONV = ("lru_conv_w", "gdn_conv_w")


def _seg_rows(shape):
    return -(-math.prod(shape) // 1024) * 8


def _pack(arrs):
    segs = []
    for t in arrs:
        rows = _seg_rows(t.shape)
        flat = t.reshape(-1)
        segs.append(jnp.pad(flat, (0, rows * 128 - flat.shape[0])).reshape(rows, 128))
    return jnp.concatenate(segs, axis=0)


def _unpack(buf, shapes):
    out, o = [], 0
    for shp in shapes:
        rows = _seg_rows(shp)
        out.append(buf[o:o + rows].reshape(-1)[:math.prod(shp)].reshape(shp))
        o += rows
    return out


def _block_diag(w):
    h, n, _ = w.shape
    same = jnp.arange(h)[:, None, None, None] == jnp.arange(h)[None, None, :, None]
    return jnp.where(same, w[:, :, None, :], 0.0).reshape(h * n, h * n)


def _diag_blocks(m, h=8, n=64):
    return jnp.stack([m[i * n:(i + 1) * n, i * n:(i + 1) * n] for i in range(h)])


def _lane_row(v):
    return jnp.zeros((1, 128), F32).at[0, 4:8].set(v)


def _w_in_to_padded(raw):
    sem = jnp.concatenate([raw[k, :, :P_IN_SHARD] for k in range(N_SHARD)], axis=1)
    zero = jnp.zeros((sem.shape[0], C_GATE - C_BA - 8), sem.dtype)
    return jnp.concatenate([sem[:, 2048:4096], sem[:, 0:2048], sem[:, 4104:4616], sem[:, 4096:4104], zero,
                            sem[:, 4616:P_IN]], axis=1)


def _w_in_grad_to_shards(part):
    r = part.shape[1]
    dwp = part.transpose(1, 0, 2).reshape(r, P_PAD)
    sem = jnp.concatenate([dwp[:, C_AU:C_DX], dwp[:, 0:C_AU], dwp[:, C_BA:C_BA + 8], dwp[:, C_DX:C_BA],
                           dwp[:, C_GATE:]], axis=1)
    sh = sem.reshape(r, N_SHARD, P_IN_SHARD).transpose(1, 0, 2)
    return jnp.pad(sh, ((0, 0), (0, 0), (0, P_IN_SHARD_PAD - P_IN_SHARD)))


def kernel(x, ff1_norm, ff1_wg, ff1_wu, ff1_wd, mix_norm, w_in, sgu_ln_g, sgu_ln_b, sgu_w, sgu_b, lru_conv_w,
           lru_conv_b, lru_wa, lru_ba, lru_wx, lru_bx, lru_lambda, gdn_conv_w, gdn_a_log, gdn_dt_bias, gdn_norm_g,
           pool_w, pool_scale, w_branch, w_out, ff2_norm, ff2_wg, ff2_wu, ff2_wd, final_norm, loss_target, m_ff1_norm,
           m_ff1_wg, m_ff1_wu, m_ff1_wd, m_mix_norm, m_w_in, m_sgu_ln_g, m_sgu_ln_b, m_sgu_w, m_sgu_b, m_lru_conv_w,
           m_lru_conv_b, m_lru_wa, m_lru_ba, m_lru_wx, m_lru_bx, m_lru_lambda, m_gdn_conv_w, m_gdn_a_log, m_gdn_dt_bias,
           m_gdn_norm_g, m_pool_w, m_pool_scale, m_w_branch, m_w_out, m_ff2_norm, m_ff2_wg, m_ff2_wu, m_ff2_wd,
           m_final_norm, v_ff1_norm, v_ff1_wg, v_ff1_wu, v_ff1_wd, v_mix_norm, v_w_in, v_sgu_ln_g, v_sgu_ln_b, v_sgu_w,
           v_sgu_b, v_lru_conv_w, v_lru_conv_b, v_lru_wa, v_lru_ba, v_lru_wx, v_lru_bx, v_lru_lambda, v_gdn_conv_w,
           v_gdn_a_log, v_gdn_dt_bias, v_gdn_norm_g, v_pool_w, v_pool_scale, v_w_branch, v_w_out, v_ff2_norm, v_ff2_wg,
           v_ff2_wu, v_ff2_wd, v_final_norm):
    a = dict(locals())
    n_layer = ff1_norm.shape[0]
    d = D_MODEL
    x0 = x[0]
    chip = 2 * lax.axis_index("x") + lax.axis_index("y")
    c_idx = jnp.reshape(lax.axis_index("c"), (1,)).astype(jnp.int32)
    place = jnp.stack([lax.axis_index("c"), chip]).astype(jnp.int32)

    def own_slab(t, dtype=BF16):
        buf = lax.empty((N_SHARD,) + t.shape, dtype)
        return lax.dynamic_update_slice(buf, t.astype(dtype)[None], (chip,) + (0,) * t.ndim)

    assert n_layer == 2

    def layer_bufs(l, dep=None):
        fc = FF_SHARD
        whole = lambda s: (l, 0, 0)
        rows = lambda s: (l, s, 0)
        made = [dep]

        def pack(*args, name):
            made.append(pack_slab(*args, place=place, dep=made[-1], name=name))
            return made[-1]

        bufs = []
        for tag, wg, wu, wd in (("ffn1", ff1_wg, ff1_wu, ff1_wd), ("ffn2", ff2_wg, ff2_wu, ff2_wd)):
            bufs.append([pack([wg, wu], [(None, d, fc)] * 2, [whole] * 2, 2, (d, fc), name=f"pack_{tag}_wgu_{l}"),
                         pack([wd], [(None, fc // 2, d)], [rows], 2, (fc // 2, d), name=f"pack_{tag}_wd_{l}")])
        win = pack([w_in], [(None, d // 2, P_IN_SHARD_PAD)], [rows], 2, (d // 2, P_IN_SHARD_PAD),
                   name=f"pack_w_in_{l}")
        wb = pack([w_branch], [(None, None, BRANCH_W, d // 4)], [lambda s: (l, s, 0, 0)], 4, (BRANCH_W, d // 4),
                  name=f"pack_wb_{l}")
        wo = pack([w_out], [(None, d // 8, d)], [rows], 2, (d // 8, d), name=f"pack_wo_{l}")
        return bufs[0] + [win, wb.reshape(N_SHARD, 2, 2, BRANCH_W, d // 4), wo] + bufs[1]

    n_buf = 7
    groups = {"ffn1": (0, 2), "mix": (2, 5), "ffn2": (5, 7)}

    def group_sems(sems, lo, hi):
        return list(sems[3 * lo:3 * hi]) + list(sems[3 * n_buf + 3 * lo:3 * n_buf + 3 * hi])

    def ffn_weights(wgu, wd):
        return (wgu, wgu, wd.reshape(N_SHARD, 1, FF_SHARD, d), (0, 1, 0))

    def mix_weights(g_w_in, g_wb, g_wo):
        return {"w_in": _w_in_to_padded(g_w_in.reshape(4, d, P_IN_SHARD_PAD)),
                "wb": g_wb.reshape(4, 4, BRANCH_W, d // 4).transpose(1, 2, 0, 3).reshape(4, BRANCH_W, d),
                "wo": g_wo.reshape(d, d)}

    g_lcw, g_gcw = gather_shards([own_slab(lru_conv_w, F32), own_slab(gdn_conv_w, F32)], name="gather_conv_weights")

    def small_params(l):
        return {"lcw": g_lcw[:, l].transpose(1, 0, 2).reshape(4, BRANCH_W),
                "gcw": g_gcw[:, l].transpose(1, 0, 2).reshape(4, 3 * BRANCH_W),
                "lcb": lru_conv_b[l][None],
                "bt": jnp.zeros((128, 128), F32).at[:, :4].set(sgu_b[l].T),
                "wa": _block_diag(lru_wa[l]), "wx": _block_diag(lru_wx[l]),
                "alog": _lane_row(gdn_a_log[l]), "dt": _lane_row(gdn_dt_bias[l])}

    flying = {0: gather_start(layer_bufs(0), name="gather_weights_0_start")[:2]}
    handed = {}

    def hand_on(l, grp, after):
        lo, hi = groups[grp]
        bufs_l, sems_l = flying[l]
        got, sm, tk = gather_pass(bufs_l[lo:hi], group_sems(sems_l, lo, hi), after,
                                  name=f"gather_weights_{l}_{grp}_pass")
        handed[l, grp] = (got, sm)
        return tk

    def arrived(l, grp, after):
        return gather_wait(*handed.pop((l, grp)), after, name=f"gather_weights_{l}_{grp}_wait")

    saved = []
    xs = x0
    bufs1 = layer_bufs(1)
    tok = hand_on(0, "ffn1", bufs1[4])
    for l in range(n_layer):
        p = small_params(l)
        sv = {"p": p, "x0": xs}
        p["ff1"] = ffn_weights(*arrived(l, "ffn1", tok))
        x1 = ffn_fwd(xs, ff1_norm[l][None], *p["ff1"], tb=512, name=f"ffn1_fwd_{l}")
        tok = hand_on(l, "mix", x1)
        if l == 0:
            split = groups["ffn1"][1]
            first1 = gather_start(bufs1[:split], tok, name="gather_weights_1_start_ffn1")
        h = norm_fwd(x1, mix_norm[l][None], tb=512, name=f"mix_norm_fwd_{l}")
        p.update(mix_weights(*arrived(l, "mix", h)))
        proj = matmul(h, p["w_in"], tm=1024, tn=1536, tk=1024, name=f"proj_{l}")
        ya = sgu_fwd(proj, sgu_ln_g[l][None], sgu_ln_b[l][None], sgu_w[l], p["bt"], tb=512, name=f"sgu_fwd_{l}")
        xc = conv_fwd(proj, C_BX, BRANCH_W, p["lcw"], p["lcb"], tb=512, name=f"lru_conv_fwd_{l}")
        yb, hl = lru_fwd(xc, proj, p["wa"], p["wx"], lru_ba[l][None], lru_bx[l][None], lru_lambda[l][None],
                         tb=256, name=f"lru_fwd_{l}")
        hand_on(l, "ffn2", yb)
        if l == 0:
            rest1 = gather_start(bufs1[split:], yb, name="gather_weights_1_start_rest")
            k1, k2 = 3 * split, 3 * (n_buf - split)
            flying[1] = (list(first1[0]) + list(rest1[0]),
                         list(first1[1][:k1]) + list(rest1[1][:k2]) + list(first1[1][k1:]) + list(rest1[1][k2:]))
        qkv = conv_fwd(proj, C_QKV, 3 * BRANCH_W, p["gcw"], jnp.zeros((1, 3 * BRANCH_W), F32), tb=512,
                       name=f"gdn_conv_fwd_{l}")
        yc, sall = gdn_fwd(qkv, proj, p["alog"], p["dt"], gdn_norm_g[l][None], cps=GDN_CPS, name=f"gdn_fwd_{l}")
        p["ff2"] = ffn_weights(*arrived(l, "ffn2", yc))
        dep2 = None
        if l + 1 < n_layer:
            dep2 = tok = hand_on(l + 1, "ffn1", rest1[2])
        yd = pool_fwd(proj, pool_w[l], pool_scale[l][None], tb=512, name=f"pool_fwd_{l}")
        ys = (ya, yb, yc, yd)
        x2, merged = merge_fwd(x1, ys, proj, p["wb"], p["wo"], tb=256, dep=dep2, name=f"merge_fwd_{l}")
        x3 = ffn_fwd(x2, ff2_norm[l][None], *p["ff2"], tb=512, name=f"ffn2_fwd_{l}")
        tok = x3
        sv.update(x1=x1, h=h, proj=proj, xc=xc, hl=hl, qkv=qkv, sall=sall, ys=ys, merged=merged, x2=x2)
        saved.append(sv)
        xs = x3

    dx, d_final, loss_blk = loss_head(xs, final_norm[None], loss_target[0], tb=512, name="loss_head")
    loss = lax.psum(loss_blk[0, 0], ("x", "y", "c"))

    small_g = {n: [None] * n_layer for n in _SMALL if n != "final_norm"}
    big_g = [None] * n_layer
    per_layer = 7
    flight = None

    def pair_sums(bigs, recv, tag, w_in_at):
        part = [sum_halves(t, r, c_idx, name=f"grad_pair_sum_{tag}_{i}") for i, (t, r) in enumerate(zip(bigs, recv))]
        if w_in_at is not None:
            part[w_in_at] = _w_in_grad_to_shards(part[w_in_at])
        return part

    def pair_start(arrs, tag):
        return copies_start(arrs, [jax.ShapeDtypeStruct((N_SHARD,) + t.shape[2:], t.dtype) for t in arrs],
                            _pair_plan, N_SHARD, name=f"grad_pair_exchange_{tag}_start")

    def pair_to_chip(fl, after, tag, w_in_at=None):
        sent, recv = copies_wait(*fl[:3], after, _pair_plan, name=f"grad_pair_exchange_{tag}_wait")
        part = pair_sums(sent, recv, tag, w_in_at)
        return copies_start(part, [jax.ShapeDtypeStruct(t.shape, t.dtype) for t in part], _chip_plan, 3,
                            name=f"grad_chip_exchange_{tag}_start")

    def chip_finish(fl, after, tag):
        part, recv = copies_wait(*fl[:3], after, _chip_plan, name=f"grad_chip_exchange_{tag}_wait")
        return [sum_chips(r, q, place, name=f"grad_chip_sum_{tag}_{i}") for i, (r, q) in enumerate(zip(recv, part))]

    early = mid = None

    for l in reversed(range(n_layer)):
        sv = saved[l]
        p = sv["p"]
        proj = sv["proj"]
        dx2, dg_ff2, dab2, hid2, h2, dy2 = ffn_bwd(sv["x2"], dx, ff2_norm[l][None], *p["ff2"], tb=512,
                                                  dep=None if flight is None else flight[3],
                                                  name=f"ffn2_bwd_{l}")
        dwgu2 = matmul(h2, dab2, ta=True, tm=1024, tn=FF_SHARD, tk=TK_DW, name=f"ffn2_dwgu_{l}")
        dwd2 = matmul(hid2, dy2, ta=True, tm=FF_SHARD, tn=1024, tk=TK_DW, name=f"ffn2_dwd_{l}")
        if flight is not None:
            flight = pair_to_chip(flight, dx2, "1", w_in_at=2)
        dgp, dbr, dya, dyb, dyc, dyd, dx2b = merge_bwd(dx2, sv["ys"], proj, p["wb"], p["wo"], tb=256,
                                                       dep=None if flight is None else flight[3],
                                                       name=f"merge_bwd_{l}")
        dwo = matmul(sv["merged"], dx2b, ta=True, tm=1024, tn=1024, tk=TK_DW, name=f"dwo_{l}")
        dwb = jnp.stack([matmul(sv["ys"][g], dbr[g], ta=True, tm=BRANCH_W, tn=1024, tk=TK_DW, name=f"dwb{g}_{l}")
                         for g in range(4)])
        dwb_sh = dwb.reshape(4, BRANCH_W, N_SHARD, d // N_SHARD).transpose(2, 0, 1, 3)
        g_late = [dwgu2, dwd2.reshape(N_SHARD, 2, FF_SHARD // 2, d),
                  dwb_sh.reshape(N_SHARD, 2, 2 * BRANCH_W, d // N_SHARD), dwo.reshape(N_SHARD, 2, d // 8, d)]
        tok = None
        if l == 0:
            early = pair_start(g_late, "0a")
            tok = early[3]
        du, dv, dlg, dlb, dws, dbt = sgu_bwd(proj, dya, sgu_ln_g[l][None], sgu_ln_b[l][None], sgu_w[l], p["bt"],
                                             tb=512, dep=tok, name=f"sgu_bwd_{l}")
        dxc, dgate, dwa, dwx, dba, dbx, dlam = lru_bwd(sv["xc"], proj, sv["hl"], dyb, p["wa"], p["wx"],
                                                       lru_ba[l][None], lru_bx[l][None], lru_lambda[l][None],
                                                       tb=256, name=f"lru_bwd_{l}")
        dbx_in, dcw_l = conv_bwd(proj, C_BX, BRANCH_W, dxc, p["lcw"], tb=512, name=f"lru_conv_bwd_{l}")
        dqkv, dz, dbeta, dal, ddt, dng = gdn_bwd(sv["qkv"], proj, sv["sall"], dyc, p["alog"], p["dt"],
                                                 gdn_norm_g[l][None], cps=GDN_CPS, name=f"gdn_bwd_{l}")
        dqkv_in, dcw_g = conv_bwd(proj, C_QKV, 3 * BRANCH_W, dqkv, p["gcw"], tb=512, name=f"gdn_conv_bwd_{l}")
        ddx, dwp, dsc = pool_bwd(proj, dyd, pool_w[l], pool_scale[l][None], tb=512, name=f"pool_bwd_{l}")
        if l == 0:
            early = pair_to_chip(early, ddx, "0a")
            tok = early[3]
        s = dx.shape[0]
        dproj = jnp.concatenate([dqkv_in, dz, du, dv, dbx_in, dgate, ddx, dbeta,
                                 jnp.zeros((s, C_GATE - C_BA - 128), BF16), dgp], axis=1)
        dh = matmul(dproj, p["w_in"], tb=True, tm=1024, tn=1024, tk=1536, dep=tok, name=f"dh_mix_{l}")
        dw_in = matmul(sv["h"], dproj, ta=True, tm=1024, tn=1152, tk=TK_DW, col_groups=N_SHARD, name=f"dw_in_{l}")
        dw_in = dw_in.reshape(N_SHARD, 2, d // 2, P_PAD // N_SHARD)
        if l == 0:
            mid = pair_start([dw_in], "0b")
            tok = mid[3]
        dx1, dg_mix = norm_bwd(sv["x1"], mix_norm[l][None], dh, dx2, tb=512, dep=tok, name=f"mix_norm_bwd_{l}")
        dx, dg_ff1, dab1, hid1, h1, dy1 = ffn_bwd(sv["x0"], dx1, ff1_norm[l][None], *p["ff1"],
                                                 tb=512, name=f"ffn1_bwd_{l}")
        if l == 0:
            mid = pair_to_chip(mid, dx, "0b", w_in_at=0)
            tok = mid[3]
        dwgu1 = matmul(h1, dab1, ta=True, tm=1024, tn=FF_SHARD, tk=TK_DW, dep=tok, name=f"ffn1_dwgu_{l}")
        dwd1 = matmul(hid1, dy1, ta=True, tm=FF_SHARD, tn=1024, tk=TK_DW, name=f"ffn1_dwd_{l}")

        for n, g in (("ff1_norm", dg_ff1[0]), ("mix_norm", dg_mix[0]), ("sgu_ln_g", dlg[0]), ("sgu_ln_b", dlb[0]),
                     ("sgu_w", dws), ("sgu_b", dbt[:, :4].T), ("lru_conv_w", dcw_l[:4]), ("lru_conv_b", dcw_l[4]),
                     ("lru_wa", _diag_blocks(dwa)), ("lru_ba", dba[0]), ("lru_wx", _diag_blocks(dwx)),
                     ("lru_bx", dbx[0]), ("lru_lambda", dlam[0]), ("gdn_conv_w", dcw_g[:4]),
                     ("gdn_a_log", dal[0, 4:8]), ("gdn_dt_bias", ddt[0, 4:8]), ("gdn_norm_g", dng[0]),
                     ("pool_w", dwp), ("pool_scale", dsc[0]), ("ff2_norm", dg_ff2[0])):
            small_g[n][l] = g
        big_g[l] = [dwgu1, dwd1.reshape(N_SHARD, 2, FF_SHARD // 2, d), dw_in, g_late[2], g_late[3], g_late[0], g_late[1]]
        if l == n_layer - 1:
            flight = pair_start(big_g[l], "1")

    late = pair_start(big_g[0][:2], "0c")
    red = {}
    red.update(((1, k), t) for k, t in enumerate(chip_finish(flight, late[3], "1")))
    red.update(zip(((0, 5), (0, 6), (0, 3), (0, 4)), chip_finish(early, late[3], "0a")))
    red[0, 2], = chip_finish(mid, late[3], "0b")
    late = pair_to_chip(late, red[0, 2], "0c")
    full = {}

    def pair_gather_into(keys, dep, tag):
        full.update(zip(keys, pair_gather([red[k] for k in keys], dep=dep, name=f"grad_pair_gather_{tag}")))

    pair_gather_into(sorted(red), late[3], "rest")

    def layers(k, f):
        return jnp.stack([f(full[l, k]) for l in range(n_layer)])

    grads = {}
    grads["w_in"] = layers(2, lambda t: t.reshape(d, P_IN_SHARD_PAD)[:, :P_IN_SHARD])
    grads["w_branch"] = layers(3, lambda t: t.reshape(4, BRANCH_W, d // N_SHARD))
    grads["w_out"] = layers(4, lambda t: t.reshape(d // N_SHARD, d))
    grads["ff2_wg"] = layers(5, lambda t: t[0])
    grads["ff2_wu"] = layers(5, lambda t: t[1])
    grads["ff2_wd"] = layers(6, lambda t: t.reshape(FF_SHARD, d))

    small_full = [jnp.stack(small_g[n]) for n in _SMALL if n != "final_norm"] + [d_final[0]]
    summed = _unpack(all_sum_small(_pack(small_full), name="grad_small_allreduce"), [t.shape for t in small_full])
    for n, g in zip(_SMALL, summed):
        if n in _CONV:
            width = a[n].shape[-1]
            g = lax.dynamic_slice_in_dim(g, chip * width, width, axis=2)
        grads[n] = g

    delta, new_m, new_v = {}, {}, {}

    def finish_ffn1():
        red.update(zip(((0, 0), (0, 1)), chip_finish(late, new_v["ff2_wd"], "0c")))
        pair_gather_into([(0, 0), (0, 1)], None, "0c")
        grads["ff1_wg"] = layers(0, lambda t: t[0])
        grads["ff1_wu"] = layers(0, lambda t: t[1])
        grads["ff1_wd"] = layers(1, lambda t: t.reshape(FF_SHARD, d))

    for n in sorted(_BIG, key=lambda n: n.startswith("ff1_")):
        if n.startswith("ff1_") and n not in grads:
            finish_ffn1()
        args = (a[n], grads[n], a["m_" + n], a["v_" + n])
        if a[n].shape[-1] == FF_SHARD:
            outs = adamw(*(t.swapaxes(1, 2) for t in args), name=f"adamw_{n}")
            delta[n], new_m[n], new_v[n] = (t.swapaxes(1, 2) for t in outs)
        elif n == "w_in":
            outs = adamw(*(t.transpose(2, 0, 1) for t in args), name=f"adamw_{n}")
            delta[n], new_m[n], new_v[n] = (t.transpose(1, 2, 0) for t in outs)
        else:
            delta[n], new_m[n], new_v[n] = adamw(*args, name=f"adamw_{n}")
    shapes = [a[n].shape for n in _SMALL]
    packed = [_pack([src[pre + n] for n in _SMALL]) for src, pre in ((a, ""), (grads, ""), (a, "m_"), (a, "v_"))]
    for store, buf in zip((delta, new_m, new_v), adamw(*packed, name="adamw_small")):
        store.update(zip(_SMALL, _unpack(buf, shapes)))

    return (loss, dx[None], *[grads[n] for n in _WEIGHTS], *[delta[n] for n in _WEIGHTS],
            *[new_m[n] for n in _WEIGHTS], *[new_v[n] for n in _WEIGHTS])
```
